```python
import math
import jax, jax.numpy as jnp
from jax import lax
import numpy as np


D_MODEL = 1024
BATCH = 4
SEQ = 4096
DEPTH = 2

D_MIX = 2 * D_MODEL
N_GROUPS_MIX = 4
GROUP_W = D_MIX // N_GROUPS_MIX
HEAD_DIM = 64
Q_BLOCK = 128
EPS = 1e-6
NEG_INF = -1e30
TINY = 1e-30

FOX_HEADS = GROUP_W // HEAD_DIM
FOX_F_BIAS_INIT = 2.0

SSM_HEADS = GROUP_W // HEAD_DIM
SSM_STATE = 128
SSM_GROUPS = 2
SSM_CONV = 4
SSM_CHUNK = 128
SSM_CONV_DIM = GROUP_W + 2 * SSM_GROUPS * SSM_STATE
DT_MIN = 1e-3
DT_MAX = 1e-1

NSA_HEADS = GROUP_W // HEAD_DIM
NSA_KV_HEADS = 2
NSA_REP = NSA_HEADS // NSA_KV_HEADS
NSA_KV_W = NSA_KV_HEADS * HEAD_DIM
CMP_BLOCK = 32
CMP_STRIDE = 16
CMP_HIDDEN = 2 * HEAD_DIM
SEL_BLOCK = 64
SEL_TOPK = 16
WINDOW = 512
SEL_FORCE = 1e9

MEM_TOKENS = 256
MEM_HEADS = 4
MEM_HEAD_DIM = GROUP_W // MEM_HEADS

REL_BUCKETS = 32
REL_MAX_DIST = 128

FOX_COLS = 4 * GROUP_W + FOX_HEADS
SSM_COLS = GROUP_W + SSM_CONV_DIM + SSM_HEADS
NSA_COLS = 2 * GROUP_W + 6 * NSA_KV_W + 3 * NSA_HEADS
MEM_COLS = 2 * GROUP_W
IN_COLS = FOX_COLS + SSM_COLS + NSA_COLS + MEM_COLS

kernel_name = 'hybrid_fox_ssd_nsa_mem_block'


def _rmsnorm(x, g):
    xf = x.astype(jnp.float32)
    y = xf * lax.rsqrt(jnp.mean(xf * xf, axis=-1, keepdims=True) + EPS)
    return y.astype(x.dtype) * g


def _split(x, sizes):
    offs = [int(o) for o in np.cumsum(sizes)[:-1]]
    return jnp.split(x, offs, axis=-1)


def _t5_bucket(dist):
    n = jnp.maximum(dist, 0)
    max_exact = REL_BUCKETS // 2
    nf = jnp.maximum(n, 1).astype(jnp.float32)
    large = max_exact + (jnp.log(nf / max_exact) / math.log(REL_MAX_DIST / max_exact)
                         * (REL_BUCKETS - max_exact)).astype(jnp.int32)
    large = jnp.minimum(large, REL_BUCKETS - 1)
    return jnp.where(n < max_exact, n, large)


def _masked_softmax(s, mask):
    s = jnp.where(mask, s.astype(jnp.float32), NEG_INF)
    m = jnp.max(s, axis=-1, keepdims=True)
    e = jnp.where(mask, jnp.exp(s - m), 0.0)
    return e / jnp.maximum(jnp.sum(e, axis=-1, keepdims=True), TINY)


def _fox_mixer(p, f_bias):
    B, S, _ = p.shape
    q, k, v, gate, f_logit = _split(p, [GROUP_W, GROUP_W, GROUP_W, GROUP_W, FOX_HEADS])
    q = q.reshape(B, S, FOX_HEADS, HEAD_DIM) * HEAD_DIM ** -0.5
    k = k.reshape(B, S, FOX_HEADS, HEAD_DIM)
    v = v.reshape(B, S, FOX_HEADS, HEAD_DIM)
    log_f = jax.nn.log_sigmoid((f_logit + f_bias).astype(jnp.float32))
    c = jnp.cumsum(log_f, axis=1).transpose(0, 2, 1)
    kpos = jnp.arange(S)

    def block(i):
        t0 = i * Q_BLOCK
        qb = lax.dynamic_slice_in_dim(q, t0, Q_BLOCK, axis=1)
        cb = lax.dynamic_slice_in_dim(c, t0, Q_BLOCK, axis=2)
        s = jnp.einsum('bqhd,bkhd->bhqk', qb, k).astype(jnp.float32)
        s = s + cb[..., :, None] - c[..., None, :]
        qpos = t0 + jnp.arange(Q_BLOCK)
        pr = _masked_softmax(s, kpos[None, :] <= qpos[:, None])
        return jnp.einsum('bhqk,bkhd->bqhd', pr.astype(v.dtype), v)

    o = lax.map(block, jnp.arange(S // Q_BLOCK))
    o = o.transpose(1, 0, 2, 3, 4).reshape(B, S, GROUP_W)
    return o * jax.nn.silu(gate)


def _causal_dwconv(x, w, b):
    C = x.shape[-1]
    y = lax.conv_general_dilated(x, w[:, None, :], window_strides=(1,),
                                 padding=[(SSM_CONV - 1, 0)],
                                 dimension_numbers=('NWC', 'WIO', 'NWC'),
                                 feature_group_count=C)
    return y + b


def _ssd_chunked(xdt, dA, Bh, Ch):
    B, S, H, P = xdt.shape
    N = Bh.shape[-1]
    nc = S // SSM_CHUNK
    x = xdt.reshape(B, nc, SSM_CHUNK, H, P)
    Bc = Bh.reshape(B, nc, SSM_CHUNK, H, N)
    Cc = Ch.reshape(B, nc, SSM_CHUNK, H, N)
    a = dA.reshape(B, nc, SSM_CHUNK, H).transpose(0, 3, 1, 2)
    acs = jnp.cumsum(a, axis=-1)
    idx = jnp.arange(SSM_CHUNK)
    causal = idx[:, None] >= idx[None, :]
    seg = jnp.exp(jnp.where(causal, acs[..., :, None] - acs[..., None, :], -jnp.inf))
    cb = jnp.einsum('bclhn,bcshn->bhcls', Cc, Bc) * seg
    y_diag = jnp.einsum('bhcls,bcshp->bclhp', cb, x)
    decay_to_end = jnp.exp(acs[..., -1:] - acs)
    chunk_states = jnp.einsum('bclhn,bhcl,bclhp->bchpn', Bc, decay_to_end, x)
    chunk_decay = jnp.exp(acs[..., -1])

    def step(state, inp):
        st, dec = inp
        return state * dec[:, :, None, None] + st, state

    init = jnp.zeros((B, H, P, N), chunk_states.dtype)
    _, prev = lax.scan(step, init, (jnp.moveaxis(chunk_states, 1, 0), jnp.moveaxis(chunk_decay, 2, 0)))
    prev = jnp.moveaxis(prev, 0, 1)
    y_off = jnp.einsum('bclhn,bchpn,bhcl->bclhp', Cc, prev, jnp.exp(acs))
    return (y_diag + y_off).reshape(B, S, H, P)


def _ssd_mixer(p, conv_w, conv_b, dt_bias, a_log, d_skip, norm_g):
    B, S, _ = p.shape
    z, xbc, dt_raw = _split(p, [GROUP_W, SSM_CONV_DIM, SSM_HEADS])
    xbc = jax.nn.silu(_causal_dwconv(xbc, conv_w, conv_b))
    xs, bm, cm = _split(xbc, [GROUP_W, SSM_GROUPS * SSM_STATE, SSM_GROUPS * SSM_STATE])
    xs = xs.reshape(B, S, SSM_HEADS, HEAD_DIM)
    rep = SSM_HEADS // SSM_GROUPS
    bh = jnp.repeat(bm.reshape(B, S, SSM_GROUPS, SSM_STATE), rep, axis=2)
    ch = jnp.repeat(cm.reshape(B, S, SSM_GROUPS, SSM_STATE), rep, axis=2)
    dt = jax.nn.softplus((dt_raw + dt_bias).astype(jnp.float32))
    a = -jnp.exp(a_log.astype(jnp.float32))
    y = _ssd_chunked(xs * dt[..., None], dt * a, bh, ch)
    y = (y + xs * d_skip[:, None]).reshape(B, S, GROUP_W)
    yg = (y * jax.nn.silu(z)).reshape(B, S, SSM_GROUPS, GROUP_W // SSM_GROUPS)
    return _rmsnorm(yg, norm_g.reshape(SSM_GROUPS, -1)).reshape(B, S, GROUP_W)


def _nsa_mixer(p, cmp_pe, cmp_w1, cmp_w2, rel_table):
    B, S, _ = p.shape
    G, R = NSA_KV_HEADS, NSA_REP
    q, kc, vc, ks, vs, kw, vw, g_logit, gate = _split(
        p, [GROUP_W] + [NSA_KV_W] * 6 + [3 * NSA_HEADS, GROUP_W])
    q = q.reshape(B, S, G, R, HEAD_DIM).transpose(0, 2, 3, 1, 4) * HEAD_DIM ** -0.5
    kc, vc, ks, vs, kw, vw = [t.reshape(B, S, G, HEAD_DIM) for t in (kc, vc, ks, vs, kw, vw)]
    gates = jax.nn.sigmoid(g_logit.astype(jnp.float32)).reshape(B, S, 3, G, R)

    n_cmp = (S - CMP_BLOCK) // CMP_STRIDE + 1
    cidx = jnp.arange(n_cmp)[:, None] * CMP_STRIDE + jnp.arange(CMP_BLOCK)[None, :]

    def compress(t, j):
        blk = t[:, cidx] + cmp_pe[j][None, None, :, None, :]
        flat = blk.transpose(0, 1, 3, 2, 4).reshape(B, n_cmp, G, CMP_BLOCK * HEAD_DIM)
        return jax.nn.silu(flat @ cmp_w1[j]) @ cmp_w2[j]

    k_cmp = compress(kc, 0)
    v_cmp = compress(vc, 1)
    cmp_start = cidx[:, 0]
    cmp_end = cidx[:, -1]

    n_sel = S // SEL_BLOCK
    top_n = min(SEL_TOPK, n_sel)
    sel_start = jnp.arange(n_sel) * SEL_BLOCK
    overlap = ((cmp_start[:, None] < sel_start[None, :] + SEL_BLOCK)
               & (cmp_start[:, None] + CMP_BLOCK > sel_start[None, :])).astype(jnp.float32)
    ks_blk = ks.reshape(B, n_sel, SEL_BLOCK, G, HEAD_DIM).transpose(0, 3, 1, 2, 4)
    vs_blk = vs.reshape(B, n_sel, SEL_BLOCK, G, HEAD_DIM).transpose(0, 3, 1, 2, 4)
    kw_pad = jnp.pad(kw, ((0, 0), (WINDOW, 0), (0, 0), (0, 0)))
    vw_pad = jnp.pad(vw, ((0, 0), (WINDOW, 0), (0, 0), (0, 0)))
    table_g = rel_table.reshape(REL_BUCKETS, G, R)
    bidx = jnp.arange(B)[:, None, None, None]
    gidx = jnp.arange(G)[None, :, None, None]
    sel_off = jnp.arange(SEL_BLOCK)
    win_off = jnp.arange(WINDOW + Q_BLOCK)
    sel_j = jnp.arange(n_sel)[None, :]

    def head_bias(dist):
        return rel_table[_t5_bucket(dist)].reshape(dist.shape + (G, R)).transpose(2, 3, 0, 1)

    def block(i):
        t0 = i * Q_BLOCK
        qpos = t0 + jnp.arange(Q_BLOCK)
        qb = lax.dynamic_slice_in_dim(q, t0, Q_BLOCK, axis=3)
        dist_c = qpos[:, None] - cmp_end[None, :]
        s_c = jnp.einsum('bgrqd,bcgd->bgrqc', qb, k_cmp).astype(jnp.float32) + head_bias(dist_c)
        p_c = _masked_softmax(s_c, dist_c >= 0)
        o_c = jnp.einsum('bgrqc,bcgd->bgrqd', p_c.astype(v_cmp.dtype), v_cmp)
        imp = jnp.einsum('bgrqc,cj->bgqj', p_c, overlap)
        cur = qpos[:, None] // SEL_BLOCK
        forced = (sel_j == 0) | (sel_j == cur) | (sel_j == cur - 1)
        imp = jnp.where(sel_j <= cur, jnp.where(forced, SEL_FORCE, imp), -SEL_FORCE)
        _, top = lax.top_k(imp, top_n)
        k_sel = ks_blk[bidx, gidx, top].reshape(B, G, Q_BLOCK, top_n * SEL_BLOCK, HEAD_DIM)
        v_sel = vs_blk[bidx, gidx, top].reshape(B, G, Q_BLOCK, top_n * SEL_BLOCK, HEAD_DIM)
        pos_s = (top[..., None] * SEL_BLOCK + sel_off).reshape(B, G, Q_BLOCK, top_n * SEL_BLOCK)
        dist_s = qpos[None, None, :, None] - pos_s
        bias_s = table_g[_t5_bucket(dist_s), gidx].transpose(0, 1, 4, 2, 3)
        s_s = jnp.einsum('bgrqd,bgqtd->bgrqt', qb, k_sel).astype(jnp.float32) + bias_s
        p_s = _masked_softmax(s_s, (dist_s >= 0)[:, :, None])
        o_s = jnp.einsum('bgrqt,bgqtd->bgrqd', p_s.astype(v_sel.dtype), v_sel)
        kwb = lax.dynamic_slice_in_dim(kw_pad, t0, WINDOW + Q_BLOCK, axis=1)
        vwb = lax.dynamic_slice_in_dim(vw_pad, t0, WINDOW + Q_BLOCK, axis=1)
        kpos_w = t0 - WINDOW + win_off
        dist_w = qpos[:, None] - kpos_w[None, :]
        s_w = jnp.einsum('bgrqd,btgd->bgrqt', qb, kwb).astype(jnp.float32) + head_bias(dist_w)
        p_w = _masked_softmax(s_w, (dist_w >= 0) & (dist_w < WINDOW) & (kpos_w[None, :] >= 0))
        o_w = jnp.einsum('bgrqt,btgd->bgrqd', p_w.astype(vwb.dtype), vwb)
        gb = lax.dynamic_slice_in_dim(gates, t0, Q_BLOCK, axis=1).transpose(2, 0, 3, 4, 1)[..., None]
        o = gb[0] * o_c + gb[1] * o_s + gb[2] * o_w
        return o.transpose(0, 3, 1, 2, 4).reshape(B, Q_BLOCK, GROUP_W)

    o = lax.map(block, jnp.arange(S // Q_BLOCK))
    o = o.transpose(1, 0, 2, 3).reshape(B, S, GROUP_W)
    return o * jax.nn.silu(gate)


def _mem_mixer(p, mem_kv):
    B, S, _ = p.shape
    M = mem_kv.shape[1]
    q, gate = _split(p, [GROUP_W, GROUP_W])
    q = q.reshape(B, S, MEM_HEADS, MEM_HEAD_DIM) * MEM_HEAD_DIM ** -0.5
    k, v = _split(mem_kv, [GROUP_W, GROUP_W])
    k = k.reshape(B, M, MEM_HEADS, MEM_HEAD_DIM)
    v = v.reshape(B, M, MEM_HEADS, MEM_HEAD_DIM)
    s = jnp.einsum('bshd,bmhd->bhsm', q, k).astype(jnp.float32)
    pr = jax.nn.softmax(s, axis=-1).astype(v.dtype)
    o = jnp.einsum('bhsm,bmhd->bshd', pr, v).reshape(B, S, GROUP_W)
    return o * jax.nn.silu(gate)


def setup_inputs(seed: int = 0) -> dict:
    key = jax.random.key(seed)
    ks = jax.random.split(key, 20)
    f32 = jnp.float32

    def nrm(k, shape, scale):
        return scale * jax.random.normal(k, shape, f32)

    dt = jnp.exp(jax.random.uniform(ks[7], (DEPTH, SSM_HEADS), f32, math.log(DT_MIN), math.log(DT_MAX)))
    return {
        'x': nrm(ks[0], (BATCH, SEQ, D_MODEL), 1.0),
        'mem': nrm(ks[1], (BATCH, MEM_TOKENS, D_MODEL), 1.0),
        'norm_g': 1.0 + nrm(ks[2], (DEPTH, D_MODEL), 0.02),
        'w_in': nrm(ks[3], (DEPTH, D_MODEL, IN_COLS), D_MODEL ** -0.5),
        'fox_f_bias': FOX_F_BIAS_INIT + nrm(ks[4], (DEPTH, FOX_HEADS), 0.5),
        'ssm_conv_w': nrm(ks[5], (DEPTH, SSM_CONV, SSM_CONV_DIM), SSM_CONV ** -0.5),
        'ssm_conv_b': nrm(ks[6], (DEPTH, SSM_CONV_DIM), 0.02),
        'ssm_dt_bias': dt + jnp.log(-jnp.expm1(-dt)),
        'ssm_a_log': jnp.log(jax.random.uniform(ks[8], (DEPTH, SSM_HEADS), f32, 1.0, 16.0)),
        'ssm_d': 1.0 + nrm(ks[9], (DEPTH, SSM_HEADS), 0.1),
        'ssm_norm_g': 1.0 + nrm(ks[10], (DEPTH, GROUP_W), 0.02),
        'nsa_cmp_pe': nrm(ks[11], (DEPTH, 2, CMP_BLOCK, HEAD_DIM), 0.02),
        'nsa_cmp_w1': nrm(ks[12], (DEPTH, 2, CMP_BLOCK * HEAD_DIM, CMP_HIDDEN), (CMP_BLOCK * HEAD_DIM) ** -0.5),
        'nsa_cmp_w2': nrm(ks[13], (DEPTH, 2, CMP_HIDDEN, HEAD_DIM), CMP_HIDDEN ** -0.5),
        'rel_bias_table': nrm(ks[14], (REL_BUCKETS, NSA_HEADS), 0.5),
        'mem_norm_g': 1.0 + nrm(ks[15], (DEPTH, D_MODEL), 0.02),
        'w_mem_kv': nrm(ks[16], (DEPTH, D_MODEL, 2 * GROUP_W), D_MODEL ** -0.5),
        'w_out': nrm(ks[17], (DEPTH, D_MIX, D_MODEL), D_MIX ** -0.5),
        'final_norm_g': 1.0 + nrm(ks[18], (D_MODEL,), 0.02),
    }


def reference(x, mem, norm_g, w_in, fox_f_bias, ssm_conv_w, ssm_conv_b, ssm_dt_bias, ssm_a_log,
              ssm_d, ssm_norm_g, nsa_cmp_pe, nsa_cmp_w1, nsa_cmp_w2, rel_bias_table, mem_norm_g,
              w_mem_kv, w_out, final_norm_g):
    for l in range(DEPTH):
        h = _rmsnorm(x, norm_g[l])
        p_fox, p_ssm, p_nsa, p_mem = _split(h @ w_in[l], [FOX_COLS, SSM_COLS, NSA_COLS, MEM_COLS])
        mem_kv = _rmsnorm(mem, mem_norm_g[l]) @ w_mem_kv[l]
        o = jnp.concatenate([
            _fox_mixer(p_fox, fox_f_bias[l]),
            _ssd_mixer(p_ssm, ssm_conv_w[l], ssm_conv_b[l], ssm_dt_bias[l], ssm_a_log[l],
                       ssm_d[l], ssm_norm_g[l]),
            _nsa_mixer(p_nsa, nsa_cmp_pe[l], nsa_cmp_w1[l], nsa_cmp_w2[l], rel_bias_table),
            _mem_mixer(p_mem, mem_kv),
        ], axis=-1)
        x = x + o @ w_out[l]
    return _rmsnorm(x, final_norm_g)
```

```python
import functools
import math

import numpy as np
import jax
import jax.numpy as jnp
from jax import lax
from jax.experimental import pallas as pl
from jax.experimental.pallas import tpu as pltpu

F32 = jnp.float32
BF16 = jnp.bfloat16
HIGHEST = lax.Precision.HIGHEST

D_MODEL = 1024
GROUP_W = 512
HEAD_DIM = 64
EPS = 1e-6
NEG_INF = -1e30
TINY = 1e-30

FOX_HEADS = 8
SSM_HEADS = 8
SSM_STATE = 128
SSM_GROUPS = 2
SSM_CONV = 4
SSM_CHUNK = 128
SSM_CONV_DIM = GROUP_W + 2 * SSM_GROUPS * SSM_STATE

NSA_HEADS = 8
NSA_KV_HEADS = 2
NSA_REP = NSA_HEADS // NSA_KV_HEADS
NSA_KV_W = NSA_KV_HEADS * HEAD_DIM
CMP_BLOCK = 32
CMP_STRIDE = 16
CMP_HIDDEN = 2 * HEAD_DIM
SEL_BLOCK = 64
SEL_TOPK = 16
WINDOW = 512
SEL_FORCE = 1e9

MEM_HEADS = 4
MEM_HEAD_DIM = GROUP_W // MEM_HEADS
REL_BUCKETS = 32
REL_MAX_DIST = 128

FOX_COLS = 4 * GROUP_W + FOX_HEADS
SSM_COLS = GROUP_W + SSM_CONV_DIM + SSM_HEADS
NSA_COLS = 2 * GROUP_W + 6 * NSA_KV_W + 3 * NSA_HEADS
MEM_COLS = 2 * GROUP_W

LANES = 128
VMEM_LIMIT = 56 * 1024 * 1024

OFF_FOX_Q, OFF_FOX_K, OFF_FOX_V, OFF_FOX_G = 0, 512, 1024, 1536
OFF_SSM_XBC, OFF_SSM_Z = 2048, 3072
OFF_NSA_Q, OFF_NSA_G = 3584, 4096
OFF_MEM_Q, OFF_MEM_G = 4608, 5120
OFF_SEL_K, OFF_SEL_V, OFF_WIN_K, OFF_WIN_V, OFF_CMP_KV = 5632, 5888, 6144, 6400, 6656
P_COLS = 6912
PS_COLS = 3 * LANES

T_ATT = 256
N_CMP_PAD = 256


def _params(sem):
    return pltpu.CompilerParams(dimension_semantics=sem, vmem_limit_bytes=VMEM_LIMIT)


def _t5_bucket_np(dist):
    n = np.maximum(dist, 0)
    max_exact = REL_BUCKETS // 2
    nf = np.maximum(n, 1).astype(np.float32)
    large = max_exact + (np.log(nf / np.float32(max_exact)) / np.float32(math.log(REL_MAX_DIST / max_exact))
                         * np.float32(REL_BUCKETS - max_exact)).astype(np.int32)
    large = np.minimum(large, REL_BUCKETS - 1)
    return np.where(n < max_exact, n, large).astype(np.int32)


def _silu(x):
    return x * jax.nn.sigmoid(x)


def _norm_proj_kernel(x_ref, g_ref, w_ref, ws_ref, p_ref, ps_ref, *, chunk):
    x = x_ref[...]
    ms = jnp.mean(x * x, axis=-1, keepdims=True)
    h = ((x * lax.rsqrt(ms + EPS)) * g_ref[...]).astype(BF16)
    ncol = p_ref.shape[1]
    for c0 in range(0, ncol, chunk):
        c1 = min(c0 + chunk, ncol)
        p_ref[:, c0:c1] = jnp.dot(h, w_ref[:, c0:c1], preferred_element_type=F32).astype(p_ref.dtype)
    ps_ref[...] = jnp.dot(h, ws_ref[...], preferred_element_type=F32)


def _norm_proj(x2d, g, w_main, w_small, tm=512):
    n, d = x2d.shape
    pc, sc = w_main.shape[1], w_small.shape[1]
    return pl.pallas_call(
        functools.partial(_norm_proj_kernel, chunk=512),
        grid=(n // tm,),
        in_specs=[
            pl.BlockSpec((tm, d), lambda i: (i, 0)),
            pl.BlockSpec((1, d), lambda i: (0, 0)),
            pl.BlockSpec((d, pc), lambda i: (0, 0)),
            pl.BlockSpec((d, sc), lambda i: (0, 0)),
        ],
        out_specs=[
            pl.BlockSpec((tm, pc), lambda i: (i, 0)),
            pl.BlockSpec((tm, sc), lambda i: (i, 0)),
        ],
        out_shape=[jax.ShapeDtypeStruct((n, pc), BF16), jax.ShapeDtypeStruct((n, sc), F32)],
        compiler_params=_params(("parallel",)),
        name="norm_proj",
    )(x2d, g.reshape(1, d), w_main, w_small)


def _bias_table_kernel(tab_ref, idx_ref, o_ref):
    h = pl.program_id(0)
    idx = idx_ref[...]
    acc = jnp.full(idx.shape, NEG_INF, F32)
    for b in range(REL_BUCKETS):
        acc = jnp.where(idx == b, tab_ref[b * NSA_HEADS + h], acc)
    o_ref[0] = acc


def _bias_table(tab_flat, idx):
    r, c = idx.shape
    return pl.pallas_call(
        _bias_table_kernel,
        grid=(NSA_HEADS,),
        in_specs=[
            pl.BlockSpec(memory_space=pltpu.SMEM),
            pl.BlockSpec((r, c), lambda h: (0, 0)),
        ],
        out_specs=pl.BlockSpec((1, r, c), lambda h: (h, 0, 0)),
        out_shape=jax.ShapeDtypeStruct((NSA_HEADS, r, c), F32),
        compiler_params=_params(("arbitrary",)),
        name="t5_bias_table",
    )(tab_flat, jnp.asarray(idx))


def _bias_indices():
    t = T_ATT
    i = np.arange(t)[:, None]
    j = np.arange(WINDOW + t)[None, :]
    d = i + WINDOW - j
    idx_win = np.where((d >= 0) & (d < WINDOW), _t5_bucket_np(d), -1).astype(np.int32)
    j = np.arange(t)[None, :]
    d0 = i - j
    d1 = i - j + t
    idx_sel = np.concatenate([np.where(d0 >= 0, _t5_bucket_np(d0), -1), _t5_bucket_np(d1)], axis=1).astype(np.int32)
    w = np.arange(N_CMP_PAD)[None, :]
    dc = i - CMP_STRIDE * (w - t // CMP_STRIDE) - (CMP_BLOCK - 1)
    idx_cmp = np.where(w < 2 * (t // CMP_STRIDE), np.where(dc >= 0, _t5_bucket_np(dc), -1),
                       REL_BUCKETS - 1).astype(np.int32)
    return idx_win, idx_sel, idx_cmp


def _tri_lower(n):
    r = lax.broadcasted_iota(jnp.int32, (n, n), 0)
    c = lax.broadcasted_iota(jnp.int32, (n, n), 1)
    return (r >= c).astype(F32)


def _fox_cumsum_kernel(f_ref, b_ref, ctok_ref, crow_ref, carry_ref, *, ts):
    @pl.when(pl.program_id(1) == 0)
    def _():
        carry_ref[...] = jnp.zeros_like(carry_ref)

    z = f_ref[...] + b_ref[...]
    logf = jnp.minimum(z, 0.0) - jnp.log(1.0 + jnp.exp(-jnp.abs(z)))
    tri = _tri_lower(LANES)
    carry = carry_ref[...]
    for c in range(ts // LANES):
        blk = logf[c * LANES:(c + 1) * LANES]
        cs = jnp.dot(tri, blk, precision=HIGHEST, preferred_element_type=F32) + carry
        ctok_ref[c * LANES:(c + 1) * LANES, :] = cs
        crow_ref[0, :, c * LANES:(c + 1) * LANES] = cs.T[0:FOX_HEADS, :]
        carry = cs[LANES - 1:LANES, :]
    carry_ref[...] = carry


def _fox_cumsum(ps, f_bias_pad, b, s, ts=512):
    ns = s // ts
    return pl.pallas_call(
        functools.partial(_fox_cumsum_kernel, ts=ts),
        grid=(b, ns),
        in_specs=[
            pl.BlockSpec((ts, LANES), lambda bi, si: (bi * ns + si, 0)),
            pl.BlockSpec((1, LANES), lambda bi, si: (0, 0)),
        ],
        out_specs=[
            pl.BlockSpec((ts, LANES), lambda bi, si: (bi * ns + si, 0)),
            pl.BlockSpec((1, FOX_HEADS, ts), lambda bi, si: (bi, 0, si)),
        ],
        out_shape=[jax.ShapeDtypeStruct((b * s, LANES), F32), jax.ShapeDtypeStruct((b, FOX_HEADS, s), F32)],
        scratch_shapes=[pltpu.VMEM((1, LANES), F32)],
        compiler_params=_params(("parallel", "arbitrary")),
        name="fox_cumsum",
    )(ps, f_bias_pad)


def _fox_attn_kernel(q_ref, k_ref, v_ref, gate_ref, ctok_ref, crow_ref, o_ref, acc_ref, *, t):
    qi = pl.program_id(1)
    lo = lax.broadcasted_iota(jnp.int32, (t, LANES), 1) < HEAD_DIM
    row = lax.broadcasted_iota(jnp.int32, (t, t), 0)
    col = lax.broadcasted_iota(jnp.int32, (t, t), 1)
    causal = col <= row
    ctok = ctok_ref[...]
    zero = jnp.zeros((t, LANES), BF16)

    for hp in range(FOX_HEADS // 2):
        ls = slice(hp * LANES, (hp + 1) * LANES)
        qp = q_ref[:, ls]
        qs = (jnp.where(lo, qp, zero), jnp.where(lo, zero, qp))
        cc = (ctok[:, 2 * hp:2 * hp + 1], ctok[:, 2 * hp + 1:2 * hp + 2])

        def tile(ki, carry, diag):
            ks = pl.multiple_of(ki * t, t)
            kp = k_ref[pl.ds(ks, t), ls]
            vp = v_ref[pl.ds(ks, t), ls]
            cr = crow_ref[0, :, pl.ds(ks, t)]
            new, pvs, alphas = [], [], []
            for e in range(2):
                m, l = carry[2 * e], carry[2 * e + 1]
                s = lax.dot_general(qs[e], kp, (((1,), (1,)), ((), ())), preferred_element_type=F32)
                s = s + cc[e] - cr[2 * hp + e:2 * hp + e + 1, :]
                if diag:
                    s = jnp.where(causal, s, NEG_INF)
                m_new = jnp.maximum(m, jnp.max(s, axis=1, keepdims=True))
                p = jnp.exp(s - m_new)
                alpha = jnp.exp(m - m_new)
                l_new = alpha * l + jnp.sum(p, axis=1, keepdims=True)
                pvs.append(jnp.dot(p.astype(BF16), vp, preferred_element_type=F32))
                alphas.append(alpha)
                new += [m_new, l_new]
            acc_ref[...] = acc_ref[...] * jnp.where(lo, alphas[0], alphas[1]) + jnp.where(lo, pvs[0], pvs[1])
            return tuple(new)

        acc_ref[...] = jnp.zeros_like(acc_ref)
        m0 = jnp.full((t, 1), NEG_INF, F32)
        l0 = jnp.zeros((t, 1), F32)
        carry = tile(qi, (m0, l0, m0, l0), True)
        carry = lax.fori_loop(0, qi, functools.partial(tile, diag=False), carry)
        l = jnp.where(lo, carry[1], carry[3])
        o = acc_ref[...] / jnp.maximum(l, TINY)
        g = gate_ref[:, ls].astype(F32)
        o_ref[:, ls] = (o * _silu(g)).astype(o_ref.dtype)


def _fox_attn(p, ctok, crow, b, s, t=256):
    nq = s // t
    w = GROUP_W
    return pl.pallas_call(
        functools.partial(_fox_attn_kernel, t=t),
        grid=(b, nq),
        in_specs=[
            pl.BlockSpec((t, w), lambda bi, qi: (bi * nq + qi, OFF_FOX_Q // w)),
            pl.BlockSpec((s, w), lambda bi, qi: (bi, OFF_FOX_K // w)),
            pl.BlockSpec((s, w), lambda bi, qi: (bi, OFF_FOX_V // w)),
            pl.BlockSpec((t, w), lambda bi, qi: (bi * nq + qi, OFF_FOX_G // w)),
            pl.BlockSpec((t, LANES), lambda bi, qi: (bi * nq + qi, 0)),
            pl.BlockSpec((1, FOX_HEADS, s), lambda bi, qi: (bi, 0, 0)),
        ],
        out_specs=pl.BlockSpec((t, w), lambda bi, qi: (bi * nq + qi, 0)),
        out_shape=jax.ShapeDtypeStruct((b * s, w), BF16),
        scratch_shapes=[pltpu.VMEM((t, LANES), F32)],
        compiler_params=_params(("parallel", "arbitrary")),
        name="fox_attn",
    )(p, p, p, p, ctok, crow)


def _ssd_kernel(z_ref, xbc_ref, dt_ref, cw_ref, cb_ref, dtb_ref, alog_ref, dsk_ref, ng_ref, o_ref,
                xpad_ref, state_ref, y_ref):
    q = SSM_CHUNK
    halo = 8

    @pl.when(pl.program_id(1) == 0)
    def _():
        xpad_ref[0:halo, :] = jnp.zeros((halo, SSM_CONV_DIM), F32)
        state_ref[...] = jnp.zeros_like(state_ref)

    xpad_ref[halo:halo + q, :] = xbc_ref[...].astype(F32)
    y = cb_ref[...]
    for k in range(SSM_CONV):
        off = halo - (SSM_CONV - 1) + k
        y = y + cw_ref[k:k + 1, :] * xpad_ref[off:off + q, :]
    xpad_ref[0:halo, :] = xpad_ref[q:q + halo, :]
    xc = _silu(y)
    xs = xc[:, :GROUP_W]

    x_dt = dt_ref[...] + dtb_ref[...]
    dt = jnp.maximum(x_dt, 0.0) + jnp.log(1.0 + jnp.exp(-jnp.abs(x_dt)))
    a = dt * (-jnp.exp(alog_ref[...]))
    tri = _tri_lower(q)
    acs = jnp.dot(tri, a, precision=HIGHEST, preferred_element_type=F32)
    acs_t = acs.T
    er = lax.broadcasted_iota(jnp.int32, (LANES, GROUP_W), 0)
    ec = lax.broadcasted_iota(jnp.int32, (LANES, GROUP_W), 1)
    expand = (ec // HEAD_DIM == er).astype(F32)
    dt_full = jnp.dot(dt, expand, precision=HIGHEST, preferred_element_type=F32)
    eacs_full = jnp.dot(jnp.exp(acs), expand, precision=HIGHEST, preferred_element_type=F32)
    dec_full = jnp.dot(jnp.exp(acs[q - 1:q, :] - acs), expand, precision=HIGHEST, preferred_element_type=F32)
    xdt = xs * dt_full
    xdt_b = xdt.astype(BF16)
    xdec_b = (xdt * dec_full).astype(BF16)

    row = lax.broadcasted_iota(jnp.int32, (q, q), 0)
    col = lax.broadcasted_iota(jnp.int32, (q, q), 1)
    causal = row >= col
    lo = lax.broadcasted_iota(jnp.int32, (q, LANES), 1) < HEAD_DIM
    gw = GROUP_W // SSM_GROUPS
    hpg = SSM_HEADS // SSM_GROUPS
    for g in range(SSM_GROUPS):
        bm = xc[:, GROUP_W + g * SSM_STATE:GROUP_W + (g + 1) * SSM_STATE]
        cm = xc[:, GROUP_W + (SSM_GROUPS + g) * SSM_STATE:GROUP_W + (SSM_GROUPS + g + 1) * SSM_STATE]
        bm_b = bm.astype(BF16)
        cm_b = cm.astype(BF16)
        gs = slice(g * gw, (g + 1) * gw)
        cbg = lax.dot_general(cm_b, bm_b, (((1,), (1,)), ((), ())), preferred_element_type=F32)
        st = state_ref[:, gs]
        y_off = jnp.dot(cm_b, st.astype(BF16), preferred_element_type=F32) * eacs_full[:, gs]
        cst = jnp.dot(bm.T.astype(BF16), xdec_b[:, gs], preferred_element_type=F32)
        state_ref[:, gs] = st * eacs_full[q - 1:q, gs] + cst
        for hp in range(hpg // 2):
            ls = slice(g * gw + hp * LANES, g * gw + (hp + 1) * LANES)
            yd = []
            for e in range(2):
                h = g * hpg + 2 * hp + e
                seg = jnp.exp(jnp.where(causal, acs[:, h:h + 1] - acs_t[h:h + 1, :], NEG_INF))
                yd.append(jnp.dot((cbg * seg).astype(BF16), xdt_b[:, ls], preferred_element_type=F32))
            y_ref[:, ls] = jnp.where(lo, yd[0], yd[1]) + y_off[:, hp * LANES:(hp + 1) * LANES]

    yz = (y_ref[...] + xs * dsk_ref[...]) * _silu(z_ref[...].astype(F32))
    for g in range(SSM_GROUPS):
        gs = slice(g * gw, (g + 1) * gw)
        blk = yz[:, gs]
        ms = jnp.mean(blk * blk, axis=-1, keepdims=True)
        o_ref[:, gs] = ((blk * lax.rsqrt(ms + EPS)) * ng_ref[:, gs]).astype(o_ref.dtype)


def _ssd(p, ps, conv_w, conv_b, dt_bias_pad, a_log_pad, d_full, norm_g, b, s):
    q = SSM_CHUNK
    nc = s // q
    row = lambda bi, ci: (bi * nc + ci)
    const = lambda bi, ci: (0, 0)
    return pl.pallas_call(
        _ssd_kernel,
        grid=(b, nc),
        in_specs=[
            pl.BlockSpec((q, GROUP_W), lambda bi, ci: (row(bi, ci), OFF_SSM_Z // GROUP_W)),
            pl.BlockSpec((q, SSM_CONV_DIM), lambda bi, ci: (row(bi, ci), OFF_SSM_XBC // SSM_CONV_DIM)),
            pl.BlockSpec((q, LANES), lambda bi, ci: (row(bi, ci), 1)),
            pl.BlockSpec((SSM_CONV, SSM_CONV_DIM), const),
            pl.BlockSpec((1, SSM_CONV_DIM), const),
            pl.BlockSpec((1, LANES), const),
            pl.BlockSpec((1, LANES), const),
            pl.BlockSpec((1, GROUP_W), const),
            pl.BlockSpec((1, GROUP_W), const),
        ],
        out_specs=pl.BlockSpec((q, GROUP_W), lambda bi, ci: (row(bi, ci), 0)),
        out_shape=jax.ShapeDtypeStruct((b * s, GROUP_W), BF16),
        scratch_shapes=[
            pltpu.VMEM((q + 8, SSM_CONV_DIM), F32),
            pltpu.VMEM((SSM_STATE, GROUP_W), F32),
            pltpu.VMEM((q, GROUP_W), F32),
        ],
        compiler_params=_params(("parallel", "arbitrary")),
        name="ssd",
    )(p, p, ps, conv_w, conv_b, dt_bias_pad, a_log_pad, d_full, norm_g)


def _compress_kernel(r_ref, w1a_ref, w1b_ref, w2_ref, pe_ref, o_ref):
    half = CMP_STRIDE * HEAD_DIM
    r = r_ref[0, 0]
    w1a = w1a_ref[0]
    w1b = w1b_ref[0]
    a = jnp.dot(r, w1a, preferred_element_type=F32)
    bm = jnp.dot(r, w1b, preferred_element_type=F32)
    pe = pe_ref[0].astype(BF16)
    const = (jnp.dot(pe[:, :half], w1a, preferred_element_type=F32)
             + jnp.dot(pe[:, half:], w1b, preferred_element_type=F32))[0:1]
    n = bm.shape[0]
    h = _silu(a + pltpu.roll(bm, n - 1, 0) + const)
    o_ref[0, 0] = jnp.dot(h.astype(BF16), w2_ref[0], preferred_element_type=F32).astype(o_ref.dtype)


def _compress(r, w1a, w1b, w2dup, pe8):
    b, nslot, nr, width = r.shape
    return pl.pallas_call(
        _compress_kernel,
        grid=(b, nslot),
        in_specs=[
            pl.BlockSpec((1, 1, nr, width), lambda bi, si: (bi, si, 0, 0)),
            pl.BlockSpec((1, width, CMP_HIDDEN), lambda bi, si: (si // NSA_KV_HEADS, 0, 0)),
            pl.BlockSpec((1, width, CMP_HIDDEN), lambda bi, si: (si // NSA_KV_HEADS, 0, 0)),
            pl.BlockSpec((1, CMP_HIDDEN, LANES), lambda bi, si: (si // NSA_KV_HEADS, 0, 0)),
            pl.BlockSpec((1, 8, 2 * width), lambda bi, si: (si // NSA_KV_HEADS, 0, 0)),
        ],
        out_specs=pl.BlockSpec((1, 1, nr, LANES), lambda bi, si: (bi, si, 0, 0)),
        out_shape=jax.ShapeDtypeStruct((b, nslot, nr, LANES), BF16),
        compiler_params=_params(("parallel", "arbitrary")),
        name="nsa_compress",
    )(r, w1a, w1b, w2dup, pe8)


def _cmp_select_kernel(q_ref, kv_ref, gl_ref, gate_ref, fc_ref, ov_ref, o_ref, ns_ref, *, n_cmp):
    t = T_ATT
    ncp = kv_ref.shape[2]
    qi = pl.program_id(1)
    t0 = qi * t
    lo = lax.broadcasted_iota(jnp.int32, (t, LANES), 1) < HEAD_DIM
    zero = jnp.zeros((t, LANES), BF16)
    row = lax.broadcasted_iota(jnp.int32, (t, ncp), 0)
    col = lax.broadcasted_iota(jnp.int32, (t, ncp), 1)
    valid = (col * CMP_STRIDE + (CMP_BLOCK - 1) <= t0 + row) & (col < n_cmp)
    per_tile = t // CMP_STRIDE
    shift = lax.rem(qi * per_tile + (ncp - per_tile), ncp)
    beyond = col >= (qi + 1) * per_tile
    gl = gl_ref[...]

    nsel = SEL_BLOCK
    jrow = lax.broadcasted_iota(jnp.int32, (nsel, t), 0)
    cur = (t0 + lax.broadcasted_iota(jnp.int32, (nsel, t), 1)) // SEL_BLOCK
    forced = (jrow == 0) | (jrow == cur) | (jrow == cur - 1)
    past = jrow <= cur

    for g in range(NSA_KV_HEADS):
        kc = kv_ref[0, g]
        vc = kv_ref[0, NSA_KV_HEADS + g]
        psum = jnp.zeros((t, ncp), F32)
        outs = []
        for r in range(NSA_REP):
            h = g * NSA_REP + r
            qp = q_ref[:, (h // 2) * LANES:(h // 2 + 1) * LANES]
            qe = jnp.where(lo, qp, zero) if r % 2 == 0 else jnp.where(lo, zero, qp)
            s = lax.dot_general(qe, kc, (((1,), (1,)), ((), ())), preferred_element_type=F32)
            bias = jnp.where(beyond, NEG_INF, pltpu.roll(fc_ref[h], shift, 1))
            s = jnp.where(valid, s + bias, NEG_INF)
            m = jnp.max(s, axis=1, keepdims=True)
            e = jnp.where(valid, jnp.exp(s - m), 0.0)
            pr = e / jnp.maximum(jnp.sum(e, axis=1, keepdims=True), TINY)
            psum = psum + pr
            oc = jnp.dot(pr.astype(BF16), vc, preferred_element_type=F32)
            gc = jax.nn.sigmoid(gl[:, h:h + 1])
            outs.append(oc * gc)
        for hp in range(NSA_REP // 2):
            ls = slice((g * NSA_REP // 2 + hp) * LANES, (g * NSA_REP // 2 + hp + 1) * LANES)
            o = jnp.where(lo, outs[2 * hp], outs[2 * hp + 1])
            o_ref[:, ls] = (o * _silu(gate_ref[:, ls].astype(F32))).astype(o_ref.dtype)

        imp = jnp.dot(psum, ov_ref[...], precision=HIGHEST, preferred_element_type=F32)
        imp_t = imp.T[0:nsel, :]
        imp_t = jnp.where(past, jnp.where(forced, SEL_FORCE, imp_t), -SEL_FORCE)
        rank = jnp.zeros((nsel, t), F32)
        for i in range(nsel):
            bi = imp_t[i:i + 1, :]
            gt = jnp.where(bi > imp_t, 1.0, 0.0)
            ge = jnp.where(bi >= imp_t, 1.0, 0.0)
            rank = rank + jnp.where(jrow > i, ge, gt)
        notsel = jnp.where((rank < float(SEL_TOPK)) & past, 0.0, 1.0)
        ns2 = jnp.concatenate([notsel, notsel], axis=0).T
        ns_ref[:, g * LANES:(g + 1) * LANES] = ns2.astype(ns_ref.dtype)


def _cmp_select(p, ps, kvc, fc, overlap, b, s, n_cmp):
    t = T_ATT
    nq = s // t
    w = GROUP_W
    ncp = kvc.shape[2]
    return pl.pallas_call(
        functools.partial(_cmp_select_kernel, n_cmp=n_cmp),
        grid=(b, nq),
        in_specs=[
            pl.BlockSpec((t, w), lambda bi, qi: (bi * nq + qi, OFF_NSA_Q // w)),
            pl.BlockSpec((1, 2 * NSA_KV_HEADS, ncp, LANES), lambda bi, qi: (bi, 0, 0, 0)),
            pl.BlockSpec((t, LANES), lambda bi, qi: (bi * nq + qi, 2)),
            pl.BlockSpec((t, w), lambda bi, qi: (bi * nq + qi, OFF_NSA_G // w)),
            pl.BlockSpec((NSA_HEADS, t, ncp), lambda bi, qi: (0, 0, 0)),
            pl.BlockSpec((ncp, LANES), lambda bi, qi: (0, 0)),
        ],
        out_specs=[
            pl.BlockSpec((t, w), lambda bi, qi: (bi * nq + qi, 0)),
            pl.BlockSpec((t, NSA_KV_HEADS * LANES), lambda bi, qi: (bi * nq + qi, 0)),
        ],
        out_shape=[jax.ShapeDtypeStruct((b * s, w), BF16),
                   jax.ShapeDtypeStruct((b * s, NSA_KV_HEADS * LANES), BF16)],
        compiler_params=_params(("parallel", "arbitrary")),
        name="nsa_cmp_select",
    )(p, kvc, ps, p, fc, overlap)


def _sel_attn_kernel(tab_ref, q_ref, ns_ref, k_ref, v_ref, gl_ref, gate_ref, bs_ref, o_ref, acc_ref):
    t = T_ATT
    qi = pl.program_id(1)
    lane = lax.broadcasted_iota(jnp.int32, (t, LANES), 1)
    lo = lane < HEAD_DIM
    krow = lax.broadcasted_iota(jnp.int32, (t, LANES), 0)
    gl = gl_ref[...]
    prev_pen = jnp.where(qi >= 1, 0.0, NEG_INF)
    lo_f = jnp.where(lo, 1.0, 0.0)
    hi_f = 1.0 - lo_f
    lo_b = lo_f.astype(BF16)
    hi_b = hi_f.astype(BF16)

    for g in range(NSA_KV_HEADS):
        gls = slice(g * LANES, (g + 1) * LANES)
        ns = ns_ref[:, gls]
        qa = []
        for r in range(NSA_REP):
            h = g * NSA_REP + r
            qp = q_ref[:, (h // 2) * LANES:(h // 2 + 1) * LANES]
            qa.append(qp * lo_b + ns * hi_b if r % 2 == 0 else ns * lo_b + qp * hi_b)

        def tile(ki, carry, kind):
            ks = pl.multiple_of(ki * t, t)
            kd = k_ref[pl.ds(ks, t), gls]
            vd = v_ref[pl.ds(ks, t), gls]
            blk = (ks + krow) // SEL_BLOCK
            hot = jnp.where((lane % HEAD_DIM) == blk, NEG_INF, 0.0)
            k_even = kd * lo_b + (hot * hi_f).astype(BF16)
            k_odd = (hot * lo_f).astype(BF16) + kd * hi_b
            new, pvs, alphas = [], [], []
            for r in range(NSA_REP):
                h = g * NSA_REP + r
                m, l = carry[2 * r], carry[2 * r + 1]
                s = lax.dot_general(qa[r], k_even if r % 2 == 0 else k_odd, (((1,), (1,)), ((), ())),
                                    preferred_element_type=F32)
                if kind == "diag":
                    s = s + bs_ref[h, :, 0:t]
                elif kind == "prev":
                    s = s + (bs_ref[h, :, t:2 * t] + prev_pen)
                else:
                    s = s + tab_ref[(REL_BUCKETS - 1) * NSA_HEADS + h]
                m_new = jnp.maximum(m, jnp.max(s, axis=1, keepdims=True))
                p = jnp.exp(s - m_new)
                alpha = jnp.exp(m - m_new)
                new += [m_new, alpha * l + jnp.sum(p, axis=1, keepdims=True)]
                pvs.append(jnp.dot(p.astype(BF16), vd, preferred_element_type=F32))
                alphas.append(alpha)
            for hp in range(NSA_REP // 2):
                ls = slice(hp * LANES, (hp + 1) * LANES)
                acc_ref[:, ls] = (acc_ref[:, ls] * jnp.where(lo, alphas[2 * hp], alphas[2 * hp + 1])
                                  + jnp.where(lo, pvs[2 * hp], pvs[2 * hp + 1]))
            return tuple(new)

        acc_ref[...] = jnp.zeros_like(acc_ref)
        m0 = jnp.full((t, 1), NEG_INF, F32)
        l0 = jnp.zeros((t, 1), F32)
        carry = tile(qi, (m0, l0) * NSA_REP, "diag")
        carry = tile(jnp.maximum(qi - 1, 0), carry, "prev")
        carry = lax.fori_loop(0, jnp.maximum(qi - 1, 0), functools.partial(tile, kind="far"), carry)
        for hp in range(NSA_REP // 2):
            h0 = g * NSA_REP + 2 * hp
            ls = slice(hp * LANES, (hp + 1) * LANES)
            ols = slice((g * NSA_REP // 2 + hp) * LANES, (g * NSA_REP // 2 + hp + 1) * LANES)
            l = jnp.where(lo, carry[4 * hp + 1], carry[4 * hp + 3])
            gs = jnp.where(lo, jax.nn.sigmoid(gl[:, NSA_HEADS + h0:NSA_HEADS + h0 + 1]),
                           jax.nn.sigmoid(gl[:, NSA_HEADS + h0 + 1:NSA_HEADS + h0 + 2]))
            o = acc_ref[:, ls] / jnp.maximum(l, TINY) * gs
            o_ref[:, ols] = (o * _silu(gate_ref[:, ols].astype(F32))).astype(o_ref.dtype)


def _sel_attn(tab_flat, p, ps, notsel, bsel, b, s):
    t = T_ATT
    nq = s // t
    w = GROUP_W
    kw = NSA_KV_HEADS * LANES
    return pl.pallas_call(
        _sel_attn_kernel,
        grid=(b, nq),
        in_specs=[
            pl.BlockSpec(memory_space=pltpu.SMEM),
            pl.BlockSpec((t, w), lambda bi, qi: (bi * nq + qi, OFF_NSA_Q // w)),
            pl.BlockSpec((t, kw), lambda bi, qi: (bi * nq + qi, 0)),
            pl.BlockSpec((s, kw), lambda bi, qi: (bi, OFF_SEL_K // kw)),
            pl.BlockSpec((s, kw), lambda bi, qi: (bi, OFF_SEL_V // kw)),
            pl.BlockSpec((t, LANES), lambda bi, qi: (bi * nq + qi, 2)),
            pl.BlockSpec((t, w), lambda bi, qi: (bi * nq + qi, OFF_NSA_G // w)),
            pl.BlockSpec((NSA_HEADS, t, 2 * t), lambda bi, qi: (0, 0, 0)),
        ],
        out_specs=pl.BlockSpec((t, w), lambda bi, qi: (bi * nq + qi, 0)),
        out_shape=jax.ShapeDtypeStruct((b * s, w), BF16),
        scratch_shapes=[pltpu.VMEM((t, NSA_REP // 2 * LANES), F32)],
        compiler_params=_params(("parallel", "arbitrary")),
        name="nsa_sel_attn",
    )(tab_flat, p, notsel, p, p, ps, p, bsel)


def _win_attn_kernel(q_ref, k_ref, v_ref, gl_ref, gate_ref, bw_ref, o_ref):
    t = T_ATT
    nt = WINDOW // t + 1
    qi = pl.program_id(1)
    lo = lax.broadcasted_iota(jnp.int32, (t, LANES), 1) < HEAD_DIM
    zero = jnp.zeros((t, LANES), BF16)
    gl = gl_ref[...]

    for g in range(NSA_KV_HEADS):
        gls = slice(g * LANES, (g + 1) * LANES)
        kts, vts, pens = [], [], []
        for j in range(nt):
            kt = qi - (nt - 1) + j
            ks = pl.multiple_of(jnp.maximum(kt, 0) * t, t)
            kts.append(k_ref[pl.ds(ks, t), gls])
            vts.append(v_ref[pl.ds(ks, t), gls])
            pens.append(jnp.where(kt >= 0, 0.0, NEG_INF))
        outs = []
        for r in range(NSA_REP):
            h = g * NSA_REP + r
            qp = q_ref[:, (h // 2) * LANES:(h // 2 + 1) * LANES]
            qe = jnp.where(lo, qp, zero) if r % 2 == 0 else jnp.where(lo, zero, qp)
            ss = []
            for j in range(nt):
                s = lax.dot_general(qe, kts[j], (((1,), (1,)), ((), ())), preferred_element_type=F32)
                ss.append(s + (bw_ref[h, :, j * t:(j + 1) * t] + pens[j]))
            m = functools.reduce(jnp.maximum, [jnp.max(s, axis=1, keepdims=True) for s in ss])
            ps = [jnp.exp(s - m) for s in ss]
            l = functools.reduce(lambda a, c: a + c, [jnp.sum(p, axis=1, keepdims=True) for p in ps])
            o = functools.reduce(lambda a, c: a + c,
                                 [jnp.dot(p.astype(BF16), v, preferred_element_type=F32) for p, v in zip(ps, vts)])
            gw = jax.nn.sigmoid(gl[:, 2 * NSA_HEADS + h:2 * NSA_HEADS + h + 1])
            outs.append(o / jnp.maximum(l, TINY) * gw)
        for hp in range(NSA_REP // 2):
            ols = slice((g * NSA_REP // 2 + hp) * LANES, (g * NSA_REP // 2 + hp + 1) * LANES)
            o = jnp.where(lo, outs[2 * hp], outs[2 * hp + 1])
            o_ref[:, ols] = (o * _silu(gate_ref[:, ols].astype(F32))).astype(o_ref.dtype)


def _win_attn(p, ps, bwin, b, s):
    t = T_ATT
    nq = s // t
    w = GROUP_W
    kw = NSA_KV_HEADS * LANES
    return pl.pallas_call(
        _win_attn_kernel,
        grid=(b, nq),
        in_specs=[
            pl.BlockSpec((t, w), lambda bi, qi: (bi * nq + qi, OFF_NSA_Q // w)),
            pl.BlockSpec((s, kw), lambda bi, qi: (bi, OFF_WIN_K // kw)),
            pl.BlockSpec((s, kw), lambda bi, qi: (bi, OFF_WIN_V // kw)),
            pl.BlockSpec((t, LANES), lambda bi, qi: (bi * nq + qi, 2)),
            pl.BlockSpec((t, w), lambda bi, qi: (bi * nq + qi, OFF_NSA_G // w)),
            pl.BlockSpec((NSA_HEADS, t, WINDOW + t), lambda bi, qi: (0, 0, 0)),
        ],
        out_specs=pl.BlockSpec((t, w), lambda bi, qi: (bi * nq + qi, 0)),
        out_shape=jax.ShapeDtypeStruct((b * s, w), BF16),
        compiler_params=_params(("parallel", "arbitrary")),
        name="nsa_win_attn",
    )(p, p, p, ps, p, bwin)


def _mem_attn_kernel(q_ref, gate_ref, kv_ref, o_ref):
    scale = MEM_HEAD_DIM ** -0.5
    for h in range(MEM_HEADS):
        ls = slice(h * LANES, (h + 1) * LANES)
        k = kv_ref[:, ls]
        v = kv_ref[:, GROUP_W + h * LANES:GROUP_W + (h + 1) * LANES]
        s = lax.dot_general(q_ref[:, ls], k, (((1,), (1,)), ((), ())), preferred_element_type=F32) * scale
        m = jnp.max(s, axis=1, keepdims=True)
        e = jnp.exp(s - m)
        l = jnp.sum(e, axis=1, keepdims=True)
        o = jnp.dot(e.astype(BF16), v, preferred_element_type=F32) / l
        o_ref[:, ls] = (o * _silu(gate_ref[:, ls].astype(F32))).astype(o_ref.dtype)


def _mem_attn(p, mem_kv, b, s, t=512):
    nq = s // t
    w = GROUP_W
    m = mem_kv.shape[0] // b
    return pl.pallas_call(
        _mem_attn_kernel,
        grid=(b, nq),
        in_specs=[
            pl.BlockSpec((t, w), lambda bi, qi: (bi * nq + qi, OFF_MEM_Q // w)),
            pl.BlockSpec((t, w), lambda bi, qi: (bi * nq + qi, OFF_MEM_G // w)),
            pl.BlockSpec((m, 2 * w), lambda bi, qi: (bi, 0)),
        ],
        out_specs=pl.BlockSpec((t, w), lambda bi, qi: (bi * nq + qi, 0)),
        out_shape=jax.ShapeDtypeStruct((b * s, w), BF16),
        compiler_params=_params(("parallel", "arbitrary")),
        name="mem_attn",
    )(p, p, mem_kv)


def _out_proj_kernel(x_ref, of_ref, os_ref, oc_ref, osel_ref, ow_ref, om_ref, w_ref, g_ref, o_ref, *, final):
    w = GROUP_W
    nsa = (oc_ref[...].astype(F32) + osel_ref[...].astype(F32) + ow_ref[...].astype(F32)).astype(BF16)
    acc = x_ref[...]
    for i, part in enumerate((of_ref[...], os_ref[...], nsa, om_ref[...])):
        acc = acc + jnp.dot(part, w_ref[i * w:(i + 1) * w, :], preferred_element_type=F32)
    if final:
        ms = jnp.mean(acc * acc, axis=-1, keepdims=True)
        acc = (acc * lax.rsqrt(ms + EPS)) * g_ref[...]
    o_ref[...] = acc


def _out_proj(x2d, parts, w_out, g, final, tm=512):
    n, d = x2d.shape
    w = GROUP_W
    part_spec = pl.BlockSpec((tm, w), lambda i: (i, 0))
    return pl.pallas_call(
        functools.partial(_out_proj_kernel, final=final),
        grid=(n // tm,),
        in_specs=[pl.BlockSpec((tm, d), lambda i: (i, 0))] + [part_spec] * 6 + [
            pl.BlockSpec((4 * w, d), lambda i: (0, 0)),
            pl.BlockSpec((1, d), lambda i: (0, 0)),
        ],
        out_specs=pl.BlockSpec((tm, d), lambda i: (i, 0)),
        out_shape=jax.ShapeDtypeStruct((n, d), F32),
        compiler_params=_params(("parallel",)),
        name="out_proj",
    )(x2d, *parts, w_out, g.reshape(1, d))


def _pack_indices():
    fox, ssm, nsa, mem = 0, FOX_COLS, FOX_COLS + SSM_COLS, FOX_COLS + SSM_COLS + NSA_COLS
    w = GROUP_W
    rng = lambda a, n: list(range(a, a + n))
    kv = lambda slot: nsa + w + slot * NSA_KV_W
    dup = lambda base: (rng(base, HEAD_DIM) * 2) + (rng(base + HEAD_DIM, HEAD_DIM) * 2)
    main = (rng(fox, 4 * w)
            + rng(ssm + w, SSM_CONV_DIM) + rng(ssm, w)
            + rng(nsa, w) + rng(nsa + w + 6 * NSA_KV_W + 3 * NSA_HEADS, w)
            + rng(mem, 2 * w)
            + dup(kv(2)) + dup(kv(3)) + dup(kv(4)) + dup(kv(5))
            + rng(kv(0), NSA_KV_W) + rng(kv(1), NSA_KV_W))
    assert len(main) == P_COLS
    scale = np.ones((P_COLS,), np.float32)
    scale[OFF_FOX_Q:OFF_FOX_Q + w] = HEAD_DIM ** -0.5
    scale[OFF_NSA_Q:OFF_NSA_Q + w] = HEAD_DIM ** -0.5
    small = np.zeros((PS_COLS,), np.int32)
    keep = np.zeros((PS_COLS,), np.float32)
    for blk, (src, n) in enumerate(((fox + 4 * w, FOX_HEADS), (ssm + w + SSM_CONV_DIM, SSM_HEADS),
                                    (nsa + w + 6 * NSA_KV_W, 3 * NSA_HEADS))):
        small[blk * LANES:blk * LANES + n] = np.arange(src, src + n)
        keep[blk * LANES:blk * LANES + n] = 1.0
    return np.asarray(main, np.int32), scale, small, keep


def _pad_lanes(v):
    return jnp.pad(v.astype(F32), (0, LANES - v.shape[0])).reshape(1, LANES)


def _trunk(x, mem, norm_g, w_in, fox_f_bias, ssm_conv_w, ssm_conv_b, ssm_dt_bias, ssm_a_log, ssm_d,
           ssm_norm_g, nsa_cmp_pe, nsa_cmp_w1, nsa_cmp_w2, rel_bias_table, mem_norm_g, w_mem_kv, w_out,
           final_norm_g):
    b, s, d = x.shape
    depth = w_in.shape[0]
    n = b * s
    m_tok = mem.shape[1]
    n_cmp = (s - CMP_BLOCK) // CMP_STRIDE + 1
    n_rows = s // CMP_STRIDE
    assert s % 512 == 0 and s // SEL_BLOCK <= HEAD_DIM and n_rows <= N_CMP_PAD and d == D_MODEL

    main_idx, main_scale, small_idx, small_keep = _pack_indices()
    tab_flat = rel_bias_table.astype(F32).reshape(-1)
    idx_win, idx_sel, idx_cmp = _bias_indices()
    bwin = _bias_table(tab_flat, idx_win)
    bsel = _bias_table(tab_flat, idx_sel)
    fcmp = _bias_table(tab_flat, idx_cmp)

    cs = np.arange(N_CMP_PAD)[:, None] * CMP_STRIDE
    js = np.arange(LANES)[None, :] * SEL_BLOCK
    overlap = ((cs < js + SEL_BLOCK) & (cs + CMP_BLOCK > js) & (np.arange(N_CMP_PAD)[:, None] < n_cmp)
               & (np.arange(LANES)[None, :] < s // SEL_BLOCK)).astype(np.float32)
    overlap = jnp.asarray(overlap)

    x2d = x.reshape(n, d)
    mem2d = mem.reshape(b * m_tok, d)
    half = CMP_STRIDE * HEAD_DIM
    for l in range(depth):
        w_main = (w_in[l][:, main_idx] * main_scale).astype(BF16)
        w_small = (w_in[l][:, small_idx] * small_keep).astype(BF16)
        p, ps = _norm_proj(x2d, norm_g[l], w_main, w_small)

        ctok, crow = _fox_cumsum(ps, _pad_lanes(fox_f_bias[l]), b, s)
        o_fox = _fox_attn(p, ctok, crow, b, s)

        o_ssd = _ssd(p, ps, ssm_conv_w[l].astype(F32), ssm_conv_b[l].reshape(1, -1).astype(F32),
                     _pad_lanes(ssm_dt_bias[l]), _pad_lanes(ssm_a_log[l]),
                     jnp.repeat(ssm_d[l].astype(F32), HEAD_DIM).reshape(1, GROUP_W),
                     ssm_norm_g[l].reshape(1, GROUP_W).astype(F32), b, s)

        kvc = p[:, OFF_CMP_KV:OFF_CMP_KV + 2 * NSA_KV_W]
        r = kvc.reshape(b, n_rows, CMP_STRIDE, 2 * NSA_KV_HEADS, HEAD_DIM).transpose(0, 3, 1, 2, 4)
        r = r.reshape(b, 2 * NSA_KV_HEADS, n_rows, half)
        if n_rows < N_CMP_PAD:
            r = jnp.pad(r, ((0, 0), (0, 0), (0, N_CMP_PAD - n_rows), (0, 0)))
        w1 = nsa_cmp_w1[l].astype(BF16)
        w2dup = jnp.concatenate([nsa_cmp_w2[l], nsa_cmp_w2[l]], axis=-1).astype(BF16)
        pe8 = jnp.broadcast_to(nsa_cmp_pe[l].astype(F32).reshape(2, 1, 2 * half), (2, 8, 2 * half))
        kv_cmp = _compress(r, w1[:, :half], w1[:, half:], w2dup, pe8)
        o_cmp, notsel = _cmp_select(p, ps, kv_cmp, fcmp, overlap, b, s, n_cmp)
        o_sel = _sel_attn(tab_flat, p, ps, notsel, bsel, b, s)
        o_win = _win_attn(p, ps, bwin, b, s)

        w_kv = w_mem_kv[l].astype(BF16)
        mem_kv, _ = _norm_proj(mem2d, mem_norm_g[l], w_kv, w_kv[:, :LANES], tm=min(512, b * m_tok))
        o_mem = _mem_attn(p, mem_kv, b, s)

        x2d = _out_proj(x2d, (o_fox, o_ssd, o_cmp, o_sel, o_win, o_mem), w_out[l].astype(BF16),
                        final_norm_g, final=(l == depth - 1))
    return x2d.reshape(b, s, d)


def kernel(x, mem, norm_g, w_in, fox_f_bias, ssm_conv_w, ssm_conv_b, ssm_dt_bias, ssm_a_log, ssm_d, ssm_norm_g,
           nsa_cmp_pe, nsa_cmp_w1, nsa_cmp_w2, rel_bias_table, mem_norm_g, w_mem_kv, w_out, final_norm_g):
    return _trunk(x, mem, norm_g, w_in, fox_f_bias, ssm_conv_w, ssm_conv_b, ssm_dt_bias, ssm_a_log, ssm_d,
                  ssm_norm_g, nsa_cmp_pe, nsa_cmp_w1, nsa_cmp_w2, rel_bias_table, mem_norm_g, w_mem_kv, w_out,
                  final_norm_g)
```

```python
import functools
import math

import numpy as np
import jax
import jax.numpy as jnp
from jax import lax
from jax.experimental import pallas as pl
from jax.experimental.pallas import tpu as pltpu

F32 = jnp.float32
BF16 = jnp.bfloat16
HIGHEST = lax.Precision.HIGHEST

D_MODEL = 1024
GROUP_W = 512
HEAD_DIM = 64
EPS = 1e-6
NEG_INF = -1e30
TINY = 1e-30
LOG2E = math.log2(math.e)

FOX_HEADS = 8
SSM_HEADS = 8
SSM_STATE = 128
SSM_GROUPS = 2
SSM_CONV = 4
SSM_CHUNK = 128
SSM_CONV_DIM = GROUP_W + 2 * SSM_GROUPS * SSM_STATE

NSA_HEADS = 8
NSA_KV_HEADS = 2
NSA_REP = NSA_HEADS // NSA_KV_HEADS
NSA_KV_W = NSA_KV_HEADS * HEAD_DIM
CMP_BLOCK = 32
CMP_STRIDE = 16
CMP_HIDDEN = 2 * HEAD_DIM
SEL_BLOCK = 64
SEL_TOPK = 16
WINDOW = 512
SEL_FORCE = 1e9

MEM_HEADS = 4
MEM_HEAD_DIM = GROUP_W // MEM_HEADS
REL_BUCKETS = 32
REL_MAX_DIST = 128

FOX_COLS = 4 * GROUP_W + FOX_HEADS
SSM_COLS = GROUP_W + SSM_CONV_DIM + SSM_HEADS
NSA_COLS = 2 * GROUP_W + 6 * NSA_KV_W + 3 * NSA_HEADS
MEM_COLS = 2 * GROUP_W

LANES = 128
VMEM_LIMIT = 56 * 1024 * 1024

OFF_FOX_Q, OFF_FOX_K, OFF_FOX_V, OFF_FOX_G = 0, 512, 1024, 1536
OFF_SSM_XBC, OFF_SSM_Z = 2048, 3072
OFF_NSA_Q, OFF_NSA_G = 3584, 4096
OFF_MEM_Q, OFF_MEM_G = 4608, 5120
OFF_SEL_K, OFF_SEL_V, OFF_WIN_K, OFF_WIN_V, OFF_CMP_KV = 5632, 5888, 6144, 6400, 6656
P_COLS = 6912
PS_COLS = 3 * LANES

T_ATT = 256
N_CMP_PAD = 256


def _params(sem):
    return pltpu.CompilerParams(dimension_semantics=sem, vmem_limit_bytes=VMEM_LIMIT)


def _t5_bucket_np(dist):
    n = np.maximum(dist, 0)
    max_exact = REL_BUCKETS // 2
    nf = np.maximum(n, 1).astype(np.float32)
    large = max_exact + (np.log(nf / np.float32(max_exact)) / np.float32(math.log(REL_MAX_DIST / max_exact))
                         * np.float32(REL_BUCKETS - max_exact)).astype(np.int32)
    large = np.minimum(large, REL_BUCKETS - 1)
    return np.where(n < max_exact, n, large).astype(np.int32)


def _silu(x):
    return x * jax.nn.sigmoid(x)


def _norm_proj_kernel(x_ref, g_ref, w_ref, ws_ref, p_ref, ps_ref, *, chunk):
    x = x_ref[...]
    ms = jnp.mean(x * x, axis=-1, keepdims=True)
    h = ((x * lax.rsqrt(ms + EPS)) * g_ref[...]).astype(BF16)
    ncol = p_ref.shape[1]
    for c0 in range(0, ncol, chunk):
        c1 = min(c0 + chunk, ncol)
        p_ref[:, c0:c1] = jnp.dot(h, w_ref[:, c0:c1], preferred_element_type=F32).astype(p_ref.dtype)
    ps_ref[...] = jnp.dot(h, ws_ref[...], preferred_element_type=F32)


def _norm_proj(x2d, g, w_main, w_small, tm=512):
    n, d = x2d.shape
    pc, sc = w_main.shape[1], w_small.shape[1]
    return pl.pallas_call(
        functools.partial(_norm_proj_kernel, chunk=512),
        grid=(n // tm,),
        in_specs=[
            pl.BlockSpec((tm, d), lambda i: (i, 0)),
            pl.BlockSpec((1, d), lambda i: (0, 0)),
            pl.BlockSpec((d, pc), lambda i: (0, 0)),
            pl.BlockSpec((d, sc), lambda i: (0, 0)),
        ],
        out_specs=[
            pl.BlockSpec((tm, pc), lambda i: (i, 0)),
            pl.BlockSpec((tm, sc), lambda i: (i, 0)),
        ],
        out_shape=[jax.ShapeDtypeStruct((n, pc), BF16), jax.ShapeDtypeStruct((n, sc), F32)],
        compiler_params=_params(("parallel",)),
        name="norm_proj",
    )(x2d, g.reshape(1, d), w_main, w_small)


def _bias_table_kernel(tab_ref, idx_ref, o_ref):
    h = pl.program_id(0)
    idx = idx_ref[...]
    acc = jnp.full(idx.shape, NEG_INF, F32)
    for b in range(REL_BUCKETS):
        acc = jnp.where(idx == b, tab_ref[b * NSA_HEADS + h], acc)
    o_ref[0] = acc


def _bias_table(tab_flat, idx):
    r, c = idx.shape
    return pl.pallas_call(
        _bias_table_kernel,
        grid=(NSA_HEADS,),
        in_specs=[
            pl.BlockSpec(memory_space=pltpu.SMEM),
            pl.BlockSpec((r, c), lambda h: (0, 0)),
        ],
        out_specs=pl.BlockSpec((1, r, c), lambda h: (h, 0, 0)),
        out_shape=jax.ShapeDtypeStruct((NSA_HEADS, r, c), F32),
        compiler_params=_params(("arbitrary",)),
        name="t5_bias_table",
    )(tab_flat, jnp.asarray(idx))


def _bias_indices():
    t = T_ATT
    i = np.arange(t)[:, None]
    j = np.arange(WINDOW + t)[None, :]
    d = i + WINDOW - j
    idx_win = np.where((d >= 0) & (d < WINDOW), _t5_bucket_np(d), -1).astype(np.int32)
    j = np.arange(t)[None, :]
    d0 = i - j
    d1 = i - j + t
    idx_sel = np.concatenate([np.where(d0 >= 0, _t5_bucket_np(d0), -1), _t5_bucket_np(d1)], axis=1).astype(np.int32)
    w = np.arange(N_CMP_PAD)[None, :]
    dc = i - CMP_STRIDE * (w - t // CMP_STRIDE) - (CMP_BLOCK - 1)
    idx_cmp = np.where(w < 2 * (t // CMP_STRIDE), np.where(dc >= 0, _t5_bucket_np(dc), -1),
                       REL_BUCKETS - 1).astype(np.int32)
    return idx_win, idx_sel, idx_cmp


def _tri_lower(n):
    r = lax.broadcasted_iota(jnp.int32, (n, n), 0)
    c = lax.broadcasted_iota(jnp.int32, (n, n), 1)
    return (r >= c).astype(F32)


N_SPLIT = 3


def _fox_aug_lane(h, i):
    return LANES * (h // 2) + (HEAD_DIM if h % 2 == 0 else 0) + i


def _fox_aug_consts():
    pq = np.zeros((N_SPLIT * LANES, GROUP_W), np.float32)
    pk = np.zeros((N_SPLIT * LANES, GROUP_W), np.float32)
    oq = np.zeros((1, GROUP_W), np.float32)
    ok = np.zeros((1, GROUP_W), np.float32)
    for h in range(FOX_HEADS):
        for i in range(N_SPLIT):
            pq[i * LANES + h, _fox_aug_lane(h, i)] = 1.0
            pk[i * LANES + h, _fox_aug_lane(h, N_SPLIT + i)] = -1.0
            oq[0, _fox_aug_lane(h, N_SPLIT + i)] = 1.0
            ok[0, _fox_aug_lane(h, i)] = 1.0
    return pq, pk, oq, ok


def _fox_cumsum_kernel(f_ref, b_ref, pq_ref, pk_ref, oq_ref, ok_ref, qa_ref, ka_ref, carry_ref, *, ts):
    @pl.when(pl.program_id(1) == 0)
    def _():
        carry_ref[...] = jnp.zeros_like(carry_ref)

    z = f_ref[...] + b_ref[...]
    logf = (jnp.minimum(z, 0.0) - jnp.log(1.0 + jnp.exp(-jnp.abs(z)))) * LOG2E
    tri = _tri_lower(LANES)
    carry = carry_ref[...]
    for c in range(ts // LANES):
        rows = slice(c * LANES, (c + 1) * LANES)
        cs = jnp.dot(tri, logf[rows], precision=HIGHEST, preferred_element_type=F32) + carry
        carry = cs[LANES - 1:LANES, :]
        pieces, rest = [], cs
        for _ in range(N_SPLIT):
            piece = rest.astype(BF16)
            pieces.append(piece)
            rest = rest - piece.astype(F32)
        cat = jnp.concatenate(pieces, axis=1)
        qa_ref[rows, :] = (jnp.dot(cat, pq_ref[...], preferred_element_type=F32) + oq_ref[...]).astype(BF16)
        ka_ref[rows, :] = (jnp.dot(cat, pk_ref[...], preferred_element_type=F32) + ok_ref[...]).astype(BF16)
    carry_ref[...] = carry


def _fox_cumsum(ps, f_bias_pad, b, s, ts=512):
    ns = s // ts
    pq, pk, oq, ok = _fox_aug_consts()
    const = lambda bi, si: (0, 0)
    return pl.pallas_call(
        functools.partial(_fox_cumsum_kernel, ts=ts),
        grid=(b, ns),
        in_specs=[
            pl.BlockSpec((ts, LANES), lambda bi, si: (bi * ns + si, 0)),
            pl.BlockSpec((1, LANES), const),
            pl.BlockSpec(pq.shape, const),
            pl.BlockSpec(pk.shape, const),
            pl.BlockSpec(oq.shape, const),
            pl.BlockSpec(ok.shape, const),
        ],
        out_specs=[
            pl.BlockSpec((ts, GROUP_W), lambda bi, si: (bi * ns + si, 0)),
            pl.BlockSpec((ts, GROUP_W), lambda bi, si: (bi * ns + si, 0)),
        ],
        out_shape=[jax.ShapeDtypeStruct((b * s, GROUP_W), BF16), jax.ShapeDtypeStruct((b * s, GROUP_W), BF16)],
        scratch_shapes=[pltpu.VMEM((1, LANES), F32)],
        compiler_params=_params(("parallel", "arbitrary")),
        name="fox_cumsum",
    )(ps, f_bias_pad, jnp.asarray(pq, BF16), jnp.asarray(pk, BF16), jnp.asarray(oq), jnp.asarray(ok))


def _half_masks(rows):
    lo = jnp.where(lax.broadcasted_iota(jnp.int32, (rows, LANES), 1) < HEAD_DIM, 1.0, 0.0)
    return lo.astype(BF16), (1.0 - lo).astype(BF16)


def _fox_attn_kernel(q_ref, qa_ref, k_ref, ka_ref, v_ref, gate_ref, o_ref, qs_ref, kk_ref, vt_ref, m_ref, acc_ref,
                     *, t, tk, ahead):
    qi = pl.program_id(1)
    lo_q, hi_q = _half_masks(t)
    lo = lax.broadcasted_iota(jnp.int32, (t, LANES), 1) < HEAD_DIM
    cm = lax.broadcasted_iota(jnp.int32, (tk, t), 0) - lax.broadcasted_iota(jnp.int32, (tk, t), 1)

    @pl.when(qi == 0)
    def _():
        lo_k, hi_k = _half_masks(tk)

        def merge(j, c):
            rows = pl.ds(pl.multiple_of(j * tk, tk), tk)
            for hp in range(FOX_HEADS // 2):
                ls = slice(hp * LANES, (hp + 1) * LANES)
                kp, ka, vp = k_ref[rows, ls], ka_ref[rows, ls], v_ref[rows, ls]
                kk_ref[2 * hp, rows, :] = kp * lo_k + ka * hi_k
                kk_ref[2 * hp + 1, rows, :] = ka * lo_k + kp * hi_k
                vt_ref[2 * hp, :, rows] = (vp * lo_k + hi_k).astype(F32).T.astype(BF16)
                vt_ref[2 * hp + 1, :, rows] = (lo_k + vp * hi_k).astype(F32).T.astype(BF16)
            return c

        lax.fori_loop(0, k_ref.shape[0] // tk, merge, 0)

    for hp in range(FOX_HEADS // 2):
        ls = slice(hp * LANES, (hp + 1) * LANES)
        qp, qa = q_ref[:, ls], qa_ref[:, ls]
        qs_ref[2 * hp] = qp * lo_q + qa * hi_q
        qs_ref[2 * hp + 1] = qa * lo_q + qp * hi_q
    m_ref[...] = jnp.full(m_ref.shape, NEG_INF, F32)
    acc_ref[...] = jnp.zeros_like(acc_ref)

    def tile(j, diag):
        rows = pl.ds(pl.multiple_of(j * tk, tk), tk)

        def scores(h):
            return lax.dot_general(kk_ref[h, rows, :], qs_ref[h], (((1,), (1,)), ((), ())),
                                   preferred_element_type=F32)

        queue = [scores(h) for h in range(ahead)]
        for h in range(FOX_HEADS):
            if h + ahead < FOX_HEADS:
                queue.append(scores(h + ahead))
            s = queue.pop(0)
            if diag:
                s = jnp.where(cm <= qi * t - j * tk, s, NEG_INF)
            m_old = m_ref[h]
            m_new = jnp.maximum(m_old, jnp.max(s, axis=0, keepdims=True))
            m_ref[h] = m_new
            p = jnp.exp2(s - m_new).astype(BF16)
            acc_ref[h] = jnp.exp2(m_old - m_new) * acc_ref[h] + jnp.dot(vt_ref[h, :, rows], p,
                                                                        preferred_element_type=F32)

    jd = (qi * t) // tk
    for dj in range(max(t // tk, 1)):
        tile(jd + dj, True)

    def body(j, c):
        tile(j, False)
        return c

    lax.fori_loop(0, jd, body, 0)

    for hp in range(FOX_HEADS // 2):
        ls = slice(hp * LANES, (hp + 1) * LANES)
        a0, a1 = acc_ref[2 * hp].T, acc_ref[2 * hp + 1].T
        o = jnp.where(lo, a0 / jnp.maximum(a0[:, HEAD_DIM:HEAD_DIM + 1], TINY),
                      a1 / jnp.maximum(a1[:, 0:1], TINY))
        o_ref[:, ls] = (o * _silu(gate_ref[:, ls].astype(F32))).astype(o_ref.dtype)


def _fox_attn(p, qaug, kaug, b, s, t=512, tk=256, ahead=2):
    nq = s // t
    w = GROUP_W
    return pl.pallas_call(
        functools.partial(_fox_attn_kernel, t=t, tk=tk, ahead=ahead),
        grid=(b, nq),
        in_specs=[
            pl.BlockSpec((t, w), lambda bi, qi: (bi * nq + qi, OFF_FOX_Q // w)),
            pl.BlockSpec((t, w), lambda bi, qi: (bi * nq + qi, 0)),
            pl.BlockSpec((s, w), lambda bi, qi: (bi, OFF_FOX_K // w)),
            pl.BlockSpec((s, w), lambda bi, qi: (bi, 0)),
            pl.BlockSpec((s, w), lambda bi, qi: (bi, OFF_FOX_V // w)),
            pl.BlockSpec((t, w), lambda bi, qi: (bi * nq + qi, OFF_FOX_G // w)),
        ],
        out_specs=pl.BlockSpec((t, w), lambda bi, qi: (bi * nq + qi, 0)),
        out_shape=jax.ShapeDtypeStruct((b * s, w), BF16),
        scratch_shapes=[
            pltpu.VMEM((FOX_HEADS, t, LANES), BF16),
            pltpu.VMEM((FOX_HEADS, s, LANES), BF16),
            pltpu.VMEM((FOX_HEADS, LANES, s), BF16),
            pltpu.VMEM((FOX_HEADS, 1, t), F32),
            pltpu.VMEM((FOX_HEADS, LANES, t), F32),
        ],
        compiler_params=_params(("parallel", "arbitrary")),
        name="fox_attn",
    )(p, qaug, p, kaug, p, p)


def _ssd_kernel(z_ref, xbc_ref, dt_ref, cw_ref, cb_ref, dtb_ref, alog_ref, dsk_ref, ng_ref, o_ref,
                xpad_ref, state_ref, y_ref):
    q = SSM_CHUNK
    halo = 8

    @pl.when(pl.program_id(1) == 0)
    def _():
        xpad_ref[0:halo, :] = jnp.zeros((halo, SSM_CONV_DIM), F32)
        state_ref[...] = jnp.zeros_like(state_ref)

    xpad_ref[halo:halo + q, :] = xbc_ref[...].astype(F32)
    y = cb_ref[...]
    for k in range(SSM_CONV):
        off = halo - (SSM_CONV - 1) + k
        y = y + cw_ref[k:k + 1, :] * xpad_ref[off:off + q, :]
    xpad_ref[0:halo, :] = xpad_ref[q:q + halo, :]
    xc = _silu(y)
    xs = xc[:, :GROUP_W]

    x_dt = dt_ref[...] + dtb_ref[...]
    dt = jnp.maximum(x_dt, 0.0) + jnp.log(1.0 + jnp.exp(-jnp.abs(x_dt)))
    a = dt * (-jnp.exp(alog_ref[...]))
    tri = _tri_lower(q)
    acs = jnp.dot(tri, a, precision=HIGHEST, preferred_element_type=F32)
    acs_t = acs.T
    er = lax.broadcasted_iota(jnp.int32, (LANES, GROUP_W), 0)
    ec = lax.broadcasted_iota(jnp.int32, (LANES, GROUP_W), 1)
    expand = (ec // HEAD_DIM == er).astype(F32)
    dt_full = jnp.dot(dt, expand, precision=HIGHEST, preferred_element_type=F32)
    eacs_full = jnp.dot(jnp.exp(acs), expand, precision=HIGHEST, preferred_element_type=F32)
    dec_full = jnp.dot(jnp.exp(acs[q - 1:q, :] - acs), expand, precision=HIGHEST, preferred_element_type=F32)
    xdt = xs * dt_full
    xdt_b = xdt.astype(BF16)
    xdec_b = (xdt * dec_full).astype(BF16)

    row = lax.broadcasted_iota(jnp.int32, (q, q), 0)
    col = lax.broadcasted_iota(jnp.int32, (q, q), 1)
    causal = row >= col
    lo = lax.broadcasted_iota(jnp.int32, (q, LANES), 1) < HEAD_DIM
    gw = GROUP_W // SSM_GROUPS
    hpg = SSM_HEADS // SSM_GROUPS
    for g in range(SSM_GROUPS):
        bm = xc[:, GROUP_W + g * SSM_STATE:GROUP_W + (g + 1) * SSM_STATE]
        cm = xc[:, GROUP_W + (SSM_GROUPS + g) * SSM_STATE:GROUP_W + (SSM_GROUPS + g + 1) * SSM_STATE]
        bm_b = bm.astype(BF16)
        cm_b = cm.astype(BF16)
        gs = slice(g * gw, (g + 1) * gw)
        cbg = lax.dot_general(cm_b, bm_b, (((1,), (1,)), ((), ())), preferred_element_type=F32)
        st = state_ref[:, gs]
        y_off = jnp.dot(cm_b, st.astype(BF16), preferred_element_type=F32) * eacs_full[:, gs]
        cst = jnp.dot(bm.T.astype(BF16), xdec_b[:, gs], preferred_element_type=F32)
        state_ref[:, gs] = st * eacs_full[q - 1:q, gs] + cst
        for hp in range(hpg // 2):
            ls = slice(g * gw + hp * LANES, g * gw + (hp + 1) * LANES)
            yd = []
            for e in range(2):
                h = g * hpg + 2 * hp + e
                seg = jnp.exp(jnp.where(causal, acs[:, h:h + 1] - acs_t[h:h + 1, :], NEG_INF))
                yd.append(jnp.dot((cbg * seg).astype(BF16), xdt_b[:, ls], preferred_element_type=F32))
            y_ref[:, ls] = jnp.where(lo, yd[0], yd[1]) + y_off[:, hp * LANES:(hp + 1) * LANES]

    yz = (y_ref[...] + xs * dsk_ref[...]) * _silu(z_ref[...].astype(F32))
    for g in range(SSM_GROUPS):
        gs = slice(g * gw, (g + 1) * gw)
        blk = yz[:, gs]
        ms = jnp.mean(blk * blk, axis=-1, keepdims=True)
        o_ref[:, gs] = ((blk * lax.rsqrt(ms + EPS)) * ng_ref[:, gs]).astype(o_ref.dtype)


def _ssd(p, ps, conv_w, conv_b, dt_bias_pad, a_log_pad, d_full, norm_g, b, s):
    q = SSM_CHUNK
    nc = s // q
    row = lambda bi, ci: (bi * nc + ci)
    const = lambda bi, ci: (0, 0)
    return pl.pallas_call(
        _ssd_kernel,
        grid=(b, nc),
        in_specs=[
            pl.BlockSpec((q, GROUP_W), lambda bi, ci: (row(bi, ci), OFF_SSM_Z // GROUP_W)),
            pl.BlockSpec((q, SSM_CONV_DIM), lambda bi, ci: (row(bi, ci), OFF_SSM_XBC // SSM_CONV_DIM)),
            pl.BlockSpec((q, LANES), lambda bi, ci: (row(bi, ci), 1)),
            pl.BlockSpec((SSM_CONV, SSM_CONV_DIM), const),
            pl.BlockSpec((1, SSM_CONV_DIM), const),
            pl.BlockSpec((1, LANES), const),
            pl.BlockSpec((1, LANES), const),
            pl.BlockSpec((1, GROUP_W), const),
            pl.BlockSpec((1, GROUP_W), const),
        ],
        out_specs=pl.BlockSpec((q, GROUP_W), lambda bi, ci: (row(bi, ci), 0)),
        out_shape=jax.ShapeDtypeStruct((b * s, GROUP_W), BF16),
        scratch_shapes=[
            pltpu.VMEM((q + 8, SSM_CONV_DIM), F32),
            pltpu.VMEM((SSM_STATE, GROUP_W), F32),
            pltpu.VMEM((q, GROUP_W), F32),
        ],
        compiler_params=_params(("parallel", "arbitrary")),
        name="ssd",
    )(p, p, ps, conv_w, conv_b, dt_bias_pad, a_log_pad, d_full, norm_g)


def _compress_kernel(r_ref, w1a_ref, w1b_ref, w2_ref, pe_ref, o_ref):
    half = CMP_STRIDE * HEAD_DIM
    r = r_ref[0, 0]
    w1a = w1a_ref[0]
    w1b = w1b_ref[0]
    a = jnp.dot(r, w1a, preferred_element_type=F32)
    bm = jnp.dot(r, w1b, preferred_element_type=F32)
    pe = pe_ref[0].astype(BF16)
    const = (jnp.dot(pe[:, :half], w1a, preferred_element_type=F32)
             + jnp.dot(pe[:, half:], w1b, preferred_element_type=F32))[0:1]
    n = bm.shape[0]
    h = _silu(a + pltpu.roll(bm, n - 1, 0) + const)
    o_ref[0, 0] = jnp.dot(h.astype(BF16), w2_ref[0], preferred_element_type=F32).astype(o_ref.dtype)


def _compress(r, w1a, w1b, w2dup, pe8):
    b, nslot, nr, width = r.shape
    return pl.pallas_call(
        _compress_kernel,
        grid=(b, nslot),
        in_specs=[
            pl.BlockSpec((1, 1, nr, width), lambda bi, si: (bi, si, 0, 0)),
            pl.BlockSpec((1, width, CMP_HIDDEN), lambda bi, si: (si // NSA_KV_HEADS, 0, 0)),
            pl.BlockSpec((1, width, CMP_HIDDEN), lambda bi, si: (si // NSA_KV_HEADS, 0, 0)),
            pl.BlockSpec((1, CMP_HIDDEN, LANES), lambda bi, si: (si // NSA_KV_HEADS, 0, 0)),
            pl.BlockSpec((1, 8, 2 * width), lambda bi, si: (si // NSA_KV_HEADS, 0, 0)),
        ],
        out_specs=pl.BlockSpec((1, 1, nr, LANES), lambda bi, si: (bi, si, 0, 0)),
        out_shape=jax.ShapeDtypeStruct((b, nslot, nr, LANES), BF16),
        compiler_params=_params(("parallel", "arbitrary")),
        name="nsa_compress",
    )(r, w1a, w1b, w2dup, pe8)


def _cmp_select_kernel(q_ref, kv_ref, gl_ref, gate_ref, fc_ref, ov_ref, o_ref, ns_ref, *, n_cmp):
    t = T_ATT
    ncp = kv_ref.shape[2]
    qi = pl.program_id(1)
    t0 = qi * t
    lo = lax.broadcasted_iota(jnp.int32, (t, LANES), 1) < HEAD_DIM
    zero = jnp.zeros((t, LANES), BF16)
    row = lax.broadcasted_iota(jnp.int32, (t, ncp), 0)
    col = lax.broadcasted_iota(jnp.int32, (t, ncp), 1)
    valid = (col * CMP_STRIDE + (CMP_BLOCK - 1) <= t0 + row) & (col < n_cmp)
    per_tile = t // CMP_STRIDE
    shift = lax.rem(qi * per_tile + (ncp - per_tile), ncp)
    beyond = col >= (qi + 1) * per_tile
    gl = gl_ref[...]

    nsel = SEL_BLOCK
    jrow = lax.broadcasted_iota(jnp.int32, (nsel, t), 0)
    cur = (t0 + lax.broadcasted_iota(jnp.int32, (nsel, t), 1)) // SEL_BLOCK
    forced = (jrow == 0) | (jrow == cur) | (jrow == cur - 1)
    past = jrow <= cur

    for g in range(NSA_KV_HEADS):
        kc = kv_ref[0, g]
        vc = kv_ref[0, NSA_KV_HEADS + g]
        psum = jnp.zeros((t, ncp), F32)
        outs = []
        for r in range(NSA_REP):
            h = g * NSA_REP + r
            qp = q_ref[:, (h // 2) * LANES:(h // 2 + 1) * LANES]
            qe = jnp.where(lo, qp, zero) if r % 2 == 0 else jnp.where(lo, zero, qp)
            s = lax.dot_general(qe, kc, (((1,), (1,)), ((), ())), preferred_element_type=F32)
            bias = jnp.where(beyond, NEG_INF, pltpu.roll(fc_ref[h], shift, 1))
            s = jnp.where(valid, s + bias, NEG_INF)
            m = jnp.max(s, axis=1, keepdims=True)
            e = jnp.where(valid, jnp.exp(s - m), 0.0)
            pr = e / jnp.maximum(jnp.sum(e, axis=1, keepdims=True), TINY)
            psum = psum + pr
            oc = jnp.dot(pr.astype(BF16), vc, preferred_element_type=F32)
            gc = jax.nn.sigmoid(gl[:, h:h + 1])
            outs.append(oc * gc)
        for hp in range(NSA_REP // 2):
            ls = slice((g * NSA_REP // 2 + hp) * LANES, (g * NSA_REP // 2 + hp + 1) * LANES)
            o = jnp.where(lo, outs[2 * hp], outs[2 * hp + 1])
            o_ref[:, ls] = (o * _silu(gate_ref[:, ls].astype(F32))).astype(o_ref.dtype)

        imp = jnp.dot(psum, ov_ref[...], precision=HIGHEST, preferred_element_type=F32)
        imp_t = imp.T[0:nsel, :]
        imp_t = jnp.where(past, jnp.where(forced, SEL_FORCE, imp_t), -SEL_FORCE)
        rank = jnp.zeros((nsel, t), F32)
        for i in range(nsel):
            bi = imp_t[i:i + 1, :]
            gt = jnp.where(bi > imp_t, 1.0, 0.0)
            ge = jnp.where(bi >= imp_t, 1.0, 0.0)
            rank = rank + jnp.where(jrow > i, ge, gt)
        notsel = jnp.where((rank < float(SEL_TOPK)) & past, 0.0, 1.0)
        ns2 = jnp.concatenate([notsel, notsel], axis=0).T
        ns_ref[:, g * LANES:(g + 1) * LANES] = ns2.astype(ns_ref.dtype)


def _cmp_select(p, ps, kvc, fc, overlap, b, s, n_cmp):
    t = T_ATT
    nq = s // t
    w = GROUP_W
    ncp = kvc.shape[2]
    return pl.pallas_call(
        functools.partial(_cmp_select_kernel, n_cmp=n_cmp),
        grid=(b, nq),
        in_specs=[
            pl.BlockSpec((t, w), lambda bi, qi: (bi * nq + qi, OFF_NSA_Q // w)),
            pl.BlockSpec((1, 2 * NSA_KV_HEADS, ncp, LANES), lambda bi, qi: (bi, 0, 0, 0)),
            pl.BlockSpec((t, LANES), lambda bi, qi: (bi * nq + qi, 2)),
            pl.BlockSpec((t, w), lambda bi, qi: (bi * nq + qi, OFF_NSA_G // w)),
            pl.BlockSpec((NSA_HEADS, t, ncp), lambda bi, qi: (0, 0, 0)),
            pl.BlockSpec((ncp, LANES), lambda bi, qi: (0, 0)),
        ],
        out_specs=[
            pl.BlockSpec((t, w), lambda bi, qi: (bi * nq + qi, 0)),
            pl.BlockSpec((t, NSA_KV_HEADS * LANES), lambda bi, qi: (bi * nq + qi, 0)),
        ],
        out_shape=[jax.ShapeDtypeStruct((b * s, w), BF16),
                   jax.ShapeDtypeStruct((b * s, NSA_KV_HEADS * LANES), BF16)],
        compiler_params=_params(("parallel", "arbitrary")),
        name="nsa_cmp_select",
    )(p, kvc, ps, p, fc, overlap)


def _sel_attn_kernel(tab_ref, q_ref, ns_ref, k_ref, v_ref, gl_ref, gate_ref, bs_ref, o_ref, acc_ref):
    t = T_ATT
    qi = pl.program_id(1)
    lane = lax.broadcasted_iota(jnp.int32, (t, LANES), 1)
    lo = lane < HEAD_DIM
    krow = lax.broadcasted_iota(jnp.int32, (t, LANES), 0)
    gl = gl_ref[...]
    prev_pen = jnp.where(qi >= 1, 0.0, NEG_INF)
    lo_f = jnp.where(lo, 1.0, 0.0)
    hi_f = 1.0 - lo_f
    lo_b = lo_f.astype(BF16)
    hi_b = hi_f.astype(BF16)

    for g in range(NSA_KV_HEADS):
        gls = slice(g * LANES, (g + 1) * LANES)
        ns = ns_ref[:, gls]
        qa = []
        for r in range(NSA_REP):
            h = g * NSA_REP + r
            qp = q_ref[:, (h // 2) * LANES:(h // 2 + 1) * LANES]
            qa.append(qp * lo_b + ns * hi_b if r % 2 == 0 else ns * lo_b + qp * hi_b)

        def tile(ki, carry, kind):
            ks = pl.multiple_of(ki * t, t)
            kd = k_ref[pl.ds(ks, t), gls]
            vd = v_ref[pl.ds(ks, t), gls]
            blk = (ks + krow) // SEL_BLOCK
            hot = jnp.where((lane % HEAD_DIM) == blk, NEG_INF, 0.0)
            k_even = kd * lo_b + (hot * hi_f).astype(BF16)
            k_odd = (hot * lo_f).astype(BF16) + kd * hi_b
            new, pvs, alphas = [], [], []
            for r in range(NSA_REP):
                h = g * NSA_REP + r
                m, l = carry[2 * r], carry[2 * r + 1]
                s = lax.dot_general(qa[r], k_even if r % 2 == 0 else k_odd, (((1,), (1,)), ((), ())),
                                    preferred_element_type=F32)
                if kind == "diag":
                    s = s + bs_ref[h, :, 0:t]
                elif kind == "prev":
                    s = s + (bs_ref[h, :, t:2 * t] + prev_pen)
                else:
                    s = s + tab_ref[(REL_BUCKETS - 1) * NSA_HEADS + h]
                m_new = jnp.maximum(m, jnp.max(s, axis=1, keepdims=True))
                p = jnp.exp(s - m_new)
                alpha = jnp.exp(m - m_new)
                new += [m_new, alpha * l + jnp.sum(p, axis=1, keepdims=True)]
                pvs.append(jnp.dot(p.astype(BF16), vd, preferred_element_type=F32))
                alphas.append(alpha)
            for hp in range(NSA_REP // 2):
                ls = slice(hp * LANES, (hp + 1) * LANES)
                acc_ref[:, ls] = (acc_ref[:, ls] * jnp.where(lo, alphas[2 * hp], alphas[2 * hp + 1])
                                  + jnp.where(lo, pvs[2 * hp], pvs[2 * hp + 1]))
            return tuple(new)

        acc_ref[...] = jnp.zeros_like(acc_ref)
        m0 = jnp.full((t, 1), NEG_INF, F32)
        l0 = jnp.zeros((t, 1), F32)
        carry = tile(qi, (m0, l0) * NSA_REP, "diag")
        carry = tile(jnp.maximum(qi - 1, 0), carry, "prev")
        carry = lax.fori_loop(0, jnp.maximum(qi - 1, 0), functools.partial(tile, kind="far"), carry)
        for hp in range(NSA_REP // 2):
            h0 = g * NSA_REP + 2 * hp
            ls = slice(hp * LANES, (hp + 1) * LANES)
            ols = slice((g * NSA_REP // 2 + hp) * LANES, (g * NSA_REP // 2 + hp + 1) * LANES)
            l = jnp.where(lo, carry[4 * hp + 1], carry[4 * hp + 3])
            gs = jnp.where(lo, jax.nn.sigmoid(gl[:, NSA_HEADS + h0:NSA_HEADS + h0 + 1]),
                           jax.nn.sigmoid(gl[:, NSA_HEADS + h0 + 1:NSA_HEADS + h0 + 2]))
            o = acc_ref[:, ls] / jnp.maximum(l, TINY) * gs
            o_ref[:, ols] = (o * _silu(gate_ref[:, ols].astype(F32))).astype(o_ref.dtype)


def _sel_attn(tab_flat, p, ps, notsel, bsel, b, s):
    t = T_ATT
    nq = s // t
    w = GROUP_W
    kw = NSA_KV_HEADS * LANES
    return pl.pallas_call(
        _sel_attn_kernel,
        grid=(b, nq),
        in_specs=[
            pl.BlockSpec(memory_space=pltpu.SMEM),
            pl.BlockSpec((t, w), lambda bi, qi: (bi * nq + qi, OFF_NSA_Q // w)),
            pl.BlockSpec((t, kw), lambda bi, qi: (bi * nq + qi, 0)),
            pl.BlockSpec((s, kw), lambda bi, qi: (bi, OFF_SEL_K // kw)),
            pl.BlockSpec((s, kw), lambda bi, qi: (bi, OFF_SEL_V // kw)),
            pl.BlockSpec((t, LANES), lambda bi, qi: (bi * nq + qi, 2)),
            pl.BlockSpec((t, w), lambda bi, qi: (bi * nq + qi, OFF_NSA_G // w)),
            pl.BlockSpec((NSA_HEADS, t, 2 * t), lambda bi, qi: (0, 0, 0)),
        ],
        out_specs=pl.BlockSpec((t, w), lambda bi, qi: (bi * nq + qi, 0)),
        out_shape=jax.ShapeDtypeStruct((b * s, w), BF16),
        scratch_shapes=[pltpu.VMEM((t, NSA_REP // 2 * LANES), F32)],
        compiler_params=_params(("parallel", "arbitrary")),
        name="nsa_sel_attn",
    )(tab_flat, p, notsel, p, p, ps, p, bsel)


def _win_attn_kernel(q_ref, k_ref, v_ref, gl_ref, gate_ref, bw_ref, o_ref):
    t = T_ATT
    nt = WINDOW // t + 1
    qi = pl.program_id(1)
    lo = lax.broadcasted_iota(jnp.int32, (t, LANES), 1) < HEAD_DIM
    zero = jnp.zeros((t, LANES), BF16)
    gl = gl_ref[...]

    for g in range(NSA_KV_HEADS):
        gls = slice(g * LANES, (g + 1) * LANES)
        kts, vts, pens = [], [], []
        for j in range(nt):
            kt = qi - (nt - 1) + j
            ks = pl.multiple_of(jnp.maximum(kt, 0) * t, t)
            kts.append(k_ref[pl.ds(ks, t), gls])
            vts.append(v_ref[pl.ds(ks, t), gls])
            pens.append(jnp.where(kt >= 0, 0.0, NEG_INF))
        outs = []
        for r in range(NSA_REP):
            h = g * NSA_REP + r
            qp = q_ref[:, (h // 2) * LANES:(h // 2 + 1) * LANES]
            qe = jnp.where(lo, qp, zero) if r % 2 == 0 else jnp.where(lo, zero, qp)
            ss = []
            for j in range(nt):
                s = lax.dot_general(qe, kts[j], (((1,), (1,)), ((), ())), preferred_element_type=F32)
                ss.append(s + (bw_ref[h, :, j * t:(j + 1) * t] + pens[j]))
            m = functools.reduce(jnp.maximum, [jnp.max(s, axis=1, keepdims=True) for s in ss])
            ps = [jnp.exp(s - m) for s in ss]
            l = functools.reduce(lambda a, c: a + c, [jnp.sum(p, axis=1, keepdims=True) for p in ps])
            o = functools.reduce(lambda a, c: a + c,
                                 [jnp.dot(p.astype(BF16), v, preferred_element_type=F32) for p, v in zip(ps, vts)])
            gw = jax.nn.sigmoid(gl[:, 2 * NSA_HEADS + h:2 * NSA_HEADS + h + 1])
            outs.append(o / jnp.maximum(l, TINY) * gw)
        for hp in range(NSA_REP // 2):
            ols = slice((g * NSA_REP // 2 + hp) * LANES, (g * NSA_REP // 2 + hp + 1) * LANES)
            o = jnp.where(lo, outs[2 * hp], outs[2 * hp + 1])
            o_ref[:, ols] = (o * _silu(gate_ref[:, ols].astype(F32))).astype(o_ref.dtype)


def _win_attn(p, ps, bwin, b, s):
    t = T_ATT
    nq = s // t
    w = GROUP_W
    kw = NSA_KV_HEADS * LANES
    return pl.pallas_call(
        _win_attn_kernel,
        grid=(b, nq),
        in_specs=[
            pl.BlockSpec((t, w), lambda bi, qi: (bi * nq + qi, OFF_NSA_Q // w)),
            pl.BlockSpec((s, kw), lambda bi, qi: (bi, OFF_WIN_K // kw)),
            pl.BlockSpec((s, kw), lambda bi, qi: (bi, OFF_WIN_V // kw)),
            pl.BlockSpec((t, LANES), lambda bi, qi: (bi * nq + qi, 2)),
            pl.BlockSpec((t, w), lambda bi, qi: (bi * nq + qi, OFF_NSA_G // w)),
            pl.BlockSpec((NSA_HEADS, t, WINDOW + t), lambda bi, qi: (0, 0, 0)),
        ],
        out_specs=pl.BlockSpec((t, w), lambda bi, qi: (bi * nq + qi, 0)),
        out_shape=jax.ShapeDtypeStruct((b * s, w), BF16),
        compiler_params=_params(("parallel", "arbitrary")),
        name="nsa_win_attn",
    )(p, p, p, ps, p, bwin)


def _mem_attn_kernel(q_ref, gate_ref, kv_ref, o_ref):
    scale = MEM_HEAD_DIM ** -0.5
    for h in range(MEM_HEADS):
        ls = slice(h * LANES, (h + 1) * LANES)
        k = kv_ref[:, ls]
        v = kv_ref[:, GROUP_W + h * LANES:GROUP_W + (h + 1) * LANES]
        s = lax.dot_general(q_ref[:, ls], k, (((1,), (1,)), ((), ())), preferred_element_type=F32) * scale
        m = jnp.max(s, axis=1, keepdims=True)
        e = jnp.exp(s - m)
        l = jnp.sum(e, axis=1, keepdims=True)
        o = jnp.dot(e.astype(BF16), v, preferred_element_type=F32) / l
        o_ref[:, ls] = (o * _silu(gate_ref[:, ls].astype(F32))).astype(o_ref.dtype)


def _mem_attn(p, mem_kv, b, s, t=512):
    nq = s // t
    w = GROUP_W
    m = mem_kv.shape[0] // b
    return pl.pallas_call(
        _mem_attn_kernel,
        grid=(b, nq),
        in_specs=[
            pl.BlockSpec((t, w), lambda bi, qi: (bi * nq + qi, OFF_MEM_Q // w)),
            pl.BlockSpec((t, w), lambda bi, qi: (bi * nq + qi, OFF_MEM_G // w)),
            pl.BlockSpec((m, 2 * w), lambda bi, qi: (bi, 0)),
        ],
        out_specs=pl.BlockSpec((t, w), lambda bi, qi: (bi * nq + qi, 0)),
        out_shape=jax.ShapeDtypeStruct((b * s, w), BF16),
        compiler_params=_params(("parallel", "arbitrary")),
        name="mem_attn",
    )(p, p, mem_kv)


def _out_proj_kernel(x_ref, of_ref, os_ref, oc_ref, osel_ref, ow_ref, om_ref, w_ref, g_ref, o_ref, *, final):
    w = GROUP_W
    nsa = (oc_ref[...].astype(F32) + osel_ref[...].astype(F32) + ow_ref[...].astype(F32)).astype(BF16)
    acc = x_ref[...]
    for i, part in enumerate((of_ref[...], os_ref[...], nsa, om_ref[...])):
        acc = acc + jnp.dot(part, w_ref[i * w:(i + 1) * w, :], preferred_element_type=F32)
    if final:
        ms = jnp.mean(acc * acc, axis=-1, keepdims=True)
        acc = (acc * lax.rsqrt(ms + EPS)) * g_ref[...]
    o_ref[...] = acc


def _out_proj(x2d, parts, w_out, g, final, tm=512):
    n, d = x2d.shape
    w = GROUP_W
    part_spec = pl.BlockSpec((tm, w), lambda i: (i, 0))
    return pl.pallas_call(
        functools.partial(_out_proj_kernel, final=final),
        grid=(n // tm,),
        in_specs=[pl.BlockSpec((tm, d), lambda i: (i, 0))] + [part_spec] * 6 + [
            pl.BlockSpec((4 * w, d), lambda i: (0, 0)),
            pl.BlockSpec((1, d), lambda i: (0, 0)),
        ],
        out_specs=pl.BlockSpec((tm, d), lambda i: (i, 0)),
        out_shape=jax.ShapeDtypeStruct((n, d), F32),
        compiler_params=_params(("parallel",)),
        name="out_proj",
    )(x2d, *parts, w_out, g.reshape(1, d))


def _pack_indices():
    fox, ssm, nsa, mem = 0, FOX_COLS, FOX_COLS + SSM_COLS, FOX_COLS + SSM_COLS + NSA_COLS
    w = GROUP_W
    rng = lambda a, n: list(range(a, a + n))
    kv = lambda slot: nsa + w + slot * NSA_KV_W
    dup = lambda base: (rng(base, HEAD_DIM) * 2) + (rng(base + HEAD_DIM, HEAD_DIM) * 2)
    main = (rng(fox, 4 * w)
            + rng(ssm + w, SSM_CONV_DIM) + rng(ssm, w)
            + rng(nsa, w) + rng(nsa + w + 6 * NSA_KV_W + 3 * NSA_HEADS, w)
            + rng(mem, 2 * w)
            + dup(kv(2)) + dup(kv(3)) + dup(kv(4)) + dup(kv(5))
            + rng(kv(0), NSA_KV_W) + rng(kv(1), NSA_KV_W))
    assert len(main) == P_COLS
    scale = np.ones((P_COLS,), np.float32)
    scale[OFF_FOX_Q:OFF_FOX_Q + w] = HEAD_DIM ** -0.5 * LOG2E
    scale[OFF_NSA_Q:OFF_NSA_Q + w] = HEAD_DIM ** -0.5
    small = np.zeros((PS_COLS,), np.int32)
    keep = np.zeros((PS_COLS,), np.float32)
    for blk, (src, n) in enumerate(((fox + 4 * w, FOX_HEADS), (ssm + w + SSM_CONV_DIM, SSM_HEADS),
                                    (nsa + w + 6 * NSA_KV_W, 3 * NSA_HEADS))):
        small[blk * LANES:blk * LANES + n] = np.arange(src, src + n)
        keep[blk * LANES:blk * LANES + n] = 1.0
    return np.asarray(main, np.int32), scale, small, keep


def _pad_lanes(v):
    return jnp.pad(v.astype(F32), (0, LANES - v.shape[0])).reshape(1, LANES)


def _trunk(x, mem, norm_g, w_in, fox_f_bias, ssm_conv_w, ssm_conv_b, ssm_dt_bias, ssm_a_log, ssm_d,
           ssm_norm_g, nsa_cmp_pe, nsa_cmp_w1, nsa_cmp_w2, rel_bias_table, mem_norm_g, w_mem_kv, w_out,
           final_norm_g):
    b, s, d = x.shape
    depth = w_in.shape[0]
    n = b * s
    m_tok = mem.shape[1]
    n_cmp = (s - CMP_BLOCK) // CMP_STRIDE + 1
    n_rows = s // CMP_STRIDE
    assert s % 512 == 0 and s // SEL_BLOCK <= HEAD_DIM and n_rows <= N_CMP_PAD and d == D_MODEL

    main_idx, main_scale, small_idx, small_keep = _pack_indices()
    tab_flat = rel_bias_table.astype(F32).reshape(-1)
    idx_win, idx_sel, idx_cmp = _bias_indices()
    bwin = _bias_table(tab_flat, idx_win)
    bsel = _bias_table(tab_flat, idx_sel)
    fcmp = _bias_table(tab_flat, idx_cmp)

    cs = np.arange(N_CMP_PAD)[:, None] * CMP_STRIDE
    js = np.arange(LANES)[None, :] * SEL_BLOCK
    overlap = ((cs < js + SEL_BLOCK) & (cs + CMP_BLOCK > js) & (np.arange(N_CMP_PAD)[:, None] < n_cmp)
               & (np.arange(LANES)[None, :] < s // SEL_BLOCK)).astype(np.float32)
    overlap = jnp.asarray(overlap)

    x2d = x.reshape(n, d)
    mem2d = mem.reshape(b * m_tok, d)
    half = CMP_STRIDE * HEAD_DIM
    for l in range(depth):
        w_main = (w_in[l][:, main_idx] * main_scale).astype(BF16)
        w_small = (w_in[l][:, small_idx] * small_keep).astype(BF16)
        p, ps = _norm_proj(x2d, norm_g[l], w_main, w_small)

        qaug, kaug = _fox_cumsum(ps, _pad_lanes(fox_f_bias[l]), b, s)
        o_fox = _fox_attn(p, qaug, kaug, b, s)

        o_ssd = _ssd(p, ps, ssm_conv_w[l].astype(F32), ssm_conv_b[l].reshape(1, -1).astype(F32),
                     _pad_lanes(ssm_dt_bias[l]), _pad_lanes(ssm_a_log[l]),
                     jnp.repeat(ssm_d[l].astype(F32), HEAD_DIM).reshape(1, GROUP_W),
                     ssm_norm_g[l].reshape(1, GROUP_W).astype(F32), b, s)

        kvc = p[:, OFF_CMP_KV:OFF_CMP_KV + 2 * NSA_KV_W]
        r = kvc.reshape(b, n_rows, CMP_STRIDE, 2 * NSA_KV_HEADS, HEAD_DIM).transpose(0, 3, 1, 2, 4)
        r = r.reshape(b, 2 * NSA_KV_HEADS, n_rows, half)
        if n_rows < N_CMP_PAD:
            r = jnp.pad(r, ((0, 0), (0, 0), (0, N_CMP_PAD - n_rows), (0, 0)))
        w1 = nsa_cmp_w1[l].astype(BF16)
        w2dup = jnp.concatenate([nsa_cmp_w2[l], nsa_cmp_w2[l]], axis=-1).astype(BF16)
        pe8 = jnp.broadcast_to(nsa_cmp_pe[l].astype(F32).reshape(2, 1, 2 * half), (2, 8, 2 * half))
        kv_cmp = _compress(r, w1[:, :half], w1[:, half:], w2dup, pe8)
        o_cmp, notsel = _cmp_select(p, ps, kv_cmp, fcmp, overlap, b, s, n_cmp)
        o_sel = _sel_attn(tab_flat, p, ps, notsel, bsel, b, s)
        o_win = _win_attn(p, ps, bwin, b, s)

        w_kv = w_mem_kv[l].astype(BF16)
        mem_kv, _ = _norm_proj(mem2d, mem_norm_g[l], w_kv, w_kv[:, :LANES], tm=min(512, b * m_tok))
        o_mem = _mem_attn(p, mem_kv, b, s)

        x2d = _out_proj(x2d, (o_fox, o_ssd, o_cmp, o_sel, o_win, o_mem), w_out[l].astype(BF16),
                        final_norm_g, final=(l == depth - 1))
    return x2d.reshape(b, s, d)


def kernel(x, mem, norm_g, w_in, fox_f_bias, ssm_conv_w, ssm_conv_b, ssm_dt_bias, ssm_a_log, ssm_d, ssm_norm_g,
           nsa_cmp_pe, nsa_cmp_w1, nsa_cmp_w2, rel_bias_table, mem_norm_g, w_mem_kv, w_out, final_norm_g):
    return _trunk(x, mem, norm_g, w_in, fox_f_bias, ssm_conv_w, ssm_conv_b, ssm_dt_bias, ssm_a_log, ssm_d,
                  ssm_norm_g, nsa_cmp_pe, nsa_cmp_w1, nsa_cmp_w2, rel_bias_table, mem_norm_g, w_mem_kv, w_out,
                  final_norm_g)
```

```python
import functools
import math

import numpy as np
import jax
import jax.numpy as jnp
from jax import lax
from jax.experimental import pallas as pl
from jax.experimental.pallas import tpu as pltpu

F32 = jnp.float32
BF16 = jnp.bfloat16
HIGHEST = lax.Precision.HIGHEST

D_MODEL = 1024
GROUP_W = 512
HEAD_DIM = 64
EPS = 1e-6
NEG_INF = -1e30
TINY = 1e-30
LOG2E = math.log2(math.e)

FOX_HEADS = 8
SSM_HEADS = 8
SSM_STATE = 128
SSM_GROUPS = 2
SSM_CONV = 4
SSM_CHUNK = 128
SSM_CONV_DIM = GROUP_W + 2 * SSM_GROUPS * SSM_STATE

NSA_HEADS = 8
NSA_KV_HEADS = 2
NSA_REP = NSA_HEADS // NSA_KV_HEADS
NSA_KV_W = NSA_KV_HEADS * HEAD_DIM
CMP_BLOCK = 32
CMP_STRIDE = 16
CMP_HIDDEN = 2 * HEAD_DIM
SEL_BLOCK = 64
SEL_TOPK = 16
WINDOW = 512
SEL_FORCE = 1e9

MEM_HEADS = 4
MEM_HEAD_DIM = GROUP_W // MEM_HEADS
REL_BUCKETS = 32
REL_MAX_DIST = 128

FOX_COLS = 4 * GROUP_W + FOX_HEADS
SSM_COLS = GROUP_W + SSM_CONV_DIM + SSM_HEADS
NSA_COLS = 2 * GROUP_W + 6 * NSA_KV_W + 3 * NSA_HEADS
MEM_COLS = 2 * GROUP_W

LANES = 128
VMEM_LIMIT = 56 * 1024 * 1024

OFF_FOX_Q, OFF_FOX_K, OFF_FOX_V, OFF_FOX_G = 0, 512, 1024, 1536
OFF_SSM_XBC, OFF_SSM_Z = 2048, 3072
OFF_NSA_Q, OFF_NSA_G = 3584, 4096
OFF_MEM_Q, OFF_MEM_G = 4608, 5120
OFF_SEL_K, OFF_SEL_V, OFF_WIN_K, OFF_WIN_V, OFF_CMP_KV = 5632, 5888, 6144, 6400, 6656
P_COLS = 6912
PS_COLS = 3 * LANES

T_ATT = 256
SEL_T, SEL_TK = 512, 256
N_CMP_PAD = 256


def _params(sem):
    return pltpu.CompilerParams(dimension_semantics=sem, vmem_limit_bytes=VMEM_LIMIT)


def _t5_bucket_np(dist):
    n = np.maximum(dist, 0)
    max_exact = REL_BUCKETS // 2
    nf = np.maximum(n, 1).astype(np.float32)
    large = max_exact + (np.log(nf / np.float32(max_exact)) / np.float32(math.log(REL_MAX_DIST / max_exact))
                         * np.float32(REL_BUCKETS - max_exact)).astype(np.int32)
    large = np.minimum(large, REL_BUCKETS - 1)
    return np.where(n < max_exact, n, large).astype(np.int32)


def _silu(x):
    return x * jax.nn.sigmoid(x)


def _norm_proj_kernel(x_ref, g_ref, w_ref, ws_ref, p_ref, ps_ref, *, chunk):
    x = x_ref[...]
    ms = jnp.mean(x * x, axis=-1, keepdims=True)
    h = ((x * lax.rsqrt(ms + EPS)) * g_ref[...]).astype(BF16)
    ncol = p_ref.shape[1]
    for c0 in range(0, ncol, chunk):
        c1 = min(c0 + chunk, ncol)
        p_ref[:, c0:c1] = jnp.dot(h, w_ref[:, c0:c1], preferred_element_type=F32).astype(p_ref.dtype)
    ps_ref[...] = jnp.dot(h, ws_ref[...], preferred_element_type=F32)


def _norm_proj(x2d, g, w_main, w_small, tm=512):
    n, d = x2d.shape
    pc, sc = w_main.shape[1], w_small.shape[1]
    return pl.pallas_call(
        functools.partial(_norm_proj_kernel, chunk=512),
        grid=(n // tm,),
        in_specs=[
            pl.BlockSpec((tm, d), lambda i: (i, 0)),
            pl.BlockSpec((1, d), lambda i: (0, 0)),
            pl.BlockSpec((d, pc), lambda i: (0, 0)),
            pl.BlockSpec((d, sc), lambda i: (0, 0)),
        ],
        out_specs=[
            pl.BlockSpec((tm, pc), lambda i: (i, 0)),
            pl.BlockSpec((tm, sc), lambda i: (i, 0)),
        ],
        out_shape=[jax.ShapeDtypeStruct((n, pc), BF16), jax.ShapeDtypeStruct((n, sc), F32)],
        compiler_params=_params(("parallel",)),
        name="norm_proj",
    )(x2d, g.reshape(1, d), w_main, w_small)


def _bias_table_kernel(tab_ref, idx_ref, o_ref):
    h = pl.program_id(0)
    idx = idx_ref[...]
    acc = jnp.full(idx.shape, NEG_INF, F32)
    for b in range(REL_BUCKETS):
        acc = jnp.where(idx == b, tab_ref[b * NSA_HEADS + h] * LOG2E, acc)
    o_ref[0] = acc


def _bias_table(tab_flat, idx):
    r, c = idx.shape
    return pl.pallas_call(
        _bias_table_kernel,
        grid=(NSA_HEADS,),
        in_specs=[
            pl.BlockSpec(memory_space=pltpu.SMEM),
            pl.BlockSpec((r, c), lambda h: (0, 0)),
        ],
        out_specs=pl.BlockSpec((1, r, c), lambda h: (h, 0, 0)),
        out_shape=jax.ShapeDtypeStruct((NSA_HEADS, r, c), F32),
        compiler_params=_params(("arbitrary",)),
        name="t5_bias_table",
    )(tab_flat, jnp.asarray(idx))


def _bias_indices():
    t = T_ATT
    i = np.arange(t)[:, None]
    j = np.arange(WINDOW + t)[None, :]
    d = i + WINDOW - j
    idx_win = np.where((d >= 0) & (d < WINDOW), _t5_bucket_np(d), -1).astype(np.int32)
    kr = np.arange((SEL_T // SEL_TK + 1) * SEL_TK)[:, None] - SEL_TK
    ds = np.arange(SEL_T)[None, :] - kr
    idx_sel = np.where(ds >= 0, _t5_bucket_np(ds), -1).astype(np.int32)
    w = np.arange(N_CMP_PAD)[None, :]
    dc = i - CMP_STRIDE * (w - t // CMP_STRIDE) - (CMP_BLOCK - 1)
    idx_cmp = np.where(w < 2 * (t // CMP_STRIDE), np.where(dc >= 0, _t5_bucket_np(dc), -1),
                       REL_BUCKETS - 1).astype(np.int32)
    return idx_win, idx_sel, idx_cmp


def _tri_lower(n):
    r = lax.broadcasted_iota(jnp.int32, (n, n), 0)
    c = lax.broadcasted_iota(jnp.int32, (n, n), 1)
    return (r >= c).astype(F32)


N_SPLIT = 3


def _fox_aug_lane(h, i):
    return LANES * (h // 2) + (HEAD_DIM if h % 2 == 0 else 0) + i


def _fox_aug_consts():
    pq = np.zeros((N_SPLIT * LANES, GROUP_W), np.float32)
    pk = np.zeros((N_SPLIT * LANES, GROUP_W), np.float32)
    oq = np.zeros((1, GROUP_W), np.float32)
    ok = np.zeros((1, GROUP_W), np.float32)
    for h in range(FOX_HEADS):
        for i in range(N_SPLIT):
            pq[i * LANES + h, _fox_aug_lane(h, i)] = 1.0
            pk[i * LANES + h, _fox_aug_lane(h, N_SPLIT + i)] = -1.0
            oq[0, _fox_aug_lane(h, N_SPLIT + i)] = 1.0
            ok[0, _fox_aug_lane(h, i)] = 1.0
    return pq, pk, oq, ok


def _fox_cumsum_kernel(f_ref, b_ref, pq_ref, pk_ref, oq_ref, ok_ref, qa_ref, ka_ref, carry_ref, *, ts):
    @pl.when(pl.program_id(1) == 0)
    def _():
        carry_ref[...] = jnp.zeros_like(carry_ref)

    z = f_ref[...] + b_ref[...]
    logf = (jnp.minimum(z, 0.0) - jnp.log(1.0 + jnp.exp(-jnp.abs(z)))) * LOG2E
    tri = _tri_lower(LANES)
    carry = carry_ref[...]
    for c in range(ts // LANES):
        rows = slice(c * LANES, (c + 1) * LANES)
        cs = jnp.dot(tri, logf[rows], precision=HIGHEST, preferred_element_type=F32) + carry
        carry = cs[LANES - 1:LANES, :]
        pieces, rest = [], cs
        for _ in range(N_SPLIT):
            piece = rest.astype(BF16)
            pieces.append(piece)
            rest = rest - piece.astype(F32)
        cat = jnp.concatenate(pieces, axis=1)
        qa_ref[rows, :] = (jnp.dot(cat, pq_ref[...], preferred_element_type=F32) + oq_ref[...]).astype(BF16)
        ka_ref[rows, :] = (jnp.dot(cat, pk_ref[...], preferred_element_type=F32) + ok_ref[...]).astype(BF16)
    carry_ref[...] = carry


def _fox_cumsum(ps, f_bias_pad, b, s, ts=512):
    ns = s // ts
    pq, pk, oq, ok = _fox_aug_consts()
    const = lambda bi, si: (0, 0)
    return pl.pallas_call(
        functools.partial(_fox_cumsum_kernel, ts=ts),
        grid=(b, ns),
        in_specs=[
            pl.BlockSpec((ts, LANES), lambda bi, si: (bi * ns + si, 0)),
            pl.BlockSpec((1, LANES), const),
            pl.BlockSpec(pq.shape, const),
            pl.BlockSpec(pk.shape, const),
            pl.BlockSpec(oq.shape, const),
            pl.BlockSpec(ok.shape, const),
        ],
        out_specs=[
            pl.BlockSpec((ts, GROUP_W), lambda bi, si: (bi * ns + si, 0)),
            pl.BlockSpec((ts, GROUP_W), lambda bi, si: (bi * ns + si, 0)),
        ],
        out_shape=[jax.ShapeDtypeStruct((b * s, GROUP_W), BF16), jax.ShapeDtypeStruct((b * s, GROUP_W), BF16)],
        scratch_shapes=[pltpu.VMEM((1, LANES), F32)],
        compiler_params=_params(("parallel", "arbitrary")),
        name="fox_cumsum",
    )(ps, f_bias_pad, jnp.asarray(pq, BF16), jnp.asarray(pk, BF16), jnp.asarray(oq), jnp.asarray(ok))


def _half_masks(rows):
    lo = jnp.where(lax.broadcasted_iota(jnp.int32, (rows, LANES), 1) < HEAD_DIM, 1.0, 0.0)
    return lo.astype(BF16), (1.0 - lo).astype(BF16)


def _fox_attn_kernel(q_ref, qa_ref, k_ref, ka_ref, v_ref, gate_ref, o_ref, qs_ref, kk_ref, vt_ref, m_ref, acc_ref,
                     *, t, tk, ahead):
    qi = pl.program_id(1)
    lo_q, hi_q = _half_masks(t)
    lo = lax.broadcasted_iota(jnp.int32, (t, LANES), 1) < HEAD_DIM
    cm = lax.broadcasted_iota(jnp.int32, (tk, t), 0) - lax.broadcasted_iota(jnp.int32, (tk, t), 1)

    @pl.when(qi == 0)
    def _():
        lo_k, hi_k = _half_masks(tk)

        def merge(j, c):
            rows = pl.ds(pl.multiple_of(j * tk, tk), tk)
            for hp in range(FOX_HEADS // 2):
                ls = slice(hp * LANES, (hp + 1) * LANES)
                kp, ka, vp = k_ref[rows, ls], ka_ref[rows, ls], v_ref[rows, ls]
                kk_ref[2 * hp, rows, :] = kp * lo_k + ka * hi_k
                kk_ref[2 * hp + 1, rows, :] = ka * lo_k + kp * hi_k
                vt_ref[2 * hp, :, rows] = (vp * lo_k + hi_k).astype(F32).T.astype(BF16)
                vt_ref[2 * hp + 1, :, rows] = (lo_k + vp * hi_k).astype(F32).T.astype(BF16)
            return c

        lax.fori_loop(0, k_ref.shape[0] // tk, merge, 0)

    for hp in range(FOX_HEADS // 2):
        ls = slice(hp * LANES, (hp + 1) * LANES)
        qp, qa = q_ref[:, ls], qa_ref[:, ls]
        qs_ref[2 * hp] = qp * lo_q + qa * hi_q
        qs_ref[2 * hp + 1] = qa * lo_q + qp * hi_q
    m_ref[...] = jnp.full(m_ref.shape, NEG_INF, F32)
    acc_ref[...] = jnp.zeros_like(acc_ref)

    def tile(j, diag):
        rows = pl.ds(pl.multiple_of(j * tk, tk), tk)

        def scores(h):
            return lax.dot_general(kk_ref[h, rows, :], qs_ref[h], (((1,), (1,)), ((), ())),
                                   preferred_element_type=F32)

        queue = [scores(h) for h in range(ahead)]
        for h in range(FOX_HEADS):
            if h + ahead < FOX_HEADS:
                queue.append(scores(h + ahead))
            s = queue.pop(0)
            if diag:
                s = jnp.where(cm <= qi * t - j * tk, s, NEG_INF)
            m_old = m_ref[h]
            m_new = jnp.maximum(m_old, jnp.max(s, axis=0, keepdims=True))
            m_ref[h] = m_new
            p = jnp.exp2(s - m_new).astype(BF16)
            acc_ref[h] = jnp.exp2(m_old - m_new) * acc_ref[h] + jnp.dot(vt_ref[h, :, rows], p,
                                                                        preferred_element_type=F32)

    jd = (qi * t) // tk
    for dj in range(max(t // tk, 1)):
        tile(jd + dj, True)

    def body(j, c):
        tile(j, False)
        return c

    lax.fori_loop(0, jd, body, 0)

    for hp in range(FOX_HEADS // 2):
        ls = slice(hp * LANES, (hp + 1) * LANES)
        a0, a1 = acc_ref[2 * hp].T, acc_ref[2 * hp + 1].T
        o = jnp.where(lo, a0 / jnp.maximum(a0[:, HEAD_DIM:HEAD_DIM + 1], TINY),
                      a1 / jnp.maximum(a1[:, 0:1], TINY))
        o_ref[:, ls] = (o * _silu(gate_ref[:, ls].astype(F32))).astype(o_ref.dtype)


def _fox_attn(p, qaug, kaug, b, s, t=512, tk=256, ahead=2):
    nq = s // t
    w = GROUP_W
    return pl.pallas_call(
        functools.partial(_fox_attn_kernel, t=t, tk=tk, ahead=ahead),
        grid=(b, nq),
        in_specs=[
            pl.BlockSpec((t, w), lambda bi, qi: (bi * nq + qi, OFF_FOX_Q // w)),
            pl.BlockSpec((t, w), lambda bi, qi: (bi * nq + qi, 0)),
            pl.BlockSpec((s, w), lambda bi, qi: (bi, OFF_FOX_K // w)),
            pl.BlockSpec((s, w), lambda bi, qi: (bi, 0)),
            pl.BlockSpec((s, w), lambda bi, qi: (bi, OFF_FOX_V // w)),
            pl.BlockSpec((t, w), lambda bi, qi: (bi * nq + qi, OFF_FOX_G // w)),
        ],
        out_specs=pl.BlockSpec((t, w), lambda bi, qi: (bi * nq + qi, 0)),
        out_shape=jax.ShapeDtypeStruct((b * s, w), BF16),
        scratch_shapes=[
            pltpu.VMEM((FOX_HEADS, t, LANES), BF16),
            pltpu.VMEM((FOX_HEADS, s, LANES), BF16),
            pltpu.VMEM((FOX_HEADS, LANES, s), BF16),
            pltpu.VMEM((FOX_HEADS, 1, t), F32),
            pltpu.VMEM((FOX_HEADS, LANES, t), F32),
        ],
        compiler_params=_params(("parallel", "arbitrary")),
        name="fox_attn",
    )(p, qaug, p, kaug, p, p)


def _ssd_kernel(z_ref, xbc_ref, dt_ref, cw_ref, cb_ref, dtb_ref, alog_ref, dsk_ref, ng_ref, o_ref,
                xpad_ref, state_ref, y_ref):
    q = SSM_CHUNK
    halo = 8

    @pl.when(pl.program_id(1) == 0)
    def _():
        xpad_ref[0:halo, :] = jnp.zeros((halo, SSM_CONV_DIM), F32)
        state_ref[...] = jnp.zeros_like(state_ref)

    xpad_ref[halo:halo + q, :] = xbc_ref[...].astype(F32)
    y = cb_ref[...]
    for k in range(SSM_CONV):
        off = halo - (SSM_CONV - 1) + k
        y = y + cw_ref[k:k + 1, :] * xpad_ref[off:off + q, :]
    xpad_ref[0:halo, :] = xpad_ref[q:q + halo, :]
    xc = _silu(y)
    xs = xc[:, :GROUP_W]

    x_dt = dt_ref[...] + dtb_ref[...]
    dt = jnp.maximum(x_dt, 0.0) + jnp.log(1.0 + jnp.exp(-jnp.abs(x_dt)))
    a = dt * (-jnp.exp(alog_ref[...]))
    tri = _tri_lower(q)
    acs = jnp.dot(tri, a, precision=HIGHEST, preferred_element_type=F32)
    acs_t = acs.T
    er = lax.broadcasted_iota(jnp.int32, (LANES, GROUP_W), 0)
    ec = lax.broadcasted_iota(jnp.int32, (LANES, GROUP_W), 1)
    expand = (ec // HEAD_DIM == er).astype(F32)
    dt_full = jnp.dot(dt, expand, precision=HIGHEST, preferred_element_type=F32)
    eacs_full = jnp.dot(jnp.exp(acs), expand, precision=HIGHEST, preferred_element_type=F32)
    dec_full = jnp.dot(jnp.exp(acs[q - 1:q, :] - acs), expand, precision=HIGHEST, preferred_element_type=F32)
    xdt = xs * dt_full
    xdt_b = xdt.astype(BF16)
    xdec_b = (xdt * dec_full).astype(BF16)

    row = lax.broadcasted_iota(jnp.int32, (q, q), 0)
    col = lax.broadcasted_iota(jnp.int32, (q, q), 1)
    causal = row >= col
    lo = lax.broadcasted_iota(jnp.int32, (q, LANES), 1) < HEAD_DIM
    gw = GROUP_W // SSM_GROUPS
    hpg = SSM_HEADS // SSM_GROUPS
    for g in range(SSM_GROUPS):
        bm = xc[:, GROUP_W + g * SSM_STATE:GROUP_W + (g + 1) * SSM_STATE]
        cm = xc[:, GROUP_W + (SSM_GROUPS + g) * SSM_STATE:GROUP_W + (SSM_GROUPS + g + 1) * SSM_STATE]
        bm_b = bm.astype(BF16)
        cm_b = cm.astype(BF16)
        gs = slice(g * gw, (g + 1) * gw)
        cbg = lax.dot_general(cm_b, bm_b, (((1,), (1,)), ((), ())), preferred_element_type=F32)
        st = state_ref[:, gs]
        y_off = jnp.dot(cm_b, st.astype(BF16), preferred_element_type=F32) * eacs_full[:, gs]
        cst = jnp.dot(bm.T.astype(BF16), xdec_b[:, gs], preferred_element_type=F32)
        state_ref[:, gs] = st * eacs_full[q - 1:q, gs] + cst
        for hp in range(hpg // 2):
            ls = slice(g * gw + hp * LANES, g * gw + (hp + 1) * LANES)
            yd = []
            for e in range(2):
                h = g * hpg + 2 * hp + e
                seg = jnp.exp(jnp.where(causal, acs[:, h:h + 1] - acs_t[h:h + 1, :], NEG_INF))
                yd.append(jnp.dot((cbg * seg).astype(BF16), xdt_b[:, ls], preferred_element_type=F32))
            y_ref[:, ls] = jnp.where(lo, yd[0], yd[1]) + y_off[:, hp * LANES:(hp + 1) * LANES]

    yz = (y_ref[...] + xs * dsk_ref[...]) * _silu(z_ref[...].astype(F32))
    for g in range(SSM_GROUPS):
        gs = slice(g * gw, (g + 1) * gw)
        blk = yz[:, gs]
        ms = jnp.mean(blk * blk, axis=-1, keepdims=True)
        o_ref[:, gs] = ((blk * lax.rsqrt(ms + EPS)) * ng_ref[:, gs]).astype(o_ref.dtype)


def _ssd(p, ps, conv_w, conv_b, dt_bias_pad, a_log_pad, d_full, norm_g, b, s):
    q = SSM_CHUNK
    nc = s // q
    row = lambda bi, ci: (bi * nc + ci)
    const = lambda bi, ci: (0, 0)
    return pl.pallas_call(
        _ssd_kernel,
        grid=(b, nc),
        in_specs=[
            pl.BlockSpec((q, GROUP_W), lambda bi, ci: (row(bi, ci), OFF_SSM_Z // GROUP_W)),
            pl.BlockSpec((q, SSM_CONV_DIM), lambda bi, ci: (row(bi, ci), OFF_SSM_XBC // SSM_CONV_DIM)),
            pl.BlockSpec((q, LANES), lambda bi, ci: (row(bi, ci), 1)),
            pl.BlockSpec((SSM_CONV, SSM_CONV_DIM), const),
            pl.BlockSpec((1, SSM_CONV_DIM), const),
            pl.BlockSpec((1, LANES), const),
            pl.BlockSpec((1, LANES), const),
            pl.BlockSpec((1, GROUP_W), const),
            pl.BlockSpec((1, GROUP_W), const),
        ],
        out_specs=pl.BlockSpec((q, GROUP_W), lambda bi, ci: (row(bi, ci), 0)),
        out_shape=jax.ShapeDtypeStruct((b * s, GROUP_W), BF16),
        scratch_shapes=[
            pltpu.VMEM((q + 8, SSM_CONV_DIM), F32),
            pltpu.VMEM((SSM_STATE, GROUP_W), F32),
            pltpu.VMEM((q, GROUP_W), F32),
        ],
        compiler_params=_params(("parallel", "arbitrary")),
        name="ssd",
    )(p, p, ps, conv_w, conv_b, dt_bias_pad, a_log_pad, d_full, norm_g)


def _compress_kernel(r_ref, w1a_ref, w1b_ref, w2_ref, pe_ref, o_ref):
    half = CMP_STRIDE * HEAD_DIM
    r = r_ref[0, 0]
    w1a = w1a_ref[0]
    w1b = w1b_ref[0]
    a = jnp.dot(r, w1a, preferred_element_type=F32)
    bm = jnp.dot(r, w1b, preferred_element_type=F32)
    pe = pe_ref[0].astype(BF16)
    const = (jnp.dot(pe[:, :half], w1a, preferred_element_type=F32)
             + jnp.dot(pe[:, half:], w1b, preferred_element_type=F32))[0:1]
    n = bm.shape[0]
    h = _silu(a + pltpu.roll(bm, n - 1, 0) + const)
    o_ref[0, 0] = jnp.dot(h.astype(BF16), w2_ref[0], preferred_element_type=F32).astype(o_ref.dtype)


def _compress(r, w1a, w1b, w2dup, pe8):
    b, nslot, nr, width = r.shape
    return pl.pallas_call(
        _compress_kernel,
        grid=(b, nslot),
        in_specs=[
            pl.BlockSpec((1, 1, nr, width), lambda bi, si: (bi, si, 0, 0)),
            pl.BlockSpec((1, width, CMP_HIDDEN), lambda bi, si: (si // NSA_KV_HEADS, 0, 0)),
            pl.BlockSpec((1, width, CMP_HIDDEN), lambda bi, si: (si // NSA_KV_HEADS, 0, 0)),
            pl.BlockSpec((1, CMP_HIDDEN, LANES), lambda bi, si: (si // NSA_KV_HEADS, 0, 0)),
            pl.BlockSpec((1, 8, 2 * width), lambda bi, si: (si // NSA_KV_HEADS, 0, 0)),
        ],
        out_specs=pl.BlockSpec((1, 1, nr, LANES), lambda bi, si: (bi, si, 0, 0)),
        out_shape=jax.ShapeDtypeStruct((b, nslot, nr, LANES), BF16),
        compiler_params=_params(("parallel", "arbitrary")),
        name="nsa_compress",
    )(r, w1a, w1b, w2dup, pe8)


def _cmp_select_kernel(q_ref, kv_ref, gl_ref, gate_ref, fc_ref, ov_ref, o_ref, ns_ref, *, n_cmp):
    t = T_ATT
    ncp = kv_ref.shape[2]
    qi = pl.program_id(1)
    t0 = qi * t
    lo = lax.broadcasted_iota(jnp.int32, (t, LANES), 1) < HEAD_DIM
    zero = jnp.zeros((t, LANES), BF16)
    row = lax.broadcasted_iota(jnp.int32, (t, ncp), 0)
    col = lax.broadcasted_iota(jnp.int32, (t, ncp), 1)
    valid = (col * CMP_STRIDE + (CMP_BLOCK - 1) <= t0 + row) & (col < n_cmp)
    per_tile = t // CMP_STRIDE
    shift = lax.rem(qi * per_tile + (ncp - per_tile), ncp)
    beyond = col >= (qi + 1) * per_tile
    gl = gl_ref[...]

    nsel = SEL_BLOCK
    jrow = lax.broadcasted_iota(jnp.int32, (nsel, t), 0)
    cur = (t0 + lax.broadcasted_iota(jnp.int32, (nsel, t), 1)) // SEL_BLOCK
    forced = (jrow == 0) | (jrow == cur) | (jrow == cur - 1)
    past = jrow <= cur

    for g in range(NSA_KV_HEADS):
        kc = kv_ref[0, g]
        vc = kv_ref[0, NSA_KV_HEADS + g]
        psum = jnp.zeros((t, ncp), F32)
        outs = []
        for r in range(NSA_REP):
            h = g * NSA_REP + r
            qp = q_ref[:, (h // 2) * LANES:(h // 2 + 1) * LANES]
            qe = jnp.where(lo, qp, zero) if r % 2 == 0 else jnp.where(lo, zero, qp)
            s = lax.dot_general(qe, kc, (((1,), (1,)), ((), ())), preferred_element_type=F32)
            bias = jnp.where(beyond, NEG_INF, pltpu.roll(fc_ref[h], shift, 1))
            s = jnp.where(valid, s + bias, NEG_INF)
            m = jnp.max(s, axis=1, keepdims=True)
            e = jnp.where(valid, jnp.exp2(s - m), 0.0)
            pr = e / jnp.maximum(jnp.sum(e, axis=1, keepdims=True), TINY)
            psum = psum + pr
            oc = jnp.dot(pr.astype(BF16), vc, preferred_element_type=F32)
            gc = jax.nn.sigmoid(gl[:, h:h + 1])
            outs.append(oc * gc)
        for hp in range(NSA_REP // 2):
            ls = slice((g * NSA_REP // 2 + hp) * LANES, (g * NSA_REP // 2 + hp + 1) * LANES)
            o = jnp.where(lo, outs[2 * hp], outs[2 * hp + 1])
            o_ref[:, ls] = (o * _silu(gate_ref[:, ls].astype(F32))).astype(o_ref.dtype)

        imp = jnp.dot(psum, ov_ref[...], precision=HIGHEST, preferred_element_type=F32)
        imp_t = imp.T[0:nsel, :]
        imp_t = jnp.where(past, jnp.where(forced, SEL_FORCE, imp_t), -SEL_FORCE)
        rank = jnp.zeros((nsel, t), F32)
        for i in range(nsel):
            bi = imp_t[i:i + 1, :]
            gt = jnp.where(bi > imp_t, 1.0, 0.0)
            ge = jnp.where(bi >= imp_t, 1.0, 0.0)
            rank = rank + jnp.where(jrow > i, ge, gt)
        notsel = jnp.where((rank < float(SEL_TOPK)) & past, 0.0, 1.0)
        ns2 = jnp.concatenate([notsel, notsel], axis=0).T
        ns_ref[:, g * LANES:(g + 1) * LANES] = ns2.astype(ns_ref.dtype)


def _cmp_select(p, ps, kvc, fc, overlap, b, s, n_cmp):
    t = T_ATT
    nq = s // t
    w = GROUP_W
    ncp = kvc.shape[2]
    return pl.pallas_call(
        functools.partial(_cmp_select_kernel, n_cmp=n_cmp),
        grid=(b, nq),
        in_specs=[
            pl.BlockSpec((t, w), lambda bi, qi: (bi * nq + qi, OFF_NSA_Q // w)),
            pl.BlockSpec((1, 2 * NSA_KV_HEADS, ncp, LANES), lambda bi, qi: (bi, 0, 0, 0)),
            pl.BlockSpec((t, LANES), lambda bi, qi: (bi * nq + qi, 2)),
            pl.BlockSpec((t, w), lambda bi, qi: (bi * nq + qi, OFF_NSA_G // w)),
            pl.BlockSpec((NSA_HEADS, t, ncp), lambda bi, qi: (0, 0, 0)),
            pl.BlockSpec((ncp, LANES), lambda bi, qi: (0, 0)),
        ],
        out_specs=[
            pl.BlockSpec((t, w), lambda bi, qi: (bi * nq + qi, 0)),
            pl.BlockSpec((t, NSA_KV_HEADS * LANES), lambda bi, qi: (bi * nq + qi, 0)),
        ],
        out_shape=[jax.ShapeDtypeStruct((b * s, w), BF16),
                   jax.ShapeDtypeStruct((b * s, NSA_KV_HEADS * LANES), BF16)],
        compiler_params=_params(("parallel", "arbitrary")),
        name="nsa_cmp_select",
    )(p, kvc, ps, p, fc, overlap)


def _sel_attn_kernel(tab_ref, q_ref, ns_ref, k_ref, v_ref, gl_ref, gate_ref, bs_ref, o_ref,
                     qa_ref, kk_ref, vt_ref, m_ref, acc_ref, *, t, tk, ahead):
    qi = pl.program_id(1)
    lo_q, hi_q = _half_masks(t)
    lo = lax.broadcasted_iota(jnp.int32, (t, LANES), 1) < HEAD_DIM
    n_near = t // tk + 1

    @pl.when(qi == 0)
    def _():
        lane = lax.broadcasted_iota(jnp.int32, (tk, LANES), 1)
        krow = lax.broadcasted_iota(jnp.int32, (tk, LANES), 0)
        lo_f = jnp.where(lane < HEAD_DIM, 1.0, 0.0)
        hi_f = 1.0 - lo_f
        lo_k, hi_k = lo_f.astype(BF16), hi_f.astype(BF16)

        def merge(j, c):
            ks = pl.multiple_of(j * tk, tk)
            rows = pl.ds(ks, tk)
            hot = jnp.where((lane % HEAD_DIM) == (ks + krow) // SEL_BLOCK, NEG_INF, 0.0)
            for g in range(NSA_KV_HEADS):
                gls = slice(g * LANES, (g + 1) * LANES)
                kd, vd = k_ref[rows, gls], v_ref[rows, gls]
                kk_ref[2 * g, rows, :] = kd * lo_k + (hot * hi_f).astype(BF16)
                kk_ref[2 * g + 1, rows, :] = (hot * lo_f).astype(BF16) + kd * hi_k
                vt_ref[2 * g, :, rows] = (vd * lo_k + hi_k).astype(F32).T.astype(BF16)
                vt_ref[2 * g + 1, :, rows] = (lo_k + vd * hi_k).astype(F32).T.astype(BF16)
            return c

        lax.fori_loop(0, k_ref.shape[0] // tk, merge, 0)

    for h in range(NSA_HEADS):
        g = h // NSA_REP
        qp = q_ref[:, (h // 2) * LANES:(h // 2 + 1) * LANES]
        ns = ns_ref[:, g * LANES:(g + 1) * LANES]
        qa_ref[h] = qp * lo_q + ns * hi_q if h % 2 == 0 else ns * lo_q + qp * hi_q
    m_ref[...] = jnp.full(m_ref.shape, NEG_INF, F32)
    acc_ref[...] = jnp.zeros_like(acc_ref)

    def tile(j, near):
        rows = pl.ds(pl.multiple_of(j * tk, tk), tk)
        kv = lambda h: 2 * (h // NSA_REP) + h % 2

        def scores(h):
            return lax.dot_general(kk_ref[kv(h), rows, :], qa_ref[h], (((1,), (1,)), ((), ())),
                                   preferred_element_type=F32)

        queue = [scores(h) for h in range(ahead)]
        for h in range(NSA_HEADS):
            if h + ahead < NSA_HEADS:
                queue.append(scores(h + ahead))
            s = queue.pop(0)
            m_old = m_ref[h]
            if near is None:
                far = tab_ref[(REL_BUCKETS - 1) * NSA_HEADS + h] * LOG2E
                m_new = jnp.maximum(m_old, jnp.max(s, axis=0, keepdims=True) + far)
                p = jnp.exp2(s - (m_new - far)).astype(BF16)
            else:
                s = s + bs_ref[h, near * tk:(near + 1) * tk, :]
                m_new = jnp.maximum(m_old, jnp.max(s, axis=0, keepdims=True))
                p = jnp.exp2(s - m_new).astype(BF16)
            m_ref[h] = m_new
            acc_ref[h] = jnp.exp2(m_old - m_new) * acc_ref[h] + jnp.dot(vt_ref[kv(h), :, rows], p,
                                                                        preferred_element_type=F32)

    jd = (qi * t) // tk
    for dj in range(n_near - 1):
        tile(jd + dj, 1 + dj)

    @pl.when(qi > 0)
    def _():
        tile(jd - 1, 0)

    def body(j, c):
        tile(j, None)
        return c

    lax.fori_loop(0, jnp.maximum(jd - 1, 0), body, 0)

    gl = gl_ref[...]
    for hp in range(NSA_HEADS // 2):
        ls = slice(hp * LANES, (hp + 1) * LANES)
        a0, a1 = acc_ref[2 * hp].T, acc_ref[2 * hp + 1].T
        c0 = NSA_HEADS + 2 * hp
        o = jnp.where(lo, a0 / jnp.maximum(a0[:, HEAD_DIM:HEAD_DIM + 1], TINY) * jax.nn.sigmoid(gl[:, c0:c0 + 1]),
                      a1 / jnp.maximum(a1[:, 0:1], TINY) * jax.nn.sigmoid(gl[:, c0 + 1:c0 + 2]))
        o_ref[:, ls] = (o * _silu(gate_ref[:, ls].astype(F32))).astype(o_ref.dtype)


def _sel_attn(tab_flat, p, ps, notsel, bsel, b, s, t=512, tk=256, ahead=2):
    nq = s // t
    w = GROUP_W
    kw = NSA_KV_HEADS * LANES
    n_near = t // tk + 1
    return pl.pallas_call(
        functools.partial(_sel_attn_kernel, t=t, tk=tk, ahead=ahead),
        grid=(b, nq),
        in_specs=[
            pl.BlockSpec(memory_space=pltpu.SMEM),
            pl.BlockSpec((t, w), lambda bi, qi: (bi * nq + qi, OFF_NSA_Q // w)),
            pl.BlockSpec((t, kw), lambda bi, qi: (bi * nq + qi, 0)),
            pl.BlockSpec((s, kw), lambda bi, qi: (bi, OFF_SEL_K // kw)),
            pl.BlockSpec((s, kw), lambda bi, qi: (bi, OFF_SEL_V // kw)),
            pl.BlockSpec((t, LANES), lambda bi, qi: (bi * nq + qi, 2)),
            pl.BlockSpec((t, w), lambda bi, qi: (bi * nq + qi, OFF_NSA_G // w)),
            pl.BlockSpec((NSA_HEADS, n_near * tk, t), lambda bi, qi: (0, 0, 0), pipeline_mode=pl.Buffered(1)),
        ],
        out_specs=pl.BlockSpec((t, w), lambda bi, qi: (bi * nq + qi, 0)),
        out_shape=jax.ShapeDtypeStruct((b * s, w), BF16),
        scratch_shapes=[
            pltpu.VMEM((NSA_HEADS, t, LANES), BF16),
            pltpu.VMEM((2 * NSA_KV_HEADS, s, LANES), BF16),
            pltpu.VMEM((2 * NSA_KV_HEADS, LANES, s), BF16),
            pltpu.VMEM((NSA_HEADS, 1, t), F32),
            pltpu.VMEM((NSA_HEADS, LANES, t), F32),
        ],
        compiler_params=_params(("parallel", "arbitrary")),
        name="nsa_sel_attn",
    )(tab_flat, p, notsel, p, p, ps, p, bsel)


def _win_attn_kernel(q_ref, k_ref, v_ref, gl_ref, gate_ref, bw_ref, o_ref):
    t = T_ATT
    nt = WINDOW // t + 1
    qi = pl.program_id(1)
    lo = lax.broadcasted_iota(jnp.int32, (t, LANES), 1) < HEAD_DIM
    zero = jnp.zeros((t, LANES), BF16)
    gl = gl_ref[...]

    for g in range(NSA_KV_HEADS):
        gls = slice(g * LANES, (g + 1) * LANES)
        kts, vts, pens = [], [], []
        for j in range(nt):
            kt = qi - (nt - 1) + j
            ks = pl.multiple_of(jnp.maximum(kt, 0) * t, t)
            kts.append(k_ref[pl.ds(ks, t), gls])
            vts.append(v_ref[pl.ds(ks, t), gls])
            pens.append(jnp.where(kt >= 0, 0.0, NEG_INF))
        outs = []
        for r in range(NSA_REP):
            h = g * NSA_REP + r
            qp = q_ref[:, (h // 2) * LANES:(h // 2 + 1) * LANES]
            qe = jnp.where(lo, qp, zero) if r % 2 == 0 else jnp.where(lo, zero, qp)
            ss = []
            for j in range(nt):
                s = lax.dot_general(qe, kts[j], (((1,), (1,)), ((), ())), preferred_element_type=F32)
                ss.append(s + (bw_ref[h, :, j * t:(j + 1) * t] + pens[j]))
            m = functools.reduce(jnp.maximum, [jnp.max(s, axis=1, keepdims=True) for s in ss])
            ps = [jnp.exp2(s - m) for s in ss]
            l = functools.reduce(lambda a, c: a + c, [jnp.sum(p, axis=1, keepdims=True) for p in ps])
            o = functools.reduce(lambda a, c: a + c,
                                 [jnp.dot(p.astype(BF16), v, preferred_element_type=F32) for p, v in zip(ps, vts)])
            gw = jax.nn.sigmoid(gl[:, 2 * NSA_HEADS + h:2 * NSA_HEADS + h + 1])
            outs.append(o / jnp.maximum(l, TINY) * gw)
        for hp in range(NSA_REP // 2):
            ols = slice((g * NSA_REP // 2 + hp) * LANES, (g * NSA_REP // 2 + hp + 1) * LANES)
            o = jnp.where(lo, outs[2 * hp], outs[2 * hp + 1])
            o_ref[:, ols] = (o * _silu(gate_ref[:, ols].astype(F32))).astype(o_ref.dtype)


def _win_attn(p, ps, bwin, b, s):
    t = T_ATT
    nq = s // t
    w = GROUP_W
    kw = NSA_KV_HEADS * LANES
    return pl.pallas_call(
        _win_attn_kernel,
        grid=(b, nq),
        in_specs=[
            pl.BlockSpec((t, w), lambda bi, qi: (bi * nq + qi, OFF_NSA_Q // w)),
            pl.BlockSpec((s, kw), lambda bi, qi: (bi, OFF_WIN_K // kw)),
            pl.BlockSpec((s, kw), lambda bi, qi: (bi, OFF_WIN_V // kw)),
            pl.BlockSpec((t, LANES), lambda bi, qi: (bi * nq + qi, 2)),
            pl.BlockSpec((t, w), lambda bi, qi: (bi * nq + qi, OFF_NSA_G // w)),
            pl.BlockSpec((NSA_HEADS, t, WINDOW + t), lambda bi, qi: (0, 0, 0)),
        ],
        out_specs=pl.BlockSpec((t, w), lambda bi, qi: (bi * nq + qi, 0)),
        out_shape=jax.ShapeDtypeStruct((b * s, w), BF16),
        compiler_params=_params(("parallel", "arbitrary")),
        name="nsa_win_attn",
    )(p, p, p, ps, p, bwin)


def _mem_attn_kernel(q_ref, gate_ref, kv_ref, o_ref):
    scale = MEM_HEAD_DIM ** -0.5
    for h in range(MEM_HEADS):
        ls = slice(h * LANES, (h + 1) * LANES)
        k = kv_ref[:, ls]
        v = kv_ref[:, GROUP_W + h * LANES:GROUP_W + (h + 1) * LANES]
        s = lax.dot_general(q_ref[:, ls], k, (((1,), (1,)), ((), ())), preferred_element_type=F32) * scale
        m = jnp.max(s, axis=1, keepdims=True)
        e = jnp.exp(s - m)
        l = jnp.sum(e, axis=1, keepdims=True)
        o = jnp.dot(e.astype(BF16), v, preferred_element_type=F32) / l
        o_ref[:, ls] = (o * _silu(gate_ref[:, ls].astype(F32))).astype(o_ref.dtype)


def _mem_attn(p, mem_kv, b, s, t=512):
    nq = s // t
    w = GROUP_W
    m = mem_kv.shape[0] // b
    return pl.pallas_call(
        _mem_attn_kernel,
        grid=(b, nq),
        in_specs=[
            pl.BlockSpec((t, w), lambda bi, qi: (bi * nq + qi, OFF_MEM_Q // w)),
            pl.BlockSpec((t, w), lambda bi, qi: (bi * nq + qi, OFF_MEM_G // w)),
            pl.BlockSpec((m, 2 * w), lambda bi, qi: (bi, 0)),
        ],
        out_specs=pl.BlockSpec((t, w), lambda bi, qi: (bi * nq + qi, 0)),
        out_shape=jax.ShapeDtypeStruct((b * s, w), BF16),
        compiler_params=_params(("parallel", "arbitrary")),
        name="mem_attn",
    )(p, p, mem_kv)


def _out_proj_kernel(x_ref, of_ref, os_ref, oc_ref, osel_ref, ow_ref, om_ref, w_ref, g_ref, o_ref, *, final):
    w = GROUP_W
    nsa = (oc_ref[...].astype(F32) + osel_ref[...].astype(F32) + ow_ref[...].astype(F32)).astype(BF16)
    acc = x_ref[...]
    for i, part in enumerate((of_ref[...], os_ref[...], nsa, om_ref[...])):
        acc = acc + jnp.dot(part, w_ref[i * w:(i + 1) * w, :], preferred_element_type=F32)
    if final:
        ms = jnp.mean(acc * acc, axis=-1, keepdims=True)
        acc = (acc * lax.rsqrt(ms + EPS)) * g_ref[...]
    o_ref[...] = acc


def _out_proj(x2d, parts, w_out, g, final, tm=512):
    n, d = x2d.shape
    w = GROUP_W
    part_spec = pl.BlockSpec((tm, w), lambda i: (i, 0))
    return pl.pallas_call(
        functools.partial(_out_proj_kernel, final=final),
        grid=(n // tm,),
        in_specs=[pl.BlockSpec((tm, d), lambda i: (i, 0))] + [part_spec] * 6 + [
            pl.BlockSpec((4 * w, d), lambda i: (0, 0)),
            pl.BlockSpec((1, d), lambda i: (0, 0)),
        ],
        out_specs=pl.BlockSpec((tm, d), lambda i: (i, 0)),
        out_shape=jax.ShapeDtypeStruct((n, d), F32),
        compiler_params=_params(("parallel",)),
        name="out_proj",
    )(x2d, *parts, w_out, g.reshape(1, d))


def _pack_indices():
    fox, ssm, nsa, mem = 0, FOX_COLS, FOX_COLS + SSM_COLS, FOX_COLS + SSM_COLS + NSA_COLS
    w = GROUP_W
    rng = lambda a, n: list(range(a, a + n))
    kv = lambda slot: nsa + w + slot * NSA_KV_W
    dup = lambda base: (rng(base, HEAD_DIM) * 2) + (rng(base + HEAD_DIM, HEAD_DIM) * 2)
    main = (rng(fox, 4 * w)
            + rng(ssm + w, SSM_CONV_DIM) + rng(ssm, w)
            + rng(nsa, w) + rng(nsa + w + 6 * NSA_KV_W + 3 * NSA_HEADS, w)
            + rng(mem, 2 * w)
            + dup(kv(2)) + dup(kv(3)) + dup(kv(4)) + dup(kv(5))
            + rng(kv(0), NSA_KV_W) + rng(kv(1), NSA_KV_W))
    assert len(main) == P_COLS
    scale = np.ones((P_COLS,), np.float32)
    scale[OFF_FOX_Q:OFF_FOX_Q + w] = HEAD_DIM ** -0.5 * LOG2E
    scale[OFF_NSA_Q:OFF_NSA_Q + w] = HEAD_DIM ** -0.5 * LOG2E
    small = np.zeros((PS_COLS,), np.int32)
    keep = np.zeros((PS_COLS,), np.float32)
    for blk, (src, n) in enumerate(((fox + 4 * w, FOX_HEADS), (ssm + w + SSM_CONV_DIM, SSM_HEADS),
                                    (nsa + w + 6 * NSA_KV_W, 3 * NSA_HEADS))):
        small[blk * LANES:blk * LANES + n] = np.arange(src, src + n)
        keep[blk * LANES:blk * LANES + n] = 1.0
    return np.asarray(main, np.int32), scale, small, keep


def _pad_lanes(v):
    return jnp.pad(v.astype(F32), (0, LANES - v.shape[0])).reshape(1, LANES)


def _trunk(x, mem, norm_g, w_in, fox_f_bias, ssm_conv_w, ssm_conv_b, ssm_dt_bias, ssm_a_log, ssm_d,
           ssm_norm_g, nsa_cmp_pe, nsa_cmp_w1, nsa_cmp_w2, rel_bias_table, mem_norm_g, w_mem_kv, w_out,
           final_norm_g):
    b, s, d = x.shape
    depth = w_in.shape[0]
    n = b * s
    m_tok = mem.shape[1]
    n_cmp = (s - CMP_BLOCK) // CMP_STRIDE + 1
    n_rows = s // CMP_STRIDE
    assert s % 512 == 0 and s // SEL_BLOCK <= HEAD_DIM and n_rows <= N_CMP_PAD and d == D_MODEL

    main_idx, main_scale, small_idx, small_keep = _pack_indices()
    tab_flat = rel_bias_table.astype(F32).reshape(-1)
    idx_win, idx_sel, idx_cmp = _bias_indices()
    bwin = _bias_table(tab_flat, idx_win)
    bsel = _bias_table(tab_flat, idx_sel)
    fcmp = _bias_table(tab_flat, idx_cmp)

    cs = np.arange(N_CMP_PAD)[:, None] * CMP_STRIDE
    js = np.arange(LANES)[None, :] * SEL_BLOCK
    overlap = ((cs < js + SEL_BLOCK) & (cs + CMP_BLOCK > js) & (np.arange(N_CMP_PAD)[:, None] < n_cmp)
               & (np.arange(LANES)[None, :] < s // SEL_BLOCK)).astype(np.float32)
    overlap = jnp.asarray(overlap)

    x2d = x.reshape(n, d)
    mem2d = mem.reshape(b * m_tok, d)
    half = CMP_STRIDE * HEAD_DIM
    for l in range(depth):
        w_main = (w_in[l][:, main_idx] * main_scale).astype(BF16)
        w_small = (w_in[l][:, small_idx] * small_keep).astype(BF16)
        p, ps = _norm_proj(x2d, norm_g[l], w_main, w_small)

        qaug, kaug = _fox_cumsum(ps, _pad_lanes(fox_f_bias[l]), b, s)
        o_fox = _fox_attn(p, qaug, kaug, b, s)

        o_ssd = _ssd(p, ps, ssm_conv_w[l].astype(F32), ssm_conv_b[l].reshape(1, -1).astype(F32),
                     _pad_lanes(ssm_dt_bias[l]), _pad_lanes(ssm_a_log[l]),
                     jnp.repeat(ssm_d[l].astype(F32), HEAD_DIM).reshape(1, GROUP_W),
                     ssm_norm_g[l].reshape(1, GROUP_W).astype(F32), b, s)

        kvc = p[:, OFF_CMP_KV:OFF_CMP_KV + 2 * NSA_KV_W]
        r = kvc.reshape(b, n_rows, CMP_STRIDE, 2 * NSA_KV_HEADS, HEAD_DIM).transpose(0, 3, 1, 2, 4)
        r = r.reshape(b, 2 * NSA_KV_HEADS, n_rows, half)
        if n_rows < N_CMP_PAD:
            r = jnp.pad(r, ((0, 0), (0, 0), (0, N_CMP_PAD - n_rows), (0, 0)))
        w1 = nsa_cmp_w1[l].astype(BF16)
        w2dup = jnp.concatenate([nsa_cmp_w2[l], nsa_cmp_w2[l]], axis=-1).astype(BF16)
        pe8 = jnp.broadcast_to(nsa_cmp_pe[l].astype(F32).reshape(2, 1, 2 * half), (2, 8, 2 * half))
        kv_cmp = _compress(r, w1[:, :half], w1[:, half:], w2dup, pe8)
        o_cmp, notsel = _cmp_select(p, ps, kv_cmp, fcmp, overlap, b, s, n_cmp)
        o_sel = _sel_attn(tab_flat, p, ps, notsel, bsel, b, s)
        o_win = _win_attn(p, ps, bwin, b, s)

        w_kv = w_mem_kv[l].astype(BF16)
        mem_kv, _ = _norm_proj(mem2d, mem_norm_g[l], w_kv, w_kv[:, :LANES], tm=min(512, b * m_tok))
        o_mem = _mem_attn(p, mem_kv, b, s)

        x2d = _out_proj(x2d, (o_fox, o_ssd, o_cmp, o_sel, o_win, o_mem), w_out[l].astype(BF16),
                        final_norm_g, final=(l == depth - 1))
    return x2d.reshape(b, s, d)


def kernel(x, mem, norm_g, w_in, fox_f_bias, ssm_conv_w, ssm_conv_b, ssm_dt_bias, ssm_a_log, ssm_d, ssm_norm_g,
           nsa_cmp_pe, nsa_cmp_w1, nsa_cmp_w2, rel_bias_table, mem_norm_g, w_mem_kv, w_out, final_norm_g):
    return _trunk(x, mem, norm_g, w_in, fox_f_bias, ssm_conv_w, ssm_conv_b, ssm_dt_bias, ssm_a_log, ssm_d,
                  ssm_norm_g, nsa_cmp_pe, nsa_cmp_w1, nsa_cmp_w2, rel_bias_table, mem_norm_g, w_mem_kv, w_out,
                  final_norm_g)
```

```python
import functools
import math

import numpy as np
import jax
import jax.numpy as jnp
from jax import lax
from jax.experimental import pallas as pl
from jax.experimental.pallas import tpu as pltpu

F32 = jnp.float32
BF16 = jnp.bfloat16
HIGHEST = lax.Precision.HIGHEST

D_MODEL = 1024
GROUP_W = 512
HEAD_DIM = 64
EPS = 1e-6
NEG_INF = -1e30
TINY = 1e-30
LOG2E = math.log2(math.e)

FOX_HEADS = 8
SSM_HEADS = 8
SSM_STATE = 128
SSM_GROUPS = 2
SSM_CONV = 4
SSM_CHUNK = 128
SSM_CONV_DIM = GROUP_W + 2 * SSM_GROUPS * SSM_STATE

NSA_HEADS = 8
NSA_KV_HEADS = 2
NSA_REP = NSA_HEADS // NSA_KV_HEADS
NSA_KV_W = NSA_KV_HEADS * HEAD_DIM
CMP_BLOCK = 32
CMP_STRIDE = 16
CMP_HIDDEN = 2 * HEAD_DIM
SEL_BLOCK = 64
SEL_TOPK = 16
WINDOW = 512
SEL_FORCE = 1e9

MEM_HEADS = 4
MEM_HEAD_DIM = GROUP_W // MEM_HEADS
REL_BUCKETS = 32
REL_MAX_DIST = 128

FOX_COLS = 4 * GROUP_W + FOX_HEADS
SSM_COLS = GROUP_W + SSM_CONV_DIM + SSM_HEADS
NSA_COLS = 2 * GROUP_W + 6 * NSA_KV_W + 3 * NSA_HEADS
MEM_COLS = 2 * GROUP_W

LANES = 128
VMEM_LIMIT = 56 * 1024 * 1024

OFF_FOX_Q, OFF_FOX_K, OFF_FOX_V, OFF_FOX_G = 0, 512, 1024, 1536
OFF_SSM_XBC, OFF_SSM_Z = 2048, 3072
OFF_NSA_Q, OFF_NSA_G = 3584, 4096
OFF_MEM_Q, OFF_MEM_G = 4608, 5120
OFF_SEL_K, OFF_SEL_V, OFF_WIN_K, OFF_WIN_V, OFF_CMP_KV = 5632, 5888, 6144, 6400, 6656
P_COLS = 6912
PS_COLS = 3 * LANES

T_ATT = 256
SEL_T, SEL_TK = 512, 256
N_CMP_PAD = 256


def _params(sem):
    return pltpu.CompilerParams(dimension_semantics=sem, vmem_limit_bytes=VMEM_LIMIT)


def _t5_bucket_np(dist):
    n = np.maximum(dist, 0)
    max_exact = REL_BUCKETS // 2
    nf = np.maximum(n, 1).astype(np.float32)
    large = max_exact + (np.log(nf / np.float32(max_exact)) / np.float32(math.log(REL_MAX_DIST / max_exact))
                         * np.float32(REL_BUCKETS - max_exact)).astype(np.int32)
    large = np.minimum(large, REL_BUCKETS - 1)
    return np.where(n < max_exact, n, large).astype(np.int32)


def _silu(x):
    return x * jax.nn.sigmoid(x)


def _norm_proj_kernel(x_ref, g_ref, w_ref, ws_ref, p_ref, ps_ref, *, chunk):
    x = x_ref[...]
    ms = jnp.mean(x * x, axis=-1, keepdims=True)
    h = ((x * lax.rsqrt(ms + EPS)) * g_ref[...]).astype(BF16)
    ncol = p_ref.shape[1]
    for c0 in range(0, ncol, chunk):
        c1 = min(c0 + chunk, ncol)
        p_ref[:, c0:c1] = jnp.dot(h, w_ref[:, c0:c1], preferred_element_type=F32).astype(p_ref.dtype)
    ps_ref[...] = jnp.dot(h, ws_ref[...], preferred_element_type=F32)


def _norm_proj(x2d, g, w_main, w_small, tm=512):
    n, d = x2d.shape
    pc, sc = w_main.shape[1], w_small.shape[1]
    return pl.pallas_call(
        functools.partial(_norm_proj_kernel, chunk=512),
        grid=(n // tm,),
        in_specs=[
            pl.BlockSpec((tm, d), lambda i: (i, 0)),
            pl.BlockSpec((1, d), lambda i: (0, 0)),
            pl.BlockSpec((d, pc), lambda i: (0, 0)),
            pl.BlockSpec((d, sc), lambda i: (0, 0)),
        ],
        out_specs=[
            pl.BlockSpec((tm, pc), lambda i: (i, 0)),
            pl.BlockSpec((tm, sc), lambda i: (i, 0)),
        ],
        out_shape=[jax.ShapeDtypeStruct((n, pc), BF16), jax.ShapeDtypeStruct((n, sc), F32)],
        compiler_params=_params(("parallel",)),
        name="norm_proj",
    )(x2d, g.reshape(1, d), w_main, w_small)


def _bias_table_kernel(tab_ref, idx_ref, o_ref):
    h = pl.program_id(0)
    idx = idx_ref[...]
    acc = jnp.full(idx.shape, NEG_INF, F32)
    for b in range(REL_BUCKETS):
        acc = jnp.where(idx == b, tab_ref[b * NSA_HEADS + h] * LOG2E, acc)
    o_ref[0] = acc


def _bias_table(tab_flat, idx):
    r, c = idx.shape
    return pl.pallas_call(
        _bias_table_kernel,
        grid=(NSA_HEADS,),
        in_specs=[
            pl.BlockSpec(memory_space=pltpu.SMEM),
            pl.BlockSpec((r, c), lambda h: (0, 0)),
        ],
        out_specs=pl.BlockSpec((1, r, c), lambda h: (h, 0, 0)),
        out_shape=jax.ShapeDtypeStruct((NSA_HEADS, r, c), F32),
        compiler_params=_params(("arbitrary",)),
        name="t5_bias_table",
    )(tab_flat, jnp.asarray(idx))


def _bias_indices():
    t = T_ATT
    i = np.arange(t)[:, None]
    j = np.arange(WINDOW + t)[None, :]
    d = i + WINDOW - j
    idx_win = np.where((d >= 0) & (d < WINDOW), _t5_bucket_np(d), -1).astype(np.int32)
    kr = np.arange((SEL_T // SEL_TK + 1) * SEL_TK)[:, None] - SEL_TK
    ds = np.arange(SEL_T)[None, :] - kr
    idx_sel = np.where(ds >= 0, _t5_bucket_np(ds), -1).astype(np.int32)
    w = np.arange(N_CMP_PAD)[None, :]
    dc = i - CMP_STRIDE * (w - t // CMP_STRIDE) - (CMP_BLOCK - 1)
    idx_cmp = np.where(w < 2 * (t // CMP_STRIDE), np.where(dc >= 0, _t5_bucket_np(dc), -1),
                       REL_BUCKETS - 1).astype(np.int32)
    return idx_win, idx_sel, idx_cmp


def _tri_lower(n):
    r = lax.broadcasted_iota(jnp.int32, (n, n), 0)
    c = lax.broadcasted_iota(jnp.int32, (n, n), 1)
    return (r >= c).astype(F32)


N_SPLIT = 3


def _fox_aug_lane(h, i):
    return LANES * (h // 2) + (HEAD_DIM if h % 2 == 0 else 0) + i


def _fox_aug_consts():
    pq = np.zeros((N_SPLIT * LANES, GROUP_W), np.float32)
    pk = np.zeros((N_SPLIT * LANES, GROUP_W), np.float32)
    oq = np.zeros((1, GROUP_W), np.float32)
    ok = np.zeros((1, GROUP_W), np.float32)
    for h in range(FOX_HEADS):
        for i in range(N_SPLIT):
            pq[i * LANES + h, _fox_aug_lane(h, i)] = 1.0
            pk[i * LANES + h, _fox_aug_lane(h, N_SPLIT + i)] = -1.0
            oq[0, _fox_aug_lane(h, N_SPLIT + i)] = 1.0
            ok[0, _fox_aug_lane(h, i)] = 1.0
    return pq, pk, oq, ok


def _fox_cumsum_kernel(f_ref, b_ref, pq_ref, pk_ref, oq_ref, ok_ref, qa_ref, ka_ref, carry_ref, *, ts):
    @pl.when(pl.program_id(1) == 0)
    def _():
        carry_ref[...] = jnp.zeros_like(carry_ref)

    z = f_ref[...] + b_ref[...]
    logf = (jnp.minimum(z, 0.0) - jnp.log(1.0 + jnp.exp(-jnp.abs(z)))) * LOG2E
    tri = _tri_lower(LANES)
    carry = carry_ref[...]
    for c in range(ts // LANES):
        rows = slice(c * LANES, (c + 1) * LANES)
        cs = jnp.dot(tri, logf[rows], precision=HIGHEST, preferred_element_type=F32) + carry
        carry = cs[LANES - 1:LANES, :]
        pieces, rest = [], cs
        for _ in range(N_SPLIT):
            piece = rest.astype(BF16)
            pieces.append(piece)
            rest = rest - piece.astype(F32)
        cat = jnp.concatenate(pieces, axis=1)
        qa_ref[rows, :] = (jnp.dot(cat, pq_ref[...], preferred_element_type=F32) + oq_ref[...]).astype(BF16)
        ka_ref[rows, :] = (jnp.dot(cat, pk_ref[...], preferred_element_type=F32) + ok_ref[...]).astype(BF16)
    carry_ref[...] = carry


def _fox_cumsum(ps, f_bias_pad, b, s, ts=512):
    ns = s // ts
    pq, pk, oq, ok = _fox_aug_consts()
    const = lambda bi, si: (0, 0)
    return pl.pallas_call(
        functools.partial(_fox_cumsum_kernel, ts=ts),
        grid=(b, ns),
        in_specs=[
            pl.BlockSpec((ts, LANES), lambda bi, si: (bi * ns + si, 0)),
            pl.BlockSpec((1, LANES), const),
            pl.BlockSpec(pq.shape, const),
            pl.BlockSpec(pk.shape, const),
            pl.BlockSpec(oq.shape, const),
            pl.BlockSpec(ok.shape, const),
        ],
        out_specs=[
            pl.BlockSpec((ts, GROUP_W), lambda bi, si: (bi * ns + si, 0)),
            pl.BlockSpec((ts, GROUP_W), lambda bi, si: (bi * ns + si, 0)),
        ],
        out_shape=[jax.ShapeDtypeStruct((b * s, GROUP_W), BF16), jax.ShapeDtypeStruct((b * s, GROUP_W), BF16)],
        scratch_shapes=[pltpu.VMEM((1, LANES), F32)],
        compiler_params=_params(("parallel", "arbitrary")),
        name="fox_cumsum",
    )(ps, f_bias_pad, jnp.asarray(pq, BF16), jnp.asarray(pk, BF16), jnp.asarray(oq), jnp.asarray(ok))


def _half_masks(rows):
    lo = jnp.where(lax.broadcasted_iota(jnp.int32, (rows, LANES), 1) < HEAD_DIM, 1.0, 0.0)
    return lo.astype(BF16), (1.0 - lo).astype(BF16)


def _fox_attn_kernel(q_ref, qa_ref, k_ref, ka_ref, v_ref, gate_ref, o_ref, qs_ref, kk_ref, vt_ref, m_ref, acc_ref,
                     pre_ref, *, t, tk, ahead):
    qi = pl.program_id(1)
    lo_q, hi_q = _half_masks(t)
    lo = lax.broadcasted_iota(jnp.int32, (t, LANES), 1) < HEAD_DIM
    cm = lax.broadcasted_iota(jnp.int32, (tk, t), 0) - lax.broadcasted_iota(jnp.int32, (tk, t), 1)

    @pl.when(qi == 0)
    def _():
        lo_k, hi_k = _half_masks(tk)

        def merge(j, c):
            rows = pl.ds(pl.multiple_of(j * tk, tk), tk)
            for hp in range(FOX_HEADS // 2):
                ls = slice(hp * LANES, (hp + 1) * LANES)
                kp, ka, vp = k_ref[rows, ls], ka_ref[rows, ls], v_ref[rows, ls]
                kk_ref[2 * hp, rows, :] = kp * lo_k + ka * hi_k
                kk_ref[2 * hp + 1, rows, :] = ka * lo_k + kp * hi_k
                vt_ref[2 * hp, :, rows] = (vp * lo_k + hi_k).astype(F32).T.astype(BF16)
                vt_ref[2 * hp + 1, :, rows] = (lo_k + vp * hi_k).astype(F32).T.astype(BF16)
            return c

        lax.fori_loop(0, k_ref.shape[0] // tk, merge, 0)

    for hp in range(FOX_HEADS // 2):
        ls = slice(hp * LANES, (hp + 1) * LANES)
        qp, qa = q_ref[:, ls], qa_ref[:, ls]
        qs_ref[2 * hp] = qp * lo_q + qa * hi_q
        qs_ref[2 * hp + 1] = qa * lo_q + qp * hi_q
    m_ref[...] = jnp.full(m_ref.shape, NEG_INF, F32)
    acc_ref[...] = jnp.zeros_like(acc_ref)

    def rows_of(j):
        return pl.ds(pl.multiple_of(j * tk, tk), tk)

    def scores(j, h):
        return lax.dot_general(kk_ref[h, rows_of(j), :], qs_ref[h], (((1,), (1,)), ((), ())),
                               preferred_element_type=F32)

    def tile(j, diag, j_next):
        queue = [pre_ref[i] for i in range(ahead)]
        for h in range(FOX_HEADS):
            if h + ahead < FOX_HEADS:
                queue.append(scores(j, h + ahead))
            elif j_next is not None:
                pre_ref[h + ahead - FOX_HEADS] = scores(j_next, h + ahead - FOX_HEADS)
            s = queue.pop(0)
            if diag:
                s = jnp.where(cm <= qi * t - j * tk, s, NEG_INF)
            m_old = m_ref[h]
            m_new = jnp.maximum(m_old, jnp.max(s, axis=0, keepdims=True))
            m_ref[h] = m_new
            p = jnp.exp2(s - m_new).astype(BF16)
            acc_ref[h] = jnp.exp2(m_old - m_new) * acc_ref[h] + jnp.dot(vt_ref[h, :, rows_of(j)], p,
                                                                        preferred_element_type=F32)

    jd = (qi * t) // tk
    n_diag = max(t // tk, 1)
    for i in range(ahead):
        pre_ref[i] = scores(0, i)

    def body(j, c):
        tile(j, False, j + 1)
        return c

    lax.fori_loop(0, jd, body, 0)
    for dj in range(n_diag):
        tile(jd + dj, True, jd + dj + 1 if dj + 1 < n_diag else None)

    for hp in range(FOX_HEADS // 2):
        ls = slice(hp * LANES, (hp + 1) * LANES)
        a0, a1 = acc_ref[2 * hp].T, acc_ref[2 * hp + 1].T
        o = jnp.where(lo, a0 / jnp.maximum(a0[:, HEAD_DIM:HEAD_DIM + 1], TINY),
                      a1 / jnp.maximum(a1[:, 0:1], TINY))
        o_ref[:, ls] = (o * _silu(gate_ref[:, ls].astype(F32))).astype(o_ref.dtype)


def _fox_attn(p, qaug, kaug, b, s, t=512, tk=256, ahead=2):
    nq = s // t
    w = GROUP_W
    return pl.pallas_call(
        functools.partial(_fox_attn_kernel, t=t, tk=tk, ahead=ahead),
        grid=(b, nq),
        in_specs=[
            pl.BlockSpec((t, w), lambda bi, qi: (bi * nq + qi, OFF_FOX_Q // w)),
            pl.BlockSpec((t, w), lambda bi, qi: (bi * nq + qi, 0)),
            pl.BlockSpec((s, w), lambda bi, qi: (bi, OFF_FOX_K // w)),
            pl.BlockSpec((s, w), lambda bi, qi: (bi, 0)),
            pl.BlockSpec((s, w), lambda bi, qi: (bi, OFF_FOX_V // w)),
            pl.BlockSpec((t, w), lambda bi, qi: (bi * nq + qi, OFF_FOX_G // w)),
        ],
        out_specs=pl.BlockSpec((t, w), lambda bi, qi: (bi * nq + qi, 0)),
        out_shape=jax.ShapeDtypeStruct((b * s, w), BF16),
        scratch_shapes=[
            pltpu.VMEM((FOX_HEADS, t, LANES), BF16),
            pltpu.VMEM((FOX_HEADS, s, LANES), BF16),
            pltpu.VMEM((FOX_HEADS, LANES, s), BF16),
            pltpu.VMEM((FOX_HEADS, 1, t), F32),
            pltpu.VMEM((FOX_HEADS, LANES, t), F32),
            pltpu.VMEM((ahead, tk, t), F32),
        ],
        compiler_params=_params(("parallel", "arbitrary")),
        name="fox_attn",
    )(p, qaug, p, kaug, p, p)


def _ssd_kernel(z_ref, xbc_ref, dt_ref, cw_ref, cb_ref, dtb_ref, alog_ref, dsk_ref, ng_ref, o_ref,
                xpad_ref, state_ref, y_ref):
    q = SSM_CHUNK
    halo = 8

    @pl.when(pl.program_id(1) == 0)
    def _():
        xpad_ref[0:halo, :] = jnp.zeros((halo, SSM_CONV_DIM), F32)
        state_ref[...] = jnp.zeros_like(state_ref)

    xpad_ref[halo:halo + q, :] = xbc_ref[...].astype(F32)
    y = cb_ref[...]
    for k in range(SSM_CONV):
        off = halo - (SSM_CONV - 1) + k
        y = y + cw_ref[k:k + 1, :] * xpad_ref[off:off + q, :]
    xpad_ref[0:halo, :] = xpad_ref[q:q + halo, :]
    xc = _silu(y)
    xs = xc[:, :GROUP_W]

    x_dt = dt_ref[...] + dtb_ref[...]
    dt = jnp.maximum(x_dt, 0.0) + jnp.log(1.0 + jnp.exp(-jnp.abs(x_dt)))
    a = dt * (-jnp.exp(alog_ref[...]))
    tri = _tri_lower(q)
    acs = jnp.dot(tri, a, precision=HIGHEST, preferred_element_type=F32)
    acs_t = acs.T
    er = lax.broadcasted_iota(jnp.int32, (LANES, GROUP_W), 0)
    ec = lax.broadcasted_iota(jnp.int32, (LANES, GROUP_W), 1)
    expand = (ec // HEAD_DIM == er).astype(F32)
    dt_full = jnp.dot(dt, expand, precision=HIGHEST, preferred_element_type=F32)
    eacs_full = jnp.dot(jnp.exp(acs), expand, precision=HIGHEST, preferred_element_type=F32)
    dec_full = jnp.dot(jnp.exp(acs[q - 1:q, :] - acs), expand, precision=HIGHEST, preferred_element_type=F32)
    xdt = xs * dt_full
    xdt_b = xdt.astype(BF16)
    xdec_b = (xdt * dec_full).astype(BF16)

    row = lax.broadcasted_iota(jnp.int32, (q, q), 0)
    col = lax.broadcasted_iota(jnp.int32, (q, q), 1)
    causal = row >= col
    lo = lax.broadcasted_iota(jnp.int32, (q, LANES), 1) < HEAD_DIM
    gw = GROUP_W // SSM_GROUPS
    hpg = SSM_HEADS // SSM_GROUPS
    for g in range(SSM_GROUPS):
        bm = xc[:, GROUP_W + g * SSM_STATE:GROUP_W + (g + 1) * SSM_STATE]
        cm = xc[:, GROUP_W + (SSM_GROUPS + g) * SSM_STATE:GROUP_W + (SSM_GROUPS + g + 1) * SSM_STATE]
        bm_b = bm.astype(BF16)
        cm_b = cm.astype(BF16)
        gs = slice(g * gw, (g + 1) * gw)
        cbg = lax.dot_general(cm_b, bm_b, (((1,), (1,)), ((), ())), preferred_element_type=F32)
        st = state_ref[:, gs]
        y_off = jnp.dot(cm_b, st.astype(BF16), preferred_element_type=F32) * eacs_full[:, gs]
        cst = jnp.dot(bm.T.astype(BF16), xdec_b[:, gs], preferred_element_type=F32)
        state_ref[:, gs] = st * eacs_full[q - 1:q, gs] + cst
        for hp in range(hpg // 2):
            ls = slice(g * gw + hp * LANES, g * gw + (hp + 1) * LANES)
            yd = []
            for e in range(2):
                h = g * hpg + 2 * hp + e
                seg = jnp.exp(jnp.where(causal, acs[:, h:h + 1] - acs_t[h:h + 1, :], NEG_INF))
                yd.append(jnp.dot((cbg * seg).astype(BF16), xdt_b[:, ls], preferred_element_type=F32))
            y_ref[:, ls] = jnp.where(lo, yd[0], yd[1]) + y_off[:, hp * LANES:(hp + 1) * LANES]

    yz = (y_ref[...] + xs * dsk_ref[...]) * _silu(z_ref[...].astype(F32))
    for g in range(SSM_GROUPS):
        gs = slice(g * gw, (g + 1) * gw)
        blk = yz[:, gs]
        ms = jnp.mean(blk * blk, axis=-1, keepdims=True)
        o_ref[:, gs] = ((blk * lax.rsqrt(ms + EPS)) * ng_ref[:, gs]).astype(o_ref.dtype)


def _ssd(p, ps, conv_w, conv_b, dt_bias_pad, a_log_pad, d_full, norm_g, b, s):
    q = SSM_CHUNK
    nc = s // q
    row = lambda bi, ci: (bi * nc + ci)
    const = lambda bi, ci: (0, 0)
    return pl.pallas_call(
        _ssd_kernel,
        grid=(b, nc),
        in_specs=[
            pl.BlockSpec((q, GROUP_W), lambda bi, ci: (row(bi, ci), OFF_SSM_Z // GROUP_W)),
            pl.BlockSpec((q, SSM_CONV_DIM), lambda bi, ci: (row(bi, ci), OFF_SSM_XBC // SSM_CONV_DIM)),
            pl.BlockSpec((q, LANES), lambda bi, ci: (row(bi, ci), 1)),
            pl.BlockSpec((SSM_CONV, SSM_CONV_DIM), const),
            pl.BlockSpec((1, SSM_CONV_DIM), const),
            pl.BlockSpec((1, LANES), const),
            pl.BlockSpec((1, LANES), const),
            pl.BlockSpec((1, GROUP_W), const),
            pl.BlockSpec((1, GROUP_W), const),
        ],
        out_specs=pl.BlockSpec((q, GROUP_W), lambda bi, ci: (row(bi, ci), 0)),
        out_shape=jax.ShapeDtypeStruct((b * s, GROUP_W), BF16),
        scratch_shapes=[
            pltpu.VMEM((q + 8, SSM_CONV_DIM), F32),
            pltpu.VMEM((SSM_STATE, GROUP_W), F32),
            pltpu.VMEM((q, GROUP_W), F32),
        ],
        compiler_params=_params(("parallel", "arbitrary")),
        name="ssd",
    )(p, p, ps, conv_w, conv_b, dt_bias_pad, a_log_pad, d_full, norm_g)


def _compress_kernel(r_ref, w1a_ref, w1b_ref, w2_ref, pe_ref, o_ref):
    half = CMP_STRIDE * HEAD_DIM
    r = r_ref[0, 0]
    w1a = w1a_ref[0]
    w1b = w1b_ref[0]
    a = jnp.dot(r, w1a, preferred_element_type=F32)
    bm = jnp.dot(r, w1b, preferred_element_type=F32)
    pe = pe_ref[0].astype(BF16)
    const = (jnp.dot(pe[:, :half], w1a, preferred_element_type=F32)
             + jnp.dot(pe[:, half:], w1b, preferred_element_type=F32))[0:1]
    n = bm.shape[0]
    h = _silu(a + pltpu.roll(bm, n - 1, 0) + const)
    o_ref[0, 0] = jnp.dot(h.astype(BF16), w2_ref[0], preferred_element_type=F32).astype(o_ref.dtype)


def _compress(r, w1a, w1b, w2dup, pe8):
    b, nslot, nr, width = r.shape
    return pl.pallas_call(
        _compress_kernel,
        grid=(b, nslot),
        in_specs=[
            pl.BlockSpec((1, 1, nr, width), lambda bi, si: (bi, si, 0, 0)),
            pl.BlockSpec((1, width, CMP_HIDDEN), lambda bi, si: (si // NSA_KV_HEADS, 0, 0)),
            pl.BlockSpec((1, width, CMP_HIDDEN), lambda bi, si: (si // NSA_KV_HEADS, 0, 0)),
            pl.BlockSpec((1, CMP_HIDDEN, LANES), lambda bi, si: (si // NSA_KV_HEADS, 0, 0)),
            pl.BlockSpec((1, 8, 2 * width), lambda bi, si: (si // NSA_KV_HEADS, 0, 0)),
        ],
        out_specs=pl.BlockSpec((1, 1, nr, LANES), lambda bi, si: (bi, si, 0, 0)),
        out_shape=jax.ShapeDtypeStruct((b, nslot, nr, LANES), BF16),
        compiler_params=_params(("parallel", "arbitrary")),
        name="nsa_compress",
    )(r, w1a, w1b, w2dup, pe8)


def _cmp_select_kernel(q_ref, kv_ref, gl_ref, gate_ref, fc_ref, ov_ref, o_ref, ns_ref, *, n_cmp):
    t = T_ATT
    ncp = kv_ref.shape[2]
    qi = pl.program_id(1)
    t0 = qi * t
    lo = lax.broadcasted_iota(jnp.int32, (t, LANES), 1) < HEAD_DIM
    zero = jnp.zeros((t, LANES), BF16)
    row = lax.broadcasted_iota(jnp.int32, (t, ncp), 0)
    col = lax.broadcasted_iota(jnp.int32, (t, ncp), 1)
    valid = (col * CMP_STRIDE + (CMP_BLOCK - 1) <= t0 + row) & (col < n_cmp)
    per_tile = t // CMP_STRIDE
    shift = lax.rem(qi * per_tile + (ncp - per_tile), ncp)
    beyond = col >= (qi + 1) * per_tile
    gl = gl_ref[...]

    nsel = SEL_BLOCK
    jrow = lax.broadcasted_iota(jnp.int32, (nsel, t), 0)
    cur = (t0 + lax.broadcasted_iota(jnp.int32, (nsel, t), 1)) // SEL_BLOCK
    forced = (jrow == 0) | (jrow == cur) | (jrow == cur - 1)
    past = jrow <= cur

    for g in range(NSA_KV_HEADS):
        kc = kv_ref[0, g]
        vc = kv_ref[0, NSA_KV_HEADS + g]
        psum = jnp.zeros((t, ncp), F32)
        outs = []
        for r in range(NSA_REP):
            h = g * NSA_REP + r
            qp = q_ref[:, (h // 2) * LANES:(h // 2 + 1) * LANES]
            qe = jnp.where(lo, qp, zero) if r % 2 == 0 else jnp.where(lo, zero, qp)
            s = lax.dot_general(qe, kc, (((1,), (1,)), ((), ())), preferred_element_type=F32)
            bias = jnp.where(beyond, NEG_INF, pltpu.roll(fc_ref[h], shift, 1))
            s = jnp.where(valid, s + bias, NEG_INF)
            m = jnp.max(s, axis=1, keepdims=True)
            e = jnp.where(valid, jnp.exp2(s - m), 0.0)
            pr = e / jnp.maximum(jnp.sum(e, axis=1, keepdims=True), TINY)
            psum = psum + pr
            oc = jnp.dot(pr.astype(BF16), vc, preferred_element_type=F32)
            gc = jax.nn.sigmoid(gl[:, h:h + 1])
            outs.append(oc * gc)
        for hp in range(NSA_REP // 2):
            ls = slice((g * NSA_REP // 2 + hp) * LANES, (g * NSA_REP // 2 + hp + 1) * LANES)
            o = jnp.where(lo, outs[2 * hp], outs[2 * hp + 1])
            o_ref[:, ls] = (o * _silu(gate_ref[:, ls].astype(F32))).astype(o_ref.dtype)

        imp = jnp.dot(psum, ov_ref[...], precision=HIGHEST, preferred_element_type=F32)
        imp_t = imp.T[0:nsel, :]
        imp_t = jnp.where(past, jnp.where(forced, SEL_FORCE, imp_t), -SEL_FORCE)
        rank = jnp.zeros((nsel, t), F32)
        for i in range(nsel):
            bi = imp_t[i:i + 1, :]
            gt = jnp.where(bi > imp_t, 1.0, 0.0)
            ge = jnp.where(bi >= imp_t, 1.0, 0.0)
            rank = rank + jnp.where(jrow > i, ge, gt)
        notsel = jnp.where((rank < float(SEL_TOPK)) & past, 0.0, 1.0)
        ns2 = jnp.concatenate([notsel, notsel], axis=0).T
        ns_ref[:, g * LANES:(g + 1) * LANES] = ns2.astype(ns_ref.dtype)


def _cmp_select(p, ps, kvc, fc, overlap, b, s, n_cmp):
    t = T_ATT
    nq = s // t
    w = GROUP_W
    ncp = kvc.shape[2]
    return pl.pallas_call(
        functools.partial(_cmp_select_kernel, n_cmp=n_cmp),
        grid=(b, nq),
        in_specs=[
            pl.BlockSpec((t, w), lambda bi, qi: (bi * nq + qi, OFF_NSA_Q // w)),
            pl.BlockSpec((1, 2 * NSA_KV_HEADS, ncp, LANES), lambda bi, qi: (bi, 0, 0, 0)),
            pl.BlockSpec((t, LANES), lambda bi, qi: (bi * nq + qi, 2)),
            pl.BlockSpec((t, w), lambda bi, qi: (bi * nq + qi, OFF_NSA_G // w)),
            pl.BlockSpec((NSA_HEADS, t, ncp), lambda bi, qi: (0, 0, 0)),
            pl.BlockSpec((ncp, LANES), lambda bi, qi: (0, 0)),
        ],
        out_specs=[
            pl.BlockSpec((t, w), lambda bi, qi: (bi * nq + qi, 0)),
            pl.BlockSpec((t, NSA_KV_HEADS * LANES), lambda bi, qi: (bi * nq + qi, 0)),
        ],
        out_shape=[jax.ShapeDtypeStruct((b * s, w), BF16),
                   jax.ShapeDtypeStruct((b * s, NSA_KV_HEADS * LANES), BF16)],
        compiler_params=_params(("parallel", "arbitrary")),
        name="nsa_cmp_select",
    )(p, kvc, ps, p, fc, overlap)


def _sel_attn_kernel(tab_ref, q_ref, ns_ref, k_ref, v_ref, gl_ref, gate_ref, bs_ref, o_ref,
                     qa_ref, kk_ref, vt_ref, m_ref, acc_ref, pre_ref, *, t, tk, ahead):
    qi = pl.program_id(1)
    lo_q, hi_q = _half_masks(t)
    lo = lax.broadcasted_iota(jnp.int32, (t, LANES), 1) < HEAD_DIM
    n_near = t // tk + 1

    @pl.when(qi == 0)
    def _():
        lane = lax.broadcasted_iota(jnp.int32, (tk, LANES), 1)
        krow = lax.broadcasted_iota(jnp.int32, (tk, LANES), 0)
        lo_f = jnp.where(lane < HEAD_DIM, 1.0, 0.0)
        hi_f = 1.0 - lo_f
        lo_k, hi_k = lo_f.astype(BF16), hi_f.astype(BF16)

        def merge(j, c):
            ks = pl.multiple_of(j * tk, tk)
            rows = pl.ds(ks, tk)
            hot = jnp.where((lane % HEAD_DIM) == (ks + krow) // SEL_BLOCK, NEG_INF, 0.0)
            for g in range(NSA_KV_HEADS):
                gls = slice(g * LANES, (g + 1) * LANES)
                kd, vd = k_ref[rows, gls], v_ref[rows, gls]
                kk_ref[2 * g, rows, :] = kd * lo_k + (hot * hi_f).astype(BF16)
                kk_ref[2 * g + 1, rows, :] = (hot * lo_f).astype(BF16) + kd * hi_k
                vt_ref[2 * g, :, rows] = (vd * lo_k + hi_k).astype(F32).T.astype(BF16)
                vt_ref[2 * g + 1, :, rows] = (lo_k + vd * hi_k).astype(F32).T.astype(BF16)
            return c

        lax.fori_loop(0, k_ref.shape[0] // tk, merge, 0)

    for h in range(NSA_HEADS):
        g = h // NSA_REP
        qp = q_ref[:, (h // 2) * LANES:(h // 2 + 1) * LANES]
        ns = ns_ref[:, g * LANES:(g + 1) * LANES]
        qa_ref[h] = qp * lo_q + ns * hi_q if h % 2 == 0 else ns * lo_q + qp * hi_q
    m_ref[...] = jnp.full(m_ref.shape, NEG_INF, F32)
    acc_ref[...] = jnp.zeros_like(acc_ref)

    kv = lambda h: 2 * (h // NSA_REP) + h % 2

    def rows_of(j):
        return pl.ds(pl.multiple_of(j * tk, tk), tk)

    def scores(j, h):
        return lax.dot_general(kk_ref[kv(h), rows_of(j), :], qa_ref[h], (((1,), (1,)), ((), ())),
                               preferred_element_type=F32)

    def tile(j, near, j_next):
        rows = rows_of(j)
        queue = [pre_ref[i] for i in range(ahead)]
        for h in range(NSA_HEADS):
            if h + ahead < NSA_HEADS:
                queue.append(scores(j, h + ahead))
            elif j_next is not None:
                pre_ref[h + ahead - NSA_HEADS] = scores(j_next, h + ahead - NSA_HEADS)
            s = queue.pop(0)
            m_old = m_ref[h]
            if near is None:
                far = tab_ref[(REL_BUCKETS - 1) * NSA_HEADS + h] * LOG2E
                m_new = jnp.maximum(m_old, jnp.max(s, axis=0, keepdims=True) + far)
                p = jnp.exp2(s - (m_new - far)).astype(BF16)
            else:
                s = s + bs_ref[h, near * tk:(near + 1) * tk, :]
                m_new = jnp.maximum(m_old, jnp.max(s, axis=0, keepdims=True))
                p = jnp.exp2(s - m_new).astype(BF16)
            m_ref[h] = m_new
            acc_ref[h] = jnp.exp2(m_old - m_new) * acc_ref[h] + jnp.dot(vt_ref[kv(h), :, rows], p,
                                                                        preferred_element_type=F32)

    jd = (qi * t) // tk
    for i in range(ahead):
        pre_ref[i] = scores(0, i)

    def body(j, c):
        tile(j, None, j + 1)
        return c

    lax.fori_loop(0, jnp.maximum(jd - 1, 0), body, 0)

    @pl.when(qi > 0)
    def _():
        tile(jd - 1, 0, jd)

    for dj in range(n_near - 1):
        tile(jd + dj, 1 + dj, jd + dj + 1 if dj + 2 < n_near else None)

    gl = gl_ref[...]
    for hp in range(NSA_HEADS // 2):
        ls = slice(hp * LANES, (hp + 1) * LANES)
        a0, a1 = acc_ref[2 * hp].T, acc_ref[2 * hp + 1].T
        c0 = NSA_HEADS + 2 * hp
        o = jnp.where(lo, a0 / jnp.maximum(a0[:, HEAD_DIM:HEAD_DIM + 1], TINY) * jax.nn.sigmoid(gl[:, c0:c0 + 1]),
                      a1 / jnp.maximum(a1[:, 0:1], TINY) * jax.nn.sigmoid(gl[:, c0 + 1:c0 + 2]))
        o_ref[:, ls] = (o * _silu(gate_ref[:, ls].astype(F32))).astype(o_ref.dtype)


def _sel_attn(tab_flat, p, ps, notsel, bsel, b, s, t=512, tk=256, ahead=2):
    nq = s // t
    w = GROUP_W
    kw = NSA_KV_HEADS * LANES
    n_near = t // tk + 1
    return pl.pallas_call(
        functools.partial(_sel_attn_kernel, t=t, tk=tk, ahead=ahead),
        grid=(b, nq),
        in_specs=[
            pl.BlockSpec(memory_space=pltpu.SMEM),
            pl.BlockSpec((t, w), lambda bi, qi: (bi * nq + qi, OFF_NSA_Q // w)),
            pl.BlockSpec((t, kw), lambda bi, qi: (bi * nq + qi, 0)),
            pl.BlockSpec((s, kw), lambda bi, qi: (bi, OFF_SEL_K // kw)),
            pl.BlockSpec((s, kw), lambda bi, qi: (bi, OFF_SEL_V // kw)),
            pl.BlockSpec((t, LANES), lambda bi, qi: (bi * nq + qi, 2)),
            pl.BlockSpec((t, w), lambda bi, qi: (bi * nq + qi, OFF_NSA_G // w)),
            pl.BlockSpec((NSA_HEADS, n_near * tk, t), lambda bi, qi: (0, 0, 0), pipeline_mode=pl.Buffered(1)),
        ],
        out_specs=pl.BlockSpec((t, w), lambda bi, qi: (bi * nq + qi, 0)),
        out_shape=jax.ShapeDtypeStruct((b * s, w), BF16),
        scratch_shapes=[
            pltpu.VMEM((NSA_HEADS, t, LANES), BF16),
            pltpu.VMEM((2 * NSA_KV_HEADS, s, LANES), BF16),
            pltpu.VMEM((2 * NSA_KV_HEADS, LANES, s), BF16),
            pltpu.VMEM((NSA_HEADS, 1, t), F32),
            pltpu.VMEM((NSA_HEADS, LANES, t), F32),
            pltpu.VMEM((ahead, tk, t), F32),
        ],
        compiler_params=_params(("parallel", "arbitrary")),
        name="nsa_sel_attn",
    )(tab_flat, p, notsel, p, p, ps, p, bsel)


def _win_attn_kernel(q_ref, k_ref, v_ref, gl_ref, gate_ref, bw_ref, o_ref):
    t = T_ATT
    nt = WINDOW // t + 1
    qi = pl.program_id(1)
    lo = lax.broadcasted_iota(jnp.int32, (t, LANES), 1) < HEAD_DIM
    zero = jnp.zeros((t, LANES), BF16)
    gl = gl_ref[...]

    for g in range(NSA_KV_HEADS):
        gls = slice(g * LANES, (g + 1) * LANES)
        kts, vts, pens = [], [], []
        for j in range(nt):
            kt = qi - (nt - 1) + j
            ks = pl.multiple_of(jnp.maximum(kt, 0) * t, t)
            kts.append(k_ref[pl.ds(ks, t), gls])
            vts.append(v_ref[pl.ds(ks, t), gls])
            pens.append(jnp.where(kt >= 0, 0.0, NEG_INF))
        outs = []
        for r in range(NSA_REP):
            h = g * NSA_REP + r
            qp = q_ref[:, (h // 2) * LANES:(h // 2 + 1) * LANES]
            qe = jnp.where(lo, qp, zero) if r % 2 == 0 else jnp.where(lo, zero, qp)
            ss = []
            for j in range(nt):
                s = lax.dot_general(qe, kts[j], (((1,), (1,)), ((), ())), preferred_element_type=F32)
                ss.append(s + (bw_ref[h, :, j * t:(j + 1) * t] + pens[j]))
            m = functools.reduce(jnp.maximum, [jnp.max(s, axis=1, keepdims=True) for s in ss])
            ps = [jnp.exp2(s - m) for s in ss]
            l = functools.reduce(lambda a, c: a + c, [jnp.sum(p, axis=1, keepdims=True) for p in ps])
            o = functools.reduce(lambda a, c: a + c,
                                 [jnp.dot(p.astype(BF16), v, preferred_element_type=F32) for p, v in zip(ps, vts)])
            gw = jax.nn.sigmoid(gl[:, 2 * NSA_HEADS + h:2 * NSA_HEADS + h + 1])
            outs.append(o / jnp.maximum(l, TINY) * gw)
        for hp in range(NSA_REP // 2):
            ols = slice((g * NSA_REP // 2 + hp) * LANES, (g * NSA_REP // 2 + hp + 1) * LANES)
            o = jnp.where(lo, outs[2 * hp], outs[2 * hp + 1])
            o_ref[:, ols] = (o * _silu(gate_ref[:, ols].astype(F32))).astype(o_ref.dtype)


def _win_attn(p, ps, bwin, b, s):
    t = T_ATT
    nq = s // t
    w = GROUP_W
    kw = NSA_KV_HEADS * LANES
    return pl.pallas_call(
        _win_attn_kernel,
        grid=(b, nq),
        in_specs=[
            pl.BlockSpec((t, w), lambda bi, qi: (bi * nq + qi, OFF_NSA_Q // w)),
            pl.BlockSpec((s, kw), lambda bi, qi: (bi, OFF_WIN_K // kw)),
            pl.BlockSpec((s, kw), lambda bi, qi: (bi, OFF_WIN_V // kw)),
            pl.BlockSpec((t, LANES), lambda bi, qi: (bi * nq + qi, 2)),
            pl.BlockSpec((t, w), lambda bi, qi: (bi * nq + qi, OFF_NSA_G // w)),
            pl.BlockSpec((NSA_HEADS, t, WINDOW + t), lambda bi, qi: (0, 0, 0)),
        ],
        out_specs=pl.BlockSpec((t, w), lambda bi, qi: (bi * nq + qi, 0)),
        out_shape=jax.ShapeDtypeStruct((b * s, w), BF16),
        compiler_params=_params(("parallel", "arbitrary")),
        name="nsa_win_attn",
    )(p, p, p, ps, p, bwin)


def _mem_attn_kernel(q_ref, gate_ref, kv_ref, o_ref):
    scale = MEM_HEAD_DIM ** -0.5
    for h in range(MEM_HEADS):
        ls = slice(h * LANES, (h + 1) * LANES)
        k = kv_ref[:, ls]
        v = kv_ref[:, GROUP_W + h * LANES:GROUP_W + (h + 1) * LANES]
        s = lax.dot_general(q_ref[:, ls], k, (((1,), (1,)), ((), ())), preferred_element_type=F32) * scale
        m = jnp.max(s, axis=1, keepdims=True)
        e = jnp.exp(s - m)
        l = jnp.sum(e, axis=1, keepdims=True)
        o = jnp.dot(e.astype(BF16), v, preferred_element_type=F32) / l
        o_ref[:, ls] = (o * _silu(gate_ref[:, ls].astype(F32))).astype(o_ref.dtype)


def _mem_attn(p, mem_kv, b, s, t=512):
    nq = s // t
    w = GROUP_W
    m = mem_kv.shape[0] // b
    return pl.pallas_call(
        _mem_attn_kernel,
        grid=(b, nq),
        in_specs=[
            pl.BlockSpec((t, w), lambda bi, qi: (bi * nq + qi, OFF_MEM_Q // w)),
            pl.BlockSpec((t, w), lambda bi, qi: (bi * nq + qi, OFF_MEM_G // w)),
            pl.BlockSpec((m, 2 * w), lambda bi, qi: (bi, 0)),
        ],
        out_specs=pl.BlockSpec((t, w), lambda bi, qi: (bi * nq + qi, 0)),
        out_shape=jax.ShapeDtypeStruct((b * s, w), BF16),
        compiler_params=_params(("parallel", "arbitrary")),
        name="mem_attn",
    )(p, p, mem_kv)


def _out_proj_kernel(x_ref, of_ref, os_ref, oc_ref, osel_ref, ow_ref, om_ref, w_ref, g_ref, o_ref, *, final):
    w = GROUP_W
    nsa = (oc_ref[...].astype(F32) + osel_ref[...].astype(F32) + ow_ref[...].astype(F32)).astype(BF16)
    acc = x_ref[...]
    for i, part in enumerate((of_ref[...], os_ref[...], nsa, om_ref[...])):
        acc = acc + jnp.dot(part, w_ref[i * w:(i + 1) * w, :], preferred_element_type=F32)
    if final:
        ms = jnp.mean(acc * acc, axis=-1, keepdims=True)
        acc = (acc * lax.rsqrt(ms + EPS)) * g_ref[...]
    o_ref[...] = acc


def _out_proj(x2d, parts, w_out, g, final, tm=512):
    n, d = x2d.shape
    w = GROUP_W
    part_spec = pl.BlockSpec((tm, w), lambda i: (i, 0))
    return pl.pallas_call(
        functools.partial(_out_proj_kernel, final=final),
        grid=(n // tm,),
        in_specs=[pl.BlockSpec((tm, d), lambda i: (i, 0))] + [part_spec] * 6 + [
            pl.BlockSpec((4 * w, d), lambda i: (0, 0)),
            pl.BlockSpec((1, d), lambda i: (0, 0)),
        ],
        out_specs=pl.BlockSpec((tm, d), lambda i: (i, 0)),
        out_shape=jax.ShapeDtypeStruct((n, d), F32),
        compiler_params=_params(("parallel",)),
        name="out_proj",
    )(x2d, *parts, w_out, g.reshape(1, d))


def _pack_indices():
    fox, ssm, nsa, mem = 0, FOX_COLS, FOX_COLS + SSM_COLS, FOX_COLS + SSM_COLS + NSA_COLS
    w = GROUP_W
    rng = lambda a, n: list(range(a, a + n))
    kv = lambda slot: nsa + w + slot * NSA_KV_W
    dup = lambda base: (rng(base, HEAD_DIM) * 2) + (rng(base + HEAD_DIM, HEAD_DIM) * 2)
    main = (rng(fox, 4 * w)
            + rng(ssm + w, SSM_CONV_DIM) + rng(ssm, w)
            + rng(nsa, w) + rng(nsa + w + 6 * NSA_KV_W + 3 * NSA_HEADS, w)
            + rng(mem, 2 * w)
            + dup(kv(2)) + dup(kv(3)) + dup(kv(4)) + dup(kv(5))
            + rng(kv(0), NSA_KV_W) + rng(kv(1), NSA_KV_W))
    assert len(main) == P_COLS
    scale = np.ones((P_COLS,), np.float32)
    scale[OFF_FOX_Q:OFF_FOX_Q + w] = HEAD_DIM ** -0.5 * LOG2E
    scale[OFF_NSA_Q:OFF_NSA_Q + w] = HEAD_DIM ** -0.5 * LOG2E
    small = np.zeros((PS_COLS,), np.int32)
    keep = np.zeros((PS_COLS,), np.float32)
    for blk, (src, n) in enumerate(((fox + 4 * w, FOX_HEADS), (ssm + w + SSM_CONV_DIM, SSM_HEADS),
                                    (nsa + w + 6 * NSA_KV_W, 3 * NSA_HEADS))):
        small[blk * LANES:blk * LANES + n] = np.arange(src, src + n)
        keep[blk * LANES:blk * LANES + n] = 1.0
    return np.asarray(main, np.int32), scale, small, keep


def _pad_lanes(v):
    return jnp.pad(v.astype(F32), (0, LANES - v.shape[0])).reshape(1, LANES)


def _trunk(x, mem, norm_g, w_in, fox_f_bias, ssm_conv_w, ssm_conv_b, ssm_dt_bias, ssm_a_log, ssm_d,
           ssm_norm_g, nsa_cmp_pe, nsa_cmp_w1, nsa_cmp_w2, rel_bias_table, mem_norm_g, w_mem_kv, w_out,
           final_norm_g):
    b, s, d = x.shape
    depth = w_in.shape[0]
    n = b * s
    m_tok = mem.shape[1]
    n_cmp = (s - CMP_BLOCK) // CMP_STRIDE + 1
    n_rows = s // CMP_STRIDE
    assert s % 512 == 0 and s // SEL_BLOCK <= HEAD_DIM and n_rows <= N_CMP_PAD and d == D_MODEL

    main_idx, main_scale, small_idx, small_keep = _pack_indices()
    tab_flat = rel_bias_table.astype(F32).reshape(-1)
    idx_win, idx_sel, idx_cmp = _bias_indices()
    bwin = _bias_table(tab_flat, idx_win)
    bsel = _bias_table(tab_flat, idx_sel)
    fcmp = _bias_table(tab_flat, idx_cmp)

    cs = np.arange(N_CMP_PAD)[:, None] * CMP_STRIDE
    js = np.arange(LANES)[None, :] * SEL_BLOCK
    overlap = ((cs < js + SEL_BLOCK) & (cs + CMP_BLOCK > js) & (np.arange(N_CMP_PAD)[:, None] < n_cmp)
               & (np.arange(LANES)[None, :] < s // SEL_BLOCK)).astype(np.float32)
    overlap = jnp.asarray(overlap)

    x2d = x.reshape(n, d)
    mem2d = mem.reshape(b * m_tok, d)
    half = CMP_STRIDE * HEAD_DIM
    for l in range(depth):
        w_main = (w_in[l][:, main_idx] * main_scale).astype(BF16)
        w_small = (w_in[l][:, small_idx] * small_keep).astype(BF16)
        p, ps = _norm_proj(x2d, norm_g[l], w_main, w_small)

        qaug, kaug = _fox_cumsum(ps, _pad_lanes(fox_f_bias[l]), b, s)
        o_fox = _fox_attn(p, qaug, kaug, b, s)

        o_ssd = _ssd(p, ps, ssm_conv_w[l].astype(F32), ssm_conv_b[l].reshape(1, -1).astype(F32),
                     _pad_lanes(ssm_dt_bias[l]), _pad_lanes(ssm_a_log[l]),
                     jnp.repeat(ssm_d[l].astype(F32), HEAD_DIM).reshape(1, GROUP_W),
                     ssm_norm_g[l].reshape(1, GROUP_W).astype(F32), b, s)

        kvc = p[:, OFF_CMP_KV:OFF_CMP_KV + 2 * NSA_KV_W]
        r = kvc.reshape(b, n_rows, CMP_STRIDE, 2 * NSA_KV_HEADS, HEAD_DIM).transpose(0, 3, 1, 2, 4)
        r = r.reshape(b, 2 * NSA_KV_HEADS, n_rows, half)
        if n_rows < N_CMP_PAD:
            r = jnp.pad(r, ((0, 0), (0, 0), (0, N_CMP_PAD - n_rows), (0, 0)))
        w1 = nsa_cmp_w1[l].astype(BF16)
        w2dup = jnp.concatenate([nsa_cmp_w2[l], nsa_cmp_w2[l]], axis=-1).astype(BF16)
        pe8 = jnp.broadcast_to(nsa_cmp_pe[l].astype(F32).reshape(2, 1, 2 * half), (2, 8, 2 * half))
        kv_cmp = _compress(r, w1[:, :half], w1[:, half:], w2dup, pe8)
        o_cmp, notsel = _cmp_select(p, ps, kv_cmp, fcmp, overlap, b, s, n_cmp)
        o_sel = _sel_attn(tab_flat, p, ps, notsel, bsel, b, s)
        o_win = _win_attn(p, ps, bwin, b, s)

        w_kv = w_mem_kv[l].astype(BF16)
        mem_kv, _ = _norm_proj(mem2d, mem_norm_g[l], w_kv, w_kv[:, :LANES], tm=min(512, b * m_tok))
        o_mem = _mem_attn(p, mem_kv, b, s)

        x2d = _out_proj(x2d, (o_fox, o_ssd, o_cmp, o_sel, o_win, o_mem), w_out[l].astype(BF16),
                        final_norm_g, final=(l == depth - 1))
    return x2d.reshape(b, s, d)


def kernel(x, mem, norm_g, w_in, fox_f_bias, ssm_conv_w, ssm_conv_b, ssm_dt_bias, ssm_a_log, ssm_d, ssm_norm_g,
           nsa_cmp_pe, nsa_cmp_w1, nsa_cmp_w2, rel_bias_table, mem_norm_g, w_mem_kv, w_out, final_norm_g):
    return _trunk(x, mem, norm_g, w_in, fox_f_bias, ssm_conv_w, ssm_conv_b, ssm_dt_bias, ssm_a_log, ssm_d,
                  ssm_norm_g, nsa_cmp_pe, nsa_cmp_w1, nsa_cmp_w2, rel_bias_table, mem_norm_g, w_mem_kv, w_out,
                  final_norm_g)
```

```python
import functools
import math

import numpy as np
import jax
import jax.numpy as jnp
from jax import lax
from jax.experimental import pallas as pl
from jax.experimental.pallas import tpu as pltpu

F32 = jnp.float32
BF16 = jnp.bfloat16
HIGHEST = lax.Precision.HIGHEST

D_MODEL = 1024
GROUP_W = 512
HEAD_DIM = 64
EPS = 1e-6
NEG_INF = -1e30
TINY = 1e-30
LOG2E = math.log2(math.e)

FOX_HEADS = 8
SSM_HEADS = 8
SSM_STATE = 128
SSM_GROUPS = 2
SSM_CONV = 4
SSM_CHUNK = 128
SSM_CONV_DIM = GROUP_W + 2 * SSM_GROUPS * SSM_STATE

NSA_HEADS = 8
NSA_KV_HEADS = 2
NSA_REP = NSA_HEADS // NSA_KV_HEADS
NSA_KV_W = NSA_KV_HEADS * HEAD_DIM
CMP_BLOCK = 32
CMP_STRIDE = 16
CMP_HIDDEN = 2 * HEAD_DIM
SEL_BLOCK = 64
SEL_TOPK = 16
WINDOW = 512
SEL_FORCE = 1e9

MEM_HEADS = 4
MEM_HEAD_DIM = GROUP_W // MEM_HEADS
REL_BUCKETS = 32
REL_MAX_DIST = 128

FOX_COLS = 4 * GROUP_W + FOX_HEADS
SSM_COLS = GROUP_W + SSM_CONV_DIM + SSM_HEADS
NSA_COLS = 2 * GROUP_W + 6 * NSA_KV_W + 3 * NSA_HEADS
MEM_COLS = 2 * GROUP_W

LANES = 128
VMEM_LIMIT = 56 * 1024 * 1024

OFF_FOX_Q, OFF_FOX_K, OFF_FOX_V, OFF_FOX_G = 0, 512, 1024, 1536
OFF_SSM_XBC, OFF_SSM_Z = 2048, 3072
OFF_NSA_Q, OFF_NSA_G = 3584, 4096
OFF_MEM_Q, OFF_MEM_G = 4608, 5120
OFF_SEL_K, OFF_SEL_V, OFF_WIN_K, OFF_WIN_V, OFF_CMP_KV = 5632, 5888, 6144, 6400, 6656
P_COLS = 6912
PS_COLS = 3 * LANES

T_ATT = 256
SEL_T, SEL_TK = 512, 256
WIN_T, WIN_TK = 512, 256
N_CMP_PAD = 256


def _params(sem):
    return pltpu.CompilerParams(dimension_semantics=sem, vmem_limit_bytes=VMEM_LIMIT)


def _t5_bucket_np(dist):
    n = np.maximum(dist, 0)
    max_exact = REL_BUCKETS // 2
    nf = np.maximum(n, 1).astype(np.float32)
    large = max_exact + (np.log(nf / np.float32(max_exact)) / np.float32(math.log(REL_MAX_DIST / max_exact))
                         * np.float32(REL_BUCKETS - max_exact)).astype(np.int32)
    large = np.minimum(large, REL_BUCKETS - 1)
    return np.where(n < max_exact, n, large).astype(np.int32)


def _silu(x):
    return x * jax.nn.sigmoid(x)


def _norm_proj_kernel(x_ref, g_ref, w_ref, ws_ref, p_ref, ps_ref, *, chunk):
    x = x_ref[...]
    ms = jnp.mean(x * x, axis=-1, keepdims=True)
    h = ((x * lax.rsqrt(ms + EPS)) * g_ref[...]).astype(BF16)
    ncol = p_ref.shape[1]
    for c0 in range(0, ncol, chunk):
        c1 = min(c0 + chunk, ncol)
        p_ref[:, c0:c1] = jnp.dot(h, w_ref[:, c0:c1], preferred_element_type=F32).astype(p_ref.dtype)
    ps_ref[...] = jnp.dot(h, ws_ref[...], preferred_element_type=F32)


def _norm_proj(x2d, g, w_main, w_small, tm=512):
    n, d = x2d.shape
    pc, sc = w_main.shape[1], w_small.shape[1]
    return pl.pallas_call(
        functools.partial(_norm_proj_kernel, chunk=512),
        grid=(n // tm,),
        in_specs=[
            pl.BlockSpec((tm, d), lambda i: (i, 0)),
            pl.BlockSpec((1, d), lambda i: (0, 0)),
            pl.BlockSpec((d, pc), lambda i: (0, 0)),
            pl.BlockSpec((d, sc), lambda i: (0, 0)),
        ],
        out_specs=[
            pl.BlockSpec((tm, pc), lambda i: (i, 0)),
            pl.BlockSpec((tm, sc), lambda i: (i, 0)),
        ],
        out_shape=[jax.ShapeDtypeStruct((n, pc), BF16), jax.ShapeDtypeStruct((n, sc), F32)],
        compiler_params=_params(("parallel",)),
        name="norm_proj",
    )(x2d, g.reshape(1, d), w_main, w_small)


def _bias_table_kernel(tab_ref, idx_ref, o_ref):
    h = pl.program_id(0)
    idx = idx_ref[...]
    acc = jnp.full(idx.shape, NEG_INF, F32)
    for b in range(REL_BUCKETS):
        acc = jnp.where(idx == b, tab_ref[b * NSA_HEADS + h] * LOG2E, acc)
    o_ref[0] = acc


def _bias_table(tab_flat, idx):
    r, c = idx.shape
    return pl.pallas_call(
        _bias_table_kernel,
        grid=(NSA_HEADS,),
        in_specs=[
            pl.BlockSpec(memory_space=pltpu.SMEM),
            pl.BlockSpec((r, c), lambda h: (0, 0)),
        ],
        out_specs=pl.BlockSpec((1, r, c), lambda h: (h, 0, 0)),
        out_shape=jax.ShapeDtypeStruct((NSA_HEADS, r, c), F32),
        compiler_params=_params(("arbitrary",)),
        name="t5_bias_table",
    )(tab_flat, jnp.asarray(idx))


def _bias_indices():
    t = T_ATT
    i = np.arange(t)[:, None]
    d = np.arange(WIN_T)[None, :] + WINDOW - np.arange(WINDOW + WIN_T)[:, None]
    idx_win = np.where((d >= 0) & (d < WINDOW), _t5_bucket_np(d), -1).astype(np.int32)
    kr = np.arange((SEL_T // SEL_TK + 1) * SEL_TK)[:, None] - SEL_TK
    ds = np.arange(SEL_T)[None, :] - kr
    idx_sel = np.where(ds >= 0, _t5_bucket_np(ds), -1).astype(np.int32)
    u = np.arange(2 * N_CMP_PAD)[:, None] - N_CMP_PAD
    dc = np.arange(t)[None, :] - CMP_STRIDE * u - (CMP_BLOCK - 1)
    idx_cmp = np.where(dc >= 0, _t5_bucket_np(dc), -1).astype(np.int32)
    return idx_win, idx_sel, idx_cmp


def _tri_lower(n):
    r = lax.broadcasted_iota(jnp.int32, (n, n), 0)
    c = lax.broadcasted_iota(jnp.int32, (n, n), 1)
    return (r >= c).astype(F32)


N_SPLIT = 3


def _fox_aug_lane(h, i):
    return LANES * (h // 2) + (HEAD_DIM if h % 2 == 0 else 0) + i


def _fox_aug_consts():
    pq = np.zeros((N_SPLIT * LANES, GROUP_W), np.float32)
    pk = np.zeros((N_SPLIT * LANES, GROUP_W), np.float32)
    oq = np.zeros((1, GROUP_W), np.float32)
    ok = np.zeros((1, GROUP_W), np.float32)
    for h in range(FOX_HEADS):
        for i in range(N_SPLIT):
            pq[i * LANES + h, _fox_aug_lane(h, i)] = 1.0
            pk[i * LANES + h, _fox_aug_lane(h, N_SPLIT + i)] = -1.0
            oq[0, _fox_aug_lane(h, N_SPLIT + i)] = 1.0
            ok[0, _fox_aug_lane(h, i)] = 1.0
    return pq, pk, oq, ok


def _fox_cumsum_kernel(f_ref, b_ref, pq_ref, pk_ref, oq_ref, ok_ref, qa_ref, ka_ref, carry_ref, *, ts):
    @pl.when(pl.program_id(1) == 0)
    def _():
        carry_ref[...] = jnp.zeros_like(carry_ref)

    z = f_ref[...] + b_ref[...]
    logf = (jnp.minimum(z, 0.0) - jnp.log(1.0 + jnp.exp(-jnp.abs(z)))) * LOG2E
    tri = _tri_lower(LANES)
    carry = carry_ref[...]
    for c in range(ts // LANES):
        rows = slice(c * LANES, (c + 1) * LANES)
        cs = jnp.dot(tri, logf[rows], precision=HIGHEST, preferred_element_type=F32) + carry
        carry = cs[LANES - 1:LANES, :]
        pieces, rest = [], cs
        for _ in range(N_SPLIT):
            piece = rest.astype(BF16)
            pieces.append(piece)
            rest = rest - piece.astype(F32)
        cat = jnp.concatenate(pieces, axis=1)
        qa_ref[rows, :] = (jnp.dot(cat, pq_ref[...], preferred_element_type=F32) + oq_ref[...]).astype(BF16)
        ka_ref[rows, :] = (jnp.dot(cat, pk_ref[...], preferred_element_type=F32) + ok_ref[...]).astype(BF16)
    carry_ref[...] = carry


def _fox_cumsum(ps, f_bias_pad, b, s, ts=512):
    ns = s // ts
    pq, pk, oq, ok = _fox_aug_consts()
    const = lambda bi, si: (0, 0)
    return pl.pallas_call(
        functools.partial(_fox_cumsum_kernel, ts=ts),
        grid=(b, ns),
        in_specs=[
            pl.BlockSpec((ts, LANES), lambda bi, si: (bi * ns + si, 0)),
            pl.BlockSpec((1, LANES), const),
            pl.BlockSpec(pq.shape, const),
            pl.BlockSpec(pk.shape, const),
            pl.BlockSpec(oq.shape, const),
            pl.BlockSpec(ok.shape, const),
        ],
        out_specs=[
            pl.BlockSpec((ts, GROUP_W), lambda bi, si: (bi * ns + si, 0)),
            pl.BlockSpec((ts, GROUP_W), lambda bi, si: (bi * ns + si, 0)),
        ],
        out_shape=[jax.ShapeDtypeStruct((b * s, GROUP_W), BF16), jax.ShapeDtypeStruct((b * s, GROUP_W), BF16)],
        scratch_shapes=[pltpu.VMEM((1, LANES), F32)],
        compiler_params=_params(("parallel", "arbitrary")),
        name="fox_cumsum",
    )(ps, f_bias_pad, jnp.asarray(pq, BF16), jnp.asarray(pk, BF16), jnp.asarray(oq), jnp.asarray(ok))


def _half_masks(rows):
    lo = jnp.where(lax.broadcasted_iota(jnp.int32, (rows, LANES), 1) < HEAD_DIM, 1.0, 0.0)
    return lo.astype(BF16), (1.0 - lo).astype(BF16)


def _fox_attn_kernel(q_ref, qa_ref, k_ref, ka_ref, v_ref, gate_ref, o_ref, qs_ref, kk_ref, vt_ref, m_ref, acc_ref,
                     pre_ref, *, t, tk, ahead):
    qi = pl.program_id(1)
    lo_q, hi_q = _half_masks(t)
    lo = lax.broadcasted_iota(jnp.int32, (t, LANES), 1) < HEAD_DIM
    cm = lax.broadcasted_iota(jnp.int32, (tk, t), 0) - lax.broadcasted_iota(jnp.int32, (tk, t), 1)

    @pl.when(qi == 0)
    def _():
        lo_k, hi_k = _half_masks(tk)

        def merge(j, c):
            rows = pl.ds(pl.multiple_of(j * tk, tk), tk)
            for hp in range(FOX_HEADS // 2):
                ls = slice(hp * LANES, (hp + 1) * LANES)
                kp, ka, vp = k_ref[rows, ls], ka_ref[rows, ls], v_ref[rows, ls]
                kk_ref[2 * hp, rows, :] = kp * lo_k + ka * hi_k
                kk_ref[2 * hp + 1, rows, :] = ka * lo_k + kp * hi_k
                vt_ref[2 * hp, :, rows] = (vp * lo_k + hi_k).astype(F32).T.astype(BF16)
                vt_ref[2 * hp + 1, :, rows] = (lo_k + vp * hi_k).astype(F32).T.astype(BF16)
            return c

        lax.fori_loop(0, k_ref.shape[0] // tk, merge, 0)

    for hp in range(FOX_HEADS // 2):
        ls = slice(hp * LANES, (hp + 1) * LANES)
        qp, qa = q_ref[:, ls], qa_ref[:, ls]
        qs_ref[2 * hp] = qp * lo_q + qa * hi_q
        qs_ref[2 * hp + 1] = qa * lo_q + qp * hi_q
    m_ref[...] = jnp.full(m_ref.shape, NEG_INF, F32)
    acc_ref[...] = jnp.zeros_like(acc_ref)

    def rows_of(j):
        return pl.ds(pl.multiple_of(j * tk, tk), tk)

    def scores(j, h):
        return lax.dot_general(kk_ref[h, rows_of(j), :], qs_ref[h], (((1,), (1,)), ((), ())),
                               preferred_element_type=F32)

    def tile(j, diag, j_next):
        queue = [pre_ref[i] for i in range(ahead)]
        for h in range(FOX_HEADS):
            if h + ahead < FOX_HEADS:
                queue.append(scores(j, h + ahead))
            elif j_next is not None:
                pre_ref[h + ahead - FOX_HEADS] = scores(j_next, h + ahead - FOX_HEADS)
            s = queue.pop(0)
            if diag:
                s = jnp.where(cm <= qi * t - j * tk, s, NEG_INF)
            m_old = m_ref[h]
            m_new = jnp.maximum(m_old, jnp.max(s, axis=0, keepdims=True))
            m_ref[h] = m_new
            p = jnp.exp2(s - m_new).astype(BF16)
            acc_ref[h] = jnp.exp2(m_old - m_new) * acc_ref[h] + jnp.dot(vt_ref[h, :, rows_of(j)], p,
                                                                        preferred_element_type=F32)

    jd = (qi * t) // tk
    n_diag = max(t // tk, 1)
    for i in range(ahead):
        pre_ref[i] = scores(0, i)

    def body(j, c):
        tile(j, False, j + 1)
        return c

    lax.fori_loop(0, jd, body, 0)
    for dj in range(n_diag):
        tile(jd + dj, True, jd + dj + 1 if dj + 1 < n_diag else None)

    for hp in range(FOX_HEADS // 2):
        ls = slice(hp * LANES, (hp + 1) * LANES)
        a0, a1 = acc_ref[2 * hp].T, acc_ref[2 * hp + 1].T
        o = jnp.where(lo, a0 / jnp.maximum(a0[:, HEAD_DIM:HEAD_DIM + 1], TINY),
                      a1 / jnp.maximum(a1[:, 0:1], TINY))
        o_ref[:, ls] = (o * _silu(gate_ref[:, ls].astype(F32))).astype(o_ref.dtype)


def _fox_attn(p, qaug, kaug, b, s, t=512, tk=256, ahead=2):
    nq = s // t
    w = GROUP_W
    return pl.pallas_call(
        functools.partial(_fox_attn_kernel, t=t, tk=tk, ahead=ahead),
        grid=(b, nq),
        in_specs=[
            pl.BlockSpec((t, w), lambda bi, qi: (bi * nq + qi, OFF_FOX_Q // w)),
            pl.BlockSpec((t, w), lambda bi, qi: (bi * nq + qi, 0)),
            pl.BlockSpec((s, w), lambda bi, qi: (bi, OFF_FOX_K // w)),
            pl.BlockSpec((s, w), lambda bi, qi: (bi, 0)),
            pl.BlockSpec((s, w), lambda bi, qi: (bi, OFF_FOX_V // w)),
            pl.BlockSpec((t, w), lambda bi, qi: (bi * nq + qi, OFF_FOX_G // w)),
        ],
        out_specs=pl.BlockSpec((t, w), lambda bi, qi: (bi * nq + qi, 0)),
        out_shape=jax.ShapeDtypeStruct((b * s, w), BF16),
        scratch_shapes=[
            pltpu.VMEM((FOX_HEADS, t, LANES), BF16),
            pltpu.VMEM((FOX_HEADS, s, LANES), BF16),
            pltpu.VMEM((FOX_HEADS, LANES, s), BF16),
            pltpu.VMEM((FOX_HEADS, 1, t), F32),
            pltpu.VMEM((FOX_HEADS, LANES, t), F32),
            pltpu.VMEM((ahead, tk, t), F32),
        ],
        compiler_params=_params(("parallel", "arbitrary")),
        name="fox_attn",
    )(p, qaug, p, kaug, p, p)


def _ssd_kernel(z_ref, xbc_ref, dt_ref, cw_ref, cb_ref, dtb_ref, alog_ref, dsk_ref, ng_ref, o_ref,
                xpad_ref, state_ref, y_ref):
    q = SSM_CHUNK
    halo = 8

    @pl.when(pl.program_id(1) == 0)
    def _():
        xpad_ref[0:halo, :] = jnp.zeros((halo, SSM_CONV_DIM), F32)
        state_ref[...] = jnp.zeros_like(state_ref)

    xpad_ref[halo:halo + q, :] = xbc_ref[...].astype(F32)
    y = cb_ref[...]
    for k in range(SSM_CONV):
        off = halo - (SSM_CONV - 1) + k
        y = y + cw_ref[k:k + 1, :] * xpad_ref[off:off + q, :]
    xpad_ref[0:halo, :] = xpad_ref[q:q + halo, :]
    xc = _silu(y)
    xs = xc[:, :GROUP_W]

    x_dt = dt_ref[...] + dtb_ref[...]
    dt = jnp.maximum(x_dt, 0.0) + jnp.log(1.0 + jnp.exp(-jnp.abs(x_dt)))
    a = dt * (-jnp.exp(alog_ref[...]))
    tri = _tri_lower(q)
    acs = jnp.dot(tri, a, precision=HIGHEST, preferred_element_type=F32)
    acs_t = acs.T
    er = lax.broadcasted_iota(jnp.int32, (LANES, GROUP_W), 0)
    ec = lax.broadcasted_iota(jnp.int32, (LANES, GROUP_W), 1)
    expand = (ec // HEAD_DIM == er).astype(F32)
    dt_full = jnp.dot(dt, expand, precision=HIGHEST, preferred_element_type=F32)
    eacs_full = jnp.dot(jnp.exp(acs), expand, precision=HIGHEST, preferred_element_type=F32)
    dec_full = jnp.dot(jnp.exp(acs[q - 1:q, :] - acs), expand, precision=HIGHEST, preferred_element_type=F32)
    xdt = xs * dt_full
    xdt_b = xdt.astype(BF16)
    xdec_b = (xdt * dec_full).astype(BF16)

    row = lax.broadcasted_iota(jnp.int32, (q, q), 0)
    col = lax.broadcasted_iota(jnp.int32, (q, q), 1)
    causal = row >= col
    lo = lax.broadcasted_iota(jnp.int32, (q, LANES), 1) < HEAD_DIM
    gw = GROUP_W // SSM_GROUPS
    hpg = SSM_HEADS // SSM_GROUPS
    for g in range(SSM_GROUPS):
        bm = xc[:, GROUP_W + g * SSM_STATE:GROUP_W + (g + 1) * SSM_STATE]
        cm = xc[:, GROUP_W + (SSM_GROUPS + g) * SSM_STATE:GROUP_W + (SSM_GROUPS + g + 1) * SSM_STATE]
        bm_b = bm.astype(BF16)
        cm_b = cm.astype(BF16)
        gs = slice(g * gw, (g + 1) * gw)
        cbg = lax.dot_general(cm_b, bm_b, (((1,), (1,)), ((), ())), preferred_element_type=F32)
        st = state_ref[:, gs]
        y_off = jnp.dot(cm_b, st.astype(BF16), preferred_element_type=F32) * eacs_full[:, gs]
        cst = jnp.dot(bm.T.astype(BF16), xdec_b[:, gs], preferred_element_type=F32)
        state_ref[:, gs] = st * eacs_full[q - 1:q, gs] + cst
        for hp in range(hpg // 2):
            ls = slice(g * gw + hp * LANES, g * gw + (hp + 1) * LANES)
            yd = []
            for e in range(2):
                h = g * hpg + 2 * hp + e
                seg = jnp.exp(jnp.where(causal, acs[:, h:h + 1] - acs_t[h:h + 1, :], NEG_INF))
                yd.append(jnp.dot((cbg * seg).astype(BF16), xdt_b[:, ls], preferred_element_type=F32))
            y_ref[:, ls] = jnp.where(lo, yd[0], yd[1]) + y_off[:, hp * LANES:(hp + 1) * LANES]

    yz = (y_ref[...] + xs * dsk_ref[...]) * _silu(z_ref[...].astype(F32))
    for g in range(SSM_GROUPS):
        gs = slice(g * gw, (g + 1) * gw)
        blk = yz[:, gs]
        ms = jnp.mean(blk * blk, axis=-1, keepdims=True)
        o_ref[:, gs] = ((blk * lax.rsqrt(ms + EPS)) * ng_ref[:, gs]).astype(o_ref.dtype)


def _ssd(p, ps, conv_w, conv_b, dt_bias_pad, a_log_pad, d_full, norm_g, b, s):
    q = SSM_CHUNK
    nc = s // q
    row = lambda bi, ci: (bi * nc + ci)
    const = lambda bi, ci: (0, 0)
    return pl.pallas_call(
        _ssd_kernel,
        grid=(b, nc),
        in_specs=[
            pl.BlockSpec((q, GROUP_W), lambda bi, ci: (row(bi, ci), OFF_SSM_Z // GROUP_W)),
            pl.BlockSpec((q, SSM_CONV_DIM), lambda bi, ci: (row(bi, ci), OFF_SSM_XBC // SSM_CONV_DIM)),
            pl.BlockSpec((q, LANES), lambda bi, ci: (row(bi, ci), 1)),
            pl.BlockSpec((SSM_CONV, SSM_CONV_DIM), const),
            pl.BlockSpec((1, SSM_CONV_DIM), const),
            pl.BlockSpec((1, LANES), const),
            pl.BlockSpec((1, LANES), const),
            pl.BlockSpec((1, GROUP_W), const),
            pl.BlockSpec((1, GROUP_W), const),
        ],
        out_specs=pl.BlockSpec((q, GROUP_W), lambda bi, ci: (row(bi, ci), 0)),
        out_shape=jax.ShapeDtypeStruct((b * s, GROUP_W), BF16),
        scratch_shapes=[
            pltpu.VMEM((q + 8, SSM_CONV_DIM), F32),
            pltpu.VMEM((SSM_STATE, GROUP_W), F32),
            pltpu.VMEM((q, GROUP_W), F32),
        ],
        compiler_params=_params(("parallel", "arbitrary")),
        name="ssd",
    )(p, p, ps, conv_w, conv_b, dt_bias_pad, a_log_pad, d_full, norm_g)


def _compress_kernel(r_ref, w1a_ref, w1b_ref, w2_ref, pe_ref, o_ref, ot_ref):
    half = CMP_STRIDE * HEAD_DIM
    r = r_ref[0, 0]
    w1a = w1a_ref[0]
    w1b = w1b_ref[0]
    a = jnp.dot(r, w1a, preferred_element_type=F32)
    bm = jnp.dot(r, w1b, preferred_element_type=F32)
    pe = pe_ref[0].astype(BF16)
    const = (jnp.dot(pe[:, :half], w1a, preferred_element_type=F32)
             + jnp.dot(pe[:, half:], w1b, preferred_element_type=F32))[0:1]
    n = bm.shape[0]
    h = _silu(a + pltpu.roll(bm, n - 1, 0) + const)
    o = jnp.dot(h.astype(BF16), w2_ref[0], preferred_element_type=F32)
    o_ref[0, 0] = o.astype(o_ref.dtype)
    ot_ref[0, 0] = o.T.astype(ot_ref.dtype)


def _compress(r, w1a, w1b, w2dup, pe8):
    b, nslot, nr, width = r.shape
    return pl.pallas_call(
        _compress_kernel,
        grid=(b, nslot),
        in_specs=[
            pl.BlockSpec((1, 1, nr, width), lambda bi, si: (bi, si, 0, 0)),
            pl.BlockSpec((1, width, CMP_HIDDEN), lambda bi, si: (si // NSA_KV_HEADS, 0, 0)),
            pl.BlockSpec((1, width, CMP_HIDDEN), lambda bi, si: (si // NSA_KV_HEADS, 0, 0)),
            pl.BlockSpec((1, CMP_HIDDEN, LANES), lambda bi, si: (si // NSA_KV_HEADS, 0, 0)),
            pl.BlockSpec((1, 8, 2 * width), lambda bi, si: (si // NSA_KV_HEADS, 0, 0)),
        ],
        out_specs=[pl.BlockSpec((1, 1, nr, LANES), lambda bi, si: (bi, si, 0, 0)),
                   pl.BlockSpec((1, 1, LANES, nr), lambda bi, si: (bi, si, 0, 0))],
        out_shape=[jax.ShapeDtypeStruct((b, nslot, nr, LANES), BF16),
                   jax.ShapeDtypeStruct((b, nslot, LANES, nr), BF16)],
        compiler_params=_params(("parallel", "arbitrary")),
        name="nsa_compress",
    )(r, w1a, w1b, w2dup, pe8)


def _cmp_select_kernel(q_ref, kv_ref, vt_ref, gl_ref, gate_ref, fc_ref, ovt_ref, o_ref, ns_ref):
    t = T_ATT
    ncp = kv_ref.shape[2]
    qi = pl.program_id(1)
    t0 = qi * t
    lo = lax.broadcasted_iota(jnp.int32, (t, LANES), 1) < HEAD_DIM
    zero = jnp.zeros((t, LANES), BF16)
    start = pl.multiple_of(ncp - qi * (t // CMP_STRIDE), CMP_STRIDE)
    gl = gl_ref[...]

    nsel = SEL_BLOCK
    jrow = lax.broadcasted_iota(jnp.int32, (nsel, t), 0)
    cur = (t0 + lax.broadcasted_iota(jnp.int32, (nsel, t), 1)) // SEL_BLOCK
    forced = (jrow == 0) | (jrow == cur) | (jrow == cur - 1)
    past = jrow <= cur

    for g in range(NSA_KV_HEADS):
        kc = kv_ref[0, g]
        vct = vt_ref[0, NSA_KV_HEADS + g]
        psum = jnp.zeros((ncp, t), F32)
        outs = []
        for r in range(NSA_REP):
            h = g * NSA_REP + r
            qp = q_ref[:, (h // 2) * LANES:(h // 2 + 1) * LANES]
            qe = jnp.where(lo, qp, zero) if r % 2 == 0 else jnp.where(lo, zero, qp)
            s = lax.dot_general(kc, qe, (((1,), (1,)), ((), ())), preferred_element_type=F32)
            s = fc_ref[h, pl.ds(start, ncp), :] + s
            m = jnp.maximum(jnp.max(s, axis=0, keepdims=True), 0.1 * NEG_INF)
            e = jnp.exp2(s - m)
            pr = e * (1.0 / jnp.maximum(jnp.sum(e, axis=0, keepdims=True), TINY))
            psum = psum + pr
            oc = jnp.dot(vct, pr.astype(BF16), preferred_element_type=F32)
            outs.append(oc.T * jax.nn.sigmoid(gl[:, h:h + 1]))
        for hp in range(NSA_REP // 2):
            ls = slice((g * NSA_REP // 2 + hp) * LANES, (g * NSA_REP // 2 + hp + 1) * LANES)
            o = jnp.where(lo, outs[2 * hp], outs[2 * hp + 1])
            o_ref[:, ls] = (o * _silu(gate_ref[:, ls].astype(F32))).astype(o_ref.dtype)

        imp_t = jnp.zeros((nsel, t), F32)
        rest = psum
        for _ in range(N_SPLIT):
            piece = rest.astype(BF16)
            rest = rest - piece.astype(F32)
            imp_t = imp_t + jnp.dot(ovt_ref[...], piece, preferred_element_type=F32)
        imp_t = jnp.where(past, jnp.where(forced, SEL_FORCE, imp_t), -SEL_FORCE)
        sub = 8
        sub_row = lax.broadcasted_iota(jnp.int32, (sub, t), 0)
        rows = [imp_t[k * sub:(k + 1) * sub] for k in range(nsel // sub)]
        rank = [jnp.zeros((sub, t), F32) for _ in rows]
        for i in range(nsel):
            bi = imp_t[i:i + 1, :]
            for k, x in enumerate(rows):
                if k * sub > i:
                    ahead = jnp.where(bi >= x, 1.0, 0.0)
                elif k * sub + sub - 1 <= i:
                    ahead = jnp.where(bi > x, 1.0, 0.0)
                else:
                    ahead = jnp.where(sub_row > i - k * sub, jnp.where(bi >= x, 1.0, 0.0),
                                      jnp.where(bi > x, 1.0, 0.0))
                rank[k] = rank[k] + ahead
        rank = jnp.concatenate(rank, axis=0)
        notsel = jnp.where((rank < float(SEL_TOPK)) & past, 0.0, 1.0)
        ns2 = jnp.concatenate([notsel, notsel], axis=0).T
        ns_ref[:, g * LANES:(g + 1) * LANES] = ns2.astype(ns_ref.dtype)


def _cmp_select(p, ps, kvc, kvc_t, fc, overlap_t, b, s):
    t = T_ATT
    nq = s // t
    w = GROUP_W
    ncp = kvc.shape[2]
    return pl.pallas_call(
        _cmp_select_kernel,
        grid=(b, nq),
        in_specs=[
            pl.BlockSpec((t, w), lambda bi, qi: (bi * nq + qi, OFF_NSA_Q // w)),
            pl.BlockSpec((1, 2 * NSA_KV_HEADS, ncp, LANES), lambda bi, qi: (bi, 0, 0, 0)),
            pl.BlockSpec((1, 2 * NSA_KV_HEADS, LANES, ncp), lambda bi, qi: (bi, 0, 0, 0)),
            pl.BlockSpec((t, LANES), lambda bi, qi: (bi * nq + qi, 2)),
            pl.BlockSpec((t, w), lambda bi, qi: (bi * nq + qi, OFF_NSA_G // w)),
            pl.BlockSpec((NSA_HEADS, 2 * ncp, t), lambda bi, qi: (0, 0, 0)),
            pl.BlockSpec((SEL_BLOCK, ncp), lambda bi, qi: (0, 0)),
        ],
        out_specs=[
            pl.BlockSpec((t, w), lambda bi, qi: (bi * nq + qi, 0)),
            pl.BlockSpec((t, NSA_KV_HEADS * LANES), lambda bi, qi: (bi * nq + qi, 0)),
        ],
        out_shape=[jax.ShapeDtypeStruct((b * s, w), BF16),
                   jax.ShapeDtypeStruct((b * s, NSA_KV_HEADS * LANES), BF16)],
        compiler_params=_params(("parallel", "arbitrary")),
        name="nsa_cmp_select",
    )(p, kvc, kvc_t, ps, p, fc, overlap_t)


def _sel_attn_kernel(tab_ref, q_ref, ns_ref, k_ref, v_ref, gl_ref, gate_ref, bs_ref, o_ref,
                     qa_ref, kk_ref, vt_ref, m_ref, acc_ref, pre_ref, *, t, tk, ahead):
    qi = pl.program_id(1)
    lo_q, hi_q = _half_masks(t)
    lo = lax.broadcasted_iota(jnp.int32, (t, LANES), 1) < HEAD_DIM
    n_near = t // tk + 1

    @pl.when(qi == 0)
    def _():
        lane = lax.broadcasted_iota(jnp.int32, (tk, LANES), 1)
        krow = lax.broadcasted_iota(jnp.int32, (tk, LANES), 0)
        lo_f = jnp.where(lane < HEAD_DIM, 1.0, 0.0)
        hi_f = 1.0 - lo_f
        lo_k, hi_k = lo_f.astype(BF16), hi_f.astype(BF16)

        def merge(j, c):
            ks = pl.multiple_of(j * tk, tk)
            rows = pl.ds(ks, tk)
            hot = jnp.where((lane % HEAD_DIM) == (ks + krow) // SEL_BLOCK, NEG_INF, 0.0)
            for g in range(NSA_KV_HEADS):
                gls = slice(g * LANES, (g + 1) * LANES)
                kd, vd = k_ref[rows, gls], v_ref[rows, gls]
                kk_ref[2 * g, rows, :] = kd * lo_k + (hot * hi_f).astype(BF16)
                kk_ref[2 * g + 1, rows, :] = (hot * lo_f).astype(BF16) + kd * hi_k
                vt_ref[2 * g, :, rows] = (vd * lo_k + hi_k).astype(F32).T.astype(BF16)
                vt_ref[2 * g + 1, :, rows] = (lo_k + vd * hi_k).astype(F32).T.astype(BF16)
            return c

        lax.fori_loop(0, k_ref.shape[0] // tk, merge, 0)

    for h in range(NSA_HEADS):
        g = h // NSA_REP
        qp = q_ref[:, (h // 2) * LANES:(h // 2 + 1) * LANES]
        ns = ns_ref[:, g * LANES:(g + 1) * LANES]
        qa_ref[h] = qp * lo_q + ns * hi_q if h % 2 == 0 else ns * lo_q + qp * hi_q
    m_ref[...] = jnp.full(m_ref.shape, NEG_INF, F32)
    acc_ref[...] = jnp.zeros_like(acc_ref)

    kv = lambda h: 2 * (h // NSA_REP) + h % 2

    def rows_of(j):
        return pl.ds(pl.multiple_of(j * tk, tk), tk)

    def scores(j, h):
        return lax.dot_general(kk_ref[kv(h), rows_of(j), :], qa_ref[h], (((1,), (1,)), ((), ())),
                               preferred_element_type=F32)

    def tile(j, near, j_next):
        rows = rows_of(j)
        queue = [pre_ref[i] for i in range(ahead)]
        for h in range(NSA_HEADS):
            if h + ahead < NSA_HEADS:
                queue.append(scores(j, h + ahead))
            elif j_next is not None:
                pre_ref[h + ahead - NSA_HEADS] = scores(j_next, h + ahead - NSA_HEADS)
            s = queue.pop(0)
            m_old = m_ref[h]
            if near is None:
                far = tab_ref[(REL_BUCKETS - 1) * NSA_HEADS + h] * LOG2E
                m_new = jnp.maximum(m_old, jnp.max(s, axis=0, keepdims=True) + far)
                p = jnp.exp2(s - (m_new - far)).astype(BF16)
            else:
                s = bs_ref[h, near * tk:(near + 1) * tk, :] + s
                m_new = jnp.maximum(m_old, jnp.max(s, axis=0, keepdims=True))
                p = jnp.exp2(s - m_new).astype(BF16)
            m_ref[h] = m_new
            acc_ref[h] = jnp.exp2(m_old - m_new) * acc_ref[h] + jnp.dot(vt_ref[kv(h), :, rows], p,
                                                                        preferred_element_type=F32)

    jd = (qi * t) // tk
    for i in range(ahead):
        pre_ref[i] = scores(0, i)

    def body(j, c):
        tile(j, None, j + 1)
        return c

    lax.fori_loop(0, jnp.maximum(jd - 1, 0), body, 0)

    @pl.when(qi > 0)
    def _():
        tile(jd - 1, 0, jd)

    for dj in range(n_near - 1):
        tile(jd + dj, 1 + dj, jd + dj + 1 if dj + 2 < n_near else None)

    gl = gl_ref[...]
    for hp in range(NSA_HEADS // 2):
        ls = slice(hp * LANES, (hp + 1) * LANES)
        a0, a1 = acc_ref[2 * hp].T, acc_ref[2 * hp + 1].T
        c0 = NSA_HEADS + 2 * hp
        o = jnp.where(lo, a0 / jnp.maximum(a0[:, HEAD_DIM:HEAD_DIM + 1], TINY) * jax.nn.sigmoid(gl[:, c0:c0 + 1]),
                      a1 / jnp.maximum(a1[:, 0:1], TINY) * jax.nn.sigmoid(gl[:, c0 + 1:c0 + 2]))
        o_ref[:, ls] = (o * _silu(gate_ref[:, ls].astype(F32))).astype(o_ref.dtype)


def _sel_attn(tab_flat, p, ps, notsel, bsel, b, s, t=512, tk=256, ahead=2):
    nq = s // t
    w = GROUP_W
    kw = NSA_KV_HEADS * LANES
    n_near = t // tk + 1
    return pl.pallas_call(
        functools.partial(_sel_attn_kernel, t=t, tk=tk, ahead=ahead),
        grid=(b, nq),
        in_specs=[
            pl.BlockSpec(memory_space=pltpu.SMEM),
            pl.BlockSpec((t, w), lambda bi, qi: (bi * nq + qi, OFF_NSA_Q // w)),
            pl.BlockSpec((t, kw), lambda bi, qi: (bi * nq + qi, 0)),
            pl.BlockSpec((s, kw), lambda bi, qi: (bi, OFF_SEL_K // kw)),
            pl.BlockSpec((s, kw), lambda bi, qi: (bi, OFF_SEL_V // kw)),
            pl.BlockSpec((t, LANES), lambda bi, qi: (bi * nq + qi, 2)),
            pl.BlockSpec((t, w), lambda bi, qi: (bi * nq + qi, OFF_NSA_G // w)),
            pl.BlockSpec((NSA_HEADS, n_near * tk, t), lambda bi, qi: (0, 0, 0), pipeline_mode=pl.Buffered(1)),
        ],
        out_specs=pl.BlockSpec((t, w), lambda bi, qi: (bi * nq + qi, 0)),
        out_shape=jax.ShapeDtypeStruct((b * s, w), BF16),
        scratch_shapes=[
            pltpu.VMEM((NSA_HEADS, t, LANES), BF16),
            pltpu.VMEM((2 * NSA_KV_HEADS, s, LANES), BF16),
            pltpu.VMEM((2 * NSA_KV_HEADS, LANES, s), BF16),
            pltpu.VMEM((NSA_HEADS, 1, t), F32),
            pltpu.VMEM((NSA_HEADS, LANES, t), F32),
            pltpu.VMEM((ahead, tk, t), F32),
        ],
        compiler_params=_params(("parallel", "arbitrary")),
        name="nsa_sel_attn",
    )(tab_flat, p, notsel, p, p, ps, p, bsel)


def _win_attn_kernel(q_ref, k_ref, v_ref, gl_ref, gate_ref, bw_ref, o_ref, qe_ref, vt_ref, m_ref, acc_ref,
                     *, t, tk, ahead):
    qi = pl.program_id(1)
    lo_q, hi_q = _half_masks(t)
    lo = lax.broadcasted_iota(jnp.int32, (t, LANES), 1) < HEAD_DIM
    n_before = WINDOW // tk
    n_tiles = n_before + t // tk

    @pl.when(qi == 0)
    def _():
        lo_k, hi_k = _half_masks(tk)

        def merge(j, c):
            rows = pl.ds(pl.multiple_of(j * tk, tk), tk)
            for g in range(NSA_KV_HEADS):
                vd = v_ref[rows, g * LANES:(g + 1) * LANES]
                vt_ref[2 * g, :, rows] = (vd * lo_k + hi_k).astype(F32).T.astype(BF16)
                vt_ref[2 * g + 1, :, rows] = (lo_k + vd * hi_k).astype(F32).T.astype(BF16)
            return c

        lax.fori_loop(0, v_ref.shape[0] // tk, merge, 0)

    for h in range(NSA_HEADS):
        qp = q_ref[:, (h // 2) * LANES:(h // 2 + 1) * LANES]
        qe_ref[h] = qp * lo_q if h % 2 == 0 else qp * hi_q
    m_ref[...] = jnp.full(m_ref.shape, NEG_INF, F32)
    acc_ref[...] = jnp.zeros_like(acc_ref)

    def tile(w):
        rows = pl.ds(pl.multiple_of((qi * (t // tk) - n_before + w) * tk, tk), tk)

        def scores(h):
            g = h // NSA_REP
            return lax.dot_general(k_ref[rows, g * LANES:(g + 1) * LANES], qe_ref[h], (((1,), (1,)), ((), ())),
                                   preferred_element_type=F32)

        queue = [scores(h) for h in range(ahead)]
        for h in range(NSA_HEADS):
            if h + ahead < NSA_HEADS:
                queue.append(scores(h + ahead))
            s = bw_ref[h, w * tk:(w + 1) * tk, :] + queue.pop(0)
            m_old = m_ref[h]
            m_new = jnp.maximum(m_old, jnp.max(s, axis=0, keepdims=True))
            m_ref[h] = m_new
            p = jnp.exp2(s - m_new).astype(BF16)
            acc_ref[h] = jnp.exp2(m_old - m_new) * acc_ref[h] + jnp.dot(
                vt_ref[2 * (h // NSA_REP) + h % 2, :, rows], p, preferred_element_type=F32)

    @pl.when(qi > 0)
    def _():
        for w in range(n_before):
            tile(w)

    for w in range(n_before, n_tiles):
        tile(w)

    gl = gl_ref[...]
    for hp in range(NSA_HEADS // 2):
        ls = slice(hp * LANES, (hp + 1) * LANES)
        a0, a1 = acc_ref[2 * hp].T, acc_ref[2 * hp + 1].T
        c0 = 2 * NSA_HEADS + 2 * hp
        o = jnp.where(lo, a0 / jnp.maximum(a0[:, HEAD_DIM:HEAD_DIM + 1], TINY) * jax.nn.sigmoid(gl[:, c0:c0 + 1]),
                      a1 / jnp.maximum(a1[:, 0:1], TINY) * jax.nn.sigmoid(gl[:, c0 + 1:c0 + 2]))
        o_ref[:, ls] = (o * _silu(gate_ref[:, ls].astype(F32))).astype(o_ref.dtype)


def _win_attn(p, ps, bwin, b, s, t=WIN_T, tk=WIN_TK, ahead=2):
    assert t % WINDOW == 0 and WINDOW % tk == 0
    nq = s // t
    w = GROUP_W
    kw = NSA_KV_HEADS * LANES
    return pl.pallas_call(
        functools.partial(_win_attn_kernel, t=t, tk=tk, ahead=ahead),
        grid=(b, nq),
        in_specs=[
            pl.BlockSpec((t, w), lambda bi, qi: (bi * nq + qi, OFF_NSA_Q // w)),
            pl.BlockSpec((s, kw), lambda bi, qi: (bi, OFF_WIN_K // kw)),
            pl.BlockSpec((s, kw), lambda bi, qi: (bi, OFF_WIN_V // kw)),
            pl.BlockSpec((t, LANES), lambda bi, qi: (bi * nq + qi, 2)),
            pl.BlockSpec((t, w), lambda bi, qi: (bi * nq + qi, OFF_NSA_G // w)),
            pl.BlockSpec((NSA_HEADS, WINDOW + t, t), lambda bi, qi: (0, 0, 0), pipeline_mode=pl.Buffered(1)),
        ],
        out_specs=pl.BlockSpec((t, w), lambda bi, qi: (bi * nq + qi, 0)),
        out_shape=jax.ShapeDtypeStruct((b * s, w), BF16),
        scratch_shapes=[
            pltpu.VMEM((NSA_HEADS, t, LANES), BF16),
            pltpu.VMEM((2 * NSA_KV_HEADS, LANES, s), BF16),
            pltpu.VMEM((NSA_HEADS, 1, t), F32),
            pltpu.VMEM((NSA_HEADS, LANES, t), F32),
        ],
        compiler_params=_params(("parallel", "arbitrary")),
        name="nsa_win_attn",
    )(p, p, p, ps, p, bwin)


def _mem_attn_kernel(q_ref, gate_ref, kv_ref, o_ref):
    scale = MEM_HEAD_DIM ** -0.5
    for h in range(MEM_HEADS):
        ls = slice(h * LANES, (h + 1) * LANES)
        k = kv_ref[:, ls]
        v = kv_ref[:, GROUP_W + h * LANES:GROUP_W + (h + 1) * LANES]
        s = lax.dot_general(q_ref[:, ls], k, (((1,), (1,)), ((), ())), preferred_element_type=F32) * scale
        m = jnp.max(s, axis=1, keepdims=True)
        e = jnp.exp(s - m)
        l = jnp.sum(e, axis=1, keepdims=True)
        o = jnp.dot(e.astype(BF16), v, preferred_element_type=F32) / l
        o_ref[:, ls] = (o * _silu(gate_ref[:, ls].astype(F32))).astype(o_ref.dtype)


def _mem_attn(p, mem_kv, b, s, t=512):
    nq = s // t
    w = GROUP_W
    m = mem_kv.shape[0] // b
    return pl.pallas_call(
        _mem_attn_kernel,
        grid=(b, nq),
        in_specs=[
            pl.BlockSpec((t, w), lambda bi, qi: (bi * nq + qi, OFF_MEM_Q // w)),
            pl.BlockSpec((t, w), lambda bi, qi: (bi * nq + qi, OFF_MEM_G // w)),
            pl.BlockSpec((m, 2 * w), lambda bi, qi: (bi, 0)),
        ],
        out_specs=pl.BlockSpec((t, w), lambda bi, qi: (bi * nq + qi, 0)),
        out_shape=jax.ShapeDtypeStruct((b * s, w), BF16),
        compiler_params=_params(("parallel", "arbitrary")),
        name="mem_attn",
    )(p, p, mem_kv)


def _out_proj_kernel(x_ref, of_ref, os_ref, oc_ref, osel_ref, ow_ref, om_ref, w_ref, g_ref, o_ref, *, final):
    w = GROUP_W
    nsa = (oc_ref[...].astype(F32) + osel_ref[...].astype(F32) + ow_ref[...].astype(F32)).astype(BF16)
    acc = x_ref[...]
    for i, part in enumerate((of_ref[...], os_ref[...], nsa, om_ref[...])):
        acc = acc + jnp.dot(part, w_ref[i * w:(i + 1) * w, :], preferred_element_type=F32)
    if final:
        ms = jnp.mean(acc * acc, axis=-1, keepdims=True)
        acc = (acc * lax.rsqrt(ms + EPS)) * g_ref[...]
    o_ref[...] = acc


def _out_proj(x2d, parts, w_out, g, final, tm=512):
    n, d = x2d.shape
    w = GROUP_W
    part_spec = pl.BlockSpec((tm, w), lambda i: (i, 0))
    return pl.pallas_call(
        functools.partial(_out_proj_kernel, final=final),
        grid=(n // tm,),
        in_specs=[pl.BlockSpec((tm, d), lambda i: (i, 0))] + [part_spec] * 6 + [
            pl.BlockSpec((4 * w, d), lambda i: (0, 0)),
            pl.BlockSpec((1, d), lambda i: (0, 0)),
        ],
        out_specs=pl.BlockSpec((tm, d), lambda i: (i, 0)),
        out_shape=jax.ShapeDtypeStruct((n, d), F32),
        compiler_params=_params(("parallel",)),
        name="out_proj",
    )(x2d, *parts, w_out, g.reshape(1, d))


def _pack_indices():
    fox, ssm, nsa, mem = 0, FOX_COLS, FOX_COLS + SSM_COLS, FOX_COLS + SSM_COLS + NSA_COLS
    w = GROUP_W
    rng = lambda a, n: list(range(a, a + n))
    kv = lambda slot: nsa + w + slot * NSA_KV_W
    dup = lambda base: (rng(base, HEAD_DIM) * 2) + (rng(base + HEAD_DIM, HEAD_DIM) * 2)
    main = (rng(fox, 4 * w)
            + rng(ssm + w, SSM_CONV_DIM) + rng(ssm, w)
            + rng(nsa, w) + rng(nsa + w + 6 * NSA_KV_W + 3 * NSA_HEADS, w)
            + rng(mem, 2 * w)
            + dup(kv(2)) + dup(kv(3)) + dup(kv(4)) + dup(kv(5))
            + rng(kv(0), NSA_KV_W) + rng(kv(1), NSA_KV_W))
    assert len(main) == P_COLS
    scale = np.ones((P_COLS,), np.float32)
    scale[OFF_FOX_Q:OFF_FOX_Q + w] = HEAD_DIM ** -0.5 * LOG2E
    scale[OFF_NSA_Q:OFF_NSA_Q + w] = HEAD_DIM ** -0.5 * LOG2E
    small = np.zeros((PS_COLS,), np.int32)
    keep = np.zeros((PS_COLS,), np.float32)
    for blk, (src, n) in enumerate(((fox + 4 * w, FOX_HEADS), (ssm + w + SSM_CONV_DIM, SSM_HEADS),
                                    (nsa + w + 6 * NSA_KV_W, 3 * NSA_HEADS))):
        small[blk * LANES:blk * LANES + n] = np.arange(src, src + n)
        keep[blk * LANES:blk * LANES + n] = 1.0
    return np.asarray(main, np.int32), scale, small, keep


def _pad_lanes(v):
    return jnp.pad(v.astype(F32), (0, LANES - v.shape[0])).reshape(1, LANES)


def _trunk(x, mem, norm_g, w_in, fox_f_bias, ssm_conv_w, ssm_conv_b, ssm_dt_bias, ssm_a_log, ssm_d,
           ssm_norm_g, nsa_cmp_pe, nsa_cmp_w1, nsa_cmp_w2, rel_bias_table, mem_norm_g, w_mem_kv, w_out,
           final_norm_g):
    b, s, d = x.shape
    depth = w_in.shape[0]
    n = b * s
    m_tok = mem.shape[1]
    n_cmp = (s - CMP_BLOCK) // CMP_STRIDE + 1
    n_rows = s // CMP_STRIDE
    assert s % 512 == 0 and s // SEL_BLOCK <= HEAD_DIM and n_rows <= N_CMP_PAD and d == D_MODEL

    main_idx, main_scale, small_idx, small_keep = _pack_indices()
    tab_flat = rel_bias_table.astype(F32).reshape(-1)
    idx_win, idx_sel, idx_cmp = _bias_indices()
    bwin = _bias_table(tab_flat, idx_win)
    bsel = _bias_table(tab_flat, idx_sel)
    fcmp = _bias_table(tab_flat, idx_cmp)

    cs = np.arange(N_CMP_PAD)[None, :] * CMP_STRIDE
    js = np.arange(SEL_BLOCK)[:, None] * SEL_BLOCK
    overlap_t = ((cs < js + SEL_BLOCK) & (cs + CMP_BLOCK > js) & (np.arange(N_CMP_PAD)[None, :] < n_cmp)
                 & (np.arange(SEL_BLOCK)[:, None] < s // SEL_BLOCK)).astype(np.float32)
    overlap_t = jnp.asarray(overlap_t, BF16)

    x2d = x.reshape(n, d)
    mem2d = mem.reshape(b * m_tok, d)
    half = CMP_STRIDE * HEAD_DIM
    for l in range(depth):
        w_main = (w_in[l][:, main_idx] * main_scale).astype(BF16)
        w_small = (w_in[l][:, small_idx] * small_keep).astype(BF16)
        p, ps = _norm_proj(x2d, norm_g[l], w_main, w_small)

        qaug, kaug = _fox_cumsum(ps, _pad_lanes(fox_f_bias[l]), b, s)
        o_fox = _fox_attn(p, qaug, kaug, b, s)

        o_ssd = _ssd(p, ps, ssm_conv_w[l].astype(F32), ssm_conv_b[l].reshape(1, -1).astype(F32),
                     _pad_lanes(ssm_dt_bias[l]), _pad_lanes(ssm_a_log[l]),
                     jnp.repeat(ssm_d[l].astype(F32), HEAD_DIM).reshape(1, GROUP_W),
                     ssm_norm_g[l].reshape(1, GROUP_W).astype(F32), b, s)

        kvc = p[:, OFF_CMP_KV:OFF_CMP_KV + 2 * NSA_KV_W]
        r = kvc.reshape(b, n_rows, CMP_STRIDE, 2 * NSA_KV_HEADS, HEAD_DIM).transpose(0, 3, 1, 2, 4)
        r = r.reshape(b, 2 * NSA_KV_HEADS, n_rows, half)
        if n_rows < N_CMP_PAD:
            r = jnp.pad(r, ((0, 0), (0, 0), (0, N_CMP_PAD - n_rows), (0, 0)))
        w1 = nsa_cmp_w1[l].astype(BF16)
        w2dup = jnp.concatenate([nsa_cmp_w2[l], nsa_cmp_w2[l]], axis=-1).astype(BF16)
        pe8 = jnp.broadcast_to(nsa_cmp_pe[l].astype(F32).reshape(2, 1, 2 * half), (2, 8, 2 * half))
        kv_cmp, kv_cmp_t = _compress(r, w1[:, :half], w1[:, half:], w2dup, pe8)
        o_cmp, notsel = _cmp_select(p, ps, kv_cmp, kv_cmp_t, fcmp, overlap_t, b, s)
        o_sel = _sel_attn(tab_flat, p, ps, notsel, bsel, b, s)
        o_win = _win_attn(p, ps, bwin, b, s)

        w_kv = w_mem_kv[l].astype(BF16)
        mem_kv, _ = _norm_proj(mem2d, mem_norm_g[l], w_kv, w_kv[:, :LANES], tm=min(512, b * m_tok))
        o_mem = _mem_attn(p, mem_kv, b, s)

        x2d = _out_proj(x2d, (o_fox, o_ssd, o_cmp, o_sel, o_win, o_mem), w_out[l].astype(BF16),
                        final_norm_g, final=(l == depth - 1))
    return x2d.reshape(b, s, d)


def kernel(x, mem, norm_g, w_in, fox_f_bias, ssm_conv_w, ssm_conv_b, ssm_dt_bias, ssm_a_log, ssm_d, ssm_norm_g,
           nsa_cmp_pe, nsa_cmp_w1, nsa_cmp_w2, rel_bias_table, mem_norm_g, w_mem_kv, w_out, final_norm_g):
    return _trunk(x, mem, norm_g, w_in, fox_f_bias, ssm_conv_w, ssm_conv_b, ssm_dt_bias, ssm_a_log, ssm_d,
                  ssm_norm_g, nsa_cmp_pe, nsa_cmp_w1, nsa_cmp_w2, rel_bias_table, mem_norm_g, w_mem_kv, w_out,
                  final_norm_g)
```

```python
import functools
import math

import numpy as np
import jax
import jax.numpy as jnp
from jax import lax
from jax.experimental import pallas as pl
from jax.experimental.pallas import tpu as pltpu

F32 = jnp.float32
BF16 = jnp.bfloat16
HIGHEST = lax.Precision.HIGHEST

D_MODEL = 1024
GROUP_W = 512
HEAD_DIM = 64
EPS = 1e-6
NEG_INF = -1e30
TINY = 1e-30
LOG2E = math.log2(math.e)

FOX_HEADS = 8
SSM_HEADS = 8
SSM_STATE = 128
SSM_GROUPS = 2
SSM_CONV = 4
SSM_CHUNK = 128
SSM_CONV_DIM = GROUP_W + 2 * SSM_GROUPS * SSM_STATE

NSA_HEADS = 8
NSA_KV_HEADS = 2
NSA_REP = NSA_HEADS // NSA_KV_HEADS
NSA_KV_W = NSA_KV_HEADS * HEAD_DIM
CMP_BLOCK = 32
CMP_STRIDE = 16
CMP_HIDDEN = 2 * HEAD_DIM
SEL_BLOCK = 64
SEL_TOPK = 16
WINDOW = 512
SEL_FORCE = 1e9

MEM_HEADS = 4
MEM_HEAD_DIM = GROUP_W // MEM_HEADS
REL_BUCKETS = 32
REL_MAX_DIST = 128

FOX_COLS = 4 * GROUP_W + FOX_HEADS
SSM_COLS = GROUP_W + SSM_CONV_DIM + SSM_HEADS
NSA_COLS = 2 * GROUP_W + 6 * NSA_KV_W + 3 * NSA_HEADS
MEM_COLS = 2 * GROUP_W

LANES = 128
VMEM_LIMIT = 56 * 1024 * 1024

OFF_FOX_Q, OFF_FOX_K, OFF_FOX_V, OFF_FOX_G = 0, 512, 1024, 1536
OFF_SSM_XBC, OFF_SSM_Z = 2048, 3072
OFF_NSA_Q, OFF_NSA_G = 3584, 4096
OFF_MEM_Q, OFF_MEM_G = 4608, 5120
OFF_SEL_K, OFF_SEL_V, OFF_WIN_K, OFF_WIN_V, OFF_CMP_KV = 5632, 5888, 6144, 6400, 6656
P_COLS = 6912
PS_COLS = 3 * LANES

T_ATT = 256
SEL_T, SEL_TK = 512, 256
WIN_T, WIN_TK = 512, 256
N_CMP_PAD = 256


def _params(sem):
    return pltpu.CompilerParams(dimension_semantics=sem, vmem_limit_bytes=VMEM_LIMIT)


def _t5_bucket_np(dist):
    n = np.maximum(dist, 0)
    max_exact = REL_BUCKETS // 2
    nf = np.maximum(n, 1).astype(np.float32)
    large = max_exact + (np.log(nf / np.float32(max_exact)) / np.float32(math.log(REL_MAX_DIST / max_exact))
                         * np.float32(REL_BUCKETS - max_exact)).astype(np.int32)
    large = np.minimum(large, REL_BUCKETS - 1)
    return np.where(n < max_exact, n, large).astype(np.int32)


def _silu(x):
    h = 0.5 * x
    return h + h * jnp.tanh(h)


def _norm_proj_kernel(x_ref, g_ref, w_ref, ws_ref, p_ref, ps_ref, *, chunk):
    x = x_ref[...]
    ms = jnp.mean(x * x, axis=-1, keepdims=True)
    h = ((x * lax.rsqrt(ms + EPS)) * g_ref[...]).astype(BF16)
    ncol = p_ref.shape[1]
    for c0 in range(0, ncol, chunk):
        c1 = min(c0 + chunk, ncol)
        p_ref[:, c0:c1] = jnp.dot(h, w_ref[:, c0:c1], preferred_element_type=F32).astype(p_ref.dtype)
    ps_ref[...] = jnp.dot(h, ws_ref[...], preferred_element_type=F32)


def _norm_proj(x2d, g, w_main, w_small, tm=512):
    n, d = x2d.shape
    pc, sc = w_main.shape[1], w_small.shape[1]
    return pl.pallas_call(
        functools.partial(_norm_proj_kernel, chunk=512),
        grid=(n // tm,),
        in_specs=[
            pl.BlockSpec((tm, d), lambda i: (i, 0)),
            pl.BlockSpec((1, d), lambda i: (0, 0)),
            pl.BlockSpec((d, pc), lambda i: (0, 0)),
            pl.BlockSpec((d, sc), lambda i: (0, 0)),
        ],
        out_specs=[
            pl.BlockSpec((tm, pc), lambda i: (i, 0)),
            pl.BlockSpec((tm, sc), lambda i: (i, 0)),
        ],
        out_shape=[jax.ShapeDtypeStruct((n, pc), BF16), jax.ShapeDtypeStruct((n, sc), F32)],
        compiler_params=_params(("parallel",)),
        name="norm_proj",
    )(x2d, g.reshape(1, d), w_main, w_small)


def _bias_table_kernel(tab_ref, idx_ref, o_ref):
    h = pl.program_id(0)
    idx = idx_ref[...]
    acc = jnp.full(idx.shape, NEG_INF, F32)
    for b in range(REL_BUCKETS):
        acc = jnp.where(idx == b, tab_ref[b * NSA_HEADS + h] * LOG2E, acc)
    o_ref[0] = acc


def _bias_table(tab_flat, idx):
    r, c = idx.shape
    return pl.pallas_call(
        _bias_table_kernel,
        grid=(NSA_HEADS,),
        in_specs=[
            pl.BlockSpec(memory_space=pltpu.SMEM),
            pl.BlockSpec((r, c), lambda h: (0, 0)),
        ],
        out_specs=pl.BlockSpec((1, r, c), lambda h: (h, 0, 0)),
        out_shape=jax.ShapeDtypeStruct((NSA_HEADS, r, c), F32),
        compiler_params=_params(("arbitrary",)),
        name="t5_bias_table",
    )(tab_flat, jnp.asarray(idx))


def _bias_indices():
    t = T_ATT
    i = np.arange(t)[:, None]
    d = np.arange(WIN_T)[None, :] + WINDOW - np.arange(WINDOW + WIN_T)[:, None]
    idx_win = np.where((d >= 0) & (d < WINDOW), _t5_bucket_np(d), -1).astype(np.int32)
    kr = np.arange((SEL_T // SEL_TK + 1) * SEL_TK)[:, None] - SEL_TK
    ds = np.arange(SEL_T)[None, :] - kr
    idx_sel = np.where(ds >= 0, _t5_bucket_np(ds), -1).astype(np.int32)
    u = np.arange(2 * N_CMP_PAD)[:, None] - N_CMP_PAD
    dc = np.arange(t)[None, :] - CMP_STRIDE * u - (CMP_BLOCK - 1)
    idx_cmp = np.where(dc >= 0, _t5_bucket_np(dc), -1).astype(np.int32)
    return idx_win, idx_sel, idx_cmp


def _tri_lower(n):
    r = lax.broadcasted_iota(jnp.int32, (n, n), 0)
    c = lax.broadcasted_iota(jnp.int32, (n, n), 1)
    return (r >= c).astype(F32)


N_SPLIT = 3


def _fox_aug_lane(h, i):
    return LANES * (h // 2) + (HEAD_DIM if h % 2 == 0 else 0) + i


def _fox_aug_consts():
    pq = np.zeros((N_SPLIT * LANES, GROUP_W), np.float32)
    pk = np.zeros((N_SPLIT * LANES, GROUP_W), np.float32)
    oq = np.zeros((1, GROUP_W), np.float32)
    ok = np.zeros((1, GROUP_W), np.float32)
    for h in range(FOX_HEADS):
        for i in range(N_SPLIT):
            pq[i * LANES + h, _fox_aug_lane(h, i)] = 1.0
            pk[i * LANES + h, _fox_aug_lane(h, N_SPLIT + i)] = -1.0
            oq[0, _fox_aug_lane(h, N_SPLIT + i)] = 1.0
            ok[0, _fox_aug_lane(h, i)] = 1.0
    return pq, pk, oq, ok


def _fox_cumsum_kernel(f_ref, b_ref, pq_ref, pk_ref, oq_ref, ok_ref, qa_ref, ka_ref, carry_ref, *, ts):
    @pl.when(pl.program_id(1) == 0)
    def _():
        carry_ref[...] = jnp.zeros_like(carry_ref)

    z = f_ref[...] + b_ref[...]
    logf = (jnp.minimum(z, 0.0) - jnp.log(1.0 + jnp.exp(-jnp.abs(z)))) * LOG2E
    tri = _tri_lower(LANES)
    carry = carry_ref[...]
    for c in range(ts // LANES):
        rows = slice(c * LANES, (c + 1) * LANES)
        cs = jnp.dot(tri, logf[rows], precision=HIGHEST, preferred_element_type=F32) + carry
        carry = cs[LANES - 1:LANES, :]
        pieces, rest = [], cs
        for _ in range(N_SPLIT):
            piece = rest.astype(BF16)
            pieces.append(piece)
            rest = rest - piece.astype(F32)
        cat = jnp.concatenate(pieces, axis=1)
        qa_ref[rows, :] = (jnp.dot(cat, pq_ref[...], preferred_element_type=F32) + oq_ref[...]).astype(BF16)
        ka_ref[rows, :] = (jnp.dot(cat, pk_ref[...], preferred_element_type=F32) + ok_ref[...]).astype(BF16)
    carry_ref[...] = carry


def _fox_cumsum(ps, f_bias_pad, b, s, ts=512):
    ns = s // ts
    pq, pk, oq, ok = _fox_aug_consts()
    const = lambda bi, si: (0, 0)
    return pl.pallas_call(
        functools.partial(_fox_cumsum_kernel, ts=ts),
        grid=(b, ns),
        in_specs=[
            pl.BlockSpec((ts, LANES), lambda bi, si: (bi * ns + si, 0)),
            pl.BlockSpec((1, LANES), const),
            pl.BlockSpec(pq.shape, const),
            pl.BlockSpec(pk.shape, const),
            pl.BlockSpec(oq.shape, const),
            pl.BlockSpec(ok.shape, const),
        ],
        out_specs=[
            pl.BlockSpec((ts, GROUP_W), lambda bi, si: (bi * ns + si, 0)),
            pl.BlockSpec((ts, GROUP_W), lambda bi, si: (bi * ns + si, 0)),
        ],
        out_shape=[jax.ShapeDtypeStruct((b * s, GROUP_W), BF16), jax.ShapeDtypeStruct((b * s, GROUP_W), BF16)],
        scratch_shapes=[pltpu.VMEM((1, LANES), F32)],
        compiler_params=_params(("parallel", "arbitrary")),
        name="fox_cumsum",
    )(ps, f_bias_pad, jnp.asarray(pq, BF16), jnp.asarray(pk, BF16), jnp.asarray(oq), jnp.asarray(ok))


def _half_masks(rows):
    lo = jnp.where(lax.broadcasted_iota(jnp.int32, (rows, LANES), 1) < HEAD_DIM, 1.0, 0.0)
    return lo.astype(BF16), (1.0 - lo).astype(BF16)


def _fox_attn_kernel(q_ref, qa_ref, k_ref, ka_ref, v_ref, gate_ref, o_ref, qs_ref, kk_ref, vt_ref, m_ref, acc_ref,
                     pre_ref, *, t, tk, ahead):
    qi = pl.program_id(1)
    lo_q, hi_q = _half_masks(t)
    lo = lax.broadcasted_iota(jnp.int32, (t, LANES), 1) < HEAD_DIM
    cm = lax.broadcasted_iota(jnp.int32, (tk, t), 0) - lax.broadcasted_iota(jnp.int32, (tk, t), 1)

    @pl.when(qi == 0)
    def _():
        lo_k, hi_k = _half_masks(tk)

        def merge(j, c):
            rows = pl.ds(pl.multiple_of(j * tk, tk), tk)
            for hp in range(FOX_HEADS // 2):
                ls = slice(hp * LANES, (hp + 1) * LANES)
                kp, ka, vp = k_ref[rows, ls], ka_ref[rows, ls], v_ref[rows, ls]
                kk_ref[2 * hp, rows, :] = kp * lo_k + ka * hi_k
                kk_ref[2 * hp + 1, rows, :] = ka * lo_k + kp * hi_k
                vt_ref[2 * hp, :, rows] = (vp * lo_k + hi_k).astype(F32).T.astype(BF16)
                vt_ref[2 * hp + 1, :, rows] = (lo_k + vp * hi_k).astype(F32).T.astype(BF16)
            return c

        lax.fori_loop(0, k_ref.shape[0] // tk, merge, 0)

    for hp in range(FOX_HEADS // 2):
        ls = slice(hp * LANES, (hp + 1) * LANES)
        qp, qa = q_ref[:, ls], qa_ref[:, ls]
        qs_ref[2 * hp] = qp * lo_q + qa * hi_q
        qs_ref[2 * hp + 1] = qa * lo_q + qp * hi_q
    m_ref[...] = jnp.full(m_ref.shape, NEG_INF, F32)
    acc_ref[...] = jnp.zeros_like(acc_ref)

    def rows_of(j):
        return pl.ds(pl.multiple_of(j * tk, tk), tk)

    def scores(j, h):
        return lax.dot_general(kk_ref[h, rows_of(j), :], qs_ref[h], (((1,), (1,)), ((), ())),
                               preferred_element_type=F32)

    def tile(j, diag, j_next):
        queue = [pre_ref[i] for i in range(ahead)]
        for h in range(FOX_HEADS):
            if h + ahead < FOX_HEADS:
                queue.append(scores(j, h + ahead))
            elif j_next is not None:
                pre_ref[h + ahead - FOX_HEADS] = scores(j_next, h + ahead - FOX_HEADS)
            s = queue.pop(0)
            if diag:
                s = jnp.where(cm <= qi * t - j * tk, s, NEG_INF)
            m_old = m_ref[h]
            m_new = jnp.maximum(m_old, jnp.max(s, axis=0, keepdims=True))
            m_ref[h] = m_new
            p = jnp.exp2(s - m_new).astype(BF16)
            acc_ref[h] = jnp.exp2(m_old - m_new) * acc_ref[h] + jnp.dot(vt_ref[h, :, rows_of(j)], p,
                                                                        preferred_element_type=F32)

    jd = (qi * t) // tk
    n_diag = max(t // tk, 1)
    for i in range(ahead):
        pre_ref[i] = scores(0, i)

    def body(j, c):
        tile(j, False, j + 1)
        return c

    lax.fori_loop(0, jd, body, 0)
    for dj in range(n_diag):
        tile(jd + dj, True, jd + dj + 1 if dj + 1 < n_diag else None)

    for hp in range(FOX_HEADS // 2):
        ls = slice(hp * LANES, (hp + 1) * LANES)
        a0, a1 = acc_ref[2 * hp].T, acc_ref[2 * hp + 1].T
        o = jnp.where(lo, a0 / jnp.maximum(a0[:, HEAD_DIM:HEAD_DIM + 1], TINY),
                      a1 / jnp.maximum(a1[:, 0:1], TINY))
        o_ref[:, ls] = (o * _silu(gate_ref[:, ls].astype(F32))).astype(o_ref.dtype)


def _fox_attn(p, qaug, kaug, b, s, t=512, tk=256, ahead=2):
    nq = s // t
    w = GROUP_W
    return pl.pallas_call(
        functools.partial(_fox_attn_kernel, t=t, tk=tk, ahead=ahead),
        grid=(b, nq),
        in_specs=[
            pl.BlockSpec((t, w), lambda bi, qi: (bi * nq + qi, OFF_FOX_Q // w)),
            pl.BlockSpec((t, w), lambda bi, qi: (bi * nq + qi, 0)),
            pl.BlockSpec((s, w), lambda bi, qi: (bi, OFF_FOX_K // w)),
            pl.BlockSpec((s, w), lambda bi, qi: (bi, 0)),
            pl.BlockSpec((s, w), lambda bi, qi: (bi, OFF_FOX_V // w)),
            pl.BlockSpec((t, w), lambda bi, qi: (bi * nq + qi, OFF_FOX_G // w)),
        ],
        out_specs=pl.BlockSpec((t, w), lambda bi, qi: (bi * nq + qi, 0)),
        out_shape=jax.ShapeDtypeStruct((b * s, w), BF16),
        scratch_shapes=[
            pltpu.VMEM((FOX_HEADS, t, LANES), BF16),
            pltpu.VMEM((FOX_HEADS, s, LANES), BF16),
            pltpu.VMEM((FOX_HEADS, LANES, s), BF16),
            pltpu.VMEM((FOX_HEADS, 1, t), F32),
            pltpu.VMEM((FOX_HEADS, LANES, t), F32),
            pltpu.VMEM((ahead, tk, t), F32),
        ],
        compiler_params=_params(("parallel", "arbitrary")),
        name="fox_attn",
    )(p, qaug, p, kaug, p, p)


def _dot_split(lhs_f32, rhs_b):
    out, rest = None, lhs_f32
    for _ in range(N_SPLIT):
        piece = rest.astype(BF16)
        rest = rest - piece.astype(F32)
        term = jnp.dot(piece, rhs_b, preferred_element_type=F32)
        out = term if out is None else out + term
    return out


def _ssd_kernel(z_ref, xbc_ref, dt_ref, cw_ref, cb_ref, dtb_ref, alog_ref, dsk_ref, ng_ref, o_ref,
                xpad_ref, xc_ref, state_ref, y_ref, *, nch):
    q = SSM_CHUNK
    rows_all = nch * q
    halo = 8

    @pl.when(pl.program_id(1) == 0)
    def _():
        xpad_ref[0:halo, :] = jnp.zeros((halo, SSM_CONV_DIM), F32)
        state_ref[...] = jnp.zeros_like(state_ref)

    xpad_ref[halo:halo + rows_all, :] = xbc_ref[...].astype(F32)
    y = cb_ref[...]
    for k in range(SSM_CONV):
        off = halo - (SSM_CONV - 1) + k
        y = y + cw_ref[k:k + 1, :] * xpad_ref[off:off + rows_all, :]
    xpad_ref[0:halo, :] = xpad_ref[rows_all:rows_all + halo, :]
    xc_ref[...] = _silu(y)

    x_dt = dt_ref[...] + dtb_ref[...]
    dt_all = jnp.maximum(x_dt, 0.0) + jnp.log(1.0 + jnp.exp(-jnp.abs(x_dt)))
    a_all = dt_all * (-jnp.exp(alog_ref[...]))
    tri_t = (lax.broadcasted_iota(jnp.int32, (q, q), 0) <= lax.broadcasted_iota(jnp.int32, (q, q), 1)).astype(BF16)
    er = lax.broadcasted_iota(jnp.int32, (LANES, GROUP_W), 0)
    ec = lax.broadcasted_iota(jnp.int32, (LANES, GROUP_W), 1)
    expand = jnp.where(ec // HEAD_DIM == er, 1.0, 0.0).astype(BF16)
    row = lax.broadcasted_iota(jnp.int32, (q, q), 0)
    col = lax.broadcasted_iota(jnp.int32, (q, q), 1)
    causal = row >= col
    lo = lax.broadcasted_iota(jnp.int32, (q, LANES), 1) < HEAD_DIM
    gw = GROUP_W // SSM_GROUPS
    hpg = SSM_HEADS // SSM_GROUPS

    for c in range(nch):
        rs = slice(c * q, (c + 1) * q)
        xs = xc_ref[rs, :GROUP_W]
        dt = dt_all[rs]
        acs_t = _dot_split(a_all[rs].T, tri_t)
        acs = acs_t.T
        stacked = jnp.concatenate([dt, jnp.exp(acs), jnp.exp(acs[q - 1:q, :] - acs)], axis=0)
        full = _dot_split(stacked, expand)
        dt_full, eacs_full, dec_full = full[0:q], full[q:2 * q], full[2 * q:3 * q]
        xdt = xs * dt_full
        xdt_b = xdt.astype(BF16)
        xdec_b = (xdt * dec_full).astype(BF16)

        for g in range(SSM_GROUPS):
            bm = xc_ref[rs, GROUP_W + g * SSM_STATE:GROUP_W + (g + 1) * SSM_STATE]
            cm = xc_ref[rs, GROUP_W + (SSM_GROUPS + g) * SSM_STATE:GROUP_W + (SSM_GROUPS + g + 1) * SSM_STATE]
            bm_b = bm.astype(BF16)
            cm_b = cm.astype(BF16)
            gs = slice(g * gw, (g + 1) * gw)
            cbg = lax.dot_general(cm_b, bm_b, (((1,), (1,)), ((), ())), preferred_element_type=F32)
            st = state_ref[:, gs]
            y_off = jnp.dot(cm_b, st.astype(BF16), preferred_element_type=F32) * eacs_full[:, gs]
            cst = jnp.dot(bm.T.astype(BF16), xdec_b[:, gs], preferred_element_type=F32)
            state_ref[:, gs] = st * eacs_full[q - 1:q, gs] + cst
            for hp in range(hpg // 2):
                ls = slice(g * gw + hp * LANES, g * gw + (hp + 1) * LANES)
                yd = []
                for e in range(2):
                    h = g * hpg + 2 * hp + e
                    seg = jnp.exp(jnp.where(causal, acs[:, h:h + 1] - acs_t[h:h + 1, :], NEG_INF))
                    yd.append(jnp.dot((cbg * seg).astype(BF16), xdt_b[:, ls], preferred_element_type=F32))
                y_ref[:, ls] = jnp.where(lo, yd[0], yd[1]) + y_off[:, hp * LANES:(hp + 1) * LANES]

        yz = (y_ref[...] + xs * dsk_ref[...]) * _silu(z_ref[rs, :].astype(F32))
        for g in range(SSM_GROUPS):
            gs = slice(g * gw, (g + 1) * gw)
            blk = yz[:, gs]
            ms = jnp.mean(blk * blk, axis=-1, keepdims=True)
            o_ref[rs, gs] = ((blk * lax.rsqrt(ms + EPS)) * ng_ref[:, gs]).astype(o_ref.dtype)


def _ssd(p, ps, conv_w, conv_b, dt_bias_pad, a_log_pad, d_full, norm_g, b, s, nch=4):
    q = SSM_CHUNK
    rows = nch * q
    nc = s // rows
    row = lambda bi, ci: (bi * nc + ci)
    const = lambda bi, ci: (0, 0)
    return pl.pallas_call(
        functools.partial(_ssd_kernel, nch=nch),
        grid=(b, nc),
        in_specs=[
            pl.BlockSpec((rows, GROUP_W), lambda bi, ci: (row(bi, ci), OFF_SSM_Z // GROUP_W)),
            pl.BlockSpec((rows, SSM_CONV_DIM), lambda bi, ci: (row(bi, ci), OFF_SSM_XBC // SSM_CONV_DIM)),
            pl.BlockSpec((rows, LANES), lambda bi, ci: (row(bi, ci), 1)),
            pl.BlockSpec((SSM_CONV, SSM_CONV_DIM), const),
            pl.BlockSpec((1, SSM_CONV_DIM), const),
            pl.BlockSpec((1, LANES), const),
            pl.BlockSpec((1, LANES), const),
            pl.BlockSpec((1, GROUP_W), const),
            pl.BlockSpec((1, GROUP_W), const),
        ],
        out_specs=pl.BlockSpec((rows, GROUP_W), lambda bi, ci: (row(bi, ci), 0)),
        out_shape=jax.ShapeDtypeStruct((b * s, GROUP_W), BF16),
        scratch_shapes=[
            pltpu.VMEM((rows + 8, SSM_CONV_DIM), F32),
            pltpu.VMEM((rows, SSM_CONV_DIM), F32),
            pltpu.VMEM((SSM_STATE, GROUP_W), F32),
            pltpu.VMEM((q, GROUP_W), F32),
        ],
        compiler_params=_params(("parallel", "arbitrary")),
        name="ssd",
    )(p, p, ps, conv_w, conv_b, dt_bias_pad, a_log_pad, d_full, norm_g)


def _compress_kernel(r_ref, w1a_ref, w1b_ref, w2_ref, pe_ref, o_ref, ot_ref):
    half = CMP_STRIDE * HEAD_DIM
    r = r_ref[0, 0]
    w1a = w1a_ref[0]
    w1b = w1b_ref[0]
    a = jnp.dot(r, w1a, preferred_element_type=F32)
    bm = jnp.dot(r, w1b, preferred_element_type=F32)
    pe = pe_ref[0].astype(BF16)
    const = (jnp.dot(pe[:, :half], w1a, preferred_element_type=F32)
             + jnp.dot(pe[:, half:], w1b, preferred_element_type=F32))[0:1]
    n = bm.shape[0]
    h = _silu(a + pltpu.roll(bm, n - 1, 0) + const)
    o = jnp.dot(h.astype(BF16), w2_ref[0], preferred_element_type=F32)
    o_ref[0, 0] = o.astype(o_ref.dtype)
    ot_ref[0, 0] = o.T.astype(ot_ref.dtype)


def _compress(r, w1a, w1b, w2dup, pe8):
    b, nslot, nr, width = r.shape
    return pl.pallas_call(
        _compress_kernel,
        grid=(b, nslot),
        in_specs=[
            pl.BlockSpec((1, 1, nr, width), lambda bi, si: (bi, si, 0, 0)),
            pl.BlockSpec((1, width, CMP_HIDDEN), lambda bi, si: (si // NSA_KV_HEADS, 0, 0)),
            pl.BlockSpec((1, width, CMP_HIDDEN), lambda bi, si: (si // NSA_KV_HEADS, 0, 0)),
            pl.BlockSpec((1, CMP_HIDDEN, LANES), lambda bi, si: (si // NSA_KV_HEADS, 0, 0)),
            pl.BlockSpec((1, 8, 2 * width), lambda bi, si: (si // NSA_KV_HEADS, 0, 0)),
        ],
        out_specs=[pl.BlockSpec((1, 1, nr, LANES), lambda bi, si: (bi, si, 0, 0)),
                   pl.BlockSpec((1, 1, LANES, nr), lambda bi, si: (bi, si, 0, 0))],
        out_shape=[jax.ShapeDtypeStruct((b, nslot, nr, LANES), BF16),
                   jax.ShapeDtypeStruct((b, nslot, LANES, nr), BF16)],
        compiler_params=_params(("parallel", "arbitrary")),
        name="nsa_compress",
    )(r, w1a, w1b, w2dup, pe8)


def _cmp_select_kernel(q_ref, kv_ref, vt_ref, gl_ref, gate_ref, fc_ref, ovt_ref, o_ref, ns_ref):
    t = T_ATT
    ncp = kv_ref.shape[2]
    qi = pl.program_id(1)
    t0 = qi * t
    lo = lax.broadcasted_iota(jnp.int32, (t, LANES), 1) < HEAD_DIM
    zero = jnp.zeros((t, LANES), BF16)
    start = pl.multiple_of(ncp - qi * (t // CMP_STRIDE), CMP_STRIDE)
    gl = gl_ref[...]

    nsel = SEL_BLOCK
    jrow = lax.broadcasted_iota(jnp.int32, (nsel, t), 0)
    cur = (t0 + lax.broadcasted_iota(jnp.int32, (nsel, t), 1)) // SEL_BLOCK
    forced = (jrow == 0) | (jrow == cur) | (jrow == cur - 1)
    past = jrow <= cur

    for g in range(NSA_KV_HEADS):
        kc = kv_ref[0, g]
        vct = vt_ref[0, NSA_KV_HEADS + g]
        psum = jnp.zeros((ncp, t), F32)
        outs = []
        for r in range(NSA_REP):
            h = g * NSA_REP + r
            qp = q_ref[:, (h // 2) * LANES:(h // 2 + 1) * LANES]
            qe = jnp.where(lo, qp, zero) if r % 2 == 0 else jnp.where(lo, zero, qp)
            s = lax.dot_general(kc, qe, (((1,), (1,)), ((), ())), preferred_element_type=F32)
            s = fc_ref[h, pl.ds(start, ncp), :] + s
            m = jnp.maximum(jnp.max(s, axis=0, keepdims=True), 0.1 * NEG_INF)
            e = jnp.exp2(s - m)
            pr = e * (1.0 / jnp.maximum(jnp.sum(e, axis=0, keepdims=True), TINY))
            psum = psum + pr
            oc = jnp.dot(vct, pr.astype(BF16), preferred_element_type=F32)
            outs.append(oc.T * jax.nn.sigmoid(gl[:, h:h + 1]))
        for hp in range(NSA_REP // 2):
            ls = slice((g * NSA_REP // 2 + hp) * LANES, (g * NSA_REP // 2 + hp + 1) * LANES)
            o = jnp.where(lo, outs[2 * hp], outs[2 * hp + 1])
            o_ref[:, ls] = (o * _silu(gate_ref[:, ls].astype(F32))).astype(o_ref.dtype)

        imp_t = jnp.zeros((nsel, t), F32)
        rest = psum
        for _ in range(N_SPLIT):
            piece = rest.astype(BF16)
            rest = rest - piece.astype(F32)
            imp_t = imp_t + jnp.dot(ovt_ref[...], piece, preferred_element_type=F32)
        imp_t = jnp.where(past, jnp.where(forced, SEL_FORCE, imp_t), -SEL_FORCE)
        sub = 8
        sub_row = lax.broadcasted_iota(jnp.int32, (sub, t), 0)
        rows = [imp_t[k * sub:(k + 1) * sub] for k in range(nsel // sub)]
        rank = [jnp.zeros((sub, t), F32) for _ in rows]
        for i in range(nsel):
            bi = imp_t[i:i + 1, :]
            for k, x in enumerate(rows):
                if k * sub > i:
                    ahead = jnp.where(bi >= x, 1.0, 0.0)
                elif k * sub + sub - 1 <= i:
                    ahead = jnp.where(bi > x, 1.0, 0.0)
                else:
                    ahead = jnp.where(sub_row > i - k * sub, jnp.where(bi >= x, 1.0, 0.0),
                                      jnp.where(bi > x, 1.0, 0.0))
                rank[k] = rank[k] + ahead
        rank = jnp.concatenate(rank, axis=0)
        notsel = jnp.where((rank < float(SEL_TOPK)) & past, 0.0, 1.0)
        ns2 = jnp.concatenate([notsel, notsel], axis=0).T
        ns_ref[:, g * LANES:(g + 1) * LANES] = ns2.astype(ns_ref.dtype)


def _cmp_select(p, ps, kvc, kvc_t, fc, overlap_t, b, s):
    t = T_ATT
    nq = s // t
    w = GROUP_W
    ncp = kvc.shape[2]
    return pl.pallas_call(
        _cmp_select_kernel,
        grid=(b, nq),
        in_specs=[
            pl.BlockSpec((t, w), lambda bi, qi: (bi * nq + qi, OFF_NSA_Q // w)),
            pl.BlockSpec((1, 2 * NSA_KV_HEADS, ncp, LANES), lambda bi, qi: (bi, 0, 0, 0)),
            pl.BlockSpec((1, 2 * NSA_KV_HEADS, LANES, ncp), lambda bi, qi: (bi, 0, 0, 0)),
            pl.BlockSpec((t, LANES), lambda bi, qi: (bi * nq + qi, 2)),
            pl.BlockSpec((t, w), lambda bi, qi: (bi * nq + qi, OFF_NSA_G // w)),
            pl.BlockSpec((NSA_HEADS, 2 * ncp, t), lambda bi, qi: (0, 0, 0)),
            pl.BlockSpec((SEL_BLOCK, ncp), lambda bi, qi: (0, 0)),
        ],
        out_specs=[
            pl.BlockSpec((t, w), lambda bi, qi: (bi * nq + qi, 0)),
            pl.BlockSpec((t, NSA_KV_HEADS * LANES), lambda bi, qi: (bi * nq + qi, 0)),
        ],
        out_shape=[jax.ShapeDtypeStruct((b * s, w), BF16),
                   jax.ShapeDtypeStruct((b * s, NSA_KV_HEADS * LANES), BF16)],
        compiler_params=_params(("parallel", "arbitrary")),
        name="nsa_cmp_select",
    )(p, kvc, kvc_t, ps, p, fc, overlap_t)


def _sel_attn_kernel(tab_ref, q_ref, ns_ref, k_ref, v_ref, gl_ref, gate_ref, bs_ref, o_ref,
                     qa_ref, kk_ref, vt_ref, m_ref, acc_ref, pre_ref, *, t, tk, ahead):
    qi = pl.program_id(1)
    lo_q, hi_q = _half_masks(t)
    lo = lax.broadcasted_iota(jnp.int32, (t, LANES), 1) < HEAD_DIM
    n_near = t // tk + 1

    @pl.when(qi == 0)
    def _():
        lane = lax.broadcasted_iota(jnp.int32, (tk, LANES), 1)
        krow = lax.broadcasted_iota(jnp.int32, (tk, LANES), 0)
        lo_f = jnp.where(lane < HEAD_DIM, 1.0, 0.0)
        hi_f = 1.0 - lo_f
        lo_k, hi_k = lo_f.astype(BF16), hi_f.astype(BF16)

        def merge(j, c):
            ks = pl.multiple_of(j * tk, tk)
            rows = pl.ds(ks, tk)
            hot = jnp.where((lane % HEAD_DIM) == (ks + krow) // SEL_BLOCK, NEG_INF, 0.0)
            for g in range(NSA_KV_HEADS):
                gls = slice(g * LANES, (g + 1) * LANES)
                kd, vd = k_ref[rows, gls], v_ref[rows, gls]
                kk_ref[2 * g, rows, :] = kd * lo_k + (hot * hi_f).astype(BF16)
                kk_ref[2 * g + 1, rows, :] = (hot * lo_f).astype(BF16) + kd * hi_k
                vt_ref[2 * g, :, rows] = (vd * lo_k + hi_k).astype(F32).T.astype(BF16)
                vt_ref[2 * g + 1, :, rows] = (lo_k + vd * hi_k).astype(F32).T.astype(BF16)
            return c

        lax.fori_loop(0, k_ref.shape[0] // tk, merge, 0)

    for h in range(NSA_HEADS):
        g = h // NSA_REP
        qp = q_ref[:, (h // 2) * LANES:(h // 2 + 1) * LANES]
        ns = ns_ref[:, g * LANES:(g + 1) * LANES]
        qa_ref[h] = qp * lo_q + ns * hi_q if h % 2 == 0 else ns * lo_q + qp * hi_q
    m_ref[...] = jnp.full(m_ref.shape, NEG_INF, F32)
    acc_ref[...] = jnp.zeros_like(acc_ref)

    kv = lambda h: 2 * (h // NSA_REP) + h % 2

    def rows_of(j):
        return pl.ds(pl.multiple_of(j * tk, tk), tk)

    def scores(j, h):
        return lax.dot_general(kk_ref[kv(h), rows_of(j), :], qa_ref[h], (((1,), (1,)), ((), ())),
                               preferred_element_type=F32)

    def tile(j, near, j_next):
        rows = rows_of(j)
        queue = [pre_ref[i] for i in range(ahead)]
        for h in range(NSA_HEADS):
            if h + ahead < NSA_HEADS:
                queue.append(scores(j, h + ahead))
            elif j_next is not None:
                pre_ref[h + ahead - NSA_HEADS] = scores(j_next, h + ahead - NSA_HEADS)
            s = queue.pop(0)
            m_old = m_ref[h]
            if near is None:
                far = tab_ref[(REL_BUCKETS - 1) * NSA_HEADS + h] * LOG2E
                m_new = jnp.maximum(m_old, jnp.max(s, axis=0, keepdims=True) + far)
                p = jnp.exp2(s - (m_new - far)).astype(BF16)
            else:
                s = bs_ref[h, near * tk:(near + 1) * tk, :] + s
                m_new = jnp.maximum(m_old, jnp.max(s, axis=0, keepdims=True))
                p = jnp.exp2(s - m_new).astype(BF16)
            m_ref[h] = m_new
            acc_ref[h] = jnp.exp2(m_old - m_new) * acc_ref[h] + jnp.dot(vt_ref[kv(h), :, rows], p,
                                                                        preferred_element_type=F32)

    jd = (qi * t) // tk
    for i in range(ahead):
        pre_ref[i] = scores(0, i)

    def body(j, c):
        tile(j, None, j + 1)
        return c

    lax.fori_loop(0, jnp.maximum(jd - 1, 0), body, 0)

    @pl.when(qi > 0)
    def _():
        tile(jd - 1, 0, jd)

    for dj in range(n_near - 1):
        tile(jd + dj, 1 + dj, jd + dj + 1 if dj + 2 < n_near else None)

    gl = gl_ref[...]
    for hp in range(NSA_HEADS // 2):
        ls = slice(hp * LANES, (hp + 1) * LANES)
        a0, a1 = acc_ref[2 * hp].T, acc_ref[2 * hp + 1].T
        c0 = NSA_HEADS + 2 * hp
        o = jnp.where(lo, a0 / jnp.maximum(a0[:, HEAD_DIM:HEAD_DIM + 1], TINY) * jax.nn.sigmoid(gl[:, c0:c0 + 1]),
                      a1 / jnp.maximum(a1[:, 0:1], TINY) * jax.nn.sigmoid(gl[:, c0 + 1:c0 + 2]))
        o_ref[:, ls] = (o * _silu(gate_ref[:, ls].astype(F32))).astype(o_ref.dtype)


def _sel_attn(tab_flat, p, ps, notsel, bsel, b, s, t=512, tk=256, ahead=2):
    nq = s // t
    w = GROUP_W
    kw = NSA_KV_HEADS * LANES
    n_near = t // tk + 1
    return pl.pallas_call(
        functools.partial(_sel_attn_kernel, t=t, tk=tk, ahead=ahead),
        grid=(b, nq),
        in_specs=[
            pl.BlockSpec(memory_space=pltpu.SMEM),
            pl.BlockSpec((t, w), lambda bi, qi: (bi * nq + qi, OFF_NSA_Q // w)),
            pl.BlockSpec((t, kw), lambda bi, qi: (bi * nq + qi, 0)),
            pl.BlockSpec((s, kw), lambda bi, qi: (bi, OFF_SEL_K // kw)),
            pl.BlockSpec((s, kw), lambda bi, qi: (bi, OFF_SEL_V // kw)),
            pl.BlockSpec((t, LANES), lambda bi, qi: (bi * nq + qi, 2)),
            pl.BlockSpec((t, w), lambda bi, qi: (bi * nq + qi, OFF_NSA_G // w)),
            pl.BlockSpec((NSA_HEADS, n_near * tk, t), lambda bi, qi: (0, 0, 0), pipeline_mode=pl.Buffered(1)),
        ],
        out_specs=pl.BlockSpec((t, w), lambda bi, qi: (bi * nq + qi, 0)),
        out_shape=jax.ShapeDtypeStruct((b * s, w), BF16),
        scratch_shapes=[
            pltpu.VMEM((NSA_HEADS, t, LANES), BF16),
            pltpu.VMEM((2 * NSA_KV_HEADS, s, LANES), BF16),
            pltpu.VMEM((2 * NSA_KV_HEADS, LANES, s), BF16),
            pltpu.VMEM((NSA_HEADS, 1, t), F32),
            pltpu.VMEM((NSA_HEADS, LANES, t), F32),
            pltpu.VMEM((ahead, tk, t), F32),
        ],
        compiler_params=_params(("parallel", "arbitrary")),
        name="nsa_sel_attn",
    )(tab_flat, p, notsel, p, p, ps, p, bsel)


def _win_attn_kernel(q_ref, k_ref, v_ref, gl_ref, gate_ref, bw_ref, o_ref, qe_ref, vt_ref, m_ref, acc_ref,
                     *, t, tk, ahead):
    qi = pl.program_id(1)
    lo_q, hi_q = _half_masks(t)
    lo = lax.broadcasted_iota(jnp.int32, (t, LANES), 1) < HEAD_DIM
    n_before = WINDOW // tk
    n_tiles = n_before + t // tk

    @pl.when(qi == 0)
    def _():
        lo_k, hi_k = _half_masks(tk)

        def merge(j, c):
            rows = pl.ds(pl.multiple_of(j * tk, tk), tk)
            for g in range(NSA_KV_HEADS):
                vd = v_ref[rows, g * LANES:(g + 1) * LANES]
                vt_ref[2 * g, :, rows] = (vd * lo_k + hi_k).astype(F32).T.astype(BF16)
                vt_ref[2 * g + 1, :, rows] = (lo_k + vd * hi_k).astype(F32).T.astype(BF16)
            return c

        lax.fori_loop(0, v_ref.shape[0] // tk, merge, 0)

    for h in range(NSA_HEADS):
        qp = q_ref[:, (h // 2) * LANES:(h // 2 + 1) * LANES]
        qe_ref[h] = qp * lo_q if h % 2 == 0 else qp * hi_q
    m_ref[...] = jnp.full(m_ref.shape, NEG_INF, F32)
    acc_ref[...] = jnp.zeros_like(acc_ref)

    def tile(w):
        rows = pl.ds(pl.multiple_of((qi * (t // tk) - n_before + w) * tk, tk), tk)

        def scores(h):
            g = h // NSA_REP
            return lax.dot_general(k_ref[rows, g * LANES:(g + 1) * LANES], qe_ref[h], (((1,), (1,)), ((), ())),
                                   preferred_element_type=F32)

        queue = [scores(h) for h in range(ahead)]
        for h in range(NSA_HEADS):
            if h + ahead < NSA_HEADS:
                queue.append(scores(h + ahead))
            s = bw_ref[h, w * tk:(w + 1) * tk, :] + queue.pop(0)
            m_old = m_ref[h]
            m_new = jnp.maximum(m_old, jnp.max(s, axis=0, keepdims=True))
            m_ref[h] = m_new
            p = jnp.exp2(s - m_new).astype(BF16)
            acc_ref[h] = jnp.exp2(m_old - m_new) * acc_ref[h] + jnp.dot(
                vt_ref[2 * (h // NSA_REP) + h % 2, :, rows], p, preferred_element_type=F32)

    @pl.when(qi > 0)
    def _():
        for w in range(n_before):
            tile(w)

    for w in range(n_before, n_tiles):
        tile(w)

    gl = gl_ref[...]
    for hp in range(NSA_HEADS // 2):
        ls = slice(hp * LANES, (hp + 1) * LANES)
        a0, a1 = acc_ref[2 * hp].T, acc_ref[2 * hp + 1].T
        c0 = 2 * NSA_HEADS + 2 * hp
        o = jnp.where(lo, a0 / jnp.maximum(a0[:, HEAD_DIM:HEAD_DIM + 1], TINY) * jax.nn.sigmoid(gl[:, c0:c0 + 1]),
                      a1 / jnp.maximum(a1[:, 0:1], TINY) * jax.nn.sigmoid(gl[:, c0 + 1:c0 + 2]))
        o_ref[:, ls] = (o * _silu(gate_ref[:, ls].astype(F32))).astype(o_ref.dtype)


def _win_attn(p, ps, bwin, b, s, t=WIN_T, tk=WIN_TK, ahead=2):
    assert t % WINDOW == 0 and WINDOW % tk == 0
    nq = s // t
    w = GROUP_W
    kw = NSA_KV_HEADS * LANES
    return pl.pallas_call(
        functools.partial(_win_attn_kernel, t=t, tk=tk, ahead=ahead),
        grid=(b, nq),
        in_specs=[
            pl.BlockSpec((t, w), lambda bi, qi: (bi * nq + qi, OFF_NSA_Q // w)),
            pl.BlockSpec((s, kw), lambda bi, qi: (bi, OFF_WIN_K // kw)),
            pl.BlockSpec((s, kw), lambda bi, qi: (bi, OFF_WIN_V // kw)),
            pl.BlockSpec((t, LANES), lambda bi, qi: (bi * nq + qi, 2)),
            pl.BlockSpec((t, w), lambda bi, qi: (bi * nq + qi, OFF_NSA_G // w)),
            pl.BlockSpec((NSA_HEADS, WINDOW + t, t), lambda bi, qi: (0, 0, 0), pipeline_mode=pl.Buffered(1)),
        ],
        out_specs=pl.BlockSpec((t, w), lambda bi, qi: (bi * nq + qi, 0)),
        out_shape=jax.ShapeDtypeStruct((b * s, w), BF16),
        scratch_shapes=[
            pltpu.VMEM((NSA_HEADS, t, LANES), BF16),
            pltpu.VMEM((2 * NSA_KV_HEADS, LANES, s), BF16),
            pltpu.VMEM((NSA_HEADS, 1, t), F32),
            pltpu.VMEM((NSA_HEADS, LANES, t), F32),
        ],
        compiler_params=_params(("parallel", "arbitrary")),
        name="nsa_win_attn",
    )(p, p, p, ps, p, bwin)


def _mem_attn_kernel(q_ref, gate_ref, kv_ref, o_ref):
    scale = MEM_HEAD_DIM ** -0.5
    for h in range(MEM_HEADS):
        ls = slice(h * LANES, (h + 1) * LANES)
        k = kv_ref[:, ls]
        v = kv_ref[:, GROUP_W + h * LANES:GROUP_W + (h + 1) * LANES]
        s = lax.dot_general(q_ref[:, ls], k, (((1,), (1,)), ((), ())), preferred_element_type=F32) * scale
        m = jnp.max(s, axis=1, keepdims=True)
        e = jnp.exp(s - m)
        l = jnp.sum(e, axis=1, keepdims=True)
        o = jnp.dot(e.astype(BF16), v, preferred_element_type=F32) / l
        o_ref[:, ls] = (o * _silu(gate_ref[:, ls].astype(F32))).astype(o_ref.dtype)


def _mem_attn(p, mem_kv, b, s, t=512):
    nq = s // t
    w = GROUP_W
    m = mem_kv.shape[0] // b
    return pl.pallas_call(
        _mem_attn_kernel,
        grid=(b, nq),
        in_specs=[
            pl.BlockSpec((t, w), lambda bi, qi: (bi * nq + qi, OFF_MEM_Q // w)),
            pl.BlockSpec((t, w), lambda bi, qi: (bi * nq + qi, OFF_MEM_G // w)),
            pl.BlockSpec((m, 2 * w), lambda bi, qi: (bi, 0)),
        ],
        out_specs=pl.BlockSpec((t, w), lambda bi, qi: (bi * nq + qi, 0)),
        out_shape=jax.ShapeDtypeStruct((b * s, w), BF16),
        compiler_params=_params(("parallel", "arbitrary")),
        name="mem_attn",
    )(p, p, mem_kv)


def _out_proj_kernel(x_ref, of_ref, os_ref, oc_ref, osel_ref, ow_ref, om_ref, w_ref, g_ref, o_ref, *, final):
    w = GROUP_W
    nsa = (oc_ref[...].astype(F32) + osel_ref[...].astype(F32) + ow_ref[...].astype(F32)).astype(BF16)
    acc = x_ref[...]
    for i, part in enumerate((of_ref[...], os_ref[...], nsa, om_ref[...])):
        acc = acc + jnp.dot(part, w_ref[i * w:(i + 1) * w, :], preferred_element_type=F32)
    if final:
        ms = jnp.mean(acc * acc, axis=-1, keepdims=True)
        acc = (acc * lax.rsqrt(ms + EPS)) * g_ref[...]
    o_ref[...] = acc


def _out_proj(x2d, parts, w_out, g, final, tm=512):
    n, d = x2d.shape
    w = GROUP_W
    part_spec = pl.BlockSpec((tm, w), lambda i: (i, 0))
    return pl.pallas_call(
        functools.partial(_out_proj_kernel, final=final),
        grid=(n // tm,),
        in_specs=[pl.BlockSpec((tm, d), lambda i: (i, 0))] + [part_spec] * 6 + [
            pl.BlockSpec((4 * w, d), lambda i: (0, 0)),
            pl.BlockSpec((1, d), lambda i: (0, 0)),
        ],
        out_specs=pl.BlockSpec((tm, d), lambda i: (i, 0)),
        out_shape=jax.ShapeDtypeStruct((n, d), F32),
        compiler_params=_params(("parallel",)),
        name="out_proj",
    )(x2d, *parts, w_out, g.reshape(1, d))


def _pack_in_proj(w_in_l):
    fox, ssm, nsa, mem = 0, FOX_COLS, FOX_COLS + SSM_COLS, FOX_COLS + SSM_COLS + NSA_COLS
    w = GROUP_W
    q_scale = HEAD_DIM ** -0.5 * LOG2E
    cols = lambda a, n: w_in_l[:, a:a + n]
    kv = lambda slot: nsa + w + slot * NSA_KV_W
    dup = lambda base: [cols(base, HEAD_DIM)] * 2 + [cols(base + HEAD_DIM, HEAD_DIM)] * 2
    main = ([cols(fox, w) * q_scale, cols(fox + w, 3 * w)]
            + [cols(ssm + w, SSM_CONV_DIM), cols(ssm, w)]
            + [cols(nsa, w) * q_scale, cols(nsa + w + 6 * NSA_KV_W + 3 * NSA_HEADS, w)]
            + [cols(mem, 2 * w)]
            + dup(kv(2)) + dup(kv(3)) + dup(kv(4)) + dup(kv(5))
            + [cols(kv(0), 2 * NSA_KV_W)])
    w_main = jnp.concatenate(main, axis=1).astype(BF16)
    assert w_main.shape[1] == P_COLS
    small = []
    for src, n in ((fox + 4 * w, FOX_HEADS), (ssm + w + SSM_CONV_DIM, SSM_HEADS), (nsa + w + 6 * NSA_KV_W, 3 * NSA_HEADS)):
        small += [cols(src, n), jnp.zeros((w_in_l.shape[0], LANES - n), w_in_l.dtype)]
    return w_main, jnp.concatenate(small, axis=1).astype(BF16)


def _pad_lanes(v):
    return jnp.pad(v.astype(F32), (0, LANES - v.shape[0])).reshape(1, LANES)


def _trunk(x, mem, norm_g, w_in, fox_f_bias, ssm_conv_w, ssm_conv_b, ssm_dt_bias, ssm_a_log, ssm_d,
           ssm_norm_g, nsa_cmp_pe, nsa_cmp_w1, nsa_cmp_w2, rel_bias_table, mem_norm_g, w_mem_kv, w_out,
           final_norm_g):
    b, s, d = x.shape
    depth = w_in.shape[0]
    n = b * s
    m_tok = mem.shape[1]
    n_cmp = (s - CMP_BLOCK) // CMP_STRIDE + 1
    n_rows = s // CMP_STRIDE
    assert s % 512 == 0 and s // SEL_BLOCK <= HEAD_DIM and n_rows <= N_CMP_PAD and d == D_MODEL

    tab_flat = rel_bias_table.astype(F32).reshape(-1)
    idx_win, idx_sel, idx_cmp = _bias_indices()
    bwin = _bias_table(tab_flat, idx_win)
    bsel = _bias_table(tab_flat, idx_sel)
    fcmp = _bias_table(tab_flat, idx_cmp)

    cs = np.arange(N_CMP_PAD)[None, :] * CMP_STRIDE
    js = np.arange(SEL_BLOCK)[:, None] * SEL_BLOCK
    overlap_t = ((cs < js + SEL_BLOCK) & (cs + CMP_BLOCK > js) & (np.arange(N_CMP_PAD)[None, :] < n_cmp)
                 & (np.arange(SEL_BLOCK)[:, None] < s // SEL_BLOCK)).astype(np.float32)
    overlap_t = jnp.asarray(overlap_t, BF16)

    x2d = x.reshape(n, d)
    mem2d = mem.reshape(b * m_tok, d)
    half = CMP_STRIDE * HEAD_DIM
    for l in range(depth):
        w_main, w_small = _pack_in_proj(w_in[l])
        p, ps = _norm_proj(x2d, norm_g[l], w_main, w_small)

        qaug, kaug = _fox_cumsum(ps, _pad_lanes(fox_f_bias[l]), b, s)
        o_fox = _fox_attn(p, qaug, kaug, b, s)

        o_ssd = _ssd(p, ps, ssm_conv_w[l].astype(F32), ssm_conv_b[l].reshape(1, -1).astype(F32),
                     _pad_lanes(ssm_dt_bias[l]), _pad_lanes(ssm_a_log[l]),
                     jnp.repeat(ssm_d[l].astype(F32), HEAD_DIM).reshape(1, GROUP_W),
                     ssm_norm_g[l].reshape(1, GROUP_W).astype(F32), b, s)

        kvc = p[:, OFF_CMP_KV:OFF_CMP_KV + 2 * NSA_KV_W]
        r = kvc.reshape(b, n_rows, CMP_STRIDE, 2 * NSA_KV_HEADS, HEAD_DIM).transpose(0, 3, 1, 2, 4)
        r = r.reshape(b, 2 * NSA_KV_HEADS, n_rows, half)
        if n_rows < N_CMP_PAD:
            r = jnp.pad(r, ((0, 0), (0, 0), (0, N_CMP_PAD - n_rows), (0, 0)))
        w1 = nsa_cmp_w1[l].astype(BF16)
        w2dup = jnp.concatenate([nsa_cmp_w2[l], nsa_cmp_w2[l]], axis=-1).astype(BF16)
        pe8 = jnp.broadcast_to(nsa_cmp_pe[l].astype(F32).reshape(2, 1, 2 * half), (2, 8, 2 * half))
        kv_cmp, kv_cmp_t = _compress(r, w1[:, :half], w1[:, half:], w2dup, pe8)
        o_cmp, notsel = _cmp_select(p, ps, kv_cmp, kv_cmp_t, fcmp, overlap_t, b, s)
        o_sel = _sel_attn(tab_flat, p, ps, notsel, bsel, b, s)
        o_win = _win_attn(p, ps, bwin, b, s)

        w_kv = w_mem_kv[l].astype(BF16)
        mem_kv, _ = _norm_proj(mem2d, mem_norm_g[l], w_kv, w_kv[:, :LANES], tm=min(512, b * m_tok))
        o_mem = _mem_attn(p, mem_kv, b, s)

        x2d = _out_proj(x2d, (o_fox, o_ssd, o_cmp, o_sel, o_win, o_mem), w_out[l].astype(BF16),
                        final_norm_g, final=(l == depth - 1))
    return x2d.reshape(b, s, d)


def kernel(x, mem, norm_g, w_in, fox_f_bias, ssm_conv_w, ssm_conv_b, ssm_dt_bias, ssm_a_log, ssm_d, ssm_norm_g,
           nsa_cmp_pe, nsa_cmp_w1, nsa_cmp_w2, rel_bias_table, mem_norm_g, w_mem_kv, w_out, final_norm_g):
    return _trunk(x, mem, norm_g, w_in, fox_f_bias, ssm_conv_w, ssm_conv_b, ssm_dt_bias, ssm_a_log, ssm_d,
                  ssm_norm_g, nsa_cmp_pe, nsa_cmp_w1, nsa_cmp_w2, rel_bias_table, mem_norm_g, w_mem_kv, w_out,
                  final_norm_g)
```

```python
import functools
import math

import numpy as np
import jax
import jax.numpy as jnp
from jax import lax
from jax.experimental import pallas as pl
from jax.experimental.pallas import tpu as pltpu

F32 = jnp.float32
BF16 = jnp.bfloat16
HIGHEST = lax.Precision.HIGHEST

D_MODEL = 1024
GROUP_W = 512
HEAD_DIM = 64
EPS = 1e-6
NEG_INF = -1e30
TINY = 1e-30
LOG2E = math.log2(math.e)

FOX_HEADS = 8
SSM_HEADS = 8
SSM_STATE = 128
SSM_GROUPS = 2
SSM_CONV = 4
SSM_CHUNK = 128
SSM_CONV_DIM = GROUP_W + 2 * SSM_GROUPS * SSM_STATE

NSA_HEADS = 8
NSA_KV_HEADS = 2
NSA_REP = NSA_HEADS // NSA_KV_HEADS
NSA_KV_W = NSA_KV_HEADS * HEAD_DIM
CMP_BLOCK = 32
CMP_STRIDE = 16
CMP_HIDDEN = 2 * HEAD_DIM
SEL_BLOCK = 64
SEL_TOPK = 16
WINDOW = 512
SEL_FORCE = 1e9

MEM_HEADS = 4
MEM_HEAD_DIM = GROUP_W // MEM_HEADS
REL_BUCKETS = 32
REL_MAX_DIST = 128

FOX_COLS = 4 * GROUP_W + FOX_HEADS
SSM_COLS = GROUP_W + SSM_CONV_DIM + SSM_HEADS
NSA_COLS = 2 * GROUP_W + 6 * NSA_KV_W + 3 * NSA_HEADS
MEM_COLS = 2 * GROUP_W

LANES = 128
VMEM_LIMIT = 56 * 1024 * 1024

OFF_FOX_Q, OFF_FOX_K, OFF_FOX_V, OFF_FOX_G = 0, 512, 1024, 1536
OFF_SSM_XBC, OFF_SSM_Z = 2048, 3072
OFF_NSA_Q, OFF_NSA_G = 3584, 4096
OFF_MEM_Q, OFF_MEM_G = 4608, 5120
OFF_SEL_K, OFF_SEL_V, OFF_WIN_K, OFF_WIN_V, OFF_CMP_KV = 5632, 5888, 6144, 6400, 6656
P_COLS = 6912
PS_COLS = 3 * LANES

T_ATT = 256
SEL_T, SEL_TK = 512, 256
WIN_T, WIN_TK = 512, 256
N_CMP_PAD = 256


def _params(sem):
    return pltpu.CompilerParams(dimension_semantics=sem, vmem_limit_bytes=VMEM_LIMIT)


def _t5_bucket_np(dist):
    n = np.maximum(dist, 0)
    max_exact = REL_BUCKETS // 2
    nf = np.maximum(n, 1).astype(np.float32)
    large = max_exact + (np.log(nf / np.float32(max_exact)) / np.float32(math.log(REL_MAX_DIST / max_exact))
                         * np.float32(REL_BUCKETS - max_exact)).astype(np.int32)
    large = np.minimum(large, REL_BUCKETS - 1)
    return np.where(n < max_exact, n, large).astype(np.int32)


def _silu(x):
    h = 0.5 * x
    return h + h * jnp.tanh(h)


def _norm_proj_kernel(x_ref, g_ref, w_ref, ws_ref, p_ref, ps_ref, *, chunk):
    x = x_ref[...]
    ms = jnp.mean(x * x, axis=-1, keepdims=True)
    h = ((x * lax.rsqrt(ms + EPS)) * g_ref[...]).astype(BF16)
    ncol = p_ref.shape[1]
    for c0 in range(0, ncol, chunk):
        c1 = min(c0 + chunk, ncol)
        p_ref[:, c0:c1] = jnp.dot(h, w_ref[:, c0:c1], preferred_element_type=F32).astype(p_ref.dtype)
    ps_ref[...] = jnp.dot(h, ws_ref[...], preferred_element_type=F32)


def _norm_proj(x2d, g, w_main, w_small, tm=512):
    n, d = x2d.shape
    pc, sc = w_main.shape[1], w_small.shape[1]
    return pl.pallas_call(
        functools.partial(_norm_proj_kernel, chunk=512),
        grid=(n // tm,),
        in_specs=[
            pl.BlockSpec((tm, d), lambda i: (i, 0)),
            pl.BlockSpec((1, d), lambda i: (0, 0)),
            pl.BlockSpec((d, pc), lambda i: (0, 0)),
            pl.BlockSpec((d, sc), lambda i: (0, 0)),
        ],
        out_specs=[
            pl.BlockSpec((tm, pc), lambda i: (i, 0)),
            pl.BlockSpec((tm, sc), lambda i: (i, 0)),
        ],
        out_shape=[jax.ShapeDtypeStruct((n, pc), BF16), jax.ShapeDtypeStruct((n, sc), F32)],
        compiler_params=_params(("parallel",)),
        name="norm_proj",
    )(x2d, g.reshape(1, d), w_main, w_small)


def _bias_table_kernel(tab_ref, idx_ref, o_ref):
    h = pl.program_id(0)
    idx = idx_ref[...]
    acc = jnp.full(idx.shape, NEG_INF, F32)
    for b in range(REL_BUCKETS):
        acc = jnp.where(idx == b, tab_ref[b * NSA_HEADS + h] * LOG2E, acc)
    o_ref[0] = acc


def _bias_table(tab_flat, idx):
    r, c = idx.shape
    return pl.pallas_call(
        _bias_table_kernel,
        grid=(NSA_HEADS,),
        in_specs=[
            pl.BlockSpec(memory_space=pltpu.SMEM),
            pl.BlockSpec((r, c), lambda h: (0, 0)),
        ],
        out_specs=pl.BlockSpec((1, r, c), lambda h: (h, 0, 0)),
        out_shape=jax.ShapeDtypeStruct((NSA_HEADS, r, c), F32),
        compiler_params=_params(("arbitrary",)),
        name="t5_bias_table",
    )(tab_flat, jnp.asarray(idx))


def _bias_indices():
    t = T_ATT
    i = np.arange(t)[:, None]
    d = np.arange(WIN_T)[None, :] + WINDOW - np.arange(WINDOW + WIN_T)[:, None]
    idx_win = np.where((d >= 0) & (d < WINDOW), _t5_bucket_np(d), -1).astype(np.int32)
    kr = np.arange((SEL_T // SEL_TK + 1) * SEL_TK)[:, None] - SEL_TK
    ds = np.arange(SEL_T)[None, :] - kr
    idx_sel = np.where(ds >= 0, _t5_bucket_np(ds), -1).astype(np.int32)
    u = np.arange(2 * N_CMP_PAD)[:, None] - N_CMP_PAD
    dc = np.arange(t)[None, :] - CMP_STRIDE * u - (CMP_BLOCK - 1)
    idx_cmp = np.where(dc >= 0, _t5_bucket_np(dc), -1).astype(np.int32)
    return idx_win, idx_sel, idx_cmp


def _tri_lower(n):
    r = lax.broadcasted_iota(jnp.int32, (n, n), 0)
    c = lax.broadcasted_iota(jnp.int32, (n, n), 1)
    return (r >= c).astype(F32)


N_SPLIT = 3


def _fox_aug_lane(h, i):
    return LANES * (h // 2) + (HEAD_DIM if h % 2 == 0 else 0) + i


def _fox_aug_consts():
    pq = np.zeros((N_SPLIT * LANES, GROUP_W), np.float32)
    pk = np.zeros((N_SPLIT * LANES, GROUP_W), np.float32)
    oq = np.zeros((1, GROUP_W), np.float32)
    ok = np.zeros((1, GROUP_W), np.float32)
    for h in range(FOX_HEADS):
        for i in range(N_SPLIT):
            pq[i * LANES + h, _fox_aug_lane(h, i)] = 1.0
            pk[i * LANES + h, _fox_aug_lane(h, N_SPLIT + i)] = -1.0
            oq[0, _fox_aug_lane(h, N_SPLIT + i)] = 1.0
            ok[0, _fox_aug_lane(h, i)] = 1.0
    return pq, pk, oq, ok


def _fox_cumsum_kernel(f_ref, b_ref, pq_ref, pk_ref, oq_ref, ok_ref, qa_ref, ka_ref, carry_ref, *, ts):
    @pl.when(pl.program_id(1) == 0)
    def _():
        carry_ref[...] = jnp.zeros_like(carry_ref)

    z = f_ref[...] + b_ref[...]
    logf = (jnp.minimum(z, 0.0) - jnp.log(1.0 + jnp.exp(-jnp.abs(z)))) * LOG2E
    tri = _tri_lower(LANES)
    carry = carry_ref[...]
    for c in range(ts // LANES):
        rows = slice(c * LANES, (c + 1) * LANES)
        cs = jnp.dot(tri, logf[rows], precision=HIGHEST, preferred_element_type=F32) + carry
        carry = cs[LANES - 1:LANES, :]
        pieces, rest = [], cs
        for _ in range(N_SPLIT):
            piece = rest.astype(BF16)
            pieces.append(piece)
            rest = rest - piece.astype(F32)
        cat = jnp.concatenate(pieces, axis=1)
        qa_ref[rows, :] = (jnp.dot(cat, pq_ref[...], preferred_element_type=F32) + oq_ref[...]).astype(BF16)
        ka_ref[rows, :] = (jnp.dot(cat, pk_ref[...], preferred_element_type=F32) + ok_ref[...]).astype(BF16)
    carry_ref[...] = carry


def _fox_cumsum(ps, f_bias_pad, b, s, ts=512):
    ns = s // ts
    pq, pk, oq, ok = _fox_aug_consts()
    const = lambda bi, si: (0, 0)
    return pl.pallas_call(
        functools.partial(_fox_cumsum_kernel, ts=ts),
        grid=(b, ns),
        in_specs=[
            pl.BlockSpec((ts, LANES), lambda bi, si: (bi * ns + si, 0)),
            pl.BlockSpec((1, LANES), const),
            pl.BlockSpec(pq.shape, const),
            pl.BlockSpec(pk.shape, const),
            pl.BlockSpec(oq.shape, const),
            pl.BlockSpec(ok.shape, const),
        ],
        out_specs=[
            pl.BlockSpec((ts, GROUP_W), lambda bi, si: (bi * ns + si, 0)),
            pl.BlockSpec((ts, GROUP_W), lambda bi, si: (bi * ns + si, 0)),
        ],
        out_shape=[jax.ShapeDtypeStruct((b * s, GROUP_W), BF16), jax.ShapeDtypeStruct((b * s, GROUP_W), BF16)],
        scratch_shapes=[pltpu.VMEM((1, LANES), F32)],
        compiler_params=_params(("parallel", "arbitrary")),
        name="fox_cumsum",
    )(ps, f_bias_pad, jnp.asarray(pq, BF16), jnp.asarray(pk, BF16), jnp.asarray(oq), jnp.asarray(ok))


ONES_ROWS = 16


def _vt_rows(h):
    return slice(0, HEAD_DIM + ONES_ROWS) if h % 2 == 0 else slice(HEAD_DIM - ONES_ROWS, LANES)


def _store_vt(vt_ref, i, rows, v_pair):
    vt = v_pair.astype(F32).T.astype(BF16)
    ones = jnp.ones((HEAD_DIM, v_pair.shape[0]), BF16)
    vt_ref[i, 0:HEAD_DIM, rows] = vt[0:HEAD_DIM]
    vt_ref[i, HEAD_DIM:LANES, rows] = ones
    vt_ref[i + 1, 0:HEAD_DIM, rows] = ones
    vt_ref[i + 1, HEAD_DIM:LANES, rows] = vt[HEAD_DIM:LANES]


def _half_masks(rows):
    lo = jnp.where(lax.broadcasted_iota(jnp.int32, (rows, LANES), 1) < HEAD_DIM, 1.0, 0.0)
    return lo.astype(BF16), (1.0 - lo).astype(BF16)


def _fox_attn_kernel(q_ref, qa_ref, k_ref, ka_ref, v_ref, gate_ref, o_ref, qs_ref, kk_ref, vt_ref, m_ref, acc_ref,
                     pre_ref, *, t, tk, ahead):
    qi = pl.program_id(1)
    lo_q, hi_q = _half_masks(t)
    lo = lax.broadcasted_iota(jnp.int32, (t, LANES), 1) < HEAD_DIM
    cm = lax.broadcasted_iota(jnp.int32, (tk, t), 0) - lax.broadcasted_iota(jnp.int32, (tk, t), 1)

    @pl.when(qi == 0)
    def _():
        lo_k, hi_k = _half_masks(tk)

        def merge(j, c):
            rows = pl.ds(pl.multiple_of(j * tk, tk), tk)
            for hp in range(FOX_HEADS // 2):
                ls = slice(hp * LANES, (hp + 1) * LANES)
                kp, ka, vp = k_ref[rows, ls], ka_ref[rows, ls], v_ref[rows, ls]
                kk_ref[2 * hp, rows, :] = kp * lo_k + ka * hi_k
                kk_ref[2 * hp + 1, rows, :] = ka * lo_k + kp * hi_k
                _store_vt(vt_ref, 2 * hp, rows, vp)
            return c

        lax.fori_loop(0, k_ref.shape[0] // tk, merge, 0)

    for hp in range(FOX_HEADS // 2):
        ls = slice(hp * LANES, (hp + 1) * LANES)
        qp, qa = q_ref[:, ls], qa_ref[:, ls]
        qs_ref[2 * hp] = qp * lo_q + qa * hi_q
        qs_ref[2 * hp + 1] = qa * lo_q + qp * hi_q
    m_ref[...] = jnp.full(m_ref.shape, NEG_INF, F32)
    acc_ref[...] = jnp.zeros_like(acc_ref)

    def rows_of(j):
        return pl.ds(pl.multiple_of(j * tk, tk), tk)

    def scores(j, h):
        return lax.dot_general(kk_ref[h, rows_of(j), :], qs_ref[h], (((1,), (1,)), ((), ())),
                               preferred_element_type=F32)

    def tile(j, diag, j_next):
        queue = [pre_ref[i] for i in range(ahead)]
        for h in range(FOX_HEADS):
            if h + ahead < FOX_HEADS:
                queue.append(scores(j, h + ahead))
            elif j_next is not None:
                pre_ref[h + ahead - FOX_HEADS] = scores(j_next, h + ahead - FOX_HEADS)
            s = queue.pop(0)
            if diag:
                s = jnp.where(cm <= qi * t - j * tk, s, NEG_INF)
            m_old = m_ref[h]
            m_new = jnp.maximum(m_old, jnp.max(s, axis=0, keepdims=True))
            m_ref[h] = m_new
            p = jnp.exp2(s - m_new).astype(BF16)
            vr = _vt_rows(h)
            acc_ref[h, vr] = jnp.exp2(m_old - m_new) * acc_ref[h, vr] + jnp.dot(vt_ref[h, vr, rows_of(j)], p,
                                                                                preferred_element_type=F32)

    jd = (qi * t) // tk
    n_diag = max(t // tk, 1)
    for i in range(ahead):
        pre_ref[i] = scores(0, i)

    def body(j, c):
        tile(j, False, j + 1)
        return c

    lax.fori_loop(0, jd, body, 0)
    for dj in range(n_diag):
        tile(jd + dj, True, jd + dj + 1 if dj + 1 < n_diag else None)

    for hp in range(FOX_HEADS // 2):
        ls = slice(hp * LANES, (hp + 1) * LANES)
        a0, a1 = acc_ref[2 * hp].T, acc_ref[2 * hp + 1].T
        o = jnp.where(lo, a0 / jnp.maximum(a0[:, HEAD_DIM:HEAD_DIM + 1], TINY),
                      a1 / jnp.maximum(a1[:, HEAD_DIM - 1:HEAD_DIM], TINY))
        o_ref[:, ls] = (o * _silu(gate_ref[:, ls].astype(F32))).astype(o_ref.dtype)


def _fox_attn(p, qaug, kaug, b, s, t=512, tk=256, ahead=2):
    nq = s // t
    w = GROUP_W
    return pl.pallas_call(
        functools.partial(_fox_attn_kernel, t=t, tk=tk, ahead=ahead),
        grid=(b, nq),
        in_specs=[
            pl.BlockSpec((t, w), lambda bi, qi: (bi * nq + qi, OFF_FOX_Q // w)),
            pl.BlockSpec((t, w), lambda bi, qi: (bi * nq + qi, 0)),
            pl.BlockSpec((s, w), lambda bi, qi: (bi, OFF_FOX_K // w)),
            pl.BlockSpec((s, w), lambda bi, qi: (bi, 0)),
            pl.BlockSpec((s, w), lambda bi, qi: (bi, OFF_FOX_V // w)),
            pl.BlockSpec((t, w), lambda bi, qi: (bi * nq + qi, OFF_FOX_G // w)),
        ],
        out_specs=pl.BlockSpec((t, w), lambda bi, qi: (bi * nq + qi, 0)),
        out_shape=jax.ShapeDtypeStruct((b * s, w), BF16),
        scratch_shapes=[
            pltpu.VMEM((FOX_HEADS, t, LANES), BF16),
            pltpu.VMEM((FOX_HEADS, s, LANES), BF16),
            pltpu.VMEM((FOX_HEADS, LANES, s), BF16),
            pltpu.VMEM((FOX_HEADS, 1, t), F32),
            pltpu.VMEM((FOX_HEADS, LANES, t), F32),
            pltpu.VMEM((ahead, tk, t), F32),
        ],
        compiler_params=_params(("parallel", "arbitrary")),
        name="fox_attn",
    )(p, qaug, p, kaug, p, p)


def _dot_split(lhs_f32, rhs_b):
    out, rest = None, lhs_f32
    for _ in range(N_SPLIT):
        piece = rest.astype(BF16)
        rest = rest - piece.astype(F32)
        term = jnp.dot(piece, rhs_b, preferred_element_type=F32)
        out = term if out is None else out + term
    return out


def _ssd_kernel(z_ref, xbc_ref, dt_ref, cw_ref, cb_ref, dtb_ref, alog_ref, dsk_ref, ng_ref, o_ref,
                xpad_ref, xc_ref, state_ref, y_ref, *, nch):
    q = SSM_CHUNK
    rows_all = nch * q
    halo = 8

    @pl.when(pl.program_id(1) == 0)
    def _():
        xpad_ref[0:halo, :] = jnp.zeros((halo, SSM_CONV_DIM), F32)
        state_ref[...] = jnp.zeros_like(state_ref)

    xpad_ref[halo:halo + rows_all, :] = xbc_ref[...].astype(F32)
    y = cb_ref[...]
    for k in range(SSM_CONV):
        off = halo - (SSM_CONV - 1) + k
        y = y + cw_ref[k:k + 1, :] * xpad_ref[off:off + rows_all, :]
    xpad_ref[0:halo, :] = xpad_ref[rows_all:rows_all + halo, :]
    xc_ref[...] = _silu(y)

    x_dt = dt_ref[...] + dtb_ref[...]
    dt_all = jnp.maximum(x_dt, 0.0) + jnp.log(1.0 + jnp.exp(-jnp.abs(x_dt)))
    a_all = dt_all * (-jnp.exp(alog_ref[...]))
    tri_t = (lax.broadcasted_iota(jnp.int32, (q, q), 0) <= lax.broadcasted_iota(jnp.int32, (q, q), 1)).astype(BF16)
    er = lax.broadcasted_iota(jnp.int32, (LANES, GROUP_W), 0)
    ec = lax.broadcasted_iota(jnp.int32, (LANES, GROUP_W), 1)
    expand = jnp.where(ec // HEAD_DIM == er, 1.0, 0.0).astype(BF16)
    row = lax.broadcasted_iota(jnp.int32, (q, q), 0)
    col = lax.broadcasted_iota(jnp.int32, (q, q), 1)
    causal = row >= col
    lo = lax.broadcasted_iota(jnp.int32, (q, LANES), 1) < HEAD_DIM
    gw = GROUP_W // SSM_GROUPS
    hpg = SSM_HEADS // SSM_GROUPS

    for c in range(nch):
        rs = slice(c * q, (c + 1) * q)
        xs = xc_ref[rs, :GROUP_W]
        dt = dt_all[rs]
        acs_t = _dot_split(a_all[rs].T, tri_t)
        acs = acs_t.T
        stacked = jnp.concatenate([dt, jnp.exp(acs), jnp.exp(acs[q - 1:q, :] - acs)], axis=0)
        full = _dot_split(stacked, expand)
        dt_full, eacs_full, dec_full = full[0:q], full[q:2 * q], full[2 * q:3 * q]
        xdt = xs * dt_full
        xdt_b = xdt.astype(BF16)
        xdec_b = (xdt * dec_full).astype(BF16)

        for g in range(SSM_GROUPS):
            bm = xc_ref[rs, GROUP_W + g * SSM_STATE:GROUP_W + (g + 1) * SSM_STATE]
            cm = xc_ref[rs, GROUP_W + (SSM_GROUPS + g) * SSM_STATE:GROUP_W + (SSM_GROUPS + g + 1) * SSM_STATE]
            bm_b = bm.astype(BF16)
            cm_b = cm.astype(BF16)
            gs = slice(g * gw, (g + 1) * gw)
            cbg = lax.dot_general(cm_b, bm_b, (((1,), (1,)), ((), ())), preferred_element_type=F32)
            st = state_ref[:, gs]
            y_off = jnp.dot(cm_b, st.astype(BF16), preferred_element_type=F32) * eacs_full[:, gs]
            cst = jnp.dot(bm.T.astype(BF16), xdec_b[:, gs], preferred_element_type=F32)
            state_ref[:, gs] = st * eacs_full[q - 1:q, gs] + cst
            for hp in range(hpg // 2):
                ls = slice(g * gw + hp * LANES, g * gw + (hp + 1) * LANES)
                yd = []
                for e in range(2):
                    h = g * hpg + 2 * hp + e
                    seg = jnp.exp(jnp.where(causal, acs[:, h:h + 1] - acs_t[h:h + 1, :], NEG_INF))
                    yd.append(jnp.dot((cbg * seg).astype(BF16), xdt_b[:, ls], preferred_element_type=F32))
                y_ref[:, ls] = jnp.where(lo, yd[0], yd[1]) + y_off[:, hp * LANES:(hp + 1) * LANES]

        yz = (y_ref[...] + xs * dsk_ref[...]) * _silu(z_ref[rs, :].astype(F32))
        for g in range(SSM_GROUPS):
            gs = slice(g * gw, (g + 1) * gw)
            blk = yz[:, gs]
            ms = jnp.mean(blk * blk, axis=-1, keepdims=True)
            o_ref[rs, gs] = ((blk * lax.rsqrt(ms + EPS)) * ng_ref[:, gs]).astype(o_ref.dtype)


def _ssd(p, ps, conv_w, conv_b, dt_bias_pad, a_log_pad, d_full, norm_g, b, s, nch=4):
    q = SSM_CHUNK
    rows = nch * q
    nc = s // rows
    row = lambda bi, ci: (bi * nc + ci)
    const = lambda bi, ci: (0, 0)
    return pl.pallas_call(
        functools.partial(_ssd_kernel, nch=nch),
        grid=(b, nc),
        in_specs=[
            pl.BlockSpec((rows, GROUP_W), lambda bi, ci: (row(bi, ci), OFF_SSM_Z // GROUP_W)),
            pl.BlockSpec((rows, SSM_CONV_DIM), lambda bi, ci: (row(bi, ci), OFF_SSM_XBC // SSM_CONV_DIM)),
            pl.BlockSpec((rows, LANES), lambda bi, ci: (row(bi, ci), 1)),
            pl.BlockSpec((SSM_CONV, SSM_CONV_DIM), const),
            pl.BlockSpec((1, SSM_CONV_DIM), const),
            pl.BlockSpec((1, LANES), const),
            pl.BlockSpec((1, LANES), const),
            pl.BlockSpec((1, GROUP_W), const),
            pl.BlockSpec((1, GROUP_W), const),
        ],
        out_specs=pl.BlockSpec((rows, GROUP_W), lambda bi, ci: (row(bi, ci), 0)),
        out_shape=jax.ShapeDtypeStruct((b * s, GROUP_W), BF16),
        scratch_shapes=[
            pltpu.VMEM((rows + 8, SSM_CONV_DIM), F32),
            pltpu.VMEM((rows, SSM_CONV_DIM), F32),
            pltpu.VMEM((SSM_STATE, GROUP_W), F32),
            pltpu.VMEM((q, GROUP_W), F32),
        ],
        compiler_params=_params(("parallel", "arbitrary")),
        name="ssd",
    )(p, p, ps, conv_w, conv_b, dt_bias_pad, a_log_pad, d_full, norm_g)


def _compress_kernel(r_ref, w1a_ref, w1b_ref, w2_ref, pe_ref, o_ref, ot_ref):
    half = CMP_STRIDE * HEAD_DIM
    r = r_ref[0, 0]
    w1a = w1a_ref[0]
    w1b = w1b_ref[0]
    a = jnp.dot(r, w1a, preferred_element_type=F32)
    bm = jnp.dot(r, w1b, preferred_element_type=F32)
    pe = pe_ref[0].astype(BF16)
    const = (jnp.dot(pe[:, :half], w1a, preferred_element_type=F32)
             + jnp.dot(pe[:, half:], w1b, preferred_element_type=F32))[0:1]
    n = bm.shape[0]
    h = _silu(a + pltpu.roll(bm, n - 1, 0) + const)
    o = jnp.dot(h.astype(BF16), w2_ref[0], preferred_element_type=F32)
    o_ref[0, 0] = o.astype(o_ref.dtype)
    ot_ref[0, 0] = o.T.astype(ot_ref.dtype)


def _compress(r, w1a, w1b, w2dup, pe8):
    b, nslot, nr, width = r.shape
    return pl.pallas_call(
        _compress_kernel,
        grid=(b, nslot),
        in_specs=[
            pl.BlockSpec((1, 1, nr, width), lambda bi, si: (bi, si, 0, 0)),
            pl.BlockSpec((1, width, CMP_HIDDEN), lambda bi, si: (si // NSA_KV_HEADS, 0, 0)),
            pl.BlockSpec((1, width, CMP_HIDDEN), lambda bi, si: (si // NSA_KV_HEADS, 0, 0)),
            pl.BlockSpec((1, CMP_HIDDEN, LANES), lambda bi, si: (si // NSA_KV_HEADS, 0, 0)),
            pl.BlockSpec((1, 8, 2 * width), lambda bi, si: (si // NSA_KV_HEADS, 0, 0)),
        ],
        out_specs=[pl.BlockSpec((1, 1, nr, LANES), lambda bi, si: (bi, si, 0, 0)),
                   pl.BlockSpec((1, 1, LANES, nr), lambda bi, si: (bi, si, 0, 0))],
        out_shape=[jax.ShapeDtypeStruct((b, nslot, nr, LANES), BF16),
                   jax.ShapeDtypeStruct((b, nslot, LANES, nr), BF16)],
        compiler_params=_params(("parallel", "arbitrary")),
        name="nsa_compress",
    )(r, w1a, w1b, w2dup, pe8)


def _cmp_select_kernel(q_ref, kv_ref, vt_ref, gl_ref, gate_ref, fc_ref, ovt_ref, o_ref, ns_ref):
    t = T_ATT
    ncp = kv_ref.shape[2]
    qi = pl.program_id(1)
    t0 = qi * t
    lo = lax.broadcasted_iota(jnp.int32, (t, LANES), 1) < HEAD_DIM
    zero = jnp.zeros((t, LANES), BF16)
    start = pl.multiple_of(ncp - qi * (t // CMP_STRIDE), CMP_STRIDE)
    gl = gl_ref[...]

    nsel = SEL_BLOCK
    jrow = lax.broadcasted_iota(jnp.int32, (nsel, t), 0)
    cur = (t0 + lax.broadcasted_iota(jnp.int32, (nsel, t), 1)) // SEL_BLOCK
    forced = (jrow == 0) | (jrow == cur) | (jrow == cur - 1)
    past = jrow <= cur

    for g in range(NSA_KV_HEADS):
        kc = kv_ref[0, g]
        vct = vt_ref[0, NSA_KV_HEADS + g]
        psum = jnp.zeros((ncp, t), F32)
        outs = []

        def scores(r):
            h = g * NSA_REP + r
            qp = q_ref[:, (h // 2) * LANES:(h // 2 + 1) * LANES]
            qe = jnp.where(lo, qp, zero) if r % 2 == 0 else jnp.where(lo, zero, qp)
            return lax.dot_general(kc, qe, (((1,), (1,)), ((), ())), preferred_element_type=F32)

        queue = [scores(0)]
        for r in range(NSA_REP):
            h = g * NSA_REP + r
            if r + 1 < NSA_REP:
                queue.append(scores(r + 1))
            s = fc_ref[h, pl.ds(start, ncp), :] + queue.pop(0)
            m = jnp.maximum(jnp.max(s, axis=0, keepdims=True), 0.1 * NEG_INF)
            e = jnp.exp2(s - m)
            pr = e * (1.0 / jnp.maximum(jnp.sum(e, axis=0, keepdims=True), TINY))
            psum = psum + pr
            oc = jnp.dot(vct, pr.astype(BF16), preferred_element_type=F32)
            outs.append(oc.T * jax.nn.sigmoid(gl[:, h:h + 1]))
        for hp in range(NSA_REP // 2):
            ls = slice((g * NSA_REP // 2 + hp) * LANES, (g * NSA_REP // 2 + hp + 1) * LANES)
            o = jnp.where(lo, outs[2 * hp], outs[2 * hp + 1])
            o_ref[:, ls] = (o * _silu(gate_ref[:, ls].astype(F32))).astype(o_ref.dtype)

        imp_t = jnp.zeros((nsel, t), F32)
        rest = psum
        for _ in range(N_SPLIT):
            piece = rest.astype(BF16)
            rest = rest - piece.astype(F32)
            imp_t = imp_t + jnp.dot(ovt_ref[...], piece, preferred_element_type=F32)
        imp_t = jnp.where(past, jnp.where(forced, SEL_FORCE, imp_t), -SEL_FORCE)
        sub = 8
        sub_row = lax.broadcasted_iota(jnp.int32, (sub, t), 0)
        rows = [imp_t[k * sub:(k + 1) * sub] for k in range(nsel // sub)]
        rank = [jnp.zeros((sub, t), F32) for _ in rows]
        for i in range(nsel):
            bi = imp_t[i:i + 1, :]
            for k, x in enumerate(rows):
                if k * sub > i:
                    ahead = jnp.where(bi >= x, 1.0, 0.0)
                elif k * sub + sub - 1 <= i:
                    ahead = jnp.where(bi > x, 1.0, 0.0)
                else:
                    ahead = jnp.where(sub_row > i - k * sub, jnp.where(bi >= x, 1.0, 0.0),
                                      jnp.where(bi > x, 1.0, 0.0))
                rank[k] = rank[k] + ahead
        rank = jnp.concatenate(rank, axis=0)
        notsel = jnp.where((rank < float(SEL_TOPK)) & past, 0.0, 1.0)
        ns2 = jnp.concatenate([notsel, notsel], axis=0).T
        ns_ref[:, g * LANES:(g + 1) * LANES] = ns2.astype(ns_ref.dtype)


def _cmp_select(p, ps, kvc, kvc_t, fc, overlap_t, b, s):
    t = T_ATT
    nq = s // t
    w = GROUP_W
    ncp = kvc.shape[2]
    return pl.pallas_call(
        _cmp_select_kernel,
        grid=(b, nq),
        in_specs=[
            pl.BlockSpec((t, w), lambda bi, qi: (bi * nq + qi, OFF_NSA_Q // w)),
            pl.BlockSpec((1, 2 * NSA_KV_HEADS, ncp, LANES), lambda bi, qi: (bi, 0, 0, 0)),
            pl.BlockSpec((1, 2 * NSA_KV_HEADS, LANES, ncp), lambda bi, qi: (bi, 0, 0, 0)),
            pl.BlockSpec((t, LANES), lambda bi, qi: (bi * nq + qi, 2)),
            pl.BlockSpec((t, w), lambda bi, qi: (bi * nq + qi, OFF_NSA_G // w)),
            pl.BlockSpec((NSA_HEADS, 2 * ncp, t), lambda bi, qi: (0, 0, 0)),
            pl.BlockSpec((SEL_BLOCK, ncp), lambda bi, qi: (0, 0)),
        ],
        out_specs=[
            pl.BlockSpec((t, w), lambda bi, qi: (bi * nq + qi, 0)),
            pl.BlockSpec((t, NSA_KV_HEADS * LANES), lambda bi, qi: (bi * nq + qi, 0)),
        ],
        out_shape=[jax.ShapeDtypeStruct((b * s, w), BF16),
                   jax.ShapeDtypeStruct((b * s, NSA_KV_HEADS * LANES), BF16)],
        compiler_params=_params(("parallel", "arbitrary")),
        name="nsa_cmp_select",
    )(p, kvc, kvc_t, ps, p, fc, overlap_t)


def _sel_attn_kernel(tab_ref, q_ref, ns_ref, k_ref, v_ref, gl_ref, gate_ref, bs_ref, o_ref,
                     qa_ref, kk_ref, vt_ref, m_ref, acc_ref, pre_ref, *, t, tk, ahead):
    qi = pl.program_id(1)
    lo_q, hi_q = _half_masks(t)
    lo = lax.broadcasted_iota(jnp.int32, (t, LANES), 1) < HEAD_DIM
    n_near = t // tk + 1

    @pl.when(qi == 0)
    def _():
        lane = lax.broadcasted_iota(jnp.int32, (tk, LANES), 1)
        krow = lax.broadcasted_iota(jnp.int32, (tk, LANES), 0)
        lo_f = jnp.where(lane < HEAD_DIM, 1.0, 0.0)
        hi_f = 1.0 - lo_f
        lo_k, hi_k = lo_f.astype(BF16), hi_f.astype(BF16)

        def merge(j, c):
            ks = pl.multiple_of(j * tk, tk)
            rows = pl.ds(ks, tk)
            hot = jnp.where((lane % HEAD_DIM) == (ks + krow) // SEL_BLOCK, NEG_INF, 0.0)
            for g in range(NSA_KV_HEADS):
                gls = slice(g * LANES, (g + 1) * LANES)
                kd, vd = k_ref[rows, gls], v_ref[rows, gls]
                kk_ref[2 * g, rows, :] = kd * lo_k + (hot * hi_f).astype(BF16)
                kk_ref[2 * g + 1, rows, :] = (hot * lo_f).astype(BF16) + kd * hi_k
                _store_vt(vt_ref, 2 * g, rows, vd)
            return c

        lax.fori_loop(0, k_ref.shape[0] // tk, merge, 0)

    for h in range(NSA_HEADS):
        g = h // NSA_REP
        qp = q_ref[:, (h // 2) * LANES:(h // 2 + 1) * LANES]
        ns = ns_ref[:, g * LANES:(g + 1) * LANES]
        qa_ref[h] = qp * lo_q + ns * hi_q if h % 2 == 0 else ns * lo_q + qp * hi_q
    m_ref[...] = jnp.full(m_ref.shape, NEG_INF, F32)
    acc_ref[...] = jnp.zeros_like(acc_ref)

    kv = lambda h: 2 * (h // NSA_REP) + h % 2

    def rows_of(j):
        return pl.ds(pl.multiple_of(j * tk, tk), tk)

    def scores(j, h):
        return lax.dot_general(kk_ref[kv(h), rows_of(j), :], qa_ref[h], (((1,), (1,)), ((), ())),
                               preferred_element_type=F32)

    def tile(j, near, j_next):
        rows = rows_of(j)
        queue = [pre_ref[i] for i in range(ahead)]
        for h in range(NSA_HEADS):
            if h + ahead < NSA_HEADS:
                queue.append(scores(j, h + ahead))
            elif j_next is not None:
                pre_ref[h + ahead - NSA_HEADS] = scores(j_next, h + ahead - NSA_HEADS)
            s = queue.pop(0)
            m_old = m_ref[h]
            if near is None:
                far = tab_ref[(REL_BUCKETS - 1) * NSA_HEADS + h] * LOG2E
                m_new = jnp.maximum(m_old, jnp.max(s, axis=0, keepdims=True) + far)
                p = jnp.exp2(s - (m_new - far)).astype(BF16)
            else:
                s = bs_ref[h, near * tk:(near + 1) * tk, :] + s
                m_new = jnp.maximum(m_old, jnp.max(s, axis=0, keepdims=True))
                p = jnp.exp2(s - m_new).astype(BF16)
            m_ref[h] = m_new
            vr = _vt_rows(h)
            acc_ref[h, vr] = jnp.exp2(m_old - m_new) * acc_ref[h, vr] + jnp.dot(vt_ref[kv(h), vr, rows], p,
                                                                                preferred_element_type=F32)

    jd = (qi * t) // tk
    for i in range(ahead):
        pre_ref[i] = scores(0, i)

    def body(j, c):
        tile(j, None, j + 1)
        return c

    lax.fori_loop(0, jnp.maximum(jd - 1, 0), body, 0)

    @pl.when(qi > 0)
    def _():
        tile(jd - 1, 0, jd)

    for dj in range(n_near - 1):
        tile(jd + dj, 1 + dj, jd + dj + 1 if dj + 2 < n_near else None)

    gl = gl_ref[...]
    for hp in range(NSA_HEADS // 2):
        ls = slice(hp * LANES, (hp + 1) * LANES)
        a0, a1 = acc_ref[2 * hp].T, acc_ref[2 * hp + 1].T
        c0 = NSA_HEADS + 2 * hp
        o = jnp.where(lo, a0 / jnp.maximum(a0[:, HEAD_DIM:HEAD_DIM + 1], TINY) * jax.nn.sigmoid(gl[:, c0:c0 + 1]),
                      a1 / jnp.maximum(a1[:, HEAD_DIM - 1:HEAD_DIM], TINY) * jax.nn.sigmoid(gl[:, c0 + 1:c0 + 2]))
        o_ref[:, ls] = (o * _silu(gate_ref[:, ls].astype(F32))).astype(o_ref.dtype)


def _sel_attn(tab_flat, p, ps, notsel, bsel, b, s, t=512, tk=256, ahead=2):
    nq = s // t
    w = GROUP_W
    kw = NSA_KV_HEADS * LANES
    n_near = t // tk + 1
    return pl.pallas_call(
        functools.partial(_sel_attn_kernel, t=t, tk=tk, ahead=ahead),
        grid=(b, nq),
        in_specs=[
            pl.BlockSpec(memory_space=pltpu.SMEM),
            pl.BlockSpec((t, w), lambda bi, qi: (bi * nq + qi, OFF_NSA_Q // w)),
            pl.BlockSpec((t, kw), lambda bi, qi: (bi * nq + qi, 0)),
            pl.BlockSpec((s, kw), lambda bi, qi: (bi, OFF_SEL_K // kw)),
            pl.BlockSpec((s, kw), lambda bi, qi: (bi, OFF_SEL_V // kw)),
            pl.BlockSpec((t, LANES), lambda bi, qi: (bi * nq + qi, 2)),
            pl.BlockSpec((t, w), lambda bi, qi: (bi * nq + qi, OFF_NSA_G // w)),
            pl.BlockSpec((NSA_HEADS, n_near * tk, t), lambda bi, qi: (0, 0, 0), pipeline_mode=pl.Buffered(1)),
        ],
        out_specs=pl.BlockSpec((t, w), lambda bi, qi: (bi * nq + qi, 0)),
        out_shape=jax.ShapeDtypeStruct((b * s, w), BF16),
        scratch_shapes=[
            pltpu.VMEM((NSA_HEADS, t, LANES), BF16),
            pltpu.VMEM((2 * NSA_KV_HEADS, s, LANES), BF16),
            pltpu.VMEM((2 * NSA_KV_HEADS, LANES, s), BF16),
            pltpu.VMEM((NSA_HEADS, 1, t), F32),
            pltpu.VMEM((NSA_HEADS, LANES, t), F32),
            pltpu.VMEM((ahead, tk, t), F32),
        ],
        compiler_params=_params(("parallel", "arbitrary")),
        name="nsa_sel_attn",
    )(tab_flat, p, notsel, p, p, ps, p, bsel)


def _win_attn_kernel(q_ref, k_ref, v_ref, gl_ref, gate_ref, bw_ref, o_ref, qe_ref, vt_ref, m_ref, acc_ref,
                     pre_ref, *, t, tk, ahead):
    qi = pl.program_id(1)
    lo_q, hi_q = _half_masks(t)
    lo = lax.broadcasted_iota(jnp.int32, (t, LANES), 1) < HEAD_DIM
    n_before = WINDOW // tk
    n_tiles = n_before + t // tk

    @pl.when(qi == 0)
    def _():
        def merge(j, c):
            rows = pl.ds(pl.multiple_of(j * tk, tk), tk)
            for g in range(NSA_KV_HEADS):
                vd = v_ref[rows, g * LANES:(g + 1) * LANES]
                _store_vt(vt_ref, 2 * g, rows, vd)
            return c

        lax.fori_loop(0, v_ref.shape[0] // tk, merge, 0)

    for h in range(NSA_HEADS):
        qp = q_ref[:, (h // 2) * LANES:(h // 2 + 1) * LANES]
        qe_ref[h] = qp * lo_q if h % 2 == 0 else qp * hi_q
    m_ref[...] = jnp.full(m_ref.shape, NEG_INF, F32)
    acc_ref[...] = jnp.zeros_like(acc_ref)

    def rows_of(w):
        return pl.ds(pl.multiple_of((qi * (t // tk) - n_before + w) * tk, tk), tk)

    def scores(w, h):
        g = h // NSA_REP
        return lax.dot_general(k_ref[rows_of(w), g * LANES:(g + 1) * LANES], qe_ref[h], (((1,), (1,)), ((), ())),
                               preferred_element_type=F32)

    def tile(w):
        rows = rows_of(w)
        queue = [pre_ref[i] for i in range(ahead)]
        for h in range(NSA_HEADS):
            if h + ahead < NSA_HEADS:
                queue.append(scores(w, h + ahead))
            elif w + 1 < n_tiles:
                pre_ref[h + ahead - NSA_HEADS] = scores(w + 1, h + ahead - NSA_HEADS)
            s = bw_ref[h, w * tk:(w + 1) * tk, :] + queue.pop(0)
            m_old = m_ref[h]
            m_new = jnp.maximum(m_old, jnp.max(s, axis=0, keepdims=True))
            m_ref[h] = m_new
            p = jnp.exp2(s - m_new).astype(BF16)
            vr = _vt_rows(h)
            acc_ref[h, vr] = jnp.exp2(m_old - m_new) * acc_ref[h, vr] + jnp.dot(
                vt_ref[2 * (h // NSA_REP) + h % 2, vr, rows], p, preferred_element_type=F32)

    first = jnp.where(qi > 0, 0, n_before)
    for i in range(ahead):
        pre_ref[i] = scores(first, i)

    @pl.when(qi > 0)
    def _():
        for w in range(n_before):
            tile(w)

    for w in range(n_before, n_tiles):
        tile(w)

    gl = gl_ref[...]
    for hp in range(NSA_HEADS // 2):
        ls = slice(hp * LANES, (hp + 1) * LANES)
        a0, a1 = acc_ref[2 * hp].T, acc_ref[2 * hp + 1].T
        c0 = 2 * NSA_HEADS + 2 * hp
        o = jnp.where(lo, a0 / jnp.maximum(a0[:, HEAD_DIM:HEAD_DIM + 1], TINY) * jax.nn.sigmoid(gl[:, c0:c0 + 1]),
                      a1 / jnp.maximum(a1[:, HEAD_DIM - 1:HEAD_DIM], TINY) * jax.nn.sigmoid(gl[:, c0 + 1:c0 + 2]))
        o_ref[:, ls] = (o * _silu(gate_ref[:, ls].astype(F32))).astype(o_ref.dtype)


def _win_attn(p, ps, bwin, b, s, t=WIN_T, tk=WIN_TK, ahead=2):
    assert t % WINDOW == 0 and WINDOW % tk == 0
    nq = s // t
    w = GROUP_W
    kw = NSA_KV_HEADS * LANES
    return pl.pallas_call(
        functools.partial(_win_attn_kernel, t=t, tk=tk, ahead=ahead),
        grid=(b, nq),
        in_specs=[
            pl.BlockSpec((t, w), lambda bi, qi: (bi * nq + qi, OFF_NSA_Q // w)),
            pl.BlockSpec((s, kw), lambda bi, qi: (bi, OFF_WIN_K // kw)),
            pl.BlockSpec((s, kw), lambda bi, qi: (bi, OFF_WIN_V // kw)),
            pl.BlockSpec((t, LANES), lambda bi, qi: (bi * nq + qi, 2)),
            pl.BlockSpec((t, w), lambda bi, qi: (bi * nq + qi, OFF_NSA_G // w)),
            pl.BlockSpec((NSA_HEADS, WINDOW + t, t), lambda bi, qi: (0, 0, 0), pipeline_mode=pl.Buffered(1)),
        ],
        out_specs=pl.BlockSpec((t, w), lambda bi, qi: (bi * nq + qi, 0)),
        out_shape=jax.ShapeDtypeStruct((b * s, w), BF16),
        scratch_shapes=[
            pltpu.VMEM((NSA_HEADS, t, LANES), BF16),
            pltpu.VMEM((2 * NSA_KV_HEADS, LANES, s), BF16),
            pltpu.VMEM((NSA_HEADS, 1, t), F32),
            pltpu.VMEM((NSA_HEADS, LANES, t), F32),
            pltpu.VMEM((ahead, tk, t), F32),
        ],
        compiler_params=_params(("parallel", "arbitrary")),
        name="nsa_win_attn",
    )(p, p, p, ps, p, bwin)


def _mem_attn_kernel(q_ref, gate_ref, kv_ref, o_ref):
    scale = MEM_HEAD_DIM ** -0.5
    for h in range(MEM_HEADS):
        ls = slice(h * LANES, (h + 1) * LANES)
        k = kv_ref[:, ls]
        v = kv_ref[:, GROUP_W + h * LANES:GROUP_W + (h + 1) * LANES]
        s = lax.dot_general(q_ref[:, ls], k, (((1,), (1,)), ((), ())), preferred_element_type=F32) * scale
        m = jnp.max(s, axis=1, keepdims=True)
        e = jnp.exp(s - m)
        l = jnp.sum(e, axis=1, keepdims=True)
        o = jnp.dot(e.astype(BF16), v, preferred_element_type=F32) / l
        o_ref[:, ls] = (o * _silu(gate_ref[:, ls].astype(F32))).astype(o_ref.dtype)


def _mem_attn(p, mem_kv, b, s, t=512):
    nq = s // t
    w = GROUP_W
    m = mem_kv.shape[0] // b
    return pl.pallas_call(
        _mem_attn_kernel,
        grid=(b, nq),
        in_specs=[
            pl.BlockSpec((t, w), lambda bi, qi: (bi * nq + qi, OFF_MEM_Q // w)),
            pl.BlockSpec((t, w), lambda bi, qi: (bi * nq + qi, OFF_MEM_G // w)),
            pl.BlockSpec((m, 2 * w), lambda bi, qi: (bi, 0)),
        ],
        out_specs=pl.BlockSpec((t, w), lambda bi, qi: (bi * nq + qi, 0)),
        out_shape=jax.ShapeDtypeStruct((b * s, w), BF16),
        compiler_params=_params(("parallel", "arbitrary")),
        name="mem_attn",
    )(p, p, mem_kv)


def _out_proj_kernel(x_ref, of_ref, os_ref, oc_ref, osel_ref, ow_ref, om_ref, w_ref, g_ref, o_ref, *, final):
    w = GROUP_W
    nsa = (oc_ref[...].astype(F32) + osel_ref[...].astype(F32) + ow_ref[...].astype(F32)).astype(BF16)
    acc = x_ref[...]
    for i, part in enumerate((of_ref[...], os_ref[...], nsa, om_ref[...])):
        acc = acc + jnp.dot(part, w_ref[i * w:(i + 1) * w, :], preferred_element_type=F32)
    if final:
        ms = jnp.mean(acc * acc, axis=-1, keepdims=True)
        acc = (acc * lax.rsqrt(ms + EPS)) * g_ref[...]
    o_ref[...] = acc


def _out_proj(x2d, parts, w_out, g, final, tm=512):
    n, d = x2d.shape
    w = GROUP_W
    part_spec = pl.BlockSpec((tm, w), lambda i: (i, 0))
    return pl.pallas_call(
        functools.partial(_out_proj_kernel, final=final),
        grid=(n // tm,),
        in_specs=[pl.BlockSpec((tm, d), lambda i: (i, 0))] + [part_spec] * 6 + [
            pl.BlockSpec((4 * w, d), lambda i: (0, 0)),
            pl.BlockSpec((1, d), lambda i: (0, 0)),
        ],
        out_specs=pl.BlockSpec((tm, d), lambda i: (i, 0)),
        out_shape=jax.ShapeDtypeStruct((n, d), F32),
        compiler_params=_params(("parallel",)),
        name="out_proj",
    )(x2d, *parts, w_out, g.reshape(1, d))


def _pack_in_proj(w_in_l):
    fox, ssm, nsa, mem = 0, FOX_COLS, FOX_COLS + SSM_COLS, FOX_COLS + SSM_COLS + NSA_COLS
    w = GROUP_W
    q_scale = HEAD_DIM ** -0.5 * LOG2E
    cols = lambda a, n: w_in_l[:, a:a + n]
    kv = lambda slot: nsa + w + slot * NSA_KV_W
    dup = lambda base: [cols(base, HEAD_DIM)] * 2 + [cols(base + HEAD_DIM, HEAD_DIM)] * 2
    main = ([cols(fox, w) * q_scale, cols(fox + w, 3 * w)]
            + [cols(ssm + w, SSM_CONV_DIM), cols(ssm, w)]
            + [cols(nsa, w) * q_scale, cols(nsa + w + 6 * NSA_KV_W + 3 * NSA_HEADS, w)]
            + [cols(mem, 2 * w)]
            + dup(kv(2)) + dup(kv(3)) + dup(kv(4)) + dup(kv(5))
            + [cols(kv(0), 2 * NSA_KV_W)])
    w_main = jnp.concatenate(main, axis=1).astype(BF16)
    assert w_main.shape[1] == P_COLS
    small = []
    for src, n in ((fox + 4 * w, FOX_HEADS), (ssm + w + SSM_CONV_DIM, SSM_HEADS), (nsa + w + 6 * NSA_KV_W, 3 * NSA_HEADS)):
        small += [cols(src, n), jnp.zeros((w_in_l.shape[0], LANES - n), w_in_l.dtype)]
    return w_main, jnp.concatenate(small, axis=1).astype(BF16)


def _pad_lanes(v):
    return jnp.pad(v.astype(F32), (0, LANES - v.shape[0])).reshape(1, LANES)


def _trunk(x, mem, norm_g, w_in, fox_f_bias, ssm_conv_w, ssm_conv_b, ssm_dt_bias, ssm_a_log, ssm_d,
           ssm_norm_g, nsa_cmp_pe, nsa_cmp_w1, nsa_cmp_w2, rel_bias_table, mem_norm_g, w_mem_kv, w_out,
           final_norm_g):
    b, s, d = x.shape
    depth = w_in.shape[0]
    n = b * s
    m_tok = mem.shape[1]
    n_cmp = (s - CMP_BLOCK) // CMP_STRIDE + 1
    n_rows = s // CMP_STRIDE
    assert s % 512 == 0 and s // SEL_BLOCK <= HEAD_DIM and n_rows <= N_CMP_PAD and d == D_MODEL

    tab_flat = rel_bias_table.astype(F32).reshape(-1)
    idx_win, idx_sel, idx_cmp = _bias_indices()
    bwin = _bias_table(tab_flat, idx_win)
    bsel = _bias_table(tab_flat, idx_sel)
    fcmp = _bias_table(tab_flat, idx_cmp)

    cs = np.arange(N_CMP_PAD)[None, :] * CMP_STRIDE
    js = np.arange(SEL_BLOCK)[:, None] * SEL_BLOCK
    overlap_t = ((cs < js + SEL_BLOCK) & (cs + CMP_BLOCK > js) & (np.arange(N_CMP_PAD)[None, :] < n_cmp)
                 & (np.arange(SEL_BLOCK)[:, None] < s // SEL_BLOCK)).astype(np.float32)
    overlap_t = jnp.asarray(overlap_t, BF16)

    x2d = x.reshape(n, d)
    mem2d = mem.reshape(b * m_tok, d)
    half = CMP_STRIDE * HEAD_DIM
    for l in range(depth):
        w_main, w_small = _pack_in_proj(w_in[l])
        p, ps = _norm_proj(x2d, norm_g[l], w_main, w_small)

        qaug, kaug = _fox_cumsum(ps, _pad_lanes(fox_f_bias[l]), b, s)
        o_fox = _fox_attn(p, qaug, kaug, b, s)

        o_ssd = _ssd(p, ps, ssm_conv_w[l].astype(F32), ssm_conv_b[l].reshape(1, -1).astype(F32),
                     _pad_lanes(ssm_dt_bias[l]), _pad_lanes(ssm_a_log[l]),
                     jnp.repeat(ssm_d[l].astype(F32), HEAD_DIM).reshape(1, GROUP_W),
                     ssm_norm_g[l].reshape(1, GROUP_W).astype(F32), b, s)

        kvc = p[:, OFF_CMP_KV:OFF_CMP_KV + 2 * NSA_KV_W]
        r = kvc.reshape(b, n_rows, CMP_STRIDE, 2 * NSA_KV_HEADS, HEAD_DIM).transpose(0, 3, 1, 2, 4)
        r = r.reshape(b, 2 * NSA_KV_HEADS, n_rows, half)
        if n_rows < N_CMP_PAD:
            r = jnp.pad(r, ((0, 0), (0, 0), (0, N_CMP_PAD - n_rows), (0, 0)))
        w1 = nsa_cmp_w1[l].astype(BF16)
        w2dup = jnp.concatenate([nsa_cmp_w2[l], nsa_cmp_w2[l]], axis=-1).astype(BF16)
        pe8 = jnp.broadcast_to(nsa_cmp_pe[l].astype(F32).reshape(2, 1, 2 * half), (2, 8, 2 * half))
        kv_cmp, kv_cmp_t = _compress(r, w1[:, :half], w1[:, half:], w2dup, pe8)
        o_cmp, notsel = _cmp_select(p, ps, kv_cmp, kv_cmp_t, fcmp, overlap_t, b, s)
        o_sel = _sel_attn(tab_flat, p, ps, notsel, bsel, b, s)
        o_win = _win_attn(p, ps, bwin, b, s)

        w_kv = w_mem_kv[l].astype(BF16)
        mem_kv, _ = _norm_proj(mem2d, mem_norm_g[l], w_kv, w_kv[:, :LANES], tm=min(512, b * m_tok))
        o_mem = _mem_attn(p, mem_kv, b, s)

        x2d = _out_proj(x2d, (o_fox, o_ssd, o_cmp, o_sel, o_win, o_mem), w_out[l].astype(BF16),
                        final_norm_g, final=(l == depth - 1))
    return x2d.reshape(b, s, d)


def kernel(x, mem, norm_g, w_in, fox_f_bias, ssm_conv_w, ssm_conv_b, ssm_dt_bias, ssm_a_log, ssm_d, ssm_norm_g,
           nsa_cmp_pe, nsa_cmp_w1, nsa_cmp_w2, rel_bias_table, mem_norm_g, w_mem_kv, w_out, final_norm_g):
    return _trunk(x, mem, norm_g, w_in, fox_f_bias, ssm_conv_w, ssm_conv_b, ssm_dt_bias, ssm_a_log, ssm_d,
                  ssm_norm_g, nsa_cmp_pe, nsa_cmp_w1, nsa_cmp_w2, rel_bias_table, mem_norm_g, w_mem_kv, w_out,
                  final_norm_g)
```

```python
import functools
import math

import numpy as np
import jax
import jax.numpy as jnp
from jax import lax
from jax.experimental import pallas as pl
from jax.experimental.pallas import tpu as pltpu

F32 = jnp.float32
BF16 = jnp.bfloat16

D_MODEL = 1024
GROUP_W = 512
HEAD_DIM = 64
EPS = 1e-6
NEG_INF = -1e30
TINY = 1e-30
LOG2E = math.log2(math.e)

FOX_HEADS = 8
SSM_HEADS = 8
SSM_STATE = 128
SSM_GROUPS = 2
SSM_CONV = 4
SSM_CHUNK = 128
SSM_CONV_DIM = GROUP_W + 2 * SSM_GROUPS * SSM_STATE

NSA_HEADS = 8
NSA_KV_HEADS = 2
NSA_REP = NSA_HEADS // NSA_KV_HEADS
NSA_KV_W = NSA_KV_HEADS * HEAD_DIM
CMP_BLOCK = 32
CMP_STRIDE = 16
CMP_HIDDEN = 2 * HEAD_DIM
SEL_BLOCK = 64
SEL_TOPK = 16
WINDOW = 512
SEL_FORCE = 1e9

MEM_HEADS = 4
MEM_HEAD_DIM = GROUP_W // MEM_HEADS
REL_BUCKETS = 32
REL_MAX_DIST = 128

FOX_COLS = 4 * GROUP_W + FOX_HEADS
SSM_COLS = GROUP_W + SSM_CONV_DIM + SSM_HEADS
NSA_COLS = 2 * GROUP_W + 6 * NSA_KV_W + 3 * NSA_HEADS
MEM_COLS = 2 * GROUP_W

LANES = 128
VMEM_LIMIT = 56 * 1024 * 1024

OFF_FOX_Q, OFF_FOX_K, OFF_FOX_V, OFF_FOX_G = 0, 512, 1024, 1536
OFF_SSM_XBC, OFF_SSM_Z = 2048, 3072
OFF_NSA_Q, OFF_NSA_G = 3584, 4096
OFF_MEM_Q, OFF_MEM_G = 4608, 5120
OFF_SEL_K, OFF_SEL_V, OFF_WIN_K, OFF_WIN_V, OFF_CMP_KV = 5632, 5888, 6144, 6400, 6656
P_COLS = 6912
PS_COLS = 3 * LANES

T_ATT = 256
SEL_T, SEL_TK = 512, 256
WIN_T, WIN_TK = 512, 256
N_CMP_PAD = 256


def _params(sem):
    return pltpu.CompilerParams(dimension_semantics=sem, vmem_limit_bytes=VMEM_LIMIT)


def _t5_bucket_np(dist):
    n = np.maximum(dist, 0)
    max_exact = REL_BUCKETS // 2
    nf = np.maximum(n, 1).astype(np.float32)
    large = max_exact + (np.log(nf / np.float32(max_exact)) / np.float32(math.log(REL_MAX_DIST / max_exact))
                         * np.float32(REL_BUCKETS - max_exact)).astype(np.int32)
    large = np.minimum(large, REL_BUCKETS - 1)
    return np.where(n < max_exact, n, large).astype(np.int32)


def _silu(x):
    h = 0.5 * x
    return h + h * jnp.tanh(h)


def _norm_proj_kernel(x_ref, g_ref, w_ref, ws_ref, p_ref, ps_ref, *, chunk):
    x = x_ref[...]
    ms = jnp.mean(x * x, axis=-1, keepdims=True)
    h = ((x * lax.rsqrt(ms + EPS)) * g_ref[...]).astype(BF16)
    ncol = p_ref.shape[1]
    for c0 in range(0, ncol, chunk):
        c1 = min(c0 + chunk, ncol)
        p_ref[:, c0:c1] = jnp.dot(h, w_ref[:, c0:c1], preferred_element_type=F32).astype(p_ref.dtype)
    ps_ref[...] = jnp.dot(h, ws_ref[...], preferred_element_type=F32)


def _norm_proj(x2d, g, w_main, w_small, tm=512):
    n, d = x2d.shape
    pc, sc = w_main.shape[1], w_small.shape[1]
    return pl.pallas_call(
        functools.partial(_norm_proj_kernel, chunk=512),
        grid=(n // tm,),
        in_specs=[
            pl.BlockSpec((tm, d), lambda i: (i, 0)),
            pl.BlockSpec((1, d), lambda i: (0, 0)),
            pl.BlockSpec((d, pc), lambda i: (0, 0)),
            pl.BlockSpec((d, sc), lambda i: (0, 0)),
        ],
        out_specs=[
            pl.BlockSpec((tm, pc), lambda i: (i, 0)),
            pl.BlockSpec((tm, sc), lambda i: (i, 0)),
        ],
        out_shape=[jax.ShapeDtypeStruct((n, pc), BF16), jax.ShapeDtypeStruct((n, sc), F32)],
        compiler_params=_params(("parallel",)),
        name="norm_proj",
    )(x2d, g.reshape(1, d), w_main, w_small)


def _t5_table_kernel(tab_ref, bucket_ref, o_ref, *, stride, off, limit):
    h = pl.program_id(0)
    rows, cols = o_ref.shape[1], o_ref.shape[2]
    bucket = bucket_ref[...]
    near = jnp.zeros(bucket.shape, F32)
    for b in range(REL_BUCKETS):
        near = jnp.where(bucket == b, tab_ref[b * NSA_HEADS + h] * LOG2E, near)
    far = tab_ref[(REL_BUCKETS - 1) * NSA_HEADS + h] * LOG2E
    base = jnp.concatenate([near] * (rows // near.shape[0]), axis=0)
    rolled = pltpu.roll(base, 0, 1, stride=stride, stride_axis=0)
    d = (lax.broadcasted_iota(jnp.int32, (rows, cols), 1) - stride * lax.broadcasted_iota(jnp.int32, (rows, cols), 0)
         + off)
    o_ref[0] = jnp.where((d < 0) | (d >= limit), NEG_INF, jnp.where(d >= REL_MAX_DIST, far, rolled))


def _t5_table(tab_flat, rows, cols, stride, off, limit):
    assert cols >= 2 * REL_MAX_DIST and rows % 8 == 0
    k = (np.arange(cols) + off) % cols
    bucket = np.broadcast_to(np.where(k < REL_MAX_DIST, _t5_bucket_np(k), -1).astype(np.int32), (8, cols))
    return pl.pallas_call(
        functools.partial(_t5_table_kernel, stride=stride, off=off, limit=limit),
        grid=(NSA_HEADS,),
        in_specs=[
            pl.BlockSpec(memory_space=pltpu.SMEM),
            pl.BlockSpec((8, cols), lambda h: (0, 0)),
        ],
        out_specs=pl.BlockSpec((1, rows, cols), lambda h: (h, 0, 0)),
        out_shape=jax.ShapeDtypeStruct((NSA_HEADS, rows, cols), F32),
        compiler_params=_params(("arbitrary",)),
        name="t5_table",
    )(tab_flat, jnp.asarray(bucket))


def _tri_lower(n):
    r = lax.broadcasted_iota(jnp.int32, (n, n), 0)
    c = lax.broadcasted_iota(jnp.int32, (n, n), 1)
    return (r >= c).astype(F32)


N_SPLIT = 3


def _fox_aug_lane(h, i):
    return LANES * (h // 2) + (HEAD_DIM if h % 2 == 0 else 0) + i


def _fox_aug_consts():
    pq = np.zeros((LANES, GROUP_W), np.float32)
    pk = np.zeros((LANES, GROUP_W), np.float32)
    oq = np.zeros((1, GROUP_W), np.float32)
    ok = np.zeros((1, GROUP_W), np.float32)
    for h in range(FOX_HEADS):
        for i in range(N_SPLIT):
            pq[i * FOX_HEADS + h, _fox_aug_lane(h, i)] = 1.0
            pk[i * FOX_HEADS + h, _fox_aug_lane(h, N_SPLIT + i)] = -1.0
            oq[0, _fox_aug_lane(h, N_SPLIT + i)] = 1.0
            ok[0, _fox_aug_lane(h, i)] = 1.0
    return pq, pk, oq, ok


def _fox_cumsum_kernel(f_ref, b_ref, pq_ref, pk_ref, oq_ref, ok_ref, qa_ref, ka_ref, carry_ref, *, ts):
    @pl.when(pl.program_id(1) == 0)
    def _():
        carry_ref[...] = jnp.zeros_like(carry_ref)

    z = f_ref[...] + b_ref[...]
    logf = (jnp.minimum(z, 0.0) - jnp.log(1.0 + jnp.exp(-jnp.abs(z)))) * LOG2E
    tri = _tri_lower(LANES).astype(BF16)
    group = lax.broadcasted_iota(jnp.int32, (LANES, LANES), 1) // FOX_HEADS
    carry = carry_ref[...]
    for c in range(ts // LANES):
        rows = slice(c * LANES, (c + 1) * LANES)
        cs, rest = carry, logf[rows]
        for _ in range(N_SPLIT):
            piece = rest.astype(BF16)
            rest = rest - piece.astype(F32)
            cs = cs + jnp.dot(tri, piece, preferred_element_type=F32)
        carry = cs[LANES - 1:LANES, :]
        cat, rest = None, cs
        for i in range(N_SPLIT):
            piece = rest.astype(BF16).astype(F32)
            rest = rest - piece
            cat = piece if cat is None else jnp.where(group == i, piece, cat)
        cat = cat.astype(BF16)
        qa_ref[rows, :] = (oq_ref[...] + jnp.dot(cat, pq_ref[...], preferred_element_type=F32)).astype(BF16)
        ka_ref[rows, :] = (ok_ref[...] + jnp.dot(cat, pk_ref[...], preferred_element_type=F32)).astype(BF16)
    carry_ref[...] = carry


def _fox_cumsum(ps, f_bias_pad, b, s, ts=512):
    ns = s // ts
    pq, pk, oq, ok = _fox_aug_consts()
    const = lambda bi, si: (0, 0)
    return pl.pallas_call(
        functools.partial(_fox_cumsum_kernel, ts=ts),
        grid=(b, ns),
        in_specs=[
            pl.BlockSpec((ts, LANES), lambda bi, si: (bi * ns + si, 0)),
            pl.BlockSpec((1, LANES), const),
            pl.BlockSpec(pq.shape, const),
            pl.BlockSpec(pk.shape, const),
            pl.BlockSpec(oq.shape, const),
            pl.BlockSpec(ok.shape, const),
        ],
        out_specs=[
            pl.BlockSpec((ts, GROUP_W), lambda bi, si: (bi * ns + si, 0)),
            pl.BlockSpec((ts, GROUP_W), lambda bi, si: (bi * ns + si, 0)),
        ],
        out_shape=[jax.ShapeDtypeStruct((b * s, GROUP_W), BF16), jax.ShapeDtypeStruct((b * s, GROUP_W), BF16)],
        scratch_shapes=[pltpu.VMEM((1, LANES), F32)],
        compiler_params=_params(("parallel", "arbitrary")),
        name="fox_cumsum",
    )(ps, f_bias_pad, jnp.asarray(pq, BF16), jnp.asarray(pk, BF16), jnp.asarray(oq), jnp.asarray(ok))


ONES_ROWS = 16


def _vt_rows(h):
    return slice(0, HEAD_DIM + ONES_ROWS) if h % 2 == 0 else slice(HEAD_DIM - ONES_ROWS, LANES)


def _store_vt(vt_ref, i, rows, v_pair):
    vt = v_pair.astype(F32).T.astype(BF16)
    ones = jnp.ones((HEAD_DIM, v_pair.shape[0]), BF16)
    vt_ref[i, 0:HEAD_DIM, rows] = vt[0:HEAD_DIM]
    vt_ref[i, HEAD_DIM:LANES, rows] = ones
    vt_ref[i + 1, 0:HEAD_DIM, rows] = ones
    vt_ref[i + 1, HEAD_DIM:LANES, rows] = vt[HEAD_DIM:LANES]


def _half_masks(rows):
    lo = jnp.where(lax.broadcasted_iota(jnp.int32, (rows, LANES), 1) < HEAD_DIM, 1.0, 0.0)
    return lo.astype(BF16), (1.0 - lo).astype(BF16)


def _fox_attn_kernel(q_ref, qa_ref, k_ref, ka_ref, v_ref, gate_ref, o_ref, qs_ref, kk_ref, vt_ref, m_ref, acc_ref,
                     pre_ref, *, t, tk, ahead):
    qi = pl.program_id(1)
    lo_q, hi_q = _half_masks(t)
    lo = lax.broadcasted_iota(jnp.int32, (t, LANES), 1) < HEAD_DIM
    cm = lax.broadcasted_iota(jnp.int32, (tk, t), 0) - lax.broadcasted_iota(jnp.int32, (tk, t), 1)

    @pl.when(qi == 0)
    def _():
        lo_k, hi_k = _half_masks(tk)

        def merge(j, c):
            rows = pl.ds(pl.multiple_of(j * tk, tk), tk)
            for hp in range(FOX_HEADS // 2):
                ls = slice(hp * LANES, (hp + 1) * LANES)
                kp, ka, vp = k_ref[rows, ls], ka_ref[rows, ls], v_ref[rows, ls]
                kk_ref[2 * hp, rows, :] = kp * lo_k + ka * hi_k
                kk_ref[2 * hp + 1, rows, :] = ka * lo_k + kp * hi_k
                _store_vt(vt_ref, 2 * hp, rows, vp)
            return c

        lax.fori_loop(0, k_ref.shape[0] // tk, merge, 0)

    for hp in range(FOX_HEADS // 2):
        ls = slice(hp * LANES, (hp + 1) * LANES)
        qp, qa = q_ref[:, ls], qa_ref[:, ls]
        qs_ref[2 * hp] = qp * lo_q + qa * hi_q
        qs_ref[2 * hp + 1] = qa * lo_q + qp * hi_q
    m_ref[...] = jnp.full(m_ref.shape, NEG_INF, F32)
    acc_ref[...] = jnp.zeros_like(acc_ref)

    def rows_of(j):
        return pl.ds(pl.multiple_of(j * tk, tk), tk)

    def scores(j, h):
        return lax.dot_general(kk_ref[h, rows_of(j), :], qs_ref[h], (((1,), (1,)), ((), ())),
                               preferred_element_type=F32)

    def tile(j, diag, j_next):
        queue = [pre_ref[i] for i in range(ahead)]
        for h in range(FOX_HEADS):
            if h + ahead < FOX_HEADS:
                queue.append(scores(j, h + ahead))
            elif j_next is not None:
                pre_ref[h + ahead - FOX_HEADS] = scores(j_next, h + ahead - FOX_HEADS)
            s = queue.pop(0)
            if diag:
                s = jnp.where(cm <= qi * t - j * tk, s, NEG_INF)
            m_old = m_ref[h]
            m_new = jnp.maximum(m_old, jnp.max(s, axis=0, keepdims=True))
            m_ref[h] = m_new
            p = jnp.exp2(s - m_new).astype(BF16)
            vr = _vt_rows(h)
            acc_ref[h, vr] = jnp.exp2(m_old - m_new) * acc_ref[h, vr] + jnp.dot(vt_ref[h, vr, rows_of(j)], p,
                                                                                preferred_element_type=F32)

    jd = (qi * t) // tk
    n_diag = max(t // tk, 1)
    for i in range(ahead):
        pre_ref[i] = scores(0, i)

    def body(j, c):
        tile(j, False, j + 1)
        return c

    lax.fori_loop(0, jd, body, 0)
    for dj in range(n_diag):
        tile(jd + dj, True, jd + dj + 1 if dj + 1 < n_diag else None)

    for hp in range(FOX_HEADS // 2):
        ls = slice(hp * LANES, (hp + 1) * LANES)
        a0, a1 = acc_ref[2 * hp].T, acc_ref[2 * hp + 1].T
        o = jnp.where(lo, a0 / jnp.maximum(a0[:, HEAD_DIM:HEAD_DIM + 1], TINY),
                      a1 / jnp.maximum(a1[:, HEAD_DIM - 1:HEAD_DIM], TINY))
        o_ref[:, ls] = (o * _silu(gate_ref[:, ls].astype(F32))).astype(o_ref.dtype)


def _fox_attn(p, qaug, kaug, b, s, t=512, tk=256, ahead=2):
    nq = s // t
    w = GROUP_W
    return pl.pallas_call(
        functools.partial(_fox_attn_kernel, t=t, tk=tk, ahead=ahead),
        grid=(b, nq),
        in_specs=[
            pl.BlockSpec((t, w), lambda bi, qi: (bi * nq + qi, OFF_FOX_Q // w)),
            pl.BlockSpec((t, w), lambda bi, qi: (bi * nq + qi, 0)),
            pl.BlockSpec((s, w), lambda bi, qi: (bi, OFF_FOX_K // w)),
            pl.BlockSpec((s, w), lambda bi, qi: (bi, 0)),
            pl.BlockSpec((s, w), lambda bi, qi: (bi, OFF_FOX_V // w)),
            pl.BlockSpec((t, w), lambda bi, qi: (bi * nq + qi, OFF_FOX_G // w)),
        ],
        out_specs=pl.BlockSpec((t, w), lambda bi, qi: (bi * nq + qi, 0)),
        out_shape=jax.ShapeDtypeStruct((b * s, w), BF16),
        scratch_shapes=[
            pltpu.VMEM((FOX_HEADS, t, LANES), BF16),
            pltpu.VMEM((FOX_HEADS, s, LANES), BF16),
            pltpu.VMEM((FOX_HEADS, LANES, s), BF16),
            pltpu.VMEM((FOX_HEADS, 1, t), F32),
            pltpu.VMEM((FOX_HEADS, LANES, t), F32),
            pltpu.VMEM((ahead, tk, t), F32),
        ],
        compiler_params=_params(("parallel", "arbitrary")),
        name="fox_attn",
    )(p, qaug, p, kaug, p, p)


def _dot_split(lhs_f32, rhs_b):
    out, rest = None, lhs_f32
    for _ in range(N_SPLIT):
        piece = rest.astype(BF16)
        rest = rest - piece.astype(F32)
        term = jnp.dot(piece, rhs_b, preferred_element_type=F32)
        out = term if out is None else out + term
    return out


def _ssd_kernel(z_ref, xbc_ref, dt_ref, cw_ref, cb_ref, dtb_ref, alog_ref, dsk_ref, ng_ref, o_ref,
                xpad_ref, xc_ref, state_ref, y_ref, *, nch):
    q = SSM_CHUNK
    rows_all = nch * q
    halo = 8

    @pl.when(pl.program_id(1) == 0)
    def _():
        xpad_ref[0:halo, :] = jnp.zeros((halo, SSM_CONV_DIM), F32)
        state_ref[...] = jnp.zeros_like(state_ref)

    xpad_ref[halo:halo + rows_all, :] = xbc_ref[...].astype(F32)
    y = cb_ref[...]
    for k in range(SSM_CONV):
        off = halo - (SSM_CONV - 1) + k
        y = y + cw_ref[k:k + 1, :] * xpad_ref[off:off + rows_all, :]
    xpad_ref[0:halo, :] = xpad_ref[rows_all:rows_all + halo, :]
    xc_ref[...] = _silu(y)

    x_dt = dt_ref[...] + dtb_ref[...]
    dt_all = jnp.maximum(x_dt, 0.0) + jnp.log(1.0 + jnp.exp(-jnp.abs(x_dt)))
    a_all = dt_all * (-jnp.exp(alog_ref[...]))
    tri_t = (lax.broadcasted_iota(jnp.int32, (q, q), 0) <= lax.broadcasted_iota(jnp.int32, (q, q), 1)).astype(BF16)
    er = lax.broadcasted_iota(jnp.int32, (LANES, GROUP_W), 0)
    ec = lax.broadcasted_iota(jnp.int32, (LANES, GROUP_W), 1)
    expand = jnp.where(ec // HEAD_DIM == er, 1.0, 0.0).astype(BF16)
    row = lax.broadcasted_iota(jnp.int32, (q, q), 0)
    col = lax.broadcasted_iota(jnp.int32, (q, q), 1)
    causal = row >= col
    lo = lax.broadcasted_iota(jnp.int32, (q, LANES), 1) < HEAD_DIM
    gw = GROUP_W // SSM_GROUPS
    hpg = SSM_HEADS // SSM_GROUPS

    for c in range(nch):
        rs = slice(c * q, (c + 1) * q)
        xs = xc_ref[rs, :GROUP_W]
        dt = dt_all[rs]
        acs_t = _dot_split(a_all[rs].T, tri_t)
        acs = acs_t.T
        stacked = jnp.concatenate([dt, jnp.exp(acs), jnp.exp(acs[q - 1:q, :] - acs)], axis=0)
        full = _dot_split(stacked, expand)
        dt_full, eacs_full, dec_full = full[0:q], full[q:2 * q], full[2 * q:3 * q]
        xdt = xs * dt_full
        xdt_b = xdt.astype(BF16)
        xdec_b = (xdt * dec_full).astype(BF16)

        for g in range(SSM_GROUPS):
            bm = xc_ref[rs, GROUP_W + g * SSM_STATE:GROUP_W + (g + 1) * SSM_STATE]
            cm = xc_ref[rs, GROUP_W + (SSM_GROUPS + g) * SSM_STATE:GROUP_W + (SSM_GROUPS + g + 1) * SSM_STATE]
            bm_b = bm.astype(BF16)
            cm_b = cm.astype(BF16)
            gs = slice(g * gw, (g + 1) * gw)
            cbg = lax.dot_general(cm_b, bm_b, (((1,), (1,)), ((), ())), preferred_element_type=F32)
            st = state_ref[:, gs]
            y_off = jnp.dot(cm_b, st.astype(BF16), preferred_element_type=F32) * eacs_full[:, gs]
            cst = jnp.dot(bm.T.astype(BF16), xdec_b[:, gs], preferred_element_type=F32)
            state_ref[:, gs] = st * eacs_full[q - 1:q, gs] + cst
            for hp in range(hpg // 2):
                ls = slice(g * gw + hp * LANES, g * gw + (hp + 1) * LANES)
                yd = []
                for e in range(2):
                    h = g * hpg + 2 * hp + e
                    seg = jnp.exp(jnp.where(causal, acs[:, h:h + 1] - acs_t[h:h + 1, :], NEG_INF))
                    yd.append(jnp.dot((cbg * seg).astype(BF16), xdt_b[:, ls], preferred_element_type=F32))
                y_ref[:, ls] = jnp.where(lo, yd[0], yd[1]) + y_off[:, hp * LANES:(hp + 1) * LANES]

        yz = (y_ref[...] + xs * dsk_ref[...]) * _silu(z_ref[rs, :].astype(F32))
        for g in range(SSM_GROUPS):
            gs = slice(g * gw, (g + 1) * gw)
            blk = yz[:, gs]
            ms = jnp.mean(blk * blk, axis=-1, keepdims=True)
            o_ref[rs, gs] = ((blk * lax.rsqrt(ms + EPS)) * ng_ref[:, gs]).astype(o_ref.dtype)


def _ssd(p, ps, conv_w, conv_b, dt_bias_pad, a_log_pad, d_full, norm_g, b, s, nch=4):
    q = SSM_CHUNK
    rows = nch * q
    nc = s // rows
    row = lambda bi, ci: (bi * nc + ci)
    const = lambda bi, ci: (0, 0)
    return pl.pallas_call(
        functools.partial(_ssd_kernel, nch=nch),
        grid=(b, nc),
        in_specs=[
            pl.BlockSpec((rows, GROUP_W), lambda bi, ci: (row(bi, ci), OFF_SSM_Z // GROUP_W)),
            pl.BlockSpec((rows, SSM_CONV_DIM), lambda bi, ci: (row(bi, ci), OFF_SSM_XBC // SSM_CONV_DIM)),
            pl.BlockSpec((rows, LANES), lambda bi, ci: (row(bi, ci), 1)),
            pl.BlockSpec((SSM_CONV, SSM_CONV_DIM), const),
            pl.BlockSpec((1, SSM_CONV_DIM), const),
            pl.BlockSpec((1, LANES), const),
            pl.BlockSpec((1, LANES), const),
            pl.BlockSpec((1, GROUP_W), const),
            pl.BlockSpec((1, GROUP_W), const),
        ],
        out_specs=pl.BlockSpec((rows, GROUP_W), lambda bi, ci: (row(bi, ci), 0)),
        out_shape=jax.ShapeDtypeStruct((b * s, GROUP_W), BF16),
        scratch_shapes=[
            pltpu.VMEM((rows + 8, SSM_CONV_DIM), F32),
            pltpu.VMEM((rows, SSM_CONV_DIM), F32),
            pltpu.VMEM((SSM_STATE, GROUP_W), F32),
            pltpu.VMEM((q, GROUP_W), F32),
        ],
        compiler_params=_params(("parallel", "arbitrary")),
        name="ssd",
    )(p, p, ps, conv_w, conv_b, dt_bias_pad, a_log_pad, d_full, norm_g)


def _compress_kernel(r_ref, w1a_ref, w1b_ref, w2_ref, pe_ref, o_ref, ot_ref):
    half = CMP_STRIDE * HEAD_DIM
    r = r_ref[0, 0]
    w1a = w1a_ref[0]
    w1b = w1b_ref[0]
    a = jnp.dot(r, w1a, preferred_element_type=F32)
    bm = jnp.dot(r, w1b, preferred_element_type=F32)
    pe = pe_ref[0].astype(BF16)
    const = (jnp.dot(pe[:, :half], w1a, preferred_element_type=F32)
             + jnp.dot(pe[:, half:], w1b, preferred_element_type=F32))[0:1]
    n = bm.shape[0]
    h = _silu(a + pltpu.roll(bm, n - 1, 0) + const)
    o = jnp.dot(h.astype(BF16), w2_ref[0], preferred_element_type=F32)
    o_ref[0, 0] = o.astype(o_ref.dtype)
    ot_ref[0, 0] = o.T.astype(ot_ref.dtype)


def _compress(r, w1a, w1b, w2dup, pe8):
    b, nslot, nr, width = r.shape
    return pl.pallas_call(
        _compress_kernel,
        grid=(b, nslot),
        in_specs=[
            pl.BlockSpec((1, 1, nr, width), lambda bi, si: (bi, si, 0, 0)),
            pl.BlockSpec((1, width, CMP_HIDDEN), lambda bi, si: (si // NSA_KV_HEADS, 0, 0)),
            pl.BlockSpec((1, width, CMP_HIDDEN), lambda bi, si: (si // NSA_KV_HEADS, 0, 0)),
            pl.BlockSpec((1, CMP_HIDDEN, LANES), lambda bi, si: (si // NSA_KV_HEADS, 0, 0)),
            pl.BlockSpec((1, 8, 2 * width), lambda bi, si: (si // NSA_KV_HEADS, 0, 0)),
        ],
        out_specs=[pl.BlockSpec((1, 1, nr, LANES), lambda bi, si: (bi, si, 0, 0)),
                   pl.BlockSpec((1, 1, LANES, nr), lambda bi, si: (bi, si, 0, 0))],
        out_shape=[jax.ShapeDtypeStruct((b, nslot, nr, LANES), BF16),
                   jax.ShapeDtypeStruct((b, nslot, LANES, nr), BF16)],
        compiler_params=_params(("parallel", "arbitrary")),
        name="nsa_compress",
    )(r, w1a, w1b, w2dup, pe8)


def _cmp_select_kernel(q_ref, kv_ref, vt_ref, gl_ref, gate_ref, fc_ref, ovt_ref, o_ref, ns_ref):
    t = T_ATT
    ncp = kv_ref.shape[2]
    qi = pl.program_id(1)
    t0 = qi * t
    lo = lax.broadcasted_iota(jnp.int32, (t, LANES), 1) < HEAD_DIM
    zero = jnp.zeros((t, LANES), BF16)
    start = pl.multiple_of(ncp - qi * (t // CMP_STRIDE), CMP_STRIDE)
    gl = gl_ref[...]

    nsel = SEL_BLOCK
    jrow = lax.broadcasted_iota(jnp.int32, (nsel, t), 0)
    cur = (t0 + lax.broadcasted_iota(jnp.int32, (nsel, t), 1)) // SEL_BLOCK
    forced = (jrow == 0) | (jrow == cur) | (jrow == cur - 1)
    past = jrow <= cur

    for g in range(NSA_KV_HEADS):
        kc = kv_ref[0, g]
        vct = vt_ref[0, NSA_KV_HEADS + g]
        psum = jnp.zeros((ncp, t), F32)
        outs = []

        def scores(r):
            h = g * NSA_REP + r
            qp = q_ref[:, (h // 2) * LANES:(h // 2 + 1) * LANES]
            qe = jnp.where(lo, qp, zero) if r % 2 == 0 else jnp.where(lo, zero, qp)
            return lax.dot_general(kc, qe, (((1,), (1,)), ((), ())), preferred_element_type=F32)

        queue = [scores(0)]
        for r in range(NSA_REP):
            h = g * NSA_REP + r
            if r + 1 < NSA_REP:
                queue.append(scores(r + 1))
            s = fc_ref[h, pl.ds(start, ncp), :] + queue.pop(0)
            m = jnp.maximum(jnp.max(s, axis=0, keepdims=True), 0.1 * NEG_INF)
            e = jnp.exp2(s - m)
            pr = e * (1.0 / jnp.maximum(jnp.sum(e, axis=0, keepdims=True), TINY))
            psum = psum + pr
            oc = jnp.dot(vct, pr.astype(BF16), preferred_element_type=F32)
            outs.append(oc.T * jax.nn.sigmoid(gl[:, h:h + 1]))
        for hp in range(NSA_REP // 2):
            ls = slice((g * NSA_REP // 2 + hp) * LANES, (g * NSA_REP // 2 + hp + 1) * LANES)
            o = jnp.where(lo, outs[2 * hp], outs[2 * hp + 1])
            o_ref[:, ls] = (o * _silu(gate_ref[:, ls].astype(F32))).astype(o_ref.dtype)

        imp_t = jnp.zeros((nsel, t), F32)
        rest = psum
        for _ in range(N_SPLIT):
            piece = rest.astype(BF16)
            rest = rest - piece.astype(F32)
            imp_t = imp_t + jnp.dot(ovt_ref[...], piece, preferred_element_type=F32)
        imp_t = jnp.where(past, jnp.where(forced, SEL_FORCE, imp_t), -SEL_FORCE)
        sub = 8
        sub_row = lax.broadcasted_iota(jnp.int32, (sub, t), 0)
        rows = [imp_t[k * sub:(k + 1) * sub] for k in range(nsel // sub)]
        rank = [jnp.zeros((sub, t), F32) for _ in rows]
        for i in range(nsel):
            bi = imp_t[i:i + 1, :]
            for k, x in enumerate(rows):
                if k * sub > i:
                    ahead = jnp.where(bi >= x, 1.0, 0.0)
                elif k * sub + sub - 1 <= i:
                    ahead = jnp.where(bi > x, 1.0, 0.0)
                else:
                    ahead = jnp.where(sub_row > i - k * sub, jnp.where(bi >= x, 1.0, 0.0),
                                      jnp.where(bi > x, 1.0, 0.0))
                rank[k] = rank[k] + ahead
        rank = jnp.concatenate(rank, axis=0)
        notsel = jnp.where((rank < float(SEL_TOPK)) & past, 0.0, 1.0)
        ns2 = jnp.concatenate([notsel, notsel], axis=0).T
        ns_ref[:, g * LANES:(g + 1) * LANES] = ns2.astype(ns_ref.dtype)


def _cmp_select(p, ps, kvc, kvc_t, fc, overlap_t, b, s):
    t = T_ATT
    nq = s // t
    w = GROUP_W
    ncp = kvc.shape[2]
    return pl.pallas_call(
        _cmp_select_kernel,
        grid=(b, nq),
        in_specs=[
            pl.BlockSpec((t, w), lambda bi, qi: (bi * nq + qi, OFF_NSA_Q // w)),
            pl.BlockSpec((1, 2 * NSA_KV_HEADS, ncp, LANES), lambda bi, qi: (bi, 0, 0, 0)),
            pl.BlockSpec((1, 2 * NSA_KV_HEADS, LANES, ncp), lambda bi, qi: (bi, 0, 0, 0)),
            pl.BlockSpec((t, LANES), lambda bi, qi: (bi * nq + qi, 2)),
            pl.BlockSpec((t, w), lambda bi, qi: (bi * nq + qi, OFF_NSA_G // w)),
            pl.BlockSpec((NSA_HEADS, 2 * ncp, t), lambda bi, qi: (0, 0, 0)),
            pl.BlockSpec((SEL_BLOCK, ncp), lambda bi, qi: (0, 0)),
        ],
        out_specs=[
            pl.BlockSpec((t, w), lambda bi, qi: (bi * nq + qi, 0)),
            pl.BlockSpec((t, NSA_KV_HEADS * LANES), lambda bi, qi: (bi * nq + qi, 0)),
        ],
        out_shape=[jax.ShapeDtypeStruct((b * s, w), BF16),
                   jax.ShapeDtypeStruct((b * s, NSA_KV_HEADS * LANES), BF16)],
        compiler_params=_params(("parallel", "arbitrary")),
        name="nsa_cmp_select",
    )(p, kvc, kvc_t, ps, p, fc, overlap_t)


def _sel_attn_kernel(tab_ref, q_ref, ns_ref, k_ref, v_ref, gl_ref, gate_ref, bs_ref, o_ref,
                     qa_ref, kk_ref, vt_ref, m_ref, acc_ref, pre_ref, *, t, tk, ahead):
    qi = pl.program_id(1)
    lo_q, hi_q = _half_masks(t)
    lo = lax.broadcasted_iota(jnp.int32, (t, LANES), 1) < HEAD_DIM
    n_near = t // tk + 1

    @pl.when(qi == 0)
    def _():
        lane = lax.broadcasted_iota(jnp.int32, (tk, LANES), 1)
        krow = lax.broadcasted_iota(jnp.int32, (tk, LANES), 0)
        lo_f = jnp.where(lane < HEAD_DIM, 1.0, 0.0)
        hi_f = 1.0 - lo_f
        lo_k, hi_k = lo_f.astype(BF16), hi_f.astype(BF16)

        def merge(j, c):
            ks = pl.multiple_of(j * tk, tk)
            rows = pl.ds(ks, tk)
            hot = jnp.where((lane % HEAD_DIM) == (ks + krow) // SEL_BLOCK, NEG_INF, 0.0)
            for g in range(NSA_KV_HEADS):
                gls = slice(g * LANES, (g + 1) * LANES)
                kd, vd = k_ref[rows, gls], v_ref[rows, gls]
                kk_ref[2 * g, rows, :] = kd * lo_k + (hot * hi_f).astype(BF16)
                kk_ref[2 * g + 1, rows, :] = (hot * lo_f).astype(BF16) + kd * hi_k
                _store_vt(vt_ref, 2 * g, rows, vd)
            return c

        lax.fori_loop(0, k_ref.shape[0] // tk, merge, 0)

    for h in range(NSA_HEADS):
        g = h // NSA_REP
        qp = q_ref[:, (h // 2) * LANES:(h // 2 + 1) * LANES]
        ns = ns_ref[:, g * LANES:(g + 1) * LANES]
        qa_ref[h] = qp * lo_q + ns * hi_q if h % 2 == 0 else ns * lo_q + qp * hi_q
    m_ref[...] = jnp.full(m_ref.shape, NEG_INF, F32)
    acc_ref[...] = jnp.zeros_like(acc_ref)

    kv = lambda h: 2 * (h // NSA_REP) + h % 2

    def rows_of(j):
        return pl.ds(pl.multiple_of(j * tk, tk), tk)

    def scores(j, h):
        return lax.dot_general(kk_ref[kv(h), rows_of(j), :], qa_ref[h], (((1,), (1,)), ((), ())),
                               preferred_element_type=F32)

    def tile(j, near, j_next):
        rows = rows_of(j)
        queue = [pre_ref[i] for i in range(ahead)]
        for h in range(NSA_HEADS):
            if h + ahead < NSA_HEADS:
                queue.append(scores(j, h + ahead))
            elif j_next is not None:
                pre_ref[h + ahead - NSA_HEADS] = scores(j_next, h + ahead - NSA_HEADS)
            s = queue.pop(0)
            m_old = m_ref[h]
            if near is None:
                far = tab_ref[(REL_BUCKETS - 1) * NSA_HEADS + h] * LOG2E
                m_new = jnp.maximum(m_old, jnp.max(s, axis=0, keepdims=True) + far)
                p = jnp.exp2(s - (m_new - far)).astype(BF16)
            else:
                s = bs_ref[h, near * tk:(near + 1) * tk, :] + s
                m_new = jnp.maximum(m_old, jnp.max(s, axis=0, keepdims=True))
                p = jnp.exp2(s - m_new).astype(BF16)
            m_ref[h] = m_new
            vr = _vt_rows(h)
            acc_ref[h, vr] = jnp.exp2(m_old - m_new) * acc_ref[h, vr] + jnp.dot(vt_ref[kv(h), vr, rows], p,
                                                                                preferred_element_type=F32)

    jd = (qi * t) // tk
    for i in range(ahead):
        pre_ref[i] = scores(0, i)

    def body(j, c):
        tile(j, None, j + 1)
        return c

    lax.fori_loop(0, jnp.maximum(jd - 1, 0), body, 0)

    @pl.when(qi > 0)
    def _():
        tile(jd - 1, 0, jd)

    for dj in range(n_near - 1):
        tile(jd + dj, 1 + dj, jd + dj + 1 if dj + 2 < n_near else None)

    gl = gl_ref[...]
    for hp in range(NSA_HEADS // 2):
        ls = slice(hp * LANES, (hp + 1) * LANES)
        a0, a1 = acc_ref[2 * hp].T, acc_ref[2 * hp + 1].T
        c0 = NSA_HEADS + 2 * hp
        o = jnp.where(lo, a0 / jnp.maximum(a0[:, HEAD_DIM:HEAD_DIM + 1], TINY) * jax.nn.sigmoid(gl[:, c0:c0 + 1]),
                      a1 / jnp.maximum(a1[:, HEAD_DIM - 1:HEAD_DIM], TINY) * jax.nn.sigmoid(gl[:, c0 + 1:c0 + 2]))
        o_ref[:, ls] = (o * _silu(gate_ref[:, ls].astype(F32))).astype(o_ref.dtype)


def _sel_attn(tab_flat, p, ps, notsel, bsel, b, s, t=512, tk=256, ahead=2):
    nq = s // t
    w = GROUP_W
    kw = NSA_KV_HEADS * LANES
    n_near = t // tk + 1
    return pl.pallas_call(
        functools.partial(_sel_attn_kernel, t=t, tk=tk, ahead=ahead),
        grid=(b, nq),
        in_specs=[
            pl.BlockSpec(memory_space=pltpu.SMEM),
            pl.BlockSpec((t, w), lambda bi, qi: (bi * nq + qi, OFF_NSA_Q // w)),
            pl.BlockSpec((t, kw), lambda bi, qi: (bi * nq + qi, 0)),
            pl.BlockSpec((s, kw), lambda bi, qi: (bi, OFF_SEL_K // kw)),
            pl.BlockSpec((s, kw), lambda bi, qi: (bi, OFF_SEL_V // kw)),
            pl.BlockSpec((t, LANES), lambda bi, qi: (bi * nq + qi, 2)),
            pl.BlockSpec((t, w), lambda bi, qi: (bi * nq + qi, OFF_NSA_G // w)),
            pl.BlockSpec((NSA_HEADS, n_near * tk, t), lambda bi, qi: (0, 0, 0), pipeline_mode=pl.Buffered(1)),
        ],
        out_specs=pl.BlockSpec((t, w), lambda bi, qi: (bi * nq + qi, 0)),
        out_shape=jax.ShapeDtypeStruct((b * s, w), BF16),
        scratch_shapes=[
            pltpu.VMEM((NSA_HEADS, t, LANES), BF16),
            pltpu.VMEM((2 * NSA_KV_HEADS, s, LANES), BF16),
            pltpu.VMEM((2 * NSA_KV_HEADS, LANES, s), BF16),
            pltpu.VMEM((NSA_HEADS, 1, t), F32),
            pltpu.VMEM((NSA_HEADS, LANES, t), F32),
            pltpu.VMEM((ahead, tk, t), F32),
        ],
        compiler_params=_params(("parallel", "arbitrary")),
        name="nsa_sel_attn",
    )(tab_flat, p, notsel, p, p, ps, p, bsel)


def _win_attn_kernel(q_ref, k_ref, v_ref, gl_ref, gate_ref, bw_ref, o_ref, qe_ref, vt_ref, m_ref, acc_ref,
                     pre_ref, *, t, tk, ahead):
    qi = pl.program_id(1)
    lo_q, hi_q = _half_masks(t)
    lo = lax.broadcasted_iota(jnp.int32, (t, LANES), 1) < HEAD_DIM
    n_before = WINDOW // tk
    n_tiles = n_before + t // tk

    @pl.when(qi == 0)
    def _():
        def merge(j, c):
            rows = pl.ds(pl.multiple_of(j * tk, tk), tk)
            for g in range(NSA_KV_HEADS):
                vd = v_ref[rows, g * LANES:(g + 1) * LANES]
                _store_vt(vt_ref, 2 * g, rows, vd)
            return c

        lax.fori_loop(0, v_ref.shape[0] // tk, merge, 0)

    for h in range(NSA_HEADS):
        qp = q_ref[:, (h // 2) * LANES:(h // 2 + 1) * LANES]
        qe_ref[h] = qp * lo_q if h % 2 == 0 else qp * hi_q
    m_ref[...] = jnp.full(m_ref.shape, NEG_INF, F32)
    acc_ref[...] = jnp.zeros_like(acc_ref)

    def rows_of(w):
        return pl.ds(pl.multiple_of((qi * (t // tk) - n_before + w) * tk, tk), tk)

    def scores(w, h):
        g = h // NSA_REP
        return lax.dot_general(k_ref[rows_of(w), g * LANES:(g + 1) * LANES], qe_ref[h], (((1,), (1,)), ((), ())),
                               preferred_element_type=F32)

    def tile(w):
        rows = rows_of(w)
        queue = [pre_ref[i] for i in range(ahead)]
        for h in range(NSA_HEADS):
            if h + ahead < NSA_HEADS:
                queue.append(scores(w, h + ahead))
            elif w + 1 < n_tiles:
                pre_ref[h + ahead - NSA_HEADS] = scores(w + 1, h + ahead - NSA_HEADS)
            s = bw_ref[h, w * tk:(w + 1) * tk, :] + queue.pop(0)
            m_old = m_ref[h]
            m_new = jnp.maximum(m_old, jnp.max(s, axis=0, keepdims=True))
            m_ref[h] = m_new
            p = jnp.exp2(s - m_new).astype(BF16)
            vr = _vt_rows(h)
            acc_ref[h, vr] = jnp.exp2(m_old - m_new) * acc_ref[h, vr] + jnp.dot(
                vt_ref[2 * (h // NSA_REP) + h % 2, vr, rows], p, preferred_element_type=F32)

    first = jnp.where(qi > 0, 0, n_before)
    for i in range(ahead):
        pre_ref[i] = scores(first, i)

    @pl.when(qi > 0)
    def _():
        for w in range(n_before):
            tile(w)

    for w in range(n_before, n_tiles):
        tile(w)

    gl = gl_ref[...]
    for hp in range(NSA_HEADS // 2):
        ls = slice(hp * LANES, (hp + 1) * LANES)
        a0, a1 = acc_ref[2 * hp].T, acc_ref[2 * hp + 1].T
        c0 = 2 * NSA_HEADS + 2 * hp
        o = jnp.where(lo, a0 / jnp.maximum(a0[:, HEAD_DIM:HEAD_DIM + 1], TINY) * jax.nn.sigmoid(gl[:, c0:c0 + 1]),
                      a1 / jnp.maximum(a1[:, HEAD_DIM - 1:HEAD_DIM], TINY) * jax.nn.sigmoid(gl[:, c0 + 1:c0 + 2]))
        o_ref[:, ls] = (o * _silu(gate_ref[:, ls].astype(F32))).astype(o_ref.dtype)


def _win_attn(p, ps, bwin, b, s, t=WIN_T, tk=WIN_TK, ahead=2):
    assert t % WINDOW == 0 and WINDOW % tk == 0
    nq = s // t
    w = GROUP_W
    kw = NSA_KV_HEADS * LANES
    return pl.pallas_call(
        functools.partial(_win_attn_kernel, t=t, tk=tk, ahead=ahead),
        grid=(b, nq),
        in_specs=[
            pl.BlockSpec((t, w), lambda bi, qi: (bi * nq + qi, OFF_NSA_Q // w)),
            pl.BlockSpec((s, kw), lambda bi, qi: (bi, OFF_WIN_K // kw)),
            pl.BlockSpec((s, kw), lambda bi, qi: (bi, OFF_WIN_V // kw)),
            pl.BlockSpec((t, LANES), lambda bi, qi: (bi * nq + qi, 2)),
            pl.BlockSpec((t, w), lambda bi, qi: (bi * nq + qi, OFF_NSA_G // w)),
            pl.BlockSpec((NSA_HEADS, WINDOW + t, t), lambda bi, qi: (0, 0, 0), pipeline_mode=pl.Buffered(1)),
        ],
        out_specs=pl.BlockSpec((t, w), lambda bi, qi: (bi * nq + qi, 0)),
        out_shape=jax.ShapeDtypeStruct((b * s, w), BF16),
        scratch_shapes=[
            pltpu.VMEM((NSA_HEADS, t, LANES), BF16),
            pltpu.VMEM((2 * NSA_KV_HEADS, LANES, s), BF16),
            pltpu.VMEM((NSA_HEADS, 1, t), F32),
            pltpu.VMEM((NSA_HEADS, LANES, t), F32),
            pltpu.VMEM((ahead, tk, t), F32),
        ],
        compiler_params=_params(("parallel", "arbitrary")),
        name="nsa_win_attn",
    )(p, p, p, ps, p, bwin)


def _mem_attn_kernel(q_ref, gate_ref, kv_ref, o_ref):
    scale = MEM_HEAD_DIM ** -0.5
    for h in range(MEM_HEADS):
        ls = slice(h * LANES, (h + 1) * LANES)
        k = kv_ref[:, ls]
        v = kv_ref[:, GROUP_W + h * LANES:GROUP_W + (h + 1) * LANES]
        s = lax.dot_general(q_ref[:, ls], k, (((1,), (1,)), ((), ())), preferred_element_type=F32) * scale
        m = jnp.max(s, axis=1, keepdims=True)
        e = jnp.exp(s - m)
        l = jnp.sum(e, axis=1, keepdims=True)
        o = jnp.dot(e.astype(BF16), v, preferred_element_type=F32) / l
        o_ref[:, ls] = (o * _silu(gate_ref[:, ls].astype(F32))).astype(o_ref.dtype)


def _mem_attn(p, mem_kv, b, s, t=512):
    nq = s // t
    w = GROUP_W
    m = mem_kv.shape[0] // b
    return pl.pallas_call(
        _mem_attn_kernel,
        grid=(b, nq),
        in_specs=[
            pl.BlockSpec((t, w), lambda bi, qi: (bi * nq + qi, OFF_MEM_Q // w)),
            pl.BlockSpec((t, w), lambda bi, qi: (bi * nq + qi, OFF_MEM_G // w)),
            pl.BlockSpec((m, 2 * w), lambda bi, qi: (bi, 0)),
        ],
        out_specs=pl.BlockSpec((t, w), lambda bi, qi: (bi * nq + qi, 0)),
        out_shape=jax.ShapeDtypeStruct((b * s, w), BF16),
        compiler_params=_params(("parallel", "arbitrary")),
        name="mem_attn",
    )(p, p, mem_kv)


def _out_proj_kernel(x_ref, of_ref, os_ref, oc_ref, osel_ref, ow_ref, om_ref, w_ref, g_ref, o_ref, *, final):
    w = GROUP_W
    nsa = (oc_ref[...].astype(F32) + osel_ref[...].astype(F32) + ow_ref[...].astype(F32)).astype(BF16)
    acc = x_ref[...]
    for i, part in enumerate((of_ref[...], os_ref[...], nsa, om_ref[...])):
        acc = acc + jnp.dot(part, w_ref[i * w:(i + 1) * w, :], preferred_element_type=F32)
    if final:
        ms = jnp.mean(acc * acc, axis=-1, keepdims=True)
        acc = (acc * lax.rsqrt(ms + EPS)) * g_ref[...]
    o_ref[...] = acc


def _out_proj(x2d, parts, w_out, g, final, tm=512):
    n, d = x2d.shape
    w = GROUP_W
    part_spec = pl.BlockSpec((tm, w), lambda i: (i, 0))
    return pl.pallas_call(
        functools.partial(_out_proj_kernel, final=final),
        grid=(n // tm,),
        in_specs=[pl.BlockSpec((tm, d), lambda i: (i, 0))] + [part_spec] * 6 + [
            pl.BlockSpec((4 * w, d), lambda i: (0, 0)),
            pl.BlockSpec((1, d), lambda i: (0, 0)),
        ],
        out_specs=pl.BlockSpec((tm, d), lambda i: (i, 0)),
        out_shape=jax.ShapeDtypeStruct((n, d), F32),
        compiler_params=_params(("parallel",)),
        name="out_proj",
    )(x2d, *parts, w_out, g.reshape(1, d))


def _pack_in_proj(w_in_l):
    fox, ssm, nsa, mem = 0, FOX_COLS, FOX_COLS + SSM_COLS, FOX_COLS + SSM_COLS + NSA_COLS
    w = GROUP_W
    q_scale = HEAD_DIM ** -0.5 * LOG2E
    cols = lambda a, n: w_in_l[:, a:a + n]
    kv = lambda slot: nsa + w + slot * NSA_KV_W
    dup = lambda base: [cols(base, HEAD_DIM)] * 2 + [cols(base + HEAD_DIM, HEAD_DIM)] * 2
    main = ([cols(fox, w) * q_scale, cols(fox + w, 3 * w)]
            + [cols(ssm + w, SSM_CONV_DIM), cols(ssm, w)]
            + [cols(nsa, w) * q_scale, cols(nsa + w + 6 * NSA_KV_W + 3 * NSA_HEADS, w)]
            + [cols(mem, 2 * w)]
            + dup(kv(2)) + dup(kv(3)) + dup(kv(4)) + dup(kv(5))
            + [cols(kv(0), 2 * NSA_KV_W)])
    w_main = jnp.concatenate(main, axis=1).astype(BF16)
    assert w_main.shape[1] == P_COLS
    small = []
    for src, n, rep in ((fox + 4 * w, FOX_HEADS, N_SPLIT), (ssm + w + SSM_CONV_DIM, SSM_HEADS, 1),
                        (nsa + w + 6 * NSA_KV_W, 3 * NSA_HEADS, 1)):
        small += [cols(src, n)] * rep + [jnp.zeros((w_in_l.shape[0], LANES - rep * n), w_in_l.dtype)]
    return w_main, jnp.concatenate(small, axis=1).astype(BF16)


def _pad_lanes(v):
    return jnp.pad(v.astype(F32), (0, LANES - v.shape[0])).reshape(1, LANES)


def _trunk(x, mem, norm_g, w_in, fox_f_bias, ssm_conv_w, ssm_conv_b, ssm_dt_bias, ssm_a_log, ssm_d,
           ssm_norm_g, nsa_cmp_pe, nsa_cmp_w1, nsa_cmp_w2, rel_bias_table, mem_norm_g, w_mem_kv, w_out,
           final_norm_g):
    b, s, d = x.shape
    depth = w_in.shape[0]
    n = b * s
    m_tok = mem.shape[1]
    n_cmp = (s - CMP_BLOCK) // CMP_STRIDE + 1
    n_rows = s // CMP_STRIDE
    assert s % 512 == 0 and s // SEL_BLOCK <= HEAD_DIM and n_rows <= N_CMP_PAD and d == D_MODEL

    tab_flat = rel_bias_table.astype(F32).reshape(-1)
    unbounded = 1 << 30
    bwin = _t5_table(tab_flat, WINDOW + WIN_T, WIN_T, 1, WINDOW, WINDOW)
    bsel = _t5_table(tab_flat, (SEL_T // SEL_TK + 1) * SEL_TK, SEL_T, 1, SEL_TK, unbounded)
    fcmp = _t5_table(tab_flat, 2 * N_CMP_PAD, T_ATT, CMP_STRIDE, CMP_STRIDE * N_CMP_PAD - (CMP_BLOCK - 1), unbounded)

    cs = np.arange(N_CMP_PAD)[None, :] * CMP_STRIDE
    js = np.arange(SEL_BLOCK)[:, None] * SEL_BLOCK
    overlap_t = ((cs < js + SEL_BLOCK) & (cs + CMP_BLOCK > js) & (np.arange(N_CMP_PAD)[None, :] < n_cmp)
                 & (np.arange(SEL_BLOCK)[:, None] < s // SEL_BLOCK)).astype(np.float32)
    overlap_t = jnp.asarray(overlap_t, BF16)

    x2d = x.reshape(n, d)
    mem2d = mem.reshape(b * m_tok, d)
    half = CMP_STRIDE * HEAD_DIM
    for l in range(depth):
        w_main, w_small = _pack_in_proj(w_in[l])
        p, ps = _norm_proj(x2d, norm_g[l], w_main, w_small)

        qaug, kaug = _fox_cumsum(ps, _pad_lanes(jnp.tile(fox_f_bias[l], N_SPLIT)), b, s)
        o_fox = _fox_attn(p, qaug, kaug, b, s)

        o_ssd = _ssd(p, ps, ssm_conv_w[l].astype(F32), ssm_conv_b[l].reshape(1, -1).astype(F32),
                     _pad_lanes(ssm_dt_bias[l]), _pad_lanes(ssm_a_log[l]),
                     jnp.repeat(ssm_d[l].astype(F32), HEAD_DIM).reshape(1, GROUP_W),
                     ssm_norm_g[l].reshape(1, GROUP_W).astype(F32), b, s)

        kvc = p[:, OFF_CMP_KV:OFF_CMP_KV + 2 * NSA_KV_W]
        r = kvc.reshape(b, n_rows, CMP_STRIDE, 2 * NSA_KV_HEADS, HEAD_DIM).transpose(0, 3, 1, 2, 4)
        r = r.reshape(b, 2 * NSA_KV_HEADS, n_rows, half)
        if n_rows < N_CMP_PAD:
            r = jnp.pad(r, ((0, 0), (0, 0), (0, N_CMP_PAD - n_rows), (0, 0)))
        w1 = nsa_cmp_w1[l].astype(BF16)
        w2dup = jnp.concatenate([nsa_cmp_w2[l], nsa_cmp_w2[l]], axis=-1).astype(BF16)
        pe8 = jnp.broadcast_to(nsa_cmp_pe[l].astype(F32).reshape(2, 1, 2 * half), (2, 8, 2 * half))
        kv_cmp, kv_cmp_t = _compress(r, w1[:, :half], w1[:, half:], w2dup, pe8)
        o_cmp, notsel = _cmp_select(p, ps, kv_cmp, kv_cmp_t, fcmp, overlap_t, b, s)
        o_sel = _sel_attn(tab_flat, p, ps, notsel, bsel, b, s)
        o_win = _win_attn(p, ps, bwin, b, s)

        w_kv = w_mem_kv[l].astype(BF16)
        mem_kv, _ = _norm_proj(mem2d, mem_norm_g[l], w_kv, w_kv[:, :LANES], tm=min(512, b * m_tok))
        o_mem = _mem_attn(p, mem_kv, b, s)

        x2d = _out_proj(x2d, (o_fox, o_ssd, o_cmp, o_sel, o_win, o_mem), w_out[l].astype(BF16),
                        final_norm_g, final=(l == depth - 1))
    return x2d.reshape(b, s, d)


def kernel(x, mem, norm_g, w_in, fox_f_bias, ssm_conv_w, ssm_conv_b, ssm_dt_bias, ssm_a_log, ssm_d, ssm_norm_g,
           nsa_cmp_pe, nsa_cmp_w1, nsa_cmp_w2, rel_bias_table, mem_norm_g, w_mem_kv, w_out, final_norm_g):
    return _trunk(x, mem, norm_g, w_in, fox_f_bias, ssm_conv_w, ssm_conv_b, ssm_dt_bias, ssm_a_log, ssm_d,
                  ssm_norm_g, nsa_cmp_pe, nsa_cmp_w1, nsa_cmp_w2, rel_bias_table, mem_norm_g, w_mem_kv, w_out,
                  final_norm_g)
```

```python
import functools
import math

import numpy as np
import jax
import jax.numpy as jnp
from jax import lax
from jax.experimental import pallas as pl
from jax.experimental.pallas import tpu as pltpu

F32 = jnp.float32
BF16 = jnp.bfloat16

D_MODEL = 1024
GROUP_W = 512
HEAD_DIM = 64
EPS = 1e-6
NEG_INF = -1e30
TINY = 1e-30
LOG2E = math.log2(math.e)

FOX_HEADS = 8
SSM_HEADS = 8
SSM_STATE = 128
SSM_GROUPS = 2
SSM_CONV = 4
SSM_CHUNK = 128
SSM_CONV_DIM = GROUP_W + 2 * SSM_GROUPS * SSM_STATE

NSA_HEADS = 8
NSA_KV_HEADS = 2
NSA_REP = NSA_HEADS // NSA_KV_HEADS
NSA_KV_W = NSA_KV_HEADS * HEAD_DIM
CMP_BLOCK = 32
CMP_STRIDE = 16
CMP_HIDDEN = 2 * HEAD_DIM
SEL_BLOCK = 64
SEL_TOPK = 16
WINDOW = 512
SEL_FORCE = 1e9

MEM_HEADS = 4
MEM_HEAD_DIM = GROUP_W // MEM_HEADS
REL_BUCKETS = 32
REL_MAX_DIST = 128

FOX_COLS = 4 * GROUP_W + FOX_HEADS
SSM_COLS = GROUP_W + SSM_CONV_DIM + SSM_HEADS
NSA_COLS = 2 * GROUP_W + 6 * NSA_KV_W + 3 * NSA_HEADS
MEM_COLS = 2 * GROUP_W

LANES = 128
VMEM_LIMIT = 56 * 1024 * 1024

OFF_FOX_Q, OFF_FOX_K, OFF_FOX_V, OFF_FOX_G = 0, 512, 1024, 1536
OFF_SSM_XBC, OFF_SSM_Z = 2048, 3072
OFF_NSA_Q, OFF_NSA_G = 3584, 4096
OFF_MEM_Q, OFF_MEM_G = 4608, 5120
OFF_SEL_K, OFF_SEL_V, OFF_WIN_K, OFF_WIN_V, OFF_CMP_KV = 5632, 5888, 6144, 6400, 6656
P_COLS = 6912
PS_COLS = 3 * LANES

T_ATT = 256
SEL_T, SEL_TK = 512, 256
WIN_T, WIN_TK = 512, 256
N_CMP_PAD = 256


def _params(sem):
    return pltpu.CompilerParams(dimension_semantics=sem, vmem_limit_bytes=VMEM_LIMIT)


def _t5_bucket_np(dist):
    n = np.maximum(dist, 0)
    max_exact = REL_BUCKETS // 2
    nf = np.maximum(n, 1).astype(np.float32)
    large = max_exact + (np.log(nf / np.float32(max_exact)) / np.float32(math.log(REL_MAX_DIST / max_exact))
                         * np.float32(REL_BUCKETS - max_exact)).astype(np.int32)
    large = np.minimum(large, REL_BUCKETS - 1)
    return np.where(n < max_exact, n, large).astype(np.int32)


def _silu(x):
    h = 0.5 * x
    return h + h * jnp.tanh(h)


def _norm_proj_kernel(x_ref, g_ref, w_ref, ws_ref, p_ref, ps_ref, *, chunk):
    x = x_ref[...]
    ms = jnp.mean(x * x, axis=-1, keepdims=True)
    h = ((x * lax.rsqrt(ms + EPS)) * g_ref[...]).astype(BF16)
    ncol = p_ref.shape[1]
    for c0 in range(0, ncol, chunk):
        c1 = min(c0 + chunk, ncol)
        p_ref[:, c0:c1] = jnp.dot(h, w_ref[:, c0:c1], preferred_element_type=F32).astype(p_ref.dtype)
    ps_ref[...] = jnp.dot(h, ws_ref[...], preferred_element_type=F32)


def _norm_proj(x2d, g, w_main, w_small, tm=512):
    n, d = x2d.shape
    pc, sc = w_main.shape[1], w_small.shape[1]
    return pl.pallas_call(
        functools.partial(_norm_proj_kernel, chunk=512),
        grid=(n // tm,),
        in_specs=[
            pl.BlockSpec((tm, d), lambda i: (i, 0)),
            pl.BlockSpec((1, d), lambda i: (0, 0)),
            pl.BlockSpec((d, pc), lambda i: (0, 0)),
            pl.BlockSpec((d, sc), lambda i: (0, 0)),
        ],
        out_specs=[
            pl.BlockSpec((tm, pc), lambda i: (i, 0)),
            pl.BlockSpec((tm, sc), lambda i: (i, 0)),
        ],
        out_shape=[jax.ShapeDtypeStruct((n, pc), BF16), jax.ShapeDtypeStruct((n, sc), F32)],
        compiler_params=_params(("parallel",)),
        name="norm_proj",
    )(x2d, g.reshape(1, d), w_main, w_small)


def _t5_table_kernel(tab_ref, bucket_ref, o_ref, *, stride, off, limit):
    h = pl.program_id(0)
    rows, cols = o_ref.shape[1], o_ref.shape[2]
    bucket = bucket_ref[...]
    near = jnp.zeros(bucket.shape, F32)
    for b in range(REL_BUCKETS):
        near = jnp.where(bucket == b, tab_ref[b * NSA_HEADS + h] * LOG2E, near)
    far = tab_ref[(REL_BUCKETS - 1) * NSA_HEADS + h] * LOG2E
    base = jnp.concatenate([near] * (rows // near.shape[0]), axis=0)
    rolled = pltpu.roll(base, 0, 1, stride=stride, stride_axis=0)
    d = (lax.broadcasted_iota(jnp.int32, (rows, cols), 1) - stride * lax.broadcasted_iota(jnp.int32, (rows, cols), 0)
         + off)
    o_ref[0] = jnp.where((d < 0) | (d >= limit), NEG_INF, jnp.where(d >= REL_MAX_DIST, far, rolled))


def _t5_table(tab_flat, rows, cols, stride, off, limit):
    assert cols >= 2 * REL_MAX_DIST and rows % 8 == 0
    k = (np.arange(cols) + off) % cols
    bucket = np.broadcast_to(np.where(k < REL_MAX_DIST, _t5_bucket_np(k), -1).astype(np.int32), (8, cols))
    return pl.pallas_call(
        functools.partial(_t5_table_kernel, stride=stride, off=off, limit=limit),
        grid=(NSA_HEADS,),
        in_specs=[
            pl.BlockSpec(memory_space=pltpu.SMEM),
            pl.BlockSpec((8, cols), lambda h: (0, 0)),
        ],
        out_specs=pl.BlockSpec((1, rows, cols), lambda h: (h, 0, 0)),
        out_shape=jax.ShapeDtypeStruct((NSA_HEADS, rows, cols), F32),
        compiler_params=_params(("arbitrary",)),
        name="t5_table",
    )(tab_flat, jnp.asarray(bucket))


def _tri_lower(n):
    r = lax.broadcasted_iota(jnp.int32, (n, n), 0)
    c = lax.broadcasted_iota(jnp.int32, (n, n), 1)
    return (r >= c).astype(F32)


N_SPLIT = 3


def _fox_aug_lane(h, i):
    return LANES * (h // 2) + (HEAD_DIM if h % 2 == 0 else 0) + i


def _fox_aug_consts():
    pq = np.zeros((LANES, GROUP_W), np.float32)
    pk = np.zeros((LANES, GROUP_W), np.float32)
    oq = np.zeros((1, GROUP_W), np.float32)
    ok = np.zeros((1, GROUP_W), np.float32)
    for h in range(FOX_HEADS):
        for i in range(N_SPLIT):
            pq[i * FOX_HEADS + h, _fox_aug_lane(h, i)] = 1.0
            pk[i * FOX_HEADS + h, _fox_aug_lane(h, N_SPLIT + i)] = -1.0
            oq[0, _fox_aug_lane(h, N_SPLIT + i)] = 1.0
            ok[0, _fox_aug_lane(h, i)] = 1.0
    return pq, pk, oq, ok


def _fox_cumsum_kernel(f_ref, b_ref, pq_ref, pk_ref, oq_ref, ok_ref, qa_ref, ka_ref, carry_ref, *, ts):
    @pl.when(pl.program_id(1) == 0)
    def _():
        carry_ref[...] = jnp.zeros_like(carry_ref)

    z = f_ref[...] + b_ref[...]
    logf = (jnp.minimum(z, 0.0) - jnp.log(1.0 + jnp.exp(-jnp.abs(z)))) * LOG2E
    tri = _tri_lower(LANES).astype(BF16)
    group = lax.broadcasted_iota(jnp.int32, (LANES, LANES), 1) // FOX_HEADS
    carry = carry_ref[...]
    for c in range(ts // LANES):
        rows = slice(c * LANES, (c + 1) * LANES)
        cs, rest = carry, logf[rows]
        for _ in range(N_SPLIT):
            piece = rest.astype(BF16)
            rest = rest - piece.astype(F32)
            cs = cs + jnp.dot(tri, piece, preferred_element_type=F32)
        carry = cs[LANES - 1:LANES, :]
        cat, rest = None, cs
        for i in range(N_SPLIT):
            piece = rest.astype(BF16).astype(F32)
            rest = rest - piece
            cat = piece if cat is None else jnp.where(group == i, piece, cat)
        cat = cat.astype(BF16)
        qa_ref[rows, :] = (oq_ref[...] + jnp.dot(cat, pq_ref[...], preferred_element_type=F32)).astype(BF16)
        ka_ref[rows, :] = (ok_ref[...] + jnp.dot(cat, pk_ref[...], preferred_element_type=F32)).astype(BF16)
    carry_ref[...] = carry


def _fox_cumsum(ps, f_bias_pad, b, s, ts=512):
    ns = s // ts
    pq, pk, oq, ok = _fox_aug_consts()
    const = lambda bi, si: (0, 0)
    return pl.pallas_call(
        functools.partial(_fox_cumsum_kernel, ts=ts),
        grid=(b, ns),
        in_specs=[
            pl.BlockSpec((ts, LANES), lambda bi, si: (bi * ns + si, 0)),
            pl.BlockSpec((1, LANES), const),
            pl.BlockSpec(pq.shape, const),
            pl.BlockSpec(pk.shape, const),
            pl.BlockSpec(oq.shape, const),
            pl.BlockSpec(ok.shape, const),
        ],
        out_specs=[
            pl.BlockSpec((ts, GROUP_W), lambda bi, si: (bi * ns + si, 0)),
            pl.BlockSpec((ts, GROUP_W), lambda bi, si: (bi * ns + si, 0)),
        ],
        out_shape=[jax.ShapeDtypeStruct((b * s, GROUP_W), BF16), jax.ShapeDtypeStruct((b * s, GROUP_W), BF16)],
        scratch_shapes=[pltpu.VMEM((1, LANES), F32)],
        compiler_params=_params(("parallel", "arbitrary")),
        name="fox_cumsum",
    )(ps, f_bias_pad, jnp.asarray(pq, BF16), jnp.asarray(pk, BF16), jnp.asarray(oq), jnp.asarray(ok))


ONES_ROWS = 16


def _vt_rows(h):
    return slice(0, HEAD_DIM + ONES_ROWS) if h % 2 == 0 else slice(HEAD_DIM - ONES_ROWS, LANES)


def _store_vt(vt_ref, i, rows, v_pair):
    vt = v_pair.astype(F32).T.astype(BF16)
    ones = jnp.ones((HEAD_DIM, v_pair.shape[0]), BF16)
    vt_ref[i, 0:HEAD_DIM, rows] = vt[0:HEAD_DIM]
    vt_ref[i, HEAD_DIM:LANES, rows] = ones
    vt_ref[i + 1, 0:HEAD_DIM, rows] = ones
    vt_ref[i + 1, HEAD_DIM:LANES, rows] = vt[HEAD_DIM:LANES]


def _half_masks(rows):
    lo = jnp.where(lax.broadcasted_iota(jnp.int32, (rows, LANES), 1) < HEAD_DIM, 1.0, 0.0)
    return lo.astype(BF16), (1.0 - lo).astype(BF16)


def _fox_attn_kernel(q_ref, qa_ref, k_ref, ka_ref, v_ref, gate_ref, o_ref, qs_ref, kk_ref, vt_ref, m_ref, acc_ref,
                     pre_ref, *, t, tk, ahead):
    qi = pl.program_id(1)
    lo_q, hi_q = _half_masks(t)
    lo = lax.broadcasted_iota(jnp.int32, (t, LANES), 1) < HEAD_DIM
    cm = lax.broadcasted_iota(jnp.int32, (tk, t), 0) - lax.broadcasted_iota(jnp.int32, (tk, t), 1)

    @pl.when(qi == 0)
    def _():
        lo_k, hi_k = _half_masks(tk)

        def merge(j, c):
            rows = pl.ds(pl.multiple_of(j * tk, tk), tk)
            for hp in range(FOX_HEADS // 2):
                ls = slice(hp * LANES, (hp + 1) * LANES)
                kp, ka, vp = k_ref[rows, ls], ka_ref[rows, ls], v_ref[rows, ls]
                kk_ref[2 * hp, rows, :] = kp * lo_k + ka * hi_k
                kk_ref[2 * hp + 1, rows, :] = ka * lo_k + kp * hi_k
                _store_vt(vt_ref, 2 * hp, rows, vp)
            return c

        lax.fori_loop(0, k_ref.shape[0] // tk, merge, 0)

    for hp in range(FOX_HEADS // 2):
        ls = slice(hp * LANES, (hp + 1) * LANES)
        qp, qa = q_ref[:, ls], qa_ref[:, ls]
        qs_ref[2 * hp] = (qp * lo_q + qa * hi_q).astype(F32).T.astype(BF16)
        qs_ref[2 * hp + 1] = (qa * lo_q + qp * hi_q).astype(F32).T.astype(BF16)
    m_ref[...] = jnp.full(m_ref.shape, NEG_INF, F32)
    acc_ref[...] = jnp.zeros_like(acc_ref)

    def rows_of(j):
        return pl.ds(pl.multiple_of(j * tk, tk), tk)

    def scores(j, h):
        return jnp.dot(kk_ref[h, rows_of(j), :], qs_ref[h], preferred_element_type=F32)

    def tile(j, diag, j_next):
        queue = [pre_ref[i] for i in range(ahead)]
        for h in range(FOX_HEADS):
            if h + ahead < FOX_HEADS:
                queue.append(scores(j, h + ahead))
            elif j_next is not None:
                pre_ref[h + ahead - FOX_HEADS] = scores(j_next, h + ahead - FOX_HEADS)
            s = queue.pop(0)
            if diag:
                s = jnp.where(cm <= qi * t - j * tk, s, NEG_INF)
            m_old = m_ref[h]
            m_new = jnp.maximum(m_old, jnp.max(s, axis=0, keepdims=True))
            m_ref[h] = m_new
            p = jnp.exp2(s - m_new).astype(BF16)
            vr = _vt_rows(h)
            acc_ref[h, vr] = jnp.exp2(m_old - m_new) * acc_ref[h, vr] + jnp.dot(vt_ref[h, vr, rows_of(j)], p,
                                                                                preferred_element_type=F32)

    jd = (qi * t) // tk
    n_diag = max(t // tk, 1)
    for i in range(ahead):
        pre_ref[i] = scores(0, i)

    def body(j, c):
        tile(j, False, j + 1)
        return c

    lax.fori_loop(0, jd, body, 0)
    for dj in range(n_diag):
        tile(jd + dj, True, jd + dj + 1 if dj + 1 < n_diag else None)

    for hp in range(FOX_HEADS // 2):
        ls = slice(hp * LANES, (hp + 1) * LANES)
        a0, a1 = acc_ref[2 * hp].T, acc_ref[2 * hp + 1].T
        o = jnp.where(lo, a0 / jnp.maximum(a0[:, HEAD_DIM:HEAD_DIM + 1], TINY),
                      a1 / jnp.maximum(a1[:, HEAD_DIM - 1:HEAD_DIM], TINY))
        o_ref[:, ls] = (o * _silu(gate_ref[:, ls].astype(F32))).astype(o_ref.dtype)


def _fox_attn(p, qaug, kaug, b, s, t=512, tk=256, ahead=2):
    nq = s // t
    w = GROUP_W
    return pl.pallas_call(
        functools.partial(_fox_attn_kernel, t=t, tk=tk, ahead=ahead),
        grid=(b, nq),
        in_specs=[
            pl.BlockSpec((t, w), lambda bi, qi: (bi * nq + qi, OFF_FOX_Q // w)),
            pl.BlockSpec((t, w), lambda bi, qi: (bi * nq + qi, 0)),
            pl.BlockSpec((s, w), lambda bi, qi: (bi, OFF_FOX_K // w)),
            pl.BlockSpec((s, w), lambda bi, qi: (bi, 0)),
            pl.BlockSpec((s, w), lambda bi, qi: (bi, OFF_FOX_V // w)),
            pl.BlockSpec((t, w), lambda bi, qi: (bi * nq + qi, OFF_FOX_G // w)),
        ],
        out_specs=pl.BlockSpec((t, w), lambda bi, qi: (bi * nq + qi, 0)),
        out_shape=jax.ShapeDtypeStruct((b * s, w), BF16),
        scratch_shapes=[
            pltpu.VMEM((FOX_HEADS, LANES, t), BF16),
            pltpu.VMEM((FOX_HEADS, s, LANES), BF16),
            pltpu.VMEM((FOX_HEADS, LANES, s), BF16),
            pltpu.VMEM((FOX_HEADS, 1, t), F32),
            pltpu.VMEM((FOX_HEADS, LANES, t), F32),
            pltpu.VMEM((ahead, tk, t), F32),
        ],
        compiler_params=_params(("parallel", "arbitrary")),
        name="fox_attn",
    )(p, qaug, p, kaug, p, p)


def _dot_split(lhs_f32, rhs_b):
    out, rest = None, lhs_f32
    for _ in range(N_SPLIT):
        piece = rest.astype(BF16)
        rest = rest - piece.astype(F32)
        term = jnp.dot(piece, rhs_b, preferred_element_type=F32)
        out = term if out is None else out + term
    return out


def _ssd_kernel(z_ref, xbc_ref, dt_ref, cw_ref, cb_ref, dtb_ref, alog_ref, dsk_ref, ng_ref, o_ref,
                xpad_ref, xc_ref, state_ref, y_ref, *, nch):
    q = SSM_CHUNK
    rows_all = nch * q
    halo = 8

    @pl.when(pl.program_id(1) == 0)
    def _():
        xpad_ref[0:halo, :] = jnp.zeros((halo, SSM_CONV_DIM), F32)
        state_ref[...] = jnp.zeros_like(state_ref)

    xpad_ref[halo:halo + rows_all, :] = xbc_ref[...].astype(F32)
    y = cb_ref[...]
    for k in range(SSM_CONV):
        off = halo - (SSM_CONV - 1) + k
        y = y + cw_ref[k:k + 1, :] * xpad_ref[off:off + rows_all, :]
    xpad_ref[0:halo, :] = xpad_ref[rows_all:rows_all + halo, :]
    xc_ref[...] = _silu(y)

    x_dt = dt_ref[...] + dtb_ref[...]
    dt_all = jnp.maximum(x_dt, 0.0) + jnp.log(1.0 + jnp.exp(-jnp.abs(x_dt)))
    a_all = dt_all * (-jnp.exp(alog_ref[...]))
    tri_t = (lax.broadcasted_iota(jnp.int32, (q, q), 0) <= lax.broadcasted_iota(jnp.int32, (q, q), 1)).astype(BF16)
    er = lax.broadcasted_iota(jnp.int32, (LANES, GROUP_W), 0)
    ec = lax.broadcasted_iota(jnp.int32, (LANES, GROUP_W), 1)
    expand = jnp.where(ec // HEAD_DIM == er, 1.0, 0.0).astype(BF16)
    row = lax.broadcasted_iota(jnp.int32, (q, q), 0)
    col = lax.broadcasted_iota(jnp.int32, (q, q), 1)
    causal = row >= col
    lo = lax.broadcasted_iota(jnp.int32, (q, LANES), 1) < HEAD_DIM
    gw = GROUP_W // SSM_GROUPS
    hpg = SSM_HEADS // SSM_GROUPS

    for c in range(nch):
        rs = slice(c * q, (c + 1) * q)
        xs = xc_ref[rs, :GROUP_W]
        dt = dt_all[rs]
        acs_t = _dot_split(a_all[rs].T, tri_t)
        acs = acs_t.T
        stacked = jnp.concatenate([dt, jnp.exp(acs), jnp.exp(acs[q - 1:q, :] - acs)], axis=0)
        full = _dot_split(stacked, expand)
        dt_full, eacs_full, dec_full = full[0:q], full[q:2 * q], full[2 * q:3 * q]
        xdt = xs * dt_full
        xdt_b = xdt.astype(BF16)
        xdec_b = (xdt * dec_full).astype(BF16)

        for g in range(SSM_GROUPS):
            bm = xc_ref[rs, GROUP_W + g * SSM_STATE:GROUP_W + (g + 1) * SSM_STATE]
            cm = xc_ref[rs, GROUP_W + (SSM_GROUPS + g) * SSM_STATE:GROUP_W + (SSM_GROUPS + g + 1) * SSM_STATE]
            bm_b = bm.astype(BF16)
            cm_b = cm.astype(BF16)
            gs = slice(g * gw, (g + 1) * gw)
            cbg = lax.dot_general(cm_b, bm_b, (((1,), (1,)), ((), ())), preferred_element_type=F32)
            st = state_ref[:, gs]
            y_off = jnp.dot(cm_b, st.astype(BF16), preferred_element_type=F32) * eacs_full[:, gs]
            cst = jnp.dot(bm.T.astype(BF16), xdec_b[:, gs], preferred_element_type=F32)
            state_ref[:, gs] = st * eacs_full[q - 1:q, gs] + cst
            for hp in range(hpg // 2):
                ls = slice(g * gw + hp * LANES, g * gw + (hp + 1) * LANES)
                yd = []
                for e in range(2):
                    h = g * hpg + 2 * hp + e
                    seg = jnp.exp(jnp.where(causal, acs[:, h:h + 1] - acs_t[h:h + 1, :], NEG_INF))
                    yd.append(jnp.dot((cbg * seg).astype(BF16), xdt_b[:, ls], preferred_element_type=F32))
                y_ref[:, ls] = jnp.where(lo, yd[0], yd[1]) + y_off[:, hp * LANES:(hp + 1) * LANES]

        yz = (y_ref[...] + xs * dsk_ref[...]) * _silu(z_ref[rs, :].astype(F32))
        for g in range(SSM_GROUPS):
            gs = slice(g * gw, (g + 1) * gw)
            blk = yz[:, gs]
            ms = jnp.mean(blk * blk, axis=-1, keepdims=True)
            o_ref[rs, gs] = ((blk * lax.rsqrt(ms + EPS)) * ng_ref[:, gs]).astype(o_ref.dtype)


def _ssd(p, ps, conv_w, conv_b, dt_bias_pad, a_log_pad, d_full, norm_g, b, s, nch=4):
    q = SSM_CHUNK
    rows = nch * q
    nc = s // rows
    row = lambda bi, ci: (bi * nc + ci)
    const = lambda bi, ci: (0, 0)
    return pl.pallas_call(
        functools.partial(_ssd_kernel, nch=nch),
        grid=(b, nc),
        in_specs=[
            pl.BlockSpec((rows, GROUP_W), lambda bi, ci: (row(bi, ci), OFF_SSM_Z // GROUP_W)),
            pl.BlockSpec((rows, SSM_CONV_DIM), lambda bi, ci: (row(bi, ci), OFF_SSM_XBC // SSM_CONV_DIM)),
            pl.BlockSpec((rows, LANES), lambda bi, ci: (row(bi, ci), 1)),
            pl.BlockSpec((SSM_CONV, SSM_CONV_DIM), const),
            pl.BlockSpec((1, SSM_CONV_DIM), const),
            pl.BlockSpec((1, LANES), const),
            pl.BlockSpec((1, LANES), const),
            pl.BlockSpec((1, GROUP_W), const),
            pl.BlockSpec((1, GROUP_W), const),
        ],
        out_specs=pl.BlockSpec((rows, GROUP_W), lambda bi, ci: (row(bi, ci), 0)),
        out_shape=jax.ShapeDtypeStruct((b * s, GROUP_W), BF16),
        scratch_shapes=[
            pltpu.VMEM((rows + 8, SSM_CONV_DIM), F32),
            pltpu.VMEM((rows, SSM_CONV_DIM), F32),
            pltpu.VMEM((SSM_STATE, GROUP_W), F32),
            pltpu.VMEM((q, GROUP_W), F32),
        ],
        compiler_params=_params(("parallel", "arbitrary")),
        name="ssd",
    )(p, p, ps, conv_w, conv_b, dt_bias_pad, a_log_pad, d_full, norm_g)


def _compress_kernel(x_ref, w1_ref, w2_ref, pe_ref, o_ref, ot_ref, xf_ref):
    s = x_ref.shape[0]
    n = s // CMP_STRIDE
    for sb in range(2):
        xf_ref[sb] = x_ref[:, sb * LANES:(sb + 1) * LANES].astype(F32)
    o_ref[...] = jnp.zeros_like(o_ref)
    ot_ref[...] = jnp.zeros_like(ot_ref)
    for sb in range(2):
        first = jnp.zeros((n, 2 * CMP_HIDDEN), F32)
        second = jnp.zeros((n, 2 * CMP_HIDDEN), F32)
        for l in range(CMP_STRIDE):
            xl = xf_ref[sb, pl.ds(l, n, stride=CMP_STRIDE), :]
            first = first + jnp.dot((xl + pe_ref[sb, l:l + 1, :]).astype(BF16), w1_ref[sb, l],
                                    preferred_element_type=F32)
            second = second + jnp.dot((xl + pe_ref[sb, CMP_STRIDE + l:CMP_STRIDE + l + 1, :]).astype(BF16),
                                      w1_ref[sb, CMP_STRIDE + l], preferred_element_type=F32)
        h = _silu(first + pltpu.roll(second, n - 1, 0))
        o = jnp.dot(h.astype(BF16), w2_ref[sb], preferred_element_type=F32)
        for e in range(2):
            oe = o[:, e * LANES:(e + 1) * LANES]
            o_ref[0, 2 * sb + e, 0:n, :] = oe.astype(o_ref.dtype)
            ot_ref[0, 2 * sb + e, :, 0:n] = oe.T.astype(ot_ref.dtype)


def _blockdiag2(a):
    z = jnp.zeros_like(a)
    return jnp.concatenate([jnp.concatenate([a, z], axis=-1), jnp.concatenate([z, a], axis=-1)], axis=-2)


def _compress(p, cmp_w1, cmp_w2, cmp_pe, b, s):
    nslot = 2 * NSA_KV_HEADS
    w1 = _blockdiag2(cmp_w1.astype(BF16).reshape(2, CMP_BLOCK, HEAD_DIM, CMP_HIDDEN))
    w2 = _blockdiag2(jnp.concatenate([cmp_w2, cmp_w2], axis=-1).astype(BF16))
    pe = jnp.concatenate([cmp_pe, cmp_pe], axis=-1).astype(F32)
    kw = 2 * NSA_KV_W
    return pl.pallas_call(
        _compress_kernel,
        grid=(b,),
        in_specs=[
            pl.BlockSpec((s, kw), lambda bi: (bi, OFF_CMP_KV // kw)),
            pl.BlockSpec(w1.shape, lambda bi: (0, 0, 0, 0)),
            pl.BlockSpec(w2.shape, lambda bi: (0, 0, 0)),
            pl.BlockSpec(pe.shape, lambda bi: (0, 0, 0)),
        ],
        out_specs=[pl.BlockSpec((1, nslot, N_CMP_PAD, LANES), lambda bi: (bi, 0, 0, 0)),
                   pl.BlockSpec((1, nslot, LANES, N_CMP_PAD), lambda bi: (bi, 0, 0, 0))],
        out_shape=[jax.ShapeDtypeStruct((b, nslot, N_CMP_PAD, LANES), BF16),
                   jax.ShapeDtypeStruct((b, nslot, LANES, N_CMP_PAD), BF16)],
        scratch_shapes=[pltpu.VMEM((2, s, LANES), F32)],
        compiler_params=_params(("parallel",)),
        name="nsa_compress",
    )(p, w1, w2, pe)


def _cmp_select_kernel(q_ref, kv_ref, vt_ref, gl_ref, gate_ref, fc_ref, ovt_ref, o_ref, ns_ref):
    t = T_ATT
    ncp = kv_ref.shape[2]
    qi = pl.program_id(1)
    t0 = qi * t
    lo = lax.broadcasted_iota(jnp.int32, (t, LANES), 1) < HEAD_DIM
    zero = jnp.zeros((t, LANES), BF16)
    start = pl.multiple_of(ncp - qi * (t // CMP_STRIDE), CMP_STRIDE)
    gl = gl_ref[...]

    nsel = SEL_BLOCK
    jrow = lax.broadcasted_iota(jnp.int32, (nsel, t), 0)
    cur = (t0 + lax.broadcasted_iota(jnp.int32, (nsel, t), 1)) // SEL_BLOCK
    forced = (jrow == 0) | (jrow == cur) | (jrow == cur - 1)
    past = jrow <= cur

    for g in range(NSA_KV_HEADS):
        kc = kv_ref[0, g]
        vct = vt_ref[0, NSA_KV_HEADS + g]
        psum = jnp.zeros((ncp, t), F32)
        outs = []

        def scores(r):
            h = g * NSA_REP + r
            qp = q_ref[:, (h // 2) * LANES:(h // 2 + 1) * LANES]
            qe = jnp.where(lo, qp, zero) if r % 2 == 0 else jnp.where(lo, zero, qp)
            return lax.dot_general(kc, qe, (((1,), (1,)), ((), ())), preferred_element_type=F32)

        queue = [scores(0)]
        for r in range(NSA_REP):
            h = g * NSA_REP + r
            if r + 1 < NSA_REP:
                queue.append(scores(r + 1))
            s = fc_ref[h, pl.ds(start, ncp), :] + queue.pop(0)
            m = jnp.maximum(jnp.max(s, axis=0, keepdims=True), 0.1 * NEG_INF)
            e = jnp.exp2(s - m)
            pr = e * (1.0 / jnp.maximum(jnp.sum(e, axis=0, keepdims=True), TINY))
            psum = psum + pr
            oc = jnp.dot(vct, pr.astype(BF16), preferred_element_type=F32)
            outs.append(oc.T * jax.nn.sigmoid(gl[:, h:h + 1]))
        for hp in range(NSA_REP // 2):
            ls = slice((g * NSA_REP // 2 + hp) * LANES, (g * NSA_REP // 2 + hp + 1) * LANES)
            o = jnp.where(lo, outs[2 * hp], outs[2 * hp + 1])
            o_ref[:, ls] = (o * _silu(gate_ref[:, ls].astype(F32))).astype(o_ref.dtype)

        imp_t = jnp.zeros((nsel, t), F32)
        rest = psum
        for _ in range(N_SPLIT):
            piece = rest.astype(BF16)
            rest = rest - piece.astype(F32)
            imp_t = imp_t + jnp.dot(ovt_ref[...], piece, preferred_element_type=F32)
        imp_t = jnp.where(past, jnp.where(forced, SEL_FORCE, imp_t), -SEL_FORCE)
        sub = 8
        sub_row = lax.broadcasted_iota(jnp.int32, (sub, t), 0)
        rows = [imp_t[k * sub:(k + 1) * sub] for k in range(nsel // sub)]
        rank = [jnp.zeros((sub, t), F32) for _ in rows]
        for i in range(nsel):
            bi = imp_t[i:i + 1, :]
            for k, x in enumerate(rows):
                if k * sub > i:
                    ahead = jnp.where(bi >= x, 1.0, 0.0)
                elif k * sub + sub - 1 <= i:
                    ahead = jnp.where(bi > x, 1.0, 0.0)
                else:
                    ahead = jnp.where(sub_row > i - k * sub, jnp.where(bi >= x, 1.0, 0.0),
                                      jnp.where(bi > x, 1.0, 0.0))
                rank[k] = rank[k] + ahead
        rank = jnp.concatenate(rank, axis=0)
        notsel = jnp.where((rank < float(SEL_TOPK)) & past, 0.0, 1.0)
        ns2 = jnp.concatenate([notsel, notsel], axis=0).T
        ns_ref[:, g * LANES:(g + 1) * LANES] = ns2.astype(ns_ref.dtype)


def _cmp_select(p, ps, kvc, kvc_t, fc, overlap_t, b, s):
    t = T_ATT
    nq = s // t
    w = GROUP_W
    ncp = kvc.shape[2]
    return pl.pallas_call(
        _cmp_select_kernel,
        grid=(b, nq),
        in_specs=[
            pl.BlockSpec((t, w), lambda bi, qi: (bi * nq + qi, OFF_NSA_Q // w)),
            pl.BlockSpec((1, 2 * NSA_KV_HEADS, ncp, LANES), lambda bi, qi: (bi, 0, 0, 0)),
            pl.BlockSpec((1, 2 * NSA_KV_HEADS, LANES, ncp), lambda bi, qi: (bi, 0, 0, 0)),
            pl.BlockSpec((t, LANES), lambda bi, qi: (bi * nq + qi, 2)),
            pl.BlockSpec((t, w), lambda bi, qi: (bi * nq + qi, OFF_NSA_G // w)),
            pl.BlockSpec((NSA_HEADS, 2 * ncp, t), lambda bi, qi: (0, 0, 0)),
            pl.BlockSpec((SEL_BLOCK, ncp), lambda bi, qi: (0, 0)),
        ],
        out_specs=[
            pl.BlockSpec((t, w), lambda bi, qi: (bi * nq + qi, 0)),
            pl.BlockSpec((t, NSA_KV_HEADS * LANES), lambda bi, qi: (bi * nq + qi, 0)),
        ],
        out_shape=[jax.ShapeDtypeStruct((b * s, w), BF16),
                   jax.ShapeDtypeStruct((b * s, NSA_KV_HEADS * LANES), BF16)],
        compiler_params=_params(("parallel", "arbitrary")),
        name="nsa_cmp_select",
    )(p, kvc, kvc_t, ps, p, fc, overlap_t)


def _sel_attn_kernel(tab_ref, q_ref, ns_ref, k_ref, v_ref, gl_ref, gate_ref, bs_ref, o_ref,
                     qa_ref, kk_ref, vt_ref, m_ref, acc_ref, pre_ref, *, t, tk, ahead):
    qi = pl.program_id(1)
    lo_q, hi_q = _half_masks(t)
    lo = lax.broadcasted_iota(jnp.int32, (t, LANES), 1) < HEAD_DIM
    n_near = t // tk + 1

    @pl.when(qi == 0)
    def _():
        lane = lax.broadcasted_iota(jnp.int32, (tk, LANES), 1)
        krow = lax.broadcasted_iota(jnp.int32, (tk, LANES), 0)
        lo_f = jnp.where(lane < HEAD_DIM, 1.0, 0.0)
        hi_f = 1.0 - lo_f
        lo_k, hi_k = lo_f.astype(BF16), hi_f.astype(BF16)

        def merge(j, c):
            ks = pl.multiple_of(j * tk, tk)
            rows = pl.ds(ks, tk)
            hot = jnp.where((lane % HEAD_DIM) == (ks + krow) // SEL_BLOCK, NEG_INF, 0.0)
            for g in range(NSA_KV_HEADS):
                gls = slice(g * LANES, (g + 1) * LANES)
                kd, vd = k_ref[rows, gls], v_ref[rows, gls]
                kk_ref[2 * g, rows, :] = kd * lo_k + (hot * hi_f).astype(BF16)
                kk_ref[2 * g + 1, rows, :] = (hot * lo_f).astype(BF16) + kd * hi_k
                _store_vt(vt_ref, 2 * g, rows, vd)
            return c

        lax.fori_loop(0, k_ref.shape[0] // tk, merge, 0)

    for h in range(NSA_HEADS):
        g = h // NSA_REP
        qp = q_ref[:, (h // 2) * LANES:(h // 2 + 1) * LANES]
        ns = ns_ref[:, g * LANES:(g + 1) * LANES]
        qa = qp * lo_q + ns * hi_q if h % 2 == 0 else ns * lo_q + qp * hi_q
        qa_ref[h] = qa.astype(F32).T.astype(BF16)
    m_ref[...] = jnp.full(m_ref.shape, NEG_INF, F32)
    acc_ref[...] = jnp.zeros_like(acc_ref)

    kv = lambda h: 2 * (h // NSA_REP) + h % 2

    def rows_of(j):
        return pl.ds(pl.multiple_of(j * tk, tk), tk)

    def scores(j, h):
        return jnp.dot(kk_ref[kv(h), rows_of(j), :], qa_ref[h], preferred_element_type=F32)

    def tile(j, near, j_next):
        rows = rows_of(j)
        queue = [pre_ref[i] for i in range(ahead)]
        for h in range(NSA_HEADS):
            if h + ahead < NSA_HEADS:
                queue.append(scores(j, h + ahead))
            elif j_next is not None:
                pre_ref[h + ahead - NSA_HEADS] = scores(j_next, h + ahead - NSA_HEADS)
            s = queue.pop(0)
            m_old = m_ref[h]
            if near is None:
                far = tab_ref[(REL_BUCKETS - 1) * NSA_HEADS + h] * LOG2E
                m_new = jnp.maximum(m_old, jnp.max(s, axis=0, keepdims=True) + far)
                p = jnp.exp2(s - (m_new - far)).astype(BF16)
            else:
                s = bs_ref[h, near * tk:(near + 1) * tk, :] + s
                m_new = jnp.maximum(m_old, jnp.max(s, axis=0, keepdims=True))
                p = jnp.exp2(s - m_new).astype(BF16)
            m_ref[h] = m_new
            vr = _vt_rows(h)
            acc_ref[h, vr] = jnp.exp2(m_old - m_new) * acc_ref[h, vr] + jnp.dot(vt_ref[kv(h), vr, rows], p,
                                                                                preferred_element_type=F32)

    jd = (qi * t) // tk
    for i in range(ahead):
        pre_ref[i] = scores(0, i)

    def body(j, c):
        tile(j, None, j + 1)
        return c

    lax.fori_loop(0, jnp.maximum(jd - 1, 0), body, 0)

    @pl.when(qi > 0)
    def _():
        tile(jd - 1, 0, jd)

    for dj in range(n_near - 1):
        tile(jd + dj, 1 + dj, jd + dj + 1 if dj + 2 < n_near else None)

    gl = gl_ref[...]
    for hp in range(NSA_HEADS // 2):
        ls = slice(hp * LANES, (hp + 1) * LANES)
        a0, a1 = acc_ref[2 * hp].T, acc_ref[2 * hp + 1].T
        c0 = NSA_HEADS + 2 * hp
        o = jnp.where(lo, a0 / jnp.maximum(a0[:, HEAD_DIM:HEAD_DIM + 1], TINY) * jax.nn.sigmoid(gl[:, c0:c0 + 1]),
                      a1 / jnp.maximum(a1[:, HEAD_DIM - 1:HEAD_DIM], TINY) * jax.nn.sigmoid(gl[:, c0 + 1:c0 + 2]))
        o_ref[:, ls] = (o * _silu(gate_ref[:, ls].astype(F32))).astype(o_ref.dtype)


def _sel_attn(tab_flat, p, ps, notsel, bsel, b, s, t=512, tk=256, ahead=2):
    nq = s // t
    w = GROUP_W
    kw = NSA_KV_HEADS * LANES
    n_near = t // tk + 1
    return pl.pallas_call(
        functools.partial(_sel_attn_kernel, t=t, tk=tk, ahead=ahead),
        grid=(b, nq),
        in_specs=[
            pl.BlockSpec(memory_space=pltpu.SMEM),
            pl.BlockSpec((t, w), lambda bi, qi: (bi * nq + qi, OFF_NSA_Q // w)),
            pl.BlockSpec((t, kw), lambda bi, qi: (bi * nq + qi, 0)),
            pl.BlockSpec((s, kw), lambda bi, qi: (bi, OFF_SEL_K // kw)),
            pl.BlockSpec((s, kw), lambda bi, qi: (bi, OFF_SEL_V // kw)),
            pl.BlockSpec((t, LANES), lambda bi, qi: (bi * nq + qi, 2)),
            pl.BlockSpec((t, w), lambda bi, qi: (bi * nq + qi, OFF_NSA_G // w)),
            pl.BlockSpec((NSA_HEADS, n_near * tk, t), lambda bi, qi: (0, 0, 0), pipeline_mode=pl.Buffered(1)),
        ],
        out_specs=pl.BlockSpec((t, w), lambda bi, qi: (bi * nq + qi, 0)),
        out_shape=jax.ShapeDtypeStruct((b * s, w), BF16),
        scratch_shapes=[
            pltpu.VMEM((NSA_HEADS, LANES, t), BF16),
            pltpu.VMEM((2 * NSA_KV_HEADS, s, LANES), BF16),
            pltpu.VMEM((2 * NSA_KV_HEADS, LANES, s), BF16),
            pltpu.VMEM((NSA_HEADS, 1, t), F32),
            pltpu.VMEM((NSA_HEADS, LANES, t), F32),
            pltpu.VMEM((ahead, tk, t), F32),
        ],
        compiler_params=_params(("parallel", "arbitrary")),
        name="nsa_sel_attn",
    )(tab_flat, p, notsel, p, p, ps, p, bsel)


def _win_attn_kernel(q_ref, k_ref, v_ref, gl_ref, gate_ref, bw_ref, o_ref, qe_ref, vt_ref, m_ref, acc_ref,
                     pre_ref, *, t, tk, ahead):
    qi = pl.program_id(1)
    lo_q, hi_q = _half_masks(t)
    lo = lax.broadcasted_iota(jnp.int32, (t, LANES), 1) < HEAD_DIM
    n_before = WINDOW // tk
    n_tiles = n_before + t // tk

    @pl.when(qi == 0)
    def _():
        def merge(j, c):
            rows = pl.ds(pl.multiple_of(j * tk, tk), tk)
            for g in range(NSA_KV_HEADS):
                vd = v_ref[rows, g * LANES:(g + 1) * LANES]
                _store_vt(vt_ref, 2 * g, rows, vd)
            return c

        lax.fori_loop(0, v_ref.shape[0] // tk, merge, 0)

    for h in range(NSA_HEADS):
        qp = q_ref[:, (h // 2) * LANES:(h // 2 + 1) * LANES]
        qe_ref[h] = qp * lo_q if h % 2 == 0 else qp * hi_q
    m_ref[...] = jnp.full(m_ref.shape, NEG_INF, F32)
    acc_ref[...] = jnp.zeros_like(acc_ref)

    def rows_of(w):
        return pl.ds(pl.multiple_of((qi * (t // tk) - n_before + w) * tk, tk), tk)

    def scores(w, h):
        g = h // NSA_REP
        return lax.dot_general(k_ref[rows_of(w), g * LANES:(g + 1) * LANES], qe_ref[h], (((1,), (1,)), ((), ())),
                               preferred_element_type=F32)

    def tile(w):
        rows = rows_of(w)
        queue = [pre_ref[i] for i in range(ahead)]
        for h in range(NSA_HEADS):
            if h + ahead < NSA_HEADS:
                queue.append(scores(w, h + ahead))
            elif w + 1 < n_tiles:
                pre_ref[h + ahead - NSA_HEADS] = scores(w + 1, h + ahead - NSA_HEADS)
            s = bw_ref[h, w * tk:(w + 1) * tk, :] + queue.pop(0)
            m_old = m_ref[h]
            m_new = jnp.maximum(m_old, jnp.max(s, axis=0, keepdims=True))
            m_ref[h] = m_new
            p = jnp.exp2(s - m_new).astype(BF16)
            vr = _vt_rows(h)
            acc_ref[h, vr] = jnp.exp2(m_old - m_new) * acc_ref[h, vr] + jnp.dot(
                vt_ref[2 * (h // NSA_REP) + h % 2, vr, rows], p, preferred_element_type=F32)

    first = jnp.where(qi > 0, 0, n_before)
    for i in range(ahead):
        pre_ref[i] = scores(first, i)

    @pl.when(qi > 0)
    def _():
        for w in range(n_before):
            tile(w)

    for w in range(n_before, n_tiles):
        tile(w)

    gl = gl_ref[...]
    for hp in range(NSA_HEADS // 2):
        ls = slice(hp * LANES, (hp + 1) * LANES)
        a0, a1 = acc_ref[2 * hp].T, acc_ref[2 * hp + 1].T
        c0 = 2 * NSA_HEADS + 2 * hp
        o = jnp.where(lo, a0 / jnp.maximum(a0[:, HEAD_DIM:HEAD_DIM + 1], TINY) * jax.nn.sigmoid(gl[:, c0:c0 + 1]),
                      a1 / jnp.maximum(a1[:, HEAD_DIM - 1:HEAD_DIM], TINY) * jax.nn.sigmoid(gl[:, c0 + 1:c0 + 2]))
        o_ref[:, ls] = (o * _silu(gate_ref[:, ls].astype(F32))).astype(o_ref.dtype)


def _win_attn(p, ps, bwin, b, s, t=WIN_T, tk=WIN_TK, ahead=2):
    assert t % WINDOW == 0 and WINDOW % tk == 0
    nq = s // t
    w = GROUP_W
    kw = NSA_KV_HEADS * LANES
    return pl.pallas_call(
        functools.partial(_win_attn_kernel, t=t, tk=tk, ahead=ahead),
        grid=(b, nq),
        in_specs=[
            pl.BlockSpec((t, w), lambda bi, qi: (bi * nq + qi, OFF_NSA_Q // w)),
            pl.BlockSpec((s, kw), lambda bi, qi: (bi, OFF_WIN_K // kw)),
            pl.BlockSpec((s, kw), lambda bi, qi: (bi, OFF_WIN_V // kw)),
            pl.BlockSpec((t, LANES), lambda bi, qi: (bi * nq + qi, 2)),
            pl.BlockSpec((t, w), lambda bi, qi: (bi * nq + qi, OFF_NSA_G // w)),
            pl.BlockSpec((NSA_HEADS, WINDOW + t, t), lambda bi, qi: (0, 0, 0), pipeline_mode=pl.Buffered(1)),
        ],
        out_specs=pl.BlockSpec((t, w), lambda bi, qi: (bi * nq + qi, 0)),
        out_shape=jax.ShapeDtypeStruct((b * s, w), BF16),
        scratch_shapes=[
            pltpu.VMEM((NSA_HEADS, t, LANES), BF16),
            pltpu.VMEM((2 * NSA_KV_HEADS, LANES, s), BF16),
            pltpu.VMEM((NSA_HEADS, 1, t), F32),
            pltpu.VMEM((NSA_HEADS, LANES, t), F32),
            pltpu.VMEM((ahead, tk, t), F32),
        ],
        compiler_params=_params(("parallel", "arbitrary")),
        name="nsa_win_attn",
    )(p, p, p, ps, p, bwin)


def _mem_attn_kernel(q_ref, gate_ref, kv_ref, o_ref):
    scale = MEM_HEAD_DIM ** -0.5
    for h in range(MEM_HEADS):
        ls = slice(h * LANES, (h + 1) * LANES)
        k = kv_ref[:, ls]
        v = kv_ref[:, GROUP_W + h * LANES:GROUP_W + (h + 1) * LANES]
        s = lax.dot_general(q_ref[:, ls], k, (((1,), (1,)), ((), ())), preferred_element_type=F32) * scale
        m = jnp.max(s, axis=1, keepdims=True)
        e = jnp.exp(s - m)
        l = jnp.sum(e, axis=1, keepdims=True)
        o = jnp.dot(e.astype(BF16), v, preferred_element_type=F32) / l
        o_ref[:, ls] = (o * _silu(gate_ref[:, ls].astype(F32))).astype(o_ref.dtype)


def _mem_attn(p, mem_kv, b, s, t=512):
    nq = s // t
    w = GROUP_W
    m = mem_kv.shape[0] // b
    return pl.pallas_call(
        _mem_attn_kernel,
        grid=(b, nq),
        in_specs=[
            pl.BlockSpec((t, w), lambda bi, qi: (bi * nq + qi, OFF_MEM_Q // w)),
            pl.BlockSpec((t, w), lambda bi, qi: (bi * nq + qi, OFF_MEM_G // w)),
            pl.BlockSpec((m, 2 * w), lambda bi, qi: (bi, 0)),
        ],
        out_specs=pl.BlockSpec((t, w), lambda bi, qi: (bi * nq + qi, 0)),
        out_shape=jax.ShapeDtypeStruct((b * s, w), BF16),
        compiler_params=_params(("parallel", "arbitrary")),
        name="mem_attn",
    )(p, p, mem_kv)


def _out_proj_kernel(x_ref, of_ref, os_ref, oc_ref, osel_ref, ow_ref, om_ref, w_ref, g_ref, o_ref, *, final):
    w = GROUP_W
    nsa = (oc_ref[...].astype(F32) + osel_ref[...].astype(F32) + ow_ref[...].astype(F32)).astype(BF16)
    acc = x_ref[...]
    for i, part in enumerate((of_ref[...], os_ref[...], nsa, om_ref[...])):
        acc = acc + jnp.dot(part, w_ref[i * w:(i + 1) * w, :], preferred_element_type=F32)
    if final:
        ms = jnp.mean(acc * acc, axis=-1, keepdims=True)
        acc = (acc * lax.rsqrt(ms + EPS)) * g_ref[...]
    o_ref[...] = acc


def _out_proj(x2d, parts, w_out, g, final, tm=512):
    n, d = x2d.shape
    w = GROUP_W
    part_spec = pl.BlockSpec((tm, w), lambda i: (i, 0))
    return pl.pallas_call(
        functools.partial(_out_proj_kernel, final=final),
        grid=(n // tm,),
        in_specs=[pl.BlockSpec((tm, d), lambda i: (i, 0))] + [part_spec] * 6 + [
            pl.BlockSpec((4 * w, d), lambda i: (0, 0)),
            pl.BlockSpec((1, d), lambda i: (0, 0)),
        ],
        out_specs=pl.BlockSpec((tm, d), lambda i: (i, 0)),
        out_shape=jax.ShapeDtypeStruct((n, d), F32),
        compiler_params=_params(("parallel",)),
        name="out_proj",
    )(x2d, *parts, w_out, g.reshape(1, d))


def _pack_in_proj(w_in_l):
    fox, ssm, nsa, mem = 0, FOX_COLS, FOX_COLS + SSM_COLS, FOX_COLS + SSM_COLS + NSA_COLS
    w = GROUP_W
    q_scale = HEAD_DIM ** -0.5 * LOG2E
    cols = lambda a, n: w_in_l[:, a:a + n]
    kv = lambda slot: nsa + w + slot * NSA_KV_W
    dup = lambda base: [cols(base, HEAD_DIM)] * 2 + [cols(base + HEAD_DIM, HEAD_DIM)] * 2
    main = ([cols(fox, w) * q_scale, cols(fox + w, 3 * w)]
            + [cols(ssm + w, SSM_CONV_DIM), cols(ssm, w)]
            + [cols(nsa, w) * q_scale, cols(nsa + w + 6 * NSA_KV_W + 3 * NSA_HEADS, w)]
            + [cols(mem, 2 * w)]
            + dup(kv(2)) + dup(kv(3)) + dup(kv(4)) + dup(kv(5))
            + [cols(kv(0), 2 * NSA_KV_W)])
    w_main = jnp.concatenate(main, axis=1).astype(BF16)
    assert w_main.shape[1] == P_COLS
    small = []
    for src, n, rep in ((fox + 4 * w, FOX_HEADS, N_SPLIT), (ssm + w + SSM_CONV_DIM, SSM_HEADS, 1),
                        (nsa + w + 6 * NSA_KV_W, 3 * NSA_HEADS, 1)):
        small += [cols(src, n)] * rep + [jnp.zeros((w_in_l.shape[0], LANES - rep * n), w_in_l.dtype)]
    return w_main, jnp.concatenate(small, axis=1).astype(BF16)


def _pad_lanes(v):
    return jnp.pad(v.astype(F32), (0, LANES - v.shape[0])).reshape(1, LANES)


def _trunk(x, mem, norm_g, w_in, fox_f_bias, ssm_conv_w, ssm_conv_b, ssm_dt_bias, ssm_a_log, ssm_d,
           ssm_norm_g, nsa_cmp_pe, nsa_cmp_w1, nsa_cmp_w2, rel_bias_table, mem_norm_g, w_mem_kv, w_out,
           final_norm_g):
    b, s, d = x.shape
    depth = w_in.shape[0]
    n = b * s
    m_tok = mem.shape[1]
    n_cmp = (s - CMP_BLOCK) // CMP_STRIDE + 1
    n_rows = s // CMP_STRIDE
    assert s % 512 == 0 and s // SEL_BLOCK <= HEAD_DIM and n_rows <= N_CMP_PAD and d == D_MODEL

    tab_flat = rel_bias_table.astype(F32).reshape(-1)
    unbounded = 1 << 30
    bwin = _t5_table(tab_flat, WINDOW + WIN_T, WIN_T, 1, WINDOW, WINDOW)
    bsel = _t5_table(tab_flat, (SEL_T // SEL_TK + 1) * SEL_TK, SEL_T, 1, SEL_TK, unbounded)
    fcmp = _t5_table(tab_flat, 2 * N_CMP_PAD, T_ATT, CMP_STRIDE, CMP_STRIDE * N_CMP_PAD - (CMP_BLOCK - 1), unbounded)

    cs = np.arange(N_CMP_PAD)[None, :] * CMP_STRIDE
    js = np.arange(SEL_BLOCK)[:, None] * SEL_BLOCK
    overlap_t = ((cs < js + SEL_BLOCK) & (cs + CMP_BLOCK > js) & (np.arange(N_CMP_PAD)[None, :] < n_cmp)
                 & (np.arange(SEL_BLOCK)[:, None] < s // SEL_BLOCK)).astype(np.float32)
    overlap_t = jnp.asarray(overlap_t, BF16)

    x2d = x.reshape(n, d)
    mem2d = mem.reshape(b * m_tok, d)
    half = CMP_STRIDE * HEAD_DIM
    for l in range(depth):
        w_main, w_small = _pack_in_proj(w_in[l])
        p, ps = _norm_proj(x2d, norm_g[l], w_main, w_small)

        qaug, kaug = _fox_cumsum(ps, _pad_lanes(jnp.tile(fox_f_bias[l], N_SPLIT)), b, s)
        o_fox = _fox_attn(p, qaug, kaug, b, s)

        o_ssd = _ssd(p, ps, ssm_conv_w[l].astype(F32), ssm_conv_b[l].reshape(1, -1).astype(F32),
                     _pad_lanes(ssm_dt_bias[l]), _pad_lanes(ssm_a_log[l]),
                     jnp.repeat(ssm_d[l].astype(F32), HEAD_DIM).reshape(1, GROUP_W),
                     ssm_norm_g[l].reshape(1, GROUP_W).astype(F32), b, s)

        kv_cmp, kv_cmp_t = _compress(p, nsa_cmp_w1[l], nsa_cmp_w2[l], nsa_cmp_pe[l], b, s)
        o_cmp, notsel = _cmp_select(p, ps, kv_cmp, kv_cmp_t, fcmp, overlap_t, b, s)
        o_sel = _sel_attn(tab_flat, p, ps, notsel, bsel, b, s)
        o_win = _win_attn(p, ps, bwin, b, s)

        w_kv = w_mem_kv[l].astype(BF16)
        mem_kv, _ = _norm_proj(mem2d, mem_norm_g[l], w_kv, w_kv[:, :LANES], tm=min(512, b * m_tok))
        o_mem = _mem_attn(p, mem_kv, b, s)

        x2d = _out_proj(x2d, (o_fox, o_ssd, o_cmp, o_sel, o_win, o_mem), w_out[l].astype(BF16),
                        final_norm_g, final=(l == depth - 1))
    return x2d.reshape(b, s, d)


def kernel(x, mem, norm_g, w_in, fox_f_bias, ssm_conv_w, ssm_conv_b, ssm_dt_bias, ssm_a_log, ssm_d, ssm_norm_g,
           nsa_cmp_pe, nsa_cmp_w1, nsa_cmp_w2, rel_bias_table, mem_norm_g, w_mem_kv, w_out, final_norm_g):
    return _trunk(x, mem, norm_g, w_in, fox_f_bias, ssm_conv_w, ssm_conv_b, ssm_dt_bias, ssm_a_log, ssm_d,
                  ssm_norm_g, nsa_cmp_pe, nsa_cmp_w1, nsa_cmp_w2, rel_bias_table, mem_norm_g, w_mem_kv, w_out,
                  final_norm_g)
```

```python
import functools
import math

import numpy as np
import jax
import jax.numpy as jnp
from jax import lax
from jax.experimental import pallas as pl
from jax.experimental.pallas import tpu as pltpu

F32 = jnp.float32
BF16 = jnp.bfloat16

D_MODEL = 1024
GROUP_W = 512
HEAD_DIM = 64
EPS = 1e-6
NEG_INF = -1e30
TINY = 1e-30
LOG2E = math.log2(math.e)

FOX_HEADS = 8
SSM_HEADS = 8
SSM_STATE = 128
SSM_GROUPS = 2
SSM_CONV = 4
SSM_CHUNK = 128
SSM_CONV_DIM = GROUP_W + 2 * SSM_GROUPS * SSM_STATE

NSA_HEADS = 8
NSA_KV_HEADS = 2
NSA_REP = NSA_HEADS // NSA_KV_HEADS
NSA_KV_W = NSA_KV_HEADS * HEAD_DIM
CMP_BLOCK = 32
CMP_STRIDE = 16
CMP_HIDDEN = 2 * HEAD_DIM
SEL_BLOCK = 64
SEL_TOPK = 16
WINDOW = 512
SEL_FORCE = 1e9

MEM_HEADS = 4
MEM_HEAD_DIM = GROUP_W // MEM_HEADS
REL_BUCKETS = 32
REL_MAX_DIST = 128

FOX_COLS = 4 * GROUP_W + FOX_HEADS
SSM_COLS = GROUP_W + SSM_CONV_DIM + SSM_HEADS
NSA_COLS = 2 * GROUP_W + 6 * NSA_KV_W + 3 * NSA_HEADS
MEM_COLS = 2 * GROUP_W

LANES = 128
VMEM_LIMIT = 56 * 1024 * 1024

OFF_FOX_Q, OFF_FOX_K, OFF_FOX_V, OFF_FOX_G = 0, 512, 1024, 1536
OFF_SSM_XBC, OFF_SSM_Z = 2048, 3072
OFF_NSA_Q, OFF_NSA_G = 3584, 4096
OFF_MEM_Q, OFF_MEM_G = 4608, 5120
OFF_SEL_K, OFF_SEL_V, OFF_WIN_K, OFF_WIN_V, OFF_CMP_KV = 5632, 5760, 5888, 6016, 6144
P_COLS = 6400
PS_COLS = LANES
PS_DT_LANE = 24
PS_GATE_LANE = 32

T_ATT = 256
SEL_T, SEL_TK = 512, 256
WIN_T, WIN_TK = 512, 256
N_CMP_PAD = 256


def _params(sem):
    return pltpu.CompilerParams(dimension_semantics=sem, vmem_limit_bytes=VMEM_LIMIT)


def _t5_bucket_np(dist):
    n = np.maximum(dist, 0)
    max_exact = REL_BUCKETS // 2
    nf = np.maximum(n, 1).astype(np.float32)
    large = max_exact + (np.log(nf / np.float32(max_exact)) / np.float32(math.log(REL_MAX_DIST / max_exact))
                         * np.float32(REL_BUCKETS - max_exact)).astype(np.int32)
    large = np.minimum(large, REL_BUCKETS - 1)
    return np.where(n < max_exact, n, large).astype(np.int32)


def _silu(x):
    h = 0.5 * x
    return h + h * jnp.tanh(h)


def _norm_proj_kernel(x_ref, g_ref, w_ref, ws_ref, p_ref, ps_ref, *, chunk):
    x = x_ref[...]
    ms = jnp.mean(x * x, axis=-1, keepdims=True)
    h = ((x * lax.rsqrt(ms + EPS)) * g_ref[...]).astype(BF16)
    ncol = p_ref.shape[1]
    for c0 in range(0, ncol, chunk):
        c1 = min(c0 + chunk, ncol)
        p_ref[:, c0:c1] = jnp.dot(h, w_ref[:, c0:c1], preferred_element_type=F32).astype(p_ref.dtype)
    ps_ref[...] = jnp.dot(h, ws_ref[...], preferred_element_type=F32)


def _norm_proj(x2d, g, w_main, w_small, tm=512):
    n, d = x2d.shape
    pc, sc = w_main.shape[1], w_small.shape[1]
    return pl.pallas_call(
        functools.partial(_norm_proj_kernel, chunk=512),
        grid=(n // tm,),
        in_specs=[
            pl.BlockSpec((tm, d), lambda i: (i, 0)),
            pl.BlockSpec((1, d), lambda i: (0, 0)),
            pl.BlockSpec((d, pc), lambda i: (0, 0)),
            pl.BlockSpec((d, sc), lambda i: (0, 0)),
        ],
        out_specs=[
            pl.BlockSpec((tm, pc), lambda i: (i, 0)),
            pl.BlockSpec((tm, sc), lambda i: (i, 0)),
        ],
        out_shape=[jax.ShapeDtypeStruct((n, pc), BF16), jax.ShapeDtypeStruct((n, sc), F32)],
        compiler_params=_params(("parallel",)),
        name="norm_proj",
    )(x2d, g.reshape(1, d), w_main, w_small)


def _t5_table_kernel(tab_ref, bucket_ref, o_ref, *, stride, off, limit):
    h = pl.program_id(0)
    rows, cols = o_ref.shape[1], o_ref.shape[2]
    bucket = bucket_ref[...]
    near = jnp.zeros(bucket.shape, F32)
    for b in range(REL_BUCKETS):
        near = jnp.where(bucket == b, tab_ref[b * NSA_HEADS + h] * LOG2E, near)
    far = tab_ref[(REL_BUCKETS - 1) * NSA_HEADS + h] * LOG2E
    base = jnp.concatenate([near] * (rows // near.shape[0]), axis=0)
    rolled = pltpu.roll(base, 0, 1, stride=stride, stride_axis=0)
    d = (lax.broadcasted_iota(jnp.int32, (rows, cols), 1) - stride * lax.broadcasted_iota(jnp.int32, (rows, cols), 0)
         + off)
    o_ref[0] = jnp.where((d < 0) | (d >= limit), NEG_INF, jnp.where(d >= REL_MAX_DIST, far, rolled))


def _t5_table(tab_flat, rows, cols, stride, off, limit):
    assert cols >= 2 * REL_MAX_DIST and rows % 8 == 0
    k = (np.arange(cols) + off) % cols
    bucket = np.broadcast_to(np.where(k < REL_MAX_DIST, _t5_bucket_np(k), -1).astype(np.int32), (8, cols))
    return pl.pallas_call(
        functools.partial(_t5_table_kernel, stride=stride, off=off, limit=limit),
        grid=(NSA_HEADS,),
        in_specs=[
            pl.BlockSpec(memory_space=pltpu.SMEM),
            pl.BlockSpec((8, cols), lambda h: (0, 0)),
        ],
        out_specs=pl.BlockSpec((1, rows, cols), lambda h: (h, 0, 0)),
        out_shape=jax.ShapeDtypeStruct((NSA_HEADS, rows, cols), F32),
        compiler_params=_params(("arbitrary",)),
        name="t5_table",
    )(tab_flat, jnp.asarray(bucket))


def _tri_lower(n):
    r = lax.broadcasted_iota(jnp.int32, (n, n), 0)
    c = lax.broadcasted_iota(jnp.int32, (n, n), 1)
    return (r >= c).astype(F32)


N_SPLIT = 3


def _fox_aug_lane(h, i):
    return LANES * (h // 2) + (HEAD_DIM if h % 2 == 0 else 0) + i


def _fox_aug_consts():
    pq = np.zeros((LANES, GROUP_W), np.float32)
    pk = np.zeros((LANES, GROUP_W), np.float32)
    oq = np.zeros((1, GROUP_W), np.float32)
    ok = np.zeros((1, GROUP_W), np.float32)
    for h in range(FOX_HEADS):
        for i in range(N_SPLIT):
            pq[i * FOX_HEADS + h, _fox_aug_lane(h, i)] = 1.0
            pk[i * FOX_HEADS + h, _fox_aug_lane(h, N_SPLIT + i)] = -1.0
            oq[0, _fox_aug_lane(h, N_SPLIT + i)] = 1.0
            ok[0, _fox_aug_lane(h, i)] = 1.0
    return pq, pk, oq, ok


def _fox_cumsum_kernel(f_ref, b_ref, pq_ref, pk_ref, oq_ref, ok_ref, qa_ref, ka_ref, carry_ref, *, ts):
    @pl.when(pl.program_id(1) == 0)
    def _():
        carry_ref[...] = jnp.zeros_like(carry_ref)

    z = f_ref[...] + b_ref[...]
    logf = (jnp.minimum(z, 0.0) - jnp.log(1.0 + jnp.exp(-jnp.abs(z)))) * LOG2E
    tri = _tri_lower(LANES).astype(BF16)
    group = lax.broadcasted_iota(jnp.int32, (LANES, LANES), 1) // FOX_HEADS
    carry = carry_ref[...]
    for c in range(ts // LANES):
        rows = slice(c * LANES, (c + 1) * LANES)
        cs, rest = carry, logf[rows]
        for _ in range(N_SPLIT):
            piece = rest.astype(BF16)
            rest = rest - piece.astype(F32)
            cs = cs + jnp.dot(tri, piece, preferred_element_type=F32)
        carry = cs[LANES - 1:LANES, :]
        cat, rest = None, cs
        for i in range(N_SPLIT):
            piece = rest.astype(BF16).astype(F32)
            rest = rest - piece
            cat = piece if cat is None else jnp.where(group == i, piece, cat)
        cat = cat.astype(BF16)
        qa_ref[rows, :] = (oq_ref[...] + jnp.dot(cat, pq_ref[...], preferred_element_type=F32)).astype(BF16)
        ka_ref[rows, :] = (ok_ref[...] + jnp.dot(cat, pk_ref[...], preferred_element_type=F32)).astype(BF16)
    carry_ref[...] = carry


def _fox_cumsum(ps, f_bias_pad, b, s, ts=512):
    ns = s // ts
    pq, pk, oq, ok = _fox_aug_consts()
    const = lambda bi, si: (0, 0)
    return pl.pallas_call(
        functools.partial(_fox_cumsum_kernel, ts=ts),
        grid=(b, ns),
        in_specs=[
            pl.BlockSpec((ts, LANES), lambda bi, si: (bi * ns + si, 0)),
            pl.BlockSpec((1, LANES), const),
            pl.BlockSpec(pq.shape, const),
            pl.BlockSpec(pk.shape, const),
            pl.BlockSpec(oq.shape, const),
            pl.BlockSpec(ok.shape, const),
        ],
        out_specs=[
            pl.BlockSpec((ts, GROUP_W), lambda bi, si: (bi * ns + si, 0)),
            pl.BlockSpec((ts, GROUP_W), lambda bi, si: (bi * ns + si, 0)),
        ],
        out_shape=[jax.ShapeDtypeStruct((b * s, GROUP_W), BF16), jax.ShapeDtypeStruct((b * s, GROUP_W), BF16)],
        scratch_shapes=[pltpu.VMEM((1, LANES), F32)],
        compiler_params=_params(("parallel", "arbitrary")),
        name="fox_cumsum",
    )(ps, f_bias_pad, jnp.asarray(pq, BF16), jnp.asarray(pk, BF16), jnp.asarray(oq), jnp.asarray(ok))


ONES_ROWS = 16


def _vt_rows(h):
    return slice(0, HEAD_DIM + ONES_ROWS) if h % 2 == 0 else slice(HEAD_DIM - ONES_ROWS, LANES)


def _store_vt(vt_ref, i, rows, v_pair):
    vt = v_pair.astype(F32).T.astype(BF16)
    ones = jnp.ones((HEAD_DIM, v_pair.shape[0]), BF16)
    vt_ref[i, 0:HEAD_DIM, rows] = vt[0:HEAD_DIM]
    vt_ref[i, HEAD_DIM:LANES, rows] = ones
    vt_ref[i + 1, 0:HEAD_DIM, rows] = ones
    vt_ref[i + 1, HEAD_DIM:LANES, rows] = vt[HEAD_DIM:LANES]


def _store_vt_groups(vt_ref, rows, v_groups):
    vt = v_groups.astype(F32).T.astype(BF16)
    ones = jnp.ones((HEAD_DIM, v_groups.shape[0]), BF16)
    for g in range(NSA_KV_HEADS):
        vg = vt[g * HEAD_DIM:(g + 1) * HEAD_DIM]
        vt_ref[2 * g, 0:HEAD_DIM, rows] = vg
        vt_ref[2 * g, HEAD_DIM:LANES, rows] = ones
        vt_ref[2 * g + 1, 0:HEAD_DIM, rows] = ones
        vt_ref[2 * g + 1, HEAD_DIM:LANES, rows] = vg


def _swap_halves(x_b):
    return pltpu.roll(x_b.astype(F32), HEAD_DIM, 1).astype(BF16)


def _half_masks(rows):
    lo = jnp.where(lax.broadcasted_iota(jnp.int32, (rows, LANES), 1) < HEAD_DIM, 1.0, 0.0)
    return lo.astype(BF16), (1.0 - lo).astype(BF16)


def _fox_attn_kernel(q_ref, qa_ref, k_ref, ka_ref, v_ref, gate_ref, o_ref, qs_ref, kk_ref, vt_ref, m_ref, acc_ref,
                     pre_ref, *, t, tk, ahead):
    qi = pl.program_id(1)
    lo_q, hi_q = _half_masks(t)
    lo = lax.broadcasted_iota(jnp.int32, (t, LANES), 1) < HEAD_DIM
    cm = lax.broadcasted_iota(jnp.int32, (tk, t), 0) - lax.broadcasted_iota(jnp.int32, (tk, t), 1)

    @pl.when(qi == 0)
    def _():
        lo_k, hi_k = _half_masks(tk)

        def merge(j, c):
            rows = pl.ds(pl.multiple_of(j * tk, tk), tk)
            for hp in range(FOX_HEADS // 2):
                ls = slice(hp * LANES, (hp + 1) * LANES)
                kp, ka, vp = k_ref[rows, ls], ka_ref[rows, ls], v_ref[rows, ls]
                kk_ref[2 * hp, rows, :] = kp * lo_k + ka * hi_k
                kk_ref[2 * hp + 1, rows, :] = ka * lo_k + kp * hi_k
                _store_vt(vt_ref, 2 * hp, rows, vp)
            return c

        lax.fori_loop(0, k_ref.shape[0] // tk, merge, 0)

    for hp in range(FOX_HEADS // 2):
        ls = slice(hp * LANES, (hp + 1) * LANES)
        qp, qa = q_ref[:, ls], qa_ref[:, ls]
        qs_ref[2 * hp] = (qp * lo_q + qa * hi_q).astype(F32).T.astype(BF16)
        qs_ref[2 * hp + 1] = (qa * lo_q + qp * hi_q).astype(F32).T.astype(BF16)
    m_ref[...] = jnp.full(m_ref.shape, NEG_INF, F32)
    acc_ref[...] = jnp.zeros_like(acc_ref)

    def rows_of(j):
        return pl.ds(pl.multiple_of(j * tk, tk), tk)

    def scores(j, h):
        return jnp.dot(kk_ref[h, rows_of(j), :], qs_ref[h], preferred_element_type=F32)

    def tile(j, diag, j_next):
        queue = [pre_ref[i] for i in range(ahead)]
        for h in range(FOX_HEADS):
            if h + ahead < FOX_HEADS:
                queue.append(scores(j, h + ahead))
            elif j_next is not None:
                pre_ref[h + ahead - FOX_HEADS] = scores(j_next, h + ahead - FOX_HEADS)
            s = queue.pop(0)
            if diag:
                s = jnp.where(cm <= qi * t - j * tk, s, NEG_INF)
            m_old = m_ref[h]
            m_new = jnp.maximum(m_old, jnp.max(s, axis=0, keepdims=True))
            m_ref[h] = m_new
            p = jnp.exp2(s - m_new).astype(BF16)
            vr = _vt_rows(h)
            acc_ref[h, vr] = jnp.exp2(m_old - m_new) * acc_ref[h, vr] + jnp.dot(vt_ref[h, vr, rows_of(j)], p,
                                                                                preferred_element_type=F32)

    jd = (qi * t) // tk
    n_diag = max(t // tk, 1)
    for i in range(ahead):
        pre_ref[i] = scores(0, i)

    def body(j, c):
        tile(j, False, j + 1)
        return c

    lax.fori_loop(0, jd, body, 0)
    for dj in range(n_diag):
        tile(jd + dj, True, jd + dj + 1 if dj + 1 < n_diag else None)

    for hp in range(FOX_HEADS // 2):
        ls = slice(hp * LANES, (hp + 1) * LANES)
        a0, a1 = acc_ref[2 * hp].T, acc_ref[2 * hp + 1].T
        o = jnp.where(lo, a0 / jnp.maximum(a0[:, HEAD_DIM:HEAD_DIM + 1], TINY),
                      a1 / jnp.maximum(a1[:, HEAD_DIM - 1:HEAD_DIM], TINY))
        o_ref[:, ls] = (o * _silu(gate_ref[:, ls].astype(F32))).astype(o_ref.dtype)


def _fox_attn(p, qaug, kaug, b, s, t=512, tk=256, ahead=2):
    nq = s // t
    w = GROUP_W
    return pl.pallas_call(
        functools.partial(_fox_attn_kernel, t=t, tk=tk, ahead=ahead),
        grid=(b, nq),
        in_specs=[
            pl.BlockSpec((t, w), lambda bi, qi: (bi * nq + qi, OFF_FOX_Q // w)),
            pl.BlockSpec((t, w), lambda bi, qi: (bi * nq + qi, 0)),
            pl.BlockSpec((s, w), lambda bi, qi: (bi, OFF_FOX_K // w)),
            pl.BlockSpec((s, w), lambda bi, qi: (bi, 0)),
            pl.BlockSpec((s, w), lambda bi, qi: (bi, OFF_FOX_V // w)),
            pl.BlockSpec((t, w), lambda bi, qi: (bi * nq + qi, OFF_FOX_G // w)),
        ],
        out_specs=pl.BlockSpec((t, w), lambda bi, qi: (bi * nq + qi, 0)),
        out_shape=jax.ShapeDtypeStruct((b * s, w), BF16),
        scratch_shapes=[
            pltpu.VMEM((FOX_HEADS, LANES, t), BF16),
            pltpu.VMEM((FOX_HEADS, s, LANES), BF16),
            pltpu.VMEM((FOX_HEADS, LANES, s), BF16),
            pltpu.VMEM((FOX_HEADS, 1, t), F32),
            pltpu.VMEM((FOX_HEADS, LANES, t), F32),
            pltpu.VMEM((ahead, tk, t), F32),
        ],
        compiler_params=_params(("parallel", "arbitrary")),
        name="fox_attn",
    )(p, qaug, p, kaug, p, p)


def _dot_split(lhs_f32, rhs_b):
    out, rest = None, lhs_f32
    for _ in range(N_SPLIT):
        piece = rest.astype(BF16)
        rest = rest - piece.astype(F32)
        term = jnp.dot(piece, rhs_b, preferred_element_type=F32)
        out = term if out is None else out + term
    return out


def _ssd_kernel(z_ref, xbc_ref, dt_ref, cw_ref, cb_ref, dtb_ref, alog_ref, dsk_ref, ng_ref, o_ref,
                xpad_ref, xc_ref, state_ref, y_ref, *, nch):
    q = SSM_CHUNK
    rows_all = nch * q
    halo = 8

    @pl.when(pl.program_id(1) == 0)
    def _():
        xpad_ref[0:halo, :] = jnp.zeros((halo, SSM_CONV_DIM), F32)
        state_ref[...] = jnp.zeros_like(state_ref)

    xpad_ref[halo:halo + rows_all, :] = xbc_ref[...].astype(F32)
    y = cb_ref[...]
    for k in range(SSM_CONV):
        off = halo - (SSM_CONV - 1) + k
        y = y + cw_ref[k:k + 1, :] * xpad_ref[off:off + rows_all, :]
    xpad_ref[0:halo, :] = xpad_ref[rows_all:rows_all + halo, :]
    xc_ref[...] = _silu(y)

    x_dt = dt_ref[...] + dtb_ref[...]
    dt_all = jnp.maximum(x_dt, 0.0) + jnp.log(1.0 + jnp.exp(-jnp.abs(x_dt)))
    a_all = dt_all * (-jnp.exp(alog_ref[...]))
    tri_t = (lax.broadcasted_iota(jnp.int32, (q, q), 0) <= lax.broadcasted_iota(jnp.int32, (q, q), 1)).astype(BF16)
    er = lax.broadcasted_iota(jnp.int32, (LANES, GROUP_W), 0)
    ec = lax.broadcasted_iota(jnp.int32, (LANES, GROUP_W), 1)
    expand = jnp.where(ec // HEAD_DIM == er - PS_DT_LANE, 1.0, 0.0).astype(BF16)
    row = lax.broadcasted_iota(jnp.int32, (q, q), 0)
    col = lax.broadcasted_iota(jnp.int32, (q, q), 1)
    causal = row >= col
    lo = lax.broadcasted_iota(jnp.int32, (q, LANES), 1) < HEAD_DIM
    gw = GROUP_W // SSM_GROUPS
    hpg = SSM_HEADS // SSM_GROUPS

    for c in range(nch):
        rs = slice(c * q, (c + 1) * q)
        xs = xc_ref[rs, :GROUP_W]
        dt = dt_all[rs]
        acs_t = _dot_split(a_all[rs].T, tri_t)
        acs = acs_t.T
        stacked = jnp.concatenate([dt, jnp.exp(acs), jnp.exp(acs[q - 1:q, :] - acs)], axis=0)
        full = _dot_split(stacked, expand)
        dt_full, eacs_full, dec_full = full[0:q], full[q:2 * q], full[2 * q:3 * q]
        xdt = xs * dt_full
        xdt_b = xdt.astype(BF16)
        xdec_b = (xdt * dec_full).astype(BF16)

        for g in range(SSM_GROUPS):
            bm = xc_ref[rs, GROUP_W + g * SSM_STATE:GROUP_W + (g + 1) * SSM_STATE]
            cm = xc_ref[rs, GROUP_W + (SSM_GROUPS + g) * SSM_STATE:GROUP_W + (SSM_GROUPS + g + 1) * SSM_STATE]
            bm_b = bm.astype(BF16)
            cm_b = cm.astype(BF16)
            gs = slice(g * gw, (g + 1) * gw)
            cbg = lax.dot_general(cm_b, bm_b, (((1,), (1,)), ((), ())), preferred_element_type=F32)
            st = state_ref[:, gs]
            y_off = jnp.dot(cm_b, st.astype(BF16), preferred_element_type=F32) * eacs_full[:, gs]
            cst = jnp.dot(bm.T.astype(BF16), xdec_b[:, gs], preferred_element_type=F32)
            state_ref[:, gs] = st * eacs_full[q - 1:q, gs] + cst
            for hp in range(hpg // 2):
                ls = slice(g * gw + hp * LANES, g * gw + (hp + 1) * LANES)
                yd = []
                for e in range(2):
                    h = g * hpg + 2 * hp + e
                    hl = PS_DT_LANE + h
                    seg = jnp.exp(jnp.where(causal, acs[:, hl:hl + 1] - acs_t[hl:hl + 1, :], NEG_INF))
                    yd.append(jnp.dot((cbg * seg).astype(BF16), xdt_b[:, ls], preferred_element_type=F32))
                y_ref[:, ls] = jnp.where(lo, yd[0], yd[1]) + y_off[:, hp * LANES:(hp + 1) * LANES]

        yz = (y_ref[...] + xs * dsk_ref[...]) * _silu(z_ref[rs, :].astype(F32))
        for g in range(SSM_GROUPS):
            gs = slice(g * gw, (g + 1) * gw)
            blk = yz[:, gs]
            ms = jnp.mean(blk * blk, axis=-1, keepdims=True)
            o_ref[rs, gs] = ((blk * lax.rsqrt(ms + EPS)) * ng_ref[:, gs]).astype(o_ref.dtype)


def _ssd(p, ps, conv_w, conv_b, dt_bias_pad, a_log_pad, d_full, norm_g, b, s, nch=4):
    q = SSM_CHUNK
    rows = nch * q
    nc = s // rows
    row = lambda bi, ci: (bi * nc + ci)
    const = lambda bi, ci: (0, 0)
    return pl.pallas_call(
        functools.partial(_ssd_kernel, nch=nch),
        grid=(b, nc),
        in_specs=[
            pl.BlockSpec((rows, GROUP_W), lambda bi, ci: (row(bi, ci), OFF_SSM_Z // GROUP_W)),
            pl.BlockSpec((rows, SSM_CONV_DIM), lambda bi, ci: (row(bi, ci), OFF_SSM_XBC // SSM_CONV_DIM)),
            pl.BlockSpec((rows, LANES), lambda bi, ci: (row(bi, ci), 0)),
            pl.BlockSpec((SSM_CONV, SSM_CONV_DIM), const),
            pl.BlockSpec((1, SSM_CONV_DIM), const),
            pl.BlockSpec((1, LANES), const),
            pl.BlockSpec((1, LANES), const),
            pl.BlockSpec((1, GROUP_W), const),
            pl.BlockSpec((1, GROUP_W), const),
        ],
        out_specs=pl.BlockSpec((rows, GROUP_W), lambda bi, ci: (row(bi, ci), 0)),
        out_shape=jax.ShapeDtypeStruct((b * s, GROUP_W), BF16),
        scratch_shapes=[
            pltpu.VMEM((rows + 8, SSM_CONV_DIM), F32),
            pltpu.VMEM((rows, SSM_CONV_DIM), F32),
            pltpu.VMEM((SSM_STATE, GROUP_W), F32),
            pltpu.VMEM((q, GROUP_W), F32),
        ],
        compiler_params=_params(("parallel", "arbitrary")),
        name="ssd",
    )(p, p, ps, conv_w, conv_b, dt_bias_pad, a_log_pad, d_full, norm_g)


def _compress_kernel(x_ref, w1_ref, w2_ref, pe_ref, o_ref, ot_ref, xf_ref):
    s = x_ref.shape[0]
    n = s // CMP_STRIDE
    for sb in range(2):
        xf_ref[sb] = x_ref[:, sb * LANES:(sb + 1) * LANES].astype(F32)
    o_ref[...] = jnp.zeros_like(o_ref)
    ot_ref[...] = jnp.zeros_like(ot_ref)
    for sb in range(2):
        first = jnp.zeros((n, 2 * CMP_HIDDEN), F32)
        second = jnp.zeros((n, 2 * CMP_HIDDEN), F32)
        for l in range(CMP_STRIDE):
            xl = xf_ref[sb, pl.ds(l, n, stride=CMP_STRIDE), :]
            first = first + jnp.dot((xl + pe_ref[sb, l:l + 1, :]).astype(BF16), w1_ref[sb, l],
                                    preferred_element_type=F32)
            second = second + jnp.dot((xl + pe_ref[sb, CMP_STRIDE + l:CMP_STRIDE + l + 1, :]).astype(BF16),
                                      w1_ref[sb, CMP_STRIDE + l], preferred_element_type=F32)
        h = _silu(first + pltpu.roll(second, n - 1, 0))
        o = jnp.dot(h.astype(BF16), w2_ref[sb], preferred_element_type=F32)
        for e in range(2):
            oe = o[:, e * LANES:(e + 1) * LANES]
            o_ref[0, 2 * sb + e, 0:n, :] = oe.astype(o_ref.dtype)
            ot_ref[0, 2 * sb + e, :, 0:n] = oe.T.astype(ot_ref.dtype)


def _blockdiag2(a):
    z = jnp.zeros_like(a)
    return jnp.concatenate([jnp.concatenate([a, z], axis=-1), jnp.concatenate([z, a], axis=-1)], axis=-2)


def _compress(p, cmp_w1, cmp_w2, cmp_pe, b, s):
    nslot = 2 * NSA_KV_HEADS
    w1 = _blockdiag2(cmp_w1.astype(BF16).reshape(2, CMP_BLOCK, HEAD_DIM, CMP_HIDDEN))
    w2 = _blockdiag2(jnp.concatenate([cmp_w2, cmp_w2], axis=-1).astype(BF16))
    pe = jnp.concatenate([cmp_pe, cmp_pe], axis=-1).astype(F32)
    kw = 2 * NSA_KV_W
    return pl.pallas_call(
        _compress_kernel,
        grid=(b,),
        in_specs=[
            pl.BlockSpec((s, kw), lambda bi: (bi, OFF_CMP_KV // kw)),
            pl.BlockSpec(w1.shape, lambda bi: (0, 0, 0, 0)),
            pl.BlockSpec(w2.shape, lambda bi: (0, 0, 0)),
            pl.BlockSpec(pe.shape, lambda bi: (0, 0, 0)),
        ],
        out_specs=[pl.BlockSpec((1, nslot, N_CMP_PAD, LANES), lambda bi: (bi, 0, 0, 0)),
                   pl.BlockSpec((1, nslot, LANES, N_CMP_PAD), lambda bi: (bi, 0, 0, 0))],
        out_shape=[jax.ShapeDtypeStruct((b, nslot, N_CMP_PAD, LANES), BF16),
                   jax.ShapeDtypeStruct((b, nslot, LANES, N_CMP_PAD), BF16)],
        scratch_shapes=[pltpu.VMEM((2, s, LANES), F32)],
        compiler_params=_params(("parallel",)),
        name="nsa_compress",
    )(p, w1, w2, pe)


def _cmp_select_kernel(q_ref, kv_ref, vt_ref, gl_ref, gate_ref, fc_ref, ovt_ref, o_ref, ns_ref):
    t = T_ATT
    ncp = kv_ref.shape[2]
    qi = pl.program_id(1)
    t0 = qi * t
    lo = lax.broadcasted_iota(jnp.int32, (t, LANES), 1) < HEAD_DIM
    zero = jnp.zeros((t, LANES), BF16)
    start = pl.multiple_of(ncp - qi * (t // CMP_STRIDE), CMP_STRIDE)
    gl = gl_ref[...]

    nsel = SEL_BLOCK
    jrow = lax.broadcasted_iota(jnp.int32, (nsel, t), 0)
    cur = (t0 + lax.broadcasted_iota(jnp.int32, (nsel, t), 1)) // SEL_BLOCK
    forced = (jrow == 0) | (jrow == cur) | (jrow == cur - 1)
    past = jrow <= cur

    for g in range(NSA_KV_HEADS):
        kc = kv_ref[0, g]
        vct = vt_ref[0, NSA_KV_HEADS + g]
        psum = jnp.zeros((ncp, t), F32)
        outs = []

        def scores(r):
            h = g * NSA_REP + r
            qp = q_ref[:, (h // 2) * LANES:(h // 2 + 1) * LANES]
            qe = jnp.where(lo, qp, zero) if r % 2 == 0 else jnp.where(lo, zero, qp)
            return lax.dot_general(kc, qe, (((1,), (1,)), ((), ())), preferred_element_type=F32)

        queue = [scores(0)]
        for r in range(NSA_REP):
            h = g * NSA_REP + r
            if r + 1 < NSA_REP:
                queue.append(scores(r + 1))
            s = fc_ref[h, pl.ds(start, ncp), :] + queue.pop(0)
            m = jnp.maximum(jnp.max(s, axis=0, keepdims=True), 0.1 * NEG_INF)
            e = jnp.exp2(s - m)
            pr = e * (1.0 / jnp.maximum(jnp.sum(e, axis=0, keepdims=True), TINY))
            psum = psum + pr
            oc = jnp.dot(vct, pr.astype(BF16), preferred_element_type=F32)
            outs.append(oc.T * jax.nn.sigmoid(gl[:, PS_GATE_LANE + h:PS_GATE_LANE + h + 1]))
        for hp in range(NSA_REP // 2):
            ls = slice((g * NSA_REP // 2 + hp) * LANES, (g * NSA_REP // 2 + hp + 1) * LANES)
            o = jnp.where(lo, outs[2 * hp], outs[2 * hp + 1])
            o_ref[:, ls] = (o * _silu(gate_ref[:, ls].astype(F32))).astype(o_ref.dtype)

        imp_t = jnp.zeros((nsel, t), F32)
        rest = psum
        for _ in range(N_SPLIT):
            piece = rest.astype(BF16)
            rest = rest - piece.astype(F32)
            imp_t = imp_t + jnp.dot(ovt_ref[...], piece, preferred_element_type=F32)
        imp_t = jnp.where(past, jnp.where(forced, SEL_FORCE, imp_t), -SEL_FORCE)
        sub = 8
        sub_row = lax.broadcasted_iota(jnp.int32, (sub, t), 0)
        rows = [imp_t[k * sub:(k + 1) * sub] for k in range(nsel // sub)]
        rank = [jnp.zeros((sub, t), F32) for _ in rows]
        for i in range(nsel):
            bi = imp_t[i:i + 1, :]
            for k, x in enumerate(rows):
                if k * sub > i:
                    ahead = jnp.where(bi >= x, 1.0, 0.0)
                elif k * sub + sub - 1 <= i:
                    ahead = jnp.where(bi > x, 1.0, 0.0)
                else:
                    ahead = jnp.where(sub_row > i - k * sub, jnp.where(bi >= x, 1.0, 0.0),
                                      jnp.where(bi > x, 1.0, 0.0))
                rank[k] = rank[k] + ahead
        rank = jnp.concatenate(rank, axis=0)
        notsel = jnp.where((rank < float(SEL_TOPK)) & past, 0.0, 1.0)
        ns2 = jnp.concatenate([notsel, notsel], axis=0).T
        ns_ref[:, g * LANES:(g + 1) * LANES] = ns2.astype(ns_ref.dtype)


def _cmp_select(p, ps, kvc, kvc_t, fc, overlap_t, b, s):
    t = T_ATT
    nq = s // t
    w = GROUP_W
    ncp = kvc.shape[2]
    return pl.pallas_call(
        _cmp_select_kernel,
        grid=(b, nq),
        in_specs=[
            pl.BlockSpec((t, w), lambda bi, qi: (bi * nq + qi, OFF_NSA_Q // w)),
            pl.BlockSpec((1, 2 * NSA_KV_HEADS, ncp, LANES), lambda bi, qi: (bi, 0, 0, 0)),
            pl.BlockSpec((1, 2 * NSA_KV_HEADS, LANES, ncp), lambda bi, qi: (bi, 0, 0, 0)),
            pl.BlockSpec((t, LANES), lambda bi, qi: (bi * nq + qi, 0)),
            pl.BlockSpec((t, w), lambda bi, qi: (bi * nq + qi, OFF_NSA_G // w)),
            pl.BlockSpec((NSA_HEADS, 2 * ncp, t), lambda bi, qi: (0, 0, 0)),
            pl.BlockSpec((SEL_BLOCK, ncp), lambda bi, qi: (0, 0)),
        ],
        out_specs=[
            pl.BlockSpec((t, w), lambda bi, qi: (bi * nq + qi, 0)),
            pl.BlockSpec((t, NSA_KV_HEADS * LANES), lambda bi, qi: (bi * nq + qi, 0)),
        ],
        out_shape=[jax.ShapeDtypeStruct((b * s, w), BF16),
                   jax.ShapeDtypeStruct((b * s, NSA_KV_HEADS * LANES), BF16)],
        compiler_params=_params(("parallel", "arbitrary")),
        name="nsa_cmp_select",
    )(p, kvc, kvc_t, ps, p, fc, overlap_t)


def _sel_attn_kernel(tab_ref, q_ref, ns_ref, k_ref, v_ref, gl_ref, gate_ref, bs_ref, o_ref,
                     qa_ref, kk_ref, vt_ref, m_ref, acc_ref, pre_ref, *, t, tk, ahead):
    qi = pl.program_id(1)
    lo_q, hi_q = _half_masks(t)
    lo = lax.broadcasted_iota(jnp.int32, (t, LANES), 1) < HEAD_DIM
    n_near = t // tk + 1

    @pl.when(qi == 0)
    def _():
        lane = lax.broadcasted_iota(jnp.int32, (tk, LANES), 1)
        krow = lax.broadcasted_iota(jnp.int32, (tk, LANES), 0)
        lo_f = jnp.where(lane < HEAD_DIM, 1.0, 0.0)
        hi_f = 1.0 - lo_f
        lo_k, hi_k = lo_f.astype(BF16), hi_f.astype(BF16)

        def merge(j, c):
            ks = pl.multiple_of(j * tk, tk)
            rows = pl.ds(ks, tk)
            hot = jnp.where((lane % HEAD_DIM) == (ks + krow) // SEL_BLOCK, NEG_INF, 0.0)
            hot_lo, hot_hi = (hot * lo_f).astype(BF16), (hot * hi_f).astype(BF16)
            k01 = k_ref[rows, :]
            k10 = _swap_halves(k01)
            kk_ref[0, rows, :] = k01 * lo_k + hot_hi
            kk_ref[1, rows, :] = hot_lo + k10 * hi_k
            kk_ref[2, rows, :] = k10 * lo_k + hot_hi
            kk_ref[3, rows, :] = hot_lo + k01 * hi_k
            _store_vt_groups(vt_ref, rows, v_ref[rows, :])
            return c

        lax.fori_loop(0, k_ref.shape[0] // tk, merge, 0)

    for h in range(NSA_HEADS):
        g = h // NSA_REP
        qp = q_ref[:, (h // 2) * LANES:(h // 2 + 1) * LANES]
        ns = ns_ref[:, g * LANES:(g + 1) * LANES]
        qa = qp * lo_q + ns * hi_q if h % 2 == 0 else ns * lo_q + qp * hi_q
        qa_ref[h] = qa.astype(F32).T.astype(BF16)
    m_ref[...] = jnp.full(m_ref.shape, NEG_INF, F32)
    acc_ref[...] = jnp.zeros_like(acc_ref)

    kv = lambda h: 2 * (h // NSA_REP) + h % 2

    def rows_of(j):
        return pl.ds(pl.multiple_of(j * tk, tk), tk)

    def scores(j, h):
        return jnp.dot(kk_ref[kv(h), rows_of(j), :], qa_ref[h], preferred_element_type=F32)

    def tile(j, near, j_next):
        rows = rows_of(j)
        queue = [pre_ref[i] for i in range(ahead)]
        for h in range(NSA_HEADS):
            if h + ahead < NSA_HEADS:
                queue.append(scores(j, h + ahead))
            elif j_next is not None:
                pre_ref[h + ahead - NSA_HEADS] = scores(j_next, h + ahead - NSA_HEADS)
            s = queue.pop(0)
            m_old = m_ref[h]
            if near is None:
                far = tab_ref[(REL_BUCKETS - 1) * NSA_HEADS + h] * LOG2E
                m_new = jnp.maximum(m_old, jnp.max(s, axis=0, keepdims=True) + far)
                p = jnp.exp2(s - (m_new - far)).astype(BF16)
            else:
                s = bs_ref[h, near * tk:(near + 1) * tk, :] + s
                m_new = jnp.maximum(m_old, jnp.max(s, axis=0, keepdims=True))
                p = jnp.exp2(s - m_new).astype(BF16)
            m_ref[h] = m_new
            vr = _vt_rows(h)
            acc_ref[h, vr] = jnp.exp2(m_old - m_new) * acc_ref[h, vr] + jnp.dot(vt_ref[kv(h), vr, rows], p,
                                                                                preferred_element_type=F32)

    jd = (qi * t) // tk
    for i in range(ahead):
        pre_ref[i] = scores(0, i)

    def body(j, c):
        tile(j, None, j + 1)
        return c

    lax.fori_loop(0, jnp.maximum(jd - 1, 0), body, 0)

    @pl.when(qi > 0)
    def _():
        tile(jd - 1, 0, jd)

    for dj in range(n_near - 1):
        tile(jd + dj, 1 + dj, jd + dj + 1 if dj + 2 < n_near else None)

    gl = gl_ref[...]
    for hp in range(NSA_HEADS // 2):
        ls = slice(hp * LANES, (hp + 1) * LANES)
        a0, a1 = acc_ref[2 * hp].T, acc_ref[2 * hp + 1].T
        c0 = PS_GATE_LANE + NSA_HEADS + 2 * hp
        o = jnp.where(lo, a0 / jnp.maximum(a0[:, HEAD_DIM:HEAD_DIM + 1], TINY) * jax.nn.sigmoid(gl[:, c0:c0 + 1]),
                      a1 / jnp.maximum(a1[:, HEAD_DIM - 1:HEAD_DIM], TINY) * jax.nn.sigmoid(gl[:, c0 + 1:c0 + 2]))
        o_ref[:, ls] = (o * _silu(gate_ref[:, ls].astype(F32))).astype(o_ref.dtype)


def _sel_attn(tab_flat, p, ps, notsel, bsel, b, s, t=512, tk=256, ahead=2):
    nq = s // t
    w = GROUP_W
    kw = NSA_KV_HEADS * LANES
    n_near = t // tk + 1
    return pl.pallas_call(
        functools.partial(_sel_attn_kernel, t=t, tk=tk, ahead=ahead),
        grid=(b, nq),
        in_specs=[
            pl.BlockSpec(memory_space=pltpu.SMEM),
            pl.BlockSpec((t, w), lambda bi, qi: (bi * nq + qi, OFF_NSA_Q // w)),
            pl.BlockSpec((t, kw), lambda bi, qi: (bi * nq + qi, 0)),
            pl.BlockSpec((s, NSA_KV_W), lambda bi, qi: (bi, OFF_SEL_K // NSA_KV_W)),
            pl.BlockSpec((s, NSA_KV_W), lambda bi, qi: (bi, OFF_SEL_V // NSA_KV_W)),
            pl.BlockSpec((t, LANES), lambda bi, qi: (bi * nq + qi, 0)),
            pl.BlockSpec((t, w), lambda bi, qi: (bi * nq + qi, OFF_NSA_G // w)),
            pl.BlockSpec((NSA_HEADS, n_near * tk, t), lambda bi, qi: (0, 0, 0), pipeline_mode=pl.Buffered(1)),
        ],
        out_specs=pl.BlockSpec((t, w), lambda bi, qi: (bi * nq + qi, 0)),
        out_shape=jax.ShapeDtypeStruct((b * s, w), BF16),
        scratch_shapes=[
            pltpu.VMEM((NSA_HEADS, LANES, t), BF16),
            pltpu.VMEM((2 * NSA_KV_HEADS, s, LANES), BF16),
            pltpu.VMEM((2 * NSA_KV_HEADS, LANES, s), BF16),
            pltpu.VMEM((NSA_HEADS, 1, t), F32),
            pltpu.VMEM((NSA_HEADS, LANES, t), F32),
            pltpu.VMEM((ahead, tk, t), F32),
        ],
        compiler_params=_params(("parallel", "arbitrary")),
        name="nsa_sel_attn",
    )(tab_flat, p, notsel, p, p, ps, p, bsel)


def _win_attn_kernel(q_ref, k_ref, v_ref, gl_ref, gate_ref, bw_ref, o_ref, qe_ref, kk_ref, vt_ref, m_ref, acc_ref,
                     pre_ref, *, t, tk, ahead):
    qi = pl.program_id(1)
    lo_q, hi_q = _half_masks(t)
    lo = lax.broadcasted_iota(jnp.int32, (t, LANES), 1) < HEAD_DIM
    n_before = WINDOW // tk
    n_tiles = n_before + t // tk

    @pl.when(qi == 0)
    def _():
        lo_k, hi_k = _half_masks(tk)

        def merge(j, c):
            rows = pl.ds(pl.multiple_of(j * tk, tk), tk)
            k01 = k_ref[rows, :]
            k10 = _swap_halves(k01)
            kk_ref[0, rows, :] = k01 * lo_k + k10 * hi_k
            kk_ref[1, rows, :] = k10 * lo_k + k01 * hi_k
            _store_vt_groups(vt_ref, rows, v_ref[rows, :])
            return c

        lax.fori_loop(0, v_ref.shape[0] // tk, merge, 0)

    for h in range(NSA_HEADS):
        qp = q_ref[:, (h // 2) * LANES:(h // 2 + 1) * LANES]
        qe_ref[h] = qp * lo_q if h % 2 == 0 else qp * hi_q
    m_ref[...] = jnp.full(m_ref.shape, NEG_INF, F32)
    acc_ref[...] = jnp.zeros_like(acc_ref)

    def rows_of(w):
        return pl.ds(pl.multiple_of((qi * (t // tk) - n_before + w) * tk, tk), tk)

    def scores(w, h):
        return lax.dot_general(kk_ref[h // NSA_REP, rows_of(w), :], qe_ref[h], (((1,), (1,)), ((), ())),
                               preferred_element_type=F32)

    def tile(w):
        rows = rows_of(w)
        queue = [pre_ref[i] for i in range(ahead)]
        for h in range(NSA_HEADS):
            if h + ahead < NSA_HEADS:
                queue.append(scores(w, h + ahead))
            elif w + 1 < n_tiles:
                pre_ref[h + ahead - NSA_HEADS] = scores(w + 1, h + ahead - NSA_HEADS)
            s = bw_ref[h, w * tk:(w + 1) * tk, :] + queue.pop(0)
            m_old = m_ref[h]
            m_new = jnp.maximum(m_old, jnp.max(s, axis=0, keepdims=True))
            m_ref[h] = m_new
            p = jnp.exp2(s - m_new).astype(BF16)
            vr = _vt_rows(h)
            acc_ref[h, vr] = jnp.exp2(m_old - m_new) * acc_ref[h, vr] + jnp.dot(
                vt_ref[2 * (h // NSA_REP) + h % 2, vr, rows], p, preferred_element_type=F32)

    first = jnp.maximum(n_before - qi * (t // tk), 0)
    for i in range(ahead):
        pre_ref[i] = scores(first, i)
    for w in range(n_tiles):
        if w < n_before:
            pl.when(qi * (t // tk) + w >= n_before)(functools.partial(tile, w))
        else:
            tile(w)

    gl = gl_ref[...]
    for hp in range(NSA_HEADS // 2):
        ls = slice(hp * LANES, (hp + 1) * LANES)
        a0, a1 = acc_ref[2 * hp].T, acc_ref[2 * hp + 1].T
        c0 = PS_GATE_LANE + 2 * NSA_HEADS + 2 * hp
        o = jnp.where(lo, a0 / jnp.maximum(a0[:, HEAD_DIM:HEAD_DIM + 1], TINY) * jax.nn.sigmoid(gl[:, c0:c0 + 1]),
                      a1 / jnp.maximum(a1[:, HEAD_DIM - 1:HEAD_DIM], TINY) * jax.nn.sigmoid(gl[:, c0 + 1:c0 + 2]))
        o_ref[:, ls] = (o * _silu(gate_ref[:, ls].astype(F32))).astype(o_ref.dtype)


def _win_attn(p, ps, bwin, b, s, t=WIN_T, tk=WIN_TK, ahead=2):
    assert t % tk == 0 and WINDOW % tk == 0
    nq = s // t
    w = GROUP_W
    return pl.pallas_call(
        functools.partial(_win_attn_kernel, t=t, tk=tk, ahead=ahead),
        grid=(b, nq),
        in_specs=[
            pl.BlockSpec((t, w), lambda bi, qi: (bi * nq + qi, OFF_NSA_Q // w)),
            pl.BlockSpec((s, NSA_KV_W), lambda bi, qi: (bi, OFF_WIN_K // NSA_KV_W)),
            pl.BlockSpec((s, NSA_KV_W), lambda bi, qi: (bi, OFF_WIN_V // NSA_KV_W)),
            pl.BlockSpec((t, LANES), lambda bi, qi: (bi * nq + qi, 0)),
            pl.BlockSpec((t, w), lambda bi, qi: (bi * nq + qi, OFF_NSA_G // w)),
            pl.BlockSpec((NSA_HEADS, WINDOW + t, t), lambda bi, qi: (0, 0, 0), pipeline_mode=pl.Buffered(1)),
        ],
        out_specs=pl.BlockSpec((t, w), lambda bi, qi: (bi * nq + qi, 0)),
        out_shape=jax.ShapeDtypeStruct((b * s, w), BF16),
        scratch_shapes=[
            pltpu.VMEM((NSA_HEADS, t, LANES), BF16),
            pltpu.VMEM((NSA_KV_HEADS, s, LANES), BF16),
            pltpu.VMEM((2 * NSA_KV_HEADS, LANES, s), BF16),
            pltpu.VMEM((NSA_HEADS, 1, t), F32),
            pltpu.VMEM((NSA_HEADS, LANES, t), F32),
            pltpu.VMEM((ahead, tk, t), F32),
        ],
        compiler_params=_params(("parallel", "arbitrary")),
        name="nsa_win_attn",
    )(p, p, p, ps, p, bwin)


def _mem_attn_kernel(q_ref, gate_ref, kv_ref, o_ref):
    scale = MEM_HEAD_DIM ** -0.5
    for h in range(MEM_HEADS):
        ls = slice(h * LANES, (h + 1) * LANES)
        k = kv_ref[:, ls]
        v = kv_ref[:, GROUP_W + h * LANES:GROUP_W + (h + 1) * LANES]
        s = lax.dot_general(q_ref[:, ls], k, (((1,), (1,)), ((), ())), preferred_element_type=F32) * scale
        m = jnp.max(s, axis=1, keepdims=True)
        e = jnp.exp(s - m)
        l = jnp.sum(e, axis=1, keepdims=True)
        o = jnp.dot(e.astype(BF16), v, preferred_element_type=F32) / l
        o_ref[:, ls] = (o * _silu(gate_ref[:, ls].astype(F32))).astype(o_ref.dtype)


def _mem_attn(p, mem_kv, b, s, t=512):
    nq = s // t
    w = GROUP_W
    m = mem_kv.shape[0] // b
    return pl.pallas_call(
        _mem_attn_kernel,
        grid=(b, nq),
        in_specs=[
            pl.BlockSpec((t, w), lambda bi, qi: (bi * nq + qi, OFF_MEM_Q // w)),
            pl.BlockSpec((t, w), lambda bi, qi: (bi * nq + qi, OFF_MEM_G // w)),
            pl.BlockSpec((m, 2 * w), lambda bi, qi: (bi, 0)),
        ],
        out_specs=pl.BlockSpec((t, w), lambda bi, qi: (bi * nq + qi, 0)),
        out_shape=jax.ShapeDtypeStruct((b * s, w), BF16),
        compiler_params=_params(("parallel", "arbitrary")),
        name="mem_attn",
    )(p, p, mem_kv)


def _out_proj_kernel(x_ref, of_ref, os_ref, oc_ref, osel_ref, ow_ref, om_ref, w_ref, g_ref, o_ref, *, final):
    w = GROUP_W
    nsa = (oc_ref[...].astype(F32) + osel_ref[...].astype(F32) + ow_ref[...].astype(F32)).astype(BF16)
    acc = x_ref[...]
    for i, part in enumerate((of_ref[...], os_ref[...], nsa, om_ref[...])):
        acc = acc + jnp.dot(part, w_ref[i * w:(i + 1) * w, :], preferred_element_type=F32)
    if final:
        ms = jnp.mean(acc * acc, axis=-1, keepdims=True)
        acc = (acc * lax.rsqrt(ms + EPS)) * g_ref[...]
    o_ref[...] = acc


def _out_proj(x2d, parts, w_out, g, final, tm=512):
    n, d = x2d.shape
    w = GROUP_W
    part_spec = pl.BlockSpec((tm, w), lambda i: (i, 0))
    return pl.pallas_call(
        functools.partial(_out_proj_kernel, final=final),
        grid=(n // tm,),
        in_specs=[pl.BlockSpec((tm, d), lambda i: (i, 0))] + [part_spec] * 6 + [
            pl.BlockSpec((4 * w, d), lambda i: (0, 0)),
            pl.BlockSpec((1, d), lambda i: (0, 0)),
        ],
        out_specs=pl.BlockSpec((tm, d), lambda i: (i, 0)),
        out_shape=jax.ShapeDtypeStruct((n, d), F32),
        compiler_params=_params(("parallel",)),
        name="out_proj",
    )(x2d, *parts, w_out, g.reshape(1, d))


def _pack_in_proj(w_in_l):
    fox, ssm, nsa, mem = 0, FOX_COLS, FOX_COLS + SSM_COLS, FOX_COLS + SSM_COLS + NSA_COLS
    w = GROUP_W
    q_scale = HEAD_DIM ** -0.5 * LOG2E
    cols = lambda a, n: w_in_l[:, a:a + n]
    kv = lambda slot: nsa + w + slot * NSA_KV_W
    main = ([cols(fox, w) * q_scale, cols(fox + w, 3 * w)]
            + [cols(ssm + w, SSM_CONV_DIM), cols(ssm, w)]
            + [cols(nsa, w) * q_scale, cols(nsa + w + 6 * NSA_KV_W + 3 * NSA_HEADS, w)]
            + [cols(mem, 2 * w)]
            + [cols(kv(2), 4 * NSA_KV_W)]
            + [cols(kv(0), 2 * NSA_KV_W)])
    w_main = jnp.concatenate(main, axis=1).astype(BF16)
    assert w_main.shape[1] == P_COLS
    small = ([cols(fox + 4 * w, FOX_HEADS)] * N_SPLIT
             + [cols(ssm + w + SSM_CONV_DIM, SSM_HEADS)]
             + [cols(nsa + w + 6 * NSA_KV_W, 3 * NSA_HEADS)])
    assert N_SPLIT * FOX_HEADS == PS_DT_LANE and PS_DT_LANE + SSM_HEADS == PS_GATE_LANE
    used = PS_GATE_LANE + 3 * NSA_HEADS
    small.append(jnp.zeros((w_in_l.shape[0], PS_COLS - used), w_in_l.dtype))
    return w_main, jnp.concatenate(small, axis=1).astype(BF16)


def _pad_lanes(v, first=0):
    return jnp.pad(v.astype(F32), (first, LANES - first - v.shape[0])).reshape(1, LANES)


def _trunk(x, mem, norm_g, w_in, fox_f_bias, ssm_conv_w, ssm_conv_b, ssm_dt_bias, ssm_a_log, ssm_d,
           ssm_norm_g, nsa_cmp_pe, nsa_cmp_w1, nsa_cmp_w2, rel_bias_table, mem_norm_g, w_mem_kv, w_out,
           final_norm_g):
    b, s, d = x.shape
    depth = w_in.shape[0]
    n = b * s
    m_tok = mem.shape[1]
    n_cmp = (s - CMP_BLOCK) // CMP_STRIDE + 1
    n_rows = s // CMP_STRIDE
    assert s % 512 == 0 and s // SEL_BLOCK <= HEAD_DIM and n_rows <= N_CMP_PAD and d == D_MODEL

    tab_flat = rel_bias_table.astype(F32).reshape(-1)
    unbounded = 1 << 30
    bwin = _t5_table(tab_flat, WINDOW + WIN_T, WIN_T, 1, WINDOW, WINDOW)
    bsel = _t5_table(tab_flat, (SEL_T // SEL_TK + 1) * SEL_TK, SEL_T, 1, SEL_TK, unbounded)
    fcmp = _t5_table(tab_flat, 2 * N_CMP_PAD, T_ATT, CMP_STRIDE, CMP_STRIDE * N_CMP_PAD - (CMP_BLOCK - 1), unbounded)

    cs = np.arange(N_CMP_PAD)[None, :] * CMP_STRIDE
    js = np.arange(SEL_BLOCK)[:, None] * SEL_BLOCK
    overlap_t = ((cs < js + SEL_BLOCK) & (cs + CMP_BLOCK > js) & (np.arange(N_CMP_PAD)[None, :] < n_cmp)
                 & (np.arange(SEL_BLOCK)[:, None] < s // SEL_BLOCK)).astype(np.float32)
    overlap_t = jnp.asarray(overlap_t, BF16)

    x2d = x.reshape(n, d)
    mem2d = mem.reshape(b * m_tok, d)
    half = CMP_STRIDE * HEAD_DIM
    for l in range(depth):
        w_main, w_small = _pack_in_proj(w_in[l])
        p, ps = _norm_proj(x2d, norm_g[l], w_main, w_small)

        qaug, kaug = _fox_cumsum(ps, _pad_lanes(jnp.tile(fox_f_bias[l], N_SPLIT)), b, s)
        o_fox = _fox_attn(p, qaug, kaug, b, s)

        o_ssd = _ssd(p, ps, ssm_conv_w[l].astype(F32), ssm_conv_b[l].reshape(1, -1).astype(F32),
                     _pad_lanes(ssm_dt_bias[l], PS_DT_LANE), _pad_lanes(ssm_a_log[l], PS_DT_LANE),
                     jnp.repeat(ssm_d[l].astype(F32), HEAD_DIM).reshape(1, GROUP_W),
                     ssm_norm_g[l].reshape(1, GROUP_W).astype(F32), b, s)

        kv_cmp, kv_cmp_t = _compress(p, nsa_cmp_w1[l], nsa_cmp_w2[l], nsa_cmp_pe[l], b, s)
        o_cmp, notsel = _cmp_select(p, ps, kv_cmp, kv_cmp_t, fcmp, overlap_t, b, s)
        o_sel = _sel_attn(tab_flat, p, ps, notsel, bsel, b, s)
        o_win = _win_attn(p, ps, bwin, b, s)

        w_kv = w_mem_kv[l].astype(BF16)
        mem_kv, _ = _norm_proj(mem2d, mem_norm_g[l], w_kv, w_kv[:, :LANES], tm=min(512, b * m_tok))
        o_mem = _mem_attn(p, mem_kv, b, s)

        x2d = _out_proj(x2d, (o_fox, o_ssd, o_cmp, o_sel, o_win, o_mem), w_out[l].astype(BF16),
                        final_norm_g, final=(l == depth - 1))
    return x2d.reshape(b, s, d)


def kernel(x, mem, norm_g, w_in, fox_f_bias, ssm_conv_w, ssm_conv_b, ssm_dt_bias, ssm_a_log, ssm_d, ssm_norm_g,
           nsa_cmp_pe, nsa_cmp_w1, nsa_cmp_w2, rel_bias_table, mem_norm_g, w_mem_kv, w_out, final_norm_g):
    return _trunk(x, mem, norm_g, w_in, fox_f_bias, ssm_conv_w, ssm_conv_b, ssm_dt_bias, ssm_a_log, ssm_d,
                  ssm_norm_g, nsa_cmp_pe, nsa_cmp_w1, nsa_cmp_w2, rel_bias_table, mem_norm_g, w_mem_kv, w_out,
                  final_norm_g)
```

```python
import functools
import math

import numpy as np
import jax
import jax.numpy as jnp
from jax import lax
from jax.experimental import pallas as pl
from jax.experimental.pallas import tpu as pltpu

F32 = jnp.float32
BF16 = jnp.bfloat16

D_MODEL = 1024
GROUP_W = 512
HEAD_DIM = 64
EPS = 1e-6
NEG_INF = -1e30
TINY = 1e-30
LOG2E = math.log2(math.e)

FOX_HEADS = 8
SSM_HEADS = 8
SSM_STATE = 128
SSM_GROUPS = 2
SSM_CONV = 4
SSM_CHUNK = 128
SSM_CONV_DIM = GROUP_W + 2 * SSM_GROUPS * SSM_STATE

NSA_HEADS = 8
NSA_KV_HEADS = 2
NSA_REP = NSA_HEADS // NSA_KV_HEADS
NSA_KV_W = NSA_KV_HEADS * HEAD_DIM
CMP_BLOCK = 32
CMP_STRIDE = 16
CMP_HIDDEN = 2 * HEAD_DIM
SEL_BLOCK = 64
SEL_TOPK = 16
WINDOW = 512
SEL_FORCE = 1e9

MEM_HEADS = 4
MEM_HEAD_DIM = GROUP_W // MEM_HEADS
REL_BUCKETS = 32
REL_MAX_DIST = 128

FOX_COLS = 4 * GROUP_W + FOX_HEADS
SSM_COLS = GROUP_W + SSM_CONV_DIM + SSM_HEADS
NSA_COLS = 2 * GROUP_W + 6 * NSA_KV_W + 3 * NSA_HEADS
MEM_COLS = 2 * GROUP_W

LANES = 128
VMEM_LIMIT = 56 * 1024 * 1024

OFF_FOX_Q, OFF_FOX_K, OFF_FOX_V, OFF_FOX_G = 0, 512, 1024, 1536
OFF_SSM_XBC, OFF_SSM_Z = 2048, 3072
OFF_NSA_Q, OFF_NSA_G = 3584, 4096
OFF_MEM_Q, OFF_MEM_G = 4608, 5120
OFF_SEL_K, OFF_SEL_V, OFF_WIN_K, OFF_WIN_V, OFF_CMP_KV = 5632, 5760, 5888, 6016, 6144
P_COLS = 6400
PS_COLS = LANES
PS_DT_LANE = 24
PS_GATE_LANE = 32

T_ATT = 256
SEL_T, SEL_TK = 512, 256
WIN_T, WIN_TK = 512, 256
N_CMP_PAD = 256


def _params(sem):
    return pltpu.CompilerParams(dimension_semantics=sem, vmem_limit_bytes=VMEM_LIMIT)


def _t5_bucket_np(dist):
    n = np.maximum(dist, 0)
    max_exact = REL_BUCKETS // 2
    nf = np.maximum(n, 1).astype(np.float32)
    large = max_exact + (np.log(nf / np.float32(max_exact)) / np.float32(math.log(REL_MAX_DIST / max_exact))
                         * np.float32(REL_BUCKETS - max_exact)).astype(np.int32)
    large = np.minimum(large, REL_BUCKETS - 1)
    return np.where(n < max_exact, n, large).astype(np.int32)


def _silu(x):
    h = 0.5 * x
    return h + h * jnp.tanh(h)


def _norm_proj_kernel(x_ref, g_ref, w_ref, ws_ref, p_ref, ps_ref, *, chunk):
    x = x_ref[...]
    ms = jnp.mean(x * x, axis=-1, keepdims=True)
    h = ((x * lax.rsqrt(ms + EPS)) * g_ref[...]).astype(BF16)
    ncol = p_ref.shape[1]
    for c0 in range(0, ncol, chunk):
        c1 = min(c0 + chunk, ncol)
        p_ref[:, c0:c1] = jnp.dot(h, w_ref[:, c0:c1], preferred_element_type=F32).astype(p_ref.dtype)
    ps_ref[...] = jnp.dot(h, ws_ref[...], preferred_element_type=F32)


def _norm_proj(x2d, g, w_main, w_small, tm=512):
    n, d = x2d.shape
    pc, sc = w_main.shape[1], w_small.shape[1]
    return pl.pallas_call(
        functools.partial(_norm_proj_kernel, chunk=512),
        grid=(n // tm,),
        in_specs=[
            pl.BlockSpec((tm, d), lambda i: (i, 0)),
            pl.BlockSpec((1, d), lambda i: (0, 0)),
            pl.BlockSpec((d, pc), lambda i: (0, 0)),
            pl.BlockSpec((d, sc), lambda i: (0, 0)),
        ],
        out_specs=[
            pl.BlockSpec((tm, pc), lambda i: (i, 0)),
            pl.BlockSpec((tm, sc), lambda i: (i, 0)),
        ],
        out_shape=[jax.ShapeDtypeStruct((n, pc), BF16), jax.ShapeDtypeStruct((n, sc), F32)],
        compiler_params=_params(("parallel",)),
        name="norm_proj",
    )(x2d, g.reshape(1, d), w_main, w_small)


def _t5_table_kernel(tab_ref, bucket_ref, o_ref, *, stride, off, limit):
    h = pl.program_id(0)
    rows, cols = o_ref.shape[1], o_ref.shape[2]
    bucket = bucket_ref[...]
    near = jnp.zeros(bucket.shape, F32)
    for b in range(REL_BUCKETS):
        near = jnp.where(bucket == b, tab_ref[b * NSA_HEADS + h] * LOG2E, near)
    far = tab_ref[(REL_BUCKETS - 1) * NSA_HEADS + h] * LOG2E
    base = jnp.concatenate([near] * (rows // near.shape[0]), axis=0)
    rolled = pltpu.roll(base, 0, 1, stride=stride, stride_axis=0)
    d = (lax.broadcasted_iota(jnp.int32, (rows, cols), 1) - stride * lax.broadcasted_iota(jnp.int32, (rows, cols), 0)
         + off)
    o_ref[0] = jnp.where((d < 0) | (d >= limit), NEG_INF, jnp.where(d >= REL_MAX_DIST, far, rolled))


def _t5_table(tab_flat, rows, cols, stride, off, limit):
    assert cols >= 2 * REL_MAX_DIST and rows % 8 == 0
    k = (np.arange(cols) + off) % cols
    bucket = np.broadcast_to(np.where(k < REL_MAX_DIST, _t5_bucket_np(k), -1).astype(np.int32), (8, cols))
    return pl.pallas_call(
        functools.partial(_t5_table_kernel, stride=stride, off=off, limit=limit),
        grid=(NSA_HEADS,),
        in_specs=[
            pl.BlockSpec(memory_space=pltpu.SMEM),
            pl.BlockSpec((8, cols), lambda h: (0, 0)),
        ],
        out_specs=pl.BlockSpec((1, rows, cols), lambda h: (h, 0, 0)),
        out_shape=jax.ShapeDtypeStruct((NSA_HEADS, rows, cols), F32),
        compiler_params=_params(("arbitrary",)),
        name="t5_table",
    )(tab_flat, jnp.asarray(bucket))


def _tri_lower(n):
    r = lax.broadcasted_iota(jnp.int32, (n, n), 0)
    c = lax.broadcasted_iota(jnp.int32, (n, n), 1)
    return (r >= c).astype(F32)


N_SPLIT = 3


def _fox_aug_lane(h, i):
    return LANES * (h // 2) + (HEAD_DIM if h % 2 == 0 else 0) + i


def _fox_aug_consts():
    pq = np.zeros((LANES, GROUP_W), np.float32)
    pk = np.zeros((LANES, GROUP_W), np.float32)
    oq = np.zeros((1, GROUP_W), np.float32)
    ok = np.zeros((1, GROUP_W), np.float32)
    for h in range(FOX_HEADS):
        for i in range(N_SPLIT):
            pq[i * FOX_HEADS + h, _fox_aug_lane(h, i)] = 1.0
            pk[i * FOX_HEADS + h, _fox_aug_lane(h, N_SPLIT + i)] = -1.0
            oq[0, _fox_aug_lane(h, N_SPLIT + i)] = 1.0
            ok[0, _fox_aug_lane(h, i)] = 1.0
    return pq, pk, oq, ok


def _fox_cumsum_kernel(f_ref, b_ref, pq_ref, pk_ref, oq_ref, ok_ref, qa_ref, ka_ref, carry_ref, *, ts):
    @pl.when(pl.program_id(1) == 0)
    def _():
        carry_ref[...] = jnp.zeros_like(carry_ref)

    z = f_ref[...] + b_ref[...]
    logf = (jnp.minimum(z, 0.0) - jnp.log(1.0 + jnp.exp(-jnp.abs(z)))) * LOG2E
    tri = _tri_lower(LANES).astype(BF16)
    group = lax.broadcasted_iota(jnp.int32, (LANES, LANES), 1) // FOX_HEADS
    carry = carry_ref[...]
    for c in range(ts // LANES):
        rows = slice(c * LANES, (c + 1) * LANES)
        cs, rest = carry, logf[rows]
        for _ in range(N_SPLIT):
            piece = rest.astype(BF16)
            rest = rest - piece.astype(F32)
            cs = cs + jnp.dot(tri, piece, preferred_element_type=F32)
        carry = cs[LANES - 1:LANES, :]
        cat, rest = None, cs
        for i in range(N_SPLIT):
            piece = rest.astype(BF16).astype(F32)
            rest = rest - piece
            cat = piece if cat is None else jnp.where(group == i, piece, cat)
        cat = cat.astype(BF16)
        qa_ref[rows, :] = (oq_ref[...] + jnp.dot(cat, pq_ref[...], preferred_element_type=F32)).astype(BF16)
        ka_ref[rows, :] = (ok_ref[...] + jnp.dot(cat, pk_ref[...], preferred_element_type=F32)).astype(BF16)
    carry_ref[...] = carry


def _fox_cumsum(ps, f_bias_pad, b, s, ts=512):
    ns = s // ts
    pq, pk, oq, ok = _fox_aug_consts()
    const = lambda bi, si: (0, 0)
    return pl.pallas_call(
        functools.partial(_fox_cumsum_kernel, ts=ts),
        grid=(b, ns),
        in_specs=[
            pl.BlockSpec((ts, LANES), lambda bi, si: (bi * ns + si, 0)),
            pl.BlockSpec((1, LANES), const),
            pl.BlockSpec(pq.shape, const),
            pl.BlockSpec(pk.shape, const),
            pl.BlockSpec(oq.shape, const),
            pl.BlockSpec(ok.shape, const),
        ],
        out_specs=[
            pl.BlockSpec((ts, GROUP_W), lambda bi, si: (bi * ns + si, 0)),
            pl.BlockSpec((ts, GROUP_W), lambda bi, si: (bi * ns + si, 0)),
        ],
        out_shape=[jax.ShapeDtypeStruct((b * s, GROUP_W), BF16), jax.ShapeDtypeStruct((b * s, GROUP_W), BF16)],
        scratch_shapes=[pltpu.VMEM((1, LANES), F32)],
        compiler_params=_params(("parallel", "arbitrary")),
        name="fox_cumsum",
    )(ps, f_bias_pad, jnp.asarray(pq, BF16), jnp.asarray(pk, BF16), jnp.asarray(oq), jnp.asarray(ok))


ONES_ROWS = 16


def _vt_rows(h):
    return slice(0, HEAD_DIM + ONES_ROWS) if h % 2 == 0 else slice(HEAD_DIM - ONES_ROWS, LANES)


def _store_vt(vt_ref, i, rows, v_pair):
    vt = v_pair.astype(F32).T.astype(BF16)
    ones = jnp.ones((HEAD_DIM, v_pair.shape[0]), BF16)
    vt_ref[i, 0:HEAD_DIM, rows] = vt[0:HEAD_DIM]
    vt_ref[i, HEAD_DIM:LANES, rows] = ones
    vt_ref[i + 1, 0:HEAD_DIM, rows] = ones
    vt_ref[i + 1, HEAD_DIM:LANES, rows] = vt[HEAD_DIM:LANES]


def _store_vt_groups(vt_ref, rows, v_groups):
    vt = v_groups.astype(F32).T.astype(BF16)
    ones = jnp.ones((HEAD_DIM, v_groups.shape[0]), BF16)
    for g in range(NSA_KV_HEADS):
        vg = vt[g * HEAD_DIM:(g + 1) * HEAD_DIM]
        vt_ref[2 * g, 0:HEAD_DIM, rows] = vg
        vt_ref[2 * g, HEAD_DIM:LANES, rows] = ones
        vt_ref[2 * g + 1, 0:HEAD_DIM, rows] = ones
        vt_ref[2 * g + 1, HEAD_DIM:LANES, rows] = vg


def _finish_pair(acc_ref, hp):
    ae, ao = acc_ref[2 * hp], acc_ref[2 * hp + 1]
    top = lax.broadcasted_iota(jnp.int32, ae.shape, 0) < HEAD_DIM
    both = jnp.where(top, ae / jnp.maximum(ae[HEAD_DIM:HEAD_DIM + 1, :], TINY),
                     ao / jnp.maximum(ao[HEAD_DIM - 1:HEAD_DIM, :], TINY))
    return both.T


def _swap_halves(x_b):
    return pltpu.roll(x_b.astype(F32), HEAD_DIM, 1).astype(BF16)


def _half_masks(rows):
    lo = jnp.where(lax.broadcasted_iota(jnp.int32, (rows, LANES), 1) < HEAD_DIM, 1.0, 0.0)
    return lo.astype(BF16), (1.0 - lo).astype(BF16)


def _fox_attn_kernel(q_ref, qa_ref, k_ref, ka_ref, v_ref, gate_ref, o_ref, qs_ref, kk_ref, vt_ref, m_ref, acc_ref,
                     pre_ref, *, t, tk, ahead):
    qi = pl.program_id(1)
    lo_q, hi_q = _half_masks(t)
    lo = lax.broadcasted_iota(jnp.int32, (t, LANES), 1) < HEAD_DIM
    cm = lax.broadcasted_iota(jnp.int32, (tk, t), 0) - lax.broadcasted_iota(jnp.int32, (tk, t), 1)

    @pl.when(qi == 0)
    def _():
        lo_k, hi_k = _half_masks(tk)

        def merge(j, c):
            rows = pl.ds(pl.multiple_of(j * tk, tk), tk)
            for hp in range(FOX_HEADS // 2):
                ls = slice(hp * LANES, (hp + 1) * LANES)
                kp, ka, vp = k_ref[rows, ls], ka_ref[rows, ls], v_ref[rows, ls]
                kk_ref[2 * hp, rows, :] = kp * lo_k + ka * hi_k
                kk_ref[2 * hp + 1, rows, :] = ka * lo_k + kp * hi_k
                _store_vt(vt_ref, 2 * hp, rows, vp)
            return c

        lax.fori_loop(0, k_ref.shape[0] // tk, merge, 0)

    for hp in range(FOX_HEADS // 2):
        ls = slice(hp * LANES, (hp + 1) * LANES)
        qp, qa = q_ref[:, ls], qa_ref[:, ls]
        qs_ref[2 * hp] = (qp * lo_q + qa * hi_q).astype(F32).T.astype(BF16)
        qs_ref[2 * hp + 1] = (qa * lo_q + qp * hi_q).astype(F32).T.astype(BF16)
    m_ref[...] = jnp.full(m_ref.shape, NEG_INF, F32)
    acc_ref[...] = jnp.zeros_like(acc_ref)

    def rows_of(j):
        return pl.ds(pl.multiple_of(j * tk, tk), tk)

    def scores(j, h):
        return jnp.dot(kk_ref[h, rows_of(j), :], qs_ref[h], preferred_element_type=F32)

    def tile(j, diag, j_next):
        queue = [pre_ref[i] for i in range(ahead)]
        for h in range(FOX_HEADS):
            if h + ahead < FOX_HEADS:
                queue.append(scores(j, h + ahead))
            elif j_next is not None:
                pre_ref[h + ahead - FOX_HEADS] = scores(j_next, h + ahead - FOX_HEADS)
            s = queue.pop(0)
            if diag:
                s = jnp.where(cm <= qi * t - j * tk, s, NEG_INF)
            m_old = m_ref[h]
            m_new = jnp.maximum(m_old, jnp.max(s, axis=0, keepdims=True))
            m_ref[h] = m_new
            p = jnp.exp2(s - m_new).astype(BF16)
            vr = _vt_rows(h)
            acc_ref[h, vr] = jnp.exp2(m_old - m_new) * acc_ref[h, vr] + jnp.dot(vt_ref[h, vr, rows_of(j)], p,
                                                                                preferred_element_type=F32)

    jd = (qi * t) // tk
    n_diag = max(t // tk, 1)
    for i in range(ahead):
        pre_ref[i] = scores(0, i)

    def body(j, c):
        tile(j, False, j + 1)
        return c

    lax.fori_loop(0, jd, body, 0)
    for dj in range(n_diag):
        tile(jd + dj, True, jd + dj + 1 if dj + 1 < n_diag else None)

    for hp in range(FOX_HEADS // 2):
        ls = slice(hp * LANES, (hp + 1) * LANES)
        o_ref[:, ls] = (_finish_pair(acc_ref, hp) * _silu(gate_ref[:, ls].astype(F32))).astype(o_ref.dtype)


def _fox_attn(p, qaug, kaug, b, s, t=512, tk=256, ahead=2):
    nq = s // t
    w = GROUP_W
    return pl.pallas_call(
        functools.partial(_fox_attn_kernel, t=t, tk=tk, ahead=ahead),
        grid=(b, nq),
        in_specs=[
            pl.BlockSpec((t, w), lambda bi, qi: (bi * nq + qi, OFF_FOX_Q // w)),
            pl.BlockSpec((t, w), lambda bi, qi: (bi * nq + qi, 0)),
            pl.BlockSpec((s, w), lambda bi, qi: (bi, OFF_FOX_K // w)),
            pl.BlockSpec((s, w), lambda bi, qi: (bi, 0)),
            pl.BlockSpec((s, w), lambda bi, qi: (bi, OFF_FOX_V // w)),
            pl.BlockSpec((t, w), lambda bi, qi: (bi * nq + qi, OFF_FOX_G // w)),
        ],
        out_specs=pl.BlockSpec((t, w), lambda bi, qi: (bi * nq + qi, 0)),
        out_shape=jax.ShapeDtypeStruct((b * s, w), BF16),
        scratch_shapes=[
            pltpu.VMEM((FOX_HEADS, LANES, t), BF16),
            pltpu.VMEM((FOX_HEADS, s, LANES), BF16),
            pltpu.VMEM((FOX_HEADS, LANES, s), BF16),
            pltpu.VMEM((FOX_HEADS, 1, t), F32),
            pltpu.VMEM((FOX_HEADS, LANES, t), F32),
            pltpu.VMEM((ahead, tk, t), F32),
        ],
        compiler_params=_params(("parallel", "arbitrary")),
        name="fox_attn",
    )(p, qaug, p, kaug, p, p)


def _dot_split(lhs_f32, rhs_b):
    out, rest = None, lhs_f32
    for _ in range(N_SPLIT):
        piece = rest.astype(BF16)
        rest = rest - piece.astype(F32)
        term = jnp.dot(piece, rhs_b, preferred_element_type=F32)
        out = term if out is None else out + term
    return out


def _ssd_kernel(z_ref, xbc_ref, dt_ref, cw_ref, cb_ref, dtb_ref, alog_ref, dsk_ref, ng_ref, o_ref,
                xpad_ref, xc_ref, state_ref, y_ref, *, nch):
    q = SSM_CHUNK
    rows_all = nch * q
    halo = 8

    @pl.when(pl.program_id(1) == 0)
    def _():
        xpad_ref[0:halo, :] = jnp.zeros((halo, SSM_CONV_DIM), F32)
        state_ref[...] = jnp.zeros_like(state_ref)

    xpad_ref[halo:halo + rows_all, :] = xbc_ref[...].astype(F32)
    y = cb_ref[...]
    for k in range(SSM_CONV):
        off = halo - (SSM_CONV - 1) + k
        y = y + cw_ref[k:k + 1, :] * xpad_ref[off:off + rows_all, :]
    xpad_ref[0:halo, :] = xpad_ref[rows_all:rows_all + halo, :]
    xc_ref[...] = _silu(y)

    x_dt = dt_ref[...] + dtb_ref[...]
    dt_all = jnp.maximum(x_dt, 0.0) + jnp.log(1.0 + jnp.exp(-jnp.abs(x_dt)))
    a_all = dt_all * (-jnp.exp(alog_ref[...]))
    tri_t = (lax.broadcasted_iota(jnp.int32, (q, q), 0) <= lax.broadcasted_iota(jnp.int32, (q, q), 1)).astype(BF16)
    er = lax.broadcasted_iota(jnp.int32, (LANES, GROUP_W), 0)
    ec = lax.broadcasted_iota(jnp.int32, (LANES, GROUP_W), 1)
    expand = jnp.where(ec // HEAD_DIM == er - PS_DT_LANE, 1.0, 0.0).astype(BF16)
    row = lax.broadcasted_iota(jnp.int32, (q, q), 0)
    col = lax.broadcasted_iota(jnp.int32, (q, q), 1)
    causal = row >= col
    lo = lax.broadcasted_iota(jnp.int32, (q, LANES), 1) < HEAD_DIM
    gw = GROUP_W // SSM_GROUPS
    hpg = SSM_HEADS // SSM_GROUPS

    for c in range(nch):
        rs = slice(c * q, (c + 1) * q)
        xs = xc_ref[rs, :GROUP_W]
        dt = dt_all[rs]
        acs_t = _dot_split(a_all[rs].T, tri_t)
        acs = acs_t.T
        stacked = jnp.concatenate([dt, jnp.exp(acs), jnp.exp(acs[q - 1:q, :] - acs)], axis=0)
        full = _dot_split(stacked, expand)
        dt_full, eacs_full, dec_full = full[0:q], full[q:2 * q], full[2 * q:3 * q]
        xdt = xs * dt_full
        xdt_b = xdt.astype(BF16)
        xdec_b = (xdt * dec_full).astype(BF16)

        for g in range(SSM_GROUPS):
            bm = xc_ref[rs, GROUP_W + g * SSM_STATE:GROUP_W + (g + 1) * SSM_STATE]
            cm = xc_ref[rs, GROUP_W + (SSM_GROUPS + g) * SSM_STATE:GROUP_W + (SSM_GROUPS + g + 1) * SSM_STATE]
            bm_b = bm.astype(BF16)
            cm_b = cm.astype(BF16)
            gs = slice(g * gw, (g + 1) * gw)
            cbg = lax.dot_general(cm_b, bm_b, (((1,), (1,)), ((), ())), preferred_element_type=F32)
            st = state_ref[:, gs]
            y_off = jnp.dot(cm_b, st.astype(BF16), preferred_element_type=F32) * eacs_full[:, gs]
            cst = jnp.dot(bm.T.astype(BF16), xdec_b[:, gs], preferred_element_type=F32)
            state_ref[:, gs] = st * eacs_full[q - 1:q, gs] + cst
            for hp in range(hpg // 2):
                ls = slice(g * gw + hp * LANES, g * gw + (hp + 1) * LANES)
                yd = []
                for e in range(2):
                    h = g * hpg + 2 * hp + e
                    hl = PS_DT_LANE + h
                    seg = jnp.exp(jnp.where(causal, acs[:, hl:hl + 1] - acs_t[hl:hl + 1, :], NEG_INF))
                    yd.append(jnp.dot((cbg * seg).astype(BF16), xdt_b[:, ls], preferred_element_type=F32))
                y_ref[:, ls] = jnp.where(lo, yd[0], yd[1]) + y_off[:, hp * LANES:(hp + 1) * LANES]

        yz = (y_ref[...] + xs * dsk_ref[...]) * _silu(z_ref[rs, :].astype(F32))
        for g in range(SSM_GROUPS):
            gs = slice(g * gw, (g + 1) * gw)
            blk = yz[:, gs]
            ms = jnp.mean(blk * blk, axis=-1, keepdims=True)
            o_ref[rs, gs] = ((blk * lax.rsqrt(ms + EPS)) * ng_ref[:, gs]).astype(o_ref.dtype)


def _ssd(p, ps, conv_w, conv_b, dt_bias_pad, a_log_pad, d_full, norm_g, b, s, nch=4):
    q = SSM_CHUNK
    rows = nch * q
    nc = s // rows
    row = lambda bi, ci: (bi * nc + ci)
    const = lambda bi, ci: (0, 0)
    return pl.pallas_call(
        functools.partial(_ssd_kernel, nch=nch),
        grid=(b, nc),
        in_specs=[
            pl.BlockSpec((rows, GROUP_W), lambda bi, ci: (row(bi, ci), OFF_SSM_Z // GROUP_W)),
            pl.BlockSpec((rows, SSM_CONV_DIM), lambda bi, ci: (row(bi, ci), OFF_SSM_XBC // SSM_CONV_DIM)),
            pl.BlockSpec((rows, LANES), lambda bi, ci: (row(bi, ci), 0)),
            pl.BlockSpec((SSM_CONV, SSM_CONV_DIM), const),
            pl.BlockSpec((1, SSM_CONV_DIM), const),
            pl.BlockSpec((1, LANES), const),
            pl.BlockSpec((1, LANES), const),
            pl.BlockSpec((1, GROUP_W), const),
            pl.BlockSpec((1, GROUP_W), const),
        ],
        out_specs=pl.BlockSpec((rows, GROUP_W), lambda bi, ci: (row(bi, ci), 0)),
        out_shape=jax.ShapeDtypeStruct((b * s, GROUP_W), BF16),
        scratch_shapes=[
            pltpu.VMEM((rows + 8, SSM_CONV_DIM), F32),
            pltpu.VMEM((rows, SSM_CONV_DIM), F32),
            pltpu.VMEM((SSM_STATE, GROUP_W), F32),
            pltpu.VMEM((q, GROUP_W), F32),
        ],
        compiler_params=_params(("parallel", "arbitrary")),
        name="ssd",
    )(p, p, ps, conv_w, conv_b, dt_bias_pad, a_log_pad, d_full, norm_g)


def _compress_kernel(x_ref, w1_ref, w2_ref, pe_ref, o_ref, ot_ref, xf_ref):
    s = x_ref.shape[0]
    n = s // CMP_STRIDE
    for sb in range(2):
        xf_ref[sb] = x_ref[:, sb * LANES:(sb + 1) * LANES].astype(F32)
    o_ref[...] = jnp.zeros_like(o_ref)
    ot_ref[...] = jnp.zeros_like(ot_ref)
    for sb in range(2):
        first = jnp.zeros((n, 2 * CMP_HIDDEN), F32)
        second = jnp.zeros((n, 2 * CMP_HIDDEN), F32)
        for l in range(CMP_STRIDE):
            xl = xf_ref[sb, pl.ds(l, n, stride=CMP_STRIDE), :]
            first = first + jnp.dot((xl + pe_ref[sb, l:l + 1, :]).astype(BF16), w1_ref[sb, l],
                                    preferred_element_type=F32)
            second = second + jnp.dot((xl + pe_ref[sb, CMP_STRIDE + l:CMP_STRIDE + l + 1, :]).astype(BF16),
                                      w1_ref[sb, CMP_STRIDE + l], preferred_element_type=F32)
        h = _silu(first + pltpu.roll(second, n - 1, 0))
        o = jnp.dot(h.astype(BF16), w2_ref[sb], preferred_element_type=F32)
        for e in range(2):
            oe = o[:, e * LANES:(e + 1) * LANES]
            o_ref[0, 2 * sb + e, 0:n, :] = oe.astype(o_ref.dtype)
            ot_ref[0, 2 * sb + e, :, 0:n] = oe.T.astype(ot_ref.dtype)


def _blockdiag2(a):
    z = jnp.zeros_like(a)
    return jnp.concatenate([jnp.concatenate([a, z], axis=-1), jnp.concatenate([z, a], axis=-1)], axis=-2)


def _compress(p, cmp_w1, cmp_w2, cmp_pe, b, s):
    nslot = 2 * NSA_KV_HEADS
    w1 = _blockdiag2(cmp_w1.astype(BF16).reshape(2, CMP_BLOCK, HEAD_DIM, CMP_HIDDEN))
    w2 = _blockdiag2(jnp.concatenate([cmp_w2, cmp_w2], axis=-1).astype(BF16))
    pe = jnp.concatenate([cmp_pe, cmp_pe], axis=-1).astype(F32)
    kw = 2 * NSA_KV_W
    return pl.pallas_call(
        _compress_kernel,
        grid=(b,),
        in_specs=[
            pl.BlockSpec((s, kw), lambda bi: (bi, OFF_CMP_KV // kw)),
            pl.BlockSpec(w1.shape, lambda bi: (0, 0, 0, 0)),
            pl.BlockSpec(w2.shape, lambda bi: (0, 0, 0)),
            pl.BlockSpec(pe.shape, lambda bi: (0, 0, 0)),
        ],
        out_specs=[pl.BlockSpec((1, nslot, N_CMP_PAD, LANES), lambda bi: (bi, 0, 0, 0)),
                   pl.BlockSpec((1, nslot, LANES, N_CMP_PAD), lambda bi: (bi, 0, 0, 0))],
        out_shape=[jax.ShapeDtypeStruct((b, nslot, N_CMP_PAD, LANES), BF16),
                   jax.ShapeDtypeStruct((b, nslot, LANES, N_CMP_PAD), BF16)],
        scratch_shapes=[pltpu.VMEM((2, s, LANES), F32)],
        compiler_params=_params(("parallel",)),
        name="nsa_compress",
    )(p, w1, w2, pe)


def _cmp_select_kernel(q_ref, kv_ref, vt_ref, gl_ref, gate_ref, fc_ref, ovt_ref, o_ref, ns_ref):
    t = T_ATT
    ncp = kv_ref.shape[2]
    qi = pl.program_id(1)
    t0 = qi * t
    lo = lax.broadcasted_iota(jnp.int32, (t, LANES), 1) < HEAD_DIM
    zero = jnp.zeros((t, LANES), BF16)
    start = pl.multiple_of(ncp - qi * (t // CMP_STRIDE), CMP_STRIDE)
    gl = gl_ref[...]

    nsel = SEL_BLOCK
    jrow = lax.broadcasted_iota(jnp.int32, (nsel, t), 0)
    cur = (t0 + lax.broadcasted_iota(jnp.int32, (nsel, t), 1)) // SEL_BLOCK
    forced = (jrow == 0) | (jrow == cur) | (jrow == cur - 1)
    past = jrow <= cur

    for g in range(NSA_KV_HEADS):
        kc = kv_ref[0, g]
        vct = vt_ref[0, NSA_KV_HEADS + g]
        psum = jnp.zeros((ncp, t), F32)
        outs = []

        def scores(r):
            h = g * NSA_REP + r
            qp = q_ref[:, (h // 2) * LANES:(h // 2 + 1) * LANES]
            qe = jnp.where(lo, qp, zero) if r % 2 == 0 else jnp.where(lo, zero, qp)
            return lax.dot_general(kc, qe, (((1,), (1,)), ((), ())), preferred_element_type=F32)

        queue = [scores(0)]
        for r in range(NSA_REP):
            h = g * NSA_REP + r
            if r + 1 < NSA_REP:
                queue.append(scores(r + 1))
            s = fc_ref[h, pl.ds(start, ncp), :] + queue.pop(0)
            m = jnp.maximum(jnp.max(s, axis=0, keepdims=True), 0.1 * NEG_INF)
            e = jnp.exp2(s - m)
            pr = e * (1.0 / jnp.maximum(jnp.sum(e, axis=0, keepdims=True), TINY))
            psum = psum + pr
            oc = jnp.dot(vct, pr.astype(BF16), preferred_element_type=F32)
            outs.append(oc.T * jax.nn.sigmoid(gl[:, PS_GATE_LANE + h:PS_GATE_LANE + h + 1]))
        for hp in range(NSA_REP // 2):
            ls = slice((g * NSA_REP // 2 + hp) * LANES, (g * NSA_REP // 2 + hp + 1) * LANES)
            o = jnp.where(lo, outs[2 * hp], outs[2 * hp + 1])
            o_ref[:, ls] = (o * _silu(gate_ref[:, ls].astype(F32))).astype(o_ref.dtype)

        imp_t = jnp.zeros((nsel, t), F32)
        rest = psum
        for _ in range(N_SPLIT):
            piece = rest.astype(BF16)
            rest = rest - piece.astype(F32)
            imp_t = imp_t + jnp.dot(ovt_ref[...], piece, preferred_element_type=F32)
        imp_t = jnp.where(past, jnp.where(forced, SEL_FORCE, imp_t), -SEL_FORCE)
        sub = 8
        sub_row = lax.broadcasted_iota(jnp.int32, (sub, t), 0)
        rows = [imp_t[k * sub:(k + 1) * sub] for k in range(nsel // sub)]
        rank = [jnp.zeros((sub, t), F32) for _ in rows]
        for i in range(nsel):
            bi = imp_t[i:i + 1, :]
            for k, x in enumerate(rows):
                if k * sub > i:
                    ahead = jnp.where(bi >= x, 1.0, 0.0)
                elif k * sub + sub - 1 <= i:
                    ahead = jnp.where(bi > x, 1.0, 0.0)
                else:
                    ahead = jnp.where(sub_row > i - k * sub, jnp.where(bi >= x, 1.0, 0.0),
                                      jnp.where(bi > x, 1.0, 0.0))
                rank[k] = rank[k] + ahead
        rank = jnp.concatenate(rank, axis=0)
        notsel = jnp.where((rank < float(SEL_TOPK)) & past, 0.0, 1.0)
        ns2 = jnp.concatenate([notsel, notsel], axis=0).T
        ns_ref[:, g * LANES:(g + 1) * LANES] = ns2.astype(ns_ref.dtype)


def _cmp_select(p, ps, kvc, kvc_t, fc, overlap_t, b, s):
    t = T_ATT
    nq = s // t
    w = GROUP_W
    ncp = kvc.shape[2]
    return pl.pallas_call(
        _cmp_select_kernel,
        grid=(b, nq),
        in_specs=[
            pl.BlockSpec((t, w), lambda bi, qi: (bi * nq + qi, OFF_NSA_Q // w)),
            pl.BlockSpec((1, 2 * NSA_KV_HEADS, ncp, LANES), lambda bi, qi: (bi, 0, 0, 0)),
            pl.BlockSpec((1, 2 * NSA_KV_HEADS, LANES, ncp), lambda bi, qi: (bi, 0, 0, 0)),
            pl.BlockSpec((t, LANES), lambda bi, qi: (bi * nq + qi, 0)),
            pl.BlockSpec((t, w), lambda bi, qi: (bi * nq + qi, OFF_NSA_G // w)),
            pl.BlockSpec((NSA_HEADS, 2 * ncp, t), lambda bi, qi: (0, 0, 0)),
            pl.BlockSpec((SEL_BLOCK, ncp), lambda bi, qi: (0, 0)),
        ],
        out_specs=[
            pl.BlockSpec((t, w), lambda bi, qi: (bi * nq + qi, 0)),
            pl.BlockSpec((t, NSA_KV_HEADS * LANES), lambda bi, qi: (bi * nq + qi, 0)),
        ],
        out_shape=[jax.ShapeDtypeStruct((b * s, w), BF16),
                   jax.ShapeDtypeStruct((b * s, NSA_KV_HEADS * LANES), BF16)],
        compiler_params=_params(("parallel", "arbitrary")),
        name="nsa_cmp_select",
    )(p, kvc, kvc_t, ps, p, fc, overlap_t)


def _sel_attn_kernel(tab_ref, q_ref, ns_ref, k_ref, v_ref, gl_ref, gate_ref, bs_ref, o_ref,
                     qa_ref, kk_ref, vt_ref, m_ref, acc_ref, pre_ref, *, t, tk, ahead):
    qi = pl.program_id(1)
    lo_q, hi_q = _half_masks(t)
    lo = lax.broadcasted_iota(jnp.int32, (t, LANES), 1) < HEAD_DIM
    n_near = t // tk + 1

    @pl.when(qi == 0)
    def _():
        lane = lax.broadcasted_iota(jnp.int32, (tk, LANES), 1)
        krow = lax.broadcasted_iota(jnp.int32, (tk, LANES), 0)
        lo_f = jnp.where(lane < HEAD_DIM, 1.0, 0.0)
        hi_f = 1.0 - lo_f
        lo_k, hi_k = lo_f.astype(BF16), hi_f.astype(BF16)

        def merge(j, c):
            ks = pl.multiple_of(j * tk, tk)
            rows = pl.ds(ks, tk)
            hot = jnp.where((lane % HEAD_DIM) == (ks + krow) // SEL_BLOCK, NEG_INF, 0.0)
            hot_lo, hot_hi = (hot * lo_f).astype(BF16), (hot * hi_f).astype(BF16)
            k01 = k_ref[rows, :]
            k10 = _swap_halves(k01)
            kk_ref[0, rows, :] = k01 * lo_k + hot_hi
            kk_ref[1, rows, :] = hot_lo + k10 * hi_k
            kk_ref[2, rows, :] = k10 * lo_k + hot_hi
            kk_ref[3, rows, :] = hot_lo + k01 * hi_k
            _store_vt_groups(vt_ref, rows, v_ref[rows, :])
            return c

        lax.fori_loop(0, k_ref.shape[0] // tk, merge, 0)

    for h in range(NSA_HEADS):
        g = h // NSA_REP
        qp = q_ref[:, (h // 2) * LANES:(h // 2 + 1) * LANES]
        ns = ns_ref[:, g * LANES:(g + 1) * LANES]
        qa = qp * lo_q + ns * hi_q if h % 2 == 0 else ns * lo_q + qp * hi_q
        qa_ref[h] = qa.astype(F32).T.astype(BF16)
    m_ref[...] = jnp.full(m_ref.shape, NEG_INF, F32)
    acc_ref[...] = jnp.zeros_like(acc_ref)

    kv = lambda h: 2 * (h // NSA_REP) + h % 2

    def rows_of(j):
        return pl.ds(pl.multiple_of(j * tk, tk), tk)

    def scores(j, h):
        return jnp.dot(kk_ref[kv(h), rows_of(j), :], qa_ref[h], preferred_element_type=F32)

    def tile(j, near, j_next):
        rows = rows_of(j)
        queue = [pre_ref[i] for i in range(ahead)]
        for h in range(NSA_HEADS):
            if h + ahead < NSA_HEADS:
                queue.append(scores(j, h + ahead))
            elif j_next is not None:
                pre_ref[h + ahead - NSA_HEADS] = scores(j_next, h + ahead - NSA_HEADS)
            s = queue.pop(0)
            m_old = m_ref[h]
            if near is None:
                far = tab_ref[(REL_BUCKETS - 1) * NSA_HEADS + h] * LOG2E
                m_new = jnp.maximum(m_old, jnp.max(s, axis=0, keepdims=True) + far)
                p = jnp.exp2(s - (m_new - far)).astype(BF16)
            else:
                s = bs_ref[h, near * tk:(near + 1) * tk, :] + s
                m_new = jnp.maximum(m_old, jnp.max(s, axis=0, keepdims=True))
                p = jnp.exp2(s - m_new).astype(BF16)
            m_ref[h] = m_new
            vr = _vt_rows(h)
            acc_ref[h, vr] = jnp.exp2(m_old - m_new) * acc_ref[h, vr] + jnp.dot(vt_ref[kv(h), vr, rows], p,
                                                                                preferred_element_type=F32)

    jd = (qi * t) // tk
    for i in range(ahead):
        pre_ref[i] = scores(0, i)

    def body(j, c):
        tile(j, None, j + 1)
        return c

    lax.fori_loop(0, jnp.maximum(jd - 1, 0), body, 0)

    @pl.when(qi > 0)
    def _():
        tile(jd - 1, 0, jd)

    for dj in range(n_near - 1):
        tile(jd + dj, 1 + dj, jd + dj + 1 if dj + 2 < n_near else None)

    gl = gl_ref[...]
    for hp in range(NSA_HEADS // 2):
        ls = slice(hp * LANES, (hp + 1) * LANES)
        c0 = PS_GATE_LANE + NSA_HEADS + 2 * hp
        branch = jnp.where(lo, jax.nn.sigmoid(gl[:, c0:c0 + 1]), jax.nn.sigmoid(gl[:, c0 + 1:c0 + 2]))
        o_ref[:, ls] = (_finish_pair(acc_ref, hp) * branch * _silu(gate_ref[:, ls].astype(F32))).astype(o_ref.dtype)


def _sel_attn(tab_flat, p, ps, notsel, bsel, b, s, t=512, tk=256, ahead=2):
    nq = s // t
    w = GROUP_W
    kw = NSA_KV_HEADS * LANES
    n_near = t // tk + 1
    return pl.pallas_call(
        functools.partial(_sel_attn_kernel, t=t, tk=tk, ahead=ahead),
        grid=(b, nq),
        in_specs=[
            pl.BlockSpec(memory_space=pltpu.SMEM),
            pl.BlockSpec((t, w), lambda bi, qi: (bi * nq + qi, OFF_NSA_Q // w)),
            pl.BlockSpec((t, kw), lambda bi, qi: (bi * nq + qi, 0)),
            pl.BlockSpec((s, NSA_KV_W), lambda bi, qi: (bi, OFF_SEL_K // NSA_KV_W)),
            pl.BlockSpec((s, NSA_KV_W), lambda bi, qi: (bi, OFF_SEL_V // NSA_KV_W)),
            pl.BlockSpec((t, LANES), lambda bi, qi: (bi * nq + qi, 0)),
            pl.BlockSpec((t, w), lambda bi, qi: (bi * nq + qi, OFF_NSA_G // w)),
            pl.BlockSpec((NSA_HEADS, n_near * tk, t), lambda bi, qi: (0, 0, 0), pipeline_mode=pl.Buffered(1)),
        ],
        out_specs=pl.BlockSpec((t, w), lambda bi, qi: (bi * nq + qi, 0)),
        out_shape=jax.ShapeDtypeStruct((b * s, w), BF16),
        scratch_shapes=[
            pltpu.VMEM((NSA_HEADS, LANES, t), BF16),
            pltpu.VMEM((2 * NSA_KV_HEADS, s, LANES), BF16),
            pltpu.VMEM((2 * NSA_KV_HEADS, LANES, s), BF16),
            pltpu.VMEM((NSA_HEADS, 1, t), F32),
            pltpu.VMEM((NSA_HEADS, LANES, t), F32),
            pltpu.VMEM((ahead, tk, t), F32),
        ],
        compiler_params=_params(("parallel", "arbitrary")),
        name="nsa_sel_attn",
    )(tab_flat, p, notsel, p, p, ps, p, bsel)


def _win_attn_kernel(q_ref, k_ref, v_ref, gl_ref, gate_ref, bw_ref, o_ref, qe_ref, kk_ref, vt_ref, m_ref, acc_ref,
                     pre_ref, *, t, tk, ahead):
    qi = pl.program_id(1)
    lo_q, hi_q = _half_masks(t)
    lo = lax.broadcasted_iota(jnp.int32, (t, LANES), 1) < HEAD_DIM
    n_before = WINDOW // tk
    n_tiles = n_before + t // tk

    @pl.when(qi == 0)
    def _():
        lo_k, hi_k = _half_masks(tk)

        def merge(j, c):
            rows = pl.ds(pl.multiple_of(j * tk, tk), tk)
            k01 = k_ref[rows, :]
            k10 = _swap_halves(k01)
            kk_ref[0, rows, :] = k01 * lo_k + k10 * hi_k
            kk_ref[1, rows, :] = k10 * lo_k + k01 * hi_k
            _store_vt_groups(vt_ref, rows, v_ref[rows, :])
            return c

        lax.fori_loop(0, v_ref.shape[0] // tk, merge, 0)

    for h in range(NSA_HEADS):
        qp = q_ref[:, (h // 2) * LANES:(h // 2 + 1) * LANES]
        qe_ref[h] = qp * lo_q if h % 2 == 0 else qp * hi_q
    m_ref[...] = jnp.full(m_ref.shape, NEG_INF, F32)
    acc_ref[...] = jnp.zeros_like(acc_ref)

    def rows_of(w):
        return pl.ds(pl.multiple_of((qi * (t // tk) - n_before + w) * tk, tk), tk)

    def scores(w, h):
        return lax.dot_general(kk_ref[h // NSA_REP, rows_of(w), :], qe_ref[h], (((1,), (1,)), ((), ())),
                               preferred_element_type=F32)

    def tile(w):
        rows = rows_of(w)
        queue = [pre_ref[i] for i in range(ahead)]
        for h in range(NSA_HEADS):
            if h + ahead < NSA_HEADS:
                queue.append(scores(w, h + ahead))
            elif w + 1 < n_tiles:
                pre_ref[h + ahead - NSA_HEADS] = scores(w + 1, h + ahead - NSA_HEADS)
            s = bw_ref[h, w * tk:(w + 1) * tk, :] + queue.pop(0)
            m_old = m_ref[h]
            m_new = jnp.maximum(m_old, jnp.max(s, axis=0, keepdims=True))
            m_ref[h] = m_new
            p = jnp.exp2(s - m_new).astype(BF16)
            vr = _vt_rows(h)
            acc_ref[h, vr] = jnp.exp2(m_old - m_new) * acc_ref[h, vr] + jnp.dot(
                vt_ref[2 * (h // NSA_REP) + h % 2, vr, rows], p, preferred_element_type=F32)

    first = jnp.maximum(n_before - qi * (t // tk), 0)
    for i in range(ahead):
        pre_ref[i] = scores(first, i)
    per_q = t // tk
    first_qi = [-(-(n_before - w) // per_q) if w < n_before else 0 for w in range(n_tiles)]
    for need in sorted(set(first_qi), reverse=True):
        group = [w for w in range(n_tiles) if first_qi[w] == need]

        def run(group=group):
            for w in group:
                tile(w)

        if need > 0:
            pl.when(qi >= need)(run)
        else:
            run()

    gl = gl_ref[...]
    for hp in range(NSA_HEADS // 2):
        ls = slice(hp * LANES, (hp + 1) * LANES)
        c0 = PS_GATE_LANE + 2 * NSA_HEADS + 2 * hp
        branch = jnp.where(lo, jax.nn.sigmoid(gl[:, c0:c0 + 1]), jax.nn.sigmoid(gl[:, c0 + 1:c0 + 2]))
        o_ref[:, ls] = (_finish_pair(acc_ref, hp) * branch * _silu(gate_ref[:, ls].astype(F32))).astype(o_ref.dtype)


def _win_attn(p, ps, bwin, b, s, t=WIN_T, tk=WIN_TK, ahead=2):
    assert t % tk == 0 and WINDOW % tk == 0
    nq = s // t
    w = GROUP_W
    return pl.pallas_call(
        functools.partial(_win_attn_kernel, t=t, tk=tk, ahead=ahead),
        grid=(b, nq),
        in_specs=[
            pl.BlockSpec((t, w), lambda bi, qi: (bi * nq + qi, OFF_NSA_Q // w)),
            pl.BlockSpec((s, NSA_KV_W), lambda bi, qi: (bi, OFF_WIN_K // NSA_KV_W)),
            pl.BlockSpec((s, NSA_KV_W), lambda bi, qi: (bi, OFF_WIN_V // NSA_KV_W)),
            pl.BlockSpec((t, LANES), lambda bi, qi: (bi * nq + qi, 0)),
            pl.BlockSpec((t, w), lambda bi, qi: (bi * nq + qi, OFF_NSA_G // w)),
            pl.BlockSpec((NSA_HEADS, WINDOW + t, t), lambda bi, qi: (0, 0, 0), pipeline_mode=pl.Buffered(1)),
        ],
        out_specs=pl.BlockSpec((t, w), lambda bi, qi: (bi * nq + qi, 0)),
        out_shape=jax.ShapeDtypeStruct((b * s, w), BF16),
        scratch_shapes=[
            pltpu.VMEM((NSA_HEADS, t, LANES), BF16),
            pltpu.VMEM((NSA_KV_HEADS, s, LANES), BF16),
            pltpu.VMEM((2 * NSA_KV_HEADS, LANES, s), BF16),
            pltpu.VMEM((NSA_HEADS, 1, t), F32),
            pltpu.VMEM((NSA_HEADS, LANES, t), F32),
            pltpu.VMEM((ahead, tk, t), F32),
        ],
        compiler_params=_params(("parallel", "arbitrary")),
        name="nsa_win_attn",
    )(p, p, p, ps, p, bwin)


def _mem_attn_kernel(q_ref, gate_ref, kv_ref, o_ref):
    scale = MEM_HEAD_DIM ** -0.5
    for h in range(MEM_HEADS):
        ls = slice(h * LANES, (h + 1) * LANES)
        k = kv_ref[:, ls]
        v = kv_ref[:, GROUP_W + h * LANES:GROUP_W + (h + 1) * LANES]
        s = lax.dot_general(q_ref[:, ls], k, (((1,), (1,)), ((), ())), preferred_element_type=F32) * scale
        m = jnp.max(s, axis=1, keepdims=True)
        e = jnp.exp(s - m)
        l = jnp.sum(e, axis=1, keepdims=True)
        o = jnp.dot(e.astype(BF16), v, preferred_element_type=F32) / l
        o_ref[:, ls] = (o * _silu(gate_ref[:, ls].astype(F32))).astype(o_ref.dtype)


def _mem_attn(p, mem_kv, b, s, t=512):
    nq = s // t
    w = GROUP_W
    m = mem_kv.shape[0] // b
    return pl.pallas_call(
        _mem_attn_kernel,
        grid=(b, nq),
        in_specs=[
            pl.BlockSpec((t, w), lambda bi, qi: (bi * nq + qi, OFF_MEM_Q // w)),
            pl.BlockSpec((t, w), lambda bi, qi: (bi * nq + qi, OFF_MEM_G // w)),
            pl.BlockSpec((m, 2 * w), lambda bi, qi: (bi, 0)),
        ],
        out_specs=pl.BlockSpec((t, w), lambda bi, qi: (bi * nq + qi, 0)),
        out_shape=jax.ShapeDtypeStruct((b * s, w), BF16),
        compiler_params=_params(("parallel", "arbitrary")),
        name="mem_attn",
    )(p, p, mem_kv)


def _out_proj_kernel(x_ref, of_ref, os_ref, oc_ref, osel_ref, ow_ref, om_ref, w_ref, g_ref, o_ref, *, final):
    w = GROUP_W
    nsa = (oc_ref[...].astype(F32) + osel_ref[...].astype(F32) + ow_ref[...].astype(F32)).astype(BF16)
    acc = x_ref[...]
    for i, part in enumerate((of_ref[...], os_ref[...], nsa, om_ref[...])):
        acc = acc + jnp.dot(part, w_ref[i * w:(i + 1) * w, :], preferred_element_type=F32)
    if final:
        ms = jnp.mean(acc * acc, axis=-1, keepdims=True)
        acc = (acc * lax.rsqrt(ms + EPS)) * g_ref[...]
    o_ref[...] = acc


def _out_proj(x2d, parts, w_out, g, final, tm=512):
    n, d = x2d.shape
    w = GROUP_W
    part_spec = pl.BlockSpec((tm, w), lambda i: (i, 0))
    return pl.pallas_call(
        functools.partial(_out_proj_kernel, final=final),
        grid=(n // tm,),
        in_specs=[pl.BlockSpec((tm, d), lambda i: (i, 0))] + [part_spec] * 6 + [
            pl.BlockSpec((4 * w, d), lambda i: (0, 0)),
            pl.BlockSpec((1, d), lambda i: (0, 0)),
        ],
        out_specs=pl.BlockSpec((tm, d), lambda i: (i, 0)),
        out_shape=jax.ShapeDtypeStruct((n, d), F32),
        compiler_params=_params(("parallel",)),
        name="out_proj",
    )(x2d, *parts, w_out, g.reshape(1, d))


def _pack_in_proj(w_in_l):
    fox, ssm, nsa, mem = 0, FOX_COLS, FOX_COLS + SSM_COLS, FOX_COLS + SSM_COLS + NSA_COLS
    w = GROUP_W
    q_scale = HEAD_DIM ** -0.5 * LOG2E
    cols = lambda a, n: w_in_l[:, a:a + n]
    kv = lambda slot: nsa + w + slot * NSA_KV_W
    main = ([cols(fox, w) * q_scale, cols(fox + w, 3 * w)]
            + [cols(ssm + w, SSM_CONV_DIM), cols(ssm, w)]
            + [cols(nsa, w) * q_scale, cols(nsa + w + 6 * NSA_KV_W + 3 * NSA_HEADS, w)]
            + [cols(mem, 2 * w)]
            + [cols(kv(2), 4 * NSA_KV_W)]
            + [cols(kv(0), 2 * NSA_KV_W)])
    w_main = jnp.concatenate(main, axis=1).astype(BF16)
    assert w_main.shape[1] == P_COLS
    small = ([cols(fox + 4 * w, FOX_HEADS)] * N_SPLIT
             + [cols(ssm + w + SSM_CONV_DIM, SSM_HEADS)]
             + [cols(nsa + w + 6 * NSA_KV_W, 3 * NSA_HEADS)])
    assert N_SPLIT * FOX_HEADS == PS_DT_LANE and PS_DT_LANE + SSM_HEADS == PS_GATE_LANE
    used = PS_GATE_LANE + 3 * NSA_HEADS
    small.append(jnp.zeros((w_in_l.shape[0], PS_COLS - used), w_in_l.dtype))
    return w_main, jnp.concatenate(small, axis=1).astype(BF16)


def _pad_lanes(v, first=0):
    return jnp.pad(v.astype(F32), (first, LANES - first - v.shape[0])).reshape(1, LANES)


def _trunk(x, mem, norm_g, w_in, fox_f_bias, ssm_conv_w, ssm_conv_b, ssm_dt_bias, ssm_a_log, ssm_d,
           ssm_norm_g, nsa_cmp_pe, nsa_cmp_w1, nsa_cmp_w2, rel_bias_table, mem_norm_g, w_mem_kv, w_out,
           final_norm_g):
    b, s, d = x.shape
    depth = w_in.shape[0]
    n = b * s
    m_tok = mem.shape[1]
    n_cmp = (s - CMP_BLOCK) // CMP_STRIDE + 1
    n_rows = s // CMP_STRIDE
    assert s % 512 == 0 and s // SEL_BLOCK <= HEAD_DIM and n_rows <= N_CMP_PAD and d == D_MODEL

    tab_flat = rel_bias_table.astype(F32).reshape(-1)
    unbounded = 1 << 30
    bwin = _t5_table(tab_flat, WINDOW + WIN_T, WIN_T, 1, WINDOW, WINDOW)
    bsel = _t5_table(tab_flat, (SEL_T // SEL_TK + 1) * SEL_TK, SEL_T, 1, SEL_TK, unbounded)
    fcmp = _t5_table(tab_flat, 2 * N_CMP_PAD, T_ATT, CMP_STRIDE, CMP_STRIDE * N_CMP_PAD - (CMP_BLOCK - 1), unbounded)

    cs = np.arange(N_CMP_PAD)[None, :] * CMP_STRIDE
    js = np.arange(SEL_BLOCK)[:, None] * SEL_BLOCK
    overlap_t = ((cs < js + SEL_BLOCK) & (cs + CMP_BLOCK > js) & (np.arange(N_CMP_PAD)[None, :] < n_cmp)
                 & (np.arange(SEL_BLOCK)[:, None] < s // SEL_BLOCK)).astype(np.float32)
    overlap_t = jnp.asarray(overlap_t, BF16)

    x2d = x.reshape(n, d)
    mem2d = mem.reshape(b * m_tok, d)
    half = CMP_STRIDE * HEAD_DIM
    for l in range(depth):
        w_main, w_small = _pack_in_proj(w_in[l])
        p, ps = _norm_proj(x2d, norm_g[l], w_main, w_small)

        qaug, kaug = _fox_cumsum(ps, _pad_lanes(jnp.tile(fox_f_bias[l], N_SPLIT)), b, s)
        o_fox = _fox_attn(p, qaug, kaug, b, s)

        o_ssd = _ssd(p, ps, ssm_conv_w[l].astype(F32), ssm_conv_b[l].reshape(1, -1).astype(F32),
                     _pad_lanes(ssm_dt_bias[l], PS_DT_LANE), _pad_lanes(ssm_a_log[l], PS_DT_LANE),
                     jnp.repeat(ssm_d[l].astype(F32), HEAD_DIM).reshape(1, GROUP_W),
                     ssm_norm_g[l].reshape(1, GROUP_W).astype(F32), b, s)

        kv_cmp, kv_cmp_t = _compress(p, nsa_cmp_w1[l], nsa_cmp_w2[l], nsa_cmp_pe[l], b, s)
        o_cmp, notsel = _cmp_select(p, ps, kv_cmp, kv_cmp_t, fcmp, overlap_t, b, s)
        o_sel = _sel_attn(tab_flat, p, ps, notsel, bsel, b, s)
        o_win = _win_attn(p, ps, bwin, b, s)

        w_kv = w_mem_kv[l].astype(BF16)
        mem_kv, _ = _norm_proj(mem2d, mem_norm_g[l], w_kv, w_kv[:, :LANES], tm=min(512, b * m_tok))
        o_mem = _mem_attn(p, mem_kv, b, s)

        x2d = _out_proj(x2d, (o_fox, o_ssd, o_cmp, o_sel, o_win, o_mem), w_out[l].astype(BF16),
                        final_norm_g, final=(l == depth - 1))
    return x2d.reshape(b, s, d)


def kernel(x, mem, norm_g, w_in, fox_f_bias, ssm_conv_w, ssm_conv_b, ssm_dt_bias, ssm_a_log, ssm_d, ssm_norm_g,
           nsa_cmp_pe, nsa_cmp_w1, nsa_cmp_w2, rel_bias_table, mem_norm_g, w_mem_kv, w_out, final_norm_g):
    return _trunk(x, mem, norm_g, w_in, fox_f_bias, ssm_conv_w, ssm_conv_b, ssm_dt_bias, ssm_a_log, ssm_d,
                  ssm_norm_g, nsa_cmp_pe, nsa_cmp_w1, nsa_cmp_w2, rel_bias_table, mem_norm_g, w_mem_kv, w_out,
                  final_norm_g)
```

```python
import functools
import math

import numpy as np
import jax
import jax.numpy as jnp
from jax import lax
from jax.experimental import pallas as pl
from jax.experimental.pallas import tpu as pltpu

F32 = jnp.float32
BF16 = jnp.bfloat16

D_MODEL = 1024
GROUP_W = 512
HEAD_DIM = 64
EPS = 1e-6
NEG_INF = -1e30
TINY = 1e-30
LOG2E = math.log2(math.e)

FOX_HEADS = 8
SSM_HEADS = 8
SSM_STATE = 128
SSM_GROUPS = 2
SSM_CONV = 4
SSM_CHUNK = 128
SSM_CONV_DIM = GROUP_W + 2 * SSM_GROUPS * SSM_STATE

NSA_HEADS = 8
NSA_KV_HEADS = 2
NSA_REP = NSA_HEADS // NSA_KV_HEADS
NSA_KV_W = NSA_KV_HEADS * HEAD_DIM
CMP_BLOCK = 32
CMP_STRIDE = 16
CMP_HIDDEN = 2 * HEAD_DIM
SEL_BLOCK = 64
SEL_TOPK = 16
WINDOW = 512
SEL_FORCE = 1e9

MEM_HEADS = 4
MEM_HEAD_DIM = GROUP_W // MEM_HEADS
REL_BUCKETS = 32
REL_MAX_DIST = 128

FOX_COLS = 4 * GROUP_W + FOX_HEADS
SSM_COLS = GROUP_W + SSM_CONV_DIM + SSM_HEADS
NSA_COLS = 2 * GROUP_W + 6 * NSA_KV_W + 3 * NSA_HEADS
MEM_COLS = 2 * GROUP_W

LANES = 128
VMEM_LIMIT = 56 * 1024 * 1024

OFF_FOX_Q, OFF_FOX_K, OFF_FOX_V, OFF_FOX_G = 0, 512, 1024, 1536
OFF_SSM_XBC, OFF_SSM_Z = 2048, 3072
OFF_NSA_Q, OFF_NSA_G = 3584, 4096
OFF_MEM_Q, OFF_MEM_G = 4608, 5120
OFF_SEL_K, OFF_SEL_V, OFF_WIN_K, OFF_WIN_V, OFF_CMP_KV = 5632, 5760, 5888, 6016, 6144
P_COLS = 6400
PS_COLS = LANES
PS_DT_LANE = 24
PS_GATE_LANE = 32

T_ATT = 256
SEL_T, SEL_TK = 512, 256
WIN_T, WIN_TK = 512, 256
N_CMP_PAD = 256


def _params(sem):
    return pltpu.CompilerParams(dimension_semantics=sem, vmem_limit_bytes=VMEM_LIMIT)


def _t5_bucket_np(dist):
    n = np.maximum(dist, 0)
    max_exact = REL_BUCKETS // 2
    nf = np.maximum(n, 1).astype(np.float32)
    large = max_exact + (np.log(nf / np.float32(max_exact)) / np.float32(math.log(REL_MAX_DIST / max_exact))
                         * np.float32(REL_BUCKETS - max_exact)).astype(np.int32)
    large = np.minimum(large, REL_BUCKETS - 1)
    return np.where(n < max_exact, n, large).astype(np.int32)


def _silu(x):
    h = 0.5 * x
    return h + h * jnp.tanh(h)


def _norm_proj_kernel(x_ref, g_ref, *refs, chunk):
    x = x_ref[...]
    ms = jnp.mean(x * x, axis=-1, keepdims=True)
    h = ((x * lax.rsqrt(ms + EPS)) * g_ref[...]).astype(BF16)
    n_out = len(refs) // 2
    for w_ref, o_ref in zip(refs[:n_out], refs[n_out:]):
        ncol = o_ref.shape[1]
        for c0 in range(0, ncol, chunk):
            c1 = min(c0 + chunk, ncol)
            o_ref[:, c0:c1] = jnp.dot(h, w_ref[:, c0:c1], preferred_element_type=F32).astype(o_ref.dtype)


def _norm_proj(x2d, g, weights, out_dtypes, tm=512):
    n, d = x2d.shape
    return pl.pallas_call(
        functools.partial(_norm_proj_kernel, chunk=512),
        grid=(n // tm,),
        in_specs=[pl.BlockSpec((tm, d), lambda i: (i, 0)), pl.BlockSpec((1, d), lambda i: (0, 0))]
        + [pl.BlockSpec(w.shape, lambda i: (0, 0)) for w in weights],
        out_specs=[pl.BlockSpec((tm, w.shape[1]), lambda i: (i, 0)) for w in weights],
        out_shape=[jax.ShapeDtypeStruct((n, w.shape[1]), dt) for w, dt in zip(weights, out_dtypes)],
        compiler_params=_params(("parallel",)),
        name="norm_proj",
    )(x2d, g.reshape(1, d), *weights)


def _t5_table_kernel(tab_ref, bucket_ref, o_ref, *, stride, off, limit):
    h = pl.program_id(0)
    rows, cols = o_ref.shape[1], o_ref.shape[2]
    bucket = bucket_ref[...]
    near = jnp.zeros(bucket.shape, F32)
    for b in range(REL_BUCKETS):
        near = jnp.where(bucket == b, tab_ref[b * NSA_HEADS + h] * LOG2E, near)
    far = tab_ref[(REL_BUCKETS - 1) * NSA_HEADS + h] * LOG2E
    base = jnp.concatenate([near] * (rows // near.shape[0]), axis=0)
    rolled = pltpu.roll(base, 0, 1, stride=stride, stride_axis=0)
    d = (lax.broadcasted_iota(jnp.int32, (rows, cols), 1) - stride * lax.broadcasted_iota(jnp.int32, (rows, cols), 0)
         + off)
    o_ref[0] = jnp.where((d < 0) | (d >= limit), NEG_INF, jnp.where(d >= REL_MAX_DIST, far, rolled))


def _t5_table(tab_flat, rows, cols, stride, off, limit):
    assert cols >= 2 * REL_MAX_DIST and rows % 8 == 0
    k = (np.arange(cols) + off) % cols
    bucket = np.broadcast_to(np.where(k < REL_MAX_DIST, _t5_bucket_np(k), -1).astype(np.int32), (8, cols))
    return pl.pallas_call(
        functools.partial(_t5_table_kernel, stride=stride, off=off, limit=limit),
        grid=(NSA_HEADS,),
        in_specs=[
            pl.BlockSpec(memory_space=pltpu.SMEM),
            pl.BlockSpec((8, cols), lambda h: (0, 0)),
        ],
        out_specs=pl.BlockSpec((1, rows, cols), lambda h: (h, 0, 0)),
        out_shape=jax.ShapeDtypeStruct((NSA_HEADS, rows, cols), F32),
        compiler_params=_params(("arbitrary",)),
        name="t5_table",
    )(tab_flat, jnp.asarray(bucket))


def _tri_lower(n):
    r = lax.broadcasted_iota(jnp.int32, (n, n), 0)
    c = lax.broadcasted_iota(jnp.int32, (n, n), 1)
    return (r >= c).astype(F32)


N_SPLIT = 3


def _fox_aug_lane(h, i):
    return LANES * (h // 2) + (HEAD_DIM if h % 2 == 0 else 0) + i


def _fox_aug_consts():
    pq = np.zeros((LANES, GROUP_W), np.float32)
    pk = np.zeros((LANES, GROUP_W), np.float32)
    oq = np.zeros((1, GROUP_W), np.float32)
    ok = np.zeros((1, GROUP_W), np.float32)
    for h in range(FOX_HEADS):
        for i in range(N_SPLIT):
            pq[i * FOX_HEADS + h, _fox_aug_lane(h, i)] = 1.0
            pk[i * FOX_HEADS + h, _fox_aug_lane(h, N_SPLIT + i)] = -1.0
            oq[0, _fox_aug_lane(h, N_SPLIT + i)] = 1.0
            ok[0, _fox_aug_lane(h, i)] = 1.0
    return pq, pk, oq, ok


def _fox_cumsum_kernel(f_ref, b_ref, pq_ref, pk_ref, oq_ref, ok_ref, qa_ref, ka_ref, carry_ref, *, ts):
    @pl.when(pl.program_id(1) == 0)
    def _():
        carry_ref[...] = jnp.zeros_like(carry_ref)

    z = f_ref[...] + b_ref[...]
    logf = (jnp.minimum(z, 0.0) - jnp.log(1.0 + jnp.exp(-jnp.abs(z)))) * LOG2E
    tri = _tri_lower(LANES).astype(BF16)
    group = lax.broadcasted_iota(jnp.int32, (LANES, LANES), 1) // FOX_HEADS
    carry = carry_ref[...]
    for c in range(ts // LANES):
        rows = slice(c * LANES, (c + 1) * LANES)
        cs, rest = carry, logf[rows]
        for _ in range(N_SPLIT):
            piece = rest.astype(BF16)
            rest = rest - piece.astype(F32)
            cs = cs + jnp.dot(tri, piece, preferred_element_type=F32)
        carry = cs[LANES - 1:LANES, :]
        cat, rest = None, cs
        for i in range(N_SPLIT):
            piece = rest.astype(BF16).astype(F32)
            rest = rest - piece
            cat = piece if cat is None else jnp.where(group == i, piece, cat)
        cat = cat.astype(BF16)
        qa_ref[rows, :] = (oq_ref[...] + jnp.dot(cat, pq_ref[...], preferred_element_type=F32)).astype(BF16)
        ka_ref[rows, :] = (ok_ref[...] + jnp.dot(cat, pk_ref[...], preferred_element_type=F32)).astype(BF16)
    carry_ref[...] = carry


def _fox_cumsum(ps, f_bias_pad, b, s, ts=512):
    ns = s // ts
    pq, pk, oq, ok = _fox_aug_consts()
    const = lambda bi, si: (0, 0)
    return pl.pallas_call(
        functools.partial(_fox_cumsum_kernel, ts=ts),
        grid=(b, ns),
        in_specs=[
            pl.BlockSpec((ts, LANES), lambda bi, si: (bi * ns + si, 0)),
            pl.BlockSpec((1, LANES), const),
            pl.BlockSpec(pq.shape, const),
            pl.BlockSpec(pk.shape, const),
            pl.BlockSpec(oq.shape, const),
            pl.BlockSpec(ok.shape, const),
        ],
        out_specs=[
            pl.BlockSpec((ts, GROUP_W), lambda bi, si: (bi * ns + si, 0)),
            pl.BlockSpec((ts, GROUP_W), lambda bi, si: (bi * ns + si, 0)),
        ],
        out_shape=[jax.ShapeDtypeStruct((b * s, GROUP_W), BF16), jax.ShapeDtypeStruct((b * s, GROUP_W), BF16)],
        scratch_shapes=[pltpu.VMEM((1, LANES), F32)],
        compiler_params=_params(("parallel", "arbitrary")),
        name="fox_cumsum",
    )(ps, f_bias_pad, jnp.asarray(pq, BF16), jnp.asarray(pk, BF16), jnp.asarray(oq), jnp.asarray(ok))


ONES_ROWS = 16


def _vt_rows(h):
    return slice(0, HEAD_DIM + ONES_ROWS) if h % 2 == 0 else slice(HEAD_DIM - ONES_ROWS, LANES)


def _store_vt(vt_ref, i, rows, v_pair):
    vt = v_pair.astype(F32).T.astype(BF16)
    ones = jnp.ones((HEAD_DIM, v_pair.shape[0]), BF16)
    vt_ref[i, 0:HEAD_DIM, rows] = vt[0:HEAD_DIM]
    vt_ref[i, HEAD_DIM:LANES, rows] = ones
    vt_ref[i + 1, 0:HEAD_DIM, rows] = ones
    vt_ref[i + 1, HEAD_DIM:LANES, rows] = vt[HEAD_DIM:LANES]


def _store_vt_groups(vt_ref, rows, v_groups):
    vt = v_groups.astype(F32).T.astype(BF16)
    ones = jnp.ones((HEAD_DIM, v_groups.shape[0]), BF16)
    for g in range(NSA_KV_HEADS):
        vg = vt[g * HEAD_DIM:(g + 1) * HEAD_DIM]
        vt_ref[2 * g, 0:HEAD_DIM, rows] = vg
        vt_ref[2 * g, HEAD_DIM:LANES, rows] = ones
        vt_ref[2 * g + 1, 0:HEAD_DIM, rows] = ones
        vt_ref[2 * g + 1, HEAD_DIM:LANES, rows] = vg


def _finish_pair(acc_ref, hp):
    ae, ao = acc_ref[2 * hp], acc_ref[2 * hp + 1]
    top = lax.broadcasted_iota(jnp.int32, ae.shape, 0) < HEAD_DIM
    both = jnp.where(top, ae / jnp.maximum(ae[HEAD_DIM:HEAD_DIM + 1, :], TINY),
                     ao / jnp.maximum(ao[HEAD_DIM - 1:HEAD_DIM, :], TINY))
    return both.T


def _swap_halves(x_b):
    return pltpu.roll(x_b.astype(F32), HEAD_DIM, 1).astype(BF16)


def _half_masks(rows):
    lo = jnp.where(lax.broadcasted_iota(jnp.int32, (rows, LANES), 1) < HEAD_DIM, 1.0, 0.0)
    return lo.astype(BF16), (1.0 - lo).astype(BF16)


def _fox_attn_kernel(q_ref, qa_ref, k_ref, ka_ref, v_ref, gate_ref, o_ref, qs_ref, kk_ref, vt_ref, m_ref, acc_ref,
                     pre_ref, *, t, tk, ahead):
    qi = pl.program_id(1)
    lo_q, hi_q = _half_masks(t)
    cm = lax.broadcasted_iota(jnp.int32, (tk, t), 0) - lax.broadcasted_iota(jnp.int32, (tk, t), 1)

    @pl.when(qi == 0)
    def _():
        lo_k, hi_k = _half_masks(tk)

        def merge(j, c):
            rows = pl.ds(pl.multiple_of(j * tk, tk), tk)
            for hp in range(FOX_HEADS // 2):
                ls = slice(hp * LANES, (hp + 1) * LANES)
                kp, ka, vp = k_ref[rows, ls], ka_ref[rows, ls], v_ref[rows, ls]
                kk_ref[2 * hp, rows, :] = kp * lo_k + ka * hi_k
                kk_ref[2 * hp + 1, rows, :] = ka * lo_k + kp * hi_k
                _store_vt(vt_ref, 2 * hp, rows, vp)
            return c

        lax.fori_loop(0, k_ref.shape[0] // tk, merge, 0)

    for hp in range(FOX_HEADS // 2):
        ls = slice(hp * LANES, (hp + 1) * LANES)
        qp, qa = q_ref[:, ls], qa_ref[:, ls]
        qs_ref[2 * hp] = (qp * lo_q + qa * hi_q).astype(F32).T.astype(BF16)
        qs_ref[2 * hp + 1] = (qa * lo_q + qp * hi_q).astype(F32).T.astype(BF16)
    m_ref[...] = jnp.full(m_ref.shape, NEG_INF, F32)
    acc_ref[...] = jnp.zeros_like(acc_ref)

    def rows_of(j):
        return pl.ds(pl.multiple_of(j * tk, tk), tk)

    def scores(j, h):
        return jnp.dot(kk_ref[h, rows_of(j), :], qs_ref[h], preferred_element_type=F32)

    def tile(j, diag, j_next):
        queue = [pre_ref[i] for i in range(ahead)]
        for h in range(FOX_HEADS):
            if h + ahead < FOX_HEADS:
                queue.append(scores(j, h + ahead))
            elif j_next is not None:
                pre_ref[h + ahead - FOX_HEADS] = scores(j_next, h + ahead - FOX_HEADS)
            s = queue.pop(0)
            if diag:
                s = jnp.where(cm <= qi * t - j * tk, s, NEG_INF)
            m_old = m_ref[h]
            m_new = jnp.maximum(m_old, jnp.max(s, axis=0, keepdims=True))
            m_ref[h] = m_new
            p = jnp.exp2(s - m_new).astype(BF16)
            vr = _vt_rows(h)
            acc_ref[h, vr] = jnp.exp2(m_old - m_new) * acc_ref[h, vr] + jnp.dot(vt_ref[h, vr, rows_of(j)], p,
                                                                                preferred_element_type=F32)

    jd = (qi * t) // tk
    n_diag = max(t // tk, 1)
    for i in range(ahead):
        pre_ref[i] = scores(0, i)

    def body(j, c):
        tile(j, False, j + 1)
        return c

    lax.fori_loop(0, jd, body, 0)
    for dj in range(n_diag):
        tile(jd + dj, True, jd + dj + 1 if dj + 1 < n_diag else None)

    for hp in range(FOX_HEADS // 2):
        ls = slice(hp * LANES, (hp + 1) * LANES)
        o_ref[:, ls] = (_finish_pair(acc_ref, hp) * _silu(gate_ref[:, ls].astype(F32))).astype(o_ref.dtype)


def _fox_attn(p, qaug, kaug, b, s, t=512, tk=256, ahead=2):
    nq = s // t
    w = GROUP_W
    return pl.pallas_call(
        functools.partial(_fox_attn_kernel, t=t, tk=tk, ahead=ahead),
        grid=(b, nq),
        in_specs=[
            pl.BlockSpec((t, w), lambda bi, qi: (bi * nq + qi, OFF_FOX_Q // w)),
            pl.BlockSpec((t, w), lambda bi, qi: (bi * nq + qi, 0)),
            pl.BlockSpec((s, w), lambda bi, qi: (bi, OFF_FOX_K // w)),
            pl.BlockSpec((s, w), lambda bi, qi: (bi, 0)),
            pl.BlockSpec((s, w), lambda bi, qi: (bi, OFF_FOX_V // w)),
            pl.BlockSpec((t, w), lambda bi, qi: (bi * nq + qi, OFF_FOX_G // w)),
        ],
        out_specs=pl.BlockSpec((t, w), lambda bi, qi: (bi * nq + qi, 0)),
        out_shape=jax.ShapeDtypeStruct((b * s, w), BF16),
        scratch_shapes=[
            pltpu.VMEM((FOX_HEADS, LANES, t), BF16),
            pltpu.VMEM((FOX_HEADS, s, LANES), BF16),
            pltpu.VMEM((FOX_HEADS, LANES, s), BF16),
            pltpu.VMEM((FOX_HEADS, 1, t), F32),
            pltpu.VMEM((FOX_HEADS, LANES, t), F32),
            pltpu.VMEM((ahead, tk, t), F32),
        ],
        compiler_params=_params(("parallel", "arbitrary")),
        name="fox_attn",
    )(p, qaug, p, kaug, p, p)


def _dot_split(lhs_f32, rhs_b):
    out, rest = None, lhs_f32
    for _ in range(N_SPLIT):
        piece = rest.astype(BF16)
        rest = rest - piece.astype(F32)
        term = jnp.dot(piece, rhs_b, preferred_element_type=F32)
        out = term if out is None else out + term
    return out


def _ssd_kernel(z_ref, xbc_ref, dt_ref, cw_ref, cb_ref, dtb_ref, alog_ref, dsk_ref, ng_ref, o_ref,
                xpad_ref, xc_ref, state_ref, y_ref, *, nch):
    q = SSM_CHUNK
    rows_all = nch * q
    halo = 8

    @pl.when(pl.program_id(1) == 0)
    def _():
        xpad_ref[0:halo, :] = jnp.zeros((halo, SSM_CONV_DIM), F32)
        state_ref[...] = jnp.zeros_like(state_ref)

    xpad_ref[halo:halo + rows_all, :] = xbc_ref[...].astype(F32)
    y = cb_ref[...]
    for k in range(SSM_CONV):
        off = halo - (SSM_CONV - 1) + k
        y = y + cw_ref[k:k + 1, :] * xpad_ref[off:off + rows_all, :]
    xpad_ref[0:halo, :] = xpad_ref[rows_all:rows_all + halo, :]
    xc_ref[...] = _silu(y)

    x_dt = dt_ref[...] + dtb_ref[...]
    dt_all = jnp.maximum(x_dt, 0.0) + jnp.log(1.0 + jnp.exp(-jnp.abs(x_dt)))
    a_all = dt_all * (-jnp.exp(alog_ref[...]))
    tri_t = (lax.broadcasted_iota(jnp.int32, (q, q), 0) <= lax.broadcasted_iota(jnp.int32, (q, q), 1)).astype(BF16)
    er = lax.broadcasted_iota(jnp.int32, (LANES, GROUP_W), 0)
    ec = lax.broadcasted_iota(jnp.int32, (LANES, GROUP_W), 1)
    expand = jnp.where(ec // HEAD_DIM == er - PS_DT_LANE, 1.0, 0.0).astype(BF16)
    row = lax.broadcasted_iota(jnp.int32, (q, q), 0)
    col = lax.broadcasted_iota(jnp.int32, (q, q), 1)
    causal = row >= col
    lo = lax.broadcasted_iota(jnp.int32, (q, LANES), 1) < HEAD_DIM
    gw = GROUP_W // SSM_GROUPS
    hpg = SSM_HEADS // SSM_GROUPS

    for c in range(nch):
        rs = slice(c * q, (c + 1) * q)
        xs = xc_ref[rs, :GROUP_W]
        dt = dt_all[rs]
        acs_t = _dot_split(a_all[rs].T, tri_t)
        acs = acs_t.T
        stacked = jnp.concatenate([dt, jnp.exp(acs), jnp.exp(acs[q - 1:q, :] - acs)], axis=0)
        full = _dot_split(stacked, expand)
        dt_full, eacs_full, dec_full = full[0:q], full[q:2 * q], full[2 * q:3 * q]
        xdt = xs * dt_full
        xdt_b = xdt.astype(BF16)
        xdec_b = (xdt * dec_full).astype(BF16)

        for g in range(SSM_GROUPS):
            bm = xc_ref[rs, GROUP_W + g * SSM_STATE:GROUP_W + (g + 1) * SSM_STATE]
            cm = xc_ref[rs, GROUP_W + (SSM_GROUPS + g) * SSM_STATE:GROUP_W + (SSM_GROUPS + g + 1) * SSM_STATE]
            bm_b = bm.astype(BF16)
            cm_b = cm.astype(BF16)
            gs = slice(g * gw, (g + 1) * gw)
            cbg = lax.dot_general(cm_b, bm_b, (((1,), (1,)), ((), ())), preferred_element_type=F32)
            st = state_ref[:, gs]
            y_off = jnp.dot(cm_b, st.astype(BF16), preferred_element_type=F32) * eacs_full[:, gs]
            cst = jnp.dot(bm.T.astype(BF16), xdec_b[:, gs], preferred_element_type=F32)
            state_ref[:, gs] = st * eacs_full[q - 1:q, gs] + cst
            for hp in range(hpg // 2):
                ls = slice(g * gw + hp * LANES, g * gw + (hp + 1) * LANES)
                yd = []
                for e in range(2):
                    h = g * hpg + 2 * hp + e
                    hl = PS_DT_LANE + h
                    seg = jnp.exp(jnp.where(causal, acs[:, hl:hl + 1] - acs_t[hl:hl + 1, :], NEG_INF))
                    yd.append(jnp.dot((cbg * seg).astype(BF16), xdt_b[:, ls], preferred_element_type=F32))
                y_ref[:, ls] = jnp.where(lo, yd[0], yd[1]) + y_off[:, hp * LANES:(hp + 1) * LANES]

        yz = (y_ref[...] + xs * dsk_ref[...]) * _silu(z_ref[rs, :].astype(F32))
        for g in range(SSM_GROUPS):
            gs = slice(g * gw, (g + 1) * gw)
            blk = yz[:, gs]
            ms = jnp.mean(blk * blk, axis=-1, keepdims=True)
            o_ref[rs, gs] = ((blk * lax.rsqrt(ms + EPS)) * ng_ref[:, gs]).astype(o_ref.dtype)


def _ssd(p, ps, conv_w, conv_b, dt_bias_pad, a_log_pad, d_full, norm_g, b, s, nch=8):
    q = SSM_CHUNK
    rows = nch * q
    nc = s // rows
    row = lambda bi, ci: (bi * nc + ci)
    const = lambda bi, ci: (0, 0)
    return pl.pallas_call(
        functools.partial(_ssd_kernel, nch=nch),
        grid=(b, nc),
        in_specs=[
            pl.BlockSpec((rows, GROUP_W), lambda bi, ci: (row(bi, ci), OFF_SSM_Z // GROUP_W)),
            pl.BlockSpec((rows, SSM_CONV_DIM), lambda bi, ci: (row(bi, ci), OFF_SSM_XBC // SSM_CONV_DIM)),
            pl.BlockSpec((rows, LANES), lambda bi, ci: (row(bi, ci), 0)),
            pl.BlockSpec((SSM_CONV, SSM_CONV_DIM), const),
            pl.BlockSpec((1, SSM_CONV_DIM), const),
            pl.BlockSpec((1, LANES), const),
            pl.BlockSpec((1, LANES), const),
            pl.BlockSpec((1, GROUP_W), const),
            pl.BlockSpec((1, GROUP_W), const),
        ],
        out_specs=pl.BlockSpec((rows, GROUP_W), lambda bi, ci: (row(bi, ci), 0)),
        out_shape=jax.ShapeDtypeStruct((b * s, GROUP_W), BF16),
        scratch_shapes=[
            pltpu.VMEM((rows + 8, SSM_CONV_DIM), F32),
            pltpu.VMEM((rows, SSM_CONV_DIM), F32),
            pltpu.VMEM((SSM_STATE, GROUP_W), F32),
            pltpu.VMEM((q, GROUP_W), F32),
        ],
        compiler_params=_params(("parallel", "arbitrary")),
        name="ssd",
    )(p, p, ps, conv_w, conv_b, dt_bias_pad, a_log_pad, d_full, norm_g)


def _compress_kernel(x_ref, w1_ref, w2_ref, pe_ref, o_ref, ot_ref, xf_ref):
    s = x_ref.shape[0]
    n = s // CMP_STRIDE
    for sb in range(2):
        xf_ref[sb] = x_ref[:, sb * LANES:(sb + 1) * LANES].astype(F32)
    o_ref[...] = jnp.zeros_like(o_ref)
    ot_ref[...] = jnp.zeros_like(ot_ref)
    for sb in range(2):
        first = jnp.zeros((n, 2 * CMP_HIDDEN), F32)
        second = jnp.zeros((n, 2 * CMP_HIDDEN), F32)
        for l in range(CMP_STRIDE):
            xl = xf_ref[sb, pl.ds(l, n, stride=CMP_STRIDE), :]
            first = first + jnp.dot((xl + pe_ref[sb, l:l + 1, :]).astype(BF16), w1_ref[sb, l],
                                    preferred_element_type=F32)
            second = second + jnp.dot((xl + pe_ref[sb, CMP_STRIDE + l:CMP_STRIDE + l + 1, :]).astype(BF16),
                                      w1_ref[sb, CMP_STRIDE + l], preferred_element_type=F32)
        h = _silu(first + pltpu.roll(second, n - 1, 0))
        o = jnp.dot(h.astype(BF16), w2_ref[sb], preferred_element_type=F32)
        for e in range(2):
            oe = o[:, e * LANES:(e + 1) * LANES]
            o_ref[0, 2 * sb + e, 0:n, :] = oe.astype(o_ref.dtype)
            ot_ref[0, 2 * sb + e, :, 0:n] = oe.T.astype(ot_ref.dtype)


def _blockdiag2(a):
    z = jnp.zeros_like(a)
    return jnp.concatenate([jnp.concatenate([a, z], axis=-1), jnp.concatenate([z, a], axis=-1)], axis=-2)


def _compress(p, cmp_w1, cmp_w2, cmp_pe, b, s):
    nslot = 2 * NSA_KV_HEADS
    w1 = _blockdiag2(cmp_w1.astype(BF16).reshape(2, CMP_BLOCK, HEAD_DIM, CMP_HIDDEN))
    w2 = _blockdiag2(jnp.concatenate([cmp_w2, cmp_w2], axis=-1).astype(BF16))
    pe = jnp.concatenate([cmp_pe, cmp_pe], axis=-1).astype(F32)
    kw = 2 * NSA_KV_W
    return pl.pallas_call(
        _compress_kernel,
        grid=(b,),
        in_specs=[
            pl.BlockSpec((s, kw), lambda bi: (bi, OFF_CMP_KV // kw)),
            pl.BlockSpec(w1.shape, lambda bi: (0, 0, 0, 0)),
            pl.BlockSpec(w2.shape, lambda bi: (0, 0, 0)),
            pl.BlockSpec(pe.shape, lambda bi: (0, 0, 0)),
        ],
        out_specs=[pl.BlockSpec((1, nslot, N_CMP_PAD, LANES), lambda bi: (bi, 0, 0, 0)),
                   pl.BlockSpec((1, nslot, LANES, N_CMP_PAD), lambda bi: (bi, 0, 0, 0))],
        out_shape=[jax.ShapeDtypeStruct((b, nslot, N_CMP_PAD, LANES), BF16),
                   jax.ShapeDtypeStruct((b, nslot, LANES, N_CMP_PAD), BF16)],
        scratch_shapes=[pltpu.VMEM((2, s, LANES), F32)],
        compiler_params=_params(("parallel",)),
        name="nsa_compress",
    )(p, w1, w2, pe)


def _cmp_select_kernel(q_ref, kv_ref, vt_ref, gl_ref, gate_ref, fc_ref, ovt_ref, o_ref, ns_ref):
    t = T_ATT
    ncp = kv_ref.shape[2]
    qi = pl.program_id(1)
    t0 = qi * t
    lo = lax.broadcasted_iota(jnp.int32, (t, LANES), 1) < HEAD_DIM
    zero = jnp.zeros((t, LANES), BF16)
    start = pl.multiple_of(ncp - qi * (t // CMP_STRIDE), CMP_STRIDE)
    gl = gl_ref[...]

    nsel = SEL_BLOCK
    jrow = lax.broadcasted_iota(jnp.int32, (nsel, t), 0)
    cur = (t0 + lax.broadcasted_iota(jnp.int32, (nsel, t), 1)) // SEL_BLOCK
    forced = (jrow == 0) | (jrow == cur) | (jrow == cur - 1)
    past = jrow <= cur

    for g in range(NSA_KV_HEADS):
        kc = kv_ref[0, g]
        vct = vt_ref[0, NSA_KV_HEADS + g]
        psum = jnp.zeros((ncp, t), F32)
        outs = []

        def scores(r):
            h = g * NSA_REP + r
            qp = q_ref[:, (h // 2) * LANES:(h // 2 + 1) * LANES]
            qe = jnp.where(lo, qp, zero) if r % 2 == 0 else jnp.where(lo, zero, qp)
            return lax.dot_general(kc, qe, (((1,), (1,)), ((), ())), preferred_element_type=F32)

        queue = [scores(0)]
        for r in range(NSA_REP):
            h = g * NSA_REP + r
            if r + 1 < NSA_REP:
                queue.append(scores(r + 1))
            s = fc_ref[h, pl.ds(start, ncp), :] + queue.pop(0)
            m = jnp.maximum(jnp.max(s, axis=0, keepdims=True), 0.1 * NEG_INF)
            e = jnp.exp2(s - m)
            pr = e * (1.0 / jnp.maximum(jnp.sum(e, axis=0, keepdims=True), TINY))
            psum = psum + pr
            oc = jnp.dot(vct, pr.astype(BF16), preferred_element_type=F32)
            outs.append(oc.T * jax.nn.sigmoid(gl[:, PS_GATE_LANE + h:PS_GATE_LANE + h + 1]))
        for hp in range(NSA_REP // 2):
            ls = slice((g * NSA_REP // 2 + hp) * LANES, (g * NSA_REP // 2 + hp + 1) * LANES)
            o = jnp.where(lo, outs[2 * hp], outs[2 * hp + 1])
            o_ref[:, ls] = (o * _silu(gate_ref[:, ls].astype(F32))).astype(o_ref.dtype)

        imp_t = jnp.zeros((nsel, t), F32)
        rest = psum
        for _ in range(N_SPLIT):
            piece = rest.astype(BF16)
            rest = rest - piece.astype(F32)
            imp_t = imp_t + jnp.dot(ovt_ref[...], piece, preferred_element_type=F32)
        imp_t = jnp.where(past, jnp.where(forced, SEL_FORCE, imp_t), -SEL_FORCE)
        sub = 8
        sub_row = lax.broadcasted_iota(jnp.int32, (sub, t), 0)
        rows = [imp_t[k * sub:(k + 1) * sub] for k in range(nsel // sub)]
        rank = [jnp.zeros((sub, t), F32) for _ in rows]
        for i in range(nsel):
            bi = imp_t[i:i + 1, :]
            for k, x in enumerate(rows):
                if k * sub > i:
                    ahead = jnp.where(bi >= x, 1.0, 0.0)
                elif k * sub + sub - 1 <= i:
                    ahead = jnp.where(bi > x, 1.0, 0.0)
                else:
                    ahead = jnp.where(sub_row > i - k * sub, jnp.where(bi >= x, 1.0, 0.0),
                                      jnp.where(bi > x, 1.0, 0.0))
                rank[k] = rank[k] + ahead
        rank = jnp.concatenate(rank, axis=0)
        notsel = jnp.where((rank < float(SEL_TOPK)) & past, 0.0, 1.0)
        ns2 = jnp.concatenate([notsel, notsel], axis=0).T
        ns_ref[:, g * LANES:(g + 1) * LANES] = ns2.astype(ns_ref.dtype)


def _cmp_select(p, ps, kvc, kvc_t, fc, overlap_t, b, s):
    t = T_ATT
    nq = s // t
    w = GROUP_W
    ncp = kvc.shape[2]
    return pl.pallas_call(
        _cmp_select_kernel,
        grid=(b, nq),
        in_specs=[
            pl.BlockSpec((t, w), lambda bi, qi: (bi * nq + qi, OFF_NSA_Q // w)),
            pl.BlockSpec((1, 2 * NSA_KV_HEADS, ncp, LANES), lambda bi, qi: (bi, 0, 0, 0)),
            pl.BlockSpec((1, 2 * NSA_KV_HEADS, LANES, ncp), lambda bi, qi: (bi, 0, 0, 0)),
            pl.BlockSpec((t, LANES), lambda bi, qi: (bi * nq + qi, 0)),
            pl.BlockSpec((t, w), lambda bi, qi: (bi * nq + qi, OFF_NSA_G // w)),
            pl.BlockSpec((NSA_HEADS, 2 * ncp, t), lambda bi, qi: (0, 0, 0)),
            pl.BlockSpec((SEL_BLOCK, ncp), lambda bi, qi: (0, 0)),
        ],
        out_specs=[
            pl.BlockSpec((t, w), lambda bi, qi: (bi * nq + qi, 0)),
            pl.BlockSpec((t, NSA_KV_HEADS * LANES), lambda bi, qi: (bi * nq + qi, 0)),
        ],
        out_shape=[jax.ShapeDtypeStruct((b * s, w), BF16),
                   jax.ShapeDtypeStruct((b * s, NSA_KV_HEADS * LANES), BF16)],
        compiler_params=_params(("parallel", "arbitrary")),
        name="nsa_cmp_select",
    )(p, kvc, kvc_t, ps, p, fc, overlap_t)


def _sel_attn_kernel(tab_ref, q_ref, ns_ref, k_ref, v_ref, gl_ref, gate_ref, bs_ref, o_ref,
                     qa_ref, kk_ref, vt_ref, m_ref, acc_ref, pre_ref, *, t, tk, ahead):
    qi = pl.program_id(1)
    lo_q, hi_q = _half_masks(t)
    lo = lax.broadcasted_iota(jnp.int32, (t, LANES), 1) < HEAD_DIM
    n_near = t // tk + 1

    @pl.when(qi == 0)
    def _():
        lane = lax.broadcasted_iota(jnp.int32, (tk, LANES), 1)
        krow = lax.broadcasted_iota(jnp.int32, (tk, LANES), 0)
        lo_f = jnp.where(lane < HEAD_DIM, 1.0, 0.0)
        hi_f = 1.0 - lo_f
        lo_k, hi_k = lo_f.astype(BF16), hi_f.astype(BF16)

        def merge(j, c):
            ks = pl.multiple_of(j * tk, tk)
            rows = pl.ds(ks, tk)
            hot = jnp.where((lane % HEAD_DIM) == (ks + krow) // SEL_BLOCK, NEG_INF, 0.0)
            hot_lo, hot_hi = (hot * lo_f).astype(BF16), (hot * hi_f).astype(BF16)
            k01 = k_ref[rows, :]
            k10 = _swap_halves(k01)
            kk_ref[0, rows, :] = k01 * lo_k + hot_hi
            kk_ref[1, rows, :] = hot_lo + k10 * hi_k
            kk_ref[2, rows, :] = k10 * lo_k + hot_hi
            kk_ref[3, rows, :] = hot_lo + k01 * hi_k
            _store_vt_groups(vt_ref, rows, v_ref[rows, :])
            return c

        lax.fori_loop(0, k_ref.shape[0] // tk, merge, 0)

    for h in range(NSA_HEADS):
        g = h // NSA_REP
        qp = q_ref[:, (h // 2) * LANES:(h // 2 + 1) * LANES]
        ns = ns_ref[:, g * LANES:(g + 1) * LANES]
        qa = qp * lo_q + ns * hi_q if h % 2 == 0 else ns * lo_q + qp * hi_q
        qa_ref[h] = qa.astype(F32).T.astype(BF16)
    m_ref[...] = jnp.full(m_ref.shape, NEG_INF, F32)
    acc_ref[...] = jnp.zeros_like(acc_ref)

    kv = lambda h: 2 * (h // NSA_REP) + h % 2

    def rows_of(j):
        return pl.ds(pl.multiple_of(j * tk, tk), tk)

    def scores(j, h):
        return jnp.dot(kk_ref[kv(h), rows_of(j), :], qa_ref[h], preferred_element_type=F32)

    def tile(j, near, j_next):
        rows = rows_of(j)
        queue = [pre_ref[i] for i in range(ahead)]
        for h in range(NSA_HEADS):
            if h + ahead < NSA_HEADS:
                queue.append(scores(j, h + ahead))
            elif j_next is not None:
                pre_ref[h + ahead - NSA_HEADS] = scores(j_next, h + ahead - NSA_HEADS)
            s = queue.pop(0)
            m_old = m_ref[h]
            if near is None:
                far = tab_ref[(REL_BUCKETS - 1) * NSA_HEADS + h] * LOG2E
                m_new = jnp.maximum(m_old, jnp.max(s, axis=0, keepdims=True) + far)
                p = jnp.exp2(s - (m_new - far)).astype(BF16)
            else:
                s = bs_ref[h, near * tk:(near + 1) * tk, :] + s
                m_new = jnp.maximum(m_old, jnp.max(s, axis=0, keepdims=True))
                p = jnp.exp2(s - m_new).astype(BF16)
            m_ref[h] = m_new
            vr = _vt_rows(h)
            acc_ref[h, vr] = jnp.exp2(m_old - m_new) * acc_ref[h, vr] + jnp.dot(vt_ref[kv(h), vr, rows], p,
                                                                                preferred_element_type=F32)

    jd = (qi * t) // tk
    for i in range(ahead):
        pre_ref[i] = scores(0, i)

    def body(j, c):
        tile(j, None, j + 1)
        return c

    lax.fori_loop(0, jnp.maximum(jd - 1, 0), body, 0)

    @pl.when(qi > 0)
    def _():
        tile(jd - 1, 0, jd)

    for dj in range(n_near - 1):
        tile(jd + dj, 1 + dj, jd + dj + 1 if dj + 2 < n_near else None)

    gl = gl_ref[...]
    for hp in range(NSA_HEADS // 2):
        ls = slice(hp * LANES, (hp + 1) * LANES)
        c0 = PS_GATE_LANE + NSA_HEADS + 2 * hp
        branch = jnp.where(lo, jax.nn.sigmoid(gl[:, c0:c0 + 1]), jax.nn.sigmoid(gl[:, c0 + 1:c0 + 2]))
        o_ref[:, ls] = (_finish_pair(acc_ref, hp) * branch * _silu(gate_ref[:, ls].astype(F32))).astype(o_ref.dtype)


def _sel_attn(tab_flat, p, ps, notsel, bsel, b, s, t=512, tk=256, ahead=2):
    nq = s // t
    w = GROUP_W
    kw = NSA_KV_HEADS * LANES
    n_near = t // tk + 1
    return pl.pallas_call(
        functools.partial(_sel_attn_kernel, t=t, tk=tk, ahead=ahead),
        grid=(b, nq),
        in_specs=[
            pl.BlockSpec(memory_space=pltpu.SMEM),
            pl.BlockSpec((t, w), lambda bi, qi: (bi * nq + qi, OFF_NSA_Q // w)),
            pl.BlockSpec((t, kw), lambda bi, qi: (bi * nq + qi, 0)),
            pl.BlockSpec((s, NSA_KV_W), lambda bi, qi: (bi, OFF_SEL_K // NSA_KV_W)),
            pl.BlockSpec((s, NSA_KV_W), lambda bi, qi: (bi, OFF_SEL_V // NSA_KV_W)),
            pl.BlockSpec((t, LANES), lambda bi, qi: (bi * nq + qi, 0)),
            pl.BlockSpec((t, w), lambda bi, qi: (bi * nq + qi, OFF_NSA_G // w)),
            pl.BlockSpec((NSA_HEADS, n_near * tk, t), lambda bi, qi: (0, 0, 0), pipeline_mode=pl.Buffered(1)),
        ],
        out_specs=pl.BlockSpec((t, w), lambda bi, qi: (bi * nq + qi, 0)),
        out_shape=jax.ShapeDtypeStruct((b * s, w), BF16),
        scratch_shapes=[
            pltpu.VMEM((NSA_HEADS, LANES, t), BF16),
            pltpu.VMEM((2 * NSA_KV_HEADS, s, LANES), BF16),
            pltpu.VMEM((2 * NSA_KV_HEADS, LANES, s), BF16),
            pltpu.VMEM((NSA_HEADS, 1, t), F32),
            pltpu.VMEM((NSA_HEADS, LANES, t), F32),
            pltpu.VMEM((ahead, tk, t), F32),
        ],
        compiler_params=_params(("parallel", "arbitrary")),
        name="nsa_sel_attn",
    )(tab_flat, p, notsel, p, p, ps, p, bsel)


def _win_attn_kernel(q_ref, k_ref, v_ref, gl_ref, gate_ref, bw_ref, o_ref, qe_ref, kk_ref, vt_ref, m_ref, acc_ref,
                     pre_ref, *, t, tk, ahead):
    qi = pl.program_id(1)
    lo_q, hi_q = _half_masks(t)
    lo = lax.broadcasted_iota(jnp.int32, (t, LANES), 1) < HEAD_DIM
    n_before = WINDOW // tk
    n_tiles = n_before + t // tk

    @pl.when(qi == 0)
    def _():
        lo_k, hi_k = _half_masks(tk)

        def merge(j, c):
            rows = pl.ds(pl.multiple_of(j * tk, tk), tk)
            k01 = k_ref[rows, :]
            k10 = _swap_halves(k01)
            kk_ref[0, rows, :] = k01 * lo_k + k10 * hi_k
            kk_ref[1, rows, :] = k10 * lo_k + k01 * hi_k
            _store_vt_groups(vt_ref, rows, v_ref[rows, :])
            return c

        lax.fori_loop(0, v_ref.shape[0] // tk, merge, 0)

    for h in range(NSA_HEADS):
        qp = q_ref[:, (h // 2) * LANES:(h // 2 + 1) * LANES]
        qe_ref[h] = qp * lo_q if h % 2 == 0 else qp * hi_q
    m_ref[...] = jnp.full(m_ref.shape, NEG_INF, F32)
    acc_ref[...] = jnp.zeros_like(acc_ref)

    def rows_of(w):
        return pl.ds(pl.multiple_of((qi * (t // tk) - n_before + w) * tk, tk), tk)

    def scores(w, h):
        return lax.dot_general(kk_ref[h // NSA_REP, rows_of(w), :], qe_ref[h], (((1,), (1,)), ((), ())),
                               preferred_element_type=F32)

    def tile(w):
        rows = rows_of(w)
        queue = [pre_ref[i] for i in range(ahead)]
        for h in range(NSA_HEADS):
            if h + ahead < NSA_HEADS:
                queue.append(scores(w, h + ahead))
            elif w + 1 < n_tiles:
                pre_ref[h + ahead - NSA_HEADS] = scores(w + 1, h + ahead - NSA_HEADS)
            s = bw_ref[h, w * tk:(w + 1) * tk, :] + queue.pop(0)
            m_old = m_ref[h]
            m_new = jnp.maximum(m_old, jnp.max(s, axis=0, keepdims=True))
            m_ref[h] = m_new
            p = jnp.exp2(s - m_new).astype(BF16)
            vr = _vt_rows(h)
            acc_ref[h, vr] = jnp.exp2(m_old - m_new) * acc_ref[h, vr] + jnp.dot(
                vt_ref[2 * (h // NSA_REP) + h % 2, vr, rows], p, preferred_element_type=F32)

    first = jnp.maximum(n_before - qi * (t // tk), 0)
    for i in range(ahead):
        pre_ref[i] = scores(first, i)
    per_q = t // tk
    first_qi = [-(-(n_before - w) // per_q) if w < n_before else 0 for w in range(n_tiles)]
    for need in sorted(set(first_qi), reverse=True):
        group = [w for w in range(n_tiles) if first_qi[w] == need]

        def run(group=group):
            for w in group:
                tile(w)

        if need > 0:
            pl.when(qi >= need)(run)
        else:
            run()

    gl = gl_ref[...]
    for hp in range(NSA_HEADS // 2):
        ls = slice(hp * LANES, (hp + 1) * LANES)
        c0 = PS_GATE_LANE + 2 * NSA_HEADS + 2 * hp
        branch = jnp.where(lo, jax.nn.sigmoid(gl[:, c0:c0 + 1]), jax.nn.sigmoid(gl[:, c0 + 1:c0 + 2]))
        o_ref[:, ls] = (_finish_pair(acc_ref, hp) * branch * _silu(gate_ref[:, ls].astype(F32))).astype(o_ref.dtype)


def _win_attn(p, ps, bwin, b, s, t=WIN_T, tk=WIN_TK, ahead=2):
    assert t % tk == 0 and WINDOW % tk == 0
    nq = s // t
    w = GROUP_W
    return pl.pallas_call(
        functools.partial(_win_attn_kernel, t=t, tk=tk, ahead=ahead),
        grid=(b, nq),
        in_specs=[
            pl.BlockSpec((t, w), lambda bi, qi: (bi * nq + qi, OFF_NSA_Q // w)),
            pl.BlockSpec((s, NSA_KV_W), lambda bi, qi: (bi, OFF_WIN_K // NSA_KV_W)),
            pl.BlockSpec((s, NSA_KV_W), lambda bi, qi: (bi, OFF_WIN_V // NSA_KV_W)),
            pl.BlockSpec((t, LANES), lambda bi, qi: (bi * nq + qi, 0)),
            pl.BlockSpec((t, w), lambda bi, qi: (bi * nq + qi, OFF_NSA_G // w)),
            pl.BlockSpec((NSA_HEADS, WINDOW + t, t), lambda bi, qi: (0, 0, 0), pipeline_mode=pl.Buffered(1)),
        ],
        out_specs=pl.BlockSpec((t, w), lambda bi, qi: (bi * nq + qi, 0)),
        out_shape=jax.ShapeDtypeStruct((b * s, w), BF16),
        scratch_shapes=[
            pltpu.VMEM((NSA_HEADS, t, LANES), BF16),
            pltpu.VMEM((NSA_KV_HEADS, s, LANES), BF16),
            pltpu.VMEM((2 * NSA_KV_HEADS, LANES, s), BF16),
            pltpu.VMEM((NSA_HEADS, 1, t), F32),
            pltpu.VMEM((NSA_HEADS, LANES, t), F32),
            pltpu.VMEM((ahead, tk, t), F32),
        ],
        compiler_params=_params(("parallel", "arbitrary")),
        name="nsa_win_attn",
    )(p, p, p, ps, p, bwin)


def _mem_attn_kernel(q_ref, gate_ref, kv_ref, o_ref):
    scale = MEM_HEAD_DIM ** -0.5
    for h in range(MEM_HEADS):
        ls = slice(h * LANES, (h + 1) * LANES)
        k = kv_ref[:, ls]
        v = kv_ref[:, GROUP_W + h * LANES:GROUP_W + (h + 1) * LANES]
        s = lax.dot_general(q_ref[:, ls], k, (((1,), (1,)), ((), ())), preferred_element_type=F32) * scale
        m = jnp.max(s, axis=1, keepdims=True)
        e = jnp.exp(s - m)
        l = jnp.sum(e, axis=1, keepdims=True)
        o = jnp.dot(e.astype(BF16), v, preferred_element_type=F32) / l
        o_ref[:, ls] = (o * _silu(gate_ref[:, ls].astype(F32))).astype(o_ref.dtype)


def _mem_attn(p, mem_kv, b, s, t=1024):
    nq = s // t
    w = GROUP_W
    m = mem_kv.shape[0] // b
    return pl.pallas_call(
        _mem_attn_kernel,
        grid=(b, nq),
        in_specs=[
            pl.BlockSpec((t, w), lambda bi, qi: (bi * nq + qi, OFF_MEM_Q // w)),
            pl.BlockSpec((t, w), lambda bi, qi: (bi * nq + qi, OFF_MEM_G // w)),
            pl.BlockSpec((m, 2 * w), lambda bi, qi: (bi, 0)),
        ],
        out_specs=pl.BlockSpec((t, w), lambda bi, qi: (bi * nq + qi, 0)),
        out_shape=jax.ShapeDtypeStruct((b * s, w), BF16),
        compiler_params=_params(("parallel", "arbitrary")),
        name="mem_attn",
    )(p, p, mem_kv)


def _out_proj_kernel(x_ref, of_ref, os_ref, oc_ref, osel_ref, ow_ref, om_ref, w_ref, g_ref, o_ref, *, final):
    w = GROUP_W
    nsa = (oc_ref[...].astype(F32) + osel_ref[...].astype(F32) + ow_ref[...].astype(F32)).astype(BF16)
    acc = x_ref[...]
    for i, part in enumerate((of_ref[...], os_ref[...], nsa, om_ref[...])):
        acc = acc + jnp.dot(part, w_ref[i * w:(i + 1) * w, :], preferred_element_type=F32)
    if final:
        ms = jnp.mean(acc * acc, axis=-1, keepdims=True)
        acc = (acc * lax.rsqrt(ms + EPS)) * g_ref[...]
    o_ref[...] = acc


def _out_proj(x2d, parts, w_out, g, final, tm=1024):
    n, d = x2d.shape
    w = GROUP_W
    part_spec = pl.BlockSpec((tm, w), lambda i: (i, 0))
    return pl.pallas_call(
        functools.partial(_out_proj_kernel, final=final),
        grid=(n // tm,),
        in_specs=[pl.BlockSpec((tm, d), lambda i: (i, 0))] + [part_spec] * 6 + [
            pl.BlockSpec((4 * w, d), lambda i: (0, 0)),
            pl.BlockSpec((1, d), lambda i: (0, 0)),
        ],
        out_specs=pl.BlockSpec((tm, d), lambda i: (i, 0)),
        out_shape=jax.ShapeDtypeStruct((n, d), F32),
        compiler_params=_params(("parallel",)),
        name="out_proj",
    )(x2d, *parts, w_out, g.reshape(1, d))


def _pack_in_proj(w_in_l):
    fox, ssm, nsa, mem = 0, FOX_COLS, FOX_COLS + SSM_COLS, FOX_COLS + SSM_COLS + NSA_COLS
    w = GROUP_W
    q_scale = HEAD_DIM ** -0.5 * LOG2E
    cols = lambda a, n: w_in_l[:, a:a + n]
    kv = lambda slot: nsa + w + slot * NSA_KV_W
    main = ([cols(fox, w) * q_scale, cols(fox + w, 3 * w)]
            + [cols(ssm + w, SSM_CONV_DIM), cols(ssm, w)]
            + [cols(nsa, w) * q_scale, cols(nsa + w + 6 * NSA_KV_W + 3 * NSA_HEADS, w)]
            + [cols(mem, 2 * w)]
            + [cols(kv(2), 4 * NSA_KV_W)]
            + [cols(kv(0), 2 * NSA_KV_W)])
    w_main = jnp.concatenate(main, axis=1).astype(BF16)
    assert w_main.shape[1] == P_COLS
    small = ([cols(fox + 4 * w, FOX_HEADS)] * N_SPLIT
             + [cols(ssm + w + SSM_CONV_DIM, SSM_HEADS)]
             + [cols(nsa + w + 6 * NSA_KV_W, 3 * NSA_HEADS)])
    assert N_SPLIT * FOX_HEADS == PS_DT_LANE and PS_DT_LANE + SSM_HEADS == PS_GATE_LANE
    used = PS_GATE_LANE + 3 * NSA_HEADS
    small.append(jnp.zeros((w_in_l.shape[0], PS_COLS - used), w_in_l.dtype))
    return w_main, jnp.concatenate(small, axis=1).astype(BF16)


def _pad_lanes(v, first=0):
    return jnp.pad(v.astype(F32), (first, LANES - first - v.shape[0])).reshape(1, LANES)


def _trunk(x, mem, norm_g, w_in, fox_f_bias, ssm_conv_w, ssm_conv_b, ssm_dt_bias, ssm_a_log, ssm_d,
           ssm_norm_g, nsa_cmp_pe, nsa_cmp_w1, nsa_cmp_w2, rel_bias_table, mem_norm_g, w_mem_kv, w_out,
           final_norm_g):
    b, s, d = x.shape
    depth = w_in.shape[0]
    n = b * s
    m_tok = mem.shape[1]
    n_cmp = (s - CMP_BLOCK) // CMP_STRIDE + 1
    n_rows = s // CMP_STRIDE
    assert s % 1024 == 0 and s // SEL_BLOCK <= HEAD_DIM and n_rows <= N_CMP_PAD and d == D_MODEL

    tab_flat = rel_bias_table.astype(F32).reshape(-1)
    unbounded = 1 << 30
    bwin = _t5_table(tab_flat, WINDOW + WIN_T, WIN_T, 1, WINDOW, WINDOW)
    bsel = _t5_table(tab_flat, (SEL_T // SEL_TK + 1) * SEL_TK, SEL_T, 1, SEL_TK, unbounded)
    fcmp = _t5_table(tab_flat, 2 * N_CMP_PAD, T_ATT, CMP_STRIDE, CMP_STRIDE * N_CMP_PAD - (CMP_BLOCK - 1), unbounded)

    cs = np.arange(N_CMP_PAD)[None, :] * CMP_STRIDE
    js = np.arange(SEL_BLOCK)[:, None] * SEL_BLOCK
    overlap_t = ((cs < js + SEL_BLOCK) & (cs + CMP_BLOCK > js) & (np.arange(N_CMP_PAD)[None, :] < n_cmp)
                 & (np.arange(SEL_BLOCK)[:, None] < s // SEL_BLOCK)).astype(np.float32)
    overlap_t = jnp.asarray(overlap_t, BF16)

    x2d = x.reshape(n, d)
    mem2d = mem.reshape(b * m_tok, d)
    for l in range(depth):
        w_main, w_small = _pack_in_proj(w_in[l])
        p, ps = _norm_proj(x2d, norm_g[l], (w_main, w_small), (BF16, F32))

        qaug, kaug = _fox_cumsum(ps, _pad_lanes(jnp.tile(fox_f_bias[l], N_SPLIT)), b, s)
        o_fox = _fox_attn(p, qaug, kaug, b, s)

        o_ssd = _ssd(p, ps, ssm_conv_w[l].astype(F32), ssm_conv_b[l].reshape(1, -1).astype(F32),
                     _pad_lanes(ssm_dt_bias[l], PS_DT_LANE), _pad_lanes(ssm_a_log[l], PS_DT_LANE),
                     jnp.repeat(ssm_d[l].astype(F32), HEAD_DIM).reshape(1, GROUP_W),
                     ssm_norm_g[l].reshape(1, GROUP_W).astype(F32), b, s)

        kv_cmp, kv_cmp_t = _compress(p, nsa_cmp_w1[l], nsa_cmp_w2[l], nsa_cmp_pe[l], b, s)
        o_cmp, notsel = _cmp_select(p, ps, kv_cmp, kv_cmp_t, fcmp, overlap_t, b, s)
        o_sel = _sel_attn(tab_flat, p, ps, notsel, bsel, b, s)
        o_win = _win_attn(p, ps, bwin, b, s)

        w_kv = w_mem_kv[l].astype(BF16)
        (mem_kv,) = _norm_proj(mem2d, mem_norm_g[l], (w_kv,), (BF16,), tm=min(512, b * m_tok))
        o_mem = _mem_attn(p, mem_kv, b, s)

        x2d = _out_proj(x2d, (o_fox, o_ssd, o_cmp, o_sel, o_win, o_mem), w_out[l].astype(BF16),
                        final_norm_g, final=(l == depth - 1))
    return x2d.reshape(b, s, d)


def kernel(x, mem, norm_g, w_in, fox_f_bias, ssm_conv_w, ssm_conv_b, ssm_dt_bias, ssm_a_log, ssm_d, ssm_norm_g,
           nsa_cmp_pe, nsa_cmp_w1, nsa_cmp_w2, rel_bias_table, mem_norm_g, w_mem_kv, w_out, final_norm_g):
    return _trunk(x, mem, norm_g, w_in, fox_f_bias, ssm_conv_w, ssm_conv_b, ssm_dt_bias, ssm_a_log, ssm_d,
                  ssm_norm_g, nsa_cmp_pe, nsa_cmp_w1, nsa_cmp_w2, rel_bias_table, mem_norm_g, w_mem_kv, w_out,
                  final_norm_g)
```

```python
import functools
import math

import numpy as np
import jax
import jax.numpy as jnp
from jax import lax
from jax.experimental import pallas as pl
from jax.experimental.pallas import tpu as pltpu

F32 = jnp.float32
BF16 = jnp.bfloat16

D_MODEL = 1024
GROUP_W = 512
HEAD_DIM = 64
EPS = 1e-6
NEG_INF = -1e30
TINY = 1e-30
LOG2E = math.log2(math.e)

FOX_HEADS = 8
SSM_HEADS = 8
SSM_STATE = 128
SSM_GROUPS = 2
SSM_CONV = 4
SSM_CHUNK = 128
SSM_CONV_DIM = GROUP_W + 2 * SSM_GROUPS * SSM_STATE

NSA_HEADS = 8
NSA_KV_HEADS = 2
NSA_REP = NSA_HEADS // NSA_KV_HEADS
NSA_KV_W = NSA_KV_HEADS * HEAD_DIM
CMP_BLOCK = 32
CMP_STRIDE = 16
CMP_HIDDEN = 2 * HEAD_DIM
SEL_BLOCK = 64
SEL_TOPK = 16
WINDOW = 512
SEL_FORCE = 1e9

MEM_HEADS = 4
MEM_HEAD_DIM = GROUP_W // MEM_HEADS
REL_BUCKETS = 32
REL_MAX_DIST = 128

FOX_COLS = 4 * GROUP_W + FOX_HEADS
SSM_COLS = GROUP_W + SSM_CONV_DIM + SSM_HEADS
NSA_COLS = 2 * GROUP_W + 6 * NSA_KV_W + 3 * NSA_HEADS
MEM_COLS = 2 * GROUP_W

LANES = 128
VMEM_LIMIT = 56 * 1024 * 1024

OFF_FOX_Q, OFF_FOX_K, OFF_FOX_V, OFF_FOX_G = 0, 512, 1024, 1536
OFF_SSM_XBC, OFF_SSM_Z = 2048, 3072
OFF_NSA_Q, OFF_NSA_G = 3584, 4096
OFF_MEM_Q, OFF_MEM_G = 4608, 5120
OFF_SEL_K, OFF_SEL_V, OFF_WIN_K, OFF_WIN_V, OFF_CMP_KV = 5632, 5760, 5888, 6016, 6144
P_COLS = 6400
PS_COLS = LANES
PS_DT_LANE = 24
PS_GATE_LANE = 32

T_ATT = 256
SEL_T, SEL_TK = 512, 256
WIN_T, WIN_TK = 512, 256
N_CMP_PAD = 256


def _params(sem):
    return pltpu.CompilerParams(dimension_semantics=sem, vmem_limit_bytes=VMEM_LIMIT)


def _t5_bucket_np(dist):
    n = np.maximum(dist, 0)
    max_exact = REL_BUCKETS // 2
    nf = np.maximum(n, 1).astype(np.float32)
    large = max_exact + (np.log(nf / np.float32(max_exact)) / np.float32(math.log(REL_MAX_DIST / max_exact))
                         * np.float32(REL_BUCKETS - max_exact)).astype(np.int32)
    large = np.minimum(large, REL_BUCKETS - 1)
    return np.where(n < max_exact, n, large).astype(np.int32)


def _silu(x):
    h = 0.5 * x
    return h + h * jnp.tanh(h)


def _norm_proj_kernel(x_ref, g_ref, *refs, chunk):
    x = x_ref[...]
    ms = jnp.mean(x * x, axis=-1, keepdims=True)
    h = ((x * lax.rsqrt(ms + EPS)) * g_ref[...]).astype(BF16)
    n_out = len(refs) // 2
    for w_ref, o_ref in zip(refs[:n_out], refs[n_out:]):
        ncol = o_ref.shape[1]
        for c0 in range(0, ncol, chunk):
            c1 = min(c0 + chunk, ncol)
            o_ref[:, c0:c1] = jnp.dot(h, w_ref[:, c0:c1], preferred_element_type=F32).astype(o_ref.dtype)


def _norm_proj(x2d, g, weights, out_dtypes, tm=512):
    n, d = x2d.shape
    return pl.pallas_call(
        functools.partial(_norm_proj_kernel, chunk=512),
        grid=(n // tm,),
        in_specs=[pl.BlockSpec((tm, d), lambda i: (i, 0)), pl.BlockSpec((1, d), lambda i: (0, 0))]
        + [pl.BlockSpec(w.shape, lambda i: (0, 0)) for w in weights],
        out_specs=[pl.BlockSpec((tm, w.shape[1]), lambda i: (i, 0)) for w in weights],
        out_shape=[jax.ShapeDtypeStruct((n, w.shape[1]), dt) for w, dt in zip(weights, out_dtypes)],
        compiler_params=_params(("parallel",)),
        name="norm_proj",
    )(x2d, g.reshape(1, d), *weights)


def _t5_table_kernel(tab_ref, bucket_ref, o_ref, *, stride, off, limit):
    h = pl.program_id(0)
    rows, cols = o_ref.shape[1], o_ref.shape[2]
    bucket = bucket_ref[...]
    near = jnp.zeros(bucket.shape, F32)
    for b in range(REL_BUCKETS):
        near = jnp.where(bucket == b, tab_ref[b * NSA_HEADS + h] * LOG2E, near)
    far = tab_ref[(REL_BUCKETS - 1) * NSA_HEADS + h] * LOG2E
    base = jnp.concatenate([near] * (rows // near.shape[0]), axis=0)
    rolled = pltpu.roll(base, 0, 1, stride=stride, stride_axis=0)
    d = (lax.broadcasted_iota(jnp.int32, (rows, cols), 1) - stride * lax.broadcasted_iota(jnp.int32, (rows, cols), 0)
         + off)
    o_ref[0] = jnp.where((d < 0) | (d >= limit), NEG_INF, jnp.where(d >= REL_MAX_DIST, far, rolled))


def _t5_table(tab_flat, rows, cols, stride, off, limit):
    assert cols >= 2 * REL_MAX_DIST and rows % 8 == 0
    k = (np.arange(cols) + off) % cols
    bucket = np.broadcast_to(np.where(k < REL_MAX_DIST, _t5_bucket_np(k), -1).astype(np.int32), (8, cols))
    return pl.pallas_call(
        functools.partial(_t5_table_kernel, stride=stride, off=off, limit=limit),
        grid=(NSA_HEADS,),
        in_specs=[
            pl.BlockSpec(memory_space=pltpu.SMEM),
            pl.BlockSpec((8, cols), lambda h: (0, 0)),
        ],
        out_specs=pl.BlockSpec((1, rows, cols), lambda h: (h, 0, 0)),
        out_shape=jax.ShapeDtypeStruct((NSA_HEADS, rows, cols), F32),
        compiler_params=_params(("arbitrary",)),
        name="t5_table",
    )(tab_flat, jnp.asarray(bucket))


def _tri_lower(n):
    r = lax.broadcasted_iota(jnp.int32, (n, n), 0)
    c = lax.broadcasted_iota(jnp.int32, (n, n), 1)
    return (r >= c).astype(F32)


N_SPLIT = 3


def _fox_aug_lane(h, i):
    return LANES * (h // 2) + (HEAD_DIM if h % 2 == 0 else 0) + i


def _fox_aug_consts():
    pq = np.zeros((LANES, GROUP_W), np.float32)
    pk = np.zeros((LANES, GROUP_W), np.float32)
    oq = np.zeros((1, GROUP_W), np.float32)
    ok = np.zeros((1, GROUP_W), np.float32)
    for h in range(FOX_HEADS):
        for i in range(N_SPLIT):
            pq[i * FOX_HEADS + h, _fox_aug_lane(h, i)] = 1.0
            pk[i * FOX_HEADS + h, _fox_aug_lane(h, N_SPLIT + i)] = -1.0
            oq[0, _fox_aug_lane(h, N_SPLIT + i)] = 1.0
            ok[0, _fox_aug_lane(h, i)] = 1.0
    return pq, pk, oq, ok


def _fox_cumsum_kernel(f_ref, b_ref, pq_ref, pk_ref, oq_ref, ok_ref, qa_ref, ka_ref, carry_ref, *, ts):
    @pl.when(pl.program_id(1) == 0)
    def _():
        carry_ref[...] = jnp.zeros_like(carry_ref)

    z = f_ref[...] + b_ref[...]
    logf = (jnp.minimum(z, 0.0) - jnp.log(1.0 + jnp.exp(-jnp.abs(z)))) * LOG2E
    tri = _tri_lower(LANES).astype(BF16)
    group = lax.broadcasted_iota(jnp.int32, (LANES, LANES), 1) // FOX_HEADS
    carry = carry_ref[...]
    for c in range(ts // LANES):
        rows = slice(c * LANES, (c + 1) * LANES)
        cs, rest = carry, logf[rows]
        for _ in range(N_SPLIT):
            piece = rest.astype(BF16)
            rest = rest - piece.astype(F32)
            cs = cs + jnp.dot(tri, piece, preferred_element_type=F32)
        carry = cs[LANES - 1:LANES, :]
        cat, rest = None, cs
        for i in range(N_SPLIT):
            piece = rest.astype(BF16).astype(F32)
            rest = rest - piece
            cat = piece if cat is None else jnp.where(group == i, piece, cat)
        cat = cat.astype(BF16)
        qa_ref[rows, :] = (oq_ref[...] + jnp.dot(cat, pq_ref[...], preferred_element_type=F32)).astype(BF16)
        ka_ref[rows, :] = (ok_ref[...] + jnp.dot(cat, pk_ref[...], preferred_element_type=F32)).astype(BF16)
    carry_ref[...] = carry


def _fox_cumsum(ps, f_bias_pad, b, s, ts=1024):
    ns = s // ts
    pq, pk, oq, ok = _fox_aug_consts()
    const = lambda bi, si: (0, 0)
    return pl.pallas_call(
        functools.partial(_fox_cumsum_kernel, ts=ts),
        grid=(b, ns),
        in_specs=[
            pl.BlockSpec((ts, LANES), lambda bi, si: (bi * ns + si, 0)),
            pl.BlockSpec((1, LANES), const),
            pl.BlockSpec(pq.shape, const),
            pl.BlockSpec(pk.shape, const),
            pl.BlockSpec(oq.shape, const),
            pl.BlockSpec(ok.shape, const),
        ],
        out_specs=[
            pl.BlockSpec((ts, GROUP_W), lambda bi, si: (bi * ns + si, 0)),
            pl.BlockSpec((ts, GROUP_W), lambda bi, si: (bi * ns + si, 0)),
        ],
        out_shape=[jax.ShapeDtypeStruct((b * s, GROUP_W), BF16), jax.ShapeDtypeStruct((b * s, GROUP_W), BF16)],
        scratch_shapes=[pltpu.VMEM((1, LANES), F32)],
        compiler_params=_params(("parallel", "arbitrary")),
        name="fox_cumsum",
    )(ps, f_bias_pad, jnp.asarray(pq, BF16), jnp.asarray(pk, BF16), jnp.asarray(oq), jnp.asarray(ok))


ONES_ROWS = 16


def _vt_rows(h):
    return slice(0, HEAD_DIM + ONES_ROWS) if h % 2 == 0 else slice(HEAD_DIM - ONES_ROWS, LANES)


def _store_vt(vt_ref, i, rows, v_pair):
    vt = v_pair.astype(F32).T.astype(BF16)
    ones = jnp.ones((HEAD_DIM, v_pair.shape[0]), BF16)
    vt_ref[i, 0:HEAD_DIM, rows] = vt[0:HEAD_DIM]
    vt_ref[i, HEAD_DIM:LANES, rows] = ones
    vt_ref[i + 1, 0:HEAD_DIM, rows] = ones
    vt_ref[i + 1, HEAD_DIM:LANES, rows] = vt[HEAD_DIM:LANES]


def _store_vt_groups(vt_ref, rows, v_groups):
    vt = v_groups.astype(F32).T.astype(BF16)
    ones = jnp.ones((HEAD_DIM, v_groups.shape[0]), BF16)
    for g in range(NSA_KV_HEADS):
        vg = vt[g * HEAD_DIM:(g + 1) * HEAD_DIM]
        vt_ref[2 * g, 0:HEAD_DIM, rows] = vg
        vt_ref[2 * g, HEAD_DIM:LANES, rows] = ones
        vt_ref[2 * g + 1, 0:HEAD_DIM, rows] = ones
        vt_ref[2 * g + 1, HEAD_DIM:LANES, rows] = vg


def _finish_pair(acc_ref, hp):
    ae, ao = acc_ref[2 * hp], acc_ref[2 * hp + 1]
    top = lax.broadcasted_iota(jnp.int32, ae.shape, 0) < HEAD_DIM
    both = jnp.where(top, ae / jnp.maximum(ae[HEAD_DIM:HEAD_DIM + 1, :], TINY),
                     ao / jnp.maximum(ao[HEAD_DIM - 1:HEAD_DIM, :], TINY))
    return both.T


def _swap_halves(x_b):
    return pltpu.roll(x_b.astype(F32), HEAD_DIM, 1).astype(BF16)


def _half_masks(rows):
    lo = jnp.where(lax.broadcasted_iota(jnp.int32, (rows, LANES), 1) < HEAD_DIM, 1.0, 0.0)
    return lo.astype(BF16), (1.0 - lo).astype(BF16)


def _fox_attn_kernel(q_ref, qa_ref, k_ref, ka_ref, v_ref, gate_ref, o_ref, qs_ref, kk_ref, vt_ref, m_ref, acc_ref,
                     pre_ref, *, t, tk, ahead):
    qi = pl.program_id(1)
    lo_q, hi_q = _half_masks(t)
    cm = lax.broadcasted_iota(jnp.int32, (tk, t), 0) - lax.broadcasted_iota(jnp.int32, (tk, t), 1)

    @pl.when(qi == 0)
    def _():
        lo_k, hi_k = _half_masks(tk)

        def merge(j, c):
            rows = pl.ds(pl.multiple_of(j * tk, tk), tk)
            for hp in range(FOX_HEADS // 2):
                ls = slice(hp * LANES, (hp + 1) * LANES)
                kp, ka, vp = k_ref[rows, ls], ka_ref[rows, ls], v_ref[rows, ls]
                kk_ref[2 * hp, rows, :] = kp * lo_k + ka * hi_k
                kk_ref[2 * hp + 1, rows, :] = ka * lo_k + kp * hi_k
                _store_vt(vt_ref, 2 * hp, rows, vp)
            return c

        lax.fori_loop(0, k_ref.shape[0] // tk, merge, 0)

    for hp in range(FOX_HEADS // 2):
        ls = slice(hp * LANES, (hp + 1) * LANES)
        qp, qa = q_ref[:, ls], qa_ref[:, ls]
        qs_ref[2 * hp] = (qp * lo_q + qa * hi_q).astype(F32).T.astype(BF16)
        qs_ref[2 * hp + 1] = (qa * lo_q + qp * hi_q).astype(F32).T.astype(BF16)
    m_ref[...] = jnp.full(m_ref.shape, NEG_INF, F32)
    acc_ref[...] = jnp.zeros_like(acc_ref)

    def rows_of(j):
        return pl.ds(pl.multiple_of(j * tk, tk), tk)

    def scores(j, h):
        return jnp.dot(kk_ref[h, rows_of(j), :], qs_ref[h], preferred_element_type=F32)

    def tile(j, diag, j_next):
        queue = [pre_ref[i] for i in range(ahead)]
        for h in range(FOX_HEADS):
            if h + ahead < FOX_HEADS:
                queue.append(scores(j, h + ahead))
            elif j_next is not None:
                pre_ref[h + ahead - FOX_HEADS] = scores(j_next, h + ahead - FOX_HEADS)
            s = queue.pop(0)
            if diag:
                s = jnp.where(cm <= qi * t - j * tk, s, NEG_INF)
            m_old = m_ref[h]
            m_new = jnp.maximum(m_old, jnp.max(s, axis=0, keepdims=True))
            m_ref[h] = m_new
            p = jnp.exp2(s - m_new).astype(BF16)
            vr = _vt_rows(h)
            acc_ref[h, vr] = jnp.exp2(m_old - m_new) * acc_ref[h, vr] + jnp.dot(vt_ref[h, vr, rows_of(j)], p,
                                                                                preferred_element_type=F32)

    jd = (qi * t) // tk
    n_diag = max(t // tk, 1)
    for i in range(ahead):
        pre_ref[i] = scores(0, i)

    def body(j, c):
        tile(j, False, j + 1)
        return c

    lax.fori_loop(0, jd, body, 0)
    for dj in range(n_diag):
        tile(jd + dj, True, jd + dj + 1 if dj + 1 < n_diag else None)

    for hp in range(FOX_HEADS // 2):
        ls = slice(hp * LANES, (hp + 1) * LANES)
        o_ref[:, ls] = (_finish_pair(acc_ref, hp) * _silu(gate_ref[:, ls].astype(F32))).astype(o_ref.dtype)


def _fox_attn(p, qaug, kaug, b, s, t=512, tk=256, ahead=2):
    nq = s // t
    w = GROUP_W
    return pl.pallas_call(
        functools.partial(_fox_attn_kernel, t=t, tk=tk, ahead=ahead),
        grid=(b, nq),
        in_specs=[
            pl.BlockSpec((t, w), lambda bi, qi: (bi * nq + qi, OFF_FOX_Q // w)),
            pl.BlockSpec((t, w), lambda bi, qi: (bi * nq + qi, 0)),
            pl.BlockSpec((s, w), lambda bi, qi: (bi, OFF_FOX_K // w)),
            pl.BlockSpec((s, w), lambda bi, qi: (bi, 0)),
            pl.BlockSpec((s, w), lambda bi, qi: (bi, OFF_FOX_V // w)),
            pl.BlockSpec((t, w), lambda bi, qi: (bi * nq + qi, OFF_FOX_G // w)),
        ],
        out_specs=pl.BlockSpec((t, w), lambda bi, qi: (bi * nq + qi, 0)),
        out_shape=jax.ShapeDtypeStruct((b * s, w), BF16),
        scratch_shapes=[
            pltpu.VMEM((FOX_HEADS, LANES, t), BF16),
            pltpu.VMEM((FOX_HEADS, s, LANES), BF16),
            pltpu.VMEM((FOX_HEADS, LANES, s), BF16),
            pltpu.VMEM((FOX_HEADS, 1, t), F32),
            pltpu.VMEM((FOX_HEADS, LANES, t), F32),
            pltpu.VMEM((ahead, tk, t), F32),
        ],
        compiler_params=_params(("parallel", "arbitrary")),
        name="fox_attn",
    )(p, qaug, p, kaug, p, p)


def _dot_split(lhs_f32, rhs_b):
    out, rest = None, lhs_f32
    for _ in range(N_SPLIT):
        piece = rest.astype(BF16)
        rest = rest - piece.astype(F32)
        term = jnp.dot(piece, rhs_b, preferred_element_type=F32)
        out = term if out is None else out + term
    return out


def _ssd_kernel(z_ref, xbc_ref, dt_ref, cw_ref, cb_ref, dtb_ref, alog_ref, dsk_ref, ng_ref, o_ref,
                xpad_ref, xc_ref, state_ref, y_ref, *, nch):
    q = SSM_CHUNK
    rows_all = nch * q
    halo = 8

    @pl.when(pl.program_id(1) == 0)
    def _():
        xpad_ref[0:halo, :] = jnp.zeros((halo, SSM_CONV_DIM), F32)
        state_ref[...] = jnp.zeros_like(state_ref)

    xpad_ref[halo:halo + rows_all, :] = xbc_ref[...].astype(F32)
    y = cb_ref[...]
    for k in range(SSM_CONV):
        off = halo - (SSM_CONV - 1) + k
        y = y + cw_ref[k:k + 1, :] * xpad_ref[off:off + rows_all, :]
    xpad_ref[0:halo, :] = xpad_ref[rows_all:rows_all + halo, :]
    xc_ref[...] = _silu(y)

    x_dt = dt_ref[...] + dtb_ref[...]
    dt_all = jnp.maximum(x_dt, 0.0) + jnp.log(1.0 + jnp.exp(-jnp.abs(x_dt)))
    a_all = dt_all * (-jnp.exp(alog_ref[...]))
    tri_t = (lax.broadcasted_iota(jnp.int32, (q, q), 0) <= lax.broadcasted_iota(jnp.int32, (q, q), 1)).astype(BF16)
    er = lax.broadcasted_iota(jnp.int32, (LANES, GROUP_W), 0)
    ec = lax.broadcasted_iota(jnp.int32, (LANES, GROUP_W), 1)
    expand = jnp.where(ec // HEAD_DIM == er - PS_DT_LANE, 1.0, 0.0).astype(BF16)
    row = lax.broadcasted_iota(jnp.int32, (q, q), 0)
    col = lax.broadcasted_iota(jnp.int32, (q, q), 1)
    causal = row >= col
    lo = lax.broadcasted_iota(jnp.int32, (q, LANES), 1) < HEAD_DIM
    gw = GROUP_W // SSM_GROUPS
    hpg = SSM_HEADS // SSM_GROUPS

    for c in range(nch):
        rs = slice(c * q, (c + 1) * q)
        xs = xc_ref[rs, :GROUP_W]
        dt = dt_all[rs]
        acs_t = _dot_split(a_all[rs].T, tri_t)
        acs = acs_t.T
        stacked = jnp.concatenate([dt, jnp.exp(acs), jnp.exp(acs[q - 1:q, :] - acs)], axis=0)
        full = _dot_split(stacked, expand)
        dt_full, eacs_full, dec_full = full[0:q], full[q:2 * q], full[2 * q:3 * q]
        xdt = xs * dt_full
        xdt_b = xdt.astype(BF16)
        xdec_b = (xdt * dec_full).astype(BF16)

        for g in range(SSM_GROUPS):
            bm = xc_ref[rs, GROUP_W + g * SSM_STATE:GROUP_W + (g + 1) * SSM_STATE]
            cm = xc_ref[rs, GROUP_W + (SSM_GROUPS + g) * SSM_STATE:GROUP_W + (SSM_GROUPS + g + 1) * SSM_STATE]
            bm_b = bm.astype(BF16)
            cm_b = cm.astype(BF16)
            gs = slice(g * gw, (g + 1) * gw)
            cbg = lax.dot_general(cm_b, bm_b, (((1,), (1,)), ((), ())), preferred_element_type=F32)
            st = state_ref[:, gs]
            y_off = jnp.dot(cm_b, st.astype(BF16), preferred_element_type=F32) * eacs_full[:, gs]
            cst = jnp.dot(bm.T.astype(BF16), xdec_b[:, gs], preferred_element_type=F32)
            state_ref[:, gs] = st * eacs_full[q - 1:q, gs] + cst
            for hp in range(hpg // 2):
                ls = slice(g * gw + hp * LANES, g * gw + (hp + 1) * LANES)
                yd = []
                for e in range(2):
                    h = g * hpg + 2 * hp + e
                    hl = PS_DT_LANE + h
                    seg = jnp.exp(jnp.where(causal, acs[:, hl:hl + 1] - acs_t[hl:hl + 1, :], NEG_INF))
                    yd.append(jnp.dot((cbg * seg).astype(BF16), xdt_b[:, ls], preferred_element_type=F32))
                y_ref[:, ls] = jnp.where(lo, yd[0], yd[1]) + y_off[:, hp * LANES:(hp + 1) * LANES]

        yz = (y_ref[...] + xs * dsk_ref[...]) * _silu(z_ref[rs, :].astype(F32))
        for g in range(SSM_GROUPS):
            gs = slice(g * gw, (g + 1) * gw)
            blk = yz[:, gs]
            ms = jnp.mean(blk * blk, axis=-1, keepdims=True)
            o_ref[rs, gs] = ((blk * lax.rsqrt(ms + EPS)) * ng_ref[:, gs]).astype(o_ref.dtype)


def _ssd(p, ps, conv_w, conv_b, dt_bias_pad, a_log_pad, d_full, norm_g, b, s, nch=8):
    q = SSM_CHUNK
    rows = nch * q
    nc = s // rows
    row = lambda bi, ci: (bi * nc + ci)
    const = lambda bi, ci: (0, 0)
    return pl.pallas_call(
        functools.partial(_ssd_kernel, nch=nch),
        grid=(b, nc),
        in_specs=[
            pl.BlockSpec((rows, GROUP_W), lambda bi, ci: (row(bi, ci), OFF_SSM_Z // GROUP_W)),
            pl.BlockSpec((rows, SSM_CONV_DIM), lambda bi, ci: (row(bi, ci), OFF_SSM_XBC // SSM_CONV_DIM)),
            pl.BlockSpec((rows, LANES), lambda bi, ci: (row(bi, ci), 0)),
            pl.BlockSpec((SSM_CONV, SSM_CONV_DIM), const),
            pl.BlockSpec((1, SSM_CONV_DIM), const),
            pl.BlockSpec((1, LANES), const),
            pl.BlockSpec((1, LANES), const),
            pl.BlockSpec((1, GROUP_W), const),
            pl.BlockSpec((1, GROUP_W), const),
        ],
        out_specs=pl.BlockSpec((rows, GROUP_W), lambda bi, ci: (row(bi, ci), 0)),
        out_shape=jax.ShapeDtypeStruct((b * s, GROUP_W), BF16),
        scratch_shapes=[
            pltpu.VMEM((rows + 8, SSM_CONV_DIM), F32),
            pltpu.VMEM((rows, SSM_CONV_DIM), F32),
            pltpu.VMEM((SSM_STATE, GROUP_W), F32),
            pltpu.VMEM((q, GROUP_W), F32),
        ],
        compiler_params=_params(("parallel", "arbitrary")),
        name="ssd",
    )(p, p, ps, conv_w, conv_b, dt_bias_pad, a_log_pad, d_full, norm_g)


def _compress_kernel(x_ref, w1_ref, w2_ref, pe_ref, o_ref, ot_ref, xf_ref):
    s = x_ref.shape[0]
    n = s // CMP_STRIDE
    for sb in range(2):
        xf_ref[sb] = x_ref[:, sb * LANES:(sb + 1) * LANES].astype(F32)
    o_ref[...] = jnp.zeros_like(o_ref)
    ot_ref[...] = jnp.zeros_like(ot_ref)
    for sb in range(2):
        first = jnp.zeros((n, 2 * CMP_HIDDEN), F32)
        second = jnp.zeros((n, 2 * CMP_HIDDEN), F32)
        for l in range(CMP_STRIDE):
            xl = xf_ref[sb, pl.ds(l, n, stride=CMP_STRIDE), :]
            first = first + jnp.dot((xl + pe_ref[sb, l:l + 1, :]).astype(BF16), w1_ref[sb, l],
                                    preferred_element_type=F32)
            second = second + jnp.dot((xl + pe_ref[sb, CMP_STRIDE + l:CMP_STRIDE + l + 1, :]).astype(BF16),
                                      w1_ref[sb, CMP_STRIDE + l], preferred_element_type=F32)
        h = _silu(first + pltpu.roll(second, n - 1, 0))
        o = jnp.dot(h.astype(BF16), w2_ref[sb], preferred_element_type=F32)
        for e in range(2):
            oe = o[:, e * LANES:(e + 1) * LANES]
            o_ref[0, 2 * sb + e, 0:n, :] = oe.astype(o_ref.dtype)
            ot_ref[0, 2 * sb + e, :, 0:n] = oe.T.astype(ot_ref.dtype)


def _blockdiag2(a):
    z = jnp.zeros_like(a)
    return jnp.concatenate([jnp.concatenate([a, z], axis=-1), jnp.concatenate([z, a], axis=-1)], axis=-2)


def _compress(p, cmp_w1, cmp_w2, cmp_pe, b, s):
    nslot = 2 * NSA_KV_HEADS
    w1 = _blockdiag2(cmp_w1.astype(BF16).reshape(2, CMP_BLOCK, HEAD_DIM, CMP_HIDDEN))
    w2 = _blockdiag2(jnp.concatenate([cmp_w2, cmp_w2], axis=-1).astype(BF16))
    pe = jnp.concatenate([cmp_pe, cmp_pe], axis=-1).astype(F32)
    kw = 2 * NSA_KV_W
    return pl.pallas_call(
        _compress_kernel,
        grid=(b,),
        in_specs=[
            pl.BlockSpec((s, kw), lambda bi: (bi, OFF_CMP_KV // kw)),
            pl.BlockSpec(w1.shape, lambda bi: (0, 0, 0, 0)),
            pl.BlockSpec(w2.shape, lambda bi: (0, 0, 0)),
            pl.BlockSpec(pe.shape, lambda bi: (0, 0, 0)),
        ],
        out_specs=[pl.BlockSpec((1, nslot, N_CMP_PAD, LANES), lambda bi: (bi, 0, 0, 0)),
                   pl.BlockSpec((1, nslot, LANES, N_CMP_PAD), lambda bi: (bi, 0, 0, 0))],
        out_shape=[jax.ShapeDtypeStruct((b, nslot, N_CMP_PAD, LANES), BF16),
                   jax.ShapeDtypeStruct((b, nslot, LANES, N_CMP_PAD), BF16)],
        scratch_shapes=[pltpu.VMEM((2, s, LANES), F32)],
        compiler_params=_params(("parallel",)),
        name="nsa_compress",
    )(p, w1, w2, pe)


def _cmp_select_kernel(q_ref, kv_ref, vt_ref, gl_ref, gate_ref, fc_ref, ovt_ref, o_ref, ns_ref):
    t = T_ATT
    ncp = kv_ref.shape[2]
    qi = pl.program_id(1)
    t0 = qi * t
    lo = lax.broadcasted_iota(jnp.int32, (t, LANES), 1) < HEAD_DIM
    zero = jnp.zeros((t, LANES), BF16)
    start = pl.multiple_of(ncp - qi * (t // CMP_STRIDE), CMP_STRIDE)
    gl = gl_ref[...]

    nsel = SEL_BLOCK
    jrow = lax.broadcasted_iota(jnp.int32, (nsel, t), 0)
    cur = (t0 + lax.broadcasted_iota(jnp.int32, (nsel, t), 1)) // SEL_BLOCK
    forced = (jrow == 0) | (jrow == cur) | (jrow == cur - 1)
    past = jrow <= cur

    for g in range(NSA_KV_HEADS):
        kc = kv_ref[0, g]
        vct = vt_ref[0, NSA_KV_HEADS + g]
        psum = jnp.zeros((ncp, t), F32)
        outs = []

        def scores(r):
            h = g * NSA_REP + r
            qp = q_ref[:, (h // 2) * LANES:(h // 2 + 1) * LANES]
            qe = jnp.where(lo, qp, zero) if r % 2 == 0 else jnp.where(lo, zero, qp)
            return lax.dot_general(kc, qe, (((1,), (1,)), ((), ())), preferred_element_type=F32)

        queue = [scores(0)]
        for r in range(NSA_REP):
            h = g * NSA_REP + r
            if r + 1 < NSA_REP:
                queue.append(scores(r + 1))
            s = fc_ref[h, pl.ds(start, ncp), :] + queue.pop(0)
            m = jnp.maximum(jnp.max(s, axis=0, keepdims=True), 0.1 * NEG_INF)
            e = jnp.exp2(s - m)
            pr = e * (1.0 / jnp.maximum(jnp.sum(e, axis=0, keepdims=True), TINY))
            psum = psum + pr
            outs.append(jnp.dot(vct, pr.astype(BF16), preferred_element_type=F32))
        top = lax.broadcasted_iota(jnp.int32, (LANES, t), 0) < HEAD_DIM
        for hp in range(NSA_REP // 2):
            ls = slice((g * NSA_REP // 2 + hp) * LANES, (g * NSA_REP // 2 + hp + 1) * LANES)
            c0 = PS_GATE_LANE + g * NSA_REP + 2 * hp
            o = jnp.where(top, outs[2 * hp], outs[2 * hp + 1]).T
            branch = jnp.where(lo, jax.nn.sigmoid(gl[:, c0:c0 + 1]), jax.nn.sigmoid(gl[:, c0 + 1:c0 + 2]))
            o_ref[:, ls] = (o * branch * _silu(gate_ref[:, ls].astype(F32))).astype(o_ref.dtype)

        imp_t = jnp.zeros((nsel, t), F32)
        rest = psum
        for _ in range(N_SPLIT):
            piece = rest.astype(BF16)
            rest = rest - piece.astype(F32)
            imp_t = imp_t + jnp.dot(ovt_ref[...], piece, preferred_element_type=F32)
        imp_t = jnp.where(past, jnp.where(forced, SEL_FORCE, imp_t), -SEL_FORCE)
        sub = 8
        sub_row = lax.broadcasted_iota(jnp.int32, (sub, t), 0)
        rows = [imp_t[k * sub:(k + 1) * sub] for k in range(nsel // sub)]
        rank = [jnp.zeros((sub, t), F32) for _ in rows]
        for i in range(nsel):
            bi = imp_t[i:i + 1, :]
            for k, x in enumerate(rows):
                if k * sub > i:
                    ahead = jnp.where(bi >= x, 1.0, 0.0)
                elif k * sub + sub - 1 <= i:
                    ahead = jnp.where(bi > x, 1.0, 0.0)
                else:
                    ahead = jnp.where(sub_row > i - k * sub, jnp.where(bi >= x, 1.0, 0.0),
                                      jnp.where(bi > x, 1.0, 0.0))
                rank[k] = rank[k] + ahead
        rank = jnp.concatenate(rank, axis=0)
        notsel = jnp.where((rank < float(SEL_TOPK)) & past, 0.0, 1.0)
        ns2 = jnp.concatenate([notsel, notsel], axis=0).T
        ns_ref[:, g * LANES:(g + 1) * LANES] = ns2.astype(ns_ref.dtype)


def _cmp_select(p, ps, kvc, kvc_t, fc, overlap_t, b, s):
    t = T_ATT
    nq = s // t
    w = GROUP_W
    ncp = kvc.shape[2]
    return pl.pallas_call(
        _cmp_select_kernel,
        grid=(b, nq),
        in_specs=[
            pl.BlockSpec((t, w), lambda bi, qi: (bi * nq + qi, OFF_NSA_Q // w)),
            pl.BlockSpec((1, 2 * NSA_KV_HEADS, ncp, LANES), lambda bi, qi: (bi, 0, 0, 0)),
            pl.BlockSpec((1, 2 * NSA_KV_HEADS, LANES, ncp), lambda bi, qi: (bi, 0, 0, 0)),
            pl.BlockSpec((t, LANES), lambda bi, qi: (bi * nq + qi, 0)),
            pl.BlockSpec((t, w), lambda bi, qi: (bi * nq + qi, OFF_NSA_G // w)),
            pl.BlockSpec((NSA_HEADS, 2 * ncp, t), lambda bi, qi: (0, 0, 0)),
            pl.BlockSpec((SEL_BLOCK, ncp), lambda bi, qi: (0, 0)),
        ],
        out_specs=[
            pl.BlockSpec((t, w), lambda bi, qi: (bi * nq + qi, 0)),
            pl.BlockSpec((t, NSA_KV_HEADS * LANES), lambda bi, qi: (bi * nq + qi, 0)),
        ],
        out_shape=[jax.ShapeDtypeStruct((b * s, w), BF16),
                   jax.ShapeDtypeStruct((b * s, NSA_KV_HEADS * LANES), BF16)],
        compiler_params=_params(("parallel", "arbitrary")),
        name="nsa_cmp_select",
    )(p, kvc, kvc_t, ps, p, fc, overlap_t)


def _sel_attn_kernel(tab_ref, q_ref, ns_ref, k_ref, v_ref, gl_ref, gate_ref, bs_ref, o_ref,
                     qa_ref, kk_ref, vt_ref, m_ref, acc_ref, pre_ref, *, t, tk, ahead):
    qi = pl.program_id(1)
    lo_q, hi_q = _half_masks(t)
    lo = lax.broadcasted_iota(jnp.int32, (t, LANES), 1) < HEAD_DIM
    n_near = t // tk + 1

    @pl.when(qi == 0)
    def _():
        lane = lax.broadcasted_iota(jnp.int32, (tk, LANES), 1)
        krow = lax.broadcasted_iota(jnp.int32, (tk, LANES), 0)
        lo_f = jnp.where(lane < HEAD_DIM, 1.0, 0.0)
        hi_f = 1.0 - lo_f
        lo_k, hi_k = lo_f.astype(BF16), hi_f.astype(BF16)

        def merge(j, c):
            ks = pl.multiple_of(j * tk, tk)
            rows = pl.ds(ks, tk)
            hot = jnp.where((lane % HEAD_DIM) == (ks + krow) // SEL_BLOCK, NEG_INF, 0.0)
            hot_lo, hot_hi = (hot * lo_f).astype(BF16), (hot * hi_f).astype(BF16)
            k01 = k_ref[rows, :]
            k10 = _swap_halves(k01)
            kk_ref[0, rows, :] = k01 * lo_k + hot_hi
            kk_ref[1, rows, :] = hot_lo + k10 * hi_k
            kk_ref[2, rows, :] = k10 * lo_k + hot_hi
            kk_ref[3, rows, :] = hot_lo + k01 * hi_k
            _store_vt_groups(vt_ref, rows, v_ref[rows, :])
            return c

        lax.fori_loop(0, k_ref.shape[0] // tk, merge, 0)

    for h in range(NSA_HEADS):
        g = h // NSA_REP
        qp = q_ref[:, (h // 2) * LANES:(h // 2 + 1) * LANES]
        ns = ns_ref[:, g * LANES:(g + 1) * LANES]
        qa = qp * lo_q + ns * hi_q if h % 2 == 0 else ns * lo_q + qp * hi_q
        qa_ref[h] = qa.astype(F32).T.astype(BF16)
    m_ref[...] = jnp.full(m_ref.shape, NEG_INF, F32)
    acc_ref[...] = jnp.zeros_like(acc_ref)

    kv = lambda h: 2 * (h // NSA_REP) + h % 2

    def rows_of(j):
        return pl.ds(pl.multiple_of(j * tk, tk), tk)

    def scores(j, h):
        return jnp.dot(kk_ref[kv(h), rows_of(j), :], qa_ref[h], preferred_element_type=F32)

    def tile(j, near, j_next):
        rows = rows_of(j)
        queue = [pre_ref[i] for i in range(ahead)]
        for h in range(NSA_HEADS):
            if h + ahead < NSA_HEADS:
                queue.append(scores(j, h + ahead))
            elif j_next is not None:
                pre_ref[h + ahead - NSA_HEADS] = scores(j_next, h + ahead - NSA_HEADS)
            s = queue.pop(0)
            m_old = m_ref[h]
            if near is None:
                far = tab_ref[(REL_BUCKETS - 1) * NSA_HEADS + h] * LOG2E
                m_new = jnp.maximum(m_old, jnp.max(s, axis=0, keepdims=True) + far)
                p = jnp.exp2(s - (m_new - far)).astype(BF16)
            else:
                s = bs_ref[h, near * tk:(near + 1) * tk, :] + s
                m_new = jnp.maximum(m_old, jnp.max(s, axis=0, keepdims=True))
                p = jnp.exp2(s - m_new).astype(BF16)
            m_ref[h] = m_new
            vr = _vt_rows(h)
            acc_ref[h, vr] = jnp.exp2(m_old - m_new) * acc_ref[h, vr] + jnp.dot(vt_ref[kv(h), vr, rows], p,
                                                                                preferred_element_type=F32)

    jd = (qi * t) // tk
    for i in range(ahead):
        pre_ref[i] = scores(0, i)

    def body(j, c):
        tile(j, None, j + 1)
        return c

    lax.fori_loop(0, jnp.maximum(jd - 1, 0), body, 0)

    @pl.when(qi > 0)
    def _():
        tile(jd - 1, 0, jd)

    for dj in range(n_near - 1):
        tile(jd + dj, 1 + dj, jd + dj + 1 if dj + 2 < n_near else None)

    gl = gl_ref[...]
    for hp in range(NSA_HEADS // 2):
        ls = slice(hp * LANES, (hp + 1) * LANES)
        c0 = PS_GATE_LANE + NSA_HEADS + 2 * hp
        branch = jnp.where(lo, jax.nn.sigmoid(gl[:, c0:c0 + 1]), jax.nn.sigmoid(gl[:, c0 + 1:c0 + 2]))
        o_ref[:, ls] = (_finish_pair(acc_ref, hp) * branch * _silu(gate_ref[:, ls].astype(F32))).astype(o_ref.dtype)


def _sel_attn(tab_flat, p, ps, notsel, bsel, b, s, t=512, tk=256, ahead=2):
    nq = s // t
    w = GROUP_W
    kw = NSA_KV_HEADS * LANES
    n_near = t // tk + 1
    return pl.pallas_call(
        functools.partial(_sel_attn_kernel, t=t, tk=tk, ahead=ahead),
        grid=(b, nq),
        in_specs=[
            pl.BlockSpec(memory_space=pltpu.SMEM),
            pl.BlockSpec((t, w), lambda bi, qi: (bi * nq + qi, OFF_NSA_Q // w)),
            pl.BlockSpec((t, kw), lambda bi, qi: (bi * nq + qi, 0)),
            pl.BlockSpec((s, NSA_KV_W), lambda bi, qi: (bi, OFF_SEL_K // NSA_KV_W)),
            pl.BlockSpec((s, NSA_KV_W), lambda bi, qi: (bi, OFF_SEL_V // NSA_KV_W)),
            pl.BlockSpec((t, LANES), lambda bi, qi: (bi * nq + qi, 0)),
            pl.BlockSpec((t, w), lambda bi, qi: (bi * nq + qi, OFF_NSA_G // w)),
            pl.BlockSpec((NSA_HEADS, n_near * tk, t), lambda bi, qi: (0, 0, 0), pipeline_mode=pl.Buffered(1)),
        ],
        out_specs=pl.BlockSpec((t, w), lambda bi, qi: (bi * nq + qi, 0)),
        out_shape=jax.ShapeDtypeStruct((b * s, w), BF16),
        scratch_shapes=[
            pltpu.VMEM((NSA_HEADS, LANES, t), BF16),
            pltpu.VMEM((2 * NSA_KV_HEADS, s, LANES), BF16),
            pltpu.VMEM((2 * NSA_KV_HEADS, LANES, s), BF16),
            pltpu.VMEM((NSA_HEADS, 1, t), F32),
            pltpu.VMEM((NSA_HEADS, LANES, t), F32),
            pltpu.VMEM((ahead, tk, t), F32),
        ],
        compiler_params=_params(("parallel", "arbitrary")),
        name="nsa_sel_attn",
    )(tab_flat, p, notsel, p, p, ps, p, bsel)


def _win_attn_kernel(q_ref, k_ref, v_ref, gl_ref, gate_ref, bw_ref, o_ref, qe_ref, kk_ref, vt_ref, m_ref, acc_ref,
                     pre_ref, *, t, tk, ahead):
    qi = pl.program_id(1)
    lo_q, hi_q = _half_masks(t)
    lo = lax.broadcasted_iota(jnp.int32, (t, LANES), 1) < HEAD_DIM
    n_before = WINDOW // tk
    n_tiles = n_before + t // tk

    @pl.when(qi == 0)
    def _():
        lo_k, hi_k = _half_masks(tk)

        def merge(j, c):
            rows = pl.ds(pl.multiple_of(j * tk, tk), tk)
            k01 = k_ref[rows, :]
            k10 = _swap_halves(k01)
            kk_ref[0, rows, :] = k01 * lo_k + k10 * hi_k
            kk_ref[1, rows, :] = k10 * lo_k + k01 * hi_k
            _store_vt_groups(vt_ref, rows, v_ref[rows, :])
            return c

        lax.fori_loop(0, v_ref.shape[0] // tk, merge, 0)

    for h in range(NSA_HEADS):
        qp = q_ref[:, (h // 2) * LANES:(h // 2 + 1) * LANES]
        qe_ref[h] = qp * lo_q if h % 2 == 0 else qp * hi_q
    m_ref[...] = jnp.full(m_ref.shape, NEG_INF, F32)
    acc_ref[...] = jnp.zeros_like(acc_ref)

    def rows_of(w):
        return pl.ds(pl.multiple_of((qi * (t // tk) - n_before + w) * tk, tk), tk)

    def scores(w, h):
        return lax.dot_general(kk_ref[h // NSA_REP, rows_of(w), :], qe_ref[h], (((1,), (1,)), ((), ())),
                               preferred_element_type=F32)

    def tile(w):
        rows = rows_of(w)
        queue = [pre_ref[i] for i in range(ahead)]
        for h in range(NSA_HEADS):
            if h + ahead < NSA_HEADS:
                queue.append(scores(w, h + ahead))
            elif w + 1 < n_tiles:
                pre_ref[h + ahead - NSA_HEADS] = scores(w + 1, h + ahead - NSA_HEADS)
            s = bw_ref[h, w * tk:(w + 1) * tk, :] + queue.pop(0)
            m_old = m_ref[h]
            m_new = jnp.maximum(m_old, jnp.max(s, axis=0, keepdims=True))
            m_ref[h] = m_new
            p = jnp.exp2(s - m_new).astype(BF16)
            vr = _vt_rows(h)
            acc_ref[h, vr] = jnp.exp2(m_old - m_new) * acc_ref[h, vr] + jnp.dot(
                vt_ref[2 * (h // NSA_REP) + h % 2, vr, rows], p, preferred_element_type=F32)

    first = jnp.maximum(n_before - qi * (t // tk), 0)
    for i in range(ahead):
        pre_ref[i] = scores(first, i)
    per_q = t // tk
    first_qi = [-(-(n_before - w) // per_q) if w < n_before else 0 for w in range(n_tiles)]
    for need in sorted(set(first_qi), reverse=True):
        group = [w for w in range(n_tiles) if first_qi[w] == need]

        def run(group=group):
            for w in group:
                tile(w)

        if need > 0:
            pl.when(qi >= need)(run)
        else:
            run()

    gl = gl_ref[...]
    for hp in range(NSA_HEADS // 2):
        ls = slice(hp * LANES, (hp + 1) * LANES)
        c0 = PS_GATE_LANE + 2 * NSA_HEADS + 2 * hp
        branch = jnp.where(lo, jax.nn.sigmoid(gl[:, c0:c0 + 1]), jax.nn.sigmoid(gl[:, c0 + 1:c0 + 2]))
        o_ref[:, ls] = (_finish_pair(acc_ref, hp) * branch * _silu(gate_ref[:, ls].astype(F32))).astype(o_ref.dtype)


def _win_attn(p, ps, bwin, b, s, t=WIN_T, tk=WIN_TK, ahead=2):
    assert t % tk == 0 and WINDOW % tk == 0
    nq = s // t
    w = GROUP_W
    return pl.pallas_call(
        functools.partial(_win_attn_kernel, t=t, tk=tk, ahead=ahead),
        grid=(b, nq),
        in_specs=[
            pl.BlockSpec((t, w), lambda bi, qi: (bi * nq + qi, OFF_NSA_Q // w)),
            pl.BlockSpec((s, NSA_KV_W), lambda bi, qi: (bi, OFF_WIN_K // NSA_KV_W)),
            pl.BlockSpec((s, NSA_KV_W), lambda bi, qi: (bi, OFF_WIN_V // NSA_KV_W)),
            pl.BlockSpec((t, LANES), lambda bi, qi: (bi * nq + qi, 0)),
            pl.BlockSpec((t, w), lambda bi, qi: (bi * nq + qi, OFF_NSA_G // w)),
            pl.BlockSpec((NSA_HEADS, WINDOW + t, t), lambda bi, qi: (0, 0, 0), pipeline_mode=pl.Buffered(1)),
        ],
        out_specs=pl.BlockSpec((t, w), lambda bi, qi: (bi * nq + qi, 0)),
        out_shape=jax.ShapeDtypeStruct((b * s, w), BF16),
        scratch_shapes=[
            pltpu.VMEM((NSA_HEADS, t, LANES), BF16),
            pltpu.VMEM((NSA_KV_HEADS, s, LANES), BF16),
            pltpu.VMEM((2 * NSA_KV_HEADS, LANES, s), BF16),
            pltpu.VMEM((NSA_HEADS, 1, t), F32),
            pltpu.VMEM((NSA_HEADS, LANES, t), F32),
            pltpu.VMEM((ahead, tk, t), F32),
        ],
        compiler_params=_params(("parallel", "arbitrary")),
        name="nsa_win_attn",
    )(p, p, p, ps, p, bwin)


def _mem_attn_kernel(q_ref, gate_ref, kv_ref, o_ref):
    scale = MEM_HEAD_DIM ** -0.5
    for h in range(MEM_HEADS):
        ls = slice(h * LANES, (h + 1) * LANES)
        k = kv_ref[:, ls]
        v = kv_ref[:, GROUP_W + h * LANES:GROUP_W + (h + 1) * LANES]
        s = lax.dot_general(q_ref[:, ls], k, (((1,), (1,)), ((), ())), preferred_element_type=F32) * scale
        m = jnp.max(s, axis=1, keepdims=True)
        e = jnp.exp(s - m)
        l = jnp.sum(e, axis=1, keepdims=True)
        o = jnp.dot(e.astype(BF16), v, preferred_element_type=F32) / l
        o_ref[:, ls] = (o * _silu(gate_ref[:, ls].astype(F32))).astype(o_ref.dtype)


def _mem_attn(p, mem_kv, b, s, t=1024):
    nq = s // t
    w = GROUP_W
    m = mem_kv.shape[0] // b
    return pl.pallas_call(
        _mem_attn_kernel,
        grid=(b, nq),
        in_specs=[
            pl.BlockSpec((t, w), lambda bi, qi: (bi * nq + qi, OFF_MEM_Q // w)),
            pl.BlockSpec((t, w), lambda bi, qi: (bi * nq + qi, OFF_MEM_G // w)),
            pl.BlockSpec((m, 2 * w), lambda bi, qi: (bi, 0)),
        ],
        out_specs=pl.BlockSpec((t, w), lambda bi, qi: (bi * nq + qi, 0)),
        out_shape=jax.ShapeDtypeStruct((b * s, w), BF16),
        compiler_params=_params(("parallel", "arbitrary")),
        name="mem_attn",
    )(p, p, mem_kv)


def _out_proj_kernel(x_ref, of_ref, os_ref, oc_ref, osel_ref, ow_ref, om_ref, w_ref, g_ref, o_ref, *, final):
    w = GROUP_W
    nsa = (oc_ref[...].astype(F32) + osel_ref[...].astype(F32) + ow_ref[...].astype(F32)).astype(BF16)
    acc = x_ref[...]
    for i, part in enumerate((of_ref[...], os_ref[...], nsa, om_ref[...])):
        acc = acc + jnp.dot(part, w_ref[i * w:(i + 1) * w, :], preferred_element_type=F32)
    if final:
        ms = jnp.mean(acc * acc, axis=-1, keepdims=True)
        acc = (acc * lax.rsqrt(ms + EPS)) * g_ref[...]
    o_ref[...] = acc


def _out_proj(x2d, parts, w_out, g, final, tm=1024):
    n, d = x2d.shape
    w = GROUP_W
    part_spec = pl.BlockSpec((tm, w), lambda i: (i, 0))
    return pl.pallas_call(
        functools.partial(_out_proj_kernel, final=final),
        grid=(n // tm,),
        in_specs=[pl.BlockSpec((tm, d), lambda i: (i, 0))] + [part_spec] * 6 + [
            pl.BlockSpec((4 * w, d), lambda i: (0, 0)),
            pl.BlockSpec((1, d), lambda i: (0, 0)),
        ],
        out_specs=pl.BlockSpec((tm, d), lambda i: (i, 0)),
        out_shape=jax.ShapeDtypeStruct((n, d), F32),
        compiler_params=_params(("parallel",)),
        name="out_proj",
    )(x2d, *parts, w_out, g.reshape(1, d))


def _pack_in_proj(w_in_l):
    fox, ssm, nsa, mem = 0, FOX_COLS, FOX_COLS + SSM_COLS, FOX_COLS + SSM_COLS + NSA_COLS
    w = GROUP_W
    q_scale = HEAD_DIM ** -0.5 * LOG2E
    cols = lambda a, n: w_in_l[:, a:a + n]
    kv = lambda slot: nsa + w + slot * NSA_KV_W
    main = ([cols(fox, w) * q_scale, cols(fox + w, 3 * w)]
            + [cols(ssm + w, SSM_CONV_DIM), cols(ssm, w)]
            + [cols(nsa, w) * q_scale, cols(nsa + w + 6 * NSA_KV_W + 3 * NSA_HEADS, w)]
            + [cols(mem, 2 * w)]
            + [cols(kv(2), 4 * NSA_KV_W)]
            + [cols(kv(0), 2 * NSA_KV_W)])
    w_main = jnp.concatenate(main, axis=1).astype(BF16)
    assert w_main.shape[1] == P_COLS
    small = ([cols(fox + 4 * w, FOX_HEADS)] * N_SPLIT
             + [cols(ssm + w + SSM_CONV_DIM, SSM_HEADS)]
             + [cols(nsa + w + 6 * NSA_KV_W, 3 * NSA_HEADS)])
    assert N_SPLIT * FOX_HEADS == PS_DT_LANE and PS_DT_LANE + SSM_HEADS == PS_GATE_LANE
    used = PS_GATE_LANE + 3 * NSA_HEADS
    small.append(jnp.zeros((w_in_l.shape[0], PS_COLS - used), w_in_l.dtype))
    return w_main, jnp.concatenate(small, axis=1).astype(BF16)


def _pad_lanes(v, first=0):
    return jnp.pad(v.astype(F32), (first, LANES - first - v.shape[0])).reshape(1, LANES)


def _trunk(x, mem, norm_g, w_in, fox_f_bias, ssm_conv_w, ssm_conv_b, ssm_dt_bias, ssm_a_log, ssm_d,
           ssm_norm_g, nsa_cmp_pe, nsa_cmp_w1, nsa_cmp_w2, rel_bias_table, mem_norm_g, w_mem_kv, w_out,
           final_norm_g):
    b, s, d = x.shape
    depth = w_in.shape[0]
    n = b * s
    m_tok = mem.shape[1]
    n_cmp = (s - CMP_BLOCK) // CMP_STRIDE + 1
    n_rows = s // CMP_STRIDE
    assert s % 1024 == 0 and s // SEL_BLOCK <= HEAD_DIM and n_rows <= N_CMP_PAD and d == D_MODEL

    tab_flat = rel_bias_table.astype(F32).reshape(-1)
    unbounded = 1 << 30
    bwin = _t5_table(tab_flat, WINDOW + WIN_T, WIN_T, 1, WINDOW, WINDOW)
    bsel = _t5_table(tab_flat, (SEL_T // SEL_TK + 1) * SEL_TK, SEL_T, 1, SEL_TK, unbounded)
    fcmp = _t5_table(tab_flat, 2 * N_CMP_PAD, T_ATT, CMP_STRIDE, CMP_STRIDE * N_CMP_PAD - (CMP_BLOCK - 1), unbounded)

    cs = np.arange(N_CMP_PAD)[None, :] * CMP_STRIDE
    js = np.arange(SEL_BLOCK)[:, None] * SEL_BLOCK
    overlap_t = ((cs < js + SEL_BLOCK) & (cs + CMP_BLOCK > js) & (np.arange(N_CMP_PAD)[None, :] < n_cmp)
                 & (np.arange(SEL_BLOCK)[:, None] < s // SEL_BLOCK)).astype(np.float32)
    overlap_t = jnp.asarray(overlap_t, BF16)

    x2d = x.reshape(n, d)
    mem2d = mem.reshape(b * m_tok, d)
    for l in range(depth):
        w_main, w_small = _pack_in_proj(w_in[l])
        p, ps = _norm_proj(x2d, norm_g[l], (w_main, w_small), (BF16, F32))

        qaug, kaug = _fox_cumsum(ps, _pad_lanes(jnp.tile(fox_f_bias[l], N_SPLIT)), b, s)
        o_fox = _fox_attn(p, qaug, kaug, b, s)

        o_ssd = _ssd(p, ps, ssm_conv_w[l].astype(F32), ssm_conv_b[l].reshape(1, -1).astype(F32),
                     _pad_lanes(ssm_dt_bias[l], PS_DT_LANE), _pad_lanes(ssm_a_log[l], PS_DT_LANE),
                     jnp.repeat(ssm_d[l].astype(F32), HEAD_DIM).reshape(1, GROUP_W),
                     ssm_norm_g[l].reshape(1, GROUP_W).astype(F32), b, s)

        kv_cmp, kv_cmp_t = _compress(p, nsa_cmp_w1[l], nsa_cmp_w2[l], nsa_cmp_pe[l], b, s)
        o_cmp, notsel = _cmp_select(p, ps, kv_cmp, kv_cmp_t, fcmp, overlap_t, b, s)
        o_sel = _sel_attn(tab_flat, p, ps, notsel, bsel, b, s)
        o_win = _win_attn(p, ps, bwin, b, s)

        w_kv = w_mem_kv[l].astype(BF16)
        (mem_kv,) = _norm_proj(mem2d, mem_norm_g[l], (w_kv,), (BF16,), tm=min(512, b * m_tok))
        o_mem = _mem_attn(p, mem_kv, b, s)

        x2d = _out_proj(x2d, (o_fox, o_ssd, o_cmp, o_sel, o_win, o_mem), w_out[l].astype(BF16),
                        final_norm_g, final=(l == depth - 1))
    return x2d.reshape(b, s, d)


def kernel(x, mem, norm_g, w_in, fox_f_bias, ssm_conv_w, ssm_conv_b, ssm_dt_bias, ssm_a_log, ssm_d, ssm_norm_g,
           nsa_cmp_pe, nsa_cmp_w1, nsa_cmp_w2, rel_bias_table, mem_norm_g, w_mem_kv, w_out, final_norm_g):
    return _trunk(x, mem, norm_g, w_in, fox_f_bias, ssm_conv_w, ssm_conv_b, ssm_dt_bias, ssm_a_log, ssm_d,
                  ssm_norm_g, nsa_cmp_pe, nsa_cmp_w1, nsa_cmp_w2, rel_bias_table, mem_norm_g, w_mem_kv, w_out,
                  final_norm_g)
```

```python
import functools
import math

import numpy as np
import jax
import jax.numpy as jnp
from jax import lax
from jax.experimental import pallas as pl
from jax.experimental.pallas import tpu as pltpu

F32 = jnp.float32
BF16 = jnp.bfloat16

D_MODEL = 1024
GROUP_W = 512
HEAD_DIM = 64
EPS = 1e-6
NEG_INF = -1e30
TINY = 1e-30
LOG2E = math.log2(math.e)

FOX_HEADS = 8
SSM_HEADS = 8
SSM_STATE = 128
SSM_GROUPS = 2
SSM_CONV = 4
SSM_CHUNK = 128
SSM_CONV_DIM = GROUP_W + 2 * SSM_GROUPS * SSM_STATE

NSA_HEADS = 8
NSA_KV_HEADS = 2
NSA_REP = NSA_HEADS // NSA_KV_HEADS
NSA_KV_W = NSA_KV_HEADS * HEAD_DIM
CMP_BLOCK = 32
CMP_STRIDE = 16
CMP_HIDDEN = 2 * HEAD_DIM
SEL_BLOCK = 64
SEL_TOPK = 16
WINDOW = 512
SEL_FORCE = 1e9

MEM_HEADS = 4
MEM_HEAD_DIM = GROUP_W // MEM_HEADS
REL_BUCKETS = 32
REL_MAX_DIST = 128

FOX_COLS = 4 * GROUP_W + FOX_HEADS
SSM_COLS = GROUP_W + SSM_CONV_DIM + SSM_HEADS
NSA_COLS = 2 * GROUP_W + 6 * NSA_KV_W + 3 * NSA_HEADS
MEM_COLS = 2 * GROUP_W

LANES = 128
VMEM_LIMIT = 56 * 1024 * 1024

OFF_FOX_Q, OFF_FOX_K, OFF_FOX_V, OFF_FOX_G = 0, 512, 1024, 1536
OFF_SSM_XBC, OFF_SSM_Z = 2048, 3072
OFF_NSA_Q, OFF_NSA_G = 3584, 4096
OFF_MEM_Q, OFF_MEM_G = 4608, 5120
OFF_SEL_K, OFF_SEL_V, OFF_WIN_K, OFF_WIN_V, OFF_CMP_KV = 5632, 5760, 5888, 6016, 6144
P_COLS = 6400
PS_COLS = LANES
PS_DT_LANE = 24
PS_GATE_LANE = 32

T_ATT = 256
SEL_T, SEL_TK = 512, 256
WIN_T, WIN_TK = 512, 256
N_CMP_PAD = 256


def _params(sem):
    return pltpu.CompilerParams(dimension_semantics=sem, vmem_limit_bytes=VMEM_LIMIT)


def _t5_bucket_np(dist):
    n = np.maximum(dist, 0)
    max_exact = REL_BUCKETS // 2
    nf = np.maximum(n, 1).astype(np.float32)
    large = max_exact + (np.log(nf / np.float32(max_exact)) / np.float32(math.log(REL_MAX_DIST / max_exact))
                         * np.float32(REL_BUCKETS - max_exact)).astype(np.int32)
    large = np.minimum(large, REL_BUCKETS - 1)
    return np.where(n < max_exact, n, large).astype(np.int32)


def _silu(x):
    h = 0.5 * x
    return h + h * jnp.tanh(h)


def _norm_proj_kernel(x_ref, g_ref, *refs, chunk):
    x = x_ref[...]
    ms = jnp.mean(x * x, axis=-1, keepdims=True)
    h = ((x * lax.rsqrt(ms + EPS)) * g_ref[...]).astype(BF16)
    n_out = len(refs) // 2
    for w_ref, o_ref in zip(refs[:n_out], refs[n_out:]):
        ncol = o_ref.shape[1]
        for c0 in range(0, ncol, chunk):
            c1 = min(c0 + chunk, ncol)
            o_ref[:, c0:c1] = jnp.dot(h, w_ref[:, c0:c1], preferred_element_type=F32).astype(o_ref.dtype)


def _norm_proj(x2d, g, weights, out_dtypes, tm=512):
    n, d = x2d.shape
    return pl.pallas_call(
        functools.partial(_norm_proj_kernel, chunk=512),
        grid=(n // tm,),
        in_specs=[pl.BlockSpec((tm, d), lambda i: (i, 0)), pl.BlockSpec((1, d), lambda i: (0, 0))]
        + [pl.BlockSpec(w.shape, lambda i: (0, 0)) for w in weights],
        out_specs=[pl.BlockSpec((tm, w.shape[1]), lambda i: (i, 0)) for w in weights],
        out_shape=[jax.ShapeDtypeStruct((n, w.shape[1]), dt) for w, dt in zip(weights, out_dtypes)],
        compiler_params=_params(("parallel",)),
        name="norm_proj",
    )(x2d, g.reshape(1, d), *weights)


def _t5_table_kernel(tab_ref, bucket_ref, o_ref, *, stride, off, limit):
    h = pl.program_id(0)
    rows, cols = o_ref.shape[1], o_ref.shape[2]
    bucket = bucket_ref[...]
    near = jnp.zeros(bucket.shape, F32)
    for b in range(REL_BUCKETS):
        near = jnp.where(bucket == b, tab_ref[b * NSA_HEADS + h] * LOG2E, near)
    far = tab_ref[(REL_BUCKETS - 1) * NSA_HEADS + h] * LOG2E
    base = jnp.concatenate([near] * (rows // near.shape[0]), axis=0)
    rolled = pltpu.roll(base, 0, 1, stride=stride, stride_axis=0)
    d = (lax.broadcasted_iota(jnp.int32, (rows, cols), 1) - stride * lax.broadcasted_iota(jnp.int32, (rows, cols), 0)
         + off)
    o_ref[0] = jnp.where((d < 0) | (d >= limit), NEG_INF, jnp.where(d >= REL_MAX_DIST, far, rolled))


def _t5_table(tab_flat, rows, cols, stride, off, limit):
    assert cols >= 2 * REL_MAX_DIST and rows % 8 == 0
    k = (np.arange(cols) + off) % cols
    bucket = np.broadcast_to(np.where(k < REL_MAX_DIST, _t5_bucket_np(k), -1).astype(np.int32), (8, cols))
    return pl.pallas_call(
        functools.partial(_t5_table_kernel, stride=stride, off=off, limit=limit),
        grid=(NSA_HEADS,),
        in_specs=[
            pl.BlockSpec(memory_space=pltpu.SMEM),
            pl.BlockSpec((8, cols), lambda h: (0, 0)),
        ],
        out_specs=pl.BlockSpec((1, rows, cols), lambda h: (h, 0, 0)),
        out_shape=jax.ShapeDtypeStruct((NSA_HEADS, rows, cols), F32),
        compiler_params=_params(("arbitrary",)),
        name="t5_table",
    )(tab_flat, jnp.asarray(bucket))


def _tri_lower(n):
    r = lax.broadcasted_iota(jnp.int32, (n, n), 0)
    c = lax.broadcasted_iota(jnp.int32, (n, n), 1)
    return (r >= c).astype(F32)


N_SPLIT = 3


def _fox_aug_lane(h, i):
    return LANES * (h // 2) + (HEAD_DIM if h % 2 == 0 else 0) + i


def _fox_aug_consts():
    pq = np.zeros((LANES, GROUP_W), np.float32)
    pk = np.zeros((LANES, GROUP_W), np.float32)
    oq = np.zeros((1, GROUP_W), np.float32)
    ok = np.zeros((1, GROUP_W), np.float32)
    for h in range(FOX_HEADS):
        for i in range(N_SPLIT):
            pq[i * FOX_HEADS + h, _fox_aug_lane(h, i)] = 1.0
            pk[i * FOX_HEADS + h, _fox_aug_lane(h, N_SPLIT + i)] = -1.0
            oq[0, _fox_aug_lane(h, N_SPLIT + i)] = 1.0
            ok[0, _fox_aug_lane(h, i)] = 1.0
    return pq, pk, oq, ok


def _fox_cumsum_kernel(f_ref, b_ref, pq_ref, pk_ref, oq_ref, ok_ref, qa_ref, ka_ref, carry_ref, *, ts):
    @pl.when(pl.program_id(1) == 0)
    def _():
        carry_ref[...] = jnp.zeros_like(carry_ref)

    z = f_ref[...] + b_ref[...]
    logf = (jnp.minimum(z, 0.0) - jnp.log(1.0 + jnp.exp(-jnp.abs(z)))) * LOG2E
    tri = _tri_lower(LANES).astype(BF16)
    group = lax.broadcasted_iota(jnp.int32, (LANES, LANES), 1) // FOX_HEADS
    carry = carry_ref[...]
    for c in range(ts // LANES):
        rows = slice(c * LANES, (c + 1) * LANES)
        cs, rest = carry, logf[rows]
        for _ in range(N_SPLIT):
            piece = rest.astype(BF16)
            rest = rest - piece.astype(F32)
            cs = cs + jnp.dot(tri, piece, preferred_element_type=F32)
        carry = cs[LANES - 1:LANES, :]
        cat, rest = None, cs
        for i in range(N_SPLIT):
            piece = rest.astype(BF16).astype(F32)
            rest = rest - piece
            cat = piece if cat is None else jnp.where(group == i, piece, cat)
        cat = cat.astype(BF16)
        qa_ref[rows, :] = (oq_ref[...] + jnp.dot(cat, pq_ref[...], preferred_element_type=F32)).astype(BF16)
        ka_ref[rows, :] = (ok_ref[...] + jnp.dot(cat, pk_ref[...], preferred_element_type=F32)).astype(BF16)
    carry_ref[...] = carry


def _fox_cumsum(ps, f_bias_pad, b, s, ts=1024):
    ns = s // ts
    pq, pk, oq, ok = _fox_aug_consts()
    const = lambda bi, si: (0, 0)
    return pl.pallas_call(
        functools.partial(_fox_cumsum_kernel, ts=ts),
        grid=(b, ns),
        in_specs=[
            pl.BlockSpec((ts, LANES), lambda bi, si: (bi * ns + si, 0)),
            pl.BlockSpec((1, LANES), const),
            pl.BlockSpec(pq.shape, const),
            pl.BlockSpec(pk.shape, const),
            pl.BlockSpec(oq.shape, const),
            pl.BlockSpec(ok.shape, const),
        ],
        out_specs=[
            pl.BlockSpec((ts, GROUP_W), lambda bi, si: (bi * ns + si, 0)),
            pl.BlockSpec((ts, GROUP_W), lambda bi, si: (bi * ns + si, 0)),
        ],
        out_shape=[jax.ShapeDtypeStruct((b * s, GROUP_W), BF16), jax.ShapeDtypeStruct((b * s, GROUP_W), BF16)],
        scratch_shapes=[pltpu.VMEM((1, LANES), F32)],
        compiler_params=_params(("parallel", "arbitrary")),
        name="fox_cumsum",
    )(ps, f_bias_pad, jnp.asarray(pq, BF16), jnp.asarray(pk, BF16), jnp.asarray(oq), jnp.asarray(ok))


ONES_ROWS = 16


def _vt_rows(h):
    return slice(0, HEAD_DIM + ONES_ROWS) if h % 2 == 0 else slice(HEAD_DIM - ONES_ROWS, LANES)


def _store_vt(vt_ref, i, rows, v_pair):
    vt = v_pair.astype(F32).T.astype(BF16)
    ones = jnp.ones((HEAD_DIM, v_pair.shape[0]), BF16)
    vt_ref[i, 0:HEAD_DIM, rows] = vt[0:HEAD_DIM]
    vt_ref[i, HEAD_DIM:LANES, rows] = ones
    vt_ref[i + 1, 0:HEAD_DIM, rows] = ones
    vt_ref[i + 1, HEAD_DIM:LANES, rows] = vt[HEAD_DIM:LANES]


def _store_vt_groups(vt_ref, rows, v_groups):
    vt = v_groups.astype(F32).T.astype(BF16)
    ones = jnp.ones((HEAD_DIM, v_groups.shape[0]), BF16)
    for g in range(NSA_KV_HEADS):
        vg = vt[g * HEAD_DIM:(g + 1) * HEAD_DIM]
        vt_ref[2 * g, 0:HEAD_DIM, rows] = vg
        vt_ref[2 * g, HEAD_DIM:LANES, rows] = ones
        vt_ref[2 * g + 1, 0:HEAD_DIM, rows] = ones
        vt_ref[2 * g + 1, HEAD_DIM:LANES, rows] = vg


def _finish_pair(acc_ref, hp):
    ae, ao = acc_ref[2 * hp], acc_ref[2 * hp + 1]
    top = lax.broadcasted_iota(jnp.int32, ae.shape, 0) < HEAD_DIM
    both = jnp.where(top, ae / jnp.maximum(ae[HEAD_DIM:HEAD_DIM + 1, :], TINY),
                     ao / jnp.maximum(ao[HEAD_DIM - 1:HEAD_DIM, :], TINY))
    return both.T


def _swap_halves(x_b):
    return pltpu.roll(x_b.astype(F32), HEAD_DIM, 1).astype(BF16)


def _half_masks(rows):
    lo = jnp.where(lax.broadcasted_iota(jnp.int32, (rows, LANES), 1) < HEAD_DIM, 1.0, 0.0)
    return lo.astype(BF16), (1.0 - lo).astype(BF16)


def _fox_attn_kernel(q_ref, qa_ref, k_ref, ka_ref, v_ref, gate_ref, o_ref, qs_ref, kk_ref, vt_ref, m_ref, acc_ref,
                     pre_ref, *, t, tk, ahead):
    qi = pl.program_id(1)
    lo_q, hi_q = _half_masks(t)
    cm = lax.broadcasted_iota(jnp.int32, (tk, t), 0) - lax.broadcasted_iota(jnp.int32, (tk, t), 1)

    @pl.when(qi == 0)
    def _():
        lo_k, hi_k = _half_masks(tk)

        def merge(j, c):
            rows = pl.ds(pl.multiple_of(j * tk, tk), tk)
            for hp in range(FOX_HEADS // 2):
                ls = slice(hp * LANES, (hp + 1) * LANES)
                kp, ka, vp = k_ref[rows, ls], ka_ref[rows, ls], v_ref[rows, ls]
                kk_ref[2 * hp, rows, :] = kp * lo_k + ka * hi_k
                kk_ref[2 * hp + 1, rows, :] = ka * lo_k + kp * hi_k
                _store_vt(vt_ref, 2 * hp, rows, vp)
            return c

        lax.fori_loop(0, k_ref.shape[0] // tk, merge, 0)

    for hp in range(FOX_HEADS // 2):
        ls = slice(hp * LANES, (hp + 1) * LANES)
        qp, qa = q_ref[:, ls], qa_ref[:, ls]
        qs_ref[2 * hp] = (qp * lo_q + qa * hi_q).astype(F32).T.astype(BF16)
        qs_ref[2 * hp + 1] = (qa * lo_q + qp * hi_q).astype(F32).T.astype(BF16)
    m_ref[...] = jnp.full(m_ref.shape, NEG_INF, F32)
    acc_ref[...] = jnp.zeros_like(acc_ref)

    def rows_of(j):
        return pl.ds(pl.multiple_of(j * tk, tk), tk)

    def scores(j, h):
        return jnp.dot(kk_ref[h, rows_of(j), :], qs_ref[h], preferred_element_type=F32)

    def tile(j, diag, j_next):
        queue = [pre_ref[i] for i in range(ahead)]
        for h in range(FOX_HEADS):
            if h + ahead < FOX_HEADS:
                queue.append(scores(j, h + ahead))
            elif j_next is not None:
                pre_ref[h + ahead - FOX_HEADS] = scores(j_next, h + ahead - FOX_HEADS)
            s = queue.pop(0)
            if diag:
                s = jnp.where(cm <= qi * t - j * tk, s, NEG_INF)
            m_old = m_ref[h]
            m_new = jnp.maximum(m_old, jnp.max(s, axis=0, keepdims=True))
            m_ref[h] = m_new
            p = jnp.exp2(s - m_new).astype(BF16)
            vr = _vt_rows(h)
            acc_ref[h, vr] = jnp.exp2(m_old - m_new) * acc_ref[h, vr] + jnp.dot(vt_ref[h, vr, rows_of(j)], p,
                                                                                preferred_element_type=F32)

    jd = (qi * t) // tk
    n_diag = max(t // tk, 1)
    for i in range(ahead):
        pre_ref[i] = scores(0, i)

    def body(j, c):
        tile(j, False, j + 1)
        return c

    lax.fori_loop(0, jd, body, 0)
    for dj in range(n_diag):
        tile(jd + dj, True, jd + dj + 1 if dj + 1 < n_diag else None)

    for hp in range(FOX_HEADS // 2):
        ls = slice(hp * LANES, (hp + 1) * LANES)
        o_ref[:, ls] = (_finish_pair(acc_ref, hp) * _silu(gate_ref[:, ls].astype(F32))).astype(o_ref.dtype)


def _fox_attn(p, qaug, kaug, b, s, t=512, tk=256, ahead=2):
    nq = s // t
    w = GROUP_W
    return pl.pallas_call(
        functools.partial(_fox_attn_kernel, t=t, tk=tk, ahead=ahead),
        grid=(b, nq),
        in_specs=[
            pl.BlockSpec((t, w), lambda bi, qi: (bi * nq + qi, OFF_FOX_Q // w)),
            pl.BlockSpec((t, w), lambda bi, qi: (bi * nq + qi, 0)),
            pl.BlockSpec((s, w), lambda bi, qi: (bi, OFF_FOX_K // w)),
            pl.BlockSpec((s, w), lambda bi, qi: (bi, 0)),
            pl.BlockSpec((s, w), lambda bi, qi: (bi, OFF_FOX_V // w)),
            pl.BlockSpec((t, w), lambda bi, qi: (bi * nq + qi, OFF_FOX_G // w)),
        ],
        out_specs=pl.BlockSpec((t, w), lambda bi, qi: (bi * nq + qi, 0)),
        out_shape=jax.ShapeDtypeStruct((b * s, w), BF16),
        scratch_shapes=[
            pltpu.VMEM((FOX_HEADS, LANES, t), BF16),
            pltpu.VMEM((FOX_HEADS, s, LANES), BF16),
            pltpu.VMEM((FOX_HEADS, LANES, s), BF16),
            pltpu.VMEM((FOX_HEADS, 1, t), F32),
            pltpu.VMEM((FOX_HEADS, LANES, t), F32),
            pltpu.VMEM((ahead, tk, t), F32),
        ],
        compiler_params=_params(("parallel", "arbitrary")),
        name="fox_attn",
    )(p, qaug, p, kaug, p, p)


def _dot_split(lhs_f32, rhs_b):
    out, rest = None, lhs_f32
    for _ in range(N_SPLIT):
        piece = rest.astype(BF16)
        rest = rest - piece.astype(F32)
        term = jnp.dot(piece, rhs_b, preferred_element_type=F32)
        out = term if out is None else out + term
    return out


def _ssd_kernel(z_ref, xbc_ref, dt_ref, cw_ref, cb_ref, dtb_ref, alog_ref, dsk_ref, ng_ref, o_ref,
                xpad_ref, xc_ref, state_ref, y_ref, *, nch):
    q = SSM_CHUNK
    rows_all = nch * q
    halo = 8

    @pl.when(pl.program_id(1) == 0)
    def _():
        xpad_ref[0:halo, :] = jnp.zeros((halo, SSM_CONV_DIM), F32)
        state_ref[...] = jnp.zeros_like(state_ref)

    xpad_ref[halo:halo + rows_all, :] = xbc_ref[...].astype(F32)
    y = cb_ref[...]
    for k in range(SSM_CONV):
        off = halo - (SSM_CONV - 1) + k
        y = y + cw_ref[k:k + 1, :] * xpad_ref[off:off + rows_all, :]
    xpad_ref[0:halo, :] = xpad_ref[rows_all:rows_all + halo, :]
    xc_ref[...] = _silu(y)

    x_dt = dt_ref[...] + dtb_ref[...]
    dt_all = jnp.maximum(x_dt, 0.0) + jnp.log(1.0 + jnp.exp(-jnp.abs(x_dt)))
    a_all = dt_all * (-jnp.exp(alog_ref[...]))
    tri_t = (lax.broadcasted_iota(jnp.int32, (q, q), 0) <= lax.broadcasted_iota(jnp.int32, (q, q), 1)).astype(BF16)
    er = lax.broadcasted_iota(jnp.int32, (LANES, GROUP_W), 0)
    ec = lax.broadcasted_iota(jnp.int32, (LANES, GROUP_W), 1)
    expand = jnp.where(ec // HEAD_DIM == er - PS_DT_LANE, 1.0, 0.0).astype(BF16)
    row = lax.broadcasted_iota(jnp.int32, (q, q), 0)
    col = lax.broadcasted_iota(jnp.int32, (q, q), 1)
    causal = row >= col
    lo = lax.broadcasted_iota(jnp.int32, (q, LANES), 1) < HEAD_DIM
    gw = GROUP_W // SSM_GROUPS
    hpg = SSM_HEADS // SSM_GROUPS

    for c in range(nch):
        rs = slice(c * q, (c + 1) * q)
        xs = xc_ref[rs, :GROUP_W]
        dt = dt_all[rs]
        acs_t = _dot_split(a_all[rs].T, tri_t)
        acs = acs_t.T
        stacked = jnp.concatenate([dt, jnp.exp(acs), jnp.exp(acs[q - 1:q, :] - acs)], axis=0)
        full = _dot_split(stacked, expand)
        dt_full, eacs_full, dec_full = full[0:q], full[q:2 * q], full[2 * q:3 * q]
        xdt = xs * dt_full
        xdt_b = xdt.astype(BF16)
        xdec_b = (xdt * dec_full).astype(BF16)

        for g in range(SSM_GROUPS):
            bm = xc_ref[rs, GROUP_W + g * SSM_STATE:GROUP_W + (g + 1) * SSM_STATE]
            cm = xc_ref[rs, GROUP_W + (SSM_GROUPS + g) * SSM_STATE:GROUP_W + (SSM_GROUPS + g + 1) * SSM_STATE]
            bm_b = bm.astype(BF16)
            cm_b = cm.astype(BF16)
            gs = slice(g * gw, (g + 1) * gw)
            cbg = lax.dot_general(cm_b, bm_b, (((1,), (1,)), ((), ())), preferred_element_type=F32)
            st = state_ref[:, gs]
            y_off = jnp.dot(cm_b, st.astype(BF16), preferred_element_type=F32) * eacs_full[:, gs]
            cst = jnp.dot(bm.T.astype(BF16), xdec_b[:, gs], preferred_element_type=F32)
            state_ref[:, gs] = st * eacs_full[q - 1:q, gs] + cst
            for hp in range(hpg // 2):
                ls = slice(g * gw + hp * LANES, g * gw + (hp + 1) * LANES)
                yd = []
                for e in range(2):
                    h = g * hpg + 2 * hp + e
                    hl = PS_DT_LANE + h
                    seg = jnp.exp(jnp.where(causal, acs[:, hl:hl + 1] - acs_t[hl:hl + 1, :], NEG_INF))
                    yd.append(jnp.dot((cbg * seg).astype(BF16), xdt_b[:, ls], preferred_element_type=F32))
                y_ref[:, ls] = jnp.where(lo, yd[0], yd[1]) + y_off[:, hp * LANES:(hp + 1) * LANES]

        yz = (y_ref[...] + xs * dsk_ref[...]) * _silu(z_ref[rs, :].astype(F32))
        for g in range(SSM_GROUPS):
            gs = slice(g * gw, (g + 1) * gw)
            blk = yz[:, gs]
            ms = jnp.mean(blk * blk, axis=-1, keepdims=True)
            o_ref[rs, gs] = ((blk * lax.rsqrt(ms + EPS)) * ng_ref[:, gs]).astype(o_ref.dtype)


def _ssd(p, ps, conv_w, conv_b, dt_bias_pad, a_log_pad, d_full, norm_g, b, s, nch=8):
    q = SSM_CHUNK
    rows = nch * q
    nc = s // rows
    row = lambda bi, ci: (bi * nc + ci)
    const = lambda bi, ci: (0, 0)
    return pl.pallas_call(
        functools.partial(_ssd_kernel, nch=nch),
        grid=(b, nc),
        in_specs=[
            pl.BlockSpec((rows, GROUP_W), lambda bi, ci: (row(bi, ci), OFF_SSM_Z // GROUP_W)),
            pl.BlockSpec((rows, SSM_CONV_DIM), lambda bi, ci: (row(bi, ci), OFF_SSM_XBC // SSM_CONV_DIM)),
            pl.BlockSpec((rows, LANES), lambda bi, ci: (row(bi, ci), 0)),
            pl.BlockSpec((SSM_CONV, SSM_CONV_DIM), const),
            pl.BlockSpec((1, SSM_CONV_DIM), const),
            pl.BlockSpec((1, LANES), const),
            pl.BlockSpec((1, LANES), const),
            pl.BlockSpec((1, GROUP_W), const),
            pl.BlockSpec((1, GROUP_W), const),
        ],
        out_specs=pl.BlockSpec((rows, GROUP_W), lambda bi, ci: (row(bi, ci), 0)),
        out_shape=jax.ShapeDtypeStruct((b * s, GROUP_W), BF16),
        scratch_shapes=[
            pltpu.VMEM((rows + 8, SSM_CONV_DIM), F32),
            pltpu.VMEM((rows, SSM_CONV_DIM), F32),
            pltpu.VMEM((SSM_STATE, GROUP_W), F32),
            pltpu.VMEM((q, GROUP_W), F32),
        ],
        compiler_params=_params(("parallel", "arbitrary")),
        name="ssd",
    )(p, p, ps, conv_w, conv_b, dt_bias_pad, a_log_pad, d_full, norm_g)


def _compress_kernel(x_ref, w1_ref, w2_ref, pe_ref, o_ref, ot_ref, xf_ref):
    s = x_ref.shape[0]
    n = s // CMP_STRIDE
    for sb in range(2):
        xf_ref[sb] = x_ref[:, sb * LANES:(sb + 1) * LANES].astype(F32)
    o_ref[...] = jnp.zeros_like(o_ref)
    ot_ref[...] = jnp.zeros_like(ot_ref)
    for sb in range(2):
        first = jnp.zeros((n, 2 * CMP_HIDDEN), F32)
        second = jnp.zeros((n, 2 * CMP_HIDDEN), F32)
        for l in range(CMP_STRIDE):
            xl = xf_ref[sb, pl.ds(l, n, stride=CMP_STRIDE), :]
            first = first + jnp.dot((xl + pe_ref[sb, l:l + 1, :]).astype(BF16), w1_ref[sb, l],
                                    preferred_element_type=F32)
            second = second + jnp.dot((xl + pe_ref[sb, CMP_STRIDE + l:CMP_STRIDE + l + 1, :]).astype(BF16),
                                      w1_ref[sb, CMP_STRIDE + l], preferred_element_type=F32)
        h = _silu(first + pltpu.roll(second, n - 1, 0))
        o = jnp.dot(h.astype(BF16), w2_ref[sb], preferred_element_type=F32)
        for e in range(2):
            oe = o[:, e * LANES:(e + 1) * LANES]
            o_ref[0, 2 * sb + e, 0:n, :] = oe.astype(o_ref.dtype)
            ot_ref[0, 2 * sb + e, :, 0:n] = oe.T.astype(ot_ref.dtype)


def _blockdiag2(a):
    z = jnp.zeros_like(a)
    return jnp.concatenate([jnp.concatenate([a, z], axis=-1), jnp.concatenate([z, a], axis=-1)], axis=-2)


def _compress(p, cmp_w1, cmp_w2, cmp_pe, b, s):
    nslot = 2 * NSA_KV_HEADS
    w1 = _blockdiag2(cmp_w1.astype(BF16).reshape(2, CMP_BLOCK, HEAD_DIM, CMP_HIDDEN))
    w2 = _blockdiag2(jnp.concatenate([cmp_w2, cmp_w2], axis=-1).astype(BF16))
    pe = jnp.concatenate([cmp_pe, cmp_pe], axis=-1).astype(F32)
    kw = 2 * NSA_KV_W
    return pl.pallas_call(
        _compress_kernel,
        grid=(b,),
        in_specs=[
            pl.BlockSpec((s, kw), lambda bi: (bi, OFF_CMP_KV // kw)),
            pl.BlockSpec(w1.shape, lambda bi: (0, 0, 0, 0)),
            pl.BlockSpec(w2.shape, lambda bi: (0, 0, 0)),
            pl.BlockSpec(pe.shape, lambda bi: (0, 0, 0)),
        ],
        out_specs=[pl.BlockSpec((1, nslot, N_CMP_PAD, LANES), lambda bi: (bi, 0, 0, 0)),
                   pl.BlockSpec((1, nslot, LANES, N_CMP_PAD), lambda bi: (bi, 0, 0, 0))],
        out_shape=[jax.ShapeDtypeStruct((b, nslot, N_CMP_PAD, LANES), BF16),
                   jax.ShapeDtypeStruct((b, nslot, LANES, N_CMP_PAD), BF16)],
        scratch_shapes=[pltpu.VMEM((2, s, LANES), F32)],
        compiler_params=_params(("parallel",)),
        name="nsa_compress",
    )(p, w1, w2, pe)


def _cmp_select_kernel(q_ref, kv_ref, vt_ref, gl_ref, gate_ref, fc_ref, ovt_ref, o_ref, ns_ref):
    t = T_ATT
    ncp = kv_ref.shape[2]
    qi = pl.program_id(1)
    t0 = qi * t
    lo = lax.broadcasted_iota(jnp.int32, (t, LANES), 1) < HEAD_DIM
    zero = jnp.zeros((t, LANES), BF16)
    start = pl.multiple_of(ncp - qi * (t // CMP_STRIDE), CMP_STRIDE)
    gl = gl_ref[...]

    nsel = SEL_BLOCK
    jrow = lax.broadcasted_iota(jnp.int32, (nsel, t), 0)
    cur = (t0 + lax.broadcasted_iota(jnp.int32, (nsel, t), 1)) // SEL_BLOCK
    forced = (jrow == 0) | (jrow == cur) | (jrow == cur - 1)
    past = jrow <= cur

    for g in range(NSA_KV_HEADS):
        kc = kv_ref[0, g]
        vct = vt_ref[0, NSA_KV_HEADS + g]
        psum = jnp.zeros((ncp, t), F32)
        outs = []

        def scores(r):
            h = g * NSA_REP + r
            qp = q_ref[:, (h // 2) * LANES:(h // 2 + 1) * LANES]
            qe = jnp.where(lo, qp, zero) if r % 2 == 0 else jnp.where(lo, zero, qp)
            return lax.dot_general(kc, qe, (((1,), (1,)), ((), ())), preferred_element_type=F32)

        queue = [scores(0)]
        for r in range(NSA_REP):
            h = g * NSA_REP + r
            if r + 1 < NSA_REP:
                queue.append(scores(r + 1))
            s = fc_ref[h, pl.ds(start, ncp), :] + queue.pop(0)
            m = jnp.maximum(jnp.max(s, axis=0, keepdims=True), 0.1 * NEG_INF)
            e = jnp.exp2(s - m)
            pr = e * (1.0 / jnp.maximum(jnp.sum(e, axis=0, keepdims=True), TINY))
            psum = psum + pr
            oc = jnp.dot(vct, pr.astype(BF16), preferred_element_type=F32)
            outs.append(oc.T * jax.nn.sigmoid(gl[:, PS_GATE_LANE + h:PS_GATE_LANE + h + 1]))
        for hp in range(NSA_REP // 2):
            ls = slice((g * NSA_REP // 2 + hp) * LANES, (g * NSA_REP // 2 + hp + 1) * LANES)
            o = jnp.where(lo, outs[2 * hp], outs[2 * hp + 1])
            o_ref[:, ls] = (o * _silu(gate_ref[:, ls].astype(F32))).astype(o_ref.dtype)

        imp_t = jnp.zeros((nsel, t), F32)
        rest = psum
        for _ in range(N_SPLIT):
            piece = rest.astype(BF16)
            rest = rest - piece.astype(F32)
            imp_t = imp_t + jnp.dot(ovt_ref[...], piece, preferred_element_type=F32)
        imp_t = jnp.where(past, jnp.where(forced, SEL_FORCE, imp_t), -SEL_FORCE)
        sub = 8
        sub_row = lax.broadcasted_iota(jnp.int32, (sub, t), 0)
        rows = [imp_t[k * sub:(k + 1) * sub] for k in range(nsel // sub)]
        rank = [jnp.zeros((sub, t), F32) for _ in rows]
        for i in range(nsel):
            bi = imp_t[i:i + 1, :]
            for k, x in enumerate(rows):
                if k * sub > i:
                    ahead = jnp.where(bi >= x, 1.0, 0.0)
                elif k * sub + sub - 1 <= i:
                    ahead = jnp.where(bi > x, 1.0, 0.0)
                else:
                    ahead = jnp.where(sub_row > i - k * sub, jnp.where(bi >= x, 1.0, 0.0),
                                      jnp.where(bi > x, 1.0, 0.0))
                rank[k] = rank[k] + ahead
        rank = jnp.concatenate(rank, axis=0)
        notsel = jnp.where((rank < float(SEL_TOPK)) & past, 0.0, 1.0)
        ns2 = jnp.concatenate([notsel, notsel], axis=0).T
        ns_ref[:, g * LANES:(g + 1) * LANES] = ns2.astype(ns_ref.dtype)


def _cmp_select(p, ps, kvc, kvc_t, fc, overlap_t, b, s):
    t = T_ATT
    nq = s // t
    w = GROUP_W
    ncp = kvc.shape[2]
    return pl.pallas_call(
        _cmp_select_kernel,
        grid=(b, nq),
        in_specs=[
            pl.BlockSpec((t, w), lambda bi, qi: (bi * nq + qi, OFF_NSA_Q // w)),
            pl.BlockSpec((1, 2 * NSA_KV_HEADS, ncp, LANES), lambda bi, qi: (bi, 0, 0, 0)),
            pl.BlockSpec((1, 2 * NSA_KV_HEADS, LANES, ncp), lambda bi, qi: (bi, 0, 0, 0)),
            pl.BlockSpec((t, LANES), lambda bi, qi: (bi * nq + qi, 0)),
            pl.BlockSpec((t, w), lambda bi, qi: (bi * nq + qi, OFF_NSA_G // w)),
            pl.BlockSpec((NSA_HEADS, 2 * ncp, t), lambda bi, qi: (0, 0, 0)),
            pl.BlockSpec((SEL_BLOCK, ncp), lambda bi, qi: (0, 0)),
        ],
        out_specs=[
            pl.BlockSpec((t, w), lambda bi, qi: (bi * nq + qi, 0)),
            pl.BlockSpec((t, NSA_KV_HEADS * LANES), lambda bi, qi: (bi * nq + qi, 0)),
        ],
        out_shape=[jax.ShapeDtypeStruct((b * s, w), BF16),
                   jax.ShapeDtypeStruct((b * s, NSA_KV_HEADS * LANES), BF16)],
        compiler_params=_params(("parallel", "arbitrary")),
        name="nsa_cmp_select",
    )(p, kvc, kvc_t, ps, p, fc, overlap_t)


def _sel_attn_kernel(tab_ref, q_ref, ns_ref, k_ref, v_ref, gl_ref, gate_ref, bs_ref, o_ref,
                     qa_ref, kk_ref, vt_ref, m_ref, acc_ref, pre_ref, *, t, tk, ahead):
    qi = pl.program_id(1)
    lo_q, hi_q = _half_masks(t)
    lo = lax.broadcasted_iota(jnp.int32, (t, LANES), 1) < HEAD_DIM
    n_near = t // tk + 1

    @pl.when(qi == 0)
    def _():
        lane = lax.broadcasted_iota(jnp.int32, (tk, LANES), 1)
        krow = lax.broadcasted_iota(jnp.int32, (tk, LANES), 0)
        lo_f = jnp.where(lane < HEAD_DIM, 1.0, 0.0)
        hi_f = 1.0 - lo_f
        lo_k, hi_k = lo_f.astype(BF16), hi_f.astype(BF16)

        def merge(j, c):
            ks = pl.multiple_of(j * tk, tk)
            rows = pl.ds(ks, tk)
            hot = jnp.where((lane % HEAD_DIM) == (ks + krow) // SEL_BLOCK, NEG_INF, 0.0)
            hot_lo, hot_hi = (hot * lo_f).astype(BF16), (hot * hi_f).astype(BF16)
            k01 = k_ref[rows, :]
            k10 = _swap_halves(k01)
            kk_ref[0, rows, :] = k01 * lo_k + hot_hi
            kk_ref[1, rows, :] = hot_lo + k10 * hi_k
            kk_ref[2, rows, :] = k10 * lo_k + hot_hi
            kk_ref[3, rows, :] = hot_lo + k01 * hi_k
            _store_vt_groups(vt_ref, rows, v_ref[rows, :])
            return c

        lax.fori_loop(0, k_ref.shape[0] // tk, merge, 0)

    for h in range(NSA_HEADS):
        g = h // NSA_REP
        qp = q_ref[:, (h // 2) * LANES:(h // 2 + 1) * LANES]
        ns = ns_ref[:, g * LANES:(g + 1) * LANES]
        qa = qp * lo_q + ns * hi_q if h % 2 == 0 else ns * lo_q + qp * hi_q
        qa_ref[h] = qa.astype(F32).T.astype(BF16)
    m_ref[...] = jnp.full(m_ref.shape, NEG_INF, F32)
    acc_ref[...] = jnp.zeros_like(acc_ref)

    kv = lambda h: 2 * (h // NSA_REP) + h % 2

    def rows_of(j):
        return pl.ds(pl.multiple_of(j * tk, tk), tk)

    def scores(j, h):
        return jnp.dot(kk_ref[kv(h), rows_of(j), :], qa_ref[h], preferred_element_type=F32)

    def tile(j, near, j_next):
        rows = rows_of(j)
        queue = [pre_ref[i] for i in range(ahead)]
        for h in range(NSA_HEADS):
            if h + ahead < NSA_HEADS:
                queue.append(scores(j, h + ahead))
            elif j_next is not None:
                pre_ref[h + ahead - NSA_HEADS] = scores(j_next, h + ahead - NSA_HEADS)
            s = queue.pop(0)
            m_old = m_ref[h]
            if near is None:
                far = tab_ref[(REL_BUCKETS - 1) * NSA_HEADS + h] * LOG2E
                m_new = jnp.maximum(m_old, jnp.max(s, axis=0, keepdims=True) + far)
                p = jnp.exp2(s - (m_new - far)).astype(BF16)
            else:
                s = bs_ref[h, near * tk:(near + 1) * tk, :] + s
                m_new = jnp.maximum(m_old, jnp.max(s, axis=0, keepdims=True))
                p = jnp.exp2(s - m_new).astype(BF16)
            m_ref[h] = m_new
            vr = _vt_rows(h)
            acc_ref[h, vr] = jnp.exp2(m_old - m_new) * acc_ref[h, vr] + jnp.dot(vt_ref[kv(h), vr, rows], p,
                                                                                preferred_element_type=F32)

    jd = (qi * t) // tk
    for i in range(ahead):
        pre_ref[i] = scores(0, i)

    def body(j, c):
        tile(j, None, j + 1)
        return c

    lax.fori_loop(0, jnp.maximum(jd - 1, 0), body, 0)

    @pl.when(qi > 0)
    def _():
        tile(jd - 1, 0, jd)

    for dj in range(n_near - 1):
        tile(jd + dj, 1 + dj, jd + dj + 1 if dj + 2 < n_near else None)

    gl = gl_ref[...]
    for hp in range(NSA_HEADS // 2):
        ls = slice(hp * LANES, (hp + 1) * LANES)
        c0 = PS_GATE_LANE + NSA_HEADS + 2 * hp
        branch = jnp.where(lo, jax.nn.sigmoid(gl[:, c0:c0 + 1]), jax.nn.sigmoid(gl[:, c0 + 1:c0 + 2]))
        o_ref[:, ls] = (_finish_pair(acc_ref, hp) * branch * _silu(gate_ref[:, ls].astype(F32))).astype(o_ref.dtype)


def _sel_attn(tab_flat, p, ps, notsel, bsel, b, s, t=512, tk=256, ahead=2):
    nq = s // t
    w = GROUP_W
    kw = NSA_KV_HEADS * LANES
    n_near = t // tk + 1
    return pl.pallas_call(
        functools.partial(_sel_attn_kernel, t=t, tk=tk, ahead=ahead),
        grid=(b, nq),
        in_specs=[
            pl.BlockSpec(memory_space=pltpu.SMEM),
            pl.BlockSpec((t, w), lambda bi, qi: (bi * nq + qi, OFF_NSA_Q // w)),
            pl.BlockSpec((t, kw), lambda bi, qi: (bi * nq + qi, 0)),
            pl.BlockSpec((s, NSA_KV_W), lambda bi, qi: (bi, OFF_SEL_K // NSA_KV_W)),
            pl.BlockSpec((s, NSA_KV_W), lambda bi, qi: (bi, OFF_SEL_V // NSA_KV_W)),
            pl.BlockSpec((t, LANES), lambda bi, qi: (bi * nq + qi, 0)),
            pl.BlockSpec((t, w), lambda bi, qi: (bi * nq + qi, OFF_NSA_G // w)),
            pl.BlockSpec((NSA_HEADS, n_near * tk, t), lambda bi, qi: (0, 0, 0), pipeline_mode=pl.Buffered(1)),
        ],
        out_specs=pl.BlockSpec((t, w), lambda bi, qi: (bi * nq + qi, 0)),
        out_shape=jax.ShapeDtypeStruct((b * s, w), BF16),
        scratch_shapes=[
            pltpu.VMEM((NSA_HEADS, LANES, t), BF16),
            pltpu.VMEM((2 * NSA_KV_HEADS, s, LANES), BF16),
            pltpu.VMEM((2 * NSA_KV_HEADS, LANES, s), BF16),
            pltpu.VMEM((NSA_HEADS, 1, t), F32),
            pltpu.VMEM((NSA_HEADS, LANES, t), F32),
            pltpu.VMEM((ahead, tk, t), F32),
        ],
        compiler_params=_params(("parallel", "arbitrary")),
        name="nsa_sel_attn",
    )(tab_flat, p, notsel, p, p, ps, p, bsel)


def _win_attn_kernel(q_ref, k_ref, v_ref, gl_ref, gate_ref, bw_ref, o_ref, qe_ref, kk_ref, vt_ref, m_ref, acc_ref,
                     pre_ref, *, t, tk, ahead):
    qi = pl.program_id(1)
    lo_q, hi_q = _half_masks(t)
    lo = lax.broadcasted_iota(jnp.int32, (t, LANES), 1) < HEAD_DIM
    n_before = WINDOW // tk
    n_tiles = n_before + t // tk

    @pl.when(qi == 0)
    def _():
        lo_k, hi_k = _half_masks(tk)

        def merge(j, c):
            rows = pl.ds(pl.multiple_of(j * tk, tk), tk)
            k01 = k_ref[rows, :]
            k10 = _swap_halves(k01)
            kk_ref[0, rows, :] = k01 * lo_k + k10 * hi_k
            kk_ref[1, rows, :] = k10 * lo_k + k01 * hi_k
            _store_vt_groups(vt_ref, rows, v_ref[rows, :])
            return c

        lax.fori_loop(0, v_ref.shape[0] // tk, merge, 0)

    for h in range(NSA_HEADS):
        qp = q_ref[:, (h // 2) * LANES:(h // 2 + 1) * LANES]
        qe_ref[h] = qp * lo_q if h % 2 == 0 else qp * hi_q
    m_ref[...] = jnp.full(m_ref.shape, NEG_INF, F32)
    acc_ref[...] = jnp.zeros_like(acc_ref)

    def rows_of(w):
        return pl.ds(pl.multiple_of((qi * (t // tk) - n_before + w) * tk, tk), tk)

    def scores(w, h):
        return lax.dot_general(kk_ref[h // NSA_REP, rows_of(w), :], qe_ref[h], (((1,), (1,)), ((), ())),
                               preferred_element_type=F32)

    def tile(w):
        rows = rows_of(w)
        queue = [pre_ref[i] for i in range(ahead)]
        for h in range(NSA_HEADS):
            if h + ahead < NSA_HEADS:
                queue.append(scores(w, h + ahead))
            elif w + 1 < n_tiles:
                pre_ref[h + ahead - NSA_HEADS] = scores(w + 1, h + ahead - NSA_HEADS)
            s = bw_ref[h, w * tk:(w + 1) * tk, :] + queue.pop(0)
            m_old = m_ref[h]
            m_new = jnp.maximum(m_old, jnp.max(s, axis=0, keepdims=True))
            m_ref[h] = m_new
            p = jnp.exp2(s - m_new).astype(BF16)
            vr = _vt_rows(h)
            acc_ref[h, vr] = jnp.exp2(m_old - m_new) * acc_ref[h, vr] + jnp.dot(
                vt_ref[2 * (h // NSA_REP) + h % 2, vr, rows], p, preferred_element_type=F32)

    first = jnp.maximum(n_before - qi * (t // tk), 0)
    for i in range(ahead):
        pre_ref[i] = scores(first, i)
    per_q = t // tk
    first_qi = [-(-(n_before - w) // per_q) if w < n_before else 0 for w in range(n_tiles)]
    for need in sorted(set(first_qi), reverse=True):
        group = [w for w in range(n_tiles) if first_qi[w] == need]

        def run(group=group):
            for w in group:
                tile(w)

        if need > 0:
            pl.when(qi >= need)(run)
        else:
            run()

    gl = gl_ref[...]
    for hp in range(NSA_HEADS // 2):
        ls = slice(hp * LANES, (hp + 1) * LANES)
        c0 = PS_GATE_LANE + 2 * NSA_HEADS + 2 * hp
        branch = jnp.where(lo, jax.nn.sigmoid(gl[:, c0:c0 + 1]), jax.nn.sigmoid(gl[:, c0 + 1:c0 + 2]))
        o_ref[:, ls] = (_finish_pair(acc_ref, hp) * branch * _silu(gate_ref[:, ls].astype(F32))).astype(o_ref.dtype)


def _win_attn(p, ps, bwin, b, s, t=WIN_T, tk=WIN_TK, ahead=2):
    assert t % tk == 0 and WINDOW % tk == 0
    nq = s // t
    w = GROUP_W
    return pl.pallas_call(
        functools.partial(_win_attn_kernel, t=t, tk=tk, ahead=ahead),
        grid=(b, nq),
        in_specs=[
            pl.BlockSpec((t, w), lambda bi, qi: (bi * nq + qi, OFF_NSA_Q // w)),
            pl.BlockSpec((s, NSA_KV_W), lambda bi, qi: (bi, OFF_WIN_K // NSA_KV_W)),
            pl.BlockSpec((s, NSA_KV_W), lambda bi, qi: (bi, OFF_WIN_V // NSA_KV_W)),
            pl.BlockSpec((t, LANES), lambda bi, qi: (bi * nq + qi, 0)),
            pl.BlockSpec((t, w), lambda bi, qi: (bi * nq + qi, OFF_NSA_G // w)),
            pl.BlockSpec((NSA_HEADS, WINDOW + t, t), lambda bi, qi: (0, 0, 0), pipeline_mode=pl.Buffered(1)),
        ],
        out_specs=pl.BlockSpec((t, w), lambda bi, qi: (bi * nq + qi, 0)),
        out_shape=jax.ShapeDtypeStruct((b * s, w), BF16),
        scratch_shapes=[
            pltpu.VMEM((NSA_HEADS, t, LANES), BF16),
            pltpu.VMEM((NSA_KV_HEADS, s, LANES), BF16),
            pltpu.VMEM((2 * NSA_KV_HEADS, LANES, s), BF16),
            pltpu.VMEM((NSA_HEADS, 1, t), F32),
            pltpu.VMEM((NSA_HEADS, LANES, t), F32),
            pltpu.VMEM((ahead, tk, t), F32),
        ],
        compiler_params=_params(("parallel", "arbitrary")),
        name="nsa_win_attn",
    )(p, p, p, ps, p, bwin)


def _mem_attn_kernel(q_ref, gate_ref, kv_ref, o_ref):
    scale = MEM_HEAD_DIM ** -0.5
    for h in range(MEM_HEADS):
        ls = slice(h * LANES, (h + 1) * LANES)
        k = kv_ref[:, ls]
        v = kv_ref[:, GROUP_W + h * LANES:GROUP_W + (h + 1) * LANES]
        s = lax.dot_general(q_ref[:, ls], k, (((1,), (1,)), ((), ())), preferred_element_type=F32) * scale
        m = jnp.max(s, axis=1, keepdims=True)
        e = jnp.exp(s - m)
        l = jnp.sum(e, axis=1, keepdims=True)
        o = jnp.dot(e.astype(BF16), v, preferred_element_type=F32) / l
        o_ref[:, ls] = (o * _silu(gate_ref[:, ls].astype(F32))).astype(o_ref.dtype)


def _mem_attn(p, mem_kv, b, s, t=1024):
    nq = s // t
    w = GROUP_W
    m = mem_kv.shape[0] // b
    return pl.pallas_call(
        _mem_attn_kernel,
        grid=(b, nq),
        in_specs=[
            pl.BlockSpec((t, w), lambda bi, qi: (bi * nq + qi, OFF_MEM_Q // w)),
            pl.BlockSpec((t, w), lambda bi, qi: (bi * nq + qi, OFF_MEM_G // w)),
            pl.BlockSpec((m, 2 * w), lambda bi, qi: (bi, 0)),
        ],
        out_specs=pl.BlockSpec((t, w), lambda bi, qi: (bi * nq + qi, 0)),
        out_shape=jax.ShapeDtypeStruct((b * s, w), BF16),
        compiler_params=_params(("parallel", "arbitrary")),
        name="mem_attn",
    )(p, p, mem_kv)


def _out_proj_kernel(x_ref, of_ref, os_ref, oc_ref, osel_ref, ow_ref, om_ref, w_ref, g_ref, o_ref, *, final):
    w = GROUP_W
    nsa = (oc_ref[...].astype(F32) + osel_ref[...].astype(F32) + ow_ref[...].astype(F32)).astype(BF16)
    acc = x_ref[...]
    for i, part in enumerate((of_ref[...], os_ref[...], nsa, om_ref[...])):
        acc = acc + jnp.dot(part, w_ref[i * w:(i + 1) * w, :], preferred_element_type=F32)
    if final:
        ms = jnp.mean(acc * acc, axis=-1, keepdims=True)
        acc = (acc * lax.rsqrt(ms + EPS)) * g_ref[...]
    o_ref[...] = acc


def _out_proj(x2d, parts, w_out, g, final, tm=1024):
    n, d = x2d.shape
    w = GROUP_W
    part_spec = pl.BlockSpec((tm, w), lambda i: (i, 0))
    return pl.pallas_call(
        functools.partial(_out_proj_kernel, final=final),
        grid=(n // tm,),
        in_specs=[pl.BlockSpec((tm, d), lambda i: (i, 0))] + [part_spec] * 6 + [
            pl.BlockSpec((4 * w, d), lambda i: (0, 0)),
            pl.BlockSpec((1, d), lambda i: (0, 0)),
        ],
        out_specs=pl.BlockSpec((tm, d), lambda i: (i, 0)),
        out_shape=jax.ShapeDtypeStruct((n, d), F32),
        compiler_params=_params(("parallel",)),
        name="out_proj",
    )(x2d, *parts, w_out, g.reshape(1, d))


def _pack_in_proj(w_in_l):
    fox, ssm, nsa, mem = 0, FOX_COLS, FOX_COLS + SSM_COLS, FOX_COLS + SSM_COLS + NSA_COLS
    w = GROUP_W
    q_scale = HEAD_DIM ** -0.5 * LOG2E
    cols = lambda a, n: w_in_l[:, a:a + n]
    kv = lambda slot: nsa + w + slot * NSA_KV_W
    main = ([cols(fox, w) * q_scale, cols(fox + w, 3 * w)]
            + [cols(ssm + w, SSM_CONV_DIM), cols(ssm, w)]
            + [cols(nsa, w) * q_scale, cols(nsa + w + 6 * NSA_KV_W + 3 * NSA_HEADS, w)]
            + [cols(mem, 2 * w)]
            + [cols(kv(2), 4 * NSA_KV_W)]
            + [cols(kv(0), 2 * NSA_KV_W)])
    w_main = jnp.concatenate(main, axis=1).astype(BF16)
    assert w_main.shape[1] == P_COLS
    small = ([cols(fox + 4 * w, FOX_HEADS)] * N_SPLIT
             + [cols(ssm + w + SSM_CONV_DIM, SSM_HEADS)]
             + [cols(nsa + w + 6 * NSA_KV_W, 3 * NSA_HEADS)])
    assert N_SPLIT * FOX_HEADS == PS_DT_LANE and PS_DT_LANE + SSM_HEADS == PS_GATE_LANE
    used = PS_GATE_LANE + 3 * NSA_HEADS
    small.append(jnp.zeros((w_in_l.shape[0], PS_COLS - used), w_in_l.dtype))
    return w_main, jnp.concatenate(small, axis=1).astype(BF16)


def _pad_lanes(v, first=0):
    return jnp.pad(v.astype(F32), (first, LANES - first - v.shape[0])).reshape(1, LANES)


def _trunk(x, mem, norm_g, w_in, fox_f_bias, ssm_conv_w, ssm_conv_b, ssm_dt_bias, ssm_a_log, ssm_d,
           ssm_norm_g, nsa_cmp_pe, nsa_cmp_w1, nsa_cmp_w2, rel_bias_table, mem_norm_g, w_mem_kv, w_out,
           final_norm_g):
    b, s, d = x.shape
    depth = w_in.shape[0]
    n = b * s
    m_tok = mem.shape[1]
    n_cmp = (s - CMP_BLOCK) // CMP_STRIDE + 1
    n_rows = s // CMP_STRIDE
    assert s % 1024 == 0 and s // SEL_BLOCK <= HEAD_DIM and n_rows <= N_CMP_PAD and d == D_MODEL

    tab_flat = rel_bias_table.astype(F32).reshape(-1)
    unbounded = 1 << 30
    bwin = _t5_table(tab_flat, WINDOW + WIN_T, WIN_T, 1, WINDOW, WINDOW)
    bsel = _t5_table(tab_flat, (SEL_T // SEL_TK + 1) * SEL_TK, SEL_T, 1, SEL_TK, unbounded)
    fcmp = _t5_table(tab_flat, 2 * N_CMP_PAD, T_ATT, CMP_STRIDE, CMP_STRIDE * N_CMP_PAD - (CMP_BLOCK - 1), unbounded)

    cs = np.arange(N_CMP_PAD)[None, :] * CMP_STRIDE
    js = np.arange(SEL_BLOCK)[:, None] * SEL_BLOCK
    overlap_t = ((cs < js + SEL_BLOCK) & (cs + CMP_BLOCK > js) & (np.arange(N_CMP_PAD)[None, :] < n_cmp)
                 & (np.arange(SEL_BLOCK)[:, None] < s // SEL_BLOCK)).astype(np.float32)
    overlap_t = jnp.asarray(overlap_t, BF16)

    x2d = x.reshape(n, d)
    mem2d = mem.reshape(b * m_tok, d)
    for l in range(depth):
        w_main, w_small = _pack_in_proj(w_in[l])
        p, ps = _norm_proj(x2d, norm_g[l], (w_main, w_small), (BF16, F32))

        qaug, kaug = _fox_cumsum(ps, _pad_lanes(jnp.tile(fox_f_bias[l], N_SPLIT)), b, s)
        o_fox = _fox_attn(p, qaug, kaug, b, s)

        o_ssd = _ssd(p, ps, ssm_conv_w[l].astype(F32), ssm_conv_b[l].reshape(1, -1).astype(F32),
                     _pad_lanes(ssm_dt_bias[l], PS_DT_LANE), _pad_lanes(ssm_a_log[l], PS_DT_LANE),
                     jnp.repeat(ssm_d[l].astype(F32), HEAD_DIM).reshape(1, GROUP_W),
                     ssm_norm_g[l].reshape(1, GROUP_W).astype(F32), b, s)

        kv_cmp, kv_cmp_t = _compress(p, nsa_cmp_w1[l], nsa_cmp_w2[l], nsa_cmp_pe[l], b, s)
        o_cmp, notsel = _cmp_select(p, ps, kv_cmp, kv_cmp_t, fcmp, overlap_t, b, s)
        o_sel = _sel_attn(tab_flat, p, ps, notsel, bsel, b, s)
        o_win = _win_attn(p, ps, bwin, b, s)

        w_kv = w_mem_kv[l].astype(BF16)
        (mem_kv,) = _norm_proj(mem2d, mem_norm_g[l], (w_kv,), (BF16,), tm=min(512, b * m_tok))
        o_mem = _mem_attn(p, mem_kv, b, s)

        x2d = _out_proj(x2d, (o_fox, o_ssd, o_cmp, o_sel, o_win, o_mem), w_out[l].astype(BF16),
                        final_norm_g, final=(l == depth - 1))
    return x2d.reshape(b, s, d)


def kernel(x, mem, norm_g, w_in, fox_f_bias, ssm_conv_w, ssm_conv_b, ssm_dt_bias, ssm_a_log, ssm_d, ssm_norm_g,
           nsa_cmp_pe, nsa_cmp_w1, nsa_cmp_w2, rel_bias_table, mem_norm_g, w_mem_kv, w_out, final_norm_g):
    return _trunk(x, mem, norm_g, w_in, fox_f_bias, ssm_conv_w, ssm_conv_b, ssm_dt_bias, ssm_a_log, ssm_d,
                  ssm_norm_g, nsa_cmp_pe, nsa_cmp_w1, nsa_cmp_w2, rel_bias_table, mem_norm_g, w_mem_kv, w_out,
                  final_norm_g)
```

```python
import functools
import math

import numpy as np
import jax
import jax.numpy as jnp
from jax import lax
from jax.experimental import pallas as pl
from jax.experimental.pallas import tpu as pltpu

F32 = jnp.float32
BF16 = jnp.bfloat16

D_MODEL = 1024
GROUP_W = 512
HEAD_DIM = 64
EPS = 1e-6
NEG_INF = -1e30
TINY = 1e-30
LOG2E = math.log2(math.e)

FOX_HEADS = 8
SSM_HEADS = 8
SSM_STATE = 128
SSM_GROUPS = 2
SSM_CONV = 4
SSM_CHUNK = 128
SSM_CONV_DIM = GROUP_W + 2 * SSM_GROUPS * SSM_STATE

NSA_HEADS = 8
NSA_KV_HEADS = 2
NSA_REP = NSA_HEADS // NSA_KV_HEADS
NSA_KV_W = NSA_KV_HEADS * HEAD_DIM
CMP_BLOCK = 32
CMP_STRIDE = 16
CMP_HIDDEN = 2 * HEAD_DIM
SEL_BLOCK = 64
SEL_TOPK = 16
WINDOW = 512
SEL_FORCE = 1e9

MEM_HEADS = 4
MEM_HEAD_DIM = GROUP_W // MEM_HEADS
REL_BUCKETS = 32
REL_MAX_DIST = 128

FOX_COLS = 4 * GROUP_W + FOX_HEADS
SSM_COLS = GROUP_W + SSM_CONV_DIM + SSM_HEADS
NSA_COLS = 2 * GROUP_W + 6 * NSA_KV_W + 3 * NSA_HEADS
MEM_COLS = 2 * GROUP_W

LANES = 128
VMEM_LIMIT = 56 * 1024 * 1024

OFF_FOX_Q, OFF_FOX_K, OFF_FOX_V, OFF_FOX_G = 0, 512, 1024, 1536
OFF_SSM_XBC, OFF_SSM_Z = 2048, 3072
OFF_NSA_Q, OFF_NSA_G = 3584, 4096
OFF_MEM_Q, OFF_MEM_G = 4608, 5120
OFF_SEL_K, OFF_SEL_V, OFF_WIN_K, OFF_WIN_V, OFF_CMP_KV = 5632, 5760, 5888, 6016, 6144
P_COLS = 6400
PS_COLS = LANES
PS_DT_LANE = 24
PS_GATE_LANE = 32

T_ATT = 512
SEL_T, SEL_TK = 512, 256
WIN_T, WIN_TK = 512, 256
N_CMP_PAD = 256


def _params(sem):
    return pltpu.CompilerParams(dimension_semantics=sem, vmem_limit_bytes=VMEM_LIMIT)


def _t5_bucket_np(dist):
    n = np.maximum(dist, 0)
    max_exact = REL_BUCKETS // 2
    nf = np.maximum(n, 1).astype(np.float32)
    large = max_exact + (np.log(nf / np.float32(max_exact)) / np.float32(math.log(REL_MAX_DIST / max_exact))
                         * np.float32(REL_BUCKETS - max_exact)).astype(np.int32)
    large = np.minimum(large, REL_BUCKETS - 1)
    return np.where(n < max_exact, n, large).astype(np.int32)


def _silu(x):
    h = 0.5 * x
    return h + h * jnp.tanh(h)


def _norm_proj_kernel(x_ref, g_ref, *refs, chunk):
    x = x_ref[...]
    ms = jnp.mean(x * x, axis=-1, keepdims=True)
    h = ((x * lax.rsqrt(ms + EPS)) * g_ref[...]).astype(BF16)
    n_out = len(refs) // 2
    for w_ref, o_ref in zip(refs[:n_out], refs[n_out:]):
        ncol = o_ref.shape[1]
        for c0 in range(0, ncol, chunk):
            c1 = min(c0 + chunk, ncol)
            o_ref[:, c0:c1] = jnp.dot(h, w_ref[:, c0:c1], preferred_element_type=F32).astype(o_ref.dtype)


def _norm_proj(x2d, g, weights, out_dtypes, tm=512):
    n, d = x2d.shape
    return pl.pallas_call(
        functools.partial(_norm_proj_kernel, chunk=512),
        grid=(n // tm,),
        in_specs=[pl.BlockSpec((tm, d), lambda i: (i, 0)), pl.BlockSpec((1, d), lambda i: (0, 0))]
        + [pl.BlockSpec(w.shape, lambda i: (0, 0)) for w in weights],
        out_specs=[pl.BlockSpec((tm, w.shape[1]), lambda i: (i, 0)) for w in weights],
        out_shape=[jax.ShapeDtypeStruct((n, w.shape[1]), dt) for w, dt in zip(weights, out_dtypes)],
        compiler_params=_params(("parallel",)),
        name="norm_proj",
    )(x2d, g.reshape(1, d), *weights)


def _t5_table_kernel(tab_ref, bucket_ref, o_ref, *, stride, off, limit):
    h = pl.program_id(0)
    rows, cols = o_ref.shape[1], o_ref.shape[2]
    bucket = bucket_ref[...]
    near = jnp.zeros(bucket.shape, F32)
    for b in range(REL_BUCKETS):
        near = jnp.where(bucket == b, tab_ref[b * NSA_HEADS + h] * LOG2E, near)
    far = tab_ref[(REL_BUCKETS - 1) * NSA_HEADS + h] * LOG2E
    base = jnp.concatenate([near] * (rows // near.shape[0]), axis=0)
    rolled = pltpu.roll(base, 0, 1, stride=stride, stride_axis=0)
    d = (lax.broadcasted_iota(jnp.int32, (rows, cols), 1) - stride * lax.broadcasted_iota(jnp.int32, (rows, cols), 0)
         + off)
    o_ref[0] = jnp.where((d < 0) | (d >= limit), NEG_INF, jnp.where(d >= REL_MAX_DIST, far, rolled))


def _t5_table(tab_flat, rows, cols, stride, off, limit):
    assert cols >= 2 * REL_MAX_DIST and rows % 8 == 0
    k = (np.arange(cols) + off) % cols
    bucket = np.broadcast_to(np.where(k < REL_MAX_DIST, _t5_bucket_np(k), -1).astype(np.int32), (8, cols))
    return pl.pallas_call(
        functools.partial(_t5_table_kernel, stride=stride, off=off, limit=limit),
        grid=(NSA_HEADS,),
        in_specs=[
            pl.BlockSpec(memory_space=pltpu.SMEM),
            pl.BlockSpec((8, cols), lambda h: (0, 0)),
        ],
        out_specs=pl.BlockSpec((1, rows, cols), lambda h: (h, 0, 0)),
        out_shape=jax.ShapeDtypeStruct((NSA_HEADS, rows, cols), F32),
        compiler_params=_params(("arbitrary",)),
        name="t5_table",
    )(tab_flat, jnp.asarray(bucket))


def _tri_lower(n):
    r = lax.broadcasted_iota(jnp.int32, (n, n), 0)
    c = lax.broadcasted_iota(jnp.int32, (n, n), 1)
    return (r >= c).astype(F32)


N_SPLIT = 3


def _fox_aug_lane(h, i):
    return LANES * (h // 2) + (HEAD_DIM if h % 2 == 0 else 0) + i


def _fox_aug_consts():
    pq = np.zeros((LANES, GROUP_W), np.float32)
    pk = np.zeros((LANES, GROUP_W), np.float32)
    oq = np.zeros((1, GROUP_W), np.float32)
    ok = np.zeros((1, GROUP_W), np.float32)
    for h in range(FOX_HEADS):
        for i in range(N_SPLIT):
            pq[i * FOX_HEADS + h, _fox_aug_lane(h, i)] = 1.0
            pk[i * FOX_HEADS + h, _fox_aug_lane(h, N_SPLIT + i)] = -1.0
            oq[0, _fox_aug_lane(h, N_SPLIT + i)] = 1.0
            ok[0, _fox_aug_lane(h, i)] = 1.0
    return pq, pk, oq, ok


def _fox_cumsum_kernel(f_ref, b_ref, pq_ref, pk_ref, oq_ref, ok_ref, qa_ref, ka_ref, carry_ref, *, ts):
    @pl.when(pl.program_id(1) == 0)
    def _():
        carry_ref[...] = jnp.zeros_like(carry_ref)

    z = f_ref[...] + b_ref[...]
    logf = (jnp.minimum(z, 0.0) - jnp.log(1.0 + jnp.exp(-jnp.abs(z)))) * LOG2E
    tri = _tri_lower(LANES).astype(BF16)
    group = lax.broadcasted_iota(jnp.int32, (LANES, LANES), 1) // FOX_HEADS
    carry = carry_ref[...]
    for c in range(ts // LANES):
        rows = slice(c * LANES, (c + 1) * LANES)
        cs, rest = carry, logf[rows]
        for _ in range(N_SPLIT):
            piece = rest.astype(BF16)
            rest = rest - piece.astype(F32)
            cs = cs + jnp.dot(tri, piece, preferred_element_type=F32)
        carry = cs[LANES - 1:LANES, :]
        cat, rest = None, cs
        for i in range(N_SPLIT):
            piece = rest.astype(BF16).astype(F32)
            rest = rest - piece
            cat = piece if cat is None else jnp.where(group == i, piece, cat)
        cat = cat.astype(BF16)
        qa_ref[rows, :] = (oq_ref[...] + jnp.dot(cat, pq_ref[...], preferred_element_type=F32)).astype(BF16)
        ka_ref[rows, :] = (ok_ref[...] + jnp.dot(cat, pk_ref[...], preferred_element_type=F32)).astype(BF16)
    carry_ref[...] = carry


def _fox_cumsum(ps, f_bias_pad, b, s, ts=1024):
    ns = s // ts
    pq, pk, oq, ok = _fox_aug_consts()
    const = lambda bi, si: (0, 0)
    return pl.pallas_call(
        functools.partial(_fox_cumsum_kernel, ts=ts),
        grid=(b, ns),
        in_specs=[
            pl.BlockSpec((ts, LANES), lambda bi, si: (bi * ns + si, 0)),
            pl.BlockSpec((1, LANES), const),
            pl.BlockSpec(pq.shape, const),
            pl.BlockSpec(pk.shape, const),
            pl.BlockSpec(oq.shape, const),
            pl.BlockSpec(ok.shape, const),
        ],
        out_specs=[
            pl.BlockSpec((ts, GROUP_W), lambda bi, si: (bi * ns + si, 0)),
            pl.BlockSpec((ts, GROUP_W), lambda bi, si: (bi * ns + si, 0)),
        ],
        out_shape=[jax.ShapeDtypeStruct((b * s, GROUP_W), BF16), jax.ShapeDtypeStruct((b * s, GROUP_W), BF16)],
        scratch_shapes=[pltpu.VMEM((1, LANES), F32)],
        compiler_params=_params(("parallel", "arbitrary")),
        name="fox_cumsum",
    )(ps, f_bias_pad, jnp.asarray(pq, BF16), jnp.asarray(pk, BF16), jnp.asarray(oq), jnp.asarray(ok))


ONES_ROWS = 16


def _vt_rows(h):
    return slice(0, HEAD_DIM + ONES_ROWS) if h % 2 == 0 else slice(HEAD_DIM - ONES_ROWS, LANES)


def _store_vt(vt_ref, i, rows, v_pair):
    vt = v_pair.astype(F32).T.astype(BF16)
    ones = jnp.ones((HEAD_DIM, v_pair.shape[0]), BF16)
    vt_ref[i, 0:HEAD_DIM, rows] = vt[0:HEAD_DIM]
    vt_ref[i, HEAD_DIM:LANES, rows] = ones
    vt_ref[i + 1, 0:HEAD_DIM, rows] = ones
    vt_ref[i + 1, HEAD_DIM:LANES, rows] = vt[HEAD_DIM:LANES]


def _store_vt_groups(vt_ref, rows, v_groups):
    vt = v_groups.astype(F32).T.astype(BF16)
    ones = jnp.ones((HEAD_DIM, v_groups.shape[0]), BF16)
    for g in range(NSA_KV_HEADS):
        vg = vt[g * HEAD_DIM:(g + 1) * HEAD_DIM]
        vt_ref[2 * g, 0:HEAD_DIM, rows] = vg
        vt_ref[2 * g, HEAD_DIM:LANES, rows] = ones
        vt_ref[2 * g + 1, 0:HEAD_DIM, rows] = ones
        vt_ref[2 * g + 1, HEAD_DIM:LANES, rows] = vg


def _finish_pair(acc_ref, hp):
    ae, ao = acc_ref[2 * hp], acc_ref[2 * hp + 1]
    top = lax.broadcasted_iota(jnp.int32, ae.shape, 0) < HEAD_DIM
    both = jnp.where(top, ae / jnp.maximum(ae[HEAD_DIM:HEAD_DIM + 1, :], TINY),
                     ao / jnp.maximum(ao[HEAD_DIM - 1:HEAD_DIM, :], TINY))
    return both.T


def _swap_halves(x_b):
    return pltpu.roll(x_b.astype(F32), HEAD_DIM, 1).astype(BF16)


def _half_masks(rows):
    lo = jnp.where(lax.broadcasted_iota(jnp.int32, (rows, LANES), 1) < HEAD_DIM, 1.0, 0.0)
    return lo.astype(BF16), (1.0 - lo).astype(BF16)


def _fox_attn_kernel(q_ref, qa_ref, k_ref, ka_ref, v_ref, gate_ref, o_ref, qs_ref, kk_ref, vt_ref, m_ref, acc_ref,
                     pre_ref, *, t, tk, ahead):
    qi = pl.program_id(1)
    lo_q, hi_q = _half_masks(t)
    cm = lax.broadcasted_iota(jnp.int32, (tk, t), 0) - lax.broadcasted_iota(jnp.int32, (tk, t), 1)

    @pl.when(qi == 0)
    def _():
        lo_k, hi_k = _half_masks(tk)

        def merge(j, c):
            rows = pl.ds(pl.multiple_of(j * tk, tk), tk)
            for hp in range(FOX_HEADS // 2):
                ls = slice(hp * LANES, (hp + 1) * LANES)
                kp, ka, vp = k_ref[rows, ls], ka_ref[rows, ls], v_ref[rows, ls]
                kk_ref[2 * hp, rows, :] = kp * lo_k + ka * hi_k
                kk_ref[2 * hp + 1, rows, :] = ka * lo_k + kp * hi_k
                _store_vt(vt_ref, 2 * hp, rows, vp)
            return c

        lax.fori_loop(0, k_ref.shape[0] // tk, merge, 0)

    for hp in range(FOX_HEADS // 2):
        ls = slice(hp * LANES, (hp + 1) * LANES)
        qp, qa = q_ref[:, ls], qa_ref[:, ls]
        qs_ref[2 * hp] = (qp * lo_q + qa * hi_q).astype(F32).T.astype(BF16)
        qs_ref[2 * hp + 1] = (qa * lo_q + qp * hi_q).astype(F32).T.astype(BF16)
    m_ref[...] = jnp.full(m_ref.shape, NEG_INF, F32)
    acc_ref[...] = jnp.zeros_like(acc_ref)

    def rows_of(j):
        return pl.ds(pl.multiple_of(j * tk, tk), tk)

    def scores(j, h):
        return jnp.dot(kk_ref[h, rows_of(j), :], qs_ref[h], preferred_element_type=F32)

    def tile(j, diag, j_next):
        queue = [pre_ref[i] for i in range(ahead)]
        for h in range(FOX_HEADS):
            if h + ahead < FOX_HEADS:
                queue.append(scores(j, h + ahead))
            elif j_next is not None:
                pre_ref[h + ahead - FOX_HEADS] = scores(j_next, h + ahead - FOX_HEADS)
            s = queue.pop(0)
            if diag:
                s = jnp.where(cm <= qi * t - j * tk, s, NEG_INF)
            m_old = m_ref[h]
            m_new = jnp.maximum(m_old, jnp.max(s, axis=0, keepdims=True))
            m_ref[h] = m_new
            p = jnp.exp2(s - m_new).astype(BF16)
            vr = _vt_rows(h)
            acc_ref[h, vr] = jnp.exp2(m_old - m_new) * acc_ref[h, vr] + jnp.dot(vt_ref[h, vr, rows_of(j)], p,
                                                                                preferred_element_type=F32)

    jd = (qi * t) // tk
    n_diag = max(t // tk, 1)
    for i in range(ahead):
        pre_ref[i] = scores(0, i)

    def body(j, c):
        tile(j, False, j + 1)
        return c

    lax.fori_loop(0, jd, body, 0)
    for dj in range(n_diag):
        tile(jd + dj, True, jd + dj + 1 if dj + 1 < n_diag else None)

    for hp in range(FOX_HEADS // 2):
        ls = slice(hp * LANES, (hp + 1) * LANES)
        o_ref[:, ls] = (_finish_pair(acc_ref, hp) * _silu(gate_ref[:, ls].astype(F32))).astype(o_ref.dtype)


def _fox_attn(p, qaug, kaug, b, s, t=512, tk=256, ahead=2):
    nq = s // t
    w = GROUP_W
    return pl.pallas_call(
        functools.partial(_fox_attn_kernel, t=t, tk=tk, ahead=ahead),
        grid=(b, nq),
        in_specs=[
            pl.BlockSpec((t, w), lambda bi, qi: (bi * nq + qi, OFF_FOX_Q // w)),
            pl.BlockSpec((t, w), lambda bi, qi: (bi * nq + qi, 0)),
            pl.BlockSpec((s, w), lambda bi, qi: (bi, OFF_FOX_K // w)),
            pl.BlockSpec((s, w), lambda bi, qi: (bi, 0)),
            pl.BlockSpec((s, w), lambda bi, qi: (bi, OFF_FOX_V // w)),
            pl.BlockSpec((t, w), lambda bi, qi: (bi * nq + qi, OFF_FOX_G // w)),
        ],
        out_specs=pl.BlockSpec((t, w), lambda bi, qi: (bi * nq + qi, 0)),
        out_shape=jax.ShapeDtypeStruct((b * s, w), BF16),
        scratch_shapes=[
            pltpu.VMEM((FOX_HEADS, LANES, t), BF16),
            pltpu.VMEM((FOX_HEADS, s, LANES), BF16),
            pltpu.VMEM((FOX_HEADS, LANES, s), BF16),
            pltpu.VMEM((FOX_HEADS, 1, t), F32),
            pltpu.VMEM((FOX_HEADS, LANES, t), F32),
            pltpu.VMEM((ahead, tk, t), F32),
        ],
        compiler_params=_params(("parallel", "arbitrary")),
        name="fox_attn",
    )(p, qaug, p, kaug, p, p)


def _dot_split(lhs_f32, rhs_b):
    out, rest = None, lhs_f32
    for _ in range(N_SPLIT):
        piece = rest.astype(BF16)
        rest = rest - piece.astype(F32)
        term = jnp.dot(piece, rhs_b, preferred_element_type=F32)
        out = term if out is None else out + term
    return out


def _ssd_kernel(z_ref, xbc_ref, dt_ref, cw_ref, cb_ref, dtb_ref, alog_ref, dsk_ref, ng_ref, o_ref,
                xpad_ref, xc_ref, state_ref, y_ref, *, nch):
    q = SSM_CHUNK
    rows_all = nch * q
    halo = 8

    @pl.when(pl.program_id(1) == 0)
    def _():
        xpad_ref[0:halo, :] = jnp.zeros((halo, SSM_CONV_DIM), F32)
        state_ref[...] = jnp.zeros_like(state_ref)

    xpad_ref[halo:halo + rows_all, :] = xbc_ref[...].astype(F32)
    y = cb_ref[...]
    for k in range(SSM_CONV):
        off = halo - (SSM_CONV - 1) + k
        y = y + cw_ref[k:k + 1, :] * xpad_ref[off:off + rows_all, :]
    xpad_ref[0:halo, :] = xpad_ref[rows_all:rows_all + halo, :]
    xc_ref[...] = _silu(y)

    x_dt = dt_ref[...] + dtb_ref[...]
    dt_all = jnp.maximum(x_dt, 0.0) + jnp.log(1.0 + jnp.exp(-jnp.abs(x_dt)))
    a_all = dt_all * (-jnp.exp(alog_ref[...]))
    tri_t = (lax.broadcasted_iota(jnp.int32, (q, q), 0) <= lax.broadcasted_iota(jnp.int32, (q, q), 1)).astype(BF16)
    er = lax.broadcasted_iota(jnp.int32, (LANES, GROUP_W), 0)
    ec = lax.broadcasted_iota(jnp.int32, (LANES, GROUP_W), 1)
    expand = jnp.where(ec // HEAD_DIM == er - PS_DT_LANE, 1.0, 0.0).astype(BF16)
    row = lax.broadcasted_iota(jnp.int32, (q, q), 0)
    col = lax.broadcasted_iota(jnp.int32, (q, q), 1)
    causal = row >= col
    lo = lax.broadcasted_iota(jnp.int32, (q, LANES), 1) < HEAD_DIM
    gw = GROUP_W // SSM_GROUPS
    hpg = SSM_HEADS // SSM_GROUPS

    for c in range(nch):
        rs = slice(c * q, (c + 1) * q)
        xs = xc_ref[rs, :GROUP_W]
        dt = dt_all[rs]
        acs_t = _dot_split(a_all[rs].T, tri_t)
        acs = acs_t.T
        stacked = jnp.concatenate([dt, jnp.exp(acs), jnp.exp(acs[q - 1:q, :] - acs)], axis=0)
        full = _dot_split(stacked, expand)
        dt_full, eacs_full, dec_full = full[0:q], full[q:2 * q], full[2 * q:3 * q]
        xdt = xs * dt_full
        xdt_b = xdt.astype(BF16)
        xdec_b = (xdt * dec_full).astype(BF16)

        for g in range(SSM_GROUPS):
            bm = xc_ref[rs, GROUP_W + g * SSM_STATE:GROUP_W + (g + 1) * SSM_STATE]
            cm = xc_ref[rs, GROUP_W + (SSM_GROUPS + g) * SSM_STATE:GROUP_W + (SSM_GROUPS + g + 1) * SSM_STATE]
            bm_b = bm.astype(BF16)
            cm_b = cm.astype(BF16)
            gs = slice(g * gw, (g + 1) * gw)
            cbg = lax.dot_general(cm_b, bm_b, (((1,), (1,)), ((), ())), preferred_element_type=F32)
            st = state_ref[:, gs]
            y_off = jnp.dot(cm_b, st.astype(BF16), preferred_element_type=F32) * eacs_full[:, gs]
            cst = jnp.dot(bm.T.astype(BF16), xdec_b[:, gs], preferred_element_type=F32)
            state_ref[:, gs] = st * eacs_full[q - 1:q, gs] + cst
            for hp in range(hpg // 2):
                ls = slice(g * gw + hp * LANES, g * gw + (hp + 1) * LANES)
                yd = []
                for e in range(2):
                    h = g * hpg + 2 * hp + e
                    hl = PS_DT_LANE + h
                    seg = jnp.exp(jnp.where(causal, acs[:, hl:hl + 1] - acs_t[hl:hl + 1, :], NEG_INF))
                    yd.append(jnp.dot((cbg * seg).astype(BF16), xdt_b[:, ls], preferred_element_type=F32))
                y_ref[:, ls] = jnp.where(lo, yd[0], yd[1]) + y_off[:, hp * LANES:(hp + 1) * LANES]

        yz = (y_ref[...] + xs * dsk_ref[...]) * _silu(z_ref[rs, :].astype(F32))
        for g in range(SSM_GROUPS):
            gs = slice(g * gw, (g + 1) * gw)
            blk = yz[:, gs]
            ms = jnp.mean(blk * blk, axis=-1, keepdims=True)
            o_ref[rs, gs] = ((blk * lax.rsqrt(ms + EPS)) * ng_ref[:, gs]).astype(o_ref.dtype)


def _ssd(p, ps, conv_w, conv_b, dt_bias_pad, a_log_pad, d_full, norm_g, b, s, nch=8):
    q = SSM_CHUNK
    rows = nch * q
    nc = s // rows
    row = lambda bi, ci: (bi * nc + ci)
    const = lambda bi, ci: (0, 0)
    return pl.pallas_call(
        functools.partial(_ssd_kernel, nch=nch),
        grid=(b, nc),
        in_specs=[
            pl.BlockSpec((rows, GROUP_W), lambda bi, ci: (row(bi, ci), OFF_SSM_Z // GROUP_W)),
            pl.BlockSpec((rows, SSM_CONV_DIM), lambda bi, ci: (row(bi, ci), OFF_SSM_XBC // SSM_CONV_DIM)),
            pl.BlockSpec((rows, LANES), lambda bi, ci: (row(bi, ci), 0)),
            pl.BlockSpec((SSM_CONV, SSM_CONV_DIM), const),
            pl.BlockSpec((1, SSM_CONV_DIM), const),
            pl.BlockSpec((1, LANES), const),
            pl.BlockSpec((1, LANES), const),
            pl.BlockSpec((1, GROUP_W), const),
            pl.BlockSpec((1, GROUP_W), const),
        ],
        out_specs=pl.BlockSpec((rows, GROUP_W), lambda bi, ci: (row(bi, ci), 0)),
        out_shape=jax.ShapeDtypeStruct((b * s, GROUP_W), BF16),
        scratch_shapes=[
            pltpu.VMEM((rows + 8, SSM_CONV_DIM), F32),
            pltpu.VMEM((rows, SSM_CONV_DIM), F32),
            pltpu.VMEM((SSM_STATE, GROUP_W), F32),
            pltpu.VMEM((q, GROUP_W), F32),
        ],
        compiler_params=_params(("parallel", "arbitrary")),
        name="ssd",
    )(p, p, ps, conv_w, conv_b, dt_bias_pad, a_log_pad, d_full, norm_g)


def _compress_kernel(x_ref, w1_ref, w2_ref, pe_ref, o_ref, ot_ref, xf_ref):
    s = x_ref.shape[0]
    n = s // CMP_STRIDE
    for sb in range(2):
        xf_ref[sb] = x_ref[:, sb * LANES:(sb + 1) * LANES].astype(F32)
    o_ref[...] = jnp.zeros_like(o_ref)
    ot_ref[...] = jnp.zeros_like(ot_ref)
    for sb in range(2):
        first = jnp.zeros((n, 2 * CMP_HIDDEN), F32)
        second = jnp.zeros((n, 2 * CMP_HIDDEN), F32)
        for l in range(CMP_STRIDE):
            xl = xf_ref[sb, pl.ds(l, n, stride=CMP_STRIDE), :]
            first = first + jnp.dot((xl + pe_ref[sb, l:l + 1, :]).astype(BF16), w1_ref[sb, l],
                                    preferred_element_type=F32)
            second = second + jnp.dot((xl + pe_ref[sb, CMP_STRIDE + l:CMP_STRIDE + l + 1, :]).astype(BF16),
                                      w1_ref[sb, CMP_STRIDE + l], preferred_element_type=F32)
        h = _silu(first + pltpu.roll(second, n - 1, 0))
        o = jnp.dot(h.astype(BF16), w2_ref[sb], preferred_element_type=F32)
        for e in range(2):
            oe = o[:, e * LANES:(e + 1) * LANES]
            o_ref[0, 2 * sb + e, 0:n, :] = oe.astype(o_ref.dtype)
            ot_ref[0, 2 * sb + e, :, 0:n] = oe.T.astype(ot_ref.dtype)


def _blockdiag2(a):
    z = jnp.zeros_like(a)
    return jnp.concatenate([jnp.concatenate([a, z], axis=-1), jnp.concatenate([z, a], axis=-1)], axis=-2)


def _compress(p, cmp_w1, cmp_w2, cmp_pe, b, s):
    nslot = 2 * NSA_KV_HEADS
    w1 = _blockdiag2(cmp_w1.astype(BF16).reshape(2, CMP_BLOCK, HEAD_DIM, CMP_HIDDEN))
    w2 = _blockdiag2(jnp.concatenate([cmp_w2, cmp_w2], axis=-1).astype(BF16))
    pe = jnp.concatenate([cmp_pe, cmp_pe], axis=-1).astype(F32)
    kw = 2 * NSA_KV_W
    return pl.pallas_call(
        _compress_kernel,
        grid=(b,),
        in_specs=[
            pl.BlockSpec((s, kw), lambda bi: (bi, OFF_CMP_KV // kw)),
            pl.BlockSpec(w1.shape, lambda bi: (0, 0, 0, 0)),
            pl.BlockSpec(w2.shape, lambda bi: (0, 0, 0)),
            pl.BlockSpec(pe.shape, lambda bi: (0, 0, 0)),
        ],
        out_specs=[pl.BlockSpec((1, nslot, N_CMP_PAD, LANES), lambda bi: (bi, 0, 0, 0)),
                   pl.BlockSpec((1, nslot, LANES, N_CMP_PAD), lambda bi: (bi, 0, 0, 0))],
        out_shape=[jax.ShapeDtypeStruct((b, nslot, N_CMP_PAD, LANES), BF16),
                   jax.ShapeDtypeStruct((b, nslot, LANES, N_CMP_PAD), BF16)],
        scratch_shapes=[pltpu.VMEM((2, s, LANES), F32)],
        compiler_params=_params(("parallel",)),
        name="nsa_compress",
    )(p, w1, w2, pe)


def _cmp_select_kernel(q_ref, kv_ref, vt_ref, gl_ref, gate_ref, fc_ref, ovt_ref, o_ref, ns_ref):
    t = T_ATT
    ncp = kv_ref.shape[2]
    qi = pl.program_id(1)
    t0 = qi * t
    lo = lax.broadcasted_iota(jnp.int32, (t, LANES), 1) < HEAD_DIM
    zero = jnp.zeros((t, LANES), BF16)
    start = pl.multiple_of(ncp - qi * (t // CMP_STRIDE), CMP_STRIDE)
    gl = gl_ref[...]

    nsel = SEL_BLOCK
    jrow = lax.broadcasted_iota(jnp.int32, (nsel, t), 0)
    cur = (t0 + lax.broadcasted_iota(jnp.int32, (nsel, t), 1)) // SEL_BLOCK
    forced = (jrow == 0) | (jrow == cur) | (jrow == cur - 1)
    past = jrow <= cur

    for g in range(NSA_KV_HEADS):
        kc = kv_ref[0, g]
        vct = vt_ref[0, NSA_KV_HEADS + g]
        psum = jnp.zeros((ncp, t), F32)
        outs = []

        def scores(r):
            h = g * NSA_REP + r
            qp = q_ref[:, (h // 2) * LANES:(h // 2 + 1) * LANES]
            qe = jnp.where(lo, qp, zero) if r % 2 == 0 else jnp.where(lo, zero, qp)
            return lax.dot_general(kc, qe, (((1,), (1,)), ((), ())), preferred_element_type=F32)

        queue = [scores(0)]
        for r in range(NSA_REP):
            h = g * NSA_REP + r
            if r + 1 < NSA_REP:
                queue.append(scores(r + 1))
            s = fc_ref[h, pl.ds(start, ncp), :] + queue.pop(0)
            m = jnp.maximum(jnp.max(s, axis=0, keepdims=True), 0.1 * NEG_INF)
            e = jnp.exp2(s - m)
            pr = e * (1.0 / jnp.maximum(jnp.sum(e, axis=0, keepdims=True), TINY))
            psum = psum + pr
            oc = jnp.dot(vct, pr.astype(BF16), preferred_element_type=F32)
            outs.append(oc.T * jax.nn.sigmoid(gl[:, PS_GATE_LANE + h:PS_GATE_LANE + h + 1]))
        for hp in range(NSA_REP // 2):
            ls = slice((g * NSA_REP // 2 + hp) * LANES, (g * NSA_REP // 2 + hp + 1) * LANES)
            o = jnp.where(lo, outs[2 * hp], outs[2 * hp + 1])
            o_ref[:, ls] = (o * _silu(gate_ref[:, ls].astype(F32))).astype(o_ref.dtype)

        imp_t = jnp.zeros((nsel, t), F32)
        rest = psum
        for _ in range(N_SPLIT):
            piece = rest.astype(BF16)
            rest = rest - piece.astype(F32)
            imp_t = imp_t + jnp.dot(ovt_ref[...], piece, preferred_element_type=F32)
        imp_t = jnp.where(past, jnp.where(forced, SEL_FORCE, imp_t), -SEL_FORCE)
        sub = 8
        sub_row = lax.broadcasted_iota(jnp.int32, (sub, t), 0)
        rows = [imp_t[k * sub:(k + 1) * sub] for k in range(nsel // sub)]
        rank = [jnp.zeros((sub, t), F32) for _ in rows]
        for i in range(nsel):
            bi = imp_t[i:i + 1, :]
            for k, x in enumerate(rows):
                if k * sub > i:
                    ahead = jnp.where(bi >= x, 1.0, 0.0)
                elif k * sub + sub - 1 <= i:
                    ahead = jnp.where(bi > x, 1.0, 0.0)
                else:
                    ahead = jnp.where(sub_row > i - k * sub, jnp.where(bi >= x, 1.0, 0.0),
                                      jnp.where(bi > x, 1.0, 0.0))
                rank[k] = rank[k] + ahead
        rank = jnp.concatenate(rank, axis=0)
        notsel = jnp.where((rank < float(SEL_TOPK)) & past, 0.0, 1.0)
        ns2 = jnp.concatenate([notsel, notsel], axis=0).T
        ns_ref[:, g * LANES:(g + 1) * LANES] = ns2.astype(ns_ref.dtype)


def _cmp_select(p, ps, kvc, kvc_t, fc, overlap_t, b, s):
    t = T_ATT
    nq = s // t
    w = GROUP_W
    ncp = kvc.shape[2]
    return pl.pallas_call(
        _cmp_select_kernel,
        grid=(b, nq),
        in_specs=[
            pl.BlockSpec((t, w), lambda bi, qi: (bi * nq + qi, OFF_NSA_Q // w)),
            pl.BlockSpec((1, 2 * NSA_KV_HEADS, ncp, LANES), lambda bi, qi: (bi, 0, 0, 0)),
            pl.BlockSpec((1, 2 * NSA_KV_HEADS, LANES, ncp), lambda bi, qi: (bi, 0, 0, 0)),
            pl.BlockSpec((t, LANES), lambda bi, qi: (bi * nq + qi, 0)),
            pl.BlockSpec((t, w), lambda bi, qi: (bi * nq + qi, OFF_NSA_G // w)),
            pl.BlockSpec((NSA_HEADS, 2 * ncp, t), lambda bi, qi: (0, 0, 0)),
            pl.BlockSpec((SEL_BLOCK, ncp), lambda bi, qi: (0, 0)),
        ],
        out_specs=[
            pl.BlockSpec((t, w), lambda bi, qi: (bi * nq + qi, 0)),
            pl.BlockSpec((t, NSA_KV_HEADS * LANES), lambda bi, qi: (bi * nq + qi, 0)),
        ],
        out_shape=[jax.ShapeDtypeStruct((b * s, w), BF16),
                   jax.ShapeDtypeStruct((b * s, NSA_KV_HEADS * LANES), BF16)],
        compiler_params=_params(("parallel", "arbitrary")),
        name="nsa_cmp_select",
    )(p, kvc, kvc_t, ps, p, fc, overlap_t)


def _sel_attn_kernel(tab_ref, q_ref, ns_ref, k_ref, v_ref, gl_ref, gate_ref, bs_ref, o_ref,
                     qa_ref, kk_ref, vt_ref, m_ref, acc_ref, pre_ref, *, t, tk, ahead):
    qi = pl.program_id(1)
    lo_q, hi_q = _half_masks(t)
    lo = lax.broadcasted_iota(jnp.int32, (t, LANES), 1) < HEAD_DIM
    n_near = t // tk + 1

    @pl.when(qi == 0)
    def _():
        lane = lax.broadcasted_iota(jnp.int32, (tk, LANES), 1)
        krow = lax.broadcasted_iota(jnp.int32, (tk, LANES), 0)
        lo_f = jnp.where(lane < HEAD_DIM, 1.0, 0.0)
        hi_f = 1.0 - lo_f
        lo_k, hi_k = lo_f.astype(BF16), hi_f.astype(BF16)

        def merge(j, c):
            ks = pl.multiple_of(j * tk, tk)
            rows = pl.ds(ks, tk)
            hot = jnp.where((lane % HEAD_DIM) == (ks + krow) // SEL_BLOCK, NEG_INF, 0.0)
            hot_lo, hot_hi = (hot * lo_f).astype(BF16), (hot * hi_f).astype(BF16)
            k01 = k_ref[rows, :]
            k10 = _swap_halves(k01)
            kk_ref[0, rows, :] = k01 * lo_k + hot_hi
            kk_ref[1, rows, :] = hot_lo + k10 * hi_k
            kk_ref[2, rows, :] = k10 * lo_k + hot_hi
            kk_ref[3, rows, :] = hot_lo + k01 * hi_k
            _store_vt_groups(vt_ref, rows, v_ref[rows, :])
            return c

        lax.fori_loop(0, k_ref.shape[0] // tk, merge, 0)

    for h in range(NSA_HEADS):
        g = h // NSA_REP
        qp = q_ref[:, (h // 2) * LANES:(h // 2 + 1) * LANES]
        ns = ns_ref[:, g * LANES:(g + 1) * LANES]
        qa = qp * lo_q + ns * hi_q if h % 2 == 0 else ns * lo_q + qp * hi_q
        qa_ref[h] = qa.astype(F32).T.astype(BF16)
    m_ref[...] = jnp.full(m_ref.shape, NEG_INF, F32)
    acc_ref[...] = jnp.zeros_like(acc_ref)

    kv = lambda h: 2 * (h // NSA_REP) + h % 2

    def rows_of(j):
        return pl.ds(pl.multiple_of(j * tk, tk), tk)

    def scores(j, h):
        return jnp.dot(kk_ref[kv(h), rows_of(j), :], qa_ref[h], preferred_element_type=F32)

    def tile(j, near, j_next):
        rows = rows_of(j)
        queue = [pre_ref[i] for i in range(ahead)]
        for h in range(NSA_HEADS):
            if h + ahead < NSA_HEADS:
                queue.append(scores(j, h + ahead))
            elif j_next is not None:
                pre_ref[h + ahead - NSA_HEADS] = scores(j_next, h + ahead - NSA_HEADS)
            s = queue.pop(0)
            m_old = m_ref[h]
            if near is None:
                far = tab_ref[(REL_BUCKETS - 1) * NSA_HEADS + h] * LOG2E
                m_new = jnp.maximum(m_old, jnp.max(s, axis=0, keepdims=True) + far)
                p = jnp.exp2(s - (m_new - far)).astype(BF16)
            else:
                s = bs_ref[h, near * tk:(near + 1) * tk, :] + s
                m_new = jnp.maximum(m_old, jnp.max(s, axis=0, keepdims=True))
                p = jnp.exp2(s - m_new).astype(BF16)
            m_ref[h] = m_new
            vr = _vt_rows(h)
            acc_ref[h, vr] = jnp.exp2(m_old - m_new) * acc_ref[h, vr] + jnp.dot(vt_ref[kv(h), vr, rows], p,
                                                                                preferred_element_type=F32)

    jd = (qi * t) // tk
    for i in range(ahead):
        pre_ref[i] = scores(0, i)

    def body(j, c):
        tile(j, None, j + 1)
        return c

    lax.fori_loop(0, jnp.maximum(jd - 1, 0), body, 0)

    @pl.when(qi > 0)
    def _():
        tile(jd - 1, 0, jd)

    for dj in range(n_near - 1):
        tile(jd + dj, 1 + dj, jd + dj + 1 if dj + 2 < n_near else None)

    gl = gl_ref[...]
    for hp in range(NSA_HEADS // 2):
        ls = slice(hp * LANES, (hp + 1) * LANES)
        c0 = PS_GATE_LANE + NSA_HEADS + 2 * hp
        branch = jnp.where(lo, jax.nn.sigmoid(gl[:, c0:c0 + 1]), jax.nn.sigmoid(gl[:, c0 + 1:c0 + 2]))
        o_ref[:, ls] = (_finish_pair(acc_ref, hp) * branch * _silu(gate_ref[:, ls].astype(F32))).astype(o_ref.dtype)


def _sel_attn(tab_flat, p, ps, notsel, bsel, b, s, t=512, tk=256, ahead=2):
    nq = s // t
    w = GROUP_W
    kw = NSA_KV_HEADS * LANES
    n_near = t // tk + 1
    return pl.pallas_call(
        functools.partial(_sel_attn_kernel, t=t, tk=tk, ahead=ahead),
        grid=(b, nq),
        in_specs=[
            pl.BlockSpec(memory_space=pltpu.SMEM),
            pl.BlockSpec((t, w), lambda bi, qi: (bi * nq + qi, OFF_NSA_Q // w)),
            pl.BlockSpec((t, kw), lambda bi, qi: (bi * nq + qi, 0)),
            pl.BlockSpec((s, NSA_KV_W), lambda bi, qi: (bi, OFF_SEL_K // NSA_KV_W)),
            pl.BlockSpec((s, NSA_KV_W), lambda bi, qi: (bi, OFF_SEL_V // NSA_KV_W)),
            pl.BlockSpec((t, LANES), lambda bi, qi: (bi * nq + qi, 0)),
            pl.BlockSpec((t, w), lambda bi, qi: (bi * nq + qi, OFF_NSA_G // w)),
            pl.BlockSpec((NSA_HEADS, n_near * tk, t), lambda bi, qi: (0, 0, 0), pipeline_mode=pl.Buffered(1)),
        ],
        out_specs=pl.BlockSpec((t, w), lambda bi, qi: (bi * nq + qi, 0)),
        out_shape=jax.ShapeDtypeStruct((b * s, w), BF16),
        scratch_shapes=[
            pltpu.VMEM((NSA_HEADS, LANES, t), BF16),
            pltpu.VMEM((2 * NSA_KV_HEADS, s, LANES), BF16),
            pltpu.VMEM((2 * NSA_KV_HEADS, LANES, s), BF16),
            pltpu.VMEM((NSA_HEADS, 1, t), F32),
            pltpu.VMEM((NSA_HEADS, LANES, t), F32),
            pltpu.VMEM((ahead, tk, t), F32),
        ],
        compiler_params=_params(("parallel", "arbitrary")),
        name="nsa_sel_attn",
    )(tab_flat, p, notsel, p, p, ps, p, bsel)


def _win_attn_kernel(q_ref, k_ref, v_ref, gl_ref, gate_ref, bw_ref, o_ref, qe_ref, kk_ref, vt_ref, m_ref, acc_ref,
                     pre_ref, *, t, tk, ahead):
    qi = pl.program_id(1)
    lo_q, hi_q = _half_masks(t)
    lo = lax.broadcasted_iota(jnp.int32, (t, LANES), 1) < HEAD_DIM
    n_before = WINDOW // tk
    n_tiles = n_before + t // tk

    @pl.when(qi == 0)
    def _():
        lo_k, hi_k = _half_masks(tk)

        def merge(j, c):
            rows = pl.ds(pl.multiple_of(j * tk, tk), tk)
            k01 = k_ref[rows, :]
            k10 = _swap_halves(k01)
            kk_ref[0, rows, :] = k01 * lo_k + k10 * hi_k
            kk_ref[1, rows, :] = k10 * lo_k + k01 * hi_k
            _store_vt_groups(vt_ref, rows, v_ref[rows, :])
            return c

        lax.fori_loop(0, v_ref.shape[0] // tk, merge, 0)

    for h in range(NSA_HEADS):
        qp = q_ref[:, (h // 2) * LANES:(h // 2 + 1) * LANES]
        qe_ref[h] = qp * lo_q if h % 2 == 0 else qp * hi_q
    m_ref[...] = jnp.full(m_ref.shape, NEG_INF, F32)
    acc_ref[...] = jnp.zeros_like(acc_ref)

    def rows_of(w):
        return pl.ds(pl.multiple_of((qi * (t // tk) - n_before + w) * tk, tk), tk)

    def scores(w, h):
        return lax.dot_general(kk_ref[h // NSA_REP, rows_of(w), :], qe_ref[h], (((1,), (1,)), ((), ())),
                               preferred_element_type=F32)

    def tile(w):
        rows = rows_of(w)
        queue = [pre_ref[i] for i in range(ahead)]
        for h in range(NSA_HEADS):
            if h + ahead < NSA_HEADS:
                queue.append(scores(w, h + ahead))
            elif w + 1 < n_tiles:
                pre_ref[h + ahead - NSA_HEADS] = scores(w + 1, h + ahead - NSA_HEADS)
            s = bw_ref[h, w * tk:(w + 1) * tk, :] + queue.pop(0)
            m_old = m_ref[h]
            m_new = jnp.maximum(m_old, jnp.max(s, axis=0, keepdims=True))
            m_ref[h] = m_new
            p = jnp.exp2(s - m_new).astype(BF16)
            vr = _vt_rows(h)
            acc_ref[h, vr] = jnp.exp2(m_old - m_new) * acc_ref[h, vr] + jnp.dot(
                vt_ref[2 * (h // NSA_REP) + h % 2, vr, rows], p, preferred_element_type=F32)

    first = jnp.maximum(n_before - qi * (t // tk), 0)
    for i in range(ahead):
        pre_ref[i] = scores(first, i)
    per_q = t // tk
    first_qi = [-(-(n_before - w) // per_q) if w < n_before else 0 for w in range(n_tiles)]
    for need in sorted(set(first_qi), reverse=True):
        group = [w for w in range(n_tiles) if first_qi[w] == need]

        def run(group=group):
            for w in group:
                tile(w)

        if need > 0:
            pl.when(qi >= need)(run)
        else:
            run()

    gl = gl_ref[...]
    for hp in range(NSA_HEADS // 2):
        ls = slice(hp * LANES, (hp + 1) * LANES)
        c0 = PS_GATE_LANE + 2 * NSA_HEADS + 2 * hp
        branch = jnp.where(lo, jax.nn.sigmoid(gl[:, c0:c0 + 1]), jax.nn.sigmoid(gl[:, c0 + 1:c0 + 2]))
        o_ref[:, ls] = (_finish_pair(acc_ref, hp) * branch * _silu(gate_ref[:, ls].astype(F32))).astype(o_ref.dtype)


def _win_attn(p, ps, bwin, b, s, t=WIN_T, tk=WIN_TK, ahead=2):
    assert t % tk == 0 and WINDOW % tk == 0
    nq = s // t
    w = GROUP_W
    return pl.pallas_call(
        functools.partial(_win_attn_kernel, t=t, tk=tk, ahead=ahead),
        grid=(b, nq),
        in_specs=[
            pl.BlockSpec((t, w), lambda bi, qi: (bi * nq + qi, OFF_NSA_Q // w)),
            pl.BlockSpec((s, NSA_KV_W), lambda bi, qi: (bi, OFF_WIN_K // NSA_KV_W)),
            pl.BlockSpec((s, NSA_KV_W), lambda bi, qi: (bi, OFF_WIN_V // NSA_KV_W)),
            pl.BlockSpec((t, LANES), lambda bi, qi: (bi * nq + qi, 0)),
            pl.BlockSpec((t, w), lambda bi, qi: (bi * nq + qi, OFF_NSA_G // w)),
            pl.BlockSpec((NSA_HEADS, WINDOW + t, t), lambda bi, qi: (0, 0, 0), pipeline_mode=pl.Buffered(1)),
        ],
        out_specs=pl.BlockSpec((t, w), lambda bi, qi: (bi * nq + qi, 0)),
        out_shape=jax.ShapeDtypeStruct((b * s, w), BF16),
        scratch_shapes=[
            pltpu.VMEM((NSA_HEADS, t, LANES), BF16),
            pltpu.VMEM((NSA_KV_HEADS, s, LANES), BF16),
            pltpu.VMEM((2 * NSA_KV_HEADS, LANES, s), BF16),
            pltpu.VMEM((NSA_HEADS, 1, t), F32),
            pltpu.VMEM((NSA_HEADS, LANES, t), F32),
            pltpu.VMEM((ahead, tk, t), F32),
        ],
        compiler_params=_params(("parallel", "arbitrary")),
        name="nsa_win_attn",
    )(p, p, p, ps, p, bwin)


def _mem_attn_kernel(q_ref, gate_ref, kv_ref, o_ref):
    scale = MEM_HEAD_DIM ** -0.5
    for h in range(MEM_HEADS):
        ls = slice(h * LANES, (h + 1) * LANES)
        k = kv_ref[:, ls]
        v = kv_ref[:, GROUP_W + h * LANES:GROUP_W + (h + 1) * LANES]
        s = lax.dot_general(q_ref[:, ls], k, (((1,), (1,)), ((), ())), preferred_element_type=F32) * scale
        m = jnp.max(s, axis=1, keepdims=True)
        e = jnp.exp(s - m)
        l = jnp.sum(e, axis=1, keepdims=True)
        o = jnp.dot(e.astype(BF16), v, preferred_element_type=F32) / l
        o_ref[:, ls] = (o * _silu(gate_ref[:, ls].astype(F32))).astype(o_ref.dtype)


def _mem_attn(p, mem_kv, b, s, t=1024):
    nq = s // t
    w = GROUP_W
    m = mem_kv.shape[0] // b
    return pl.pallas_call(
        _mem_attn_kernel,
        grid=(b, nq),
        in_specs=[
            pl.BlockSpec((t, w), lambda bi, qi: (bi * nq + qi, OFF_MEM_Q // w)),
            pl.BlockSpec((t, w), lambda bi, qi: (bi * nq + qi, OFF_MEM_G // w)),
            pl.BlockSpec((m, 2 * w), lambda bi, qi: (bi, 0)),
        ],
        out_specs=pl.BlockSpec((t, w), lambda bi, qi: (bi * nq + qi, 0)),
        out_shape=jax.ShapeDtypeStruct((b * s, w), BF16),
        compiler_params=_params(("parallel", "arbitrary")),
        name="mem_attn",
    )(p, p, mem_kv)


def _out_proj_kernel(x_ref, of_ref, os_ref, oc_ref, osel_ref, ow_ref, om_ref, w_ref, g_ref, o_ref, *, final):
    w = GROUP_W
    nsa = (oc_ref[...].astype(F32) + osel_ref[...].astype(F32) + ow_ref[...].astype(F32)).astype(BF16)
    acc = x_ref[...]
    for i, part in enumerate((of_ref[...], os_ref[...], nsa, om_ref[...])):
        acc = acc + jnp.dot(part, w_ref[i * w:(i + 1) * w, :], preferred_element_type=F32)
    if final:
        ms = jnp.mean(acc * acc, axis=-1, keepdims=True)
        acc = (acc * lax.rsqrt(ms + EPS)) * g_ref[...]
    o_ref[...] = acc


def _out_proj(x2d, parts, w_out, g, final, tm=1024):
    n, d = x2d.shape
    w = GROUP_W
    part_spec = pl.BlockSpec((tm, w), lambda i: (i, 0))
    return pl.pallas_call(
        functools.partial(_out_proj_kernel, final=final),
        grid=(n // tm,),
        in_specs=[pl.BlockSpec((tm, d), lambda i: (i, 0))] + [part_spec] * 6 + [
            pl.BlockSpec((4 * w, d), lambda i: (0, 0)),
            pl.BlockSpec((1, d), lambda i: (0, 0)),
        ],
        out_specs=pl.BlockSpec((tm, d), lambda i: (i, 0)),
        out_shape=jax.ShapeDtypeStruct((n, d), F32),
        compiler_params=_params(("parallel",)),
        name="out_proj",
    )(x2d, *parts, w_out, g.reshape(1, d))


def _pack_in_proj(w_in_l):
    fox, ssm, nsa, mem = 0, FOX_COLS, FOX_COLS + SSM_COLS, FOX_COLS + SSM_COLS + NSA_COLS
    w = GROUP_W
    q_scale = HEAD_DIM ** -0.5 * LOG2E
    cols = lambda a, n: w_in_l[:, a:a + n]
    kv = lambda slot: nsa + w + slot * NSA_KV_W
    main = ([cols(fox, w) * q_scale, cols(fox + w, 3 * w)]
            + [cols(ssm + w, SSM_CONV_DIM), cols(ssm, w)]
            + [cols(nsa, w) * q_scale, cols(nsa + w + 6 * NSA_KV_W + 3 * NSA_HEADS, w)]
            + [cols(mem, 2 * w)]
            + [cols(kv(2), 4 * NSA_KV_W)]
            + [cols(kv(0), 2 * NSA_KV_W)])
    w_main = jnp.concatenate(main, axis=1).astype(BF16)
    assert w_main.shape[1] == P_COLS
    small = ([cols(fox + 4 * w, FOX_HEADS)] * N_SPLIT
             + [cols(ssm + w + SSM_CONV_DIM, SSM_HEADS)]
             + [cols(nsa + w + 6 * NSA_KV_W, 3 * NSA_HEADS)])
    assert N_SPLIT * FOX_HEADS == PS_DT_LANE and PS_DT_LANE + SSM_HEADS == PS_GATE_LANE
    used = PS_GATE_LANE + 3 * NSA_HEADS
    small.append(jnp.zeros((w_in_l.shape[0], PS_COLS - used), w_in_l.dtype))
    return w_main, jnp.concatenate(small, axis=1).astype(BF16)


def _pad_lanes(v, first=0):
    return jnp.pad(v.astype(F32), (first, LANES - first - v.shape[0])).reshape(1, LANES)


def _trunk(x, mem, norm_g, w_in, fox_f_bias, ssm_conv_w, ssm_conv_b, ssm_dt_bias, ssm_a_log, ssm_d,
           ssm_norm_g, nsa_cmp_pe, nsa_cmp_w1, nsa_cmp_w2, rel_bias_table, mem_norm_g, w_mem_kv, w_out,
           final_norm_g):
    b, s, d = x.shape
    depth = w_in.shape[0]
    n = b * s
    m_tok = mem.shape[1]
    n_cmp = (s - CMP_BLOCK) // CMP_STRIDE + 1
    n_rows = s // CMP_STRIDE
    assert s % 1024 == 0 and s // SEL_BLOCK <= HEAD_DIM and n_rows <= N_CMP_PAD and d == D_MODEL

    tab_flat = rel_bias_table.astype(F32).reshape(-1)
    unbounded = 1 << 30
    bwin = _t5_table(tab_flat, WINDOW + WIN_T, WIN_T, 1, WINDOW, WINDOW)
    bsel = _t5_table(tab_flat, (SEL_T // SEL_TK + 1) * SEL_TK, SEL_T, 1, SEL_TK, unbounded)
    fcmp = _t5_table(tab_flat, 2 * N_CMP_PAD, T_ATT, CMP_STRIDE, CMP_STRIDE * N_CMP_PAD - (CMP_BLOCK - 1), unbounded)

    cs = np.arange(N_CMP_PAD)[None, :] * CMP_STRIDE
    js = np.arange(SEL_BLOCK)[:, None] * SEL_BLOCK
    overlap_t = ((cs < js + SEL_BLOCK) & (cs + CMP_BLOCK > js) & (np.arange(N_CMP_PAD)[None, :] < n_cmp)
                 & (np.arange(SEL_BLOCK)[:, None] < s // SEL_BLOCK)).astype(np.float32)
    overlap_t = jnp.asarray(overlap_t, BF16)

    x2d = x.reshape(n, d)
    mem2d = mem.reshape(b * m_tok, d)
    for l in range(depth):
        w_main, w_small = _pack_in_proj(w_in[l])
        p, ps = _norm_proj(x2d, norm_g[l], (w_main, w_small), (BF16, F32))

        qaug, kaug = _fox_cumsum(ps, _pad_lanes(jnp.tile(fox_f_bias[l], N_SPLIT)), b, s)
        o_fox = _fox_attn(p, qaug, kaug, b, s)

        o_ssd = _ssd(p, ps, ssm_conv_w[l].astype(F32), ssm_conv_b[l].reshape(1, -1).astype(F32),
                     _pad_lanes(ssm_dt_bias[l], PS_DT_LANE), _pad_lanes(ssm_a_log[l], PS_DT_LANE),
                     jnp.repeat(ssm_d[l].astype(F32), HEAD_DIM).reshape(1, GROUP_W),
                     ssm_norm_g[l].reshape(1, GROUP_W).astype(F32), b, s)

        kv_cmp, kv_cmp_t = _compress(p, nsa_cmp_w1[l], nsa_cmp_w2[l], nsa_cmp_pe[l], b, s)
        o_cmp, notsel = _cmp_select(p, ps, kv_cmp, kv_cmp_t, fcmp, overlap_t, b, s)
        o_sel = _sel_attn(tab_flat, p, ps, notsel, bsel, b, s)
        o_win = _win_attn(p, ps, bwin, b, s)

        w_kv = w_mem_kv[l].astype(BF16)
        (mem_kv,) = _norm_proj(mem2d, mem_norm_g[l], (w_kv,), (BF16,), tm=min(512, b * m_tok))
        o_mem = _mem_attn(p, mem_kv, b, s)

        x2d = _out_proj(x2d, (o_fox, o_ssd, o_cmp, o_sel, o_win, o_mem), w_out[l].astype(BF16),
                        final_norm_g, final=(l == depth - 1))
    return x2d.reshape(b, s, d)


def kernel(x, mem, norm_g, w_in, fox_f_bias, ssm_conv_w, ssm_conv_b, ssm_dt_bias, ssm_a_log, ssm_d, ssm_norm_g,
           nsa_cmp_pe, nsa_cmp_w1, nsa_cmp_w2, rel_bias_table, mem_norm_g, w_mem_kv, w_out, final_norm_g):
    return _trunk(x, mem, norm_g, w_in, fox_f_bias, ssm_conv_w, ssm_conv_b, ssm_dt_bias, ssm_a_log, ssm_d,
                  ssm_norm_g, nsa_cmp_pe, nsa_cmp_w1, nsa_cmp_w2, rel_bias_table, mem_norm_g, w_mem_kv, w_out,
                  final_norm_g)
```

```python
import functools
import math

import numpy as np
import jax
import jax.numpy as jnp
from jax import lax
from jax.experimental import pallas as pl
from jax.experimental.pallas import tpu as pltpu

F32 = jnp.float32
BF16 = jnp.bfloat16

D_MODEL = 1024
GROUP_W = 512
HEAD_DIM = 64
EPS = 1e-6
NEG_INF = -1e30
TINY = 1e-30
LOG2E = math.log2(math.e)

FOX_HEADS = 8
SSM_HEADS = 8
SSM_STATE = 128
SSM_GROUPS = 2
SSM_CONV = 4
SSM_CHUNK = 128
SSM_CONV_DIM = GROUP_W + 2 * SSM_GROUPS * SSM_STATE

NSA_HEADS = 8
NSA_KV_HEADS = 2
NSA_REP = NSA_HEADS // NSA_KV_HEADS
NSA_KV_W = NSA_KV_HEADS * HEAD_DIM
CMP_BLOCK = 32
CMP_STRIDE = 16
CMP_HIDDEN = 2 * HEAD_DIM
SEL_BLOCK = 64
SEL_TOPK = 16
WINDOW = 512
SEL_FORCE = 1e9

MEM_HEADS = 4
MEM_HEAD_DIM = GROUP_W // MEM_HEADS
REL_BUCKETS = 32
REL_MAX_DIST = 128

FOX_COLS = 4 * GROUP_W + FOX_HEADS
SSM_COLS = GROUP_W + SSM_CONV_DIM + SSM_HEADS
NSA_COLS = 2 * GROUP_W + 6 * NSA_KV_W + 3 * NSA_HEADS
MEM_COLS = 2 * GROUP_W

LANES = 128
VMEM_LIMIT = 56 * 1024 * 1024

OFF_FOX_Q, OFF_FOX_K, OFF_FOX_V, OFF_FOX_G = 0, 512, 1024, 1536
OFF_SSM_XBC, OFF_SSM_Z = 2048, 3072
OFF_NSA_Q, OFF_NSA_G = 3584, 4096
OFF_MEM_Q, OFF_MEM_G = 4608, 5120
OFF_SEL_K, OFF_SEL_V, OFF_WIN_K, OFF_WIN_V, OFF_CMP_KV = 5632, 5760, 5888, 6016, 6144
P_COLS = 6400
PS_COLS = LANES
PS_DT_LANE = 24
PS_GATE_LANE = 32

T_ATT = 512
SEL_T, SEL_TK = 512, 256
WIN_T, WIN_TK = 512, 256
N_CMP_PAD = 256


def _params(sem):
    return pltpu.CompilerParams(dimension_semantics=sem, vmem_limit_bytes=VMEM_LIMIT)


def _t5_bucket_np(dist):
    n = np.maximum(dist, 0)
    max_exact = REL_BUCKETS // 2
    nf = np.maximum(n, 1).astype(np.float32)
    large = max_exact + (np.log(nf / np.float32(max_exact)) / np.float32(math.log(REL_MAX_DIST / max_exact))
                         * np.float32(REL_BUCKETS - max_exact)).astype(np.int32)
    large = np.minimum(large, REL_BUCKETS - 1)
    return np.where(n < max_exact, n, large).astype(np.int32)


def _silu(x):
    h = 0.5 * x
    return h + h * jnp.tanh(h)


def _norm_proj_kernel(x_ref, g_ref, *refs, chunk):
    x = x_ref[...]
    ms = jnp.mean(x * x, axis=-1, keepdims=True)
    h = ((x * lax.rsqrt(ms + EPS)) * g_ref[...]).astype(BF16)
    n_out = len(refs) // 2
    for w_ref, o_ref in zip(refs[:n_out], refs[n_out:]):
        ncol = o_ref.shape[1]
        for c0 in range(0, ncol, chunk):
            c1 = min(c0 + chunk, ncol)
            o_ref[:, c0:c1] = jnp.dot(h, w_ref[:, c0:c1], preferred_element_type=F32).astype(o_ref.dtype)


def _norm_proj(x2d, g, weights, out_dtypes, tm=512):
    n, d = x2d.shape
    return pl.pallas_call(
        functools.partial(_norm_proj_kernel, chunk=512),
        grid=(n // tm,),
        in_specs=[pl.BlockSpec((tm, d), lambda i: (i, 0)), pl.BlockSpec((1, d), lambda i: (0, 0))]
        + [pl.BlockSpec(w.shape, lambda i: (0, 0)) for w in weights],
        out_specs=[pl.BlockSpec((tm, w.shape[1]), lambda i: (i, 0)) for w in weights],
        out_shape=[jax.ShapeDtypeStruct((n, w.shape[1]), dt) for w, dt in zip(weights, out_dtypes)],
        compiler_params=_params(("parallel",)),
        name="norm_proj",
    )(x2d, g.reshape(1, d), *weights)


def _t5_table_kernel(tab_ref, bucket_ref, o_ref, *, stride, off, limit):
    h = pl.program_id(0)
    rows, cols = o_ref.shape[1], o_ref.shape[2]
    bucket = bucket_ref[...]
    near = jnp.zeros(bucket.shape, F32)
    for b in range(REL_BUCKETS):
        near = jnp.where(bucket == b, tab_ref[b * NSA_HEADS + h] * LOG2E, near)
    far = tab_ref[(REL_BUCKETS - 1) * NSA_HEADS + h] * LOG2E
    base = jnp.concatenate([near] * (rows // near.shape[0]), axis=0)
    rolled = pltpu.roll(base, 0, 1, stride=stride, stride_axis=0)
    d = (lax.broadcasted_iota(jnp.int32, (rows, cols), 1) - stride * lax.broadcasted_iota(jnp.int32, (rows, cols), 0)
         + off)
    o_ref[0] = jnp.where((d < 0) | (d >= limit), NEG_INF, jnp.where(d >= REL_MAX_DIST, far, rolled))


def _t5_table(tab_flat, rows, cols, stride, off, limit):
    assert cols >= 2 * REL_MAX_DIST and rows % 8 == 0
    k = (np.arange(cols) + off) % cols
    bucket = np.broadcast_to(np.where(k < REL_MAX_DIST, _t5_bucket_np(k), -1).astype(np.int32), (8, cols))
    return pl.pallas_call(
        functools.partial(_t5_table_kernel, stride=stride, off=off, limit=limit),
        grid=(NSA_HEADS,),
        in_specs=[
            pl.BlockSpec(memory_space=pltpu.SMEM),
            pl.BlockSpec((8, cols), lambda h: (0, 0)),
        ],
        out_specs=pl.BlockSpec((1, rows, cols), lambda h: (h, 0, 0)),
        out_shape=jax.ShapeDtypeStruct((NSA_HEADS, rows, cols), F32),
        compiler_params=_params(("arbitrary",)),
        name="t5_table",
    )(tab_flat, jnp.asarray(bucket))


def _tri_lower(n):
    r = lax.broadcasted_iota(jnp.int32, (n, n), 0)
    c = lax.broadcasted_iota(jnp.int32, (n, n), 1)
    return (r >= c).astype(F32)


N_SPLIT = 3


def _fox_aug_lane(h, i):
    return LANES * (h // 2) + (HEAD_DIM if h % 2 == 0 else 0) + i


def _fox_aug_consts():
    pq = np.zeros((LANES, GROUP_W), np.float32)
    pk = np.zeros((LANES, GROUP_W), np.float32)
    oq = np.zeros((1, GROUP_W), np.float32)
    ok = np.zeros((1, GROUP_W), np.float32)
    for h in range(FOX_HEADS):
        for i in range(N_SPLIT):
            pq[i * FOX_HEADS + h, _fox_aug_lane(h, i)] = 1.0
            pk[i * FOX_HEADS + h, _fox_aug_lane(h, N_SPLIT + i)] = -1.0
            oq[0, _fox_aug_lane(h, N_SPLIT + i)] = 1.0
            ok[0, _fox_aug_lane(h, i)] = 1.0
    return pq, pk, oq, ok


def _fox_cumsum_kernel(f_ref, b_ref, pq_ref, pk_ref, oq_ref, ok_ref, qa_ref, ka_ref, carry_ref, *, ts):
    @pl.when(pl.program_id(1) == 0)
    def _():
        carry_ref[...] = jnp.zeros_like(carry_ref)

    z = f_ref[...] + b_ref[...]
    logf = (jnp.minimum(z, 0.0) - jnp.log(1.0 + jnp.exp(-jnp.abs(z)))) * LOG2E
    tri = _tri_lower(LANES).astype(BF16)
    group = lax.broadcasted_iota(jnp.int32, (LANES, LANES), 1) // FOX_HEADS
    carry = carry_ref[...]
    for c in range(ts // LANES):
        rows = slice(c * LANES, (c + 1) * LANES)
        cs, rest = carry, logf[rows]
        for _ in range(N_SPLIT):
            piece = rest.astype(BF16)
            rest = rest - piece.astype(F32)
            cs = cs + jnp.dot(tri, piece, preferred_element_type=F32)
        carry = cs[LANES - 1:LANES, :]
        cat, rest = None, cs
        for i in range(N_SPLIT):
            piece = rest.astype(BF16).astype(F32)
            rest = rest - piece
            cat = piece if cat is None else jnp.where(group == i, piece, cat)
        cat = cat.astype(BF16)
        qa_ref[rows, :] = (oq_ref[...] + jnp.dot(cat, pq_ref[...], preferred_element_type=F32)).astype(BF16)
        ka_ref[rows, :] = (ok_ref[...] + jnp.dot(cat, pk_ref[...], preferred_element_type=F32)).astype(BF16)
    carry_ref[...] = carry


def _fox_cumsum(ps, f_bias_pad, b, s, ts=1024):
    ns = s // ts
    pq, pk, oq, ok = _fox_aug_consts()
    const = lambda bi, si: (0, 0)
    return pl.pallas_call(
        functools.partial(_fox_cumsum_kernel, ts=ts),
        grid=(b, ns),
        in_specs=[
            pl.BlockSpec((ts, LANES), lambda bi, si: (bi * ns + si, 0)),
            pl.BlockSpec((1, LANES), const),
            pl.BlockSpec(pq.shape, const),
            pl.BlockSpec(pk.shape, const),
            pl.BlockSpec(oq.shape, const),
            pl.BlockSpec(ok.shape, const),
        ],
        out_specs=[
            pl.BlockSpec((ts, GROUP_W), lambda bi, si: (bi * ns + si, 0)),
            pl.BlockSpec((ts, GROUP_W), lambda bi, si: (bi * ns + si, 0)),
        ],
        out_shape=[jax.ShapeDtypeStruct((b * s, GROUP_W), BF16), jax.ShapeDtypeStruct((b * s, GROUP_W), BF16)],
        scratch_shapes=[pltpu.VMEM((1, LANES), F32)],
        compiler_params=_params(("parallel", "arbitrary")),
        name="fox_cumsum",
    )(ps, f_bias_pad, jnp.asarray(pq, BF16), jnp.asarray(pk, BF16), jnp.asarray(oq), jnp.asarray(ok))


ONES_ROWS = 16


def _vt_rows(h):
    return slice(0, HEAD_DIM + ONES_ROWS) if h % 2 == 0 else slice(HEAD_DIM - ONES_ROWS, LANES)


def _store_vt(vt_ref, i, rows, v_pair):
    vt = v_pair.astype(F32).T.astype(BF16)
    ones = jnp.ones((HEAD_DIM, v_pair.shape[0]), BF16)
    vt_ref[i, 0:HEAD_DIM, rows] = vt[0:HEAD_DIM]
    vt_ref[i, HEAD_DIM:LANES, rows] = ones
    vt_ref[i + 1, 0:HEAD_DIM, rows] = ones
    vt_ref[i + 1, HEAD_DIM:LANES, rows] = vt[HEAD_DIM:LANES]


def _store_vt_groups(vt_ref, rows, v_groups):
    vt = v_groups.astype(F32).T.astype(BF16)
    ones = jnp.ones((HEAD_DIM, v_groups.shape[0]), BF16)
    for g in range(NSA_KV_HEADS):
        vg = vt[g * HEAD_DIM:(g + 1) * HEAD_DIM]
        vt_ref[2 * g, 0:HEAD_DIM, rows] = vg
        vt_ref[2 * g, HEAD_DIM:LANES, rows] = ones
        vt_ref[2 * g + 1, 0:HEAD_DIM, rows] = ones
        vt_ref[2 * g + 1, HEAD_DIM:LANES, rows] = vg


def _finish_pair(acc_ref, hp):
    ae, ao = acc_ref[2 * hp], acc_ref[2 * hp + 1]
    top = lax.broadcasted_iota(jnp.int32, ae.shape, 0) < HEAD_DIM
    both = jnp.where(top, ae / jnp.maximum(ae[HEAD_DIM:HEAD_DIM + 1, :], TINY),
                     ao / jnp.maximum(ao[HEAD_DIM - 1:HEAD_DIM, :], TINY))
    return both.T


def _swap_halves(x_b):
    return pltpu.roll(x_b.astype(F32), HEAD_DIM, 1).astype(BF16)


def _half_masks(rows):
    lo = jnp.where(lax.broadcasted_iota(jnp.int32, (rows, LANES), 1) < HEAD_DIM, 1.0, 0.0)
    return lo.astype(BF16), (1.0 - lo).astype(BF16)


def _fox_attn_kernel(q_ref, qa_ref, k_ref, ka_ref, v_ref, gate_ref, o_ref, qs_ref, kk_ref, vt_ref, m_ref, acc_ref,
                     pre_ref, *, t, tk, ahead):
    qi = pl.program_id(1)
    lo_q, hi_q = _half_masks(t)
    cm = lax.broadcasted_iota(jnp.int32, (tk, t), 0) - lax.broadcasted_iota(jnp.int32, (tk, t), 1)

    @pl.when(qi == 0)
    def _():
        lo_k, hi_k = _half_masks(tk)

        def merge(j, c):
            rows = pl.ds(pl.multiple_of(j * tk, tk), tk)
            for hp in range(FOX_HEADS // 2):
                ls = slice(hp * LANES, (hp + 1) * LANES)
                kp, ka, vp = k_ref[rows, ls], ka_ref[rows, ls], v_ref[rows, ls]
                kk_ref[2 * hp, rows, :] = kp * lo_k + ka * hi_k
                kk_ref[2 * hp + 1, rows, :] = ka * lo_k + kp * hi_k
                _store_vt(vt_ref, 2 * hp, rows, vp)
            return c

        lax.fori_loop(0, k_ref.shape[0] // tk, merge, 0)

    for hp in range(FOX_HEADS // 2):
        ls = slice(hp * LANES, (hp + 1) * LANES)
        qp, qa = q_ref[:, ls], qa_ref[:, ls]
        qs_ref[2 * hp] = (qp * lo_q + qa * hi_q).astype(F32).T.astype(BF16)
        qs_ref[2 * hp + 1] = (qa * lo_q + qp * hi_q).astype(F32).T.astype(BF16)
    m_ref[...] = jnp.full(m_ref.shape, NEG_INF, F32)
    acc_ref[...] = jnp.zeros_like(acc_ref)

    def rows_of(j):
        return pl.ds(pl.multiple_of(j * tk, tk), tk)

    def scores(j, h):
        return jnp.dot(kk_ref[h, rows_of(j), :], qs_ref[h], preferred_element_type=F32)

    def tile(j, diag, j_next):
        queue = [pre_ref[i] for i in range(ahead)]
        for h in range(FOX_HEADS):
            if h + ahead < FOX_HEADS:
                queue.append(scores(j, h + ahead))
            elif j_next is not None:
                pre_ref[h + ahead - FOX_HEADS] = scores(j_next, h + ahead - FOX_HEADS)
            s = queue.pop(0)
            if diag:
                s = jnp.where(cm <= qi * t - j * tk, s, NEG_INF)
            m_old = m_ref[h]
            m_new = jnp.maximum(m_old, jnp.max(s, axis=0, keepdims=True))
            m_ref[h] = m_new
            p = jnp.exp2(s - m_new).astype(BF16)
            vr = _vt_rows(h)
            acc_ref[h, vr] = jnp.exp2(m_old - m_new) * acc_ref[h, vr] + jnp.dot(vt_ref[h, vr, rows_of(j)], p,
                                                                                preferred_element_type=F32)

    jd = (qi * t) // tk
    n_diag = max(t // tk, 1)
    for i in range(ahead):
        pre_ref[i] = scores(0, i)

    def body(j, c):
        tile(j, False, j + 1)
        return c

    lax.fori_loop(0, jd, body, 0)
    for dj in range(n_diag):
        tile(jd + dj, True, jd + dj + 1 if dj + 1 < n_diag else None)

    for hp in range(FOX_HEADS // 2):
        ls = slice(hp * LANES, (hp + 1) * LANES)
        o_ref[:, ls] = (_finish_pair(acc_ref, hp) * _silu(gate_ref[:, ls].astype(F32))).astype(o_ref.dtype)


def _fox_attn(p, qaug, kaug, b, s, t=512, tk=256, ahead=2):
    nq = s // t
    w = GROUP_W
    return pl.pallas_call(
        functools.partial(_fox_attn_kernel, t=t, tk=tk, ahead=ahead),
        grid=(b, nq),
        in_specs=[
            pl.BlockSpec((t, w), lambda bi, qi: (bi * nq + qi, OFF_FOX_Q // w)),
            pl.BlockSpec((t, w), lambda bi, qi: (bi * nq + qi, 0)),
            pl.BlockSpec((s, w), lambda bi, qi: (bi, OFF_FOX_K // w)),
            pl.BlockSpec((s, w), lambda bi, qi: (bi, 0)),
            pl.BlockSpec((s, w), lambda bi, qi: (bi, OFF_FOX_V // w)),
            pl.BlockSpec((t, w), lambda bi, qi: (bi * nq + qi, OFF_FOX_G // w)),
        ],
        out_specs=pl.BlockSpec((t, w), lambda bi, qi: (bi * nq + qi, 0)),
        out_shape=jax.ShapeDtypeStruct((b * s, w), BF16),
        scratch_shapes=[
            pltpu.VMEM((FOX_HEADS, LANES, t), BF16),
            pltpu.VMEM((FOX_HEADS, s, LANES), BF16),
            pltpu.VMEM((FOX_HEADS, LANES, s), BF16),
            pltpu.VMEM((FOX_HEADS, 1, t), F32),
            pltpu.VMEM((FOX_HEADS, LANES, t), F32),
            pltpu.VMEM((ahead, tk, t), F32),
        ],
        compiler_params=_params(("parallel", "arbitrary")),
        name="fox_attn",
    )(p, qaug, p, kaug, p, p)


def _dot_split(lhs_f32, rhs_b):
    out, rest = None, lhs_f32
    for _ in range(N_SPLIT):
        piece = rest.astype(BF16)
        rest = rest - piece.astype(F32)
        term = jnp.dot(piece, rhs_b, preferred_element_type=F32)
        out = term if out is None else out + term
    return out


def _ssd_kernel(z_ref, xbc_ref, dt_ref, cw_ref, cb_ref, dtb_ref, alog_ref, dsk_ref, ng_ref, o_ref,
                xpad_ref, xc_ref, state_ref, y_ref, *, nch):
    q = SSM_CHUNK
    rows_all = nch * q
    halo = 8

    @pl.when(pl.program_id(1) == 0)
    def _():
        xpad_ref[0:halo, :] = jnp.zeros((halo, SSM_CONV_DIM), F32)
        state_ref[...] = jnp.zeros_like(state_ref)

    xpad_ref[halo:halo + rows_all, :] = xbc_ref[...].astype(F32)
    y = cb_ref[...]
    for k in range(SSM_CONV):
        off = halo - (SSM_CONV - 1) + k
        y = y + cw_ref[k:k + 1, :] * xpad_ref[off:off + rows_all, :]
    xpad_ref[0:halo, :] = xpad_ref[rows_all:rows_all + halo, :]
    xc_ref[...] = _silu(y)

    x_dt = dt_ref[...] + dtb_ref[...]
    dt_all = jnp.maximum(x_dt, 0.0) + jnp.log(1.0 + jnp.exp(-jnp.abs(x_dt)))
    a_all = dt_all * (-jnp.exp(alog_ref[...]))
    tri_t = (lax.broadcasted_iota(jnp.int32, (q, q), 0) <= lax.broadcasted_iota(jnp.int32, (q, q), 1)).astype(BF16)
    er = lax.broadcasted_iota(jnp.int32, (LANES, GROUP_W), 0)
    ec = lax.broadcasted_iota(jnp.int32, (LANES, GROUP_W), 1)
    expand = jnp.where(ec // HEAD_DIM == er - PS_DT_LANE, 1.0, 0.0).astype(BF16)
    row = lax.broadcasted_iota(jnp.int32, (q, q), 0)
    col = lax.broadcasted_iota(jnp.int32, (q, q), 1)
    causal = row >= col
    lo = lax.broadcasted_iota(jnp.int32, (q, LANES), 1) < HEAD_DIM
    gw = GROUP_W // SSM_GROUPS
    hpg = SSM_HEADS // SSM_GROUPS

    for c in range(nch):
        rs = slice(c * q, (c + 1) * q)
        xs = xc_ref[rs, :GROUP_W]
        dt = dt_all[rs]
        acs_t = _dot_split(a_all[rs].T, tri_t)
        acs = acs_t.T
        stacked = jnp.concatenate([dt, jnp.exp(acs), jnp.exp(acs[q - 1:q, :] - acs)], axis=0)
        full = _dot_split(stacked, expand)
        dt_full, eacs_full, dec_full = full[0:q], full[q:2 * q], full[2 * q:3 * q]
        xdt = xs * dt_full
        xdt_b = xdt.astype(BF16)
        xdec_b = (xdt * dec_full).astype(BF16)

        for g in range(SSM_GROUPS):
            bm = xc_ref[rs, GROUP_W + g * SSM_STATE:GROUP_W + (g + 1) * SSM_STATE]
            cm = xc_ref[rs, GROUP_W + (SSM_GROUPS + g) * SSM_STATE:GROUP_W + (SSM_GROUPS + g + 1) * SSM_STATE]
            bm_b = bm.astype(BF16)
            cm_b = cm.astype(BF16)
            gs = slice(g * gw, (g + 1) * gw)
            cbg = lax.dot_general(cm_b, bm_b, (((1,), (1,)), ((), ())), preferred_element_type=F32)
            st = state_ref[:, gs]
            y_off = jnp.dot(cm_b, st.astype(BF16), preferred_element_type=F32) * eacs_full[:, gs]
            cst = jnp.dot(bm.T.astype(BF16), xdec_b[:, gs], preferred_element_type=F32)
            state_ref[:, gs] = st * eacs_full[q - 1:q, gs] + cst
            for hp in range(hpg // 2):
                ls = slice(g * gw + hp * LANES, g * gw + (hp + 1) * LANES)
                yd = []
                for e in range(2):
                    h = g * hpg + 2 * hp + e
                    hl = PS_DT_LANE + h
                    seg = jnp.exp(jnp.where(causal, acs[:, hl:hl + 1] - acs_t[hl:hl + 1, :], NEG_INF))
                    yd.append(jnp.dot((cbg * seg).astype(BF16), xdt_b[:, ls], preferred_element_type=F32))
                y_ref[:, ls] = jnp.where(lo, yd[0], yd[1]) + y_off[:, hp * LANES:(hp + 1) * LANES]

        yz = (y_ref[...] + xs * dsk_ref[...]) * _silu(z_ref[rs, :].astype(F32))
        for g in range(SSM_GROUPS):
            gs = slice(g * gw, (g + 1) * gw)
            blk = yz[:, gs]
            ms = jnp.mean(blk * blk, axis=-1, keepdims=True)
            o_ref[rs, gs] = ((blk * lax.rsqrt(ms + EPS)) * ng_ref[:, gs]).astype(o_ref.dtype)


def _ssd(p, ps, conv_w, conv_b, dt_bias_pad, a_log_pad, d_full, norm_g, b, s, nch=16):
    q = SSM_CHUNK
    rows = nch * q
    nc = s // rows
    row = lambda bi, ci: (bi * nc + ci)
    const = lambda bi, ci: (0, 0)
    return pl.pallas_call(
        functools.partial(_ssd_kernel, nch=nch),
        grid=(b, nc),
        in_specs=[
            pl.BlockSpec((rows, GROUP_W), lambda bi, ci: (row(bi, ci), OFF_SSM_Z // GROUP_W)),
            pl.BlockSpec((rows, SSM_CONV_DIM), lambda bi, ci: (row(bi, ci), OFF_SSM_XBC // SSM_CONV_DIM)),
            pl.BlockSpec((rows, LANES), lambda bi, ci: (row(bi, ci), 0)),
            pl.BlockSpec((SSM_CONV, SSM_CONV_DIM), const),
            pl.BlockSpec((1, SSM_CONV_DIM), const),
            pl.BlockSpec((1, LANES), const),
            pl.BlockSpec((1, LANES), const),
            pl.BlockSpec((1, GROUP_W), const),
            pl.BlockSpec((1, GROUP_W), const),
        ],
        out_specs=pl.BlockSpec((rows, GROUP_W), lambda bi, ci: (row(bi, ci), 0)),
        out_shape=jax.ShapeDtypeStruct((b * s, GROUP_W), BF16),
        scratch_shapes=[
            pltpu.VMEM((rows + 8, SSM_CONV_DIM), F32),
            pltpu.VMEM((rows, SSM_CONV_DIM), F32),
            pltpu.VMEM((SSM_STATE, GROUP_W), F32),
            pltpu.VMEM((q, GROUP_W), F32),
        ],
        compiler_params=_params(("parallel", "arbitrary")),
        name="ssd",
    )(p, p, ps, conv_w, conv_b, dt_bias_pad, a_log_pad, d_full, norm_g)


def _compress_kernel(x_ref, w1_ref, w2_ref, pe_ref, o_ref, ot_ref, xf_ref):
    s = x_ref.shape[0]
    n = s // CMP_STRIDE
    for sb in range(2):
        xf_ref[sb] = x_ref[:, sb * LANES:(sb + 1) * LANES].astype(F32)
    o_ref[...] = jnp.zeros_like(o_ref)
    ot_ref[...] = jnp.zeros_like(ot_ref)
    for sb in range(2):
        first = jnp.zeros((n, 2 * CMP_HIDDEN), F32)
        second = jnp.zeros((n, 2 * CMP_HIDDEN), F32)
        for l in range(CMP_STRIDE):
            xl = xf_ref[sb, pl.ds(l, n, stride=CMP_STRIDE), :]
            first = first + jnp.dot((xl + pe_ref[sb, l:l + 1, :]).astype(BF16), w1_ref[sb, l],
                                    preferred_element_type=F32)
            second = second + jnp.dot((xl + pe_ref[sb, CMP_STRIDE + l:CMP_STRIDE + l + 1, :]).astype(BF16),
                                      w1_ref[sb, CMP_STRIDE + l], preferred_element_type=F32)
        h = _silu(first + pltpu.roll(second, n - 1, 0))
        o = jnp.dot(h.astype(BF16), w2_ref[sb], preferred_element_type=F32)
        for e in range(2):
            oe = o[:, e * LANES:(e + 1) * LANES]
            o_ref[0, 2 * sb + e, 0:n, :] = oe.astype(o_ref.dtype)
            ot_ref[0, 2 * sb + e, :, 0:n] = oe.T.astype(ot_ref.dtype)


def _blockdiag2(a):
    z = jnp.zeros_like(a)
    return jnp.concatenate([jnp.concatenate([a, z], axis=-1), jnp.concatenate([z, a], axis=-1)], axis=-2)


def _compress(p, cmp_w1, cmp_w2, cmp_pe, b, s):
    nslot = 2 * NSA_KV_HEADS
    w1 = _blockdiag2(cmp_w1.astype(BF16).reshape(2, CMP_BLOCK, HEAD_DIM, CMP_HIDDEN))
    w2 = _blockdiag2(jnp.concatenate([cmp_w2, cmp_w2], axis=-1).astype(BF16))
    pe = jnp.concatenate([cmp_pe, cmp_pe], axis=-1).astype(F32)
    kw = 2 * NSA_KV_W
    return pl.pallas_call(
        _compress_kernel,
        grid=(b,),
        in_specs=[
            pl.BlockSpec((s, kw), lambda bi: (bi, OFF_CMP_KV // kw)),
            pl.BlockSpec(w1.shape, lambda bi: (0, 0, 0, 0)),
            pl.BlockSpec(w2.shape, lambda bi: (0, 0, 0)),
            pl.BlockSpec(pe.shape, lambda bi: (0, 0, 0)),
        ],
        out_specs=[pl.BlockSpec((1, nslot, N_CMP_PAD, LANES), lambda bi: (bi, 0, 0, 0)),
                   pl.BlockSpec((1, nslot, LANES, N_CMP_PAD), lambda bi: (bi, 0, 0, 0))],
        out_shape=[jax.ShapeDtypeStruct((b, nslot, N_CMP_PAD, LANES), BF16),
                   jax.ShapeDtypeStruct((b, nslot, LANES, N_CMP_PAD), BF16)],
        scratch_shapes=[pltpu.VMEM((2, s, LANES), F32)],
        compiler_params=_params(("parallel",)),
        name="nsa_compress",
    )(p, w1, w2, pe)


def _cmp_select_kernel(q_ref, kv_ref, vt_ref, gl_ref, gate_ref, fc_ref, ovt_ref, o_ref, ns_ref):
    t = T_ATT
    ncp = kv_ref.shape[2]
    qi = pl.program_id(1)
    t0 = qi * t
    lo = lax.broadcasted_iota(jnp.int32, (t, LANES), 1) < HEAD_DIM
    zero = jnp.zeros((t, LANES), BF16)
    start = pl.multiple_of(ncp - qi * (t // CMP_STRIDE), CMP_STRIDE)
    gl = gl_ref[...]

    nsel = SEL_BLOCK
    jrow = lax.broadcasted_iota(jnp.int32, (nsel, t), 0)
    cur = (t0 + lax.broadcasted_iota(jnp.int32, (nsel, t), 1)) // SEL_BLOCK
    forced = (jrow == 0) | (jrow == cur) | (jrow == cur - 1)
    past = jrow <= cur

    for g in range(NSA_KV_HEADS):
        kc = kv_ref[0, g]
        vct = vt_ref[0, NSA_KV_HEADS + g]
        psum = jnp.zeros((ncp, t), F32)
        outs = []

        def scores(r):
            h = g * NSA_REP + r
            qp = q_ref[:, (h // 2) * LANES:(h // 2 + 1) * LANES]
            qe = jnp.where(lo, qp, zero) if r % 2 == 0 else jnp.where(lo, zero, qp)
            return lax.dot_general(kc, qe, (((1,), (1,)), ((), ())), preferred_element_type=F32)

        queue = [scores(0), scores(1)]
        for r in range(NSA_REP):
            h = g * NSA_REP + r
            if r + 2 < NSA_REP:
                queue.append(scores(r + 2))
            s = fc_ref[h, pl.ds(start, ncp), :] + queue.pop(0)
            m = jnp.maximum(jnp.max(s, axis=0, keepdims=True), 0.1 * NEG_INF)
            e = jnp.exp2(s - m)
            pr = e * (1.0 / jnp.maximum(jnp.sum(e, axis=0, keepdims=True), TINY))
            psum = psum + pr
            oc = jnp.dot(vct, pr.astype(BF16), preferred_element_type=F32)
            outs.append(oc.T * jax.nn.sigmoid(gl[:, PS_GATE_LANE + h:PS_GATE_LANE + h + 1]))
        for hp in range(NSA_REP // 2):
            ls = slice((g * NSA_REP // 2 + hp) * LANES, (g * NSA_REP // 2 + hp + 1) * LANES)
            o = jnp.where(lo, outs[2 * hp], outs[2 * hp + 1])
            o_ref[:, ls] = (o * _silu(gate_ref[:, ls].astype(F32))).astype(o_ref.dtype)

        imp_t = jnp.zeros((nsel, t), F32)
        rest = psum
        for _ in range(N_SPLIT):
            piece = rest.astype(BF16)
            rest = rest - piece.astype(F32)
            imp_t = imp_t + jnp.dot(ovt_ref[...], piece, preferred_element_type=F32)
        imp_t = jnp.where(past, jnp.where(forced, SEL_FORCE, imp_t), -SEL_FORCE)
        sub = 8
        sub_row = lax.broadcasted_iota(jnp.int32, (sub, t), 0)
        rows = [imp_t[k * sub:(k + 1) * sub] for k in range(nsel // sub)]
        rank = [jnp.zeros((sub, t), F32) for _ in rows]
        for i in range(nsel):
            bi = imp_t[i:i + 1, :]
            for k, x in enumerate(rows):
                if k * sub > i:
                    ahead = jnp.where(bi >= x, 1.0, 0.0)
                elif k * sub + sub - 1 <= i:
                    ahead = jnp.where(bi > x, 1.0, 0.0)
                else:
                    ahead = jnp.where(sub_row > i - k * sub, jnp.where(bi >= x, 1.0, 0.0),
                                      jnp.where(bi > x, 1.0, 0.0))
                rank[k] = rank[k] + ahead
        rank = jnp.concatenate(rank, axis=0)
        notsel = jnp.where((rank < float(SEL_TOPK)) & past, 0.0, 1.0)
        ns2 = jnp.concatenate([notsel, notsel], axis=0).T
        ns_ref[:, g * LANES:(g + 1) * LANES] = ns2.astype(ns_ref.dtype)


def _cmp_select(p, ps, kvc, kvc_t, fc, overlap_t, b, s):
    t = T_ATT
    nq = s // t
    w = GROUP_W
    ncp = kvc.shape[2]
    return pl.pallas_call(
        _cmp_select_kernel,
        grid=(b, nq),
        in_specs=[
            pl.BlockSpec((t, w), lambda bi, qi: (bi * nq + qi, OFF_NSA_Q // w)),
            pl.BlockSpec((1, 2 * NSA_KV_HEADS, ncp, LANES), lambda bi, qi: (bi, 0, 0, 0)),
            pl.BlockSpec((1, 2 * NSA_KV_HEADS, LANES, ncp), lambda bi, qi: (bi, 0, 0, 0)),
            pl.BlockSpec((t, LANES), lambda bi, qi: (bi * nq + qi, 0)),
            pl.BlockSpec((t, w), lambda bi, qi: (bi * nq + qi, OFF_NSA_G // w)),
            pl.BlockSpec((NSA_HEADS, 2 * ncp, t), lambda bi, qi: (0, 0, 0)),
            pl.BlockSpec((SEL_BLOCK, ncp), lambda bi, qi: (0, 0)),
        ],
        out_specs=[
            pl.BlockSpec((t, w), lambda bi, qi: (bi * nq + qi, 0)),
            pl.BlockSpec((t, NSA_KV_HEADS * LANES), lambda bi, qi: (bi * nq + qi, 0)),
        ],
        out_shape=[jax.ShapeDtypeStruct((b * s, w), BF16),
                   jax.ShapeDtypeStruct((b * s, NSA_KV_HEADS * LANES), BF16)],
        compiler_params=_params(("parallel", "arbitrary")),
        name="nsa_cmp_select",
    )(p, kvc, kvc_t, ps, p, fc, overlap_t)


def _sel_attn_kernel(tab_ref, q_ref, ns_ref, k_ref, v_ref, gl_ref, gate_ref, bs_ref, o_ref,
                     qa_ref, kk_ref, vt_ref, m_ref, acc_ref, pre_ref, *, t, tk, ahead):
    qi = pl.program_id(1)
    lo_q, hi_q = _half_masks(t)
    lo = lax.broadcasted_iota(jnp.int32, (t, LANES), 1) < HEAD_DIM
    n_near = t // tk + 1

    @pl.when(qi == 0)
    def _():
        lane = lax.broadcasted_iota(jnp.int32, (tk, LANES), 1)
        krow = lax.broadcasted_iota(jnp.int32, (tk, LANES), 0)
        lo_f = jnp.where(lane < HEAD_DIM, 1.0, 0.0)
        hi_f = 1.0 - lo_f
        lo_k, hi_k = lo_f.astype(BF16), hi_f.astype(BF16)

        def merge(j, c):
            ks = pl.multiple_of(j * tk, tk)
            rows = pl.ds(ks, tk)
            hot = jnp.where((lane % HEAD_DIM) == (ks + krow) // SEL_BLOCK, NEG_INF, 0.0)
            hot_lo, hot_hi = (hot * lo_f).astype(BF16), (hot * hi_f).astype(BF16)
            k01 = k_ref[rows, :]
            k10 = _swap_halves(k01)
            kk_ref[0, rows, :] = k01 * lo_k + hot_hi
            kk_ref[1, rows, :] = hot_lo + k10 * hi_k
            kk_ref[2, rows, :] = k10 * lo_k + hot_hi
            kk_ref[3, rows, :] = hot_lo + k01 * hi_k
            _store_vt_groups(vt_ref, rows, v_ref[rows, :])
            return c

        lax.fori_loop(0, k_ref.shape[0] // tk, merge, 0)

    for h in range(NSA_HEADS):
        g = h // NSA_REP
        qp = q_ref[:, (h // 2) * LANES:(h // 2 + 1) * LANES]
        ns = ns_ref[:, g * LANES:(g + 1) * LANES]
        qa = qp * lo_q + ns * hi_q if h % 2 == 0 else ns * lo_q + qp * hi_q
        qa_ref[h] = qa.astype(F32).T.astype(BF16)
    m_ref[...] = jnp.full(m_ref.shape, NEG_INF, F32)
    acc_ref[...] = jnp.zeros_like(acc_ref)

    kv = lambda h: 2 * (h // NSA_REP) + h % 2

    def rows_of(j):
        return pl.ds(pl.multiple_of(j * tk, tk), tk)

    def scores(j, h):
        return jnp.dot(kk_ref[kv(h), rows_of(j), :], qa_ref[h], preferred_element_type=F32)

    def tile(j, near, j_next):
        rows = rows_of(j)
        queue = [pre_ref[i] for i in range(ahead)]
        for h in range(NSA_HEADS):
            if h + ahead < NSA_HEADS:
                queue.append(scores(j, h + ahead))
            elif j_next is not None:
                pre_ref[h + ahead - NSA_HEADS] = scores(j_next, h + ahead - NSA_HEADS)
            s = queue.pop(0)
            m_old = m_ref[h]
            if near is None:
                far = tab_ref[(REL_BUCKETS - 1) * NSA_HEADS + h] * LOG2E
                m_new = jnp.maximum(m_old, jnp.max(s, axis=0, keepdims=True) + far)
                p = jnp.exp2(s - (m_new - far)).astype(BF16)
            else:
                s = bs_ref[h, near * tk:(near + 1) * tk, :] + s
                m_new = jnp.maximum(m_old, jnp.max(s, axis=0, keepdims=True))
                p = jnp.exp2(s - m_new).astype(BF16)
            m_ref[h] = m_new
            vr = _vt_rows(h)
            acc_ref[h, vr] = jnp.exp2(m_old - m_new) * acc_ref[h, vr] + jnp.dot(vt_ref[kv(h), vr, rows], p,
                                                                                preferred_element_type=F32)

    jd = (qi * t) // tk
    for i in range(ahead):
        pre_ref[i] = scores(0, i)

    def body(j, c):
        tile(j, None, j + 1)
        return c

    lax.fori_loop(0, jnp.maximum(jd - 1, 0), body, 0)

    @pl.when(qi > 0)
    def _():
        tile(jd - 1, 0, jd)

    for dj in range(n_near - 1):
        tile(jd + dj, 1 + dj, jd + dj + 1 if dj + 2 < n_near else None)

    gl = gl_ref[...]
    for hp in range(NSA_HEADS // 2):
        ls = slice(hp * LANES, (hp + 1) * LANES)
        c0 = PS_GATE_LANE + NSA_HEADS + 2 * hp
        branch = jnp.where(lo, jax.nn.sigmoid(gl[:, c0:c0 + 1]), jax.nn.sigmoid(gl[:, c0 + 1:c0 + 2]))
        o_ref[:, ls] = (_finish_pair(acc_ref, hp) * branch * _silu(gate_ref[:, ls].astype(F32))).astype(o_ref.dtype)


def _sel_attn(tab_flat, p, ps, notsel, bsel, b, s, t=512, tk=256, ahead=2):
    nq = s // t
    w = GROUP_W
    kw = NSA_KV_HEADS * LANES
    n_near = t // tk + 1
    return pl.pallas_call(
        functools.partial(_sel_attn_kernel, t=t, tk=tk, ahead=ahead),
        grid=(b, nq),
        in_specs=[
            pl.BlockSpec(memory_space=pltpu.SMEM),
            pl.BlockSpec((t, w), lambda bi, qi: (bi * nq + qi, OFF_NSA_Q // w)),
            pl.BlockSpec((t, kw), lambda bi, qi: (bi * nq + qi, 0)),
            pl.BlockSpec((s, NSA_KV_W), lambda bi, qi: (bi, OFF_SEL_K // NSA_KV_W)),
            pl.BlockSpec((s, NSA_KV_W), lambda bi, qi: (bi, OFF_SEL_V // NSA_KV_W)),
            pl.BlockSpec((t, LANES), lambda bi, qi: (bi * nq + qi, 0)),
            pl.BlockSpec((t, w), lambda bi, qi: (bi * nq + qi, OFF_NSA_G // w)),
            pl.BlockSpec((NSA_HEADS, n_near * tk, t), lambda bi, qi: (0, 0, 0), pipeline_mode=pl.Buffered(1)),
        ],
        out_specs=pl.BlockSpec((t, w), lambda bi, qi: (bi * nq + qi, 0)),
        out_shape=jax.ShapeDtypeStruct((b * s, w), BF16),
        scratch_shapes=[
            pltpu.VMEM((NSA_HEADS, LANES, t), BF16),
            pltpu.VMEM((2 * NSA_KV_HEADS, s, LANES), BF16),
            pltpu.VMEM((2 * NSA_KV_HEADS, LANES, s), BF16),
            pltpu.VMEM((NSA_HEADS, 1, t), F32),
            pltpu.VMEM((NSA_HEADS, LANES, t), F32),
            pltpu.VMEM((ahead, tk, t), F32),
        ],
        compiler_params=_params(("parallel", "arbitrary")),
        name="nsa_sel_attn",
    )(tab_flat, p, notsel, p, p, ps, p, bsel)


def _win_attn_kernel(q_ref, k_ref, v_ref, gl_ref, gate_ref, bw_ref, o_ref, qe_ref, kk_ref, vt_ref, m_ref, acc_ref,
                     pre_ref, *, t, tk, ahead):
    qi = pl.program_id(1)
    lo_q, hi_q = _half_masks(t)
    lo = lax.broadcasted_iota(jnp.int32, (t, LANES), 1) < HEAD_DIM
    n_before = WINDOW // tk
    n_tiles = n_before + t // tk

    @pl.when(qi == 0)
    def _():
        lo_k, hi_k = _half_masks(tk)

        def merge(j, c):
            rows = pl.ds(pl.multiple_of(j * tk, tk), tk)
            k01 = k_ref[rows, :]
            k10 = _swap_halves(k01)
            kk_ref[0, rows, :] = k01 * lo_k + k10 * hi_k
            kk_ref[1, rows, :] = k10 * lo_k + k01 * hi_k
            _store_vt_groups(vt_ref, rows, v_ref[rows, :])
            return c

        lax.fori_loop(0, v_ref.shape[0] // tk, merge, 0)

    for h in range(NSA_HEADS):
        qp = q_ref[:, (h // 2) * LANES:(h // 2 + 1) * LANES]
        qe_ref[h] = qp * lo_q if h % 2 == 0 else qp * hi_q
    m_ref[...] = jnp.full(m_ref.shape, NEG_INF, F32)
    acc_ref[...] = jnp.zeros_like(acc_ref)

    def rows_of(w):
        return pl.ds(pl.multiple_of((qi * (t // tk) - n_before + w) * tk, tk), tk)

    def scores(w, h):
        return lax.dot_general(kk_ref[h // NSA_REP, rows_of(w), :], qe_ref[h], (((1,), (1,)), ((), ())),
                               preferred_element_type=F32)

    def tile(w):
        rows = rows_of(w)
        queue = [pre_ref[i] for i in range(ahead)]
        for h in range(NSA_HEADS):
            if h + ahead < NSA_HEADS:
                queue.append(scores(w, h + ahead))
            elif w + 1 < n_tiles:
                pre_ref[h + ahead - NSA_HEADS] = scores(w + 1, h + ahead - NSA_HEADS)
            s = bw_ref[h, w * tk:(w + 1) * tk, :] + queue.pop(0)
            m_old = m_ref[h]
            m_new = jnp.maximum(m_old, jnp.max(s, axis=0, keepdims=True))
            m_ref[h] = m_new
            p = jnp.exp2(s - m_new).astype(BF16)
            vr = _vt_rows(h)
            acc_ref[h, vr] = jnp.exp2(m_old - m_new) * acc_ref[h, vr] + jnp.dot(
                vt_ref[2 * (h // NSA_REP) + h % 2, vr, rows], p, preferred_element_type=F32)

    first = jnp.maximum(n_before - qi * (t // tk), 0)
    for i in range(ahead):
        pre_ref[i] = scores(first, i)
    per_q = t // tk
    first_qi = [-(-(n_before - w) // per_q) if w < n_before else 0 for w in range(n_tiles)]
    for need in sorted(set(first_qi), reverse=True):
        group = [w for w in range(n_tiles) if first_qi[w] == need]

        def run(group=group):
            for w in group:
                tile(w)

        if need > 0:
            pl.when(qi >= need)(run)
        else:
            run()

    gl = gl_ref[...]
    for hp in range(NSA_HEADS // 2):
        ls = slice(hp * LANES, (hp + 1) * LANES)
        c0 = PS_GATE_LANE + 2 * NSA_HEADS + 2 * hp
        branch = jnp.where(lo, jax.nn.sigmoid(gl[:, c0:c0 + 1]), jax.nn.sigmoid(gl[:, c0 + 1:c0 + 2]))
        o_ref[:, ls] = (_finish_pair(acc_ref, hp) * branch * _silu(gate_ref[:, ls].astype(F32))).astype(o_ref.dtype)


def _win_attn(p, ps, bwin, b, s, t=WIN_T, tk=WIN_TK, ahead=2):
    assert t % tk == 0 and WINDOW % tk == 0
    nq = s // t
    w = GROUP_W
    return pl.pallas_call(
        functools.partial(_win_attn_kernel, t=t, tk=tk, ahead=ahead),
        grid=(b, nq),
        in_specs=[
            pl.BlockSpec((t, w), lambda bi, qi: (bi * nq + qi, OFF_NSA_Q // w)),
            pl.BlockSpec((s, NSA_KV_W), lambda bi, qi: (bi, OFF_WIN_K // NSA_KV_W)),
            pl.BlockSpec((s, NSA_KV_W), lambda bi, qi: (bi, OFF_WIN_V // NSA_KV_W)),
            pl.BlockSpec((t, LANES), lambda bi, qi: (bi * nq + qi, 0)),
            pl.BlockSpec((t, w), lambda bi, qi: (bi * nq + qi, OFF_NSA_G // w)),
            pl.BlockSpec((NSA_HEADS, WINDOW + t, t), lambda bi, qi: (0, 0, 0), pipeline_mode=pl.Buffered(1)),
        ],
        out_specs=pl.BlockSpec((t, w), lambda bi, qi: (bi * nq + qi, 0)),
        out_shape=jax.ShapeDtypeStruct((b * s, w), BF16),
        scratch_shapes=[
            pltpu.VMEM((NSA_HEADS, t, LANES), BF16),
            pltpu.VMEM((NSA_KV_HEADS, s, LANES), BF16),
            pltpu.VMEM((2 * NSA_KV_HEADS, LANES, s), BF16),
            pltpu.VMEM((NSA_HEADS, 1, t), F32),
            pltpu.VMEM((NSA_HEADS, LANES, t), F32),
            pltpu.VMEM((ahead, tk, t), F32),
        ],
        compiler_params=_params(("parallel", "arbitrary")),
        name="nsa_win_attn",
    )(p, p, p, ps, p, bwin)


def _mem_attn_kernel(q_ref, gate_ref, kv_ref, o_ref):
    scale = MEM_HEAD_DIM ** -0.5
    for h in range(MEM_HEADS):
        ls = slice(h * LANES, (h + 1) * LANES)
        k = kv_ref[:, ls]
        v = kv_ref[:, GROUP_W + h * LANES:GROUP_W + (h + 1) * LANES]
        s = lax.dot_general(q_ref[:, ls], k, (((1,), (1,)), ((), ())), preferred_element_type=F32) * scale
        m = jnp.max(s, axis=1, keepdims=True)
        e = jnp.exp(s - m)
        l = jnp.sum(e, axis=1, keepdims=True)
        o = jnp.dot(e.astype(BF16), v, preferred_element_type=F32) / l
        o_ref[:, ls] = (o * _silu(gate_ref[:, ls].astype(F32))).astype(o_ref.dtype)


def _mem_attn(p, mem_kv, b, s, t=1024):
    nq = s // t
    w = GROUP_W
    m = mem_kv.shape[0] // b
    return pl.pallas_call(
        _mem_attn_kernel,
        grid=(b, nq),
        in_specs=[
            pl.BlockSpec((t, w), lambda bi, qi: (bi * nq + qi, OFF_MEM_Q // w)),
            pl.BlockSpec((t, w), lambda bi, qi: (bi * nq + qi, OFF_MEM_G // w)),
            pl.BlockSpec((m, 2 * w), lambda bi, qi: (bi, 0)),
        ],
        out_specs=pl.BlockSpec((t, w), lambda bi, qi: (bi * nq + qi, 0)),
        out_shape=jax.ShapeDtypeStruct((b * s, w), BF16),
        compiler_params=_params(("parallel", "arbitrary")),
        name="mem_attn",
    )(p, p, mem_kv)


def _out_proj_kernel(x_ref, of_ref, os_ref, oc_ref, osel_ref, ow_ref, om_ref, w_ref, g_ref, o_ref, *, final):
    w = GROUP_W
    nsa = (oc_ref[...].astype(F32) + osel_ref[...].astype(F32) + ow_ref[...].astype(F32)).astype(BF16)
    acc = x_ref[...]
    for i, part in enumerate((of_ref[...], os_ref[...], nsa, om_ref[...])):
        acc = acc + jnp.dot(part, w_ref[i * w:(i + 1) * w, :], preferred_element_type=F32)
    if final:
        ms = jnp.mean(acc * acc, axis=-1, keepdims=True)
        acc = (acc * lax.rsqrt(ms + EPS)) * g_ref[...]
    o_ref[...] = acc


def _out_proj(x2d, parts, w_out, g, final, tm=1024):
    n, d = x2d.shape
    w = GROUP_W
    part_spec = pl.BlockSpec((tm, w), lambda i: (i, 0))
    return pl.pallas_call(
        functools.partial(_out_proj_kernel, final=final),
        grid=(n // tm,),
        in_specs=[pl.BlockSpec((tm, d), lambda i: (i, 0))] + [part_spec] * 6 + [
            pl.BlockSpec((4 * w, d), lambda i: (0, 0)),
            pl.BlockSpec((1, d), lambda i: (0, 0)),
        ],
        out_specs=pl.BlockSpec((tm, d), lambda i: (i, 0)),
        out_shape=jax.ShapeDtypeStruct((n, d), F32),
        compiler_params=_params(("parallel",)),
        name="out_proj",
    )(x2d, *parts, w_out, g.reshape(1, d))


def _pack_in_proj(w_in_l):
    fox, ssm, nsa, mem = 0, FOX_COLS, FOX_COLS + SSM_COLS, FOX_COLS + SSM_COLS + NSA_COLS
    w = GROUP_W
    q_scale = HEAD_DIM ** -0.5 * LOG2E
    cols = lambda a, n: w_in_l[:, a:a + n]
    kv = lambda slot: nsa + w + slot * NSA_KV_W
    main = ([cols(fox, w) * q_scale, cols(fox + w, 3 * w)]
            + [cols(ssm + w, SSM_CONV_DIM), cols(ssm, w)]
            + [cols(nsa, w) * q_scale, cols(nsa + w + 6 * NSA_KV_W + 3 * NSA_HEADS, w)]
            + [cols(mem, 2 * w)]
            + [cols(kv(2), 4 * NSA_KV_W)]
            + [cols(kv(0), 2 * NSA_KV_W)])
    w_main = jnp.concatenate(main, axis=1).astype(BF16)
    assert w_main.shape[1] == P_COLS
    small = ([cols(fox + 4 * w, FOX_HEADS)] * N_SPLIT
             + [cols(ssm + w + SSM_CONV_DIM, SSM_HEADS)]
             + [cols(nsa + w + 6 * NSA_KV_W, 3 * NSA_HEADS)])
    assert N_SPLIT * FOX_HEADS == PS_DT_LANE and PS_DT_LANE + SSM_HEADS == PS_GATE_LANE
    used = PS_GATE_LANE + 3 * NSA_HEADS
    small.append(jnp.zeros((w_in_l.shape[0], PS_COLS - used), w_in_l.dtype))
    return w_main, jnp.concatenate(small, axis=1).astype(BF16)


def _pad_lanes(v, first=0):
    return jnp.pad(v.astype(F32), (first, LANES - first - v.shape[0])).reshape(1, LANES)


def _trunk(x, mem, norm_g, w_in, fox_f_bias, ssm_conv_w, ssm_conv_b, ssm_dt_bias, ssm_a_log, ssm_d,
           ssm_norm_g, nsa_cmp_pe, nsa_cmp_w1, nsa_cmp_w2, rel_bias_table, mem_norm_g, w_mem_kv, w_out,
           final_norm_g):
    b, s, d = x.shape
    depth = w_in.shape[0]
    n = b * s
    m_tok = mem.shape[1]
    n_cmp = (s - CMP_BLOCK) // CMP_STRIDE + 1
    n_rows = s // CMP_STRIDE
    assert s % 2048 == 0 and s // SEL_BLOCK <= HEAD_DIM and n_rows <= N_CMP_PAD and d == D_MODEL

    tab_flat = rel_bias_table.astype(F32).reshape(-1)
    unbounded = 1 << 30
    bwin = _t5_table(tab_flat, WINDOW + WIN_T, WIN_T, 1, WINDOW, WINDOW)
    bsel = _t5_table(tab_flat, (SEL_T // SEL_TK + 1) * SEL_TK, SEL_T, 1, SEL_TK, unbounded)
    fcmp = _t5_table(tab_flat, 2 * N_CMP_PAD, T_ATT, CMP_STRIDE, CMP_STRIDE * N_CMP_PAD - (CMP_BLOCK - 1), unbounded)

    cs = np.arange(N_CMP_PAD)[None, :] * CMP_STRIDE
    js = np.arange(SEL_BLOCK)[:, None] * SEL_BLOCK
    overlap_t = ((cs < js + SEL_BLOCK) & (cs + CMP_BLOCK > js) & (np.arange(N_CMP_PAD)[None, :] < n_cmp)
                 & (np.arange(SEL_BLOCK)[:, None] < s // SEL_BLOCK)).astype(np.float32)
    overlap_t = jnp.asarray(overlap_t, BF16)

    x2d = x.reshape(n, d)
    mem2d = mem.reshape(b * m_tok, d)
    for l in range(depth):
        w_main, w_small = _pack_in_proj(w_in[l])
        p, ps = _norm_proj(x2d, norm_g[l], (w_main, w_small), (BF16, F32))

        qaug, kaug = _fox_cumsum(ps, _pad_lanes(jnp.tile(fox_f_bias[l], N_SPLIT)), b, s)
        o_fox = _fox_attn(p, qaug, kaug, b, s)

        o_ssd = _ssd(p, ps, ssm_conv_w[l].astype(F32), ssm_conv_b[l].reshape(1, -1).astype(F32),
                     _pad_lanes(ssm_dt_bias[l], PS_DT_LANE), _pad_lanes(ssm_a_log[l], PS_DT_LANE),
                     jnp.repeat(ssm_d[l].astype(F32), HEAD_DIM).reshape(1, GROUP_W),
                     ssm_norm_g[l].reshape(1, GROUP_W).astype(F32), b, s)

        kv_cmp, kv_cmp_t = _compress(p, nsa_cmp_w1[l], nsa_cmp_w2[l], nsa_cmp_pe[l], b, s)
        o_cmp, notsel = _cmp_select(p, ps, kv_cmp, kv_cmp_t, fcmp, overlap_t, b, s)
        o_sel = _sel_attn(tab_flat, p, ps, notsel, bsel, b, s)
        o_win = _win_attn(p, ps, bwin, b, s)

        w_kv = w_mem_kv[l].astype(BF16)
        (mem_kv,) = _norm_proj(mem2d, mem_norm_g[l], (w_kv,), (BF16,), tm=min(512, b * m_tok))
        o_mem = _mem_attn(p, mem_kv, b, s)

        x2d = _out_proj(x2d, (o_fox, o_ssd, o_cmp, o_sel, o_win, o_mem), w_out[l].astype(BF16),
                        final_norm_g, final=(l == depth - 1))
    return x2d.reshape(b, s, d)


def kernel(x, mem, norm_g, w_in, fox_f_bias, ssm_conv_w, ssm_conv_b, ssm_dt_bias, ssm_a_log, ssm_d, ssm_norm_g,
           nsa_cmp_pe, nsa_cmp_w1, nsa_cmp_w2, rel_bias_table, mem_norm_g, w_mem_kv, w_out, final_norm_g):
    return _trunk(x, mem, norm_g, w_in, fox_f_bias, ssm_conv_w, ssm_conv_b, ssm_dt_bias, ssm_a_log, ssm_d,
                  ssm_norm_g, nsa_cmp_pe, nsa_cmp_w1, nsa_cmp_w2, rel_bias_table, mem_norm_g, w_mem_kv, w_out,
                  final_norm_g)
```

```python
import functools
import math

import numpy as np
import jax
import jax.numpy as jnp
from jax import lax
from jax.experimental import pallas as pl
from jax.experimental.pallas import tpu as pltpu

F32 = jnp.float32
BF16 = jnp.bfloat16

D_MODEL = 1024
GROUP_W = 512
HEAD_DIM = 64
EPS = 1e-6
NEG_INF = -1e30
TINY = 1e-30
LOG2E = math.log2(math.e)

FOX_HEADS = 8
SSM_HEADS = 8
SSM_STATE = 128
SSM_GROUPS = 2
SSM_CONV = 4
SSM_CHUNK = 128
SSM_CONV_DIM = GROUP_W + 2 * SSM_GROUPS * SSM_STATE

NSA_HEADS = 8
NSA_KV_HEADS = 2
NSA_REP = NSA_HEADS // NSA_KV_HEADS
NSA_KV_W = NSA_KV_HEADS * HEAD_DIM
CMP_BLOCK = 32
CMP_STRIDE = 16
CMP_HIDDEN = 2 * HEAD_DIM
SEL_BLOCK = 64
SEL_TOPK = 16
WINDOW = 512
SEL_FORCE = 1e9

MEM_HEADS = 4
MEM_HEAD_DIM = GROUP_W // MEM_HEADS
REL_BUCKETS = 32
REL_MAX_DIST = 128

FOX_COLS = 4 * GROUP_W + FOX_HEADS
SSM_COLS = GROUP_W + SSM_CONV_DIM + SSM_HEADS
NSA_COLS = 2 * GROUP_W + 6 * NSA_KV_W + 3 * NSA_HEADS
MEM_COLS = 2 * GROUP_W

LANES = 128
VMEM_LIMIT = 56 * 1024 * 1024

OFF_FOX_Q, OFF_FOX_K, OFF_FOX_V, OFF_FOX_G = 0, 512, 1024, 1536
OFF_SSM_XBC, OFF_SSM_Z = 2048, 3072
OFF_NSA_Q, OFF_NSA_G = 3584, 4096
OFF_MEM_Q, OFF_MEM_G = 4608, 5120
OFF_SEL_K, OFF_SEL_V, OFF_WIN_K, OFF_WIN_V, OFF_CMP_KV = 5632, 5760, 5888, 6016, 6144
P_COLS = 6400
PS_COLS = LANES
PS_DT_LANE = 24
PS_GATE_LANE = 32

T_ATT = 512
SEL_T, SEL_TK = 512, 256
WIN_T, WIN_TK = 512, 256
N_CMP_PAD = 256


def _params(sem):
    return pltpu.CompilerParams(dimension_semantics=sem, vmem_limit_bytes=VMEM_LIMIT)


def _t5_bucket_np(dist):
    n = np.maximum(dist, 0)
    max_exact = REL_BUCKETS // 2
    nf = np.maximum(n, 1).astype(np.float32)
    large = max_exact + (np.log(nf / np.float32(max_exact)) / np.float32(math.log(REL_MAX_DIST / max_exact))
                         * np.float32(REL_BUCKETS - max_exact)).astype(np.int32)
    large = np.minimum(large, REL_BUCKETS - 1)
    return np.where(n < max_exact, n, large).astype(np.int32)


def _silu(x):
    h = 0.5 * x
    return h + h * jnp.tanh(h)


def _norm_proj_kernel(x_ref, g_ref, *refs, chunk):
    x = x_ref[...]
    ms = jnp.mean(x * x, axis=-1, keepdims=True)
    h = ((x * lax.rsqrt(ms + EPS)) * g_ref[...]).astype(BF16)
    n_out = len(refs) // 2
    for w_ref, o_ref in zip(refs[:n_out], refs[n_out:]):
        ncol = o_ref.shape[1]
        for c0 in range(0, ncol, chunk):
            c1 = min(c0 + chunk, ncol)
            o_ref[:, c0:c1] = jnp.dot(h, w_ref[:, c0:c1], preferred_element_type=F32).astype(o_ref.dtype)


def _norm_proj(x2d, g, weights, out_dtypes, tm=512):
    n, d = x2d.shape
    return pl.pallas_call(
        functools.partial(_norm_proj_kernel, chunk=512),
        grid=(n // tm,),
        in_specs=[pl.BlockSpec((tm, d), lambda i: (i, 0)), pl.BlockSpec((1, d), lambda i: (0, 0))]
        + [pl.BlockSpec(w.shape, lambda i: (0, 0)) for w in weights],
        out_specs=[pl.BlockSpec((tm, w.shape[1]), lambda i: (i, 0)) for w in weights],
        out_shape=[jax.ShapeDtypeStruct((n, w.shape[1]), dt) for w, dt in zip(weights, out_dtypes)],
        compiler_params=_params(("parallel",)),
        name="norm_proj",
    )(x2d, g.reshape(1, d), *weights)


def _t5_table_kernel(tab_ref, bucket_ref, o_ref, *, stride, off, limit):
    h = pl.program_id(0)
    rows, cols = o_ref.shape[1], o_ref.shape[2]
    bucket = bucket_ref[...]
    near = jnp.zeros(bucket.shape, F32)
    for b in range(REL_BUCKETS):
        near = jnp.where(bucket == b, tab_ref[b * NSA_HEADS + h] * LOG2E, near)
    far = tab_ref[(REL_BUCKETS - 1) * NSA_HEADS + h] * LOG2E
    base = jnp.concatenate([near] * (rows // near.shape[0]), axis=0)
    rolled = pltpu.roll(base, 0, 1, stride=stride, stride_axis=0)
    d = (lax.broadcasted_iota(jnp.int32, (rows, cols), 1) - stride * lax.broadcasted_iota(jnp.int32, (rows, cols), 0)
         + off)
    o_ref[0] = jnp.where((d < 0) | (d >= limit), NEG_INF, jnp.where(d >= REL_MAX_DIST, far, rolled))


def _t5_table(tab_flat, rows, cols, stride, off, limit):
    assert cols >= 2 * REL_MAX_DIST and rows % 8 == 0
    k = (np.arange(cols) + off) % cols
    bucket = np.broadcast_to(np.where(k < REL_MAX_DIST, _t5_bucket_np(k), -1).astype(np.int32), (8, cols))
    return pl.pallas_call(
        functools.partial(_t5_table_kernel, stride=stride, off=off, limit=limit),
        grid=(NSA_HEADS,),
        in_specs=[
            pl.BlockSpec(memory_space=pltpu.SMEM),
            pl.BlockSpec((8, cols), lambda h: (0, 0)),
        ],
        out_specs=pl.BlockSpec((1, rows, cols), lambda h: (h, 0, 0)),
        out_shape=jax.ShapeDtypeStruct((NSA_HEADS, rows, cols), F32),
        compiler_params=_params(("arbitrary",)),
        name="t5_table",
    )(tab_flat, jnp.asarray(bucket))


def _tri_lower(n):
    r = lax.broadcasted_iota(jnp.int32, (n, n), 0)
    c = lax.broadcasted_iota(jnp.int32, (n, n), 1)
    return (r >= c).astype(F32)


N_SPLIT = 3


def _fox_aug_lane(h, i):
    return LANES * (h // 2) + (HEAD_DIM if h % 2 == 0 else 0) + i


def _fox_aug_consts():
    pq = np.zeros((LANES, GROUP_W), np.float32)
    pk = np.zeros((LANES, GROUP_W), np.float32)
    oq = np.zeros((1, GROUP_W), np.float32)
    ok = np.zeros((1, GROUP_W), np.float32)
    for h in range(FOX_HEADS):
        for i in range(N_SPLIT):
            pq[i * FOX_HEADS + h, _fox_aug_lane(h, i)] = 1.0
            pk[i * FOX_HEADS + h, _fox_aug_lane(h, N_SPLIT + i)] = -1.0
            oq[0, _fox_aug_lane(h, N_SPLIT + i)] = 1.0
            ok[0, _fox_aug_lane(h, i)] = 1.0
    return pq, pk, oq, ok


def _fox_cumsum_kernel(f_ref, b_ref, pq_ref, pk_ref, oq_ref, ok_ref, qa_ref, ka_ref, carry_ref, *, ts):
    @pl.when(pl.program_id(1) == 0)
    def _():
        carry_ref[...] = jnp.zeros_like(carry_ref)

    z = f_ref[...] + b_ref[...]
    logf = (jnp.minimum(z, 0.0) - jnp.log(1.0 + jnp.exp(-jnp.abs(z)))) * LOG2E
    tri = _tri_lower(LANES).astype(BF16)
    group = lax.broadcasted_iota(jnp.int32, (LANES, LANES), 1) // FOX_HEADS
    carry = carry_ref[...]
    for c in range(ts // LANES):
        rows = slice(c * LANES, (c + 1) * LANES)
        cs, rest = carry, logf[rows]
        for _ in range(N_SPLIT):
            piece = rest.astype(BF16)
            rest = rest - piece.astype(F32)
            cs = cs + jnp.dot(tri, piece, preferred_element_type=F32)
        carry = cs[LANES - 1:LANES, :]
        cat, rest = None, cs
        for i in range(N_SPLIT):
            piece = rest.astype(BF16).astype(F32)
            rest = rest - piece
            cat = piece if cat is None else jnp.where(group == i, piece, cat)
        cat = cat.astype(BF16)
        qa_ref[rows, :] = (oq_ref[...] + jnp.dot(cat, pq_ref[...], preferred_element_type=F32)).astype(BF16)
        ka_ref[rows, :] = (ok_ref[...] + jnp.dot(cat, pk_ref[...], preferred_element_type=F32)).astype(BF16)
    carry_ref[...] = carry


def _fox_cumsum(ps, f_bias_pad, b, s, ts=1024):
    ns = s // ts
    pq, pk, oq, ok = _fox_aug_consts()
    const = lambda bi, si: (0, 0)
    return pl.pallas_call(
        functools.partial(_fox_cumsum_kernel, ts=ts),
        grid=(b, ns),
        in_specs=[
            pl.BlockSpec((ts, LANES), lambda bi, si: (bi * ns + si, 0)),
            pl.BlockSpec((1, LANES), const),
            pl.BlockSpec(pq.shape, const),
            pl.BlockSpec(pk.shape, const),
            pl.BlockSpec(oq.shape, const),
            pl.BlockSpec(ok.shape, const),
        ],
        out_specs=[
            pl.BlockSpec((ts, GROUP_W), lambda bi, si: (bi * ns + si, 0)),
            pl.BlockSpec((ts, GROUP_W), lambda bi, si: (bi * ns + si, 0)),
        ],
        out_shape=[jax.ShapeDtypeStruct((b * s, GROUP_W), BF16), jax.ShapeDtypeStruct((b * s, GROUP_W), BF16)],
        scratch_shapes=[pltpu.VMEM((1, LANES), F32)],
        compiler_params=_params(("parallel", "arbitrary")),
        name="fox_cumsum",
    )(ps, f_bias_pad, jnp.asarray(pq, BF16), jnp.asarray(pk, BF16), jnp.asarray(oq), jnp.asarray(ok))


ONES_ROWS = 16


def _vt_rows(h):
    return slice(0, HEAD_DIM + ONES_ROWS) if h % 2 == 0 else slice(HEAD_DIM - ONES_ROWS, LANES)


def _store_vt(vt_ref, i, rows, v_pair):
    vt = v_pair.astype(F32).T.astype(BF16)
    ones = jnp.ones((HEAD_DIM, v_pair.shape[0]), BF16)
    vt_ref[i, 0:HEAD_DIM, rows] = vt[0:HEAD_DIM]
    vt_ref[i, HEAD_DIM:LANES, rows] = ones
    vt_ref[i + 1, 0:HEAD_DIM, rows] = ones
    vt_ref[i + 1, HEAD_DIM:LANES, rows] = vt[HEAD_DIM:LANES]


def _store_vt_groups(vt_ref, rows, v_groups):
    vt = v_groups.astype(F32).T.astype(BF16)
    ones = jnp.ones((HEAD_DIM, v_groups.shape[0]), BF16)
    for g in range(NSA_KV_HEADS):
        vg = vt[g * HEAD_DIM:(g + 1) * HEAD_DIM]
        vt_ref[2 * g, 0:HEAD_DIM, rows] = vg
        vt_ref[2 * g, HEAD_DIM:LANES, rows] = ones
        vt_ref[2 * g + 1, 0:HEAD_DIM, rows] = ones
        vt_ref[2 * g + 1, HEAD_DIM:LANES, rows] = vg


def _finish_pair(acc_ref, hp):
    ae, ao = acc_ref[2 * hp], acc_ref[2 * hp + 1]
    top = lax.broadcasted_iota(jnp.int32, ae.shape, 0) < HEAD_DIM
    both = jnp.where(top, ae / jnp.maximum(ae[HEAD_DIM:HEAD_DIM + 1, :], TINY),
                     ao / jnp.maximum(ao[HEAD_DIM - 1:HEAD_DIM, :], TINY))
    return both.T


def _swap_halves(x_b):
    return pltpu.roll(x_b.astype(F32), HEAD_DIM, 1).astype(BF16)


def _half_masks(rows):
    lo = jnp.where(lax.broadcasted_iota(jnp.int32, (rows, LANES), 1) < HEAD_DIM, 1.0, 0.0)
    return lo.astype(BF16), (1.0 - lo).astype(BF16)


def _fox_attn_kernel(q_ref, qa_ref, k_ref, ka_ref, v_ref, gate_ref, o_ref, qs_ref, kk_ref, vt_ref, m_ref, acc_ref,
                     pre_ref, *, t, tk, ahead):
    qi = pl.program_id(1)
    lo_q, hi_q = _half_masks(t)
    cm = lax.broadcasted_iota(jnp.int32, (tk, t), 0) - lax.broadcasted_iota(jnp.int32, (tk, t), 1)

    @pl.when(qi == 0)
    def _():
        lo_k, hi_k = _half_masks(tk)

        def merge(j, c):
            rows = pl.ds(pl.multiple_of(j * tk, tk), tk)
            for hp in range(FOX_HEADS // 2):
                ls = slice(hp * LANES, (hp + 1) * LANES)
                kp, ka, vp = k_ref[rows, ls], ka_ref[rows, ls], v_ref[rows, ls]
                kk_ref[2 * hp, rows, :] = kp * lo_k + ka * hi_k
                kk_ref[2 * hp + 1, rows, :] = ka * lo_k + kp * hi_k
                _store_vt(vt_ref, 2 * hp, rows, vp)
            return c

        lax.fori_loop(0, k_ref.shape[0] // tk, merge, 0)

    for hp in range(FOX_HEADS // 2):
        ls = slice(hp * LANES, (hp + 1) * LANES)
        qp, qa = q_ref[:, ls], qa_ref[:, ls]
        qs_ref[2 * hp] = (qp * lo_q + qa * hi_q).astype(F32).T.astype(BF16)
        qs_ref[2 * hp + 1] = (qa * lo_q + qp * hi_q).astype(F32).T.astype(BF16)
    m_ref[...] = jnp.full(m_ref.shape, NEG_INF, F32)
    acc_ref[...] = jnp.zeros_like(acc_ref)

    def rows_of(j):
        return pl.ds(pl.multiple_of(j * tk, tk), tk)

    def scores(j, h):
        return jnp.dot(kk_ref[h, rows_of(j), :], qs_ref[h], preferred_element_type=F32)

    def tile(j, diag, j_next):
        queue = [pre_ref[i] for i in range(ahead)]
        for h in range(FOX_HEADS):
            if h + ahead < FOX_HEADS:
                queue.append(scores(j, h + ahead))
            elif j_next is not None:
                pre_ref[h + ahead - FOX_HEADS] = scores(j_next, h + ahead - FOX_HEADS)
            s = queue.pop(0)
            if diag:
                s = jnp.where(cm <= qi * t - j * tk, s, NEG_INF)
            m_old = m_ref[h]
            ps, ms = [], []
            for c in range(2):
                cs = slice(c * (t // 2), (c + 1) * (t // 2))
                sc = s[:, cs]
                mc = jnp.maximum(m_old[:, cs], jnp.max(sc, axis=0, keepdims=True))
                ps.append(jnp.exp2(sc - mc).astype(BF16))
                ms.append(mc)
            m_new = jnp.concatenate(ms, axis=1)
            p = jnp.concatenate(ps, axis=1)
            m_ref[h] = m_new
            vr = _vt_rows(h)
            acc_ref[h, vr] = jnp.exp2(m_old - m_new) * acc_ref[h, vr] + jnp.dot(vt_ref[h, vr, rows_of(j)], p,
                                                                                preferred_element_type=F32)

    jd = (qi * t) // tk
    n_diag = max(t // tk, 1)
    for i in range(ahead):
        pre_ref[i] = scores(0, i)

    def body(j, c):
        tile(j, False, j + 1)
        return c

    lax.fori_loop(0, jd, body, 0)
    for dj in range(n_diag):
        tile(jd + dj, True, jd + dj + 1 if dj + 1 < n_diag else None)

    for hp in range(FOX_HEADS // 2):
        ls = slice(hp * LANES, (hp + 1) * LANES)
        o_ref[:, ls] = (_finish_pair(acc_ref, hp) * _silu(gate_ref[:, ls].astype(F32))).astype(o_ref.dtype)


def _fox_attn(p, qaug, kaug, b, s, t=512, tk=256, ahead=2):
    nq = s // t
    w = GROUP_W
    return pl.pallas_call(
        functools.partial(_fox_attn_kernel, t=t, tk=tk, ahead=ahead),
        grid=(b, nq),
        in_specs=[
            pl.BlockSpec((t, w), lambda bi, qi: (bi * nq + qi, OFF_FOX_Q // w)),
            pl.BlockSpec((t, w), lambda bi, qi: (bi * nq + qi, 0)),
            pl.BlockSpec((s, w), lambda bi, qi: (bi, OFF_FOX_K // w)),
            pl.BlockSpec((s, w), lambda bi, qi: (bi, 0)),
            pl.BlockSpec((s, w), lambda bi, qi: (bi, OFF_FOX_V // w)),
            pl.BlockSpec((t, w), lambda bi, qi: (bi * nq + qi, OFF_FOX_G // w)),
        ],
        out_specs=pl.BlockSpec((t, w), lambda bi, qi: (bi * nq + qi, 0)),
        out_shape=jax.ShapeDtypeStruct((b * s, w), BF16),
        scratch_shapes=[
            pltpu.VMEM((FOX_HEADS, LANES, t), BF16),
            pltpu.VMEM((FOX_HEADS, s, LANES), BF16),
            pltpu.VMEM((FOX_HEADS, LANES, s), BF16),
            pltpu.VMEM((FOX_HEADS, 1, t), F32),
            pltpu.VMEM((FOX_HEADS, LANES, t), F32),
            pltpu.VMEM((ahead, tk, t), F32),
        ],
        compiler_params=_params(("parallel", "arbitrary")),
        name="fox_attn",
    )(p, qaug, p, kaug, p, p)


def _dot_split(lhs_f32, rhs_b):
    out, rest = None, lhs_f32
    for _ in range(N_SPLIT):
        piece = rest.astype(BF16)
        rest = rest - piece.astype(F32)
        term = jnp.dot(piece, rhs_b, preferred_element_type=F32)
        out = term if out is None else out + term
    return out


def _ssd_kernel(z_ref, xbc_ref, dt_ref, cw_ref, cb_ref, dtb_ref, alog_ref, dsk_ref, ng_ref, o_ref,
                xpad_ref, xc_ref, state_ref, y_ref, *, nch):
    q = SSM_CHUNK
    rows_all = nch * q
    halo = 8

    @pl.when(pl.program_id(1) == 0)
    def _():
        xpad_ref[0:halo, :] = jnp.zeros((halo, SSM_CONV_DIM), F32)
        state_ref[...] = jnp.zeros_like(state_ref)

    xpad_ref[halo:halo + rows_all, :] = xbc_ref[...].astype(F32)
    y = cb_ref[...]
    for k in range(SSM_CONV):
        off = halo - (SSM_CONV - 1) + k
        y = y + cw_ref[k:k + 1, :] * xpad_ref[off:off + rows_all, :]
    xpad_ref[0:halo, :] = xpad_ref[rows_all:rows_all + halo, :]
    xc_ref[...] = _silu(y)

    x_dt = dt_ref[...] + dtb_ref[...]
    dt_all = jnp.maximum(x_dt, 0.0) + jnp.log(1.0 + jnp.exp(-jnp.abs(x_dt)))
    a_all = dt_all * (-jnp.exp(alog_ref[...]))
    tri_t = (lax.broadcasted_iota(jnp.int32, (q, q), 0) <= lax.broadcasted_iota(jnp.int32, (q, q), 1)).astype(BF16)
    er = lax.broadcasted_iota(jnp.int32, (LANES, GROUP_W), 0)
    ec = lax.broadcasted_iota(jnp.int32, (LANES, GROUP_W), 1)
    expand = jnp.where(ec // HEAD_DIM == er - PS_DT_LANE, 1.0, 0.0).astype(BF16)
    row = lax.broadcasted_iota(jnp.int32, (q, q), 0)
    col = lax.broadcasted_iota(jnp.int32, (q, q), 1)
    causal = row >= col
    lo = lax.broadcasted_iota(jnp.int32, (q, LANES), 1) < HEAD_DIM
    gw = GROUP_W // SSM_GROUPS
    hpg = SSM_HEADS // SSM_GROUPS

    for c in range(nch):
        rs = slice(c * q, (c + 1) * q)
        xs = xc_ref[rs, :GROUP_W]
        dt = dt_all[rs]
        acs_t = _dot_split(a_all[rs].T, tri_t)
        acs = acs_t.T
        stacked = jnp.concatenate([dt, jnp.exp(acs), jnp.exp(acs[q - 1:q, :] - acs)], axis=0)
        full = _dot_split(stacked, expand)
        dt_full, eacs_full, dec_full = full[0:q], full[q:2 * q], full[2 * q:3 * q]
        xdt = xs * dt_full
        xdt_b = xdt.astype(BF16)
        xdec_b = (xdt * dec_full).astype(BF16)

        for g in range(SSM_GROUPS):
            bm = xc_ref[rs, GROUP_W + g * SSM_STATE:GROUP_W + (g + 1) * SSM_STATE]
            cm = xc_ref[rs, GROUP_W + (SSM_GROUPS + g) * SSM_STATE:GROUP_W + (SSM_GROUPS + g + 1) * SSM_STATE]
            bm_b = bm.astype(BF16)
            cm_b = cm.astype(BF16)
            gs = slice(g * gw, (g + 1) * gw)
            cbg = lax.dot_general(cm_b, bm_b, (((1,), (1,)), ((), ())), preferred_element_type=F32)
            st = state_ref[:, gs]
            y_off = jnp.dot(cm_b, st.astype(BF16), preferred_element_type=F32) * eacs_full[:, gs]
            cst = jnp.dot(bm.T.astype(BF16), xdec_b[:, gs], preferred_element_type=F32)
            state_ref[:, gs] = st * eacs_full[q - 1:q, gs] + cst
            for hp in range(hpg // 2):
                ls = slice(g * gw + hp * LANES, g * gw + (hp + 1) * LANES)
                yd = []
                for e in range(2):
                    h = g * hpg + 2 * hp + e
                    hl = PS_DT_LANE + h
                    seg = jnp.exp(jnp.where(causal, acs[:, hl:hl + 1] - acs_t[hl:hl + 1, :], NEG_INF))
                    yd.append(jnp.dot((cbg * seg).astype(BF16), xdt_b[:, ls], preferred_element_type=F32))
                y_ref[:, ls] = jnp.where(lo, yd[0], yd[1]) + y_off[:, hp * LANES:(hp + 1) * LANES]

        yz = (y_ref[...] + xs * dsk_ref[...]) * _silu(z_ref[rs, :].astype(F32))
        for g in range(SSM_GROUPS):
            gs = slice(g * gw, (g + 1) * gw)
            blk = yz[:, gs]
            ms = jnp.mean(blk * blk, axis=-1, keepdims=True)
            o_ref[rs, gs] = ((blk * lax.rsqrt(ms + EPS)) * ng_ref[:, gs]).astype(o_ref.dtype)


def _ssd(p, ps, conv_w, conv_b, dt_bias_pad, a_log_pad, d_full, norm_g, b, s, nch=16):
    q = SSM_CHUNK
    rows = nch * q
    nc = s // rows
    row = lambda bi, ci: (bi * nc + ci)
    const = lambda bi, ci: (0, 0)
    return pl.pallas_call(
        functools.partial(_ssd_kernel, nch=nch),
        grid=(b, nc),
        in_specs=[
            pl.BlockSpec((rows, GROUP_W), lambda bi, ci: (row(bi, ci), OFF_SSM_Z // GROUP_W)),
            pl.BlockSpec((rows, SSM_CONV_DIM), lambda bi, ci: (row(bi, ci), OFF_SSM_XBC // SSM_CONV_DIM)),
            pl.BlockSpec((rows, LANES), lambda bi, ci: (row(bi, ci), 0)),
            pl.BlockSpec((SSM_CONV, SSM_CONV_DIM), const),
            pl.BlockSpec((1, SSM_CONV_DIM), const),
            pl.BlockSpec((1, LANES), const),
            pl.BlockSpec((1, LANES), const),
            pl.BlockSpec((1, GROUP_W), const),
            pl.BlockSpec((1, GROUP_W), const),
        ],
        out_specs=pl.BlockSpec((rows, GROUP_W), lambda bi, ci: (row(bi, ci), 0)),
        out_shape=jax.ShapeDtypeStruct((b * s, GROUP_W), BF16),
        scratch_shapes=[
            pltpu.VMEM((rows + 8, SSM_CONV_DIM), F32),
            pltpu.VMEM((rows, SSM_CONV_DIM), F32),
            pltpu.VMEM((SSM_STATE, GROUP_W), F32),
            pltpu.VMEM((q, GROUP_W), F32),
        ],
        compiler_params=_params(("parallel", "arbitrary")),
        name="ssd",
    )(p, p, ps, conv_w, conv_b, dt_bias_pad, a_log_pad, d_full, norm_g)


def _compress_kernel(x_ref, w1_ref, w2_ref, pe_ref, o_ref, ot_ref, xf_ref):
    s = x_ref.shape[0]
    n = s // CMP_STRIDE
    for sb in range(2):
        xf_ref[sb] = x_ref[:, sb * LANES:(sb + 1) * LANES].astype(F32)
    o_ref[...] = jnp.zeros_like(o_ref)
    ot_ref[...] = jnp.zeros_like(ot_ref)
    for sb in range(2):
        first = jnp.zeros((n, 2 * CMP_HIDDEN), F32)
        second = jnp.zeros((n, 2 * CMP_HIDDEN), F32)
        for l in range(CMP_STRIDE):
            xl = xf_ref[sb, pl.ds(l, n, stride=CMP_STRIDE), :]
            first = first + jnp.dot((xl + pe_ref[sb, l:l + 1, :]).astype(BF16), w1_ref[sb, l],
                                    preferred_element_type=F32)
            second = second + jnp.dot((xl + pe_ref[sb, CMP_STRIDE + l:CMP_STRIDE + l + 1, :]).astype(BF16),
                                      w1_ref[sb, CMP_STRIDE + l], preferred_element_type=F32)
        h = _silu(first + pltpu.roll(second, n - 1, 0))
        o = jnp.dot(h.astype(BF16), w2_ref[sb], preferred_element_type=F32)
        for e in range(2):
            oe = o[:, e * LANES:(e + 1) * LANES]
            o_ref[0, 2 * sb + e, 0:n, :] = oe.astype(o_ref.dtype)
            ot_ref[0, 2 * sb + e, :, 0:n] = oe.T.astype(ot_ref.dtype)


def _blockdiag2(a):
    z = jnp.zeros_like(a)
    return jnp.concatenate([jnp.concatenate([a, z], axis=-1), jnp.concatenate([z, a], axis=-1)], axis=-2)


def _compress(p, cmp_w1, cmp_w2, cmp_pe, b, s):
    nslot = 2 * NSA_KV_HEADS
    w1 = _blockdiag2(cmp_w1.astype(BF16).reshape(2, CMP_BLOCK, HEAD_DIM, CMP_HIDDEN))
    w2 = _blockdiag2(jnp.concatenate([cmp_w2, cmp_w2], axis=-1).astype(BF16))
    pe = jnp.concatenate([cmp_pe, cmp_pe], axis=-1).astype(F32)
    kw = 2 * NSA_KV_W
    return pl.pallas_call(
        _compress_kernel,
        grid=(b,),
        in_specs=[
            pl.BlockSpec((s, kw), lambda bi: (bi, OFF_CMP_KV // kw)),
            pl.BlockSpec(w1.shape, lambda bi: (0, 0, 0, 0)),
            pl.BlockSpec(w2.shape, lambda bi: (0, 0, 0)),
            pl.BlockSpec(pe.shape, lambda bi: (0, 0, 0)),
        ],
        out_specs=[pl.BlockSpec((1, nslot, N_CMP_PAD, LANES), lambda bi: (bi, 0, 0, 0)),
                   pl.BlockSpec((1, nslot, LANES, N_CMP_PAD), lambda bi: (bi, 0, 0, 0))],
        out_shape=[jax.ShapeDtypeStruct((b, nslot, N_CMP_PAD, LANES), BF16),
                   jax.ShapeDtypeStruct((b, nslot, LANES, N_CMP_PAD), BF16)],
        scratch_shapes=[pltpu.VMEM((2, s, LANES), F32)],
        compiler_params=_params(("parallel",)),
        name="nsa_compress",
    )(p, w1, w2, pe)


def _cmp_select_kernel(q_ref, kv_ref, vt_ref, gl_ref, gate_ref, fc_ref, ovt_ref, o_ref, ns_ref):
    t = T_ATT
    ncp = kv_ref.shape[2]
    qi = pl.program_id(1)
    t0 = qi * t
    lo = lax.broadcasted_iota(jnp.int32, (t, LANES), 1) < HEAD_DIM
    zero = jnp.zeros((t, LANES), BF16)
    start = pl.multiple_of(ncp - qi * (t // CMP_STRIDE), CMP_STRIDE)
    gl = gl_ref[...]

    nsel = SEL_BLOCK
    jrow = lax.broadcasted_iota(jnp.int32, (nsel, t), 0)
    cur = (t0 + lax.broadcasted_iota(jnp.int32, (nsel, t), 1)) // SEL_BLOCK
    forced = (jrow == 0) | (jrow == cur) | (jrow == cur - 1)
    past = jrow <= cur

    for g in range(NSA_KV_HEADS):
        kc = kv_ref[0, g]
        vct = vt_ref[0, NSA_KV_HEADS + g]
        psum = jnp.zeros((ncp, t), F32)
        outs = []

        def scores(r):
            h = g * NSA_REP + r
            qp = q_ref[:, (h // 2) * LANES:(h // 2 + 1) * LANES]
            qe = jnp.where(lo, qp, zero) if r % 2 == 0 else jnp.where(lo, zero, qp)
            return lax.dot_general(kc, qe, (((1,), (1,)), ((), ())), preferred_element_type=F32)

        queue = [scores(0), scores(1)]
        for r in range(NSA_REP):
            h = g * NSA_REP + r
            if r + 2 < NSA_REP:
                queue.append(scores(r + 2))
            s = fc_ref[h, pl.ds(start, ncp), :] + queue.pop(0)
            m = jnp.maximum(jnp.max(s, axis=0, keepdims=True), 0.1 * NEG_INF)
            e = jnp.exp2(s - m)
            pr = e * (1.0 / jnp.maximum(jnp.sum(e, axis=0, keepdims=True), TINY))
            psum = psum + pr
            oc = jnp.dot(vct, pr.astype(BF16), preferred_element_type=F32)
            outs.append(oc.T * jax.nn.sigmoid(gl[:, PS_GATE_LANE + h:PS_GATE_LANE + h + 1]))
        for hp in range(NSA_REP // 2):
            ls = slice((g * NSA_REP // 2 + hp) * LANES, (g * NSA_REP // 2 + hp + 1) * LANES)
            o = jnp.where(lo, outs[2 * hp], outs[2 * hp + 1])
            o_ref[:, ls] = (o * _silu(gate_ref[:, ls].astype(F32))).astype(o_ref.dtype)

        imp_t = jnp.zeros((nsel, t), F32)
        rest = psum
        for _ in range(N_SPLIT):
            piece = rest.astype(BF16)
            rest = rest - piece.astype(F32)
            imp_t = imp_t + jnp.dot(ovt_ref[...], piece, preferred_element_type=F32)
        imp_t = jnp.where(past, jnp.where(forced, SEL_FORCE, imp_t), -SEL_FORCE)
        sub = 8
        sub_row = lax.broadcasted_iota(jnp.int32, (sub, t), 0)
        rows = [imp_t[k * sub:(k + 1) * sub] for k in range(nsel // sub)]
        rank = [jnp.zeros((sub, t), F32) for _ in rows]
        for i in range(nsel):
            bi = imp_t[i:i + 1, :]
            for k, x in enumerate(rows):
                if k * sub > i:
                    ahead = jnp.where(bi >= x, 1.0, 0.0)
                elif k * sub + sub - 1 <= i:
                    ahead = jnp.where(bi > x, 1.0, 0.0)
                else:
                    ahead = jnp.where(sub_row > i - k * sub, jnp.where(bi >= x, 1.0, 0.0),
                                      jnp.where(bi > x, 1.0, 0.0))
                rank[k] = rank[k] + ahead
        rank = jnp.concatenate(rank, axis=0)
        notsel = jnp.where((rank < float(SEL_TOPK)) & past, 0.0, 1.0)
        ns2 = jnp.concatenate([notsel, notsel], axis=0).T
        ns_ref[:, g * LANES:(g + 1) * LANES] = ns2.astype(ns_ref.dtype)


def _cmp_select(p, ps, kvc, kvc_t, fc, overlap_t, b, s):
    t = T_ATT
    nq = s // t
    w = GROUP_W
    ncp = kvc.shape[2]
    return pl.pallas_call(
        _cmp_select_kernel,
        grid=(b, nq),
        in_specs=[
            pl.BlockSpec((t, w), lambda bi, qi: (bi * nq + qi, OFF_NSA_Q // w)),
            pl.BlockSpec((1, 2 * NSA_KV_HEADS, ncp, LANES), lambda bi, qi: (bi, 0, 0, 0)),
            pl.BlockSpec((1, 2 * NSA_KV_HEADS, LANES, ncp), lambda bi, qi: (bi, 0, 0, 0)),
            pl.BlockSpec((t, LANES), lambda bi, qi: (bi * nq + qi, 0)),
            pl.BlockSpec((t, w), lambda bi, qi: (bi * nq + qi, OFF_NSA_G // w)),
            pl.BlockSpec((NSA_HEADS, 2 * ncp, t), lambda bi, qi: (0, 0, 0)),
            pl.BlockSpec((SEL_BLOCK, ncp), lambda bi, qi: (0, 0)),
        ],
        out_specs=[
            pl.BlockSpec((t, w), lambda bi, qi: (bi * nq + qi, 0)),
            pl.BlockSpec((t, NSA_KV_HEADS * LANES), lambda bi, qi: (bi * nq + qi, 0)),
        ],
        out_shape=[jax.ShapeDtypeStruct((b * s, w), BF16),
                   jax.ShapeDtypeStruct((b * s, NSA_KV_HEADS * LANES), BF16)],
        compiler_params=_params(("parallel", "arbitrary")),
        name="nsa_cmp_select",
    )(p, kvc, kvc_t, ps, p, fc, overlap_t)


def _sel_attn_kernel(tab_ref, q_ref, ns_ref, k_ref, v_ref, gl_ref, gate_ref, bs_ref, o_ref,
                     qa_ref, kk_ref, vt_ref, m_ref, acc_ref, pre_ref, *, t, tk, ahead):
    qi = pl.program_id(1)
    lo_q, hi_q = _half_masks(t)
    lo = lax.broadcasted_iota(jnp.int32, (t, LANES), 1) < HEAD_DIM
    n_near = t // tk + 1

    @pl.when(qi == 0)
    def _():
        lane = lax.broadcasted_iota(jnp.int32, (tk, LANES), 1)
        krow = lax.broadcasted_iota(jnp.int32, (tk, LANES), 0)
        lo_f = jnp.where(lane < HEAD_DIM, 1.0, 0.0)
        hi_f = 1.0 - lo_f
        lo_k, hi_k = lo_f.astype(BF16), hi_f.astype(BF16)

        def merge(j, c):
            ks = pl.multiple_of(j * tk, tk)
            rows = pl.ds(ks, tk)
            hot = jnp.where((lane % HEAD_DIM) == (ks + krow) // SEL_BLOCK, NEG_INF, 0.0)
            hot_lo, hot_hi = (hot * lo_f).astype(BF16), (hot * hi_f).astype(BF16)
            k01 = k_ref[rows, :]
            k10 = _swap_halves(k01)
            kk_ref[0, rows, :] = k01 * lo_k + hot_hi
            kk_ref[1, rows, :] = hot_lo + k10 * hi_k
            kk_ref[2, rows, :] = k10 * lo_k + hot_hi
            kk_ref[3, rows, :] = hot_lo + k01 * hi_k
            _store_vt_groups(vt_ref, rows, v_ref[rows, :])
            return c

        lax.fori_loop(0, k_ref.shape[0] // tk, merge, 0)

    for h in range(NSA_HEADS):
        g = h // NSA_REP
        qp = q_ref[:, (h // 2) * LANES:(h // 2 + 1) * LANES]
        ns = ns_ref[:, g * LANES:(g + 1) * LANES]
        qa = qp * lo_q + ns * hi_q if h % 2 == 0 else ns * lo_q + qp * hi_q
        qa_ref[h] = qa.astype(F32).T.astype(BF16)
    m_ref[...] = jnp.full(m_ref.shape, NEG_INF, F32)
    acc_ref[...] = jnp.zeros_like(acc_ref)

    kv = lambda h: 2 * (h // NSA_REP) + h % 2

    def rows_of(j):
        return pl.ds(pl.multiple_of(j * tk, tk), tk)

    def scores(j, h):
        return jnp.dot(kk_ref[kv(h), rows_of(j), :], qa_ref[h], preferred_element_type=F32)

    def tile(j, near, j_next):
        rows = rows_of(j)
        queue = [pre_ref[i] for i in range(ahead)]
        for h in range(NSA_HEADS):
            if h + ahead < NSA_HEADS:
                queue.append(scores(j, h + ahead))
            elif j_next is not None:
                pre_ref[h + ahead - NSA_HEADS] = scores(j_next, h + ahead - NSA_HEADS)
            s = queue.pop(0)
            m_old = m_ref[h]
            if near is None:
                far = tab_ref[(REL_BUCKETS - 1) * NSA_HEADS + h] * LOG2E
                m_new = jnp.maximum(m_old, jnp.max(s, axis=0, keepdims=True) + far)
                p = jnp.exp2(s - (m_new - far)).astype(BF16)
            else:
                s = bs_ref[h, near * tk:(near + 1) * tk, :] + s
                m_new = jnp.maximum(m_old, jnp.max(s, axis=0, keepdims=True))
                p = jnp.exp2(s - m_new).astype(BF16)
            m_ref[h] = m_new
            vr = _vt_rows(h)
            acc_ref[h, vr] = jnp.exp2(m_old - m_new) * acc_ref[h, vr] + jnp.dot(vt_ref[kv(h), vr, rows], p,
                                                                                preferred_element_type=F32)

    jd = (qi * t) // tk
    for i in range(ahead):
        pre_ref[i] = scores(0, i)

    def body(j, c):
        tile(j, None, j + 1)
        return c

    lax.fori_loop(0, jnp.maximum(jd - 1, 0), body, 0)

    @pl.when(qi > 0)
    def _():
        tile(jd - 1, 0, jd)

    for dj in range(n_near - 1):
        tile(jd + dj, 1 + dj, jd + dj + 1 if dj + 2 < n_near else None)

    gl = gl_ref[...]
    for hp in range(NSA_HEADS // 2):
        ls = slice(hp * LANES, (hp + 1) * LANES)
        c0 = PS_GATE_LANE + NSA_HEADS + 2 * hp
        branch = jnp.where(lo, jax.nn.sigmoid(gl[:, c0:c0 + 1]), jax.nn.sigmoid(gl[:, c0 + 1:c0 + 2]))
        o_ref[:, ls] = (_finish_pair(acc_ref, hp) * branch * _silu(gate_ref[:, ls].astype(F32))).astype(o_ref.dtype)


def _sel_attn(tab_flat, p, ps, notsel, bsel, b, s, t=512, tk=256, ahead=2):
    nq = s // t
    w = GROUP_W
    kw = NSA_KV_HEADS * LANES
    n_near = t // tk + 1
    return pl.pallas_call(
        functools.partial(_sel_attn_kernel, t=t, tk=tk, ahead=ahead),
        grid=(b, nq),
        in_specs=[
            pl.BlockSpec(memory_space=pltpu.SMEM),
            pl.BlockSpec((t, w), lambda bi, qi: (bi * nq + qi, OFF_NSA_Q // w)),
            pl.BlockSpec((t, kw), lambda bi, qi: (bi * nq + qi, 0)),
            pl.BlockSpec((s, NSA_KV_W), lambda bi, qi: (bi, OFF_SEL_K // NSA_KV_W)),
            pl.BlockSpec((s, NSA_KV_W), lambda bi, qi: (bi, OFF_SEL_V // NSA_KV_W)),
            pl.BlockSpec((t, LANES), lambda bi, qi: (bi * nq + qi, 0)),
            pl.BlockSpec((t, w), lambda bi, qi: (bi * nq + qi, OFF_NSA_G // w)),
            pl.BlockSpec((NSA_HEADS, n_near * tk, t), lambda bi, qi: (0, 0, 0), pipeline_mode=pl.Buffered(1)),
        ],
        out_specs=pl.BlockSpec((t, w), lambda bi, qi: (bi * nq + qi, 0)),
        out_shape=jax.ShapeDtypeStruct((b * s, w), BF16),
        scratch_shapes=[
            pltpu.VMEM((NSA_HEADS, LANES, t), BF16),
            pltpu.VMEM((2 * NSA_KV_HEADS, s, LANES), BF16),
            pltpu.VMEM((2 * NSA_KV_HEADS, LANES, s), BF16),
            pltpu.VMEM((NSA_HEADS, 1, t), F32),
            pltpu.VMEM((NSA_HEADS, LANES, t), F32),
            pltpu.VMEM((ahead, tk, t), F32),
        ],
        compiler_params=_params(("parallel", "arbitrary")),
        name="nsa_sel_attn",
    )(tab_flat, p, notsel, p, p, ps, p, bsel)


def _win_attn_kernel(q_ref, k_ref, v_ref, gl_ref, gate_ref, bw_ref, o_ref, qe_ref, kk_ref, vt_ref, m_ref, acc_ref,
                     pre_ref, *, t, tk, ahead):
    qi = pl.program_id(1)
    lo_q, hi_q = _half_masks(t)
    lo = lax.broadcasted_iota(jnp.int32, (t, LANES), 1) < HEAD_DIM
    n_before = WINDOW // tk
    n_tiles = n_before + t // tk

    @pl.when(qi == 0)
    def _():
        lo_k, hi_k = _half_masks(tk)

        def merge(j, c):
            rows = pl.ds(pl.multiple_of(j * tk, tk), tk)
            k01 = k_ref[rows, :]
            k10 = _swap_halves(k01)
            kk_ref[0, rows, :] = k01 * lo_k + k10 * hi_k
            kk_ref[1, rows, :] = k10 * lo_k + k01 * hi_k
            _store_vt_groups(vt_ref, rows, v_ref[rows, :])
            return c

        lax.fori_loop(0, v_ref.shape[0] // tk, merge, 0)

    for h in range(NSA_HEADS):
        qp = q_ref[:, (h // 2) * LANES:(h // 2 + 1) * LANES]
        qe_ref[h] = qp * lo_q if h % 2 == 0 else qp * hi_q
    m_ref[...] = jnp.full(m_ref.shape, NEG_INF, F32)
    acc_ref[...] = jnp.zeros_like(acc_ref)

    def rows_of(w):
        return pl.ds(pl.multiple_of((qi * (t // tk) - n_before + w) * tk, tk), tk)

    def scores(w, h):
        return lax.dot_general(kk_ref[h // NSA_REP, rows_of(w), :], qe_ref[h], (((1,), (1,)), ((), ())),
                               preferred_element_type=F32)

    def tile(w):
        rows = rows_of(w)
        queue = [pre_ref[i] for i in range(ahead)]
        for h in range(NSA_HEADS):
            if h + ahead < NSA_HEADS:
                queue.append(scores(w, h + ahead))
            elif w + 1 < n_tiles:
                pre_ref[h + ahead - NSA_HEADS] = scores(w + 1, h + ahead - NSA_HEADS)
            s = bw_ref[h, w * tk:(w + 1) * tk, :] + queue.pop(0)
            m_old = m_ref[h]
            m_new = jnp.maximum(m_old, jnp.max(s, axis=0, keepdims=True))
            m_ref[h] = m_new
            p = jnp.exp2(s - m_new).astype(BF16)
            vr = _vt_rows(h)
            acc_ref[h, vr] = jnp.exp2(m_old - m_new) * acc_ref[h, vr] + jnp.dot(
                vt_ref[2 * (h // NSA_REP) + h % 2, vr, rows], p, preferred_element_type=F32)

    first = jnp.maximum(n_before - qi * (t // tk), 0)
    for i in range(ahead):
        pre_ref[i] = scores(first, i)
    per_q = t // tk
    first_qi = [-(-(n_before - w) // per_q) if w < n_before else 0 for w in range(n_tiles)]
    for need in sorted(set(first_qi), reverse=True):
        group = [w for w in range(n_tiles) if first_qi[w] == need]

        def run(group=group):
            for w in group:
                tile(w)

        if need > 0:
            pl.when(qi >= need)(run)
        else:
            run()

    gl = gl_ref[...]
    for hp in range(NSA_HEADS // 2):
        ls = slice(hp * LANES, (hp + 1) * LANES)
        c0 = PS_GATE_LANE + 2 * NSA_HEADS + 2 * hp
        branch = jnp.where(lo, jax.nn.sigmoid(gl[:, c0:c0 + 1]), jax.nn.sigmoid(gl[:, c0 + 1:c0 + 2]))
        o_ref[:, ls] = (_finish_pair(acc_ref, hp) * branch * _silu(gate_ref[:, ls].astype(F32))).astype(o_ref.dtype)


def _win_attn(p, ps, bwin, b, s, t=WIN_T, tk=WIN_TK, ahead=2):
    assert t % tk == 0 and WINDOW % tk == 0
    nq = s // t
    w = GROUP_W
    return pl.pallas_call(
        functools.partial(_win_attn_kernel, t=t, tk=tk, ahead=ahead),
        grid=(b, nq),
        in_specs=[
            pl.BlockSpec((t, w), lambda bi, qi: (bi * nq + qi, OFF_NSA_Q // w)),
            pl.BlockSpec((s, NSA_KV_W), lambda bi, qi: (bi, OFF_WIN_K // NSA_KV_W)),
            pl.BlockSpec((s, NSA_KV_W), lambda bi, qi: (bi, OFF_WIN_V // NSA_KV_W)),
            pl.BlockSpec((t, LANES), lambda bi, qi: (bi * nq + qi, 0)),
            pl.BlockSpec((t, w), lambda bi, qi: (bi * nq + qi, OFF_NSA_G // w)),
            pl.BlockSpec((NSA_HEADS, WINDOW + t, t), lambda bi, qi: (0, 0, 0), pipeline_mode=pl.Buffered(1)),
        ],
        out_specs=pl.BlockSpec((t, w), lambda bi, qi: (bi * nq + qi, 0)),
        out_shape=jax.ShapeDtypeStruct((b * s, w), BF16),
        scratch_shapes=[
            pltpu.VMEM((NSA_HEADS, t, LANES), BF16),
            pltpu.VMEM((NSA_KV_HEADS, s, LANES), BF16),
            pltpu.VMEM((2 * NSA_KV_HEADS, LANES, s), BF16),
            pltpu.VMEM((NSA_HEADS, 1, t), F32),
            pltpu.VMEM((NSA_HEADS, LANES, t), F32),
            pltpu.VMEM((ahead, tk, t), F32),
        ],
        compiler_params=_params(("parallel", "arbitrary")),
        name="nsa_win_attn",
    )(p, p, p, ps, p, bwin)


def _mem_attn_kernel(q_ref, gate_ref, kv_ref, o_ref):
    scale = MEM_HEAD_DIM ** -0.5
    for h in range(MEM_HEADS):
        ls = slice(h * LANES, (h + 1) * LANES)
        k = kv_ref[:, ls]
        v = kv_ref[:, GROUP_W + h * LANES:GROUP_W + (h + 1) * LANES]
        s = lax.dot_general(q_ref[:, ls], k, (((1,), (1,)), ((), ())), preferred_element_type=F32) * scale
        m = jnp.max(s, axis=1, keepdims=True)
        e = jnp.exp(s - m)
        l = jnp.sum(e, axis=1, keepdims=True)
        o = jnp.dot(e.astype(BF16), v, preferred_element_type=F32) / l
        o_ref[:, ls] = (o * _silu(gate_ref[:, ls].astype(F32))).astype(o_ref.dtype)


def _mem_attn(p, mem_kv, b, s, t=1024):
    nq = s // t
    w = GROUP_W
    m = mem_kv.shape[0] // b
    return pl.pallas_call(
        _mem_attn_kernel,
        grid=(b, nq),
        in_specs=[
            pl.BlockSpec((t, w), lambda bi, qi: (bi * nq + qi, OFF_MEM_Q // w)),
            pl.BlockSpec((t, w), lambda bi, qi: (bi * nq + qi, OFF_MEM_G // w)),
            pl.BlockSpec((m, 2 * w), lambda bi, qi: (bi, 0)),
        ],
        out_specs=pl.BlockSpec((t, w), lambda bi, qi: (bi * nq + qi, 0)),
        out_shape=jax.ShapeDtypeStruct((b * s, w), BF16),
        compiler_params=_params(("parallel", "arbitrary")),
        name="mem_attn",
    )(p, p, mem_kv)


def _out_proj_kernel(x_ref, of_ref, os_ref, oc_ref, osel_ref, ow_ref, om_ref, w_ref, g_ref, o_ref, *, final):
    w = GROUP_W
    nsa = (oc_ref[...].astype(F32) + osel_ref[...].astype(F32) + ow_ref[...].astype(F32)).astype(BF16)
    acc = x_ref[...]
    for i, part in enumerate((of_ref[...], os_ref[...], nsa, om_ref[...])):
        acc = acc + jnp.dot(part, w_ref[i * w:(i + 1) * w, :], preferred_element_type=F32)
    if final:
        ms = jnp.mean(acc * acc, axis=-1, keepdims=True)
        acc = (acc * lax.rsqrt(ms + EPS)) * g_ref[...]
    o_ref[...] = acc


def _out_proj(x2d, parts, w_out, g, final, tm=1024):
    n, d = x2d.shape
    w = GROUP_W
    part_spec = pl.BlockSpec((tm, w), lambda i: (i, 0))
    return pl.pallas_call(
        functools.partial(_out_proj_kernel, final=final),
        grid=(n // tm,),
        in_specs=[pl.BlockSpec((tm, d), lambda i: (i, 0))] + [part_spec] * 6 + [
            pl.BlockSpec((4 * w, d), lambda i: (0, 0)),
            pl.BlockSpec((1, d), lambda i: (0, 0)),
        ],
        out_specs=pl.BlockSpec((tm, d), lambda i: (i, 0)),
        out_shape=jax.ShapeDtypeStruct((n, d), F32),
        compiler_params=_params(("parallel",)),
        name="out_proj",
    )(x2d, *parts, w_out, g.reshape(1, d))


def _pack_in_proj(w_in_l):
    fox, ssm, nsa, mem = 0, FOX_COLS, FOX_COLS + SSM_COLS, FOX_COLS + SSM_COLS + NSA_COLS
    w = GROUP_W
    q_scale = HEAD_DIM ** -0.5 * LOG2E
    cols = lambda a, n: w_in_l[:, a:a + n]
    kv = lambda slot: nsa + w + slot * NSA_KV_W
    main = ([cols(fox, w) * q_scale, cols(fox + w, 3 * w)]
            + [cols(ssm + w, SSM_CONV_DIM), cols(ssm, w)]
            + [cols(nsa, w) * q_scale, cols(nsa + w + 6 * NSA_KV_W + 3 * NSA_HEADS, w)]
            + [cols(mem, 2 * w)]
            + [cols(kv(2), 4 * NSA_KV_W)]
            + [cols(kv(0), 2 * NSA_KV_W)])
    w_main = jnp.concatenate(main, axis=1).astype(BF16)
    assert w_main.shape[1] == P_COLS
    small = ([cols(fox + 4 * w, FOX_HEADS)] * N_SPLIT
             + [cols(ssm + w + SSM_CONV_DIM, SSM_HEADS)]
             + [cols(nsa + w + 6 * NSA_KV_W, 3 * NSA_HEADS)])
    assert N_SPLIT * FOX_HEADS == PS_DT_LANE and PS_DT_LANE + SSM_HEADS == PS_GATE_LANE
    used = PS_GATE_LANE + 3 * NSA_HEADS
    small.append(jnp.zeros((w_in_l.shape[0], PS_COLS - used), w_in_l.dtype))
    return w_main, jnp.concatenate(small, axis=1).astype(BF16)


def _pad_lanes(v, first=0):
    return jnp.pad(v.astype(F32), (first, LANES - first - v.shape[0])).reshape(1, LANES)


def _trunk(x, mem, norm_g, w_in, fox_f_bias, ssm_conv_w, ssm_conv_b, ssm_dt_bias, ssm_a_log, ssm_d,
           ssm_norm_g, nsa_cmp_pe, nsa_cmp_w1, nsa_cmp_w2, rel_bias_table, mem_norm_g, w_mem_kv, w_out,
           final_norm_g):
    b, s, d = x.shape
    depth = w_in.shape[0]
    n = b * s
    m_tok = mem.shape[1]
    n_cmp = (s - CMP_BLOCK) // CMP_STRIDE + 1
    n_rows = s // CMP_STRIDE
    assert s % 2048 == 0 and s // SEL_BLOCK <= HEAD_DIM and n_rows <= N_CMP_PAD and d == D_MODEL

    tab_flat = rel_bias_table.astype(F32).reshape(-1)
    unbounded = 1 << 30
    bwin = _t5_table(tab_flat, WINDOW + WIN_T, WIN_T, 1, WINDOW, WINDOW)
    bsel = _t5_table(tab_flat, (SEL_T // SEL_TK + 1) * SEL_TK, SEL_T, 1, SEL_TK, unbounded)
    fcmp = _t5_table(tab_flat, 2 * N_CMP_PAD, T_ATT, CMP_STRIDE, CMP_STRIDE * N_CMP_PAD - (CMP_BLOCK - 1), unbounded)

    cs = np.arange(N_CMP_PAD)[None, :] * CMP_STRIDE
    js = np.arange(SEL_BLOCK)[:, None] * SEL_BLOCK
    overlap_t = ((cs < js + SEL_BLOCK) & (cs + CMP_BLOCK > js) & (np.arange(N_CMP_PAD)[None, :] < n_cmp)
                 & (np.arange(SEL_BLOCK)[:, None] < s // SEL_BLOCK)).astype(np.float32)
    overlap_t = jnp.asarray(overlap_t, BF16)

    x2d = x.reshape(n, d)
    mem2d = mem.reshape(b * m_tok, d)
    for l in range(depth):
        w_main, w_small = _pack_in_proj(w_in[l])
        p, ps = _norm_proj(x2d, norm_g[l], (w_main, w_small), (BF16, F32))

        qaug, kaug = _fox_cumsum(ps, _pad_lanes(jnp.tile(fox_f_bias[l], N_SPLIT)), b, s)
        o_fox = _fox_attn(p, qaug, kaug, b, s)

        o_ssd = _ssd(p, ps, ssm_conv_w[l].astype(F32), ssm_conv_b[l].reshape(1, -1).astype(F32),
                     _pad_lanes(ssm_dt_bias[l], PS_DT_LANE), _pad_lanes(ssm_a_log[l], PS_DT_LANE),
                     jnp.repeat(ssm_d[l].astype(F32), HEAD_DIM).reshape(1, GROUP_W),
                     ssm_norm_g[l].reshape(1, GROUP_W).astype(F32), b, s)

        kv_cmp, kv_cmp_t = _compress(p, nsa_cmp_w1[l], nsa_cmp_w2[l], nsa_cmp_pe[l], b, s)
        o_cmp, notsel = _cmp_select(p, ps, kv_cmp, kv_cmp_t, fcmp, overlap_t, b, s)
        o_sel = _sel_attn(tab_flat, p, ps, notsel, bsel, b, s)
        o_win = _win_attn(p, ps, bwin, b, s)

        w_kv = w_mem_kv[l].astype(BF16)
        (mem_kv,) = _norm_proj(mem2d, mem_norm_g[l], (w_kv,), (BF16,), tm=min(512, b * m_tok))
        o_mem = _mem_attn(p, mem_kv, b, s)

        x2d = _out_proj(x2d, (o_fox, o_ssd, o_cmp, o_sel, o_win, o_mem), w_out[l].astype(BF16),
                        final_norm_g, final=(l == depth - 1))
    return x2d.reshape(b, s, d)


def kernel(x, mem, norm_g, w_in, fox_f_bias, ssm_conv_w, ssm_conv_b, ssm_dt_bias, ssm_a_log, ssm_d, ssm_norm_g,
           nsa_cmp_pe, nsa_cmp_w1, nsa_cmp_w2, rel_bias_table, mem_norm_g, w_mem_kv, w_out, final_norm_g):
    return _trunk(x, mem, norm_g, w_in, fox_f_bias, ssm_conv_w, ssm_conv_b, ssm_dt_bias, ssm_a_log, ssm_d,
                  ssm_norm_g, nsa_cmp_pe, nsa_cmp_w1, nsa_cmp_w2, rel_bias_table, mem_norm_g, w_mem_kv, w_out,
                  final_norm_g)
```
